```python
import jax
import jax.numpy as jnp
from jax import lax
import numpy as np

D_MODEL = 1024
BATCH = 8
SEQ = 8192
DEPTH = 2

GRID_W = 64
CTX_LEN = 256
EPS = 1e-6
N_MOD = 6
HEAD_DIM = 64
N_Q_HEADS = D_MODEL // 128
N_KV_HEADS = N_Q_HEADS // 4
Q_GROUP = N_Q_HEADS // N_KV_HEADS
Q_W = N_Q_HEADS * HEAD_DIM
KV_W = N_KV_HEADS * HEAD_DIM
Q_BLOCK = 128
ROPE_THETA = 10000.0
ROPE_AXIS_DIM = HEAD_DIM // 2
ROPE_FREQS = ROPE_AXIS_DIM // 2
GMLP_CHUNK = 128
GMLP_GROUPS = 4
GMLP_WIDTH = D_MODEL // 2
GMLP_GROUP_W = GMLP_WIDTH // GMLP_GROUPS
GLA_HEADS = 4
GLA_QK_W = D_MODEL // 4
GLA_V_W = D_MODEL // 2
GLA_DK = GLA_QK_W // GLA_HEADS
GLA_DV = GLA_V_W // GLA_HEADS
GLA_RANK = 16
GLA_TAU = 16.0
GLA_CHUNK = 64
FFN_HIDDEN = 128 * ((8 * D_MODEL // 3 + 127) // 128)
CONV_WIDTH = 3
IN_SPLITS = (GMLP_WIDTH, GMLP_WIDTH, Q_W, KV_W, KV_W, GLA_QK_W, GLA_QK_W, GLA_V_W, GLA_RANK, GLA_RANK, GLA_V_W, D_MODEL, D_MODEL, D_MODEL)
IN_WIDTH = sum(IN_SPLITS)

kernel_name = "hybrid_gated_branch_diffusion_block"


def rms_norm(x, g):
    xf = x.astype(jnp.float32)
    y = xf * lax.rsqrt(jnp.mean(xf * xf, axis=-1, keepdims=True) + EPS)
    return (y * g.astype(jnp.float32)).astype(x.dtype)


def adaln(cond, w, b):
    m = jax.nn.silu(cond) @ w + b
    return m.reshape(cond.shape[0], 1, N_MOD, D_MODEL)


def modulate(h, shift, scale):
    return h * (1 + scale) + shift


def split_in(p):
    out, start = [], 0
    for w in IN_SPLITS:
        out.append(p[..., start:start + w])
        start += w
    return out


def to_heads(p, n_heads, dim):
    return p.reshape(p.shape[0], p.shape[1], n_heads, dim)


def axial_rope_tables(n_tokens, dtype):
    rows = n_tokens // GRID_W
    t_row = jnp.repeat(jnp.arange(rows, dtype=jnp.int32), GRID_W)
    t_col = jnp.tile(jnp.arange(GRID_W, dtype=jnp.int32), rows)
    inv_freq = ROPE_THETA ** (-jnp.arange(ROPE_FREQS, dtype=jnp.float32) / ROPE_FREQS)
    ang_r = t_row.astype(jnp.float32)[:, None] * inv_freq
    ang_c = t_col.astype(jnp.float32)[:, None] * inv_freq
    return (jnp.cos(ang_r).astype(dtype)[:, None, :], jnp.sin(ang_r).astype(dtype)[:, None, :],
            jnp.cos(ang_c).astype(dtype)[:, None, :], jnp.sin(ang_c).astype(dtype)[:, None, :])


def rope_half(x, cos, sin):
    x1, x2 = x[..., :ROPE_FREQS], x[..., ROPE_FREQS:]
    return jnp.concatenate([x1 * cos - x2 * sin, x1 * sin + x2 * cos], axis=-1)


def apply_axial_rope(x, tables):
    cr, sr, cc, sc = tables
    return jnp.concatenate([rope_half(x[..., :ROPE_AXIS_DIM], cr, sr),
                            rope_half(x[..., ROPE_AXIS_DIM:], cc, sc)], axis=-1)


def block_attention(q, k, v):
    b, t = q.shape[0], q.shape[1]
    nb = t // Q_BLOCK
    qb = q.reshape(b, nb, Q_BLOCK, N_KV_HEADS, Q_GROUP, HEAD_DIM).transpose(1, 0, 2, 3, 4, 5)
    scale = HEAD_DIM ** -0.5

    def one_block(qi):
        s = jnp.einsum('bqhgd,bkhd->bhgqk', qi, k).astype(jnp.float32) * scale
        p = jax.nn.softmax(s, axis=-1).astype(v.dtype)
        return jnp.einsum('bhgqk,bkhd->bqhgd', p, v)

    o = lax.map(one_block, qb)
    return o.transpose(1, 0, 2, 3, 4, 5).reshape(b, t, Q_W)


def chunk_gmlp(u, v, norm_g, w_s, b_s):
    b, t, _ = u.shape
    n = t // GMLP_CHUNK
    u = jax.nn.gelu(u)
    v = rms_norm(jax.nn.gelu(v).reshape(b, n, GMLP_CHUNK, GMLP_GROUPS, GMLP_GROUP_W),
                 norm_g.reshape(GMLP_GROUPS, GMLP_GROUP_W))
    f = jnp.einsum('gij,bnjgc->bnigc', w_s, v) + b_s.T[:, :, None]
    return u * f.reshape(b, t, GMLP_WIDTH)


def gla_inputs(p_q, p_k, p_v, p_af, p_ab, w_a2, b_a):
    b, t, _ = p_q.shape
    q = to_heads(p_q, GLA_HEADS, GLA_DK) * (GLA_DK ** -0.5)
    k = to_heads(p_k, GLA_HEADS, GLA_DK)
    v = to_heads(p_v, GLA_HEADS, GLA_DV)

    def log_decay(a1, w2, b2):
        z = (a1 @ w2 + b2).astype(jnp.float32)
        return (jax.nn.log_sigmoid(z) / GLA_TAU).reshape(b, t, GLA_HEADS, GLA_DK)

    return q, k, v, log_decay(p_af, w_a2[0], b_a[0]), log_decay(p_ab, w_a2[1], b_a[1])


def gla_chunked(q, k, v, log_a, s0):
    b, t, h, dk = q.shape
    dv = v.shape[-1]
    n = t // GLA_CHUNK
    q = q.astype(jnp.float32).reshape(b, n, GLA_CHUNK, h, dk)
    k = k.astype(jnp.float32).reshape(b, n, GLA_CHUNK, h, dk)
    v = v.astype(jnp.float32).reshape(b, n, GLA_CHUNK, h, dv)
    cum = jnp.cumsum(log_a.astype(jnp.float32).reshape(b, n, GLA_CHUNK, h, dk), axis=2)
    cum_last = cum[:, :, -1:]
    q_in = q * jnp.exp(cum)
    k_in = k * jnp.exp(-cum)
    k_st = k * jnp.exp(cum_last - cum)
    mask = jnp.tril(jnp.ones((GLA_CHUNK, GLA_CHUNK), dtype=bool))
    att = jnp.where(mask, jnp.einsum('bnihd,bnjhd->bnhij', q_in, k_in), 0.0)
    o = jnp.einsum('bnhij,bnjhv->bnihv', att, v)
    u = jnp.einsum('bnjhd,bnjhv->nbhdv', k_st, v)
    decay = jnp.exp(cum_last[:, :, 0]).transpose(1, 0, 2, 3)

    def step(s, inp):
        d, u_n = inp
        return d[..., None] * s + u_n, s

    s_final, s_in = lax.scan(step, s0.astype(jnp.float32), (decay, u))
    o = o + jnp.einsum('bnihd,nbhdv->bnihv', q_in, s_in)
    return o.reshape(b, t, h, dv), s_final


def gla_chunked_reverse(q, k, v, log_a, s0):
    o, s = gla_chunked(q[:, ::-1], k[:, ::-1], v[:, ::-1], log_a[:, ::-1], s0)
    return o[:, ::-1], s


def gla_output(o, r, g):
    b, t = o.shape[0], o.shape[1]
    o = rms_norm(o, g.reshape(GLA_HEADS, GLA_DV)).reshape(b, t, GLA_V_W)
    return (o * jax.nn.silu(r.astype(jnp.float32))).astype(r.dtype)


def merge_branches(gate_logits, y_a, y_b, y_c, w_a, w_b, w_c, w_o):
    g_a, g_b, g_c = gate_logits
    merged = (jax.nn.sigmoid(g_a) * (y_a @ w_a) + jax.nn.sigmoid(g_b) * (y_b @ w_b)
              + jax.nn.sigmoid(g_c) * (y_c @ w_c))
    return merged @ w_o


def conv_ffn(h, w_up, cw, cb, w_down):
    t = h.shape[1]
    half = CONV_WIDTH // 2
    a = h @ w_up
    ap = jnp.pad(a, ((0, 0), (half, half), (0, 0)))
    a = cb + sum(ap[:, j:j + t] * cw[j] for j in range(CONV_WIDTH))
    g, val = jnp.split(a, 2, axis=-1)
    return (jax.nn.silu(g) * val) @ w_down


def _fwd_setup_inputs(seed: int = 0) -> dict:
    key = jax.random.key(seed)
    ks = jax.random.split(key, 26)
    f32 = jnp.float32
    nrm = lambda k, shape, scale: jax.random.normal(k, shape, f32) * scale
    gain = lambda k, shape: 1.0 + 0.02 * jax.random.normal(k, shape, f32)
    F2 = 2 * FFN_HIDDEN
    return {
        'x': nrm(ks[0], (BATCH, SEQ, D_MODEL), 1.0),
        'c': nrm(ks[1], (BATCH, D_MODEL), 1.0),
        'ctx': nrm(ks[2], (BATCH, CTX_LEN, D_MODEL), 1.0),
        'c_ctx': nrm(ks[3], (D_MODEL,), 1.0),
        'w_ada': nrm(ks[4], (DEPTH, D_MODEL, N_MOD * D_MODEL), 0.5 * D_MODEL ** -0.5),
        'b_ada': nrm(ks[5], (DEPTH, N_MOD * D_MODEL), 0.02),
        'norm1_g': gain(ks[6], (DEPTH, D_MODEL)),
        'norm2_g': gain(ks[7], (DEPTH, D_MODEL)),
        'w_in': nrm(ks[8], (DEPTH, D_MODEL, IN_WIDTH), D_MODEL ** -0.5),
        'q_norm_g': gain(ks[9], (DEPTH, HEAD_DIM)),
        'k_norm_g': gain(ks[10], (DEPTH, HEAD_DIM)),
        'gmlp_norm_g': gain(ks[11], (DEPTH, GMLP_WIDTH)),
        'w_spatial': nrm(ks[12], (DEPTH, GMLP_GROUPS, GMLP_CHUNK, GMLP_CHUNK), 0.5 * GMLP_CHUNK ** -0.5),
        'b_spatial': gain(ks[13], (DEPTH, GMLP_GROUPS, GMLP_CHUNK)),
        'w_alpha2': nrm(ks[14], (DEPTH, 2, GLA_RANK, GLA_QK_W), GLA_RANK ** -0.5),
        'b_alpha': nrm(ks[15], (DEPTH, 2, GLA_QK_W), 0.02),
        'gla_norm_g': gain(ks[16], (DEPTH, GLA_V_W)),
        'w_br_a': nrm(ks[17], (DEPTH, GMLP_WIDTH, D_MODEL), GMLP_WIDTH ** -0.5),
        'w_br_b': nrm(ks[18], (DEPTH, Q_W, D_MODEL), Q_W ** -0.5),
        'w_br_c': nrm(ks[19], (DEPTH, GLA_V_W, D_MODEL), GLA_V_W ** -0.5),
        'w_out': nrm(ks[20], (DEPTH, D_MODEL, D_MODEL), D_MODEL ** -0.5),
        'w_ffn_up': nrm(ks[21], (DEPTH, D_MODEL, F2), D_MODEL ** -0.5),
        'conv_w': nrm(ks[22], (DEPTH, CONV_WIDTH, F2), CONV_WIDTH ** -0.5),
        'conv_b': nrm(ks[23], (DEPTH, F2), 0.02),
        'w_ffn_down': nrm(ks[24], (DEPTH, FFN_HIDDEN, D_MODEL), FFN_HIDDEN ** -0.5),
        'final_norm_g': gain(ks[25], (D_MODEL,)),
    }


def _fwd_reference(x, c, ctx, c_ctx, w_ada, b_ada, norm1_g, norm2_g, w_in, q_norm_g, k_norm_g,
              gmlp_norm_g, w_spatial, b_spatial, w_alpha2, b_alpha, gla_norm_g, w_br_a, w_br_b,
              w_br_c, w_out, w_ffn_up, conv_w, conv_b, w_ffn_down, final_norm_g):
    rope_tab = axial_rope_tables(x.shape[1], x.dtype)
    xc = ctx
    for l in range(DEPTH):
        last = l == DEPTH - 1
        mod_x = adaln(c, w_ada[l], b_ada[l])
        mod_c = adaln(c_ctx[None], w_ada[l], b_ada[l])
        hx = modulate(rms_norm(x, norm1_g[l]), mod_x[:, :, 0], mod_x[:, :, 1])
        hc = modulate(rms_norm(xc, norm1_g[l]), mod_c[:, :, 0], mod_c[:, :, 1])
        px = split_in(hx @ w_in[l])
        pc = split_in(hc @ w_in[l])

        kc = rms_norm(to_heads(pc[3], N_KV_HEADS, HEAD_DIM), k_norm_g[l])
        vc = to_heads(pc[4], N_KV_HEADS, HEAD_DIM)
        qx = apply_axial_rope(rms_norm(to_heads(px[2], N_Q_HEADS, HEAD_DIM), q_norm_g[l]), rope_tab)
        kx = apply_axial_rope(rms_norm(to_heads(px[3], N_KV_HEADS, HEAD_DIM), k_norm_g[l]), rope_tab)
        vx = to_heads(px[4], N_KV_HEADS, HEAD_DIM)
        att_x = block_attention(qx, jnp.concatenate([kc, kx], axis=1), jnp.concatenate([vc, vx], axis=1))

        qgc, kgc, vgc, lfc, lbc = gla_inputs(*pc[5:10], w_alpha2[l], b_alpha[l])
        s0 = jnp.zeros((xc.shape[0], GLA_HEADS, GLA_DK, GLA_DV), jnp.float32)
        oc_f, sc_f = gla_chunked(qgc, kgc, vgc, lfc, s0)
        oc_b, sc_b = gla_chunked_reverse(qgc, kgc, vgc, lbc, s0)
        qgx, kgx, vgx, lfx, lbx = gla_inputs(*px[5:10], w_alpha2[l], b_alpha[l])
        ox_f, _ = gla_chunked(qgx, kgx, vgx, lfx, sc_f)
        ox_b, _ = gla_chunked_reverse(qgx, kgx, vgx, lbx, sc_b)
        gla_x = gla_output(ox_f + ox_b, px[10], gla_norm_g[l])

        gm_x = chunk_gmlp(px[0], px[1], gmlp_norm_g[l], w_spatial[l], b_spatial[l])

        mix_x = merge_branches(px[11:14], gm_x, att_x, gla_x, w_br_a[l], w_br_b[l], w_br_c[l], w_out[l])
        x_mid = x + mod_x[:, :, 2] * mix_x
        hx2 = modulate(rms_norm(x_mid, norm2_g[l]), mod_x[:, :, 3], mod_x[:, :, 4])
        x = x_mid + mod_x[:, :, 5] * conv_ffn(hx2, w_ffn_up[l], conv_w[l], conv_b[l], w_ffn_down[l])

        if not last:
            qc = rms_norm(to_heads(pc[2], N_Q_HEADS, HEAD_DIM), q_norm_g[l])
            att_c = block_attention(qc, kc, vc)
            gla_c = gla_output(oc_f + oc_b, pc[10], gla_norm_g[l])
            gm_c = chunk_gmlp(pc[0], pc[1], gmlp_norm_g[l], w_spatial[l], b_spatial[l])
            mix_c = merge_branches(pc[11:14], gm_c, att_c, gla_c, w_br_a[l], w_br_b[l], w_br_c[l], w_out[l])
            xc_mid = xc + mod_c[:, :, 2] * mix_c
            hc2 = modulate(rms_norm(xc_mid, norm2_g[l]), mod_c[:, :, 3], mod_c[:, :, 4])
            xc = xc_mid + mod_c[:, :, 5] * conv_ffn(hc2, w_ffn_up[l], conv_w[l], conv_b[l], w_ffn_down[l])
    return rms_norm(x, final_norm_g)


import jax as _jax
import jax.numpy as _jnp

TWIN_FORMAT = 'train_step'
FWD_PARAMS = ['x', 'c', 'ctx', 'c_ctx', 'w_ada', 'b_ada', 'norm1_g', 'norm2_g', 'w_in', 'q_norm_g', 'k_norm_g', 'gmlp_norm_g', 'w_spatial', 'b_spatial', 'w_alpha2', 'b_alpha', 'gla_norm_g', 'w_br_a', 'w_br_b', 'w_br_c', 'w_out', 'w_ffn_up', 'conv_w', 'conv_b', 'w_ffn_down', 'final_norm_g']
TWIN_WEIGHTS = ['c_ctx', 'w_ada', 'b_ada', 'norm1_g', 'norm2_g', 'w_in', 'q_norm_g', 'k_norm_g', 'gmlp_norm_g', 'w_spatial', 'b_spatial', 'w_alpha2', 'b_alpha', 'gla_norm_g', 'w_br_a', 'w_br_b', 'w_br_c', 'w_out', 'w_ffn_up', 'conv_w', 'conv_b', 'w_ffn_down', 'final_norm_g']
TWIN_DIFF_INPUT = 'x'
TWIN_INPUTS = ['x', 'c', 'ctx', 'c_ctx', 'w_ada', 'b_ada', 'norm1_g', 'norm2_g', 'w_in', 'q_norm_g', 'k_norm_g', 'gmlp_norm_g', 'w_spatial', 'b_spatial', 'w_alpha2', 'b_alpha', 'gla_norm_g', 'w_br_a', 'w_br_b', 'w_br_c', 'w_out', 'w_ffn_up', 'conv_w', 'conv_b', 'w_ffn_down', 'final_norm_g', 'loss_target', 'm_c_ctx', 'm_w_ada', 'm_b_ada', 'm_norm1_g', 'm_norm2_g', 'm_w_in', 'm_q_norm_g', 'm_k_norm_g', 'm_gmlp_norm_g', 'm_w_spatial', 'm_b_spatial', 'm_w_alpha2', 'm_b_alpha', 'm_gla_norm_g', 'm_w_br_a', 'm_w_br_b', 'm_w_br_c', 'm_w_out', 'm_w_ffn_up', 'm_conv_w', 'm_conv_b', 'm_w_ffn_down', 'm_final_norm_g', 'v_c_ctx', 'v_w_ada', 'v_b_ada', 'v_norm1_g', 'v_norm2_g', 'v_w_in', 'v_q_norm_g', 'v_k_norm_g', 'v_gmlp_norm_g', 'v_w_spatial', 'v_b_spatial', 'v_w_alpha2', 'v_b_alpha', 'v_gla_norm_g', 'v_w_br_a', 'v_w_br_b', 'v_w_br_c', 'v_w_out', 'v_w_ffn_up', 'v_conv_w', 'v_conv_b', 'v_w_ffn_down', 'v_final_norm_g']
TWIN_OUTPUTS = ['loss', 'grad_x', 'grad_c_ctx', 'grad_w_ada', 'grad_b_ada', 'grad_norm1_g', 'grad_norm2_g', 'grad_w_in', 'grad_q_norm_g', 'grad_k_norm_g', 'grad_gmlp_norm_g', 'grad_w_spatial', 'grad_b_spatial', 'grad_w_alpha2', 'grad_b_alpha', 'grad_gla_norm_g', 'grad_w_br_a', 'grad_w_br_b', 'grad_w_br_c', 'grad_w_out', 'grad_w_ffn_up', 'grad_conv_w', 'grad_conv_b', 'grad_w_ffn_down', 'grad_final_norm_g', 'delta_c_ctx', 'delta_w_ada', 'delta_b_ada', 'delta_norm1_g', 'delta_norm2_g', 'delta_w_in', 'delta_q_norm_g', 'delta_k_norm_g', 'delta_gmlp_norm_g', 'delta_w_spatial', 'delta_b_spatial', 'delta_w_alpha2', 'delta_b_alpha', 'delta_gla_norm_g', 'delta_w_br_a', 'delta_w_br_b', 'delta_w_br_c', 'delta_w_out', 'delta_w_ffn_up', 'delta_conv_w', 'delta_conv_b', 'delta_w_ffn_down', 'delta_final_norm_g', 'new_m_c_ctx', 'new_m_w_ada', 'new_m_b_ada', 'new_m_norm1_g', 'new_m_norm2_g', 'new_m_w_in', 'new_m_q_norm_g', 'new_m_k_norm_g', 'new_m_gmlp_norm_g', 'new_m_w_spatial', 'new_m_b_spatial', 'new_m_w_alpha2', 'new_m_b_alpha', 'new_m_gla_norm_g', 'new_m_w_br_a', 'new_m_w_br_b', 'new_m_w_br_c', 'new_m_w_out', 'new_m_w_ffn_up', 'new_m_conv_w', 'new_m_conv_b', 'new_m_w_ffn_down', 'new_m_final_norm_g', 'new_v_c_ctx', 'new_v_w_ada', 'new_v_b_ada', 'new_v_norm1_g', 'new_v_norm2_g', 'new_v_w_in', 'new_v_q_norm_g', 'new_v_k_norm_g', 'new_v_gmlp_norm_g', 'new_v_w_spatial', 'new_v_b_spatial', 'new_v_w_alpha2', 'new_v_b_alpha', 'new_v_gla_norm_g', 'new_v_w_br_a', 'new_v_w_br_b', 'new_v_w_br_c', 'new_v_w_out', 'new_v_w_ffn_up', 'new_v_conv_w', 'new_v_conv_b', 'new_v_w_ffn_down', 'new_v_final_norm_g']
TWIN_LEAF_KINDS = {'loss': 'loss', 'grad_x': 'grad_x', 'grad_c_ctx': 'grad_w', 'grad_w_ada': 'grad_w', 'grad_b_ada': 'grad_w', 'grad_norm1_g': 'grad_w', 'grad_norm2_g': 'grad_w', 'grad_w_in': 'grad_w', 'grad_q_norm_g': 'grad_w', 'grad_k_norm_g': 'grad_w', 'grad_gmlp_norm_g': 'grad_w', 'grad_w_spatial': 'grad_w', 'grad_b_spatial': 'grad_w', 'grad_w_alpha2': 'grad_w', 'grad_b_alpha': 'grad_w', 'grad_gla_norm_g': 'grad_w', 'grad_w_br_a': 'grad_w', 'grad_w_br_b': 'grad_w', 'grad_w_br_c': 'grad_w', 'grad_w_out': 'grad_w', 'grad_w_ffn_up': 'grad_w', 'grad_conv_w': 'grad_w', 'grad_conv_b': 'grad_w', 'grad_w_ffn_down': 'grad_w', 'grad_final_norm_g': 'grad_w', 'delta_c_ctx': 'delta_w', 'delta_w_ada': 'delta_w', 'delta_b_ada': 'delta_w', 'delta_norm1_g': 'delta_w', 'delta_norm2_g': 'delta_w', 'delta_w_in': 'delta_w', 'delta_q_norm_g': 'delta_w', 'delta_k_norm_g': 'delta_w', 'delta_gmlp_norm_g': 'delta_w', 'delta_w_spatial': 'delta_w', 'delta_b_spatial': 'delta_w', 'delta_w_alpha2': 'delta_w', 'delta_b_alpha': 'delta_w', 'delta_gla_norm_g': 'delta_w', 'delta_w_br_a': 'delta_w', 'delta_w_br_b': 'delta_w', 'delta_w_br_c': 'delta_w', 'delta_w_out': 'delta_w', 'delta_w_ffn_up': 'delta_w', 'delta_conv_w': 'delta_w', 'delta_conv_b': 'delta_w', 'delta_w_ffn_down': 'delta_w', 'delta_final_norm_g': 'delta_w', 'new_m_c_ctx': 'new_m', 'new_m_w_ada': 'new_m', 'new_m_b_ada': 'new_m', 'new_m_norm1_g': 'new_m', 'new_m_norm2_g': 'new_m', 'new_m_w_in': 'new_m', 'new_m_q_norm_g': 'new_m', 'new_m_k_norm_g': 'new_m', 'new_m_gmlp_norm_g': 'new_m', 'new_m_w_spatial': 'new_m', 'new_m_b_spatial': 'new_m', 'new_m_w_alpha2': 'new_m', 'new_m_b_alpha': 'new_m', 'new_m_gla_norm_g': 'new_m', 'new_m_w_br_a': 'new_m', 'new_m_w_br_b': 'new_m', 'new_m_w_br_c': 'new_m', 'new_m_w_out': 'new_m', 'new_m_w_ffn_up': 'new_m', 'new_m_conv_w': 'new_m', 'new_m_conv_b': 'new_m', 'new_m_w_ffn_down': 'new_m', 'new_m_final_norm_g': 'new_m', 'new_v_c_ctx': 'new_v', 'new_v_w_ada': 'new_v', 'new_v_b_ada': 'new_v', 'new_v_norm1_g': 'new_v', 'new_v_norm2_g': 'new_v', 'new_v_w_in': 'new_v', 'new_v_q_norm_g': 'new_v', 'new_v_k_norm_g': 'new_v', 'new_v_gmlp_norm_g': 'new_v', 'new_v_w_spatial': 'new_v', 'new_v_b_spatial': 'new_v', 'new_v_w_alpha2': 'new_v', 'new_v_b_alpha': 'new_v', 'new_v_gla_norm_g': 'new_v', 'new_v_w_br_a': 'new_v', 'new_v_w_br_b': 'new_v', 'new_v_w_br_c': 'new_v', 'new_v_w_out': 'new_v', 'new_v_w_ffn_up': 'new_v', 'new_v_conv_w': 'new_v', 'new_v_conv_b': 'new_v', 'new_v_w_ffn_down': 'new_v', 'new_v_final_norm_g': 'new_v'}


def _forward(args):
    return _fwd_reference(*[args[k] for k in FWD_PARAMS])


def _output_shape():
    def fwd():
        inp = _fwd_setup_inputs(0)
        return _fwd_reference(*[inp[k] for k in FWD_PARAMS])
    out = _jax.eval_shape(fwd)
    return out.shape, out.dtype

N_MICROBATCH = 1
ADAM_LR = 0.001
ADAM_B1 = 0.9
ADAM_B2 = 0.999
ADAM_EPS = 1e-08
ADAM_WD = 0.01
ADAM_STEP = 10
PER_EXAMPLE_BATCH_AXIS = {'x': 0, 'c': 0, 'ctx': 0, 'loss_target': 0}
SHARED_INPUTS = []
_WEIGHT_DTYPES = {'c_ctx': _jnp.float32, 'w_ada': _jnp.float32, 'b_ada': _jnp.float32, 'norm1_g': _jnp.float32, 'norm2_g': _jnp.float32, 'w_in': _jnp.float32, 'q_norm_g': _jnp.float32, 'k_norm_g': _jnp.float32, 'gmlp_norm_g': _jnp.float32, 'w_spatial': _jnp.float32, 'b_spatial': _jnp.float32, 'w_alpha2': _jnp.float32, 'b_alpha': _jnp.float32, 'gla_norm_g': _jnp.float32, 'w_br_a': _jnp.float32, 'w_br_b': _jnp.float32, 'w_br_c': _jnp.float32, 'w_out': _jnp.float32, 'w_ffn_up': _jnp.float32, 'conv_w': _jnp.float32, 'conv_b': _jnp.float32, 'w_ffn_down': _jnp.float32, 'final_norm_g': _jnp.float32}
MOMENT_SCALE = {'c_ctx': 1.149396e-02, 'w_ada': 7.220353e-02, 'b_ada': 1.284280e-01, 'norm1_g': 6.702746e-02, 'norm2_g': 7.611927e-02, 'w_in': 2.903589e-02, 'q_norm_g': 1.277726e-02, 'k_norm_g': 1.401854e-02, 'gmlp_norm_g': 2.046540e-02, 'w_spatial': 4.348661e-02, 'b_spatial': 4.414807e-02, 'w_alpha2': 6.903229e-03, 'b_alpha': 1.786619e-02, 'gla_norm_g': 4.228877e-02, 'w_br_a': 3.332059e-02, 'w_br_b': 1.338190e-02, 'w_br_c': 2.714462e-02, 'w_out': 4.534660e-02, 'w_ffn_up': 3.329504e-02, 'conv_w': 3.315420e-02, 'conv_b': 2.961919e-02, 'w_ffn_down': 5.416019e-02, 'final_norm_g': 6.404847e+01}


def _to_microbatches(a, axis):
    t = _jnp.moveaxis(a, axis, 0)
    t = t.reshape((N_MICROBATCH, t.shape[0] // N_MICROBATCH) + t.shape[1:])
    return _jnp.moveaxis(t, 1, axis + 1)


def setup_inputs(seed: int = 0) -> dict:
    inp = _fwd_setup_inputs(seed)
    key = _jax.random.fold_in(_jax.random.key(seed), 7919)
    shape, _ = _output_shape()
    out = dict(inp)
    out["loss_target"] = _jax.random.normal(_jax.random.fold_in(key, 0), shape, _jnp.float32)
    for i, name in enumerate(TWIN_WEIGHTS):
        w = inp[name].astype(_jnp.float32)
        if MOMENT_SCALE is None:
            s = _jnp.sqrt(_jnp.mean(_jnp.square(w)) + 1e-30)
        else:
            s = MOMENT_SCALE[name]
        km, kv = _jax.random.split(_jax.random.fold_in(key, i + 1))
        out[name] = w
        out["m_" + name] = s * _jax.random.normal(km, w.shape, _jnp.float32)
        out["v_" + name] = (s * s) * _jax.random.uniform(kv, w.shape, _jnp.float32, 0.5, 1.5)
    if N_MICROBATCH > 1:
        for name, axis in PER_EXAMPLE_BATCH_AXIS.items():
            out[name] = _to_microbatches(out[name], axis)
    return {'x': out['x'], 'c': out['c'], 'ctx': out['ctx'], 'c_ctx': out['c_ctx'], 'w_ada': out['w_ada'], 'b_ada': out['b_ada'], 'norm1_g': out['norm1_g'], 'norm2_g': out['norm2_g'], 'w_in': out['w_in'], 'q_norm_g': out['q_norm_g'], 'k_norm_g': out['k_norm_g'], 'gmlp_norm_g': out['gmlp_norm_g'], 'w_spatial': out['w_spatial'], 'b_spatial': out['b_spatial'], 'w_alpha2': out['w_alpha2'], 'b_alpha': out['b_alpha'], 'gla_norm_g': out['gla_norm_g'], 'w_br_a': out['w_br_a'], 'w_br_b': out['w_br_b'], 'w_br_c': out['w_br_c'], 'w_out': out['w_out'], 'w_ffn_up': out['w_ffn_up'], 'conv_w': out['conv_w'], 'conv_b': out['conv_b'], 'w_ffn_down': out['w_ffn_down'], 'final_norm_g': out['final_norm_g'], 'loss_target': out['loss_target'], 'm_c_ctx': out['m_c_ctx'], 'm_w_ada': out['m_w_ada'], 'm_b_ada': out['m_b_ada'], 'm_norm1_g': out['m_norm1_g'], 'm_norm2_g': out['m_norm2_g'], 'm_w_in': out['m_w_in'], 'm_q_norm_g': out['m_q_norm_g'], 'm_k_norm_g': out['m_k_norm_g'], 'm_gmlp_norm_g': out['m_gmlp_norm_g'], 'm_w_spatial': out['m_w_spatial'], 'm_b_spatial': out['m_b_spatial'], 'm_w_alpha2': out['m_w_alpha2'], 'm_b_alpha': out['m_b_alpha'], 'm_gla_norm_g': out['m_gla_norm_g'], 'm_w_br_a': out['m_w_br_a'], 'm_w_br_b': out['m_w_br_b'], 'm_w_br_c': out['m_w_br_c'], 'm_w_out': out['m_w_out'], 'm_w_ffn_up': out['m_w_ffn_up'], 'm_conv_w': out['m_conv_w'], 'm_conv_b': out['m_conv_b'], 'm_w_ffn_down': out['m_w_ffn_down'], 'm_final_norm_g': out['m_final_norm_g'], 'v_c_ctx': out['v_c_ctx'], 'v_w_ada': out['v_w_ada'], 'v_b_ada': out['v_b_ada'], 'v_norm1_g': out['v_norm1_g'], 'v_norm2_g': out['v_norm2_g'], 'v_w_in': out['v_w_in'], 'v_q_norm_g': out['v_q_norm_g'], 'v_k_norm_g': out['v_k_norm_g'], 'v_gmlp_norm_g': out['v_gmlp_norm_g'], 'v_w_spatial': out['v_w_spatial'], 'v_b_spatial': out['v_b_spatial'], 'v_w_alpha2': out['v_w_alpha2'], 'v_b_alpha': out['v_b_alpha'], 'v_gla_norm_g': out['v_gla_norm_g'], 'v_w_br_a': out['v_w_br_a'], 'v_w_br_b': out['v_w_br_b'], 'v_w_br_c': out['v_w_br_c'], 'v_w_out': out['v_w_out'], 'v_w_ffn_up': out['v_w_ffn_up'], 'v_conv_w': out['v_conv_w'], 'v_conv_b': out['v_conv_b'], 'v_w_ffn_down': out['v_w_ffn_down'], 'v_final_norm_g': out['v_final_norm_g']}


def _loss(weights, diff, rest, loss_target):
    with _jax.named_scope("forward"):
        args = {**rest, TWIN_DIFF_INPUT: diff, **{k: w.astype(_WEIGHT_DTYPES[k]) for k, w in weights.items()}}
        y = _forward(args)
    with _jax.named_scope("loss_head"):
        err = _jnp.square(y.astype(_jnp.float32) - loss_target)
        return 0.5 * _jnp.sum(_jnp.mean(err, axis=-1)) if err.ndim else 0.5 * err


def _adamw(w, g, m, v):
    m = ADAM_B1 * m + (1.0 - ADAM_B1) * g
    v = ADAM_B2 * v + (1.0 - ADAM_B2) * _jnp.square(g)
    m_hat = m / (1.0 - ADAM_B1 ** ADAM_STEP)
    v_hat = v / (1.0 - ADAM_B2 ** ADAM_STEP)
    delta = -ADAM_LR * (m_hat / (_jnp.sqrt(v_hat) + ADAM_EPS) + ADAM_WD * w)
    return delta, m, v


def reference(x, c, ctx, c_ctx, w_ada, b_ada, norm1_g, norm2_g, w_in, q_norm_g, k_norm_g, gmlp_norm_g, w_spatial, b_spatial, w_alpha2, b_alpha, gla_norm_g, w_br_a, w_br_b, w_br_c, w_out, w_ffn_up, conv_w, conv_b, w_ffn_down, final_norm_g, loss_target, m_c_ctx, m_w_ada, m_b_ada, m_norm1_g, m_norm2_g, m_w_in, m_q_norm_g, m_k_norm_g, m_gmlp_norm_g, m_w_spatial, m_b_spatial, m_w_alpha2, m_b_alpha, m_gla_norm_g, m_w_br_a, m_w_br_b, m_w_br_c, m_w_out, m_w_ffn_up, m_conv_w, m_conv_b, m_w_ffn_down, m_final_norm_g, v_c_ctx, v_w_ada, v_b_ada, v_norm1_g, v_norm2_g, v_w_in, v_q_norm_g, v_k_norm_g, v_gmlp_norm_g, v_w_spatial, v_b_spatial, v_w_alpha2, v_b_alpha, v_gla_norm_g, v_w_br_a, v_w_br_b, v_w_br_c, v_w_out, v_w_ffn_up, v_conv_w, v_conv_b, v_w_ffn_down, v_final_norm_g):
    given = dict(x=x, c=c, ctx=ctx, c_ctx=c_ctx, w_ada=w_ada, b_ada=b_ada, norm1_g=norm1_g, norm2_g=norm2_g, w_in=w_in, q_norm_g=q_norm_g, k_norm_g=k_norm_g, gmlp_norm_g=gmlp_norm_g, w_spatial=w_spatial, b_spatial=b_spatial, w_alpha2=w_alpha2, b_alpha=b_alpha, gla_norm_g=gla_norm_g, w_br_a=w_br_a, w_br_b=w_br_b, w_br_c=w_br_c, w_out=w_out, w_ffn_up=w_ffn_up, conv_w=conv_w, conv_b=conv_b, w_ffn_down=w_ffn_down, final_norm_g=final_norm_g, loss_target=loss_target, m_c_ctx=m_c_ctx, m_w_ada=m_w_ada, m_b_ada=m_b_ada, m_norm1_g=m_norm1_g, m_norm2_g=m_norm2_g, m_w_in=m_w_in, m_q_norm_g=m_q_norm_g, m_k_norm_g=m_k_norm_g, m_gmlp_norm_g=m_gmlp_norm_g, m_w_spatial=m_w_spatial, m_b_spatial=m_b_spatial, m_w_alpha2=m_w_alpha2, m_b_alpha=m_b_alpha, m_gla_norm_g=m_gla_norm_g, m_w_br_a=m_w_br_a, m_w_br_b=m_w_br_b, m_w_br_c=m_w_br_c, m_w_out=m_w_out, m_w_ffn_up=m_w_ffn_up, m_conv_w=m_conv_w, m_conv_b=m_conv_b, m_w_ffn_down=m_w_ffn_down, m_final_norm_g=m_final_norm_g, v_c_ctx=v_c_ctx, v_w_ada=v_w_ada, v_b_ada=v_b_ada, v_norm1_g=v_norm1_g, v_norm2_g=v_norm2_g, v_w_in=v_w_in, v_q_norm_g=v_q_norm_g, v_k_norm_g=v_k_norm_g, v_gmlp_norm_g=v_gmlp_norm_g, v_w_spatial=v_w_spatial, v_b_spatial=v_b_spatial, v_w_alpha2=v_w_alpha2, v_b_alpha=v_b_alpha, v_gla_norm_g=v_gla_norm_g, v_w_br_a=v_w_br_a, v_w_br_b=v_w_br_b, v_w_br_c=v_w_br_c, v_w_out=v_w_out, v_w_ffn_up=v_w_ffn_up, v_conv_w=v_conv_w, v_conv_b=v_conv_b, v_w_ffn_down=v_w_ffn_down, v_final_norm_g=v_final_norm_g)
    weights = {n: given[n] for n in TWIN_WEIGHTS}
    shared = {n: given[n] for n in SHARED_INPUTS}
    per_example = {n: given[n] for n in ['x', 'c', 'ctx']}
    grad_fn = _jax.value_and_grad(_loss, argnums=(0, 1))

    def one_microbatch(ex, loss_target):
        ex = dict(ex)
        diff = ex.pop(TWIN_DIFF_INPUT)
        return grad_fn(weights, diff, {**shared, **ex}, loss_target)

    if N_MICROBATCH == 1:
        loss, (grad_w, grad_x) = one_microbatch(per_example, given["loss_target"])
    else:
        def body(carry, xs):
            loss_sum, grad_sum = carry
            l_k, (gw_k, gx_k) = one_microbatch(xs[0], xs[1])
            with _jax.named_scope("update"):
                return (loss_sum + l_k, _jax.tree.map(_jnp.add, grad_sum, gw_k)), gx_k

        init = (_jnp.zeros((), _jnp.float32), _jax.tree.map(_jnp.zeros_like, weights))
        (loss, grad_w), grad_x = _jax.lax.scan(body, init, (per_example, given["loss_target"]))
    with _jax.named_scope("update"):
        delta_w, new_m, new_v = {}, {}, {}
        for n in TWIN_WEIGHTS:
            delta_w[n], new_m[n], new_v[n] = _adamw(weights[n], grad_w[n], given["m_" + n], given["v_" + n])
    return (loss, grad_x, *[grad_w[n] for n in TWIN_WEIGHTS], *[delta_w[n] for n in TWIN_WEIGHTS],
            *[new_m[n] for n in TWIN_WEIGHTS], *[new_v[n] for n in TWIN_WEIGHTS])
```

```python
import functools

import jax
import jax.numpy as jnp
from jax import lax
from jax.experimental import pallas as pl
from jax.experimental.pallas import tpu as pltpu

F32 = jnp.float32
BF16 = jnp.bfloat16

D = 1024
TC = 256
GRID_W = 64
EPS = 1e-6
HD = 64
NQ = 8
NKV = 2
QG = NQ // NKV
GLA_H = 4
GLA_DK = 64
GLA_DV = 128
GLA_QK = 256
GLA_V = 512
GLA_CHUNK = 64
GLA_TAU = 16.0
GW = 512
FFN = 2816
IN_W = 6432
PW = 6528
ADA_W = 6 * D
N_CHIP = 4
N_DEV = 8
ADA_LOC = ADA_W // N_CHIP

ADAM_LR = 0.001
ADAM_B1 = 0.9
ADAM_B2 = 0.999
ADAM_EPS = 1e-08
ADAM_WD = 0.01
ADAM_STEP = 10

TM = 256
NCB = TC // TM
LANE = 128
VMEM_LIMIT = 48 * 1024 * 1024
MESH = pl.DeviceIdType.MESH

_COLS = (("gA", 3360, 1024), ("gB", 4384, 1024), ("gC", 5408, 1024), ("gu", 0, 512), ("gv", 512, 512),
         ("q", 1024, 512), ("glv", 2304, 512), ("gr", 2848, 512), ("glq", 1792, 256), ("glk", 2048, 256),
         ("k", 1536, 128), ("v", 1664, 128), ("ab", 2816, 32))
OFF = {}
_o = 0
for _n, _s, _w in _COLS:
    OFF[_n] = _o
    _o += max(_w, LANE)
assert _o == PW


def _to_new_cols(w):
    parts = [w[..., s:s + n] for _, s, n in _COLS]
    pad = jnp.zeros(w.shape[:-1] + (PW - IN_W,), w.dtype)
    return jnp.concatenate(parts + [pad], axis=-1)


def _to_ref_cols(w):
    by_start = sorted(_COLS, key=lambda t: t[1])
    return jnp.concatenate([w[..., OFF[n]:OFF[n] + wd] for n, _, wd in by_start], axis=-1)


def _tile(n, target, align=LANE):
    best = None
    t = align
    while t <= min(n, target):
        if n % t == 0:
            best = t
        t += align
    assert best is not None, (n, target, align)
    return best


def _cp(sem=None):
    return pltpu.CompilerParams(dimension_semantics=sem, vmem_limit_bytes=VMEM_LIMIT)


def _bdot_impl(a, b, ca, cb):
    return lax.dot_general(a.astype(BF16), b.astype(BF16), (((ca,), (cb,)), ((), ())),
                           preferred_element_type=F32)


@functools.partial(jax.custom_vjp, nondiff_argnums=(2, 3))
def bdot(a, b, ca, cb):
    return _bdot_impl(a, b, ca, cb)


def _bdot_fwd(a, b, ca, cb):
    return _bdot_impl(a, b, ca, cb), (a, b)


def _bdot_bwd(ca, cb, res, g):
    a, b = res
    da = bdot(g, b, 1, 1 - cb) if ca == 1 else bdot(b, g, 1 - cb, 1)
    db = bdot(a, g, 1 - ca, 0) if cb == 0 else bdot(g, a, 0, 1 - ca)
    return da.astype(a.dtype), db.astype(b.dtype)


bdot.defvjp(_bdot_fwd, _bdot_bwd)


def hdot(a, b, ca=1, cb=0):
    return lax.dot_general(a, b, (((ca,), (cb,)), ((), ())), precision=lax.Precision.HIGHEST,
                           preferred_element_type=F32)


def _rms(x, g):
    return x * lax.rsqrt(jnp.mean(x * x, axis=-1, keepdims=True) + EPS) * g


def _gelu(x):
    return 0.5 * x * (1.0 + jnp.tanh(0.7978845608028654 * (x + 0.044715 * (x * x * x))))


def _log_sigmoid(z):
    return jnp.minimum(z, 0.0) - jnp.log(1.0 + jnp.exp(-jnp.abs(z)))


def _sel(mod, is_lat, idx):
    return jnp.where(is_lat, mod[1, idx:idx + 1, :], mod[0, idx:idx + 1, :])


def _rows_call(name, fn, grid, ins, outs, acc_axes=None, sem=None):
    n_in = len(ins)
    flags = [o[2] for o in outs]
    if acc_axes is None:
        acc_axes = (len(grid) - 1,)

    def body(*refs):
        ids = tuple(pl.program_id(a) for a in range(len(grid)))
        res = fn(ids, *[r[...] for r in refs[:n_in]])
        for r, v, acc in zip(refs[n_in:], res, flags):
            if acc:
                first = functools.reduce(jnp.logical_and, [ids[a] == 0 for a in acc_axes])

                @pl.when(first)
                def _():
                    r[...] = jnp.zeros_like(r)
                r[...] += v.astype(r.dtype)
            else:
                r[...] = v.astype(r.dtype)

    return pl.pallas_call(
        body, name=name, grid=grid, in_specs=[s for _, s in ins], out_specs=[o[1] for o in outs],
        out_shape=[o[0] for o in outs],
        compiler_params=_cp(sem if sem is not None else ("arbitrary",) * len(grid)),
    )(*[a for a, _ in ins])


def _sds(shape, dtype):
    return jax.ShapeDtypeStruct(shape, dtype)


def _rowspec(width, off=0, tm=TM):
    assert off % width == 0
    return pl.BlockSpec((tm, width), lambda i, o=off // width: (i, o))


def _full(shape):
    nd = len(shape)
    return pl.BlockSpec(shape, lambda *a: (0,) * nd)


def _mm(name, a, b, mode, out_dtype, tm_t=1056, tn_t=1408, tk_t=1408):
    if mode == "nn":
        (m, k), (_, n) = a.shape, b.shape
    elif mode == "nt":
        (m, k), (n, _) = a.shape, b.shape
    else:
        (k, m), (_, n) = a.shape, b.shape
    tm = _tile(m, tm_t, 8 if m % LANE else LANE)
    tn = _tile(n, tn_t)
    tk = _tile(k, tk_t)
    nk = k // tk
    if mode == "nn":
        dims, a_spec, b_spec = ((1,), (0,)), pl.BlockSpec((tm, tk), lambda i, j, l: (i, l)), pl.BlockSpec((tk, tn), lambda i, j, l: (l, j))
    elif mode == "nt":
        dims, a_spec, b_spec = ((1,), (1,)), pl.BlockSpec((tm, tk), lambda i, j, l: (i, l)), pl.BlockSpec((tn, tk), lambda i, j, l: (j, l))
    else:
        dims, a_spec, b_spec = ((0,), (0,)), pl.BlockSpec((tk, tm), lambda i, j, l: (l, i)), pl.BlockSpec((tk, tn), lambda i, j, l: (l, j))

    def body(a_ref, b_ref, o_ref, *scratch):
        l = pl.program_id(2)
        part = lax.dot_general(a_ref[...].astype(BF16), b_ref[...].astype(BF16), (dims, ((), ())),
                               preferred_element_type=F32)
        if nk == 1:
            o_ref[...] = part.astype(o_ref.dtype)
            return
        acc_ref = scratch[0]

        @pl.when(l == 0)
        def _():
            acc_ref[...] = part

        @pl.when(l > 0)
        def _():
            acc_ref[...] += part

        @pl.when(l == nk - 1)
        def _():
            o_ref[...] = acc_ref[...].astype(o_ref.dtype)

    return pl.pallas_call(
        body, name=name, grid=(m // tm, n // tn, nk), in_specs=[a_spec, b_spec],
        out_specs=pl.BlockSpec((tm, tn), lambda i, j, l: (i, j)), out_shape=_sds((m, n), out_dtype),
        scratch_shapes=[pltpu.VMEM((tm, tn), F32)] if nk > 1 else [],
        compiler_params=_cp(("parallel", "parallel", "arbitrary")),
    )(a, b)


def _nm_fn(is_lat, x, mod, g, shift, scale):
    return _rms(x, g) * (1.0 + _sel(mod, is_lat, scale)) + _sel(mod, is_lat, shift)


def _res_nm_fn(is_lat, x, br, modg, gate, mods, g, shift, scale):
    xn = x + _sel(modg, is_lat, gate) * br
    return xn, _nm_fn(is_lat, xn, mods, g, shift, scale)


def _nm_fwd(name, x, mod, g, shift, scale):
    t = x.shape[0]
    fn = lambda ids, xv, mv, gv: (_nm_fn(ids[0] >= NCB, xv, mv, gv, shift, scale),)
    return _rows_call(name, fn, (t // TM,), [(x, _rowspec(D)), (mod, _full((2, 6, D))), (g, _full((1, D)))],
                      [(_sds((t, D), BF16), _rowspec(D), False)])[0]


def _nm_bwd(name, x, mod, g, shift, scale, dx_res, dh):
    t = x.shape[0]

    def fn(ids, xv, mv, gv, dxr, dhv):
        _, vjp = jax.vjp(lambda a, b, c: _nm_fn(ids[0] >= NCB, a, b, c, shift, scale), xv, mv, gv)
        dx, dm, dg = vjp(dhv)
        return dx + dxr, dm, dg

    return _rows_call(name, fn, (t // TM,),
                      [(x, _rowspec(D)), (mod, _full((2, 6, D))), (g, _full((1, D))), (dx_res, _rowspec(D)), (dh, _rowspec(D))],
                      [(_sds((t, D), F32), _rowspec(D), False), (_sds((2, 6, D), F32), _full((2, 6, D)), True),
                       (_sds((1, D), F32), _full((1, D)), True)])


def _res_nm_fwd(name, x, br, modg, gate, mods, g, shift, scale):
    t = x.shape[0]
    fn = lambda ids, xv, bv, mg, ms, gv: _res_nm_fn(ids[0] >= NCB, xv, bv, mg, gate, ms, gv, shift, scale)
    return _rows_call(name, fn, (t // TM,),
                      [(x, _rowspec(D)), (br, _rowspec(D)), (modg, _full((2, 6, D))), (mods, _full((2, 6, D))), (g, _full((1, D)))],
                      [(_sds((t, D), F32), _rowspec(D), False), (_sds((t, D), BF16), _rowspec(D), False)])


def _res_nm_bwd(name, x, br, modg, gate, mods, g, shift, scale, dx_res, dh):
    t = x.shape[0]

    def fn(ids, xv, bv, mg, ms, gv, dxr, dhv):
        f = lambda a, b, c, d, e: _res_nm_fn(ids[0] >= NCB, a, b, c, gate, d, e, shift, scale)
        _, vjp = jax.vjp(f, xv, bv, mg, ms, gv)
        return vjp((dxr, dhv))

    m26 = (_sds((2, 6, D), F32), _full((2, 6, D)), True)
    return _rows_call(name, fn, (t // TM,),
                      [(x, _rowspec(D)), (br, _rowspec(D)), (modg, _full((2, 6, D))), (mods, _full((2, 6, D))), (g, _full((1, D))),
                       (dx_res, _rowspec(D)), (dh, _rowspec(D))],
                      [(_sds((t, D), F32), _rowspec(D), False), (_sds((t, D), BF16), _rowspec(D), False), m26, m26,
                       (_sds((1, D), F32), _full((1, D)), True)])


def _head(name, x_mid, f, mod, gf, tgt):
    tx = tgt.shape[0]

    def fn(ids, xv, fv, mv, gv, tv):
        def loss_fn(a, b, c, d):
            y = _rms(a + c[1, 5:6, :] * b, d)
            e = y - tv
            return 0.5 * jnp.sum(jnp.mean(e * e, axis=-1))
        loss, grads = jax.value_and_grad(loss_fn, argnums=(0, 1, 2, 3))(xv, fv, mv, gv)
        return (jnp.reshape(loss, (1, 1)),) + grads

    lat = pl.BlockSpec((TM, D), lambda i: (i + NCB, 0))
    return _rows_call(name, fn, (tx // TM,),
                      [(x_mid, lat), (f, lat), (mod, _full((2, 6, D))), (gf, _full((1, D))), (tgt, _rowspec(D))],
                      [(_sds((1, 1), F32), _full((1, 1)), True), (_sds((tx, D), F32), _rowspec(D), False),
                       (_sds((tx, D), BF16), _rowspec(D), False), (_sds((2, 6, D), F32), _full((2, 6, D)), True),
                       (_sds((1, D), F32), _full((1, D)), True)])


def _gmlp_fn(u, v, g, ws, bst):
    rows = []
    for r in range(u.shape[0] // 128):
        uu, vv = _gelu(u[128 * r:128 * r + 128]), _gelu(v[128 * r:128 * r + 128])
        cols = []
        for gi in range(4):
            sl = slice(128 * gi, 128 * gi + 128)
            f = bdot(ws[gi], _rms(vv[:, sl], g[:, sl]), 1, 0) + bst[:, gi:gi + 1]
            cols.append(uu[:, sl] * f)
        rows.append(jnp.concatenate(cols, axis=-1))
    return jnp.concatenate(rows, axis=0)


def _gmlp_ins(p, g, ws, bst):
    return [(p, _rowspec(GW, OFF["gu"])), (p, _rowspec(GW, OFF["gv"])), (g, _full((1, GW))),
            (ws, _full((4, 128, 128))), (bst, _full((128, 4)))]


def _gmlp_fwd(name, p, g, ws, bst):
    t = p.shape[0]
    return _rows_call(name, lambda ids, *a: (_gmlp_fn(*a),), (t // TM,), _gmlp_ins(p, g, ws, bst),
                      [(_sds((t, GW), BF16), _rowspec(GW), False)])[0]


def _gmlp_bwd(name, p, g, ws, bst, dgm):
    t = p.shape[0]

    def fn(ids, u, v, gv, wv, bv, dv):
        _, vjp = jax.vjp(_gmlp_fn, u, v, gv, wv, bv)
        return vjp(dv)

    return _rows_call(name, fn, (t // TM,), _gmlp_ins(p, g, ws, bst) + [(dgm, _rowspec(GW))],
                      [(_sds((t, GW), BF16), _rowspec(GW), False), (_sds((t, GW), BF16), _rowspec(GW), False),
                       (_sds((1, GW), F32), _full((1, GW)), True), (_sds((4, 128, 128), F32), _full((4, 128, 128)), True),
                       (_sds((128, 4), F32), _full((128, 4)), True)])


def _qk_fn(q, k, gq, gk, cos, sin, seg, perm):
    cq, sq = jnp.concatenate([cos] * 4, axis=-1), jnp.concatenate([sin] * 4, axis=-1)
    qn = q * lax.rsqrt(hdot(q * q, seg) + EPS) * gq
    kn = k * lax.rsqrt(hdot(k * k, seg[:128, :128]) + EPS) * gk
    qr = qn * cq + hdot(qn, perm) * sq
    kr = kn * cos + hdot(kn, perm[:128, :128]) * sin
    return qr * (HD ** -0.5), kr


def _qk_ins(p, gq, gk, cos, sin, seg, perm):
    return [(p, _rowspec(512, OFF["q"])), (p, _rowspec(128, OFF["k"])), (gq, _full((1, 512))), (gk, _full((1, 128))),
            (cos, _rowspec(128)), (sin, _rowspec(128)), (seg, _full((512, 512))), (perm, _full((512, 512)))]


def _qk_fwd(name, p, gq, gk, cos, sin, seg, perm):
    t = p.shape[0]
    fn = lambda ids, q, k, a, b, c, s, sg, pm, v: _qk_fn(q, k, a, b, c, s, sg, pm) + (v,)
    return _rows_call(name, fn, (t // TM,), _qk_ins(p, gq, gk, cos, sin, seg, perm) + [(p, _rowspec(128, OFF["v"]))],
                      [(_sds((t, 512), BF16), _rowspec(512), False), (_sds((t, 128), BF16), _rowspec(128), False),
                       (_sds((t, 128), BF16), _rowspec(128), False)])


def _qk_bwd(name, p, gq, gk, cos, sin, seg, perm, dqr, dkr):
    t = p.shape[0]

    def fn(ids, q, k, a, b, c, s, sg, pm, dq, dk):
        _, vjp = jax.vjp(lambda q_, k_, a_, b_: _qk_fn(q_, k_, a_, b_, c, s, sg, pm), q, k, a, b)
        return vjp((dq, dk))

    return _rows_call(name, fn, (t // TM,),
                      _qk_ins(p, gq, gk, cos, sin, seg, perm) + [(dqr, _rowspec(512)), (dkr, _rowspec(128))],
                      [(_sds((t, 512), BF16), _rowspec(512), False), (_sds((t, 128), BF16), _rowspec(128), False),
                       (_sds((1, 512), F32), _full((1, 512)), True), (_sds((1, 128), F32), _full((1, 128)), True)])


def _attn_fwd(name, q, k, v):
    h, tq_all, _ = q.shape
    tk_all = k.shape[1]
    tq, tk = _tile(tq_all, 512), _tile(tk_all, 1056)
    nkc = tk_all // tk

    def body(q_ref, k_ref, v_ref, o_ref, lse_ref):
        qv = q_ref[0]

        def step(j, carry):
            m, l, acc = carry
            off = pl.multiple_of(j * tk, tk)
            kk, vv = k_ref[0, pl.ds(off, tk), :], v_ref[0, pl.ds(off, tk), :]
            s = lax.dot_general(qv, kk, (((1,), (1,)), ((), ())), preferred_element_type=F32)
            m_new = jnp.maximum(m, jnp.max(s, axis=-1, keepdims=True))
            alpha = jnp.exp(m - m_new)
            pr = jnp.exp(s - m_new)
            l = alpha * l + jnp.sum(pr, axis=-1, keepdims=True)
            acc = alpha * acc + jnp.dot(pr.astype(BF16), vv, preferred_element_type=F32)
            return m_new, l, acc

        init = (jnp.full((tq, 1), -jnp.inf, F32), jnp.zeros((tq, 1), F32), jnp.zeros((tq, HD), F32))
        m, l, acc = lax.fori_loop(0, nkc, step, init)
        o_ref[0] = acc / l
        lse_ref[0] = m + jnp.log(l)

    kv_spec = pl.BlockSpec((1, tk_all, HD), lambda hh, i: (hh // QG, 0, 0))
    return pl.pallas_call(
        body, name=name, grid=(h, tq_all // tq),
        in_specs=[pl.BlockSpec((1, tq, HD), lambda hh, i: (hh, i, 0)), kv_spec, kv_spec],
        out_specs=[pl.BlockSpec((1, tq, HD), lambda hh, i: (hh, i, 0)), pl.BlockSpec((1, tq, 1), lambda hh, i: (hh, i, 0))],
        out_shape=[_sds((h, tq_all, HD), F32), _sds((h, tq_all, 1), F32)],
        compiler_params=_cp(("parallel", "parallel")),
    )(q, k, v)


def _attn_bwd_q(name, q, k, v, o, do, lse):
    h, tq_all, _ = q.shape
    tk_all = k.shape[1]
    tq, tk = _tile(tq_all, 512), _tile(tk_all, 1056)
    nkc = tk_all // tk

    def body(q_ref, k_ref, v_ref, o_ref, do_ref, lse_ref, dq_ref, dl_ref):
        qv, dov, lsev = q_ref[0], do_ref[0], lse_ref[0]
        delta = jnp.sum(dov * o_ref[0], axis=-1, keepdims=True)
        dob = dov.astype(BF16)

        def step(j, dq):
            off = pl.multiple_of(j * tk, tk)
            kk, vv = k_ref[0, pl.ds(off, tk), :], v_ref[0, pl.ds(off, tk), :]
            s = lax.dot_general(qv, kk, (((1,), (1,)), ((), ())), preferred_element_type=F32)
            pr = jnp.exp(s - lsev)
            dp = lax.dot_general(dob, vv, (((1,), (1,)), ((), ())), preferred_element_type=F32)
            ds = pr * (dp - delta)
            return dq + jnp.dot(ds.astype(BF16), kk, preferred_element_type=F32)

        dq_ref[0] = lax.fori_loop(0, nkc, step, jnp.zeros((tq, HD), F32))
        dl_ref[0] = delta

    kv_spec = pl.BlockSpec((1, tk_all, HD), lambda hh, i: (hh // QG, 0, 0))
    qs = pl.BlockSpec((1, tq, HD), lambda hh, i: (hh, i, 0))
    cs = pl.BlockSpec((1, tq, 1), lambda hh, i: (hh, i, 0))
    return pl.pallas_call(
        body, name=name, grid=(h, tq_all // tq), in_specs=[qs, kv_spec, kv_spec, qs, qs, cs], out_specs=[qs, cs],
        out_shape=[_sds((h, tq_all, HD), F32), _sds((h, tq_all, 1), F32)],
        compiler_params=_cp(("parallel", "parallel")),
    )(q, k, v, o, do, lse)


def _attn_bwd_kv(name, q, k, v, do, lse_row, delta_row):
    h, tq_all, _ = q.shape
    hkv, tk_all, _ = k.shape
    tq, tk = _tile(tq_all, 1024), _tile(tk_all, 1056)
    nq = tq_all // tq

    def body(q_ref, k_ref, v_ref, do_ref, lse_ref, dl_ref, dk_ref, dv_ref, dk_acc, dv_acc):
        i = pl.program_id(2)

        @pl.when(i == 0)
        def _():
            dk_acc[...] = jnp.zeros_like(dk_acc)
            dv_acc[...] = jnp.zeros_like(dv_acc)

        kk, vv = k_ref[0], v_ref[0]
        for g in range(QG):
            qv, dob = q_ref[g], do_ref[g].astype(BF16)
            st = lax.dot_general(kk, qv, (((1,), (1,)), ((), ())), preferred_element_type=F32)
            pt = jnp.exp(st - lse_ref[g])
            dv_acc[...] += jnp.dot(pt.astype(BF16), dob, preferred_element_type=F32)
            dpt = lax.dot_general(vv, dob, (((1,), (1,)), ((), ())), preferred_element_type=F32)
            dst = pt * (dpt - dl_ref[g])
            dk_acc[...] += jnp.dot(dst.astype(BF16), qv, preferred_element_type=F32)

        @pl.when(i == nq - 1)
        def _():
            dk_ref[0] = dk_acc[...]
            dv_ref[0] = dv_acc[...]

    ks = pl.BlockSpec((1, tk, HD), lambda g, j, i: (g, j, 0))
    qs = pl.BlockSpec((QG, tq, HD), lambda g, j, i: (g, i, 0))
    rs = pl.BlockSpec((QG, 1, tq), lambda g, j, i: (g, 0, i))
    return pl.pallas_call(
        body, name=name, grid=(hkv, tk_all // tk, nq), in_specs=[qs, ks, ks, qs, rs, rs], out_specs=[ks, ks],
        out_shape=[_sds((hkv, tk_all, HD), F32), _sds((hkv, tk_all, HD), F32)],
        scratch_shapes=[pltpu.VMEM((tk, HD), F32), pltpu.VMEM((tk, HD), F32)],
        compiler_params=_cp(("parallel", "parallel", "arbitrary")),
    )(q, k, v, do, lse_row, delta_row)


def _decay_fn(a, w2, b2):
    return _log_sigmoid(bdot(a, w2, 1, 0) + b2) / GLA_TAU


def _decay_fwd(name, p, w2, b2):
    t = p.shape[0]
    return _rows_call(name, lambda ids, a, w, b: (_decay_fn(a, w, b),), (t // TM,),
                      [(p, _rowspec(128, OFF["ab"])), (w2, _full((128, 512))), (b2, _full((1, 512)))],
                      [(_sds((t, 512), F32), _rowspec(512), False)])[0]


def _decay_bwd(name, p, w2, b2, dla_f, dla_b):
    t = p.shape[0]

    def fn(ids, a, w, b, df, db):
        _, vjp = jax.vjp(_decay_fn, a, w, b)
        return vjp(jnp.concatenate([df, db], axis=-1))

    return _rows_call(name, fn, (t // TM,),
                      [(p, _rowspec(128, OFF["ab"])), (w2, _full((128, 512))), (b2, _full((1, 512))),
                       (dla_f, _rowspec(256)), (dla_b, _rowspec(256))],
                      [(_sds((t, 128), BF16), _rowspec(128), False), (_sds((128, 512), F32), _full((128, 512)), True),
                       (_sds((1, 512), F32), _full((1, 512)), True)])


def _gla_consts(reverse):
    r = lax.broadcasted_iota(jnp.int32, (GLA_CHUNK, GLA_CHUNK), 0)
    c = lax.broadcasted_iota(jnp.int32, (GLA_CHUNK, GLA_CHUNK), 1)
    trib = (r <= c) if reverse else (r >= c)
    br = lax.broadcasted_iota(jnp.int32, (GLA_QK, GLA_V), 0) // GLA_DK
    bc = lax.broadcasted_iota(jnp.int32, (GLA_QK, GLA_V), 1) // GLA_DV
    lane_head = lax.broadcasted_iota(jnp.int32, (1, GLA_QK), 1) // GLA_DK
    return trib, (br == bc).astype(F32), lane_head


def _gla_chunk(q, k, v, la, s_in, consts):
    trib, bd, lane_head = consts
    cum = hdot(trib.astype(F32), la)
    tot = jnp.sum(la, axis=0, keepdims=True)
    q_in = q * (GLA_DK ** -0.5) * jnp.exp(cum)
    k_in = k * jnp.exp(-cum)
    k_st = k * jnp.exp(tot - cum)
    outs = []
    for h in range(GLA_H):
        att = bdot(jnp.where(lane_head == h, q_in, 0.0), k_in, 1, 1)
        att = jnp.where(trib, att, 0.0)
        outs.append(bdot(att, v[:, GLA_DV * h:GLA_DV * (h + 1)], 1, 0))
    o = jnp.concatenate(outs, axis=-1) + bdot(q_in, s_in, 1, 0)
    decay = jnp.exp(hdot(la, jnp.ones((GLA_CHUNK, GLA_V), F32), 0, 0))
    s_out = decay * s_in + bdot(k_st, v, 0, 0) * bd
    return o, s_out


def _gla_order(nb, reverse, backward):
    if not reverse:
        return (lambda s: nb - 1 - s) if backward else (lambda s: s)
    if backward:
        return lambda s: jnp.where(s == nb - 1, 0, s + 1)
    return lambda s: jnp.where(s == 0, 0, nb - s)


_NCH = TM // GLA_CHUNK


def _gla_fwd(name, p, la, reverse):
    t = p.shape[0]
    nb = t // TM
    order = _gla_order(nb, reverse, False)

    def body(q_ref, k_ref, v_ref, la_ref, o_ref, sv_ref, s_ref):
        @pl.when(pl.program_id(0) == 0)
        def _():
            s_ref[...] = jnp.zeros_like(s_ref)

        consts = _gla_consts(reverse)
        for c in (range(_NCH - 1, -1, -1) if reverse else range(_NCH)):
            rows = slice(GLA_CHUNK * c, GLA_CHUNK * (c + 1))
            s_in = s_ref[...]
            for h in range(GLA_H):
                sv_ref[c, h] = s_in[GLA_DK * h:GLA_DK * (h + 1), GLA_DV * h:GLA_DV * (h + 1)]
            o, s_out = _gla_chunk(q_ref[rows, :], k_ref[rows, :], v_ref[rows, :], la_ref[rows, :], s_in, consts)
            o_ref[rows, :] = o
            s_ref[...] = s_out

    def col(width, off):
        return pl.BlockSpec((TM, width), lambda s, o=off // width: (order(s), o))

    return pl.pallas_call(
        body, name=name, grid=(nb,),
        in_specs=[col(256, OFF["glq"]), col(256, OFF["glk"]), col(512, OFF["glv"]), col(256, 256 * int(reverse))],
        out_specs=[col(512, 0), pl.BlockSpec((_NCH, GLA_H, GLA_DK, GLA_DV), lambda s: (order(s), 0, 0, 0))],
        out_shape=[_sds((t, GLA_V), F32), _sds((t // GLA_CHUNK, GLA_H, GLA_DK, GLA_DV), F32)],
        scratch_shapes=[pltpu.VMEM((GLA_QK, GLA_V), F32)], compiler_params=_cp(("arbitrary",)),
    )(p, p, p, la)


def _gla_bwd(name, p, la, sv, do, reverse, prev=None):
    t = p.shape[0]
    nb = t // TM
    order = _gla_order(nb, reverse, True)
    n_prev = 0 if prev is None else 3

    def body(*refs):
        q_ref, k_ref, v_ref, la_ref, sv_ref, do_ref = refs[:6]
        prev_refs = refs[6:6 + n_prev]
        dq_ref, dk_ref, dv_ref, dla_ref, ds_ref = refs[6 + n_prev:]

        @pl.when(pl.program_id(0) == 0)
        def _():
            ds_ref[...] = jnp.zeros_like(ds_ref)

        consts = _gla_consts(reverse)
        zero = jnp.zeros((GLA_DK, GLA_DV), F32)
        for c in (range(_NCH) if reverse else range(_NCH - 1, -1, -1)):
            rows = slice(GLA_CHUNK * c, GLA_CHUNK * (c + 1))
            s_in = jnp.concatenate(
                [jnp.concatenate([sv_ref[c, h] if hh == h else zero for hh in range(GLA_H)], axis=-1) for h in range(GLA_H)], axis=0)
            _, vjp = jax.vjp(lambda a, b, cc, d, e: _gla_chunk(a, b, cc, d, e, consts),
                             q_ref[rows, :], k_ref[rows, :], v_ref[rows, :], la_ref[rows, :], s_in)
            dq, dk, dv, dla, ds_in = vjp((do_ref[rows, :], ds_ref[...]))
            if n_prev:
                dq, dk, dv = dq + prev_refs[0][rows, :], dk + prev_refs[1][rows, :], dv + prev_refs[2][rows, :]
            dq_ref[rows, :], dk_ref[rows, :], dv_ref[rows, :], dla_ref[rows, :] = dq, dk, dv, dla
            ds_ref[...] = ds_in

    def col(width, off):
        return pl.BlockSpec((TM, width), lambda s, o=off // width: (order(s), o))

    ins = [p, p, p, la, sv, do] + (list(prev) if n_prev else [])
    in_specs = [col(256, OFF["glq"]), col(256, OFF["glk"]), col(512, OFF["glv"]), col(256, 256 * int(reverse)),
                pl.BlockSpec((_NCH, GLA_H, GLA_DK, GLA_DV), lambda s: (order(s), 0, 0, 0)), col(512, 0)]
    in_specs += [col(256, 0), col(256, 0), col(512, 0)][:n_prev]
    return pl.pallas_call(
        body, name=name, grid=(nb,), in_specs=in_specs, out_specs=[col(256, 0), col(256, 0), col(512, 0), col(256, 0)],
        out_shape=[_sds((t, GLA_QK), F32), _sds((t, GLA_QK), F32), _sds((t, GLA_V), F32), _sds((t, GLA_QK), F32)],
        scratch_shapes=[pltpu.VMEM((GLA_QK, GLA_V), F32)], compiler_params=_cp(("arbitrary",)),
    )(*ins)


def _gla_out_fn(of, ob, r, g):
    o = of + ob
    cols = [_rms(o[:, GLA_DV * h:GLA_DV * (h + 1)], g[:, GLA_DV * h:GLA_DV * (h + 1)]) for h in range(GLA_H)]
    return jnp.concatenate(cols, axis=-1) * jax.nn.silu(r)


def _gla_out_fwd(name, of, ob, p, g):
    t = p.shape[0]
    return _rows_call(name, lambda ids, *a: (_gla_out_fn(*a),), (t // TM,),
                      [(of, _rowspec(512)), (ob, _rowspec(512)), (p, _rowspec(512, OFF["gr"])), (g, _full((1, 512)))],
                      [(_sds((t, 512), BF16), _rowspec(512), False)])[0]


def _gla_out_bwd(name, of, ob, p, g, dgla):
    t = p.shape[0]

    def fn(ids, a, b, r, gv, dv):
        _, vjp = jax.vjp(_gla_out_fn, a, b, r, gv)
        do, _, dr, dg = vjp(dv)
        return do, dr, dg

    return _rows_call(name, fn, (t // TM,),
                      [(of, _rowspec(512)), (ob, _rowspec(512)), (p, _rowspec(512, OFF["gr"])), (g, _full((1, 512))),
                       (dgla, _rowspec(512))],
                      [(_sds((t, 512), F32), _rowspec(512), False), (_sds((t, 512), BF16), _rowspec(512), False),
                       (_sds((1, 512), F32), _full((1, 512)), True)])


_TMM = 384


def _merge_fwd(name, gm, att, gla, wa, wb, wc, p):
    t = p.shape[0]
    row = lambda w, off=0: pl.BlockSpec((_TMM, w), lambda i, o=off // w: (i, o))

    def fn(ids, a, b, c, wa_, wb_, wc_, ga, gb, gc):
        return (jax.nn.sigmoid(ga) * bdot(a, wa_, 1, 0) + jax.nn.sigmoid(gb) * bdot(b, wb_, 1, 0)
                + jax.nn.sigmoid(gc) * bdot(c, wc_, 1, 0),)

    return _rows_call(name, fn, (t // _TMM,),
                      [(gm, row(512)), (att, row(512)), (gla, row(512)), (wa, _full((512, D))), (wb, _full((512, D))),
                       (wc, _full((512, D))), (p, row(D, OFF["gA"])), (p, row(D, OFF["gB"])), (p, row(D, OFF["gC"]))],
                      [(_sds((t, D), BF16), row(D), False)])[0]


def _merge_bwd(name, gm, att, gla, wa, wb, wc, p, dmerged):
    t = p.shape[0]
    row = lambda w, off=0: pl.BlockSpec((_TMM, w), lambda i, o=off // w: (i, o))

    def fn(ids, a, b, c, wa_, wb_, wc_, ga, gb, gc, dm):
        outs_y, outs_g = [], []
        for br, w, g in ((a, wa_, ga), (b, wb_, gb), (c, wc_, gc)):
            s = jax.nn.sigmoid(g)
            outs_y.append(dm * s)
            outs_g.append(dm * bdot(br, w, 1, 0) * s * (1.0 - s))
        return tuple(outs_y) + tuple(outs_g)

    o = (_sds((t, D), BF16), row(D), False)
    return _rows_call(name, fn, (t // _TMM,),
                      [(gm, row(512)), (att, row(512)), (gla, row(512)), (wa, _full((512, D))), (wb, _full((512, D))),
                       (wc, _full((512, D))), (p, row(D, OFF["gA"])), (p, row(D, OFF["gB"])), (p, row(D, OFF["gC"])),
                       (dmerged, row(D))], [o] * 6)


_TNC = 1408
_NJ = FFN // _TNC


def _shift_rows(x, prev8, next8, vp, vn):
    n = x.shape[0]
    rid = lax.broadcasted_iota(jnp.int32, x.shape, 0)
    xp = jnp.where(rid == 0, jnp.where(vp, prev8[7:8, :], 0.0), pltpu.roll(x, 1, 0))
    xn = jnp.where(rid == n - 1, jnp.where(vn, next8[0:1, :], 0.0), pltpu.roll(x, n - 1, 0))
    return xp, xn


def _seq_edges(i, t):
    start, end = i * TM, (i + 1) * TM
    return jnp.logical_and(start != 0, start != TC), jnp.logical_and(end != TC, end != t)


def _halo_specs(t, colmap):
    r8 = TM // 8
    main = pl.BlockSpec((TM, _TNC), lambda j, i: (i, colmap(j)))
    prev = pl.BlockSpec((8, _TNC), lambda j, i: (jnp.maximum(i * r8 - 1, 0), colmap(j)))
    nxt = pl.BlockSpec((8, _TNC), lambda j, i: (jnp.minimum((i + 1) * r8, t // 8 - 1), colmap(j)))
    return [main, prev, nxt]


def _conv3(x, xp, xn, w, b=None):
    y = xp * w[0:1, :] + x * w[1:2, :] + xn * w[2:3, :]
    return y if b is None else b + y


def _conv_fwd(name, a, cw, cb):
    t = a.shape[0]

    def fn(ids, ag, agp, agn, av, avp, avn, wg, wv, bg, bv):
        vp, vn = _seq_edges(ids[1], t)
        cg = _conv3(ag, *_shift_rows(ag, agp, agn, vp, vn), wg, bg)
        cv = _conv3(av, *_shift_rows(av, avp, avn, vp, vn), wv, bv)
        return (jax.nn.silu(cg) * cv,)

    gcol, vcol = (lambda j: j), (lambda j: j + _NJ)
    wspec = lambda cm: pl.BlockSpec((3, _TNC), lambda j, i: (0, cm(j)))
    bspec = lambda cm: pl.BlockSpec((1, _TNC), lambda j, i: (0, cm(j)))
    ins = [(a, s) for s in _halo_specs(t, gcol) + _halo_specs(t, vcol)]
    ins += [(cw, wspec(gcol)), (cw, wspec(vcol)), (cb, bspec(gcol)), (cb, bspec(vcol))]
    return _rows_call(name, fn, (_NJ, t // TM), ins,
                      [(_sds((t, FFN), BF16), pl.BlockSpec((TM, _TNC), lambda j, i: (i, j)), False)])[0]


def _conv_bwd_gate(name, a, cw, cb, dact):
    t = a.shape[0]

    def fn(ids, ag, agp, agn, av, avp, avn, wg, wv, bg, bv, dv):
        vp, vn = _seq_edges(ids[1], t)
        cg = _conv3(ag, *_shift_rows(ag, agp, agn, vp, vn), wg, bg)
        cv = _conv3(av, *_shift_rows(av, avp, avn, vp, vn), wv, bv)
        s = jax.nn.sigmoid(cg)
        d_gate = dv * cv * s * (1.0 + cg * (1.0 - s))
        d_val = dv * cg * s
        return (jnp.where(ids[0] >= _NJ, d_val, d_gate),)

    gcol, vcol = (lambda j: j % _NJ), (lambda j: j % _NJ + _NJ)
    wspec = lambda cm: pl.BlockSpec((3, _TNC), lambda j, i: (0, cm(j)))
    bspec = lambda cm: pl.BlockSpec((1, _TNC), lambda j, i: (0, cm(j)))
    ins = [(a, s) for s in _halo_specs(t, gcol) + _halo_specs(t, vcol)]
    ins += [(cw, wspec(gcol)), (cw, wspec(vcol)), (cb, bspec(gcol)), (cb, bspec(vcol)),
            (dact, pl.BlockSpec((TM, _TNC), lambda j, i: (i, j % _NJ)))]
    return _rows_call(name, fn, (2 * _NJ, t // TM), ins,
                      [(_sds((t, 2 * FFN), F32), pl.BlockSpec((TM, _TNC), lambda j, i: (i, j)), False)])[0]


def _conv_bwd_in(name, a, cw, dconv):
    t = a.shape[0]

    def fn(ids, x, xpb, xnb, dc, dcpb, dcnb, w):
        vp, vn = _seq_edges(ids[1], t)
        xp, xn = _shift_rows(x, xpb, xnb, vp, vn)
        dcp, dcn = _shift_rows(dc, dcpb, dcnb, vp, vn)
        da = dcn * w[0:1, :] + dc * w[1:2, :] + dcp * w[2:3, :]
        sums = [jnp.sum(dc * y, axis=0, keepdims=True) for y in (xp, x, xn)]
        rid = lax.broadcasted_iota(jnp.int32, (3, x.shape[1]), 0)
        dw = jnp.where(rid == 0, sums[0], jnp.where(rid == 1, sums[1], sums[2]))
        return da, dw, jnp.sum(dc, axis=0, keepdims=True)

    col = lambda j: j
    ins = [(a, s) for s in _halo_specs(t, col)] + [(dconv, s) for s in _halo_specs(t, col)]
    ins += [(cw, pl.BlockSpec((3, _TNC), lambda j, i: (0, j)))]
    return _rows_call(name, fn, (2 * _NJ, t // TM), ins,
                      [(_sds((t, 2 * FFN), BF16), pl.BlockSpec((TM, _TNC), lambda j, i: (i, j)), False),
                       (_sds((3, 2 * FFN), F32), pl.BlockSpec((3, _TNC), lambda j, i: (0, j)), True),
                       (_sds((1, 2 * FFN), F32), pl.BlockSpec((1, _TNC), lambda j, i: (0, j)), True)])


_TNA = 512


def _adaln_fwd(name, cond, w, b):
    fn = lambda ids, cv, wv, bv: ((bdot(jax.nn.silu(cv), wv[0], 1, 0) + bv[0])[None],)
    return _rows_call(name, fn, (2, ADA_LOC // _TNA),
                      [(cond, _full((16, D))), (w, pl.BlockSpec((1, D, _TNA), lambda l, j: (l, 0, j))),
                       (b, pl.BlockSpec((1, 1, _TNA), lambda l, j: (l, 0, j)))],
                      [(_sds((2, 16, ADA_LOC), F32), pl.BlockSpec((1, 16, _TNA), lambda l, j: (l, 0, j)), False)])[0]


def _adaln_bwd(name, c8, cc8, w, dl, dc):
    def fn(ids, cv, ccv, wv, dlv, dcv):
        dcs = jnp.broadcast_to(jnp.sum(dcv[0], axis=0, keepdims=True), dcv[0].shape)
        dw = hdot(jax.nn.silu(cv), dlv[0], 0, 0) + hdot(jax.nn.silu(ccv), dcs, 0, 0)
        s = jax.nn.sigmoid(ccv)
        rid = lax.broadcasted_iota(jnp.int32, ccv.shape, 0)
        dcc = jnp.where(rid == 0, bdot(dcs, wv[0], 1, 1) * s * (1.0 + ccv * (1.0 - s)), 0.0)
        return dw[None], dcc

    dspec = pl.BlockSpec((1, 8, _TNA), lambda l, j: (l, 0, j))
    return _rows_call(name, fn, (2, ADA_LOC // _TNA),
                      [(c8, _full((8, D))), (cc8, _full((8, D))), (w, pl.BlockSpec((1, D, _TNA), lambda l, j: (l, 0, j))),
                       (dl, dspec), (dc, dspec)],
                      [(_sds((2, D, ADA_LOC), F32), pl.BlockSpec((1, D, _TNA), lambda l, j: (l, 0, j)), False),
                       (_sds((8, D), F32), _full((8, D)), True)], acc_axes=(0, 1))


def _adamw_fn(w, g, m, v):
    m = ADAM_B1 * m + (1.0 - ADAM_B1) * g
    v = ADAM_B2 * v + (1.0 - ADAM_B2) * (g * g)
    m_hat = m / (1.0 - ADAM_B1 ** ADAM_STEP)
    v_hat = v / (1.0 - ADAM_B2 ** ADAM_STEP)
    return -ADAM_LR * (m_hat / (jnp.sqrt(v_hat) + ADAM_EPS) + ADAM_WD * w), m, v


def _adamw(name, w, g, m, v):
    r, c = w.shape
    tr = _tile(r, max(8, (1 << 20) // (4 * c)), 8)
    spec = pl.BlockSpec((tr, c), lambda i: (i, 0))
    o = (_sds((r, c), F32), spec, False)
    return _rows_call(name, lambda ids, *a: _adamw_fn(*a), (r // tr,), [(x, spec) for x in (w, g, m, v)], [o, o, o],
                      sem=("parallel",))


def _coords():
    return lax.axis_index("x"), lax.axis_index("y"), lax.axis_index("c")


def _other_chips(x, y):
    return [(1 - x, y), (x, 1 - y), (1 - x, 1 - y)]


def _allgather_small(name, blk):
    m_per, n = blk.shape

    def body(x_ref, out_ref, send_sems, recv_sems, local_sem):
        x, y, c = _coords()
        me, sibling = (x, y, c), (x, y, 1 - c)
        chips = _other_chips(x, y)

        def rows(px, py, pc):
            return out_ref.at[pl.ds((4 * px + 2 * py + pc) * m_per, m_per), :]

        def copy(k, block, to, src=None):
            return pltpu.make_async_remote_copy(
                src_ref=rows(*block) if src is None else src, dst_ref=rows(*block), send_sem=send_sems.at[k],
                recv_sem=recv_sems.at[k], device_id=to, device_id_type=MESH)

        mine = pltpu.make_async_copy(x_ref, rows(*me), local_sem)
        mine.start()
        first = [copy(0, me, sibling, src=x_ref)]
        first += [copy(1 + j, me, (*chip, c), src=x_ref) for j, chip in enumerate(chips)]
        for cp in first:
            cp.start()
        passed = [copy(4 + j, (*chip, c), sibling) for j, chip in enumerate(chips)]
        for j, chip in enumerate(chips):
            copy(1 + j, (*chip, c), me).wait_recv()
            passed[j].start()
        copy(0, sibling, me).wait_recv()
        for j, chip in enumerate(chips):
            copy(4 + j, (*chip, 1 - c), me).wait_recv()
        for cp in first + passed:
            cp.wait_send()
        mine.wait()

    return pl.pallas_call(
        body, name=name, out_shape=_sds((N_DEV * m_per, n), blk.dtype),
        in_specs=[pl.BlockSpec(memory_space=pltpu.VMEM)], out_specs=pl.BlockSpec(memory_space=pltpu.VMEM),
        scratch_shapes=[pltpu.SemaphoreType.DMA((7,)), pltpu.SemaphoreType.DMA((7,)), pltpu.SemaphoreType.DMA],
        compiler_params=pltpu.CompilerParams(vmem_limit_bytes=VMEM_LIMIT),
    )(blk)


_ANY = pl.BlockSpec(memory_space=pl.ANY)


def _allgather_shards(name, loc):
    _, r, cdim = loc.shape

    def body(loc_ref, out_ref, send_sems, recv_sems, local_sem):
        x, y, c = _coords()
        k = 2 * x + y
        sibling = (x, y, 1 - c)
        chips = _other_chips(x, y)
        mine = pltpu.make_async_copy(loc_ref, out_ref.at[k], local_sem)
        mine.start()

        def copy(s, kk, half, to, src=None):
            dst = out_ref.at[kk, half]
            return pltpu.make_async_remote_copy(src_ref=dst if src is None else src, dst_ref=dst, send_sem=send_sems.at[s],
                                                recv_sem=recv_sems.at[s], device_id=to, device_id_type=MESH)

        first = [copy(j, k, c, (*chip, c), src=loc_ref.at[c]) for j, chip in enumerate(chips)]
        for cp in first:
            cp.start()
        passed = [copy(3 + j, 2 * cx + cy, c, sibling) for j, (cx, cy) in enumerate(chips)]
        for j, (cx, cy) in enumerate(chips):
            copy(j, 2 * cx + cy, c, sibling).wait_recv()
            passed[j].start()
        for j, (cx, cy) in enumerate(chips):
            copy(3 + j, 2 * cx + cy, 1 - c, sibling).wait_recv()
        for cp in first + passed:
            cp.wait_send()
        mine.wait()

    return pl.pallas_call(
        body, name=name, out_shape=_sds((N_CHIP, 2, r, cdim), loc.dtype), in_specs=[_ANY], out_specs=_ANY,
        scratch_shapes=[pltpu.SemaphoreType.DMA((6,)), pltpu.SemaphoreType.DMA((6,)), pltpu.SemaphoreType.DMA],
    )(loc)


def _rs_pair_exchange(name, g):
    _, _, r, cdim = g.shape

    def body(g_ref, out_ref, send_sems, recv_sems):
        x, y, c = _coords()
        sibling = (x, y, 1 - c)
        cps = [pltpu.make_async_remote_copy(src_ref=g_ref.at[kk, 1 - c], dst_ref=out_ref.at[kk], send_sem=send_sems.at[kk],
                                            recv_sem=recv_sems.at[kk], device_id=sibling, device_id_type=MESH)
               for kk in range(N_CHIP)]
        for cp in cps:
            cp.start()
        for cp in cps:
            cp.wait()

    return pl.pallas_call(
        body, name=name, out_shape=_sds((N_CHIP, r, cdim), g.dtype), in_specs=[_ANY], out_specs=_ANY,
        scratch_shapes=[pltpu.SemaphoreType.DMA((N_CHIP,)), pltpu.SemaphoreType.DMA((N_CHIP,))],
    )(g)


def _rs_pair_add(name, g, got):
    _, _, r, cdim = g.shape
    tr = _tile(r, 512, 8)
    c_idx = jnp.reshape(lax.axis_index("c"), (1,)).astype(jnp.int32)

    def body(c_ref, a_ref, b_ref, o_ref):
        o_ref[...] = a_ref[0] + b_ref[...]

    return pl.pallas_call(
        body, name=name, out_shape=_sds((N_CHIP, r, cdim), F32),
        grid_spec=pltpu.PrefetchScalarGridSpec(
            num_scalar_prefetch=1, grid=(N_CHIP, r // tr),
            in_specs=[pl.BlockSpec((1, 1, tr, cdim), lambda kk, i, cr: (kk, cr[0], i, 0)),
                      pl.BlockSpec((1, tr, cdim), lambda kk, i, cr: (kk, i, 0))],
            out_specs=pl.BlockSpec((1, tr, cdim), lambda kk, i, cr: (kk, i, 0))),
        compiler_params=_cp(("parallel", "parallel")),
    )(c_idx, g, got)


def _rs_chip_exchange(name, s1):
    _, r, cdim = s1.shape

    def body(s_ref, out_ref, send_sems, recv_sems, local_sem):
        x, y, c = _coords()
        k = 2 * x + y
        chips = _other_chips(x, y)
        mine = pltpu.make_async_copy(s_ref.at[k], out_ref.at[k], local_sem)
        mine.start()
        cps = [pltpu.make_async_remote_copy(src_ref=s_ref.at[2 * cx + cy], dst_ref=out_ref.at[k], send_sem=send_sems.at[j],
                                            recv_sem=recv_sems.at[j], device_id=(cx, cy, c), device_id_type=MESH)
               for j, (cx, cy) in enumerate(chips)]
        for cp in cps:
            cp.start()
        for j, (cx, cy) in enumerate(chips):
            pltpu.make_async_remote_copy(src_ref=s_ref.at[k], dst_ref=out_ref.at[2 * cx + cy], send_sem=send_sems.at[j],
                                         recv_sem=recv_sems.at[j], device_id=(cx, cy, c), device_id_type=MESH).wait_recv()
        for cp in cps:
            cp.wait_send()
        mine.wait()

    return pl.pallas_call(
        body, name=name, out_shape=_sds((N_CHIP, r, cdim), s1.dtype), in_specs=[_ANY], out_specs=_ANY,
        scratch_shapes=[pltpu.SemaphoreType.DMA((3,)), pltpu.SemaphoreType.DMA((3,)), pltpu.SemaphoreType.DMA],
    )(s1)


def _sum_slots(name, a):
    s, r, cdim = a.shape
    tr = _tile(r, 512, 8)

    def fn(ids, av):
        tot = av[0]
        for i in range(1, s):
            tot = tot + av[i]
        return (tot,)

    return _rows_call(name, fn, (r // tr,), [(a, pl.BlockSpec((s, tr, cdim), lambda i: (0, i, 0)))],
                      [(_sds((r, cdim), F32), pl.BlockSpec((tr, cdim), lambda i: (i, 0)), False)], sem=("parallel",))[0]


def _pair_allgather(name, half):
    r, cdim = half.shape

    def body(h_ref, out_ref, send_sem, recv_sem, local_sem):
        x, y, c = _coords()
        mine = pltpu.make_async_copy(h_ref, out_ref.at[c], local_sem)
        mine.start()
        cp = pltpu.make_async_remote_copy(src_ref=h_ref, dst_ref=out_ref.at[c], send_sem=send_sem, recv_sem=recv_sem,
                                          device_id=(x, y, 1 - c), device_id_type=MESH)
        cp.start()
        pltpu.make_async_remote_copy(src_ref=h_ref, dst_ref=out_ref.at[1 - c], send_sem=send_sem, recv_sem=recv_sem,
                                     device_id=(x, y, 1 - c), device_id_type=MESH).wait_recv()
        cp.wait_send()
        mine.wait()

    return pl.pallas_call(
        body, name=name, out_shape=_sds((2, r, cdim), half.dtype), in_specs=[_ANY], out_specs=_ANY,
        scratch_shapes=[pltpu.SemaphoreType.DMA, pltpu.SemaphoreType.DMA, pltpu.SemaphoreType.DMA],
    )(half)


def _reduce_scatter(g):
    got = _rs_pair_exchange("rs_pair_exchange", g)
    s1 = _rs_pair_add("rs_pair_add", g, got)
    slots = _rs_chip_exchange("rs_chip_exchange", s1)
    red = _sum_slots("rs_chip_sum", slots)
    return _pair_allgather("rs_pair_allgather", red)


PACK_C = 1024
_SHARDED = (("w_in", (2, D, IN_W // 4), 2), ("w_br_a", (2, 512, D // 4), 2), ("w_br_b", (2, 512, D // 4), 2),
            ("w_br_c", (2, 512, D // 4), 2), ("w_out", (2, D // 4, D), 1), ("w_ffn_up", (2, D, 2 * FFN // 4), 2),
            ("w_ffn_down", (2, FFN // 4, D), 1), ("conv_w", (2, 3, 2 * FFN // 4), 2), ("w_alpha2", (2, 2, 16, GLA_QK // 4), 3),
            ("b_alpha", (2, 2, GLA_QK // 4), 2))


def _prod(shape):
    n = 1
    for s in shape:
        n *= s
    return n


_PACK_N = sum(_prod(s) for _, s, _ in _SHARDED)
_PACK_R = -(-_PACK_N // (PACK_C * 32)) * 16


def _pack(shards, dtype):
    flat = jnp.concatenate([shards[n].astype(dtype).reshape(-1) for n, _, _ in _SHARDED])
    flat = jnp.concatenate([flat, jnp.zeros((2 * _PACK_R * PACK_C - _PACK_N,), dtype)])
    return flat.reshape(2, _PACK_R, PACK_C)


def _unpack(packed):
    flat = packed.reshape(-1)
    out, o = {}, 0
    for n, s, _ in _SHARDED:
        out[n] = flat[o:o + _prod(s)].reshape(s)
        o += _prod(s)
    return out


def _unpack_full(gathered):
    per_chip = [_unpack(gathered[k]) for k in range(N_CHIP)]
    return {n: jnp.concatenate([pc[n] for pc in per_chip], axis=ax) for n, _, ax in _SHARDED}


def _pack_full(full, dtype):
    blocks = []
    for k in range(N_CHIP):
        shards = {}
        for n, s, ax in _SHARDED:
            shards[n] = lax.slice_in_dim(full[n], k * s[ax], (k + 1) * s[ax], axis=ax)
        blocks.append(_pack(shards, dtype))
    return jnp.stack(blocks)


def _rope_tables(tx):
    pos = jnp.arange(tx, dtype=jnp.int32)
    inv_freq = 10000.0 ** (-jnp.arange(16, dtype=F32) / 16)
    ang_r = (pos // GRID_W).astype(F32)[:, None] * inv_freq
    ang_c = (pos % GRID_W).astype(F32)[:, None] * inv_freq
    ang = jnp.concatenate([ang_r, ang_r, ang_c, ang_c], axis=-1)
    sign = jnp.concatenate([-jnp.ones((16,), F32), jnp.ones((16,), F32)] * 2)
    cos = jnp.concatenate([jnp.ones((TC, HD), F32), jnp.cos(ang)], axis=0)
    sin = jnp.concatenate([jnp.zeros((TC, HD), F32), jnp.sin(ang) * sign], axis=0)
    return jnp.tile(cos, (1, 2)), jnp.tile(sin, (1, 2))


def _lane_consts():
    l = jnp.arange(512)
    seg = (l[:, None] // HD == l[None, :] // HD).astype(F32) / HD
    partner = jnp.where(l % 32 < 16, l + 16, l - 16)
    perm = (l[:, None] == partner[None, :]).astype(F32)
    return seg, perm


def _heads(a, n):
    return a.reshape(a.shape[0], n, HD).transpose(1, 0, 2)


def _unheads(a):
    return a.transpose(1, 0, 2).reshape(a.shape[1], a.shape[0] * HD)


def _gather_f32_shards(conv_w, b_alpha):
    n_cw, n_ba = _prod(conv_w.shape), _prod(b_alpha.shape)
    flat = jnp.concatenate([conv_w.reshape(-1), b_alpha.reshape(-1), jnp.zeros((16 * PACK_C - n_cw - n_ba,), F32)])
    got = _allgather_small("gather_f32_shards", flat.reshape(16, PACK_C)).reshape(N_CHIP, 2, 16 * PACK_C)[:, 0]
    cw = jnp.concatenate([got[k, :n_cw].reshape(conv_w.shape) for k in range(N_CHIP)], axis=2)
    ba = jnp.concatenate([got[k, n_cw:n_cw + n_ba].reshape(b_alpha.shape) for k in range(N_CHIP)], axis=2)
    return cw, ba


def _layer_params(l, wfull, small):
    w2 = wfull["w_alpha2"][l]
    w2pad = jnp.zeros((128, 512), F32).at[0:16, 0:256].set(w2[0]).at[16:32, 256:512].set(w2[1])
    return dict(
        w_in=_to_new_cols(wfull["w_in"][l]), wa=wfull["w_br_a"][l], wb=wfull["w_br_b"][l], wc=wfull["w_br_c"][l],
        w_out=wfull["w_out"][l], w_up=wfull["w_ffn_up"][l], w_down=wfull["w_ffn_down"][l],
        cw=small["conv_w_full"][l], cb=small["conv_b"][l][None], w2=w2pad.astype(F32),
        b2=small["b_alpha_full"][l].reshape(1, 512),
        g1=small["norm1_g"][l][None], g2=small["norm2_g"][l][None], gq=jnp.tile(small["q_norm_g"][l], 8)[None],
        gk=jnp.tile(small["k_norm_g"][l], 2)[None], ggm=small["gmlp_norm_g"][l][None], ws=small["w_spatial"][l],
        bst=small["b_spatial"][l].T, ggl=small["gla_norm_g"][l][None])


def _layer_fwd(l, last, x, h1, mod, P, tabs):
    cos, sin, seg, perm = tabs
    n = "l%d_" % l
    s = dict(x=x, h1=h1)
    p = _mm(n + "in_proj", h1, P["w_in"], "nn", F32, tm_t=768, tn_t=2176)
    s["p"] = p
    s["gm"] = _gmlp_fwd(n + "gmlp", p, P["ggm"], P["ws"], P["bst"])
    qr, kr, vb = _qk_fwd(n + "qk_prep", p, P["gq"], P["gk"], cos, sin, seg, perm)
    qh, kh, vh = _heads(qr, NQ), _heads(kr, NKV), _heads(vb, NKV)
    s["qh"], s["kh"], s["vh"] = qh, kh, vh
    ox, lse_x = _attn_fwd(n + "attn_x", qh[:, TC:], kh, vh)
    s["ox"], s["lse_x"] = ox, lse_x
    if last:
        oc = jnp.zeros((NQ, TC, HD), F32)
    else:
        oc, lse_c = _attn_fwd(n + "attn_c", qh[:, :TC], kh[:, :TC], vh[:, :TC])
        s["oc"], s["lse_c"] = oc, lse_c
    s["att"] = _unheads(jnp.concatenate([oc, ox], axis=1)).astype(BF16)
    la = _decay_fwd(n + "gla_decay", p, P["w2"], P["b2"])
    s["la"] = la
    s["of"], s["sf"] = _gla_fwd(n + "gla_scan_f", p, la, False)
    s["ob"], s["sb"] = _gla_fwd(n + "gla_scan_b", p, la, True)
    s["gla"] = _gla_out_fwd(n + "gla_out", s["of"], s["ob"], p, P["ggl"])
    s["merged"] = _merge_fwd(n + "merge", s["gm"], s["att"], s["gla"], P["wa"], P["wb"], P["wc"], p)
    s["mix"] = _mm(n + "out_proj", s["merged"], P["w_out"], "nn", F32)
    s["x_mid"], s["h2"] = _res_nm_fwd(n + "res1_norm2", x, s["mix"], mod, 2, mod, P["g2"], 3, 4)
    s["a"] = _mm(n + "ffn_up", s["h2"], P["w_up"], "nn", F32)
    s["act"] = _conv_fwd(n + "conv_gate", s["a"], P["cw"], P["cb"])
    s["f"] = _mm(n + "ffn_down", s["act"], P["w_down"], "nn", F32)
    return s


def _layer_bwd(l, last, s, mod, P, tabs, dx_mid, df, gw):
    cos, sin, seg, perm = tabs
    n = "l%d_b_" % l
    t = dx_mid.shape[0]
    p = s["p"]
    gw["w_ffn_down"] = _mm(n + "ffn_down_w", s["act"], df, "tn", F32)
    dact = _mm(n + "ffn_down_x", df, P["w_down"], "nt", F32)
    dconv = _conv_bwd_gate(n + "conv_gate", s["a"], P["cw"], P["cb"], dact)
    da, gw["conv_w"], dcb = _conv_bwd_in(n + "conv_in", s["a"], P["cw"], dconv)
    gw["conv_b"] = dcb[0]
    gw["w_ffn_up"] = _mm(n + "ffn_up_w", s["h2"], da, "tn", F32)
    dh2 = _mm(n + "ffn_up_x", da, P["w_up"], "nt", F32)
    dx, dmix, dmod_a, dmod_b, dg2 = _res_nm_bwd(n + "res1_norm2", s["x"], s["mix"], mod, 2, mod, P["g2"], 3, 4, dx_mid, dh2)
    dmod = dmod_a + dmod_b
    gw["norm2_g"] = dg2[0]
    gw["w_out"] = _mm(n + "out_proj_w", s["merged"], dmix, "tn", F32)
    dmerged = _mm(n + "out_proj_x", dmix, P["w_out"], "nt", F32)
    dya, dyb, dyc, dga, dgb, dgc = _merge_bwd(n + "merge", s["gm"], s["att"], s["gla"], P["wa"], P["wb"], P["wc"], p, dmerged)
    gw["w_br_a"] = _mm(n + "br_a_w", s["gm"], dya, "tn", F32)
    gw["w_br_b"] = _mm(n + "br_b_w", s["att"], dyb, "tn", F32)
    gw["w_br_c"] = _mm(n + "br_c_w", s["gla"], dyc, "tn", F32)
    dgm = _mm(n + "br_a_x", dya, P["wa"], "nt", F32)
    datt = _mm(n + "br_b_x", dyb, P["wb"], "nt", F32)
    dgla = _mm(n + "br_c_x", dyc, P["wc"], "nt", F32)
    du, dv_g, dggm, dws, dbst = _gmlp_bwd(n + "gmlp", p, P["ggm"], P["ws"], P["bst"], dgm)
    gw["gmlp_norm_g"], gw["w_spatial"], gw["b_spatial"] = dggm[0], dws, dbst.T
    doh = _heads(datt, NQ)
    qh, kh, vh = s["qh"], s["kh"], s["vh"]
    dqx, delta_x = _attn_bwd_q(n + "attn_x_q", qh[:, TC:], kh, vh, s["ox"], doh[:, TC:], s["lse_x"])
    row = lambda a: a.reshape(a.shape[0], 1, a.shape[1])
    dkh, dvh = _attn_bwd_kv(n + "attn_x_kv", qh[:, TC:], kh, vh, doh[:, TC:], row(s["lse_x"]), row(delta_x))
    if last:
        dqc = jnp.zeros((NQ, TC, HD), F32)
    else:
        dqc, delta_c = _attn_bwd_q(n + "attn_c_q", qh[:, :TC], kh[:, :TC], vh[:, :TC], s["oc"], doh[:, :TC], s["lse_c"])
        dkc, dvc = _attn_bwd_kv(n + "attn_c_kv", qh[:, :TC], kh[:, :TC], vh[:, :TC], doh[:, :TC], row(s["lse_c"]), row(delta_c))
        pad = jnp.zeros((NKV, t - TC, HD), F32)
        dkh = dkh + jnp.concatenate([dkc, pad], axis=1)
        dvh = dvh + jnp.concatenate([dvc, pad], axis=1)
    dqr = _unheads(jnp.concatenate([dqc, dqx], axis=1))
    dq, dk, dgq, dgk = _qk_bwd(n + "qk_prep", p, P["gq"], P["gk"], cos, sin, seg, perm, dqr, _unheads(dkh))
    gw["q_norm_g"], gw["k_norm_g"] = dgq.reshape(8, HD).sum(0), dgk.reshape(2, HD).sum(0)
    dv_att = _unheads(dvh).astype(BF16)
    do, dr, dggl = _gla_out_bwd(n + "gla_out", s["of"], s["ob"], p, P["ggl"], dgla)
    gw["gla_norm_g"] = dggl[0]
    dq_f, dk_f, dv_f, dla_f = _gla_bwd(n + "gla_scan_f", p, s["la"], s["sf"], do, False)
    dglq, dglk, dglv, dla_b = _gla_bwd(n + "gla_scan_b", p, s["la"], s["sb"], do, True, prev=(dq_f, dk_f, dv_f))
    dab, dw2, db2 = _decay_bwd(n + "gla_decay", p, P["w2"], P["b2"], dla_f, dla_b)
    gw["w_alpha2"] = jnp.stack([dw2[0:16, 0:256], dw2[16:32, 256:512]])
    gw["b_alpha"] = db2.reshape(2, 256)
    bf = lambda a: a.astype(BF16)
    dp = jnp.concatenate([dga, dgb, dgc, du, dv_g, dq, bf(dglv), dr, bf(dglq), bf(dglk), dk, dv_att, dab], axis=-1)
    gw["w_in"] = _to_ref_cols(_mm(n + "in_proj_w", s["h1"], dp, "tn", F32, tn_t=2176, tk_t=768))
    dh1 = _mm(n + "in_proj_x", dp, P["w_in"], "nt", F32, tk_t=2176)
    return dx, dh1, dmod


_SMALL = (("norm1_g", (2, D)), ("norm2_g", (2, D)), ("q_norm_g", (2, HD)), ("k_norm_g", (2, HD)), ("gmlp_norm_g", (2, GW)),
          ("gla_norm_g", (2, GLA_V)), ("w_spatial", (2, 4, 128, 128)), ("b_spatial", (2, 4, 128)), ("conv_b", (2, 2 * FFN)),
          ("final_norm_g", (D,)))
_SMALL_N = 2 * 2 * ADA_W + sum(_prod(s) for _, s in _SMALL)
_SMALL_R = -(-_SMALL_N // (PACK_C * 8)) * 8


def _mod_tables(c, c_ctx, w_ada, b_ada, k):
    x, y, cc = _coords()
    me = 4 * x + 2 * y + cc
    c_all = _allgather_small("gather_c", jnp.concatenate([c, jnp.zeros((7, D), F32)], axis=0))
    c8 = c_all.reshape(N_DEV, 8, D)[:, 0]
    cond = jnp.concatenate([c8, c_ctx[None], jnp.zeros((7, D), F32)], axis=0)
    b_loc = lax.dynamic_slice_in_dim(b_ada, k * ADA_LOC, ADA_LOC, axis=1)[:, None, :]
    m_loc = _adaln_fwd("adaln", cond, w_ada, b_loc)
    m_all = _allgather_small("gather_mod", m_loc.reshape(32, ADA_LOC)).reshape(N_CHIP, 2, 2, 16, ADA_LOC)[:, 0]
    m_all = m_all.transpose(1, 2, 0, 3).reshape(2, 16, ADA_W)
    rows = jnp.stack([m_all[:, 8], lax.dynamic_index_in_dim(m_all, me, axis=1, keepdims=False)], axis=1)
    return rows.reshape(2, 2, 6, D), c8


def _step(x, c, ctx, c_ctx, W, tgt):
    xc, yc, cc = _coords()
    k = 2 * xc + yc
    tx = x.shape[0]
    t = TC + tx
    small = {n: W[n] for n, _ in _SMALL}
    small["conv_w_full"], small["b_alpha_full"] = _gather_f32_shards(W["conv_w"], W["b_alpha"])

    gathered = _allgather_shards("gather_weights", _pack({n: W[n] for n, _, _ in _SHARDED}, BF16))
    wfull = _unpack_full(gathered)
    mods, c8 = _mod_tables(c, c_ctx, W["w_ada"], W["b_ada"], k)
    tabs = _rope_tables(tx) + _lane_consts()
    params = [_layer_params(l, wfull, small) for l in range(2)]

    xs = jnp.concatenate([ctx, x], axis=0)
    h1 = _nm_fwd("l0_norm1", xs, mods[0], params[0]["g1"], 0, 1)
    s0 = _layer_fwd(0, False, xs, h1, mods[0], params[0], tabs)
    x1, h1b = _res_nm_fwd("l0_res2_norm1", s0["x_mid"], s0["f"], mods[0], 5, mods[1], params[1]["g1"], 0, 1)
    s1 = _layer_fwd(1, True, x1, h1b, mods[1], params[1], tabs)
    loss, dxm_l, df_l, dmod_head, dgf = _head("head", s1["x_mid"], s1["f"], mods[1], W["final_norm_g"][None], tgt)

    gws = [dict(), dict()]
    zc = lambda dt: jnp.zeros((TC, D), dt)
    dx1, dh1b, dmod1 = _layer_bwd(1, True, s1, mods[1], params[1], tabs, jnp.concatenate([zc(F32), dxm_l]),
                                  jnp.concatenate([zc(BF16), df_l]), gws[1])
    dxm0, df0, dmod0_g, dmod1_s, dg1b = _res_nm_bwd("l0_b_res2_norm1", s0["x_mid"], s0["f"], mods[0], 5, mods[1],
                                                    params[1]["g1"], 0, 1, dx1, dh1b)
    gws[1]["norm1_g"] = dg1b[0]
    dx0, dh1, dmod0 = _layer_bwd(0, False, s0, mods[0], params[0], tabs, dxm0, df0, gws[0])
    dxs, dmod0_s, dg1 = _nm_bwd("l0_b_norm1", xs, mods[0], params[0]["g1"], 0, 1, dx0, dh1)
    gws[0]["norm1_g"] = dg1[0]
    grad_x = dxs[TC:]
    dmods = jnp.stack([dmod0 + dmod0_g + dmod0_s, dmod1 + dmod1_s + dmod_head])

    stk = {n: jnp.stack([gws[0][n], gws[1][n]]) for n, _ in _SMALL if n != "final_norm_g"}
    stk["final_norm_g"] = dgf[0]
    flat = jnp.concatenate([dmods.reshape(-1)] + [stk[n].reshape(-1) for n, _ in _SMALL])
    flat = jnp.concatenate([flat, jnp.zeros((_SMALL_R * PACK_C - _SMALL_N,), F32)]).reshape(_SMALL_R, PACK_C)
    every = _allgather_small("gather_small_grads", flat).reshape(N_DEV, _SMALL_R, PACK_C)
    tot = _sum_slots("sum_small_grads", every).reshape(-1)
    grads, o = {}, 2 * 2 * ADA_W
    for n, shp in _SMALL:
        grads[n] = tot[o:o + _prod(shp)].reshape(shp)
        o += _prod(shp)
    grads["b_ada"] = tot[:2 * 2 * ADA_W].reshape(2, 2, ADA_W).sum(axis=1)

    dm_every = every[:, :2 * 2 * ADA_W // PACK_C].reshape(N_DEV, 2, 2, ADA_W)
    dm_loc = lax.dynamic_slice_in_dim(dm_every, k * ADA_LOC, ADA_LOC, axis=3).transpose(1, 2, 0, 3)
    cc8 = jnp.concatenate([c_ctx[None], jnp.zeros((7, D), F32)], axis=0)
    grads["w_ada"], dcc = _adaln_bwd("adaln_b", c8, cc8, W["w_ada"], dm_loc[:, 1], dm_loc[:, 0])
    dcc_every = _allgather_small("gather_dcctx", dcc * 0.5).reshape(N_DEV, 8, D)
    grads["c_ctx"] = _sum_slots("sum_dcctx", dcc_every)[0]

    full = {n: jnp.stack([gws[0][n], gws[1][n]]) for n, _, _ in _SHARDED}
    red = _unpack(_reduce_scatter(_pack_full(full, F32)))
    grads.update(red)
    return loss[0, 0], grad_x, grads


_WEIGHTS = ("c_ctx", "w_ada", "b_ada", "norm1_g", "norm2_g", "w_in", "q_norm_g", "k_norm_g", "gmlp_norm_g", "w_spatial",
            "b_spatial", "w_alpha2", "b_alpha", "gla_norm_g", "w_br_a", "w_br_b", "w_br_c", "w_out", "w_ffn_up", "conv_w",
            "conv_b", "w_ffn_down", "final_norm_g")
_BIG = ("w_ada", "w_in", "w_br_a", "w_br_b", "w_br_c", "w_out", "w_ffn_up", "w_ffn_down")


def _update(W, G, M, V):
    delta, new_m, new_v = {}, {}, {}
    for n in _BIG:
        shp = W[n].shape
        two = lambda a: a.reshape(-1, shp[-1])
        d, m, v = _adamw("adamw_" + n, two(W[n]), two(G[n]), two(M[n]), two(V[n]))
        delta[n], new_m[n], new_v[n] = d.reshape(shp), m.reshape(shp), v.reshape(shp)
    rest = [n for n in _WEIGHTS if n not in _BIG]
    tot = sum(_prod(W[n].shape) for n in rest)
    rows = -(-tot // (PACK_C * 8)) * 8

    def cat(dct):
        flat = jnp.concatenate([dct[n].reshape(-1) for n in rest] + [jnp.zeros((rows * PACK_C - tot,), F32)])
        return flat.reshape(rows, PACK_C)

    outs = _adamw("adamw_small", cat(W), cat(G), cat(M), cat(V))
    o = 0
    for n in rest:
        sz, shp = _prod(W[n].shape), W[n].shape
        delta[n], new_m[n], new_v[n] = (a.reshape(-1)[o:o + sz].reshape(shp) for a in outs)
        o += sz
    return delta, new_m, new_v


def kernel(x, c, ctx, c_ctx, w_ada, b_ada, norm1_g, norm2_g, w_in, q_norm_g, k_norm_g, gmlp_norm_g, w_spatial, b_spatial, w_alpha2, b_alpha, gla_norm_g, w_br_a, w_br_b, w_br_c, w_out, w_ffn_up, conv_w, conv_b, w_ffn_down, final_norm_g, loss_target, m_c_ctx, m_w_ada, m_b_ada, m_norm1_g, m_norm2_g, m_w_in, m_q_norm_g, m_k_norm_g, m_gmlp_norm_g, m_w_spatial, m_b_spatial, m_w_alpha2, m_b_alpha, m_gla_norm_g, m_w_br_a, m_w_br_b, m_w_br_c, m_w_out, m_w_ffn_up, m_conv_w, m_conv_b, m_w_ffn_down, m_final_norm_g, v_c_ctx, v_w_ada, v_b_ada, v_norm1_g, v_norm2_g, v_w_in, v_q_norm_g, v_k_norm_g, v_gmlp_norm_g, v_w_spatial, v_b_spatial, v_w_alpha2, v_b_alpha, v_gla_norm_g, v_w_br_a, v_w_br_b, v_w_br_c, v_w_out, v_w_ffn_up, v_conv_w, v_conv_b, v_w_ffn_down, v_final_norm_g):
    W = dict(c_ctx=c_ctx, w_ada=w_ada, b_ada=b_ada, norm1_g=norm1_g, norm2_g=norm2_g, w_in=w_in, q_norm_g=q_norm_g,
             k_norm_g=k_norm_g, gmlp_norm_g=gmlp_norm_g, w_spatial=w_spatial, b_spatial=b_spatial, w_alpha2=w_alpha2,
             b_alpha=b_alpha, gla_norm_g=gla_norm_g, w_br_a=w_br_a, w_br_b=w_br_b, w_br_c=w_br_c, w_out=w_out,
             w_ffn_up=w_ffn_up, conv_w=conv_w, conv_b=conv_b, w_ffn_down=w_ffn_down, final_norm_g=final_norm_g)
    M = dict(c_ctx=m_c_ctx, w_ada=m_w_ada, b_ada=m_b_ada, norm1_g=m_norm1_g, norm2_g=m_norm2_g, w_in=m_w_in,
             q_norm_g=m_q_norm_g, k_norm_g=m_k_norm_g, gmlp_norm_g=m_gmlp_norm_g, w_spatial=m_w_spatial,
             b_spatial=m_b_spatial, w_alpha2=m_w_alpha2, b_alpha=m_b_alpha, gla_norm_g=m_gla_norm_g, w_br_a=m_w_br_a,
             w_br_b=m_w_br_b, w_br_c=m_w_br_c, w_out=m_w_out, w_ffn_up=m_w_ffn_up, conv_w=m_conv_w, conv_b=m_conv_b,
             w_ffn_down=m_w_ffn_down, final_norm_g=m_final_norm_g)
    V = dict(c_ctx=v_c_ctx, w_ada=v_w_ada, b_ada=v_b_ada, norm1_g=v_norm1_g, norm2_g=v_norm2_g, w_in=v_w_in,
             q_norm_g=v_q_norm_g, k_norm_g=v_k_norm_g, gmlp_norm_g=v_gmlp_norm_g, w_spatial=v_w_spatial,
             b_spatial=v_b_spatial, w_alpha2=v_w_alpha2, b_alpha=v_b_alpha, gla_norm_g=v_gla_norm_g, w_br_a=v_w_br_a,
             w_br_b=v_w_br_b, w_br_c=v_w_br_c, w_out=v_w_out, w_ffn_up=v_w_ffn_up, conv_w=v_conv_w, conv_b=v_conv_b,
             w_ffn_down=v_w_ffn_down, final_norm_g=v_final_norm_g)
    loss_local, grad_x, G = _step(x[0], c, ctx[0], c_ctx, W, loss_target[0])
    loss = lax.psum(loss_local, ("x", "y", "c"))
    delta, new_m, new_v = _update(W, G, M, V)
    return (loss, grad_x[None], *[G[n] for n in _WEIGHTS], *[delta[n] for n in _WEIGHTS],
            *[new_m[n] for n in _WEIGHTS], *[new_v[n] for n in _WEIGHTS])
```

```python
import functools

import jax
import jax.numpy as jnp
from jax import lax
from jax.experimental import pallas as pl
from jax.experimental.pallas import tpu as pltpu

F32 = jnp.float32
BF16 = jnp.bfloat16

D = 1024
TC = 256
GRID_W = 64
EPS = 1e-6
HD = 64
NQ = 8
NKV = 2
QG = NQ // NKV
GLA_H = 4
GLA_DK = 64
GLA_DV = 128
GLA_QK = 256
GLA_V = 512
GLA_CHUNK = 64
GLA_TAU = 16.0
GW = 512
FFN = 2816
IN_W = 6432
PW = 6528
ADA_W = 6 * D
N_CHIP = 4
N_DEV = 8
ADA_LOC = ADA_W // N_CHIP

ADAM_LR = 0.001
ADAM_B1 = 0.9
ADAM_B2 = 0.999
ADAM_EPS = 1e-08
ADAM_WD = 0.01
ADAM_STEP = 10

TM = 256
NCB = TC // TM
LANE = 128
VMEM_LIMIT = 48 * 1024 * 1024
MESH = pl.DeviceIdType.MESH

_COLS = (("gA", 3360, 1024), ("gB", 4384, 1024), ("gC", 5408, 1024), ("gu", 0, 512), ("gv", 512, 512),
         ("q", 1024, 512), ("glv", 2304, 512), ("gr", 2848, 512), ("glq", 1792, 256), ("glk", 2048, 256),
         ("k", 1536, 128), ("v", 1664, 128), ("ab", 2816, 32))
OFF = {}
_o = 0
for _n, _s, _w in _COLS:
    OFF[_n] = _o
    _o += max(_w, LANE)
assert _o == PW


def _to_new_cols(w):
    parts = [w[..., s:s + n] for _, s, n in _COLS]
    pad = jnp.zeros(w.shape[:-1] + (PW - IN_W,), w.dtype)
    return jnp.concatenate(parts + [pad], axis=-1)


def _to_ref_cols(w):
    by_start = sorted(_COLS, key=lambda t: t[1])
    return jnp.concatenate([w[..., OFF[n]:OFF[n] + wd] for n, _, wd in by_start], axis=-1)


def _tile(n, target, align=LANE):
    best = None
    t = align
    while t <= min(n, target):
        if n % t == 0:
            best = t
        t += align
    assert best is not None, (n, target, align)
    return best


def _cp(sem=None):
    return pltpu.CompilerParams(dimension_semantics=sem, vmem_limit_bytes=VMEM_LIMIT)


def _bdot_impl(a, b, ca, cb):
    return lax.dot_general(a.astype(BF16), b.astype(BF16), (((ca,), (cb,)), ((), ())),
                           preferred_element_type=F32)


@functools.partial(jax.custom_vjp, nondiff_argnums=(2, 3))
def bdot(a, b, ca, cb):
    return _bdot_impl(a, b, ca, cb)


def _bdot_fwd(a, b, ca, cb):
    return _bdot_impl(a, b, ca, cb), (a, b)


def _bdot_bwd(ca, cb, res, g):
    a, b = res
    da = bdot(g, b, 1, 1 - cb) if ca == 1 else bdot(b, g, 1 - cb, 1)
    db = bdot(a, g, 1 - ca, 0) if cb == 0 else bdot(g, a, 0, 1 - ca)
    return da.astype(a.dtype), db.astype(b.dtype)


bdot.defvjp(_bdot_fwd, _bdot_bwd)


def hdot(a, b, ca=1, cb=0):
    return lax.dot_general(a, b, (((ca,), (cb,)), ((), ())), precision=lax.Precision.HIGHEST,
                           preferred_element_type=F32)


def _rms(x, g):
    return x * lax.rsqrt(jnp.mean(x * x, axis=-1, keepdims=True) + EPS) * g


def _gelu(x):
    return 0.5 * x * (1.0 + jnp.tanh(0.7978845608028654 * (x + 0.044715 * (x * x * x))))


def _log_sigmoid(z):
    return jnp.minimum(z, 0.0) - jnp.log(1.0 + jnp.exp(-jnp.abs(z)))


def _sel(mod, is_lat, idx):
    return jnp.where(is_lat, mod[1, idx:idx + 1, :], mod[0, idx:idx + 1, :])


def _rows_call(name, fn, grid, ins, outs, acc_axes=None, sem=None):
    n_in = len(ins)
    flags = [o[2] for o in outs]
    if acc_axes is None:
        acc_axes = (len(grid) - 1,)

    def body(*refs):
        ids = tuple(pl.program_id(a) for a in range(len(grid)))
        res = fn(ids, *[r[...] for r in refs[:n_in]])
        for r, v, acc in zip(refs[n_in:], res, flags):
            if acc:
                first = functools.reduce(jnp.logical_and, [ids[a] == 0 for a in acc_axes])

                @pl.when(first)
                def _():
                    r[...] = jnp.zeros_like(r)
                r[...] += v.astype(r.dtype)
            else:
                r[...] = v.astype(r.dtype)

    return pl.pallas_call(
        body, name=name, grid=grid, in_specs=[s for _, s in ins], out_specs=[o[1] for o in outs],
        out_shape=[o[0] for o in outs],
        compiler_params=_cp(sem if sem is not None else ("arbitrary",) * len(grid)),
    )(*[a for a, _ in ins])


def _sds(shape, dtype):
    return jax.ShapeDtypeStruct(shape, dtype)


def _rowspec(width, off=0, tm=TM):
    assert off % width == 0
    return pl.BlockSpec((tm, width), lambda i, o=off // width: (i, o))


def _full(shape):
    nd = len(shape)
    return pl.BlockSpec(shape, lambda *a: (0,) * nd)


def _mm(name, a, b, mode, out_dtype, tm_t=1056, tn_t=1408, tk_t=1408):
    if mode == "nn":
        (m, k), (_, n) = a.shape, b.shape
    elif mode == "nt":
        (m, k), (n, _) = a.shape, b.shape
    else:
        (k, m), (_, n) = a.shape, b.shape
    tm = _tile(m, tm_t, 8 if m % LANE else LANE)
    tn = _tile(n, tn_t)
    tk = _tile(k, tk_t)
    nk = k // tk
    if mode == "nn":
        dims, a_spec, b_spec = ((1,), (0,)), pl.BlockSpec((tm, tk), lambda i, j, l: (i, l)), pl.BlockSpec((tk, tn), lambda i, j, l: (l, j))
    elif mode == "nt":
        dims, a_spec, b_spec = ((1,), (1,)), pl.BlockSpec((tm, tk), lambda i, j, l: (i, l)), pl.BlockSpec((tn, tk), lambda i, j, l: (j, l))
    else:
        dims, a_spec, b_spec = ((0,), (0,)), pl.BlockSpec((tk, tm), lambda i, j, l: (l, i)), pl.BlockSpec((tk, tn), lambda i, j, l: (l, j))

    def body(a_ref, b_ref, o_ref, *scratch):
        l = pl.program_id(2)
        part = lax.dot_general(a_ref[...].astype(BF16), b_ref[...].astype(BF16), (dims, ((), ())),
                               preferred_element_type=F32)
        if nk == 1:
            o_ref[...] = part.astype(o_ref.dtype)
            return
        acc_ref = scratch[0]

        @pl.when(l == 0)
        def _():
            acc_ref[...] = part

        @pl.when(l > 0)
        def _():
            acc_ref[...] += part

        @pl.when(l == nk - 1)
        def _():
            o_ref[...] = acc_ref[...].astype(o_ref.dtype)

    return pl.pallas_call(
        body, name=name, grid=(m // tm, n // tn, nk), in_specs=[a_spec, b_spec],
        out_specs=pl.BlockSpec((tm, tn), lambda i, j, l: (i, j)), out_shape=_sds((m, n), out_dtype),
        scratch_shapes=[pltpu.VMEM((tm, tn), F32)] if nk > 1 else [],
        compiler_params=_cp(("parallel", "parallel", "arbitrary")),
    )(a, b)


def _nm_fn(is_lat, x, mod, g, shift, scale):
    return _rms(x, g) * (1.0 + _sel(mod, is_lat, scale)) + _sel(mod, is_lat, shift)


def _res_nm_fn(is_lat, x, br, modg, gate, mods, g, shift, scale):
    xn = x + _sel(modg, is_lat, gate) * br
    return xn, _nm_fn(is_lat, xn, mods, g, shift, scale)


def _nm_fwd(name, x, mod, g, shift, scale):
    t = x.shape[0]
    fn = lambda ids, xv, mv, gv: (_nm_fn(ids[0] >= NCB, xv, mv, gv, shift, scale),)
    return _rows_call(name, fn, (t // TM,), [(x, _rowspec(D)), (mod, _full((2, 6, D))), (g, _full((1, D)))],
                      [(_sds((t, D), BF16), _rowspec(D), False)])[0]


def _nm_bwd(name, x, mod, g, shift, scale, dx_res, dh):
    t = x.shape[0]

    def fn(ids, xv, mv, gv, dxr, dhv):
        _, vjp = jax.vjp(lambda a, b, c: _nm_fn(ids[0] >= NCB, a, b, c, shift, scale), xv, mv, gv)
        dx, dm, dg = vjp(dhv)
        return dx + dxr, dm, dg

    return _rows_call(name, fn, (t // TM,),
                      [(x, _rowspec(D)), (mod, _full((2, 6, D))), (g, _full((1, D))), (dx_res, _rowspec(D)), (dh, _rowspec(D))],
                      [(_sds((t, D), F32), _rowspec(D), False), (_sds((2, 6, D), F32), _full((2, 6, D)), True),
                       (_sds((1, D), F32), _full((1, D)), True)])


def _res_nm_fwd(name, x, br, modg, gate, mods, g, shift, scale):
    t = x.shape[0]
    fn = lambda ids, xv, bv, mg, ms, gv: _res_nm_fn(ids[0] >= NCB, xv, bv, mg, gate, ms, gv, shift, scale)
    return _rows_call(name, fn, (t // TM,),
                      [(x, _rowspec(D)), (br, _rowspec(D)), (modg, _full((2, 6, D))), (mods, _full((2, 6, D))), (g, _full((1, D)))],
                      [(_sds((t, D), F32), _rowspec(D), False), (_sds((t, D), BF16), _rowspec(D), False)])


def _res_nm_bwd(name, x, br, modg, gate, mods, g, shift, scale, dx_res, dh):
    t = x.shape[0]

    def fn(ids, xv, bv, mg, ms, gv, dxr, dhv):
        f = lambda a, b, c, d, e: _res_nm_fn(ids[0] >= NCB, a, b, c, gate, d, e, shift, scale)
        _, vjp = jax.vjp(f, xv, bv, mg, ms, gv)
        return vjp((dxr, dhv))

    m26 = (_sds((2, 6, D), F32), _full((2, 6, D)), True)
    return _rows_call(name, fn, (t // TM,),
                      [(x, _rowspec(D)), (br, _rowspec(D)), (modg, _full((2, 6, D))), (mods, _full((2, 6, D))), (g, _full((1, D))),
                       (dx_res, _rowspec(D)), (dh, _rowspec(D))],
                      [(_sds((t, D), F32), _rowspec(D), False), (_sds((t, D), BF16), _rowspec(D), False), m26, m26,
                       (_sds((1, D), F32), _full((1, D)), True)])


def _head(name, x_mid, f, mod, gf, tgt):
    tx = tgt.shape[0]

    def fn(ids, xv, fv, mv, gv, tv):
        def loss_fn(a, b, c, d):
            y = _rms(a + c[1, 5:6, :] * b, d)
            e = y - tv
            return 0.5 * jnp.sum(jnp.mean(e * e, axis=-1))
        loss, grads = jax.value_and_grad(loss_fn, argnums=(0, 1, 2, 3))(xv, fv, mv, gv)
        return (jnp.reshape(loss, (1, 1)),) + grads

    lat = pl.BlockSpec((TM, D), lambda i: (i + NCB, 0))
    return _rows_call(name, fn, (tx // TM,),
                      [(x_mid, lat), (f, lat), (mod, _full((2, 6, D))), (gf, _full((1, D))), (tgt, _rowspec(D))],
                      [(_sds((1, 1), F32), _full((1, 1)), True), (_sds((tx, D), F32), _rowspec(D), False),
                       (_sds((tx, D), BF16), _rowspec(D), False), (_sds((2, 6, D), F32), _full((2, 6, D)), True),
                       (_sds((1, D), F32), _full((1, D)), True)])


def _gmlp_fn(u, v, g, ws, bst):
    rows = []
    for r in range(u.shape[0] // 128):
        uu, vv = _gelu(u[128 * r:128 * r + 128]), _gelu(v[128 * r:128 * r + 128])
        cols = []
        for gi in range(4):
            sl = slice(128 * gi, 128 * gi + 128)
            f = bdot(ws[gi], _rms(vv[:, sl], g[:, sl]), 1, 0) + bst[:, gi:gi + 1]
            cols.append(uu[:, sl] * f)
        rows.append(jnp.concatenate(cols, axis=-1))
    return jnp.concatenate(rows, axis=0)


def _gmlp_ins(p, g, ws, bst):
    return [(p, _rowspec(GW, OFF["gu"])), (p, _rowspec(GW, OFF["gv"])), (g, _full((1, GW))),
            (ws, _full((4, 128, 128))), (bst, _full((128, 4)))]


def _gmlp_fwd(name, p, g, ws, bst):
    t = p.shape[0]
    return _rows_call(name, lambda ids, *a: (_gmlp_fn(*a),), (t // TM,), _gmlp_ins(p, g, ws, bst),
                      [(_sds((t, GW), BF16), _rowspec(GW), False)])[0]


def _gmlp_bwd(name, p, g, ws, bst, dgm):
    t = p.shape[0]

    def fn(ids, u, v, gv, wv, bv, dv):
        _, vjp = jax.vjp(_gmlp_fn, u, v, gv, wv, bv)
        return vjp(dv)

    return _rows_call(name, fn, (t // TM,), _gmlp_ins(p, g, ws, bst) + [(dgm, _rowspec(GW))],
                      [(_sds((t, GW), BF16), _rowspec(GW), False), (_sds((t, GW), BF16), _rowspec(GW), False),
                       (_sds((1, GW), F32), _full((1, GW)), True), (_sds((4, 128, 128), F32), _full((4, 128, 128)), True),
                       (_sds((128, 4), F32), _full((128, 4)), True)])


def _qk_fn(q, k, gq, gk, cos, sin, seg, perm):
    cq, sq = jnp.concatenate([cos] * 4, axis=-1), jnp.concatenate([sin] * 4, axis=-1)
    qn = q * lax.rsqrt(hdot(q * q, seg) + EPS) * gq
    kn = k * lax.rsqrt(hdot(k * k, seg[:128, :128]) + EPS) * gk
    qr = qn * cq + hdot(qn, perm) * sq
    kr = kn * cos + hdot(kn, perm[:128, :128]) * sin
    return qr * (HD ** -0.5), kr


def _qk_ins(p, gq, gk, cos, sin, seg, perm):
    return [(p, _rowspec(512, OFF["q"])), (p, _rowspec(128, OFF["k"])), (gq, _full((1, 512))), (gk, _full((1, 128))),
            (cos, _rowspec(128)), (sin, _rowspec(128)), (seg, _full((512, 512))), (perm, _full((512, 512)))]


def _qk_fwd(name, p, gq, gk, cos, sin, seg, perm):
    t = p.shape[0]
    fn = lambda ids, q, k, a, b, c, s, sg, pm, v: _qk_fn(q, k, a, b, c, s, sg, pm) + (v,)
    return _rows_call(name, fn, (t // TM,), _qk_ins(p, gq, gk, cos, sin, seg, perm) + [(p, _rowspec(128, OFF["v"]))],
                      [(_sds((t, 512), BF16), _rowspec(512), False), (_sds((t, 128), BF16), _rowspec(128), False),
                       (_sds((t, 128), BF16), _rowspec(128), False)])


def _qk_bwd(name, p, gq, gk, cos, sin, seg, perm, dqr, dkr):
    t = p.shape[0]

    def fn(ids, q, k, a, b, c, s, sg, pm, dq, dk):
        _, vjp = jax.vjp(lambda q_, k_, a_, b_: _qk_fn(q_, k_, a_, b_, c, s, sg, pm), q, k, a, b)
        return vjp((dq, dk))

    return _rows_call(name, fn, (t // TM,),
                      _qk_ins(p, gq, gk, cos, sin, seg, perm) + [(dqr, _rowspec(512)), (dkr, _rowspec(128))],
                      [(_sds((t, 512), BF16), _rowspec(512), False), (_sds((t, 128), BF16), _rowspec(128), False),
                       (_sds((1, 512), F32), _full((1, 512)), True), (_sds((1, 128), F32), _full((1, 128)), True)])


_ATT_TQ = 256
_ATT_TK = 768


def _attn_fwd(name, q, k, v):
    h, tq_all, _ = q.shape
    hkv, tk_all, _ = k.shape
    tq, tk = _tile(tq_all, _ATT_TQ), _tile(tk_all, _ATT_TK)
    nkc = tk_all // tk

    def body(q_ref, k_ref, v_ref, o_ref, lse_ref):
        qv = q_ref[...].reshape(QG * tq, HD)

        def step(j, carry):
            m, l, acc = carry
            off = pl.multiple_of(j * tk, tk)
            kk, vv = k_ref[0, pl.ds(off, tk), :], v_ref[0, pl.ds(off, tk), :]
            s = lax.dot_general(qv, kk, (((1,), (1,)), ((), ())), preferred_element_type=F32)
            m_new = jnp.maximum(m, jnp.max(s, axis=-1, keepdims=True))
            alpha = jnp.exp(m - m_new)
            pr = jnp.exp(s - m_new)
            l = alpha * l + jnp.sum(pr, axis=-1, keepdims=True)
            acc = alpha * acc + jnp.dot(pr.astype(BF16), vv, preferred_element_type=F32)
            return m_new, l, acc

        init = (jnp.full((QG * tq, 1), -jnp.inf, F32), jnp.zeros((QG * tq, 1), F32), jnp.zeros((QG * tq, HD), F32))
        m, l, acc = lax.fori_loop(0, nkc, step, init)
        o_ref[...] = (acc / l).reshape(QG, tq, HD)
        lse_ref[...] = (m + jnp.log(l)).reshape(QG, tq, 1)

    kv_spec = pl.BlockSpec((1, tk_all, HD), lambda g, i: (g, 0, 0))
    qspec = pl.BlockSpec((QG, tq, HD), lambda g, i: (g, i, 0))
    return pl.pallas_call(
        body, name=name, grid=(hkv, tq_all // tq), in_specs=[qspec, kv_spec, kv_spec],
        out_specs=[qspec, pl.BlockSpec((QG, tq, 1), lambda g, i: (g, i, 0))],
        out_shape=[_sds((h, tq_all, HD), F32), _sds((h, tq_all, 1), F32)],
        compiler_params=_cp(("parallel", "parallel")),
    )(q, k, v)


def _attn_bwd_q(name, q, k, v, o, do, lse):
    h, tq_all, _ = q.shape
    hkv, tk_all, _ = k.shape
    tq, tk = _tile(tq_all, _ATT_TQ), _tile(tk_all, _ATT_TK)
    nkc = tk_all // tk

    def body(q_ref, k_ref, v_ref, o_ref, do_ref, lse_ref, dq_ref, dl_ref):
        qs = [q_ref[g] for g in range(QG)]
        lses = [lse_ref[g] for g in range(QG)]
        deltas = [jnp.sum(do_ref[g] * o_ref[g], axis=-1, keepdims=True) for g in range(QG)]
        dobs = [do_ref[g].astype(BF16) for g in range(QG)]

        def step(j, dqs):
            off = pl.multiple_of(j * tk, tk)
            kk, vv = k_ref[0, pl.ds(off, tk), :], v_ref[0, pl.ds(off, tk), :]
            out = []
            for g in range(QG):
                s = lax.dot_general(qs[g], kk, (((1,), (1,)), ((), ())), preferred_element_type=F32)
                pr = jnp.exp(s - lses[g])
                dp = lax.dot_general(dobs[g], vv, (((1,), (1,)), ((), ())), preferred_element_type=F32)
                ds = pr * (dp - deltas[g])
                out.append(dqs[g] + jnp.dot(ds.astype(BF16), kk, preferred_element_type=F32))
            return tuple(out)

        res = lax.fori_loop(0, nkc, step, (jnp.zeros((tq, HD), F32),) * QG)
        for g in range(QG):
            dq_ref[g] = res[g]
            dl_ref[g] = deltas[g]

    kv_spec = pl.BlockSpec((1, tk_all, HD), lambda g, i: (g, 0, 0))
    qspec = pl.BlockSpec((QG, tq, HD), lambda g, i: (g, i, 0))
    cs = pl.BlockSpec((QG, tq, 1), lambda g, i: (g, i, 0))
    return pl.pallas_call(
        body, name=name, grid=(hkv, tq_all // tq), in_specs=[qspec, kv_spec, kv_spec, qspec, qspec, cs], out_specs=[qspec, cs],
        out_shape=[_sds((h, tq_all, HD), F32), _sds((h, tq_all, 1), F32)],
        compiler_params=_cp(("parallel", "parallel")),
    )(q, k, v, o, do, lse)


def _attn_bwd_kv(name, q, k, v, do, lse_row, delta_row):
    h, tq_all, _ = q.shape
    hkv, tk_all, _ = k.shape
    tq, tk = _tile(tq_all, 1024), _tile(tk_all, 1056)
    nq = tq_all // tq

    def body(q_ref, k_ref, v_ref, do_ref, lse_ref, dl_ref, dk_ref, dv_ref, dk_acc, dv_acc):
        i = pl.program_id(2)

        @pl.when(i == 0)
        def _():
            dk_acc[...] = jnp.zeros_like(dk_acc)
            dv_acc[...] = jnp.zeros_like(dv_acc)

        kk, vv = k_ref[0], v_ref[0]
        for g in range(QG):
            qv, dob = q_ref[g], do_ref[g].astype(BF16)
            st = lax.dot_general(kk, qv, (((1,), (1,)), ((), ())), preferred_element_type=F32)
            pt = jnp.exp(st - lse_ref[g])
            dv_acc[...] += jnp.dot(pt.astype(BF16), dob, preferred_element_type=F32)
            dpt = lax.dot_general(vv, dob, (((1,), (1,)), ((), ())), preferred_element_type=F32)
            dst = pt * (dpt - dl_ref[g])
            dk_acc[...] += jnp.dot(dst.astype(BF16), qv, preferred_element_type=F32)

        @pl.when(i == nq - 1)
        def _():
            dk_ref[0] = dk_acc[...]
            dv_ref[0] = dv_acc[...]

    ks = pl.BlockSpec((1, tk, HD), lambda g, j, i: (g, j, 0))
    qs = pl.BlockSpec((QG, tq, HD), lambda g, j, i: (g, i, 0))
    rs = pl.BlockSpec((QG, 1, tq), lambda g, j, i: (g, 0, i))
    return pl.pallas_call(
        body, name=name, grid=(hkv, tk_all // tk, nq), in_specs=[qs, ks, ks, qs, rs, rs], out_specs=[ks, ks],
        out_shape=[_sds((hkv, tk_all, HD), F32), _sds((hkv, tk_all, HD), F32)],
        scratch_shapes=[pltpu.VMEM((tk, HD), F32), pltpu.VMEM((tk, HD), F32)],
        compiler_params=_cp(("parallel", "parallel", "arbitrary")),
    )(q, k, v, do, lse_row, delta_row)


def _decay_fn(a, w2, b2):
    return _log_sigmoid(bdot(a, w2, 1, 0) + b2) / GLA_TAU


def _decay_fwd(name, p, w2, b2):
    t = p.shape[0]
    return _rows_call(name, lambda ids, a, w, b: (_decay_fn(a, w, b),), (t // TM,),
                      [(p, _rowspec(128, OFF["ab"])), (w2, _full((128, 512))), (b2, _full((1, 512)))],
                      [(_sds((t, 512), F32), _rowspec(512), False)])[0]


def _decay_bwd(name, p, w2, b2, dla_f, dla_b):
    t = p.shape[0]

    def fn(ids, a, w, b, df, db):
        _, vjp = jax.vjp(_decay_fn, a, w, b)
        return vjp(jnp.concatenate([df, db], axis=-1))

    return _rows_call(name, fn, (t // TM,),
                      [(p, _rowspec(128, OFF["ab"])), (w2, _full((128, 512))), (b2, _full((1, 512))),
                       (dla_f, _rowspec(256)), (dla_b, _rowspec(256))],
                      [(_sds((t, 128), BF16), _rowspec(128), False), (_sds((128, 512), F32), _full((128, 512)), True),
                       (_sds((1, 512), F32), _full((1, 512)), True)])


def _gla_consts(reverse):
    r = lax.broadcasted_iota(jnp.int32, (GLA_CHUNK, GLA_CHUNK), 0)
    c = lax.broadcasted_iota(jnp.int32, (GLA_CHUNK, GLA_CHUNK), 1)
    trib = (r <= c) if reverse else (r >= c)
    br = lax.broadcasted_iota(jnp.int32, (GLA_QK, GLA_V), 0) // GLA_DK
    bc = lax.broadcasted_iota(jnp.int32, (GLA_QK, GLA_V), 1) // GLA_DV
    lane_head = lax.broadcasted_iota(jnp.int32, (1, GLA_QK), 1) // GLA_DK
    return trib, (br == bc).astype(F32), lane_head


def _gla_chunk(q, k, v, la, s_in, consts):
    trib, bd, lane_head = consts
    cum = hdot(trib.astype(F32), la)
    tot = jnp.sum(la, axis=0, keepdims=True)
    q_in = q * (GLA_DK ** -0.5) * jnp.exp(cum)
    k_in = k * jnp.exp(-cum)
    k_st = k * jnp.exp(tot - cum)
    outs = []
    for h in range(GLA_H):
        att = bdot(jnp.where(lane_head == h, q_in, 0.0), k_in, 1, 1)
        att = jnp.where(trib, att, 0.0)
        outs.append(bdot(att, v[:, GLA_DV * h:GLA_DV * (h + 1)], 1, 0))
    o = jnp.concatenate(outs, axis=-1) + bdot(q_in, s_in, 1, 0)
    decay = jnp.exp(hdot(la, jnp.ones((GLA_CHUNK, GLA_V), F32), 0, 0))
    s_out = decay * s_in + bdot(k_st, v, 0, 0) * bd
    return o, s_out


def _gla_order(nb, reverse, backward):
    if not reverse:
        return (lambda s: nb - 1 - s) if backward else (lambda s: s)
    if backward:
        return lambda s: jnp.where(s == nb - 1, 0, s + 1)
    return lambda s: jnp.where(s == 0, 0, nb - s)


_NCH = TM // GLA_CHUNK


def _gla_fwd(name, p, la, reverse):
    t = p.shape[0]
    nb = t // TM
    order = _gla_order(nb, reverse, False)

    def body(q_ref, k_ref, v_ref, la_ref, o_ref, sv_ref, s_ref):
        @pl.when(pl.program_id(0) == 0)
        def _():
            s_ref[...] = jnp.zeros_like(s_ref)

        consts = _gla_consts(reverse)
        for c in (range(_NCH - 1, -1, -1) if reverse else range(_NCH)):
            rows = slice(GLA_CHUNK * c, GLA_CHUNK * (c + 1))
            s_in = s_ref[...]
            for h in range(GLA_H):
                sv_ref[c, h] = s_in[GLA_DK * h:GLA_DK * (h + 1), GLA_DV * h:GLA_DV * (h + 1)]
            o, s_out = _gla_chunk(q_ref[rows, :], k_ref[rows, :], v_ref[rows, :], la_ref[rows, :], s_in, consts)
            o_ref[rows, :] = o
            s_ref[...] = s_out

    def col(width, off):
        return pl.BlockSpec((TM, width), lambda s, o=off // width: (order(s), o))

    return pl.pallas_call(
        body, name=name, grid=(nb,),
        in_specs=[col(256, OFF["glq"]), col(256, OFF["glk"]), col(512, OFF["glv"]), col(256, 256 * int(reverse))],
        out_specs=[col(512, 0), pl.BlockSpec((_NCH, GLA_H, GLA_DK, GLA_DV), lambda s: (order(s), 0, 0, 0))],
        out_shape=[_sds((t, GLA_V), F32), _sds((t // GLA_CHUNK, GLA_H, GLA_DK, GLA_DV), F32)],
        scratch_shapes=[pltpu.VMEM((GLA_QK, GLA_V), F32)], compiler_params=_cp(("arbitrary",)),
    )(p, p, p, la)


def _gla_bwd(name, p, la, sv, do, reverse, prev=None):
    t = p.shape[0]
    nb = t // TM
    order = _gla_order(nb, reverse, True)
    n_prev = 0 if prev is None else 3

    def body(*refs):
        q_ref, k_ref, v_ref, la_ref, sv_ref, do_ref = refs[:6]
        prev_refs = refs[6:6 + n_prev]
        dq_ref, dk_ref, dv_ref, dla_ref, ds_ref = refs[6 + n_prev:]

        @pl.when(pl.program_id(0) == 0)
        def _():
            ds_ref[...] = jnp.zeros_like(ds_ref)

        consts = _gla_consts(reverse)
        zero = jnp.zeros((GLA_DK, GLA_DV), F32)
        for c in (range(_NCH) if reverse else range(_NCH - 1, -1, -1)):
            rows = slice(GLA_CHUNK * c, GLA_CHUNK * (c + 1))
            s_in = jnp.concatenate(
                [jnp.concatenate([sv_ref[c, h] if hh == h else zero for hh in range(GLA_H)], axis=-1) for h in range(GLA_H)], axis=0)
            _, vjp = jax.vjp(lambda a, b, cc, d, e: _gla_chunk(a, b, cc, d, e, consts),
                             q_ref[rows, :], k_ref[rows, :], v_ref[rows, :], la_ref[rows, :], s_in)
            dq, dk, dv, dla, ds_in = vjp((do_ref[rows, :], ds_ref[...]))
            if n_prev:
                dq, dk, dv = dq + prev_refs[0][rows, :], dk + prev_refs[1][rows, :], dv + prev_refs[2][rows, :]
            dq_ref[rows, :], dk_ref[rows, :], dv_ref[rows, :], dla_ref[rows, :] = dq, dk, dv, dla
            ds_ref[...] = ds_in

    def col(width, off):
        return pl.BlockSpec((TM, width), lambda s, o=off // width: (order(s), o))

    ins = [p, p, p, la, sv, do] + (list(prev) if n_prev else [])
    in_specs = [col(256, OFF["glq"]), col(256, OFF["glk"]), col(512, OFF["glv"]), col(256, 256 * int(reverse)),
                pl.BlockSpec((_NCH, GLA_H, GLA_DK, GLA_DV), lambda s: (order(s), 0, 0, 0)), col(512, 0)]
    in_specs += [col(256, 0), col(256, 0), col(512, 0)][:n_prev]
    return pl.pallas_call(
        body, name=name, grid=(nb,), in_specs=in_specs, out_specs=[col(256, 0), col(256, 0), col(512, 0), col(256, 0)],
        out_shape=[_sds((t, GLA_QK), F32), _sds((t, GLA_QK), F32), _sds((t, GLA_V), F32), _sds((t, GLA_QK), F32)],
        scratch_shapes=[pltpu.VMEM((GLA_QK, GLA_V), F32)], compiler_params=_cp(("arbitrary",)),
    )(*ins)


def _gla_out_fn(of, ob, r, g):
    o = of + ob
    cols = [_rms(o[:, GLA_DV * h:GLA_DV * (h + 1)], g[:, GLA_DV * h:GLA_DV * (h + 1)]) for h in range(GLA_H)]
    return jnp.concatenate(cols, axis=-1) * jax.nn.silu(r)


def _gla_out_fwd(name, of, ob, p, g):
    t = p.shape[0]
    return _rows_call(name, lambda ids, *a: (_gla_out_fn(*a),), (t // TM,),
                      [(of, _rowspec(512)), (ob, _rowspec(512)), (p, _rowspec(512, OFF["gr"])), (g, _full((1, 512)))],
                      [(_sds((t, 512), BF16), _rowspec(512), False)])[0]


def _gla_out_bwd(name, of, ob, p, g, dgla):
    t = p.shape[0]

    def fn(ids, a, b, r, gv, dv):
        _, vjp = jax.vjp(_gla_out_fn, a, b, r, gv)
        do, _, dr, dg = vjp(dv)
        return do, dr, dg

    return _rows_call(name, fn, (t // TM,),
                      [(of, _rowspec(512)), (ob, _rowspec(512)), (p, _rowspec(512, OFF["gr"])), (g, _full((1, 512))),
                       (dgla, _rowspec(512))],
                      [(_sds((t, 512), F32), _rowspec(512), False), (_sds((t, 512), BF16), _rowspec(512), False),
                       (_sds((1, 512), F32), _full((1, 512)), True)])


_TMM = 384


def _merge_fwd(name, gm, att, gla, wa, wb, wc, p):
    t = p.shape[0]
    row = lambda w, off=0: pl.BlockSpec((_TMM, w), lambda i, o=off // w: (i, o))

    def fn(ids, a, b, c, wa_, wb_, wc_, ga, gb, gc):
        return (jax.nn.sigmoid(ga) * bdot(a, wa_, 1, 0) + jax.nn.sigmoid(gb) * bdot(b, wb_, 1, 0)
                + jax.nn.sigmoid(gc) * bdot(c, wc_, 1, 0),)

    return _rows_call(name, fn, (t // _TMM,),
                      [(gm, row(512)), (att, row(512)), (gla, row(512)), (wa, _full((512, D))), (wb, _full((512, D))),
                       (wc, _full((512, D))), (p, row(D, OFF["gA"])), (p, row(D, OFF["gB"])), (p, row(D, OFF["gC"]))],
                      [(_sds((t, D), BF16), row(D), False)])[0]


def _merge_bwd(name, gm, att, gla, wa, wb, wc, p, dmerged):
    t = p.shape[0]
    row = lambda w, off=0: pl.BlockSpec((_TMM, w), lambda i, o=off // w: (i, o))

    def fn(ids, a, b, c, wa_, wb_, wc_, ga, gb, gc, dm):
        outs_y, outs_g = [], []
        for br, w, g in ((a, wa_, ga), (b, wb_, gb), (c, wc_, gc)):
            s = jax.nn.sigmoid(g)
            outs_y.append(dm * s)
            outs_g.append(dm * bdot(br, w, 1, 0) * s * (1.0 - s))
        return tuple(outs_y) + tuple(outs_g)

    o = (_sds((t, D), BF16), row(D), False)
    return _rows_call(name, fn, (t // _TMM,),
                      [(gm, row(512)), (att, row(512)), (gla, row(512)), (wa, _full((512, D))), (wb, _full((512, D))),
                       (wc, _full((512, D))), (p, row(D, OFF["gA"])), (p, row(D, OFF["gB"])), (p, row(D, OFF["gC"])),
                       (dmerged, row(D))], [o] * 6)


_TNC = 1408
_NJ = FFN // _TNC


def _shift_rows(x, prev8, next8, vp, vn):
    n = x.shape[0]
    rid = lax.broadcasted_iota(jnp.int32, x.shape, 0)
    xp = jnp.where(rid == 0, jnp.where(vp, prev8[7:8, :], 0.0), pltpu.roll(x, 1, 0))
    xn = jnp.where(rid == n - 1, jnp.where(vn, next8[0:1, :], 0.0), pltpu.roll(x, n - 1, 0))
    return xp, xn


def _seq_edges(i, t):
    start, end = i * TM, (i + 1) * TM
    return jnp.logical_and(start != 0, start != TC), jnp.logical_and(end != TC, end != t)


def _halo_specs(t, colmap):
    r8 = TM // 8
    main = pl.BlockSpec((TM, _TNC), lambda j, i: (i, colmap(j)))
    prev = pl.BlockSpec((8, _TNC), lambda j, i: (jnp.maximum(i * r8 - 1, 0), colmap(j)))
    nxt = pl.BlockSpec((8, _TNC), lambda j, i: (jnp.minimum((i + 1) * r8, t // 8 - 1), colmap(j)))
    return [main, prev, nxt]


def _conv3(x, xp, xn, w, b=None):
    y = xp * w[0:1, :] + x * w[1:2, :] + xn * w[2:3, :]
    return y if b is None else b + y


def _conv_fwd(name, a, cw, cb):
    t = a.shape[0]

    def fn(ids, ag, agp, agn, av, avp, avn, wg, wv, bg, bv):
        vp, vn = _seq_edges(ids[1], t)
        cg = _conv3(ag, *_shift_rows(ag, agp, agn, vp, vn), wg, bg)
        cv = _conv3(av, *_shift_rows(av, avp, avn, vp, vn), wv, bv)
        return (jax.nn.silu(cg) * cv,)

    gcol, vcol = (lambda j: j), (lambda j: j + _NJ)
    wspec = lambda cm: pl.BlockSpec((3, _TNC), lambda j, i: (0, cm(j)))
    bspec = lambda cm: pl.BlockSpec((1, _TNC), lambda j, i: (0, cm(j)))
    ins = [(a, s) for s in _halo_specs(t, gcol) + _halo_specs(t, vcol)]
    ins += [(cw, wspec(gcol)), (cw, wspec(vcol)), (cb, bspec(gcol)), (cb, bspec(vcol))]
    return _rows_call(name, fn, (_NJ, t // TM), ins,
                      [(_sds((t, FFN), BF16), pl.BlockSpec((TM, _TNC), lambda j, i: (i, j)), False)])[0]


def _conv_bwd_gate(name, a, cw, cb, dact):
    t = a.shape[0]

    def fn(ids, ag, agp, agn, av, avp, avn, wg, wv, bg, bv, dv):
        vp, vn = _seq_edges(ids[1], t)
        cg = _conv3(ag, *_shift_rows(ag, agp, agn, vp, vn), wg, bg)
        cv = _conv3(av, *_shift_rows(av, avp, avn, vp, vn), wv, bv)
        s = jax.nn.sigmoid(cg)
        d_gate = dv * cv * s * (1.0 + cg * (1.0 - s))
        d_val = dv * cg * s
        return (jnp.where(ids[0] >= _NJ, d_val, d_gate),)

    gcol, vcol = (lambda j: j % _NJ), (lambda j: j % _NJ + _NJ)
    wspec = lambda cm: pl.BlockSpec((3, _TNC), lambda j, i: (0, cm(j)))
    bspec = lambda cm: pl.BlockSpec((1, _TNC), lambda j, i: (0, cm(j)))
    ins = [(a, s) for s in _halo_specs(t, gcol) + _halo_specs(t, vcol)]
    ins += [(cw, wspec(gcol)), (cw, wspec(vcol)), (cb, bspec(gcol)), (cb, bspec(vcol)),
            (dact, pl.BlockSpec((TM, _TNC), lambda j, i: (i, j % _NJ)))]
    return _rows_call(name, fn, (2 * _NJ, t // TM), ins,
                      [(_sds((t, 2 * FFN), F32), pl.BlockSpec((TM, _TNC), lambda j, i: (i, j)), False)])[0]


def _conv_bwd_in(name, a, cw, dconv):
    t = a.shape[0]

    def fn(ids, x, xpb, xnb, dc, dcpb, dcnb, w):
        vp, vn = _seq_edges(ids[1], t)
        xp, xn = _shift_rows(x, xpb, xnb, vp, vn)
        dcp, dcn = _shift_rows(dc, dcpb, dcnb, vp, vn)
        da = dcn * w[0:1, :] + dc * w[1:2, :] + dcp * w[2:3, :]
        sums = [jnp.sum(dc * y, axis=0, keepdims=True) for y in (xp, x, xn)]
        rid = lax.broadcasted_iota(jnp.int32, (3, x.shape[1]), 0)
        dw = jnp.where(rid == 0, sums[0], jnp.where(rid == 1, sums[1], sums[2]))
        return da, dw, jnp.sum(dc, axis=0, keepdims=True)

    col = lambda j: j
    ins = [(a, s) for s in _halo_specs(t, col)] + [(dconv, s) for s in _halo_specs(t, col)]
    ins += [(cw, pl.BlockSpec((3, _TNC), lambda j, i: (0, j)))]
    return _rows_call(name, fn, (2 * _NJ, t // TM), ins,
                      [(_sds((t, 2 * FFN), BF16), pl.BlockSpec((TM, _TNC), lambda j, i: (i, j)), False),
                       (_sds((3, 2 * FFN), F32), pl.BlockSpec((3, _TNC), lambda j, i: (0, j)), True),
                       (_sds((1, 2 * FFN), F32), pl.BlockSpec((1, _TNC), lambda j, i: (0, j)), True)])


_TNA = 512


def _adaln_fwd(name, cond, w, b):
    fn = lambda ids, cv, wv, bv: ((bdot(jax.nn.silu(cv), wv[0], 1, 0) + bv[0])[None],)
    return _rows_call(name, fn, (2, ADA_LOC // _TNA),
                      [(cond, _full((16, D))), (w, pl.BlockSpec((1, D, _TNA), lambda l, j: (l, 0, j))),
                       (b, pl.BlockSpec((1, 1, _TNA), lambda l, j: (l, 0, j)))],
                      [(_sds((2, 16, ADA_LOC), F32), pl.BlockSpec((1, 16, _TNA), lambda l, j: (l, 0, j)), False)])[0]


def _adaln_bwd(name, c8, cc8, w, dl, dc):
    def fn(ids, cv, ccv, wv, dlv, dcv):
        dcs = jnp.broadcast_to(jnp.sum(dcv[0], axis=0, keepdims=True), dcv[0].shape)
        dw = hdot(jax.nn.silu(cv), dlv[0], 0, 0) + hdot(jax.nn.silu(ccv), dcs, 0, 0)
        s = jax.nn.sigmoid(ccv)
        rid = lax.broadcasted_iota(jnp.int32, ccv.shape, 0)
        dcc = jnp.where(rid == 0, bdot(dcs, wv[0], 1, 1) * s * (1.0 + ccv * (1.0 - s)), 0.0)
        return dw[None], dcc

    dspec = pl.BlockSpec((1, 8, _TNA), lambda l, j: (l, 0, j))
    return _rows_call(name, fn, (2, ADA_LOC // _TNA),
                      [(c8, _full((8, D))), (cc8, _full((8, D))), (w, pl.BlockSpec((1, D, _TNA), lambda l, j: (l, 0, j))),
                       (dl, dspec), (dc, dspec)],
                      [(_sds((2, D, ADA_LOC), F32), pl.BlockSpec((1, D, _TNA), lambda l, j: (l, 0, j)), False),
                       (_sds((8, D), F32), _full((8, D)), True)], acc_axes=(0, 1))


def _adamw_fn(w, g, m, v):
    m = ADAM_B1 * m + (1.0 - ADAM_B1) * g
    v = ADAM_B2 * v + (1.0 - ADAM_B2) * (g * g)
    m_hat = m / (1.0 - ADAM_B1 ** ADAM_STEP)
    v_hat = v / (1.0 - ADAM_B2 ** ADAM_STEP)
    return -ADAM_LR * (m_hat / (jnp.sqrt(v_hat) + ADAM_EPS) + ADAM_WD * w), m, v


def _adamw(name, w, g, m, v):
    r, c = w.shape
    tr = _tile(r, max(8, (1 << 20) // (4 * c)), 8)
    spec = pl.BlockSpec((tr, c), lambda i: (i, 0))
    o = (_sds((r, c), F32), spec, False)
    return _rows_call(name, lambda ids, *a: _adamw_fn(*a), (r // tr,), [(x, spec) for x in (w, g, m, v)], [o, o, o],
                      sem=("parallel",))


def _coords():
    return lax.axis_index("x"), lax.axis_index("y"), lax.axis_index("c")


def _other_chips(x, y):
    return [(1 - x, y), (x, 1 - y), (1 - x, 1 - y)]


def _allgather_small(name, blk):
    m_per, n = blk.shape

    def body(x_ref, out_ref, send_sems, recv_sems, local_sem):
        x, y, c = _coords()
        me, sibling = (x, y, c), (x, y, 1 - c)
        chips = _other_chips(x, y)

        def rows(px, py, pc):
            return out_ref.at[pl.ds((4 * px + 2 * py + pc) * m_per, m_per), :]

        def copy(k, block, to, src=None):
            return pltpu.make_async_remote_copy(
                src_ref=rows(*block) if src is None else src, dst_ref=rows(*block), send_sem=send_sems.at[k],
                recv_sem=recv_sems.at[k], device_id=to, device_id_type=MESH)

        mine = pltpu.make_async_copy(x_ref, rows(*me), local_sem)
        mine.start()
        first = [copy(0, me, sibling, src=x_ref)]
        first += [copy(1 + j, me, (*chip, c), src=x_ref) for j, chip in enumerate(chips)]
        for cp in first:
            cp.start()
        passed = [copy(4 + j, (*chip, c), sibling) for j, chip in enumerate(chips)]
        for j, chip in enumerate(chips):
            copy(1 + j, (*chip, c), me).wait_recv()
            passed[j].start()
        copy(0, sibling, me).wait_recv()
        for j, chip in enumerate(chips):
            copy(4 + j, (*chip, 1 - c), me).wait_recv()
        for cp in first + passed:
            cp.wait_send()
        mine.wait()

    return pl.pallas_call(
        body, name=name, out_shape=_sds((N_DEV * m_per, n), blk.dtype),
        in_specs=[pl.BlockSpec(memory_space=pltpu.VMEM)], out_specs=pl.BlockSpec(memory_space=pltpu.VMEM),
        scratch_shapes=[pltpu.SemaphoreType.DMA((7,)), pltpu.SemaphoreType.DMA((7,)), pltpu.SemaphoreType.DMA],
        compiler_params=pltpu.CompilerParams(vmem_limit_bytes=VMEM_LIMIT),
    )(blk)


_ANY = pl.BlockSpec(memory_space=pl.ANY)


def _remote(src, dst, send_sems, recv_sems, s, to):
    return pltpu.make_async_remote_copy(src_ref=src, dst_ref=dst, send_sem=send_sems.at[s], recv_sem=recv_sems.at[s],
                                        device_id=to, device_id_type=MESH)


def _comm_call(name, body, ins, out_shapes, n_sems, n_local):
    return pl.pallas_call(
        body, name=name, out_shape=out_shapes, in_specs=[_ANY] * len(ins), out_specs=[_ANY] * len(out_shapes),
        scratch_shapes=[pltpu.SemaphoreType.DMA((n_sems,)), pltpu.SemaphoreType.DMA((n_sems,)),
                        pltpu.SemaphoreType.DMA((n_local,))],
    )(*ins)


def _allgather_layers(name, locs):
    n = len(locs)

    def body(*refs):
        ins, outs, (send_sems, recv_sems, local_sems) = refs[:n], refs[n:2 * n], refs[2 * n:]
        x, y, c = _coords()
        k = 2 * x + y
        sibling = (x, y, 1 - c)
        chips = _other_chips(x, y)
        mine = [pltpu.make_async_copy(ins[t], outs[t].at[k], local_sems.at[t]) for t in range(n)]
        for cp in mine:
            cp.start()
        first = [_remote(ins[t].at[c], outs[t].at[k, c], send_sems, recv_sems, 6 * t + j, (*chip, c))
                 for t in range(n) for j, chip in enumerate(chips)]
        for cp in first:
            cp.start()
        passed = []
        for t in range(n):
            for j, (cx, cy) in enumerate(chips):
                there = outs[t].at[2 * cx + cy, c]
                _remote(there, there, send_sems, recv_sems, 6 * t + j, sibling).wait_recv()
                passed.append(_remote(there, there, send_sems, recv_sems, 6 * t + 3 + j, sibling))
                passed[-1].start()
        for t in range(n):
            for j, (cx, cy) in enumerate(chips):
                there = outs[t].at[2 * cx + cy, 1 - c]
                _remote(there, there, send_sems, recv_sems, 6 * t + 3 + j, sibling).wait_recv()
        for cp in first + passed:
            cp.wait_send()
        for cp in mine:
            cp.wait()

    return _comm_call(name, body, locs, [_sds((N_CHIP,) + a.shape, a.dtype) for a in locs], 6 * n, n)


def _rs_pair_exchange(name, g0, g1):
    n = len(g0)

    def body(*refs):
        a0, a1, outs, (send_sems, recv_sems, _) = refs[:n], refs[n:2 * n], refs[2 * n:3 * n], refs[3 * n:]
        x, y, c = _coords()

        def run(srcs):
            cps = [_remote(srcs[t], outs[t], send_sems, recv_sems, t, (x, y, 1 - c)) for t in range(n)]
            for cp in cps:
                cp.start()
            for cp in cps:
                cp.wait()

        pl.when(c == 0)(lambda: run(a1))
        pl.when(c == 1)(lambda: run(a0))

    return _comm_call(name, body, list(g0) + list(g1), [_sds(a.shape, a.dtype) for a in g0], n, 1)


def _ew2d(name, fn, ins, out_dtype):
    shape = ins[0].shape
    r, c = _prod(shape[:-1]), shape[-1]
    tr = _tile(r, max(8, (1 << 20) // (4 * c)), 8)
    spec = pl.BlockSpec((tr, c), lambda i: (i, 0))
    out = _rows_call(name, lambda ids, *a: (fn(*a),), (r // tr,), [(a.reshape(r, c), spec) for a in ins],
                     [(_sds((r, c), out_dtype), spec, False)], sem=("parallel",))[0]
    return out.reshape(shape)


def _rs_chip_exchange(name, s1):
    n = len(s1)

    def body(*refs):
        ins, outs, (send_sems, recv_sems, local_sems) = refs[:n], refs[n:2 * n], refs[2 * n:]
        x, y, c = _coords()
        k = 2 * x + y
        chips = _other_chips(x, y)
        mine = [pltpu.make_async_copy(ins[t].at[k], outs[t].at[k], local_sems.at[t]) for t in range(n)]
        for cp in mine:
            cp.start()
        cps = [_remote(ins[t].at[2 * cx + cy], outs[t].at[k], send_sems, recv_sems, 3 * t + j, (cx, cy, c))
               for t in range(n) for j, (cx, cy) in enumerate(chips)]
        for cp in cps:
            cp.start()
        for t in range(n):
            for j, (cx, cy) in enumerate(chips):
                there = outs[t].at[2 * cx + cy]
                _remote(there, there, send_sems, recv_sems, 3 * t + j, (cx, cy, c)).wait_recv()
        for cp in cps:
            cp.wait_send()
        for cp in mine:
            cp.wait()

    return _comm_call(name, body, s1, [_sds(a.shape, a.dtype) for a in s1], 3 * n, n)


def _sum_slots(name, a):
    s, r, cdim = a.shape
    tr = _tile(r, 512, 8)

    def fn(ids, av):
        tot = av[0]
        for i in range(1, s):
            tot = tot + av[i]
        return (tot,)

    return _rows_call(name, fn, (r // tr,), [(a, pl.BlockSpec((s, tr, cdim), lambda i: (0, i, 0)))],
                      [(_sds((r, cdim), F32), pl.BlockSpec((tr, cdim), lambda i: (i, 0)), False)], sem=("parallel",))[0]


def _pair_allgather(name, halves):
    n = len(halves)

    def body(*refs):
        ins, outs, (send_sems, recv_sems, local_sems) = refs[:n], refs[n:2 * n], refs[2 * n:]
        x, y, c = _coords()
        mine = [pltpu.make_async_copy(ins[t], outs[t].at[c], local_sems.at[t]) for t in range(n)]
        for cp in mine:
            cp.start()
        cps = [_remote(ins[t], outs[t].at[c], send_sems, recv_sems, t, (x, y, 1 - c)) for t in range(n)]
        for cp in cps:
            cp.start()
        for t in range(n):
            _remote(ins[t], outs[t].at[1 - c], send_sems, recv_sems, t, (x, y, 1 - c)).wait_recv()
        for cp in cps:
            cp.wait_send()
        for cp in mine:
            cp.wait()

    return _comm_call(name, body, halves, [_sds((2,) + a.shape, a.dtype) for a in halves], n, n)


def _reduce_scatter(g0, g1):
    n = len(g0)
    got = _rs_pair_exchange("rs_pair_exchange", g0, g1)
    keep = lambda a, b, r: jnp.where(lax.axis_index("c") == 0, a, b) + r
    s1 = [_ew2d("rs_pair_add_%d" % t, keep, [g0[t], g1[t], got[t]], BF16) for t in range(n)]
    slots = _rs_chip_exchange("rs_chip_exchange", s1)
    red = [_sum_slots("rs_chip_sum_%d" % t, a.reshape(N_CHIP, -1, a.shape[-1])).reshape(a.shape[1:])
           for t, a in enumerate(slots)]
    return _pair_allgather("rs_pair_allgather", red)


PACK_C = 1024
_SHARDED = (("w_in", 1), ("w_br_a", 1), ("w_br_b", 1), ("w_br_c", 1), ("w_out", 0), ("w_ffn_up", 1), ("w_ffn_down", 0))
_SHARDED_SMALL = (("conv_w", (3, 2 * FFN), 1), ("w_alpha2", (2, 16, GLA_QK), 2), ("b_alpha", (2, GLA_QK), 1))


def _prod(shape):
    n = 1
    for s in shape:
        n *= s
    return n


def _to_blocks(full, axis):
    shp = full.shape
    split = full.reshape(shp[:axis] + (N_CHIP, shp[axis] // N_CHIP) + shp[axis + 1:])
    return jnp.moveaxis(split, axis, 0)


def _from_blocks(blocks, axis):
    return jnp.concatenate([blocks[k] for k in range(N_CHIP)], axis=axis)


def _rope_tables(tx):
    pos = jnp.arange(tx, dtype=jnp.int32)
    inv_freq = 10000.0 ** (-jnp.arange(16, dtype=F32) / 16)
    ang_r = (pos // GRID_W).astype(F32)[:, None] * inv_freq
    ang_c = (pos % GRID_W).astype(F32)[:, None] * inv_freq
    ang = jnp.concatenate([ang_r, ang_r, ang_c, ang_c], axis=-1)
    sign = jnp.concatenate([-jnp.ones((16,), F32), jnp.ones((16,), F32)] * 2)
    cos = jnp.concatenate([jnp.ones((TC, HD), F32), jnp.cos(ang)], axis=0)
    sin = jnp.concatenate([jnp.zeros((TC, HD), F32), jnp.sin(ang) * sign], axis=0)
    return jnp.tile(cos, (1, 2)), jnp.tile(sin, (1, 2))


def _lane_consts():
    l = jnp.arange(512)
    seg = (l[:, None] // HD == l[None, :] // HD).astype(F32) / HD
    partner = jnp.where(l % 32 < 16, l + 16, l - 16)
    perm = (l[:, None] == partner[None, :]).astype(F32)
    return seg, perm


def _heads(a, n):
    return a.reshape(a.shape[0], n, HD).transpose(1, 0, 2)


def _unheads(a):
    return a.transpose(1, 0, 2).reshape(a.shape[1], a.shape[0] * HD)


def _gather_f32_shards(shards):
    sizes = [_prod(a.shape) for a in shards]
    flat = jnp.concatenate([a.reshape(-1) for a in shards] + [jnp.zeros((16 * PACK_C - sum(sizes),), F32)])
    got = _allgather_small("gather_f32_shards", flat.reshape(16, PACK_C)).reshape(N_CHIP, 2, 16 * PACK_C)[:, 0]
    out, o = {}, 0
    for (n, _, ax), a, sz in zip(_SHARDED_SMALL, shards, sizes):
        out[n] = jnp.concatenate([got[k, o:o + sz].reshape(a.shape) for k in range(N_CHIP)], axis=ax + 1)
        o += sz
    return out


def _layer_params(l, wfull, small):
    w2 = small["w_alpha2_full"][l]
    w2pad = jnp.zeros((128, 512), F32).at[0:16, 0:256].set(w2[0]).at[16:32, 256:512].set(w2[1])
    full = {n: _from_blocks(wfull[n][:, l], ax) for n, ax in _SHARDED}
    return dict(
        w_in=_to_new_cols(full["w_in"]), wa=full["w_br_a"], wb=full["w_br_b"], wc=full["w_br_c"],
        w_out=full["w_out"], w_up=full["w_ffn_up"], w_down=full["w_ffn_down"],
        cw=small["conv_w_full"][l], cb=small["conv_b"][l][None], w2=w2pad,
        b2=small["b_alpha_full"][l].reshape(1, 512),
        g1=small["norm1_g"][l][None], g2=small["norm2_g"][l][None], gq=jnp.tile(small["q_norm_g"][l], 8)[None],
        gk=jnp.tile(small["k_norm_g"][l], 2)[None], ggm=small["gmlp_norm_g"][l][None], ws=small["w_spatial"][l],
        bst=small["b_spatial"][l].T, ggl=small["gla_norm_g"][l][None])


def _layer_fwd(l, last, x, h1, mod, P, tabs):
    cos, sin, seg, perm = tabs
    n = "l%d_" % l
    s = dict(x=x, h1=h1)
    p = _mm(n + "in_proj", h1, P["w_in"], "nn", F32, tm_t=768, tn_t=2176)
    s["p"] = p
    s["gm"] = _gmlp_fwd(n + "gmlp", p, P["ggm"], P["ws"], P["bst"])
    qr, kr, vb = _qk_fwd(n + "qk_prep", p, P["gq"], P["gk"], cos, sin, seg, perm)
    qh, kh, vh = _heads(qr, NQ), _heads(kr, NKV), _heads(vb, NKV)
    s["qh"], s["kh"], s["vh"] = qh, kh, vh
    ox, lse_x = _attn_fwd(n + "attn_x", qh[:, TC:], kh, vh)
    s["ox"], s["lse_x"] = ox, lse_x
    if last:
        oc = jnp.zeros((NQ, TC, HD), F32)
    else:
        oc, lse_c = _attn_fwd(n + "attn_c", qh[:, :TC], kh[:, :TC], vh[:, :TC])
        s["oc"], s["lse_c"] = oc, lse_c
    s["att"] = _unheads(jnp.concatenate([oc, ox], axis=1)).astype(BF16)
    la = _decay_fwd(n + "gla_decay", p, P["w2"], P["b2"])
    s["la"] = la
    s["of"], s["sf"] = _gla_fwd(n + "gla_scan_f", p, la, False)
    s["ob"], s["sb"] = _gla_fwd(n + "gla_scan_b", p, la, True)
    s["gla"] = _gla_out_fwd(n + "gla_out", s["of"], s["ob"], p, P["ggl"])
    s["merged"] = _merge_fwd(n + "merge", s["gm"], s["att"], s["gla"], P["wa"], P["wb"], P["wc"], p)
    s["mix"] = _mm(n + "out_proj", s["merged"], P["w_out"], "nn", F32)
    s["x_mid"], s["h2"] = _res_nm_fwd(n + "res1_norm2", x, s["mix"], mod, 2, mod, P["g2"], 3, 4)
    s["a"] = _mm(n + "ffn_up", s["h2"], P["w_up"], "nn", F32)
    s["act"] = _conv_fwd(n + "conv_gate", s["a"], P["cw"], P["cb"])
    s["f"] = _mm(n + "ffn_down", s["act"], P["w_down"], "nn", F32)
    return s


def _layer_bwd(l, last, s, mod, P, tabs, dx_mid, df, gw):
    cos, sin, seg, perm = tabs
    n = "l%d_b_" % l
    t = dx_mid.shape[0]
    p = s["p"]
    gw["w_ffn_down"] = _mm(n + "ffn_down_w", s["act"], df, "tn", F32)
    dact = _mm(n + "ffn_down_x", df, P["w_down"], "nt", F32)
    dconv = _conv_bwd_gate(n + "conv_gate", s["a"], P["cw"], P["cb"], dact)
    da, gw["conv_w"], dcb = _conv_bwd_in(n + "conv_in", s["a"], P["cw"], dconv)
    gw["conv_b"] = dcb[0]
    gw["w_ffn_up"] = _mm(n + "ffn_up_w", s["h2"], da, "tn", F32)
    dh2 = _mm(n + "ffn_up_x", da, P["w_up"], "nt", F32)
    dx, dmix, dmod_a, dmod_b, dg2 = _res_nm_bwd(n + "res1_norm2", s["x"], s["mix"], mod, 2, mod, P["g2"], 3, 4, dx_mid, dh2)
    dmod = dmod_a + dmod_b
    gw["norm2_g"] = dg2[0]
    gw["w_out"] = _mm(n + "out_proj_w", s["merged"], dmix, "tn", F32)
    dmerged = _mm(n + "out_proj_x", dmix, P["w_out"], "nt", F32)
    dya, dyb, dyc, dga, dgb, dgc = _merge_bwd(n + "merge", s["gm"], s["att"], s["gla"], P["wa"], P["wb"], P["wc"], p, dmerged)
    gw["w_br_a"] = _mm(n + "br_a_w", s["gm"], dya, "tn", F32)
    gw["w_br_b"] = _mm(n + "br_b_w", s["att"], dyb, "tn", F32)
    gw["w_br_c"] = _mm(n + "br_c_w", s["gla"], dyc, "tn", F32)
    dgm = _mm(n + "br_a_x", dya, P["wa"], "nt", F32)
    datt = _mm(n + "br_b_x", dyb, P["wb"], "nt", F32)
    dgla = _mm(n + "br_c_x", dyc, P["wc"], "nt", F32)
    du, dv_g, dggm, dws, dbst = _gmlp_bwd(n + "gmlp", p, P["ggm"], P["ws"], P["bst"], dgm)
    gw["gmlp_norm_g"], gw["w_spatial"], gw["b_spatial"] = dggm[0], dws, dbst.T
    doh = _heads(datt, NQ)
    qh, kh, vh = s["qh"], s["kh"], s["vh"]
    dqx, delta_x = _attn_bwd_q(n + "attn_x_q", qh[:, TC:], kh, vh, s["ox"], doh[:, TC:], s["lse_x"])
    row = lambda a: a.reshape(a.shape[0], 1, a.shape[1])
    dkh, dvh = _attn_bwd_kv(n + "attn_x_kv", qh[:, TC:], kh, vh, doh[:, TC:], row(s["lse_x"]), row(delta_x))
    if last:
        dqc = jnp.zeros((NQ, TC, HD), F32)
    else:
        dqc, delta_c = _attn_bwd_q(n + "attn_c_q", qh[:, :TC], kh[:, :TC], vh[:, :TC], s["oc"], doh[:, :TC], s["lse_c"])
        dkc, dvc = _attn_bwd_kv(n + "attn_c_kv", qh[:, :TC], kh[:, :TC], vh[:, :TC], doh[:, :TC], row(s["lse_c"]), row(delta_c))
        pad = jnp.zeros((NKV, t - TC, HD), F32)
        dkh = dkh + jnp.concatenate([dkc, pad], axis=1)
        dvh = dvh + jnp.concatenate([dvc, pad], axis=1)
    dqr = _unheads(jnp.concatenate([dqc, dqx], axis=1))
    dq, dk, dgq, dgk = _qk_bwd(n + "qk_prep", p, P["gq"], P["gk"], cos, sin, seg, perm, dqr, _unheads(dkh))
    gw["q_norm_g"], gw["k_norm_g"] = dgq.reshape(8, HD).sum(0), dgk.reshape(2, HD).sum(0)
    dv_att = _unheads(dvh).astype(BF16)
    do, dr, dggl = _gla_out_bwd(n + "gla_out", s["of"], s["ob"], p, P["ggl"], dgla)
    gw["gla_norm_g"] = dggl[0]
    dq_f, dk_f, dv_f, dla_f = _gla_bwd(n + "gla_scan_f", p, s["la"], s["sf"], do, False)
    dglq, dglk, dglv, dla_b = _gla_bwd(n + "gla_scan_b", p, s["la"], s["sb"], do, True, prev=(dq_f, dk_f, dv_f))
    dab, dw2, db2 = _decay_bwd(n + "gla_decay", p, P["w2"], P["b2"], dla_f, dla_b)
    gw["w_alpha2"] = jnp.stack([dw2[0:16, 0:256], dw2[16:32, 256:512]])
    gw["b_alpha"] = db2.reshape(2, 256)
    bf = lambda a: a.astype(BF16)
    dp = jnp.concatenate([dga, dgb, dgc, du, dv_g, dq, bf(dglv), dr, bf(dglq), bf(dglk), dk, dv_att, dab], axis=-1)
    gw["w_in"] = _to_ref_cols(_mm(n + "in_proj_w", s["h1"], dp, "tn", F32, tn_t=2176, tk_t=768))
    dh1 = _mm(n + "in_proj_x", dp, P["w_in"], "nt", F32, tk_t=2176)
    return dx, dh1, dmod


_SMALL = (("norm1_g", (2, D)), ("norm2_g", (2, D)), ("q_norm_g", (2, HD)), ("k_norm_g", (2, HD)), ("gmlp_norm_g", (2, GW)),
          ("gla_norm_g", (2, GLA_V)), ("w_spatial", (2, 4, 128, 128)), ("b_spatial", (2, 4, 128)), ("conv_b", (2, 2 * FFN)),
          ("final_norm_g", (D,))) + tuple((n, (2,) + s) for n, s, _ in _SHARDED_SMALL)
_SMALL_N = 2 * 2 * ADA_W + sum(_prod(s) for _, s in _SMALL)
_SMALL_R = -(-_SMALL_N // (PACK_C * 8)) * 8


def _mod_tables(c, c_ctx, w_ada, b_ada, k):
    x, y, cc = _coords()
    me = 4 * x + 2 * y + cc
    c_all = _allgather_small("gather_c", jnp.concatenate([c, jnp.zeros((7, D), F32)], axis=0))
    c8 = c_all.reshape(N_DEV, 8, D)[:, 0]
    cond = jnp.concatenate([c8, c_ctx[None], jnp.zeros((7, D), F32)], axis=0)
    b_loc = lax.dynamic_slice_in_dim(b_ada, k * ADA_LOC, ADA_LOC, axis=1)[:, None, :]
    m_loc = _adaln_fwd("adaln", cond, w_ada, b_loc)
    m_all = _allgather_small("gather_mod", m_loc.reshape(32, ADA_LOC)).reshape(N_CHIP, 2, 2, 16, ADA_LOC)[:, 0]
    m_all = m_all.transpose(1, 2, 0, 3).reshape(2, 16, ADA_W)
    rows = jnp.stack([m_all[:, 8], lax.dynamic_index_in_dim(m_all, me, axis=1, keepdims=False)], axis=1)
    return rows.reshape(2, 2, 6, D), c8


def _step(x, c, ctx, c_ctx, W, tgt):
    xc, yc, cc = _coords()
    k = 2 * xc + yc
    tx = x.shape[0]
    t = TC + tx
    small = {n: W[n] for n, _ in _SMALL}
    for n, a in _gather_f32_shards([W[n] for n, _, _ in _SHARDED_SMALL]).items():
        small[n + "_full"] = a

    gathered = _allgather_layers("gather_weights", [W[n].astype(BF16) for n, _ in _SHARDED])
    wfull = {n: a for (n, _), a in zip(_SHARDED, gathered)}
    mods, c8 = _mod_tables(c, c_ctx, W["w_ada"], W["b_ada"], k)
    tabs = _rope_tables(tx) + _lane_consts()
    params = [_layer_params(l, wfull, small) for l in range(2)]

    xs = jnp.concatenate([ctx, x], axis=0)
    h1 = _nm_fwd("l0_norm1", xs, mods[0], params[0]["g1"], 0, 1)
    s0 = _layer_fwd(0, False, xs, h1, mods[0], params[0], tabs)
    x1, h1b = _res_nm_fwd("l0_res2_norm1", s0["x_mid"], s0["f"], mods[0], 5, mods[1], params[1]["g1"], 0, 1)
    s1 = _layer_fwd(1, True, x1, h1b, mods[1], params[1], tabs)
    loss, dxm_l, df_l, dmod_head, dgf = _head("head", s1["x_mid"], s1["f"], mods[1], W["final_norm_g"][None], tgt)

    gws = [dict(), dict()]
    zc = lambda dt: jnp.zeros((TC, D), dt)
    dx1, dh1b, dmod1 = _layer_bwd(1, True, s1, mods[1], params[1], tabs, jnp.concatenate([zc(F32), dxm_l]),
                                  jnp.concatenate([zc(BF16), df_l]), gws[1])
    dxm0, df0, dmod0_g, dmod1_s, dg1b = _res_nm_bwd("l0_b_res2_norm1", s0["x_mid"], s0["f"], mods[0], 5, mods[1],
                                                    params[1]["g1"], 0, 1, dx1, dh1b)
    gws[1]["norm1_g"] = dg1b[0]
    dx0, dh1, dmod0 = _layer_bwd(0, False, s0, mods[0], params[0], tabs, dxm0, df0, gws[0])
    dxs, dmod0_s, dg1 = _nm_bwd("l0_b_norm1", xs, mods[0], params[0]["g1"], 0, 1, dx0, dh1)
    gws[0]["norm1_g"] = dg1[0]
    grad_x = dxs[TC:]
    dmods = jnp.stack([dmod0 + dmod0_g + dmod0_s, dmod1 + dmod1_s + dmod_head])

    stk = {n: jnp.stack([gws[0][n], gws[1][n]]) for n, _ in _SMALL if n != "final_norm_g"}
    stk["final_norm_g"] = dgf[0]
    flat = jnp.concatenate([dmods.reshape(-1)] + [stk[n].reshape(-1) for n, _ in _SMALL])
    flat = jnp.concatenate([flat, jnp.zeros((_SMALL_R * PACK_C - _SMALL_N,), F32)]).reshape(_SMALL_R, PACK_C)
    every = _allgather_small("gather_small_grads", flat).reshape(N_DEV, _SMALL_R, PACK_C)
    tot = _sum_slots("sum_small_grads", every).reshape(-1)
    grads, o = {}, 2 * 2 * ADA_W
    for n, shp in _SMALL:
        grads[n] = tot[o:o + _prod(shp)].reshape(shp)
        o += _prod(shp)
    grads["b_ada"] = tot[:2 * 2 * ADA_W].reshape(2, 2, ADA_W).sum(axis=1)

    dm_every = every[:, :2 * 2 * ADA_W // PACK_C].reshape(N_DEV, 2, 2, ADA_W)
    dm_loc = lax.dynamic_slice_in_dim(dm_every, k * ADA_LOC, ADA_LOC, axis=3).transpose(1, 2, 0, 3)
    cc8 = jnp.concatenate([c_ctx[None], jnp.zeros((7, D), F32)], axis=0)
    grads["w_ada"], dcc = _adaln_bwd("adaln_b", c8, cc8, W["w_ada"], dm_loc[:, 1], dm_loc[:, 0])
    dcc_every = _allgather_small("gather_dcctx", dcc * 0.5).reshape(N_DEV, 8, D)
    grads["c_ctx"] = _sum_slots("sum_dcctx", dcc_every)[0]

    for n, shp, ax in _SHARDED_SMALL:
        grads[n] = lax.dynamic_slice_in_dim(grads[n], k * (shp[ax] // N_CHIP), shp[ax] // N_CHIP, axis=ax + 1)
    red = _reduce_scatter(*[[_to_blocks(gws[l][n], ax) for n, ax in _SHARDED] for l in range(2)])
    grads.update({n: a for (n, _), a in zip(_SHARDED, red)})
    return loss[0, 0], grad_x, grads


_WEIGHTS = ("c_ctx", "w_ada", "b_ada", "norm1_g", "norm2_g", "w_in", "q_norm_g", "k_norm_g", "gmlp_norm_g", "w_spatial",
            "b_spatial", "w_alpha2", "b_alpha", "gla_norm_g", "w_br_a", "w_br_b", "w_br_c", "w_out", "w_ffn_up", "conv_w",
            "conv_b", "w_ffn_down", "final_norm_g")
_BIG = ("w_ada", "w_in", "w_br_a", "w_br_b", "w_br_c", "w_out", "w_ffn_up", "w_ffn_down")


def _update(W, G, M, V):
    delta, new_m, new_v = {}, {}, {}
    for n in _BIG:
        shp = W[n].shape
        two = lambda a: a.reshape(-1, shp[-1])
        d, m, v = _adamw("adamw_" + n, two(W[n]), two(G[n]), two(M[n]), two(V[n]))
        delta[n], new_m[n], new_v[n] = d.reshape(shp), m.reshape(shp), v.reshape(shp)
    rest = [n for n in _WEIGHTS if n not in _BIG]
    tot = sum(_prod(W[n].shape) for n in rest)
    rows = -(-tot // (PACK_C * 8)) * 8

    def cat(dct):
        flat = jnp.concatenate([dct[n].reshape(-1) for n in rest] + [jnp.zeros((rows * PACK_C - tot,), F32)])
        return flat.reshape(rows, PACK_C)

    outs = _adamw("adamw_small", cat(W), cat(G), cat(M), cat(V))
    o = 0
    for n in rest:
        sz, shp = _prod(W[n].shape), W[n].shape
        delta[n], new_m[n], new_v[n] = (a.reshape(-1)[o:o + sz].reshape(shp) for a in outs)
        o += sz
    return delta, new_m, new_v


def kernel(x, c, ctx, c_ctx, w_ada, b_ada, norm1_g, norm2_g, w_in, q_norm_g, k_norm_g, gmlp_norm_g, w_spatial, b_spatial, w_alpha2, b_alpha, gla_norm_g, w_br_a, w_br_b, w_br_c, w_out, w_ffn_up, conv_w, conv_b, w_ffn_down, final_norm_g, loss_target, m_c_ctx, m_w_ada, m_b_ada, m_norm1_g, m_norm2_g, m_w_in, m_q_norm_g, m_k_norm_g, m_gmlp_norm_g, m_w_spatial, m_b_spatial, m_w_alpha2, m_b_alpha, m_gla_norm_g, m_w_br_a, m_w_br_b, m_w_br_c, m_w_out, m_w_ffn_up, m_conv_w, m_conv_b, m_w_ffn_down, m_final_norm_g, v_c_ctx, v_w_ada, v_b_ada, v_norm1_g, v_norm2_g, v_w_in, v_q_norm_g, v_k_norm_g, v_gmlp_norm_g, v_w_spatial, v_b_spatial, v_w_alpha2, v_b_alpha, v_gla_norm_g, v_w_br_a, v_w_br_b, v_w_br_c, v_w_out, v_w_ffn_up, v_conv_w, v_conv_b, v_w_ffn_down, v_final_norm_g):
    W = dict(c_ctx=c_ctx, w_ada=w_ada, b_ada=b_ada, norm1_g=norm1_g, norm2_g=norm2_g, w_in=w_in, q_norm_g=q_norm_g,
             k_norm_g=k_norm_g, gmlp_norm_g=gmlp_norm_g, w_spatial=w_spatial, b_spatial=b_spatial, w_alpha2=w_alpha2,
             b_alpha=b_alpha, gla_norm_g=gla_norm_g, w_br_a=w_br_a, w_br_b=w_br_b, w_br_c=w_br_c, w_out=w_out,
             w_ffn_up=w_ffn_up, conv_w=conv_w, conv_b=conv_b, w_ffn_down=w_ffn_down, final_norm_g=final_norm_g)
    M = dict(c_ctx=m_c_ctx, w_ada=m_w_ada, b_ada=m_b_ada, norm1_g=m_norm1_g, norm2_g=m_norm2_g, w_in=m_w_in,
             q_norm_g=m_q_norm_g, k_norm_g=m_k_norm_g, gmlp_norm_g=m_gmlp_norm_g, w_spatial=m_w_spatial,
             b_spatial=m_b_spatial, w_alpha2=m_w_alpha2, b_alpha=m_b_alpha, gla_norm_g=m_gla_norm_g, w_br_a=m_w_br_a,
             w_br_b=m_w_br_b, w_br_c=m_w_br_c, w_out=m_w_out, w_ffn_up=m_w_ffn_up, conv_w=m_conv_w, conv_b=m_conv_b,
             w_ffn_down=m_w_ffn_down, final_norm_g=m_final_norm_g)
    V = dict(c_ctx=v_c_ctx, w_ada=v_w_ada, b_ada=v_b_ada, norm1_g=v_norm1_g, norm2_g=v_norm2_g, w_in=v_w_in,
             q_norm_g=v_q_norm_g, k_norm_g=v_k_norm_g, gmlp_norm_g=v_gmlp_norm_g, w_spatial=v_w_spatial,
             b_spatial=v_b_spatial, w_alpha2=v_w_alpha2, b_alpha=v_b_alpha, gla_norm_g=v_gla_norm_g, w_br_a=v_w_br_a,
             w_br_b=v_w_br_b, w_br_c=v_w_br_c, w_out=v_w_out, w_ffn_up=v_w_ffn_up, conv_w=v_conv_w, conv_b=v_conv_b,
             w_ffn_down=v_w_ffn_down, final_norm_g=v_final_norm_g)
    loss_local, grad_x, G = _step(x[0], c, ctx[0], c_ctx, W, loss_target[0])
    loss = lax.psum(loss_local, ("x", "y", "c"))
    delta, new_m, new_v = _update(W, G, M, V)
    return (loss, grad_x[None], *[G[n] for n in _WEIGHTS], *[delta[n] for n in _WEIGHTS],
            *[new_m[n] for n in _WEIGHTS], *[new_v[n] for n in _WEIGHTS])
```

```python
import functools

import jax
import jax.numpy as jnp
from jax import lax
from jax.experimental import pallas as pl
from jax.experimental.pallas import tpu as pltpu

F32 = jnp.float32
BF16 = jnp.bfloat16

D = 1024
TC = 256
GRID_W = 64
EPS = 1e-6
HD = 64
NQ = 8
NKV = 2
QG = NQ // NKV
GLA_H = 4
GLA_DK = 64
GLA_DV = 128
GLA_QK = 256
GLA_V = 512
GLA_CHUNK = 64
GLA_TAU = 16.0
GW = 512
FFN = 2816
IN_W = 6432
PW = 6528
ADA_W = 6 * D
N_CHIP = 4
N_DEV = 8
ADA_LOC = ADA_W // N_CHIP

ADAM_LR = 0.001
ADAM_B1 = 0.9
ADAM_B2 = 0.999
ADAM_EPS = 1e-08
ADAM_WD = 0.01
ADAM_STEP = 10

TM = 256
NCB = TC // TM
LANE = 128
VMEM_LIMIT = 48 * 1024 * 1024
MESH = pl.DeviceIdType.MESH

_COLS = (("gA", 3360, 1024), ("gB", 4384, 1024), ("gC", 5408, 1024), ("gu", 0, 512), ("gv", 512, 512),
         ("q", 1024, 512), ("glv", 2304, 512), ("gr", 2848, 512), ("glq", 1792, 256), ("glk", 2048, 256),
         ("k", 1536, 128), ("v", 1664, 128), ("ab", 2816, 32))
OFF = {}
_o = 0
for _n, _s, _w in _COLS:
    OFF[_n] = _o
    _o += max(_w, LANE)
assert _o == PW


def _to_new_cols(w):
    parts = [w[..., s:s + n] for _, s, n in _COLS]
    pad = jnp.zeros(w.shape[:-1] + (PW - IN_W,), w.dtype)
    return jnp.concatenate(parts + [pad], axis=-1)


def _to_ref_cols(w):
    by_start = sorted(_COLS, key=lambda t: t[1])
    return jnp.concatenate([w[..., OFF[n]:OFF[n] + wd] for n, _, wd in by_start], axis=-1)


def _tile(n, target, align=LANE):
    best = None
    t = align
    while t <= min(n, target):
        if n % t == 0:
            best = t
        t += align
    assert best is not None, (n, target, align)
    return best


def _cp(sem=None):
    return pltpu.CompilerParams(dimension_semantics=sem, vmem_limit_bytes=VMEM_LIMIT)


def _bdot_impl(a, b, ca, cb):
    return lax.dot_general(a.astype(BF16), b.astype(BF16), (((ca,), (cb,)), ((), ())),
                           preferred_element_type=F32)


@functools.partial(jax.custom_vjp, nondiff_argnums=(2, 3))
def bdot(a, b, ca, cb):
    return _bdot_impl(a, b, ca, cb)


def _bdot_fwd(a, b, ca, cb):
    return _bdot_impl(a, b, ca, cb), (a, b)


def _bdot_bwd(ca, cb, res, g):
    a, b = res
    da = bdot(g, b, 1, 1 - cb) if ca == 1 else bdot(b, g, 1 - cb, 1)
    db = bdot(a, g, 1 - ca, 0) if cb == 0 else bdot(g, a, 0, 1 - ca)
    return da.astype(a.dtype), db.astype(b.dtype)


bdot.defvjp(_bdot_fwd, _bdot_bwd)


def hdot(a, b, ca=1, cb=0):
    return lax.dot_general(a, b, (((ca,), (cb,)), ((), ())), precision=lax.Precision.HIGHEST,
                           preferred_element_type=F32)


def _rms(x, g):
    return x * lax.rsqrt(jnp.mean(x * x, axis=-1, keepdims=True) + EPS) * g


def _gelu(x):
    return 0.5 * x * (1.0 + jnp.tanh(0.7978845608028654 * (x + 0.044715 * (x * x * x))))


def _log_sigmoid(z):
    return jnp.minimum(z, 0.0) - jnp.log(1.0 + jnp.exp(-jnp.abs(z)))


def _sel(mod, is_lat, idx):
    return jnp.where(is_lat, mod[1, idx:idx + 1, :], mod[0, idx:idx + 1, :])


def _rows_call(name, fn, grid, ins, outs, acc_axes=None, sem=None):
    n_in = len(ins)
    flags = [o[2] for o in outs]
    if acc_axes is None:
        acc_axes = (len(grid) - 1,)

    def body(*refs):
        ids = tuple(pl.program_id(a) for a in range(len(grid)))
        res = fn(ids, *[r[...] for r in refs[:n_in]])
        for r, v, acc in zip(refs[n_in:], res, flags):
            if acc:
                first = functools.reduce(jnp.logical_and, [ids[a] == 0 for a in acc_axes])

                @pl.when(first)
                def _():
                    r[...] = jnp.zeros_like(r)
                r[...] += v.astype(r.dtype)
            else:
                r[...] = v.astype(r.dtype)

    return pl.pallas_call(
        body, name=name, grid=grid, in_specs=[s for _, s in ins], out_specs=[o[1] for o in outs],
        out_shape=[o[0] for o in outs],
        compiler_params=_cp(sem if sem is not None else ("arbitrary",) * len(grid)),
    )(*[a for a, _ in ins])


def _sds(shape, dtype):
    return jax.ShapeDtypeStruct(shape, dtype)


def _rowspec(width, off=0, tm=TM):
    assert off % width == 0
    return pl.BlockSpec((tm, width), lambda i, o=off // width: (i, o))


def _full(shape):
    nd = len(shape)
    return pl.BlockSpec(shape, lambda *a: (0,) * nd)


def _mm(name, a, b, mode, out_dtype, tm_t=1056, tn_t=1408, tk_t=1408):
    if mode == "nn":
        (m, k), (_, n) = a.shape, b.shape
    elif mode == "nt":
        (m, k), (n, _) = a.shape, b.shape
    else:
        (k, m), (_, n) = a.shape, b.shape
    tm = _tile(m, tm_t, 8 if m % LANE else LANE)
    tn = _tile(n, tn_t)
    tk = _tile(k, tk_t)
    nk = k // tk
    if mode == "nn":
        dims, a_spec, b_spec = ((1,), (0,)), pl.BlockSpec((tm, tk), lambda i, j, l: (i, l)), pl.BlockSpec((tk, tn), lambda i, j, l: (l, j))
    elif mode == "nt":
        dims, a_spec, b_spec = ((1,), (1,)), pl.BlockSpec((tm, tk), lambda i, j, l: (i, l)), pl.BlockSpec((tn, tk), lambda i, j, l: (j, l))
    else:
        dims, a_spec, b_spec = ((0,), (0,)), pl.BlockSpec((tk, tm), lambda i, j, l: (l, i)), pl.BlockSpec((tk, tn), lambda i, j, l: (l, j))

    def body(a_ref, b_ref, o_ref, *scratch):
        l = pl.program_id(2)
        part = lax.dot_general(a_ref[...].astype(BF16), b_ref[...].astype(BF16), (dims, ((), ())),
                               preferred_element_type=F32)
        if nk == 1:
            o_ref[...] = part.astype(o_ref.dtype)
            return
        acc_ref = scratch[0]

        @pl.when(l == 0)
        def _():
            acc_ref[...] = part

        @pl.when(l > 0)
        def _():
            acc_ref[...] += part

        @pl.when(l == nk - 1)
        def _():
            o_ref[...] = acc_ref[...].astype(o_ref.dtype)

    return pl.pallas_call(
        body, name=name, grid=(m // tm, n // tn, nk), in_specs=[a_spec, b_spec],
        out_specs=pl.BlockSpec((tm, tn), lambda i, j, l: (i, j)), out_shape=_sds((m, n), out_dtype),
        scratch_shapes=[pltpu.VMEM((tm, tn), F32)] if nk > 1 else [],
        compiler_params=_cp(("parallel", "parallel", "arbitrary")),
    )(a, b)


def _nm_fn(is_lat, x, mod, g, shift, scale):
    return _rms(x, g) * (1.0 + _sel(mod, is_lat, scale)) + _sel(mod, is_lat, shift)


def _res_nm_fn(is_lat, x, br, modg, gate, mods, g, shift, scale):
    xn = x + _sel(modg, is_lat, gate) * br
    return xn, _nm_fn(is_lat, xn, mods, g, shift, scale)


def _nm_fwd(name, x, mod, g, shift, scale):
    t = x.shape[0]
    fn = lambda ids, xv, mv, gv: (_nm_fn(ids[0] >= NCB, xv, mv, gv, shift, scale),)
    return _rows_call(name, fn, (t // TM,), [(x, _rowspec(D)), (mod, _full((2, 6, D))), (g, _full((1, D)))],
                      [(_sds((t, D), BF16), _rowspec(D), False)])[0]


def _nm_bwd(name, x, mod, g, shift, scale, dx_res, dh):
    t = x.shape[0]

    def fn(ids, xv, mv, gv, dxr, dhv):
        _, vjp = jax.vjp(lambda a, b, c: _nm_fn(ids[0] >= NCB, a, b, c, shift, scale), xv, mv, gv)
        dx, dm, dg = vjp(dhv)
        return dx + dxr, dm, dg

    return _rows_call(name, fn, (t // TM,),
                      [(x, _rowspec(D)), (mod, _full((2, 6, D))), (g, _full((1, D))), (dx_res, _rowspec(D)), (dh, _rowspec(D))],
                      [(_sds((t, D), F32), _rowspec(D), False), (_sds((2, 6, D), F32), _full((2, 6, D)), True),
                       (_sds((1, D), F32), _full((1, D)), True)])


def _res_nm_fwd(name, x, br, modg, gate, mods, g, shift, scale):
    t = x.shape[0]
    fn = lambda ids, xv, bv, mg, ms, gv: _res_nm_fn(ids[0] >= NCB, xv, bv, mg, gate, ms, gv, shift, scale)
    return _rows_call(name, fn, (t // TM,),
                      [(x, _rowspec(D)), (br, _rowspec(D)), (modg, _full((2, 6, D))), (mods, _full((2, 6, D))), (g, _full((1, D)))],
                      [(_sds((t, D), F32), _rowspec(D), False), (_sds((t, D), BF16), _rowspec(D), False)])


def _res_nm_bwd(name, x, br, modg, gate, mods, g, shift, scale, dx_res, dh):
    t = x.shape[0]

    def fn(ids, xv, bv, mg, ms, gv, dxr, dhv):
        f = lambda a, b, c, d, e: _res_nm_fn(ids[0] >= NCB, a, b, c, gate, d, e, shift, scale)
        _, vjp = jax.vjp(f, xv, bv, mg, ms, gv)
        return vjp((dxr, dhv))

    m26 = (_sds((2, 6, D), F32), _full((2, 6, D)), True)
    return _rows_call(name, fn, (t // TM,),
                      [(x, _rowspec(D)), (br, _rowspec(D)), (modg, _full((2, 6, D))), (mods, _full((2, 6, D))), (g, _full((1, D))),
                       (dx_res, _rowspec(D)), (dh, _rowspec(D))],
                      [(_sds((t, D), F32), _rowspec(D), False), (_sds((t, D), BF16), _rowspec(D), False), m26, m26,
                       (_sds((1, D), F32), _full((1, D)), True)])


def _head(name, x_mid, f, mod, gf, tgt):
    tx = tgt.shape[0]

    def fn(ids, xv, fv, mv, gv, tv):
        def loss_fn(a, b, c, d):
            y = _rms(a + c[1, 5:6, :] * b, d)
            e = y - tv
            return 0.5 * jnp.sum(jnp.mean(e * e, axis=-1))
        loss, grads = jax.value_and_grad(loss_fn, argnums=(0, 1, 2, 3))(xv, fv, mv, gv)
        return (jnp.reshape(loss, (1, 1)),) + grads

    lat = pl.BlockSpec((TM, D), lambda i: (i + NCB, 0))
    return _rows_call(name, fn, (tx // TM,),
                      [(x_mid, lat), (f, lat), (mod, _full((2, 6, D))), (gf, _full((1, D))), (tgt, _rowspec(D))],
                      [(_sds((1, 1), F32), _full((1, 1)), True), (_sds((tx, D), F32), _rowspec(D), False),
                       (_sds((tx, D), BF16), _rowspec(D), False), (_sds((2, 6, D), F32), _full((2, 6, D)), True),
                       (_sds((1, D), F32), _full((1, D)), True)])


def _gmlp_fn(u, v, g, ws, bst):
    rows = []
    for r in range(u.shape[0] // 128):
        uu, vv = _gelu(u[128 * r:128 * r + 128]), _gelu(v[128 * r:128 * r + 128])
        cols = []
        for gi in range(4):
            sl = slice(128 * gi, 128 * gi + 128)
            f = bdot(ws[gi], _rms(vv[:, sl], g[:, sl]), 1, 0) + bst[:, gi:gi + 1]
            cols.append(uu[:, sl] * f)
        rows.append(jnp.concatenate(cols, axis=-1))
    return jnp.concatenate(rows, axis=0)


def _gmlp_ins(p, g, ws, bst):
    return [(p, _rowspec(GW, OFF["gu"])), (p, _rowspec(GW, OFF["gv"])), (g, _full((1, GW))),
            (ws, _full((4, 128, 128))), (bst, _full((128, 4)))]


def _gmlp_fwd(name, p, g, ws, bst):
    t = p.shape[0]
    return _rows_call(name, lambda ids, *a: (_gmlp_fn(*a),), (t // TM,), _gmlp_ins(p, g, ws, bst),
                      [(_sds((t, GW), BF16), _rowspec(GW), False)])[0]


def _gmlp_bwd(name, p, g, ws, bst, dgm):
    t = p.shape[0]

    def fn(ids, u, v, gv, wv, bv, dv):
        _, vjp = jax.vjp(_gmlp_fn, u, v, gv, wv, bv)
        return vjp(dv)

    return _rows_call(name, fn, (t // TM,), _gmlp_ins(p, g, ws, bst) + [(dgm, _rowspec(GW))],
                      [(_sds((t, GW), BF16), _rowspec(GW), False), (_sds((t, GW), BF16), _rowspec(GW), False),
                       (_sds((1, GW), F32), _full((1, GW)), True), (_sds((4, 128, 128), F32), _full((4, 128, 128)), True),
                       (_sds((128, 4), F32), _full((128, 4)), True)])


def _qk_fn(q, k, gq, gk, cos, sin, seg, perm):
    cq, sq = jnp.concatenate([cos] * 4, axis=-1), jnp.concatenate([sin] * 4, axis=-1)
    qn = q * lax.rsqrt(hdot(q * q, seg) + EPS) * gq
    kn = k * lax.rsqrt(hdot(k * k, seg[:128, :128]) + EPS) * gk
    qr = qn * cq + hdot(qn, perm) * sq
    kr = kn * cos + hdot(kn, perm[:128, :128]) * sin
    return qr * (HD ** -0.5), kr


def _qk_ins(p, gq, gk, cos, sin, seg, perm):
    return [(p, _rowspec(512, OFF["q"])), (p, _rowspec(128, OFF["k"])), (gq, _full((1, 512))), (gk, _full((1, 128))),
            (cos, _rowspec(128)), (sin, _rowspec(128)), (seg, _full((512, 512))), (perm, _full((512, 512)))]


def _qk_fwd(name, p, gq, gk, cos, sin, seg, perm):
    t = p.shape[0]
    fn = lambda ids, q, k, a, b, c, s, sg, pm, v: _qk_fn(q, k, a, b, c, s, sg, pm) + (v,)
    return _rows_call(name, fn, (t // TM,), _qk_ins(p, gq, gk, cos, sin, seg, perm) + [(p, _rowspec(128, OFF["v"]))],
                      [(_sds((t, 512), BF16), _rowspec(512), False), (_sds((t, 128), BF16), _rowspec(128), False),
                       (_sds((t, 128), BF16), _rowspec(128), False)])


def _qk_bwd(name, p, gq, gk, cos, sin, seg, perm, dqr, dkr):
    t = p.shape[0]

    def fn(ids, q, k, a, b, c, s, sg, pm, dq, dk):
        _, vjp = jax.vjp(lambda q_, k_, a_, b_: _qk_fn(q_, k_, a_, b_, c, s, sg, pm), q, k, a, b)
        return vjp((dq, dk))

    return _rows_call(name, fn, (t // TM,),
                      _qk_ins(p, gq, gk, cos, sin, seg, perm) + [(dqr, _rowspec(512)), (dkr, _rowspec(128))],
                      [(_sds((t, 512), BF16), _rowspec(512), False), (_sds((t, 128), BF16), _rowspec(128), False),
                       (_sds((1, 512), F32), _full((1, 512)), True), (_sds((1, 128), F32), _full((1, 128)), True)])


_ATT_TQ = 256
_ATT_TK = 768


def _attn_fwd(name, q, k, v):
    h, tq_all, _ = q.shape
    hkv, tk_all, _ = k.shape
    tq, tk = _tile(tq_all, _ATT_TQ), _tile(tk_all, _ATT_TK)
    nkc = tk_all // tk

    def body(q_ref, k_ref, v_ref, o_ref, lse_ref):
        qv = q_ref[...].reshape(QG * tq, HD)

        def step(j, carry):
            m, acc = carry
            off = pl.multiple_of(j * tk, tk)
            kk, vv = k_ref[0, pl.ds(off, tk), :], v_ref[0, pl.ds(off, tk), :]
            s = lax.dot_general(qv, kk, (((1,), (1,)), ((), ())), preferred_element_type=F32)
            m_new = jnp.maximum(m, jnp.max(s, axis=-1, keepdims=True))
            pr = jnp.exp(s - m_new)
            acc = jnp.exp(m - m_new) * acc + jnp.dot(pr.astype(BF16), vv, preferred_element_type=F32)
            return m_new, acc

        init = (jnp.full((QG * tq, 1), -jnp.inf, F32), jnp.zeros((QG * tq, 2 * HD), F32))
        m, acc = lax.fori_loop(0, nkc, step, init)
        l = acc[:, HD:HD + 1]
        o_ref[...] = (acc[:, :HD] / l).reshape(QG, tq, HD)
        lse_ref[...] = (m + jnp.log(l)).reshape(QG, tq, 1)

    kv_spec = pl.BlockSpec((1, tk_all, HD), lambda g, i: (g, 0, 0))
    v1_spec = pl.BlockSpec((1, tk_all, 2 * HD), lambda g, i: (g, 0, 0))
    qspec = pl.BlockSpec((QG, tq, HD), lambda g, i: (g, i, 0))
    return pl.pallas_call(
        body, name=name, grid=(hkv, tq_all // tq), in_specs=[qspec, kv_spec, v1_spec],
        out_specs=[qspec, pl.BlockSpec((QG, tq, 1), lambda g, i: (g, i, 0))],
        out_shape=[_sds((h, tq_all, HD), F32), _sds((h, tq_all, 1), F32)],
        compiler_params=_cp(("parallel", "parallel")),
    )(q, k, v)


def _attn_bwd_q(name, q, k, v, o, do, lse):
    h, tq_all, _ = q.shape
    hkv, tk_all, _ = k.shape
    tq, tk = _tile(tq_all, _ATT_TQ), _tile(tk_all, _ATT_TK)
    nkc = tk_all // tk

    def body(q_ref, k_ref, v_ref, o_ref, do_ref, lse_ref, dq_ref, dl_ref):
        qs = [q_ref[g] for g in range(QG)]
        lses = [lse_ref[g] for g in range(QG)]
        deltas = [jnp.sum(do_ref[g] * o_ref[g], axis=-1, keepdims=True) for g in range(QG)]
        dobs = [do_ref[g].astype(BF16) for g in range(QG)]

        def step(j, dqs):
            off = pl.multiple_of(j * tk, tk)
            kk, vv = k_ref[0, pl.ds(off, tk), :], v_ref[0, pl.ds(off, tk), :]
            out = []
            for g in range(QG):
                s = lax.dot_general(qs[g], kk, (((1,), (1,)), ((), ())), preferred_element_type=F32)
                pr = jnp.exp(s - lses[g])
                dp = lax.dot_general(dobs[g], vv, (((1,), (1,)), ((), ())), preferred_element_type=F32)
                ds = pr * (dp - deltas[g])
                out.append(dqs[g] + jnp.dot(ds.astype(BF16), kk, preferred_element_type=F32))
            return tuple(out)

        res = lax.fori_loop(0, nkc, step, (jnp.zeros((tq, HD), F32),) * QG)
        for g in range(QG):
            dq_ref[g] = res[g]
            dl_ref[g] = deltas[g]

    kv_spec = pl.BlockSpec((1, tk_all, HD), lambda g, i: (g, 0, 0))
    qspec = pl.BlockSpec((QG, tq, HD), lambda g, i: (g, i, 0))
    cs = pl.BlockSpec((QG, tq, 1), lambda g, i: (g, i, 0))
    return pl.pallas_call(
        body, name=name, grid=(hkv, tq_all // tq), in_specs=[qspec, kv_spec, kv_spec, qspec, qspec, cs], out_specs=[qspec, cs],
        out_shape=[_sds((h, tq_all, HD), F32), _sds((h, tq_all, 1), F32)],
        compiler_params=_cp(("parallel", "parallel")),
    )(q, k, v, o, do, lse)


def _attn_bwd(name, q, k, v, o, do, lse_row):
    h, tq_all, _ = q.shape
    hkv, tk_all, _ = k.shape
    tq, tk = _tile(tq_all, 1024), _tile(tk_all, 768)

    def body(q_ref, k_ref, v_ref, o_ref, do_ref, lse_ref, dq_ref, dk_ref, dv_ref, dl_ref):
        i, j = pl.program_id(1), pl.program_id(2)

        @pl.when(j == 0)
        def _():
            ones = jnp.ones((8, HD), F32)
            for g in range(QG):
                dl_ref[g] = hdot(ones, do_ref[g] * o_ref[g], 1, 1)

        kk, vv = k_ref[0], v_ref[0]
        dk_acc, dv_acc = jnp.zeros((tk, HD), F32), jnp.zeros((tk, HD), F32)
        for g in range(QG):
            qv, dob = q_ref[g], do_ref[g].astype(BF16)
            st = lax.dot_general(kk, qv, (((1,), (1,)), ((), ())), preferred_element_type=F32)
            pt = jnp.exp(st - lse_ref[g])
            dv_acc += jnp.dot(pt.astype(BF16), dob, preferred_element_type=F32)
            dpt = lax.dot_general(vv, dob, (((1,), (1,)), ((), ())), preferred_element_type=F32)
            dst = (pt * (dpt - dl_ref[g, 0:1, :])).astype(BF16)
            dk_acc += jnp.dot(dst, qv, preferred_element_type=F32)
            dq_part = lax.dot_general(dst, kk, (((0,), (0,)), ((), ())), preferred_element_type=F32)

            @pl.when(j == 0)
            def _():
                dq_ref[g] = dq_part

            @pl.when(j > 0)
            def _():
                dq_ref[g] += dq_part

        rows = pl.ds(pl.multiple_of(j * tk, tk), tk)

        @pl.when(i == 0)
        def _():
            dk_ref[0, rows, :] = dk_acc
            dv_ref[0, rows, :] = dv_acc

        @pl.when(i > 0)
        def _():
            dk_ref[0, rows, :] += dk_acc
            dv_ref[0, rows, :] += dv_acc

    ks = pl.BlockSpec((1, tk, HD), lambda g, i, j: (g, j, 0))
    qs = pl.BlockSpec((QG, tq, HD), lambda g, i, j: (g, i, 0))
    rs = pl.BlockSpec((QG, 1, tq), lambda g, i, j: (g, 0, i))
    full = pl.BlockSpec((1, tk_all, HD), lambda g, i, j: (g, 0, 0))
    return pl.pallas_call(
        body, name=name, grid=(hkv, tq_all // tq, tk_all // tk), in_specs=[qs, ks, ks, qs, qs, rs], out_specs=[qs, full, full],
        out_shape=[_sds((h, tq_all, HD), F32), _sds((hkv, tk_all, HD), F32), _sds((hkv, tk_all, HD), F32)],
        scratch_shapes=[pltpu.VMEM((QG, 8, tq), F32)],
        compiler_params=_cp(("parallel", "arbitrary", "arbitrary")),
    )(q, k, v, o, do, lse_row)


def _attn_bwd_kv(name, q, k, v, do, lse_row, delta_row):
    h, tq_all, _ = q.shape
    hkv, tk_all, _ = k.shape
    tq, tk = _tile(tq_all, 1024), _tile(tk_all, 1056)
    nq = tq_all // tq

    def body(q_ref, k_ref, v_ref, do_ref, lse_ref, dl_ref, dk_ref, dv_ref, dk_acc, dv_acc):
        i = pl.program_id(2)

        @pl.when(i == 0)
        def _():
            dk_acc[...] = jnp.zeros_like(dk_acc)
            dv_acc[...] = jnp.zeros_like(dv_acc)

        kk, vv = k_ref[0], v_ref[0]
        for g in range(QG):
            qv, dob = q_ref[g], do_ref[g].astype(BF16)
            st = lax.dot_general(kk, qv, (((1,), (1,)), ((), ())), preferred_element_type=F32)
            pt = jnp.exp(st - lse_ref[g])
            dv_acc[...] += jnp.dot(pt.astype(BF16), dob, preferred_element_type=F32)
            dpt = lax.dot_general(vv, dob, (((1,), (1,)), ((), ())), preferred_element_type=F32)
            dst = pt * (dpt - dl_ref[g])
            dk_acc[...] += jnp.dot(dst.astype(BF16), qv, preferred_element_type=F32)

        @pl.when(i == nq - 1)
        def _():
            dk_ref[0] = dk_acc[...]
            dv_ref[0] = dv_acc[...]

    ks = pl.BlockSpec((1, tk, HD), lambda g, j, i: (g, j, 0))
    qs = pl.BlockSpec((QG, tq, HD), lambda g, j, i: (g, i, 0))
    rs = pl.BlockSpec((QG, 1, tq), lambda g, j, i: (g, 0, i))
    return pl.pallas_call(
        body, name=name, grid=(hkv, tk_all // tk, nq), in_specs=[qs, ks, ks, qs, rs, rs], out_specs=[ks, ks],
        out_shape=[_sds((hkv, tk_all, HD), F32), _sds((hkv, tk_all, HD), F32)],
        scratch_shapes=[pltpu.VMEM((tk, HD), F32), pltpu.VMEM((tk, HD), F32)],
        compiler_params=_cp(("parallel", "parallel", "arbitrary")),
    )(q, k, v, do, lse_row, delta_row)


def _decay_fn(a, w2, b2):
    return _log_sigmoid(bdot(a, w2, 1, 0) + b2) / GLA_TAU


def _decay_fwd(name, p, w2, b2):
    t = p.shape[0]
    return _rows_call(name, lambda ids, a, w, b: (_decay_fn(a, w, b),), (t // TM,),
                      [(p, _rowspec(128, OFF["ab"])), (w2, _full((128, 512))), (b2, _full((1, 512)))],
                      [(_sds((t, 512), F32), _rowspec(512), False)])[0]


def _decay_bwd(name, p, w2, b2, dla_f, dla_b):
    t = p.shape[0]

    def fn(ids, a, w, b, df, db):
        _, vjp = jax.vjp(_decay_fn, a, w, b)
        return vjp(jnp.concatenate([df, db], axis=-1))

    return _rows_call(name, fn, (t // TM,),
                      [(p, _rowspec(128, OFF["ab"])), (w2, _full((128, 512))), (b2, _full((1, 512))),
                       (dla_f, _rowspec(256)), (dla_b, _rowspec(256))],
                      [(_sds((t, 128), BF16), _rowspec(128), False), (_sds((128, 512), F32), _full((128, 512)), True),
                       (_sds((1, 512), F32), _full((1, 512)), True)])


def _gla_consts(reverse):
    r = lax.broadcasted_iota(jnp.int32, (GLA_CHUNK, GLA_CHUNK), 0)
    c = lax.broadcasted_iota(jnp.int32, (GLA_CHUNK, GLA_CHUNK), 1)
    trib = (r <= c) if reverse else (r >= c)
    br = lax.broadcasted_iota(jnp.int32, (GLA_QK, GLA_V), 0) // GLA_DK
    bc = lax.broadcasted_iota(jnp.int32, (GLA_QK, GLA_V), 1) // GLA_DV
    lane_head = lax.broadcasted_iota(jnp.int32, (1, GLA_QK), 1) // GLA_DK
    return trib, (br == bc).astype(F32), lane_head


def _gla_chunk(q, k, v, la, s_in, consts):
    trib, bd, lane_head = consts
    cum = hdot(trib.astype(F32), la)
    tot = jnp.sum(la, axis=0, keepdims=True)
    q_in = q * (GLA_DK ** -0.5) * jnp.exp(cum)
    k_in = k * jnp.exp(-cum)
    k_st = k * jnp.exp(tot - cum)
    outs = []
    for h in range(GLA_H):
        att = bdot(jnp.where(lane_head == h, q_in, 0.0), k_in, 1, 1)
        att = jnp.where(trib, att, 0.0)
        outs.append(bdot(att, v[:, GLA_DV * h:GLA_DV * (h + 1)], 1, 0))
    o = jnp.concatenate(outs, axis=-1) + bdot(q_in, s_in, 1, 0)
    decay = jnp.exp(hdot(la, jnp.ones((GLA_CHUNK, GLA_V), F32), 0, 0))
    s_out = decay * s_in + bdot(k_st, v, 0, 0) * bd
    return o, s_out


def _gla_order(nb, reverse, backward):
    if not reverse:
        return (lambda s: nb - 1 - s) if backward else (lambda s: s)
    if backward:
        return lambda s: jnp.where(s == nb - 1, 0, s + 1)
    return lambda s: jnp.where(s == 0, 0, nb - s)


_NCH = TM // GLA_CHUNK


def _gla_fwd(name, p, la, reverse):
    t = p.shape[0]
    nb = t // TM
    order = _gla_order(nb, reverse, False)

    def body(q_ref, k_ref, v_ref, la_ref, o_ref, sv_ref, s_ref):
        @pl.when(pl.program_id(0) == 0)
        def _():
            s_ref[...] = jnp.zeros_like(s_ref)

        consts = _gla_consts(reverse)
        for c in (range(_NCH - 1, -1, -1) if reverse else range(_NCH)):
            rows = slice(GLA_CHUNK * c, GLA_CHUNK * (c + 1))
            s_in = s_ref[...]
            for h in range(GLA_H):
                sv_ref[c, h] = s_in[GLA_DK * h:GLA_DK * (h + 1), GLA_DV * h:GLA_DV * (h + 1)]
            o, s_out = _gla_chunk(q_ref[rows, :], k_ref[rows, :], v_ref[rows, :], la_ref[rows, :], s_in, consts)
            o_ref[rows, :] = o
            s_ref[...] = s_out

    def col(width, off):
        return pl.BlockSpec((TM, width), lambda s, o=off // width: (order(s), o))

    return pl.pallas_call(
        body, name=name, grid=(nb,),
        in_specs=[col(256, OFF["glq"]), col(256, OFF["glk"]), col(512, OFF["glv"]), col(256, 256 * int(reverse))],
        out_specs=[col(512, 0), pl.BlockSpec((_NCH, GLA_H, GLA_DK, GLA_DV), lambda s: (order(s), 0, 0, 0))],
        out_shape=[_sds((t, GLA_V), F32), _sds((t // GLA_CHUNK, GLA_H, GLA_DK, GLA_DV), F32)],
        scratch_shapes=[pltpu.VMEM((GLA_QK, GLA_V), F32)], compiler_params=_cp(("arbitrary",)),
    )(p, p, p, la)


def _gla_bwd(name, p, la, sv, do, reverse, prev=None):
    t = p.shape[0]
    nb = t // TM
    order = _gla_order(nb, reverse, True)
    n_prev = 0 if prev is None else 3

    def body(*refs):
        q_ref, k_ref, v_ref, la_ref, sv_ref, do_ref = refs[:6]
        prev_refs = refs[6:6 + n_prev]
        dq_ref, dk_ref, dv_ref, dla_ref, ds_ref = refs[6 + n_prev:]

        @pl.when(pl.program_id(0) == 0)
        def _():
            ds_ref[...] = jnp.zeros_like(ds_ref)

        consts = _gla_consts(reverse)
        zero = jnp.zeros((GLA_DK, GLA_DV), F32)
        for c in (range(_NCH) if reverse else range(_NCH - 1, -1, -1)):
            rows = slice(GLA_CHUNK * c, GLA_CHUNK * (c + 1))
            s_in = jnp.concatenate(
                [jnp.concatenate([sv_ref[c, h] if hh == h else zero for hh in range(GLA_H)], axis=-1) for h in range(GLA_H)], axis=0)
            _, vjp = jax.vjp(lambda a, b, cc, d, e: _gla_chunk(a, b, cc, d, e, consts),
                             q_ref[rows, :], k_ref[rows, :], v_ref[rows, :], la_ref[rows, :], s_in)
            dq, dk, dv, dla, ds_in = vjp((do_ref[rows, :], ds_ref[...]))
            if n_prev:
                dq, dk, dv = dq + prev_refs[0][rows, :], dk + prev_refs[1][rows, :], dv + prev_refs[2][rows, :]
            dq_ref[rows, :], dk_ref[rows, :], dv_ref[rows, :], dla_ref[rows, :] = dq, dk, dv, dla
            ds_ref[...] = ds_in

    def col(width, off):
        return pl.BlockSpec((TM, width), lambda s, o=off // width: (order(s), o))

    ins = [p, p, p, la, sv, do] + (list(prev) if n_prev else [])
    in_specs = [col(256, OFF["glq"]), col(256, OFF["glk"]), col(512, OFF["glv"]), col(256, 256 * int(reverse)),
                pl.BlockSpec((_NCH, GLA_H, GLA_DK, GLA_DV), lambda s: (order(s), 0, 0, 0)), col(512, 0)]
    in_specs += [col(256, 0), col(256, 0), col(512, 0)][:n_prev]
    return pl.pallas_call(
        body, name=name, grid=(nb,), in_specs=in_specs, out_specs=[col(256, 0), col(256, 0), col(512, 0), col(256, 0)],
        out_shape=[_sds((t, GLA_QK), F32), _sds((t, GLA_QK), F32), _sds((t, GLA_V), F32), _sds((t, GLA_QK), F32)],
        scratch_shapes=[pltpu.VMEM((GLA_QK, GLA_V), F32)], compiler_params=_cp(("arbitrary",)),
    )(*ins)


def _gla_out_fn(of, ob, r, g):
    o = of + ob
    cols = [_rms(o[:, GLA_DV * h:GLA_DV * (h + 1)], g[:, GLA_DV * h:GLA_DV * (h + 1)]) for h in range(GLA_H)]
    return jnp.concatenate(cols, axis=-1) * jax.nn.silu(r)


def _gla_out_fwd(name, of, ob, p, g):
    t = p.shape[0]
    return _rows_call(name, lambda ids, *a: (_gla_out_fn(*a),), (t // TM,),
                      [(of, _rowspec(512)), (ob, _rowspec(512)), (p, _rowspec(512, OFF["gr"])), (g, _full((1, 512)))],
                      [(_sds((t, 512), BF16), _rowspec(512), False)])[0]


def _gla_out_bwd(name, of, ob, p, g, dgla):
    t = p.shape[0]

    def fn(ids, a, b, r, gv, dv):
        _, vjp = jax.vjp(_gla_out_fn, a, b, r, gv)
        do, _, dr, dg = vjp(dv)
        return do, dr, dg

    return _rows_call(name, fn, (t // TM,),
                      [(of, _rowspec(512)), (ob, _rowspec(512)), (p, _rowspec(512, OFF["gr"])), (g, _full((1, 512))),
                       (dgla, _rowspec(512))],
                      [(_sds((t, 512), F32), _rowspec(512), False), (_sds((t, 512), BF16), _rowspec(512), False),
                       (_sds((1, 512), F32), _full((1, 512)), True)])


_TMM = 384


def _merge_fwd(name, gm, att, gla, wa, wb, wc, p):
    t = p.shape[0]
    row = lambda w, off=0: pl.BlockSpec((_TMM, w), lambda i, o=off // w: (i, o))

    def fn(ids, a, b, c, wa_, wb_, wc_, ga, gb, gc):
        return (jax.nn.sigmoid(ga) * bdot(a, wa_, 1, 0) + jax.nn.sigmoid(gb) * bdot(b, wb_, 1, 0)
                + jax.nn.sigmoid(gc) * bdot(c, wc_, 1, 0),)

    return _rows_call(name, fn, (t // _TMM,),
                      [(gm, row(512)), (att, row(512)), (gla, row(512)), (wa, _full((512, D))), (wb, _full((512, D))),
                       (wc, _full((512, D))), (p, row(D, OFF["gA"])), (p, row(D, OFF["gB"])), (p, row(D, OFF["gC"]))],
                      [(_sds((t, D), BF16), row(D), False)])[0]


def _merge_bwd(name, gm, att, gla, wa, wb, wc, p, dmerged):
    t = p.shape[0]
    row = lambda w, off=0: pl.BlockSpec((_TMM, w), lambda i, o=off // w: (i, o))

    def fn(ids, a, b, c, wa_, wb_, wc_, ga, gb, gc, dm):
        outs_y, outs_g = [], []
        for br, w, g in ((a, wa_, ga), (b, wb_, gb), (c, wc_, gc)):
            s = jax.nn.sigmoid(g)
            outs_y.append(dm * s)
            outs_g.append(dm * bdot(br, w, 1, 0) * s * (1.0 - s))
        return tuple(outs_y) + tuple(outs_g)

    o = (_sds((t, D), BF16), row(D), False)
    return _rows_call(name, fn, (t // _TMM,),
                      [(gm, row(512)), (att, row(512)), (gla, row(512)), (wa, _full((512, D))), (wb, _full((512, D))),
                       (wc, _full((512, D))), (p, row(D, OFF["gA"])), (p, row(D, OFF["gB"])), (p, row(D, OFF["gC"])),
                       (dmerged, row(D))], [o] * 6)


_TNC = 1408
_NJ = FFN // _TNC


def _shift_rows(x, prev8, next8, vp, vn):
    n = x.shape[0]
    rid = lax.broadcasted_iota(jnp.int32, x.shape, 0)
    xp = jnp.where(rid == 0, jnp.where(vp, prev8[7:8, :], 0.0), pltpu.roll(x, 1, 0))
    xn = jnp.where(rid == n - 1, jnp.where(vn, next8[0:1, :], 0.0), pltpu.roll(x, n - 1, 0))
    return xp, xn


def _seq_edges(i, t):
    start, end = i * TM, (i + 1) * TM
    return jnp.logical_and(start != 0, start != TC), jnp.logical_and(end != TC, end != t)


def _halo_specs(t, colmap):
    r8 = TM // 8
    main = pl.BlockSpec((TM, _TNC), lambda j, i: (i, colmap(j)))
    prev = pl.BlockSpec((8, _TNC), lambda j, i: (jnp.maximum(i * r8 - 1, 0), colmap(j)))
    nxt = pl.BlockSpec((8, _TNC), lambda j, i: (jnp.minimum((i + 1) * r8, t // 8 - 1), colmap(j)))
    return [main, prev, nxt]


def _conv3(x, xp, xn, w, b=None):
    y = xp * w[0:1, :] + x * w[1:2, :] + xn * w[2:3, :]
    return y if b is None else b + y


def _conv_fwd(name, a, cw, cb):
    t = a.shape[0]

    def fn(ids, ag, agp, agn, av, avp, avn, wg, wv, bg, bv):
        vp, vn = _seq_edges(ids[1], t)
        cg = _conv3(ag, *_shift_rows(ag, agp, agn, vp, vn), wg, bg)
        cv = _conv3(av, *_shift_rows(av, avp, avn, vp, vn), wv, bv)
        return (jax.nn.silu(cg) * cv,)

    gcol, vcol = (lambda j: j), (lambda j: j + _NJ)
    wspec = lambda cm: pl.BlockSpec((3, _TNC), lambda j, i: (0, cm(j)))
    bspec = lambda cm: pl.BlockSpec((1, _TNC), lambda j, i: (0, cm(j)))
    ins = [(a, s) for s in _halo_specs(t, gcol) + _halo_specs(t, vcol)]
    ins += [(cw, wspec(gcol)), (cw, wspec(vcol)), (cb, bspec(gcol)), (cb, bspec(vcol))]
    return _rows_call(name, fn, (_NJ, t // TM), ins,
                      [(_sds((t, FFN), BF16), pl.BlockSpec((TM, _TNC), lambda j, i: (i, j)), False)])[0]


def _conv_bwd_gate(name, a, cw, cb, dact):
    t = a.shape[0]

    def fn(ids, ag, agp, agn, av, avp, avn, wg, wv, bg, bv, dv):
        vp, vn = _seq_edges(ids[1], t)
        cg = _conv3(ag, *_shift_rows(ag, agp, agn, vp, vn), wg, bg)
        cv = _conv3(av, *_shift_rows(av, avp, avn, vp, vn), wv, bv)
        s = jax.nn.sigmoid(cg)
        d_gate = dv * cv * s * (1.0 + cg * (1.0 - s))
        d_val = dv * cg * s
        return (jnp.where(ids[0] >= _NJ, d_val, d_gate),)

    gcol, vcol = (lambda j: j % _NJ), (lambda j: j % _NJ + _NJ)
    wspec = lambda cm: pl.BlockSpec((3, _TNC), lambda j, i: (0, cm(j)))
    bspec = lambda cm: pl.BlockSpec((1, _TNC), lambda j, i: (0, cm(j)))
    ins = [(a, s) for s in _halo_specs(t, gcol) + _halo_specs(t, vcol)]
    ins += [(cw, wspec(gcol)), (cw, wspec(vcol)), (cb, bspec(gcol)), (cb, bspec(vcol)),
            (dact, pl.BlockSpec((TM, _TNC), lambda j, i: (i, j % _NJ)))]
    return _rows_call(name, fn, (2 * _NJ, t // TM), ins,
                      [(_sds((t, 2 * FFN), F32), pl.BlockSpec((TM, _TNC), lambda j, i: (i, j)), False)])[0]


def _conv_bwd_in(name, a, cw, dconv):
    t = a.shape[0]

    def fn(ids, x, xpb, xnb, dc, dcpb, dcnb, w):
        vp, vn = _seq_edges(ids[1], t)
        xp, xn = _shift_rows(x, xpb, xnb, vp, vn)
        dcp, dcn = _shift_rows(dc, dcpb, dcnb, vp, vn)
        da = dcn * w[0:1, :] + dc * w[1:2, :] + dcp * w[2:3, :]
        sums = [jnp.sum(dc * y, axis=0, keepdims=True) for y in (xp, x, xn)]
        rid = lax.broadcasted_iota(jnp.int32, (3, x.shape[1]), 0)
        dw = jnp.where(rid == 0, sums[0], jnp.where(rid == 1, sums[1], sums[2]))
        return da, dw, jnp.sum(dc, axis=0, keepdims=True)

    col = lambda j: j
    ins = [(a, s) for s in _halo_specs(t, col)] + [(dconv, s) for s in _halo_specs(t, col)]
    ins += [(cw, pl.BlockSpec((3, _TNC), lambda j, i: (0, j)))]
    return _rows_call(name, fn, (2 * _NJ, t // TM), ins,
                      [(_sds((t, 2 * FFN), BF16), pl.BlockSpec((TM, _TNC), lambda j, i: (i, j)), False),
                       (_sds((3, 2 * FFN), F32), pl.BlockSpec((3, _TNC), lambda j, i: (0, j)), True),
                       (_sds((1, 2 * FFN), F32), pl.BlockSpec((1, _TNC), lambda j, i: (0, j)), True)])


_TNA = 512


def _adaln_fwd(name, cond, w, b):
    fn = lambda ids, cv, wv, bv: ((bdot(jax.nn.silu(cv), wv[0], 1, 0) + bv[0])[None],)
    return _rows_call(name, fn, (2, ADA_LOC // _TNA),
                      [(cond, _full((16, D))), (w, pl.BlockSpec((1, D, _TNA), lambda l, j: (l, 0, j))),
                       (b, pl.BlockSpec((1, 1, _TNA), lambda l, j: (l, 0, j)))],
                      [(_sds((2, 16, ADA_LOC), F32), pl.BlockSpec((1, 16, _TNA), lambda l, j: (l, 0, j)), False)])[0]


def _adaln_bwd(name, c8, cc8, w, dl, dc):
    def fn(ids, cv, ccv, wv, dlv, dcv):
        dcs = jnp.broadcast_to(jnp.sum(dcv[0], axis=0, keepdims=True), dcv[0].shape)
        dw = hdot(jax.nn.silu(cv), dlv[0], 0, 0) + hdot(jax.nn.silu(ccv), dcs, 0, 0)
        s = jax.nn.sigmoid(ccv)
        rid = lax.broadcasted_iota(jnp.int32, ccv.shape, 0)
        dcc = jnp.where(rid == 0, bdot(dcs, wv[0], 1, 1) * s * (1.0 + ccv * (1.0 - s)), 0.0)
        return dw[None], dcc

    dspec = pl.BlockSpec((1, 8, _TNA), lambda l, j: (l, 0, j))
    return _rows_call(name, fn, (2, ADA_LOC // _TNA),
                      [(c8, _full((8, D))), (cc8, _full((8, D))), (w, pl.BlockSpec((1, D, _TNA), lambda l, j: (l, 0, j))),
                       (dl, dspec), (dc, dspec)],
                      [(_sds((2, D, ADA_LOC), F32), pl.BlockSpec((1, D, _TNA), lambda l, j: (l, 0, j)), False),
                       (_sds((8, D), F32), _full((8, D)), True)], acc_axes=(0, 1))


def _adamw_fn(w, g, m, v):
    m = ADAM_B1 * m + (1.0 - ADAM_B1) * g
    v = ADAM_B2 * v + (1.0 - ADAM_B2) * (g * g)
    m_hat = m / (1.0 - ADAM_B1 ** ADAM_STEP)
    v_hat = v / (1.0 - ADAM_B2 ** ADAM_STEP)
    return -ADAM_LR * (m_hat / (jnp.sqrt(v_hat) + ADAM_EPS) + ADAM_WD * w), m, v


def _adamw(name, w, g, m, v):
    r, c = w.shape
    tr = _tile(r, max(8, (1 << 20) // (4 * c)), 8)
    spec = pl.BlockSpec((tr, c), lambda i: (i, 0))
    o = (_sds((r, c), F32), spec, False)
    return _rows_call(name, lambda ids, *a: _adamw_fn(*a), (r // tr,), [(x, spec) for x in (w, g, m, v)], [o, o, o],
                      sem=("parallel",))


def _coords():
    return lax.axis_index("x"), lax.axis_index("y"), lax.axis_index("c")


def _other_chips(x, y):
    return [(1 - x, y), (x, 1 - y), (1 - x, 1 - y)]


def _allgather_small(name, blk):
    m_per, n = blk.shape

    def body(x_ref, out_ref, send_sems, recv_sems, local_sem):
        x, y, c = _coords()
        me, sibling = (x, y, c), (x, y, 1 - c)
        chips = _other_chips(x, y)

        def rows(px, py, pc):
            return out_ref.at[pl.ds((4 * px + 2 * py + pc) * m_per, m_per), :]

        def copy(k, block, to, src=None):
            return pltpu.make_async_remote_copy(
                src_ref=rows(*block) if src is None else src, dst_ref=rows(*block), send_sem=send_sems.at[k],
                recv_sem=recv_sems.at[k], device_id=to, device_id_type=MESH)

        mine = pltpu.make_async_copy(x_ref, rows(*me), local_sem)
        mine.start()
        first = [copy(0, me, sibling, src=x_ref)]
        first += [copy(1 + j, me, (*chip, c), src=x_ref) for j, chip in enumerate(chips)]
        for cp in first:
            cp.start()
        passed = [copy(4 + j, (*chip, c), sibling) for j, chip in enumerate(chips)]
        for j, chip in enumerate(chips):
            copy(1 + j, (*chip, c), me).wait_recv()
            passed[j].start()
        copy(0, sibling, me).wait_recv()
        for j, chip in enumerate(chips):
            copy(4 + j, (*chip, 1 - c), me).wait_recv()
        for cp in first + passed:
            cp.wait_send()
        mine.wait()

    return pl.pallas_call(
        body, name=name, out_shape=_sds((N_DEV * m_per, n), blk.dtype),
        in_specs=[pl.BlockSpec(memory_space=pltpu.VMEM)], out_specs=pl.BlockSpec(memory_space=pltpu.VMEM),
        scratch_shapes=[pltpu.SemaphoreType.DMA((7,)), pltpu.SemaphoreType.DMA((7,)), pltpu.SemaphoreType.DMA],
        compiler_params=pltpu.CompilerParams(vmem_limit_bytes=VMEM_LIMIT),
    )(blk)


_ANY = pl.BlockSpec(memory_space=pl.ANY)


def _remote(src, dst, send_sems, recv_sems, s, to):
    return pltpu.make_async_remote_copy(src_ref=src, dst_ref=dst, send_sem=send_sems.at[s], recv_sem=recv_sems.at[s],
                                        device_id=to, device_id_type=MESH)


def _comm_call(name, body, ins, out_shapes, n_sems, n_local):
    return pl.pallas_call(
        body, name=name, out_shape=out_shapes, in_specs=[_ANY] * len(ins), out_specs=[_ANY] * len(out_shapes),
        scratch_shapes=[pltpu.SemaphoreType.DMA((n_sems,)), pltpu.SemaphoreType.DMA((n_sems,)),
                        pltpu.SemaphoreType.DMA((n_local,))],
    )(*ins)


def _allgather_layers(name, locs):
    n = len(locs)

    def body(*refs):
        ins, outs, (send_sems, recv_sems, local_sems) = refs[:n], refs[n:2 * n], refs[2 * n:]
        x, y, c = _coords()
        k = 2 * x + y
        sibling = (x, y, 1 - c)
        chips = _other_chips(x, y)
        first = [_remote(ins[t].at[c], outs[t].at[k, c], send_sems, recv_sems, 6 * t + j, (*chip, c))
                 for t in range(n) for j, chip in enumerate(chips)]
        for cp in first:
            cp.start()
        passed = []
        for t in range(n):
            for j, (cx, cy) in enumerate(chips):
                there = outs[t].at[2 * cx + cy, c]
                _remote(there, there, send_sems, recv_sems, 6 * t + j, sibling).wait_recv()
                passed.append(_remote(there, there, send_sems, recv_sems, 6 * t + 3 + j, sibling))
                passed[-1].start()
        for t in range(n):
            for j, (cx, cy) in enumerate(chips):
                there = outs[t].at[2 * cx + cy, 1 - c]
                _remote(there, there, send_sems, recv_sems, 6 * t + 3 + j, sibling).wait_recv()
        for cp in first + passed:
            cp.wait_send()

    outs = _comm_call(name, body, locs, [_sds((N_CHIP,) + a.shape, a.dtype) for a in locs], 6 * n, 1)
    k = 2 * lax.axis_index("x") + lax.axis_index("y")
    return [lax.dynamic_update_slice_in_dim(o, a[None], k, axis=0) for o, a in zip(outs, locs)]


def _rs_pair_exchange(name, g0, g1):
    n = len(g0)

    def body(*refs):
        a0, a1, outs, (send_sems, recv_sems, _) = refs[:n], refs[n:2 * n], refs[2 * n:3 * n], refs[3 * n:]
        x, y, c = _coords()

        def run(srcs):
            cps = [_remote(srcs[t], outs[t], send_sems, recv_sems, t, (x, y, 1 - c)) for t in range(n)]
            for cp in cps:
                cp.start()
            for cp in cps:
                cp.wait()

        pl.when(c == 0)(lambda: run(a1))
        pl.when(c == 1)(lambda: run(a0))

    return _comm_call(name, body, list(g0) + list(g1), [_sds(a.shape, a.dtype) for a in g0], n, 1)


def _ew2d(name, fn, ins, out_dtype):
    shape = ins[0].shape
    r, c = _prod(shape[:-1]), shape[-1]
    tr = _tile(r, max(8, (1 << 20) // (4 * c)), 8)
    spec = pl.BlockSpec((tr, c), lambda i: (i, 0))
    out = _rows_call(name, lambda ids, *a: (fn(*a),), (r // tr,), [(a.reshape(r, c), spec) for a in ins],
                     [(_sds((r, c), out_dtype), spec, False)], sem=("parallel",))[0]
    return out.reshape(shape)


def _rs_chip_exchange(name, s1):
    n = len(s1)

    def body(*refs):
        ins, outs, (send_sems, recv_sems, local_sems) = refs[:n], refs[n:2 * n], refs[2 * n:]
        x, y, c = _coords()
        k = 2 * x + y
        chips = _other_chips(x, y)
        cps = [_remote(ins[t].at[2 * cx + cy], outs[t].at[k], send_sems, recv_sems, 3 * t + j, (cx, cy, c))
               for t in range(n) for j, (cx, cy) in enumerate(chips)]
        for cp in cps:
            cp.start()
        for t in range(n):
            for j, (cx, cy) in enumerate(chips):
                there = outs[t].at[2 * cx + cy]
                _remote(there, there, send_sems, recv_sems, 3 * t + j, (cx, cy, c)).wait_recv()
        for cp in cps:
            cp.wait_send()

    outs = _comm_call(name, body, s1, [_sds(a.shape, a.dtype) for a in s1], 3 * n, 1)
    k = 2 * lax.axis_index("x") + lax.axis_index("y")
    own = [lax.dynamic_index_in_dim(a, k, axis=0, keepdims=True) for a in s1]
    return [lax.dynamic_update_slice_in_dim(o, a, k, axis=0) for o, a in zip(outs, own)]


def _sum_slots(name, a):
    s, r, cdim = a.shape
    tr = _tile(r, 512, 8)

    def fn(ids, av):
        tot = av[0]
        for i in range(1, s):
            tot = tot + av[i]
        return (tot,)

    return _rows_call(name, fn, (r // tr,), [(a, pl.BlockSpec((s, tr, cdim), lambda i: (0, i, 0)))],
                      [(_sds((r, cdim), F32), pl.BlockSpec((tr, cdim), lambda i: (i, 0)), False)], sem=("parallel",))[0]


def _pair_allgather(name, halves):
    n = len(halves)

    def body(*refs):
        ins, outs, (send_sems, recv_sems, local_sems) = refs[:n], refs[n:2 * n], refs[2 * n:]
        x, y, c = _coords()
        cps = [_remote(ins[t], outs[t].at[c], send_sems, recv_sems, t, (x, y, 1 - c)) for t in range(n)]
        for cp in cps:
            cp.start()
        for t in range(n):
            _remote(ins[t], outs[t].at[1 - c], send_sems, recv_sems, t, (x, y, 1 - c)).wait_recv()
        for cp in cps:
            cp.wait_send()

    outs = _comm_call(name, body, halves, [_sds((2,) + a.shape, a.dtype) for a in halves], n, 1)
    return [lax.dynamic_update_slice_in_dim(o, a[None], lax.axis_index("c"), axis=0) for o, a in zip(outs, halves)]


def _reduce_scatter(g0, g1):
    n = len(g0)
    got = _rs_pair_exchange("rs_pair_exchange", g0, g1)
    keep = lambda a, b, r: jnp.where(lax.axis_index("c") == 0, a, b) + r
    s1 = [_ew2d("rs_pair_add_%d" % t, keep, [g0[t], g1[t], got[t]], BF16) for t in range(n)]
    slots = _rs_chip_exchange("rs_chip_exchange", s1)
    red = [_sum_slots("rs_chip_sum_%d" % t, a.reshape(N_CHIP, -1, a.shape[-1])).reshape(a.shape[1:])
           for t, a in enumerate(slots)]
    return _pair_allgather("rs_pair_allgather", red)


PACK_C = 1024
_SHARDED = (("w_in", 1), ("w_br_a", 1), ("w_br_b", 1), ("w_br_c", 1), ("w_out", 0), ("w_ffn_up", 1), ("w_ffn_down", 0))
_SHARDED_SMALL = (("conv_w", (3, 2 * FFN), 1), ("w_alpha2", (2, 16, GLA_QK), 2), ("b_alpha", (2, GLA_QK), 1))


def _prod(shape):
    n = 1
    for s in shape:
        n *= s
    return n


def _to_blocks(full, axis):
    shp = full.shape
    split = full.reshape(shp[:axis] + (N_CHIP, shp[axis] // N_CHIP) + shp[axis + 1:])
    return jnp.moveaxis(split, axis, 0)


def _from_blocks(blocks, axis):
    return jnp.concatenate([blocks[k] for k in range(N_CHIP)], axis=axis)


def _rope_tables(tx):
    pos = jnp.arange(tx, dtype=jnp.int32)
    inv_freq = 10000.0 ** (-jnp.arange(16, dtype=F32) / 16)
    ang_r = (pos // GRID_W).astype(F32)[:, None] * inv_freq
    ang_c = (pos % GRID_W).astype(F32)[:, None] * inv_freq
    ang = jnp.concatenate([ang_r, ang_r, ang_c, ang_c], axis=-1)
    sign = jnp.concatenate([-jnp.ones((16,), F32), jnp.ones((16,), F32)] * 2)
    cos = jnp.concatenate([jnp.ones((TC, HD), F32), jnp.cos(ang)], axis=0)
    sin = jnp.concatenate([jnp.zeros((TC, HD), F32), jnp.sin(ang) * sign], axis=0)
    return jnp.tile(cos, (1, 2)), jnp.tile(sin, (1, 2))


def _lane_consts():
    l = jnp.arange(512)
    seg = (l[:, None] // HD == l[None, :] // HD).astype(F32) / HD
    partner = jnp.where(l % 32 < 16, l + 16, l - 16)
    perm = (l[:, None] == partner[None, :]).astype(F32)
    return seg, perm


def _heads(a, n):
    return a.reshape(a.shape[0], n, HD).transpose(1, 0, 2)


def _unheads(a):
    return a.transpose(1, 0, 2).reshape(a.shape[1], a.shape[0] * HD)


def _gather_f32_shards(shards):
    sizes = [_prod(a.shape) for a in shards]
    flat = jnp.concatenate([a.reshape(-1) for a in shards] + [jnp.zeros((16 * PACK_C - sum(sizes),), F32)])
    got = _allgather_small("gather_f32_shards", flat.reshape(16, PACK_C)).reshape(N_CHIP, 2, 16 * PACK_C)[:, 0]
    out, o = {}, 0
    for (n, _, ax), a, sz in zip(_SHARDED_SMALL, shards, sizes):
        out[n] = jnp.concatenate([got[k, o:o + sz].reshape(a.shape) for k in range(N_CHIP)], axis=ax + 1)
        o += sz
    return out


def _layer_params(l, wfull, small):
    w2 = small["w_alpha2_full"][l]
    w2pad = jnp.zeros((128, 512), F32).at[0:16, 0:256].set(w2[0]).at[16:32, 256:512].set(w2[1])
    full = {n: _from_blocks(wfull[n][:, l], ax) for n, ax in _SHARDED}
    return dict(
        w_in=_to_new_cols(full["w_in"]), wa=full["w_br_a"], wb=full["w_br_b"], wc=full["w_br_c"],
        w_out=full["w_out"], w_up=full["w_ffn_up"], w_down=full["w_ffn_down"],
        cw=small["conv_w_full"][l], cb=small["conv_b"][l][None], w2=w2pad,
        b2=small["b_alpha_full"][l].reshape(1, 512),
        g1=small["norm1_g"][l][None], g2=small["norm2_g"][l][None], gq=jnp.tile(small["q_norm_g"][l], 8)[None],
        gk=jnp.tile(small["k_norm_g"][l], 2)[None], ggm=small["gmlp_norm_g"][l][None], ws=small["w_spatial"][l],
        bst=small["b_spatial"][l].T, ggl=small["gla_norm_g"][l][None])


def _layer_fwd(l, last, x, h1, mod, P, tabs):
    cos, sin, seg, perm = tabs
    n = "l%d_" % l
    s = dict(x=x, h1=h1)
    p = _mm(n + "in_proj", h1, P["w_in"], "nn", F32, tm_t=768, tn_t=2176)
    s["p"] = p
    s["gm"] = _gmlp_fwd(n + "gmlp", p, P["ggm"], P["ws"], P["bst"])
    qr, kr, vb = _qk_fwd(n + "qk_prep", p, P["gq"], P["gk"], cos, sin, seg, perm)
    qh, kh, vh = _heads(qr, NQ), _heads(kr, NKV), _heads(vb, NKV)
    s["qh"], s["kh"], s["vh"] = qh, kh, vh
    one_hot = (jnp.arange(HD) == 0).astype(BF16)
    v1 = jnp.concatenate([vh, jnp.broadcast_to(one_hot, vh.shape)], axis=-1)
    ox, lse_x = _attn_fwd(n + "attn_x", qh[:, TC:], kh, v1)
    s["ox"], s["lse_x"] = ox, lse_x
    if last:
        oc = jnp.zeros((NQ, TC, HD), F32)
    else:
        oc, lse_c = _attn_fwd(n + "attn_c", qh[:, :TC], kh[:, :TC], v1[:, :TC])
        s["oc"], s["lse_c"] = oc, lse_c
    s["att"] = _unheads(jnp.concatenate([oc, ox], axis=1)).astype(BF16)
    la = _decay_fwd(n + "gla_decay", p, P["w2"], P["b2"])
    s["la"] = la
    s["of"], s["sf"] = _gla_fwd(n + "gla_scan_f", p, la, False)
    s["ob"], s["sb"] = _gla_fwd(n + "gla_scan_b", p, la, True)
    s["gla"] = _gla_out_fwd(n + "gla_out", s["of"], s["ob"], p, P["ggl"])
    s["merged"] = _merge_fwd(n + "merge", s["gm"], s["att"], s["gla"], P["wa"], P["wb"], P["wc"], p)
    s["mix"] = _mm(n + "out_proj", s["merged"], P["w_out"], "nn", F32)
    s["x_mid"], s["h2"] = _res_nm_fwd(n + "res1_norm2", x, s["mix"], mod, 2, mod, P["g2"], 3, 4)
    s["a"] = _mm(n + "ffn_up", s["h2"], P["w_up"], "nn", F32)
    s["act"] = _conv_fwd(n + "conv_gate", s["a"], P["cw"], P["cb"])
    s["f"] = _mm(n + "ffn_down", s["act"], P["w_down"], "nn", F32)
    return s


def _layer_bwd(l, last, s, mod, P, tabs, dx_mid, df, gw):
    cos, sin, seg, perm = tabs
    n = "l%d_b_" % l
    t = dx_mid.shape[0]
    p = s["p"]
    gw["w_ffn_down"] = _mm(n + "ffn_down_w", s["act"], df, "tn", F32)
    dact = _mm(n + "ffn_down_x", df, P["w_down"], "nt", F32)
    dconv = _conv_bwd_gate(n + "conv_gate", s["a"], P["cw"], P["cb"], dact)
    da, gw["conv_w"], dcb = _conv_bwd_in(n + "conv_in", s["a"], P["cw"], dconv)
    gw["conv_b"] = dcb[0]
    gw["w_ffn_up"] = _mm(n + "ffn_up_w", s["h2"], da, "tn", F32)
    dh2 = _mm(n + "ffn_up_x", da, P["w_up"], "nt", F32)
    dx, dmix, dmod_a, dmod_b, dg2 = _res_nm_bwd(n + "res1_norm2", s["x"], s["mix"], mod, 2, mod, P["g2"], 3, 4, dx_mid, dh2)
    dmod = dmod_a + dmod_b
    gw["norm2_g"] = dg2[0]
    gw["w_out"] = _mm(n + "out_proj_w", s["merged"], dmix, "tn", F32)
    dmerged = _mm(n + "out_proj_x", dmix, P["w_out"], "nt", F32)
    dya, dyb, dyc, dga, dgb, dgc = _merge_bwd(n + "merge", s["gm"], s["att"], s["gla"], P["wa"], P["wb"], P["wc"], p, dmerged)
    gw["w_br_a"] = _mm(n + "br_a_w", s["gm"], dya, "tn", F32)
    gw["w_br_b"] = _mm(n + "br_b_w", s["att"], dyb, "tn", F32)
    gw["w_br_c"] = _mm(n + "br_c_w", s["gla"], dyc, "tn", F32)
    dgm = _mm(n + "br_a_x", dya, P["wa"], "nt", F32)
    datt = _mm(n + "br_b_x", dyb, P["wb"], "nt", F32)
    dgla = _mm(n + "br_c_x", dyc, P["wc"], "nt", F32)
    du, dv_g, dggm, dws, dbst = _gmlp_bwd(n + "gmlp", p, P["ggm"], P["ws"], P["bst"], dgm)
    gw["gmlp_norm_g"], gw["w_spatial"], gw["b_spatial"] = dggm[0], dws, dbst.T
    doh = _heads(datt, NQ)
    qh, kh, vh = s["qh"], s["kh"], s["vh"]
    row = lambda a: a.reshape(a.shape[0], 1, a.shape[1])
    dqx, dkh, dvh = _attn_bwd(n + "attn_x", qh[:, TC:], kh, vh, s["ox"], doh[:, TC:], row(s["lse_x"]))
    if last:
        dqc = jnp.zeros((NQ, TC, HD), F32)
    else:
        dqc, dkc, dvc = _attn_bwd(n + "attn_c", qh[:, :TC], kh[:, :TC], vh[:, :TC], s["oc"], doh[:, :TC], row(s["lse_c"]))
        pad = jnp.zeros((NKV, t - TC, HD), F32)
        dkh = dkh + jnp.concatenate([dkc, pad], axis=1)
        dvh = dvh + jnp.concatenate([dvc, pad], axis=1)
    dqr = _unheads(jnp.concatenate([dqc, dqx], axis=1))
    dq, dk, dgq, dgk = _qk_bwd(n + "qk_prep", p, P["gq"], P["gk"], cos, sin, seg, perm, dqr, _unheads(dkh))
    gw["q_norm_g"], gw["k_norm_g"] = dgq.reshape(8, HD).sum(0), dgk.reshape(2, HD).sum(0)
    dv_att = _unheads(dvh).astype(BF16)
    do, dr, dggl = _gla_out_bwd(n + "gla_out", s["of"], s["ob"], p, P["ggl"], dgla)
    gw["gla_norm_g"] = dggl[0]
    dq_f, dk_f, dv_f, dla_f = _gla_bwd(n + "gla_scan_f", p, s["la"], s["sf"], do, False)
    dglq, dglk, dglv, dla_b = _gla_bwd(n + "gla_scan_b", p, s["la"], s["sb"], do, True, prev=(dq_f, dk_f, dv_f))
    dab, dw2, db2 = _decay_bwd(n + "gla_decay", p, P["w2"], P["b2"], dla_f, dla_b)
    gw["w_alpha2"] = jnp.stack([dw2[0:16, 0:256], dw2[16:32, 256:512]])
    gw["b_alpha"] = db2.reshape(2, 256)
    bf = lambda a: a.astype(BF16)
    dp = jnp.concatenate([dga, dgb, dgc, du, dv_g, dq, bf(dglv), dr, bf(dglq), bf(dglk), dk, dv_att, dab], axis=-1)
    gw["w_in"] = _to_ref_cols(_mm(n + "in_proj_w", s["h1"], dp, "tn", F32, tn_t=2176, tk_t=768))
    dh1 = _mm(n + "in_proj_x", dp, P["w_in"], "nt", F32, tk_t=2176)
    return dx, dh1, dmod


_SMALL = (("norm1_g", (2, D)), ("norm2_g", (2, D)), ("q_norm_g", (2, HD)), ("k_norm_g", (2, HD)), ("gmlp_norm_g", (2, GW)),
          ("gla_norm_g", (2, GLA_V)), ("w_spatial", (2, 4, 128, 128)), ("b_spatial", (2, 4, 128)), ("conv_b", (2, 2 * FFN)),
          ("final_norm_g", (D,))) + tuple((n, (2,) + s) for n, s, _ in _SHARDED_SMALL)
_SMALL_N = 2 * 2 * ADA_W + sum(_prod(s) for _, s in _SMALL)
_SMALL_R = -(-_SMALL_N // (PACK_C * 8)) * 8


def _mod_tables(c, c_ctx, w_ada, b_ada, k):
    x, y, cc = _coords()
    me = 4 * x + 2 * y + cc
    c_all = _allgather_small("gather_c", jnp.concatenate([c, jnp.zeros((7, D), F32)], axis=0))
    c8 = c_all.reshape(N_DEV, 8, D)[:, 0]
    cond = jnp.concatenate([c8, c_ctx[None], jnp.zeros((7, D), F32)], axis=0)
    b_loc = lax.dynamic_slice_in_dim(b_ada, k * ADA_LOC, ADA_LOC, axis=1)[:, None, :]
    m_loc = _adaln_fwd("adaln", cond, w_ada, b_loc)
    m_all = _allgather_small("gather_mod", m_loc.reshape(32, ADA_LOC)).reshape(N_CHIP, 2, 2, 16, ADA_LOC)[:, 0]
    m_all = m_all.transpose(1, 2, 0, 3).reshape(2, 16, ADA_W)
    rows = jnp.stack([m_all[:, 8], lax.dynamic_index_in_dim(m_all, me, axis=1, keepdims=False)], axis=1)
    return rows.reshape(2, 2, 6, D), c8


def _step(x, c, ctx, c_ctx, W, tgt):
    xc, yc, cc = _coords()
    k = 2 * xc + yc
    tx = x.shape[0]
    t = TC + tx
    small = {n: W[n] for n, _ in _SMALL}
    for n, a in _gather_f32_shards([W[n] for n, _, _ in _SHARDED_SMALL]).items():
        small[n + "_full"] = a

    gathered = _allgather_layers("gather_weights", [W[n].astype(BF16) for n, _ in _SHARDED])
    wfull = {n: a for (n, _), a in zip(_SHARDED, gathered)}
    mods, c8 = _mod_tables(c, c_ctx, W["w_ada"], W["b_ada"], k)
    tabs = _rope_tables(tx) + _lane_consts()
    params = [_layer_params(l, wfull, small) for l in range(2)]

    xs = jnp.concatenate([ctx, x], axis=0)
    h1 = _nm_fwd("l0_norm1", xs, mods[0], params[0]["g1"], 0, 1)
    s0 = _layer_fwd(0, False, xs, h1, mods[0], params[0], tabs)
    x1, h1b = _res_nm_fwd("l0_res2_norm1", s0["x_mid"], s0["f"], mods[0], 5, mods[1], params[1]["g1"], 0, 1)
    s1 = _layer_fwd(1, True, x1, h1b, mods[1], params[1], tabs)
    loss, dxm_l, df_l, dmod_head, dgf = _head("head", s1["x_mid"], s1["f"], mods[1], W["final_norm_g"][None], tgt)

    gws = [dict(), dict()]
    zc = lambda dt: jnp.zeros((TC, D), dt)
    dx1, dh1b, dmod1 = _layer_bwd(1, True, s1, mods[1], params[1], tabs, jnp.concatenate([zc(F32), dxm_l]),
                                  jnp.concatenate([zc(BF16), df_l]), gws[1])
    dxm0, df0, dmod0_g, dmod1_s, dg1b = _res_nm_bwd("l0_b_res2_norm1", s0["x_mid"], s0["f"], mods[0], 5, mods[1],
                                                    params[1]["g1"], 0, 1, dx1, dh1b)
    gws[1]["norm1_g"] = dg1b[0]
    dx0, dh1, dmod0 = _layer_bwd(0, False, s0, mods[0], params[0], tabs, dxm0, df0, gws[0])
    dxs, dmod0_s, dg1 = _nm_bwd("l0_b_norm1", xs, mods[0], params[0]["g1"], 0, 1, dx0, dh1)
    gws[0]["norm1_g"] = dg1[0]
    grad_x = dxs[TC:]
    dmods = jnp.stack([dmod0 + dmod0_g + dmod0_s, dmod1 + dmod1_s + dmod_head])

    stk = {n: jnp.stack([gws[0][n], gws[1][n]]) for n, _ in _SMALL if n != "final_norm_g"}
    stk["final_norm_g"] = dgf[0]
    flat = jnp.concatenate([dmods.reshape(-1)] + [stk[n].reshape(-1) for n, _ in _SMALL])
    flat = jnp.concatenate([flat, jnp.zeros((_SMALL_R * PACK_C - _SMALL_N,), F32)]).reshape(_SMALL_R, PACK_C)
    every = _allgather_small("gather_small_grads", flat).reshape(N_DEV, _SMALL_R, PACK_C)
    tot = _sum_slots("sum_small_grads", every).reshape(-1)
    grads, o = {}, 2 * 2 * ADA_W
    for n, shp in _SMALL:
        grads[n] = tot[o:o + _prod(shp)].reshape(shp)
        o += _prod(shp)
    grads["b_ada"] = tot[:2 * 2 * ADA_W].reshape(2, 2, ADA_W).sum(axis=1)

    dm_every = every[:, :2 * 2 * ADA_W // PACK_C].reshape(N_DEV, 2, 2, ADA_W)
    dm_loc = lax.dynamic_slice_in_dim(dm_every, k * ADA_LOC, ADA_LOC, axis=3).transpose(1, 2, 0, 3)
    cc8 = jnp.concatenate([c_ctx[None], jnp.zeros((7, D), F32)], axis=0)
    grads["w_ada"], dcc = _adaln_bwd("adaln_b", c8, cc8, W["w_ada"], dm_loc[:, 1], dm_loc[:, 0])
    dcc_every = _allgather_small("gather_dcctx", dcc * 0.5).reshape(N_DEV, 8, D)
    grads["c_ctx"] = _sum_slots("sum_dcctx", dcc_every)[0]

    for n, shp, ax in _SHARDED_SMALL:
        grads[n] = lax.dynamic_slice_in_dim(grads[n], k * (shp[ax] // N_CHIP), shp[ax] // N_CHIP, axis=ax + 1)
    red = _reduce_scatter(*[[_to_blocks(gws[l][n], ax) for n, ax in _SHARDED] for l in range(2)])
    grads.update({n: a for (n, _), a in zip(_SHARDED, red)})
    return loss[0, 0], grad_x, grads


_WEIGHTS = ("c_ctx", "w_ada", "b_ada", "norm1_g", "norm2_g", "w_in", "q_norm_g", "k_norm_g", "gmlp_norm_g", "w_spatial",
            "b_spatial", "w_alpha2", "b_alpha", "gla_norm_g", "w_br_a", "w_br_b", "w_br_c", "w_out", "w_ffn_up", "conv_w",
            "conv_b", "w_ffn_down", "final_norm_g")
_BIG = ("w_ada", "w_in", "w_br_a", "w_br_b", "w_br_c", "w_out", "w_ffn_up", "w_ffn_down")


def _update(W, G, M, V):
    delta, new_m, new_v = {}, {}, {}
    for n in _BIG:
        shp = W[n].shape
        two = lambda a: a.reshape(-1, shp[-1])
        d, m, v = _adamw("adamw_" + n, two(W[n]), two(G[n]), two(M[n]), two(V[n]))
        delta[n], new_m[n], new_v[n] = d.reshape(shp), m.reshape(shp), v.reshape(shp)
    rest = [n for n in _WEIGHTS if n not in _BIG]
    tot = sum(_prod(W[n].shape) for n in rest)
    rows = -(-tot // (PACK_C * 8)) * 8

    def cat(dct):
        flat = jnp.concatenate([dct[n].reshape(-1) for n in rest] + [jnp.zeros((rows * PACK_C - tot,), F32)])
        return flat.reshape(rows, PACK_C)

    outs = _adamw("adamw_small", cat(W), cat(G), cat(M), cat(V))
    o = 0
    for n in rest:
        sz, shp = _prod(W[n].shape), W[n].shape
        delta[n], new_m[n], new_v[n] = (a.reshape(-1)[o:o + sz].reshape(shp) for a in outs)
        o += sz
    return delta, new_m, new_v


def kernel(x, c, ctx, c_ctx, w_ada, b_ada, norm1_g, norm2_g, w_in, q_norm_g, k_norm_g, gmlp_norm_g, w_spatial, b_spatial, w_alpha2, b_alpha, gla_norm_g, w_br_a, w_br_b, w_br_c, w_out, w_ffn_up, conv_w, conv_b, w_ffn_down, final_norm_g, loss_target, m_c_ctx, m_w_ada, m_b_ada, m_norm1_g, m_norm2_g, m_w_in, m_q_norm_g, m_k_norm_g, m_gmlp_norm_g, m_w_spatial, m_b_spatial, m_w_alpha2, m_b_alpha, m_gla_norm_g, m_w_br_a, m_w_br_b, m_w_br_c, m_w_out, m_w_ffn_up, m_conv_w, m_conv_b, m_w_ffn_down, m_final_norm_g, v_c_ctx, v_w_ada, v_b_ada, v_norm1_g, v_norm2_g, v_w_in, v_q_norm_g, v_k_norm_g, v_gmlp_norm_g, v_w_spatial, v_b_spatial, v_w_alpha2, v_b_alpha, v_gla_norm_g, v_w_br_a, v_w_br_b, v_w_br_c, v_w_out, v_w_ffn_up, v_conv_w, v_conv_b, v_w_ffn_down, v_final_norm_g):
    W = dict(c_ctx=c_ctx, w_ada=w_ada, b_ada=b_ada, norm1_g=norm1_g, norm2_g=norm2_g, w_in=w_in, q_norm_g=q_norm_g,
             k_norm_g=k_norm_g, gmlp_norm_g=gmlp_norm_g, w_spatial=w_spatial, b_spatial=b_spatial, w_alpha2=w_alpha2,
             b_alpha=b_alpha, gla_norm_g=gla_norm_g, w_br_a=w_br_a, w_br_b=w_br_b, w_br_c=w_br_c, w_out=w_out,
             w_ffn_up=w_ffn_up, conv_w=conv_w, conv_b=conv_b, w_ffn_down=w_ffn_down, final_norm_g=final_norm_g)
    M = dict(c_ctx=m_c_ctx, w_ada=m_w_ada, b_ada=m_b_ada, norm1_g=m_norm1_g, norm2_g=m_norm2_g, w_in=m_w_in,
             q_norm_g=m_q_norm_g, k_norm_g=m_k_norm_g, gmlp_norm_g=m_gmlp_norm_g, w_spatial=m_w_spatial,
             b_spatial=m_b_spatial, w_alpha2=m_w_alpha2, b_alpha=m_b_alpha, gla_norm_g=m_gla_norm_g, w_br_a=m_w_br_a,
             w_br_b=m_w_br_b, w_br_c=m_w_br_c, w_out=m_w_out, w_ffn_up=m_w_ffn_up, conv_w=m_conv_w, conv_b=m_conv_b,
             w_ffn_down=m_w_ffn_down, final_norm_g=m_final_norm_g)
    V = dict(c_ctx=v_c_ctx, w_ada=v_w_ada, b_ada=v_b_ada, norm1_g=v_norm1_g, norm2_g=v_norm2_g, w_in=v_w_in,
             q_norm_g=v_q_norm_g, k_norm_g=v_k_norm_g, gmlp_norm_g=v_gmlp_norm_g, w_spatial=v_w_spatial,
             b_spatial=v_b_spatial, w_alpha2=v_w_alpha2, b_alpha=v_b_alpha, gla_norm_g=v_gla_norm_g, w_br_a=v_w_br_a,
             w_br_b=v_w_br_b, w_br_c=v_w_br_c, w_out=v_w_out, w_ffn_up=v_w_ffn_up, conv_w=v_conv_w, conv_b=v_conv_b,
             w_ffn_down=v_w_ffn_down, final_norm_g=v_final_norm_g)
    loss_local, grad_x, G = _step(x[0], c, ctx[0], c_ctx, W, loss_target[0])
    loss = lax.psum(loss_local, ("x", "y", "c"))
    delta, new_m, new_v = _update(W, G, M, V)
    return (loss, grad_x[None], *[G[n] for n in _WEIGHTS], *[delta[n] for n in _WEIGHTS],
            *[new_m[n] for n in _WEIGHTS], *[new_v[n] for n in _WEIGHTS])
```

```python
import functools

import jax
import jax.numpy as jnp
from jax import lax
from jax.experimental import pallas as pl
from jax.experimental.pallas import tpu as pltpu

F32 = jnp.float32
BF16 = jnp.bfloat16

D = 1024
TC = 256
GRID_W = 64
EPS = 1e-6
HD = 64
NQ = 8
NKV = 2
QG = NQ // NKV
GLA_H = 4
GLA_DK = 64
GLA_DV = 128
GLA_QK = 256
GLA_V = 512
GLA_CHUNK = 64
GLA_TAU = 16.0
GW = 512
FFN = 2816
IN_W = 6432
PW = 6528
ADA_W = 6 * D
N_CHIP = 4
N_DEV = 8
ADA_LOC = ADA_W // N_CHIP

ADAM_LR = 0.001
ADAM_B1 = 0.9
ADAM_B2 = 0.999
ADAM_EPS = 1e-08
ADAM_WD = 0.01
ADAM_STEP = 10

TM = 256
NCB = TC // TM
LANE = 128
VMEM_LIMIT = 48 * 1024 * 1024
MESH = pl.DeviceIdType.MESH

_COLS = (("gA", 3360, 1024), ("gB", 4384, 1024), ("gC", 5408, 1024), ("gu", 0, 512), ("gv", 512, 512),
         ("q", 1024, 512), ("glv", 2304, 512), ("gr", 2848, 512), ("glq", 1792, 256), ("glk", 2048, 256),
         ("k", 1536, 128), ("v", 1664, 128), ("ab", 2816, 32))
OFF = {}
_o = 0
for _n, _s, _w in _COLS:
    OFF[_n] = _o
    _o += max(_w, LANE)
assert _o == PW


def _to_new_cols(w):
    parts = [w[..., s:s + n] for _, s, n in _COLS]
    pad = jnp.zeros(w.shape[:-1] + (PW - IN_W,), w.dtype)
    return jnp.concatenate(parts + [pad], axis=-1)


def _to_ref_cols(w):
    by_start = sorted(_COLS, key=lambda t: t[1])
    return jnp.concatenate([w[..., OFF[n]:OFF[n] + wd] for n, _, wd in by_start], axis=-1)


def _tile(n, target, align=LANE):
    best = None
    t = align
    while t <= min(n, target):
        if n % t == 0:
            best = t
        t += align
    assert best is not None, (n, target, align)
    return best


def _cp(sem=None):
    return pltpu.CompilerParams(dimension_semantics=sem, vmem_limit_bytes=VMEM_LIMIT)


def _bdot_impl(a, b, ca, cb):
    return lax.dot_general(a.astype(BF16), b.astype(BF16), (((ca,), (cb,)), ((), ())),
                           preferred_element_type=F32)


@functools.partial(jax.custom_vjp, nondiff_argnums=(2, 3))
def bdot(a, b, ca, cb):
    return _bdot_impl(a, b, ca, cb)


def _bdot_fwd(a, b, ca, cb):
    return _bdot_impl(a, b, ca, cb), (a, b)


def _bdot_bwd(ca, cb, res, g):
    a, b = res
    da = bdot(g, b, 1, 1 - cb) if ca == 1 else bdot(b, g, 1 - cb, 1)
    db = bdot(a, g, 1 - ca, 0) if cb == 0 else bdot(g, a, 0, 1 - ca)
    return da.astype(a.dtype), db.astype(b.dtype)


bdot.defvjp(_bdot_fwd, _bdot_bwd)


def hdot(a, b, ca=1, cb=0):
    return lax.dot_general(a, b, (((ca,), (cb,)), ((), ())), precision=lax.Precision.HIGH,
                           preferred_element_type=F32)


def _rms(x, g):
    return x * lax.rsqrt(jnp.mean(x * x, axis=-1, keepdims=True) + EPS) * g


def _gelu(x):
    return 0.5 * x * (1.0 + jnp.tanh(0.7978845608028654 * (x + 0.044715 * (x * x * x))))


def _log_sigmoid(z):
    return jnp.minimum(z, 0.0) - jnp.log(1.0 + jnp.exp(-jnp.abs(z)))


def _sel(mod, is_lat, idx):
    return jnp.where(is_lat, mod[1, idx:idx + 1, :], mod[0, idx:idx + 1, :])


def _rows_call(name, fn, grid, ins, outs, acc_axes=None, sem=None):
    n_in = len(ins)
    flags = [o[2] for o in outs]
    if acc_axes is None:
        acc_axes = (len(grid) - 1,)

    def body(*refs):
        ids = tuple(pl.program_id(a) for a in range(len(grid)))
        res = fn(ids, *[r[...] for r in refs[:n_in]])
        for r, v, acc in zip(refs[n_in:], res, flags):
            if acc:
                first = functools.reduce(jnp.logical_and, [ids[a] == 0 for a in acc_axes])

                @pl.when(first)
                def _():
                    r[...] = jnp.zeros_like(r)
                r[...] += v.astype(r.dtype)
            else:
                r[...] = v.astype(r.dtype)

    return pl.pallas_call(
        body, name=name, grid=grid, in_specs=[s for _, s in ins], out_specs=[o[1] for o in outs],
        out_shape=[o[0] for o in outs],
        compiler_params=_cp(sem if sem is not None else ("arbitrary",) * len(grid)),
    )(*[a for a, _ in ins])


def _sds(shape, dtype):
    return jax.ShapeDtypeStruct(shape, dtype)


def _rowspec(width, off=0, tm=TM):
    assert off % width == 0
    return pl.BlockSpec((tm, width), lambda i, o=off // width: (i, o))


def _full(shape):
    nd = len(shape)
    return pl.BlockSpec(shape, lambda *a: (0,) * nd)


def _mm(name, a, b, mode, out_dtype, tm_t=1056, tn_t=1408, tk_t=1408):
    if mode == "nn":
        (m, k), (_, n) = a.shape, b.shape
    elif mode == "nt":
        (m, k), (n, _) = a.shape, b.shape
    else:
        (k, m), (_, n) = a.shape, b.shape
    tm = _tile(m, tm_t, 8 if m % LANE else LANE)
    tn = _tile(n, tn_t)
    tk = _tile(k, tk_t)
    nk = k // tk
    if mode == "nn":
        dims, a_spec, b_spec = ((1,), (0,)), pl.BlockSpec((tm, tk), lambda i, j, l: (i, l)), pl.BlockSpec((tk, tn), lambda i, j, l: (l, j))
    elif mode == "nt":
        dims, a_spec, b_spec = ((1,), (1,)), pl.BlockSpec((tm, tk), lambda i, j, l: (i, l)), pl.BlockSpec((tn, tk), lambda i, j, l: (j, l))
    else:
        dims, a_spec, b_spec = ((0,), (0,)), pl.BlockSpec((tk, tm), lambda i, j, l: (l, i)), pl.BlockSpec((tk, tn), lambda i, j, l: (l, j))

    def body(a_ref, b_ref, o_ref, *scratch):
        l = pl.program_id(2)
        part = lax.dot_general(a_ref[...].astype(BF16), b_ref[...].astype(BF16), (dims, ((), ())),
                               preferred_element_type=F32)
        if nk == 1:
            o_ref[...] = part.astype(o_ref.dtype)
            return
        acc_ref = scratch[0]

        @pl.when(l == 0)
        def _():
            acc_ref[...] = part

        @pl.when(l > 0)
        def _():
            acc_ref[...] += part

        @pl.when(l == nk - 1)
        def _():
            o_ref[...] = acc_ref[...].astype(o_ref.dtype)

    return pl.pallas_call(
        body, name=name, grid=(m // tm, n // tn, nk), in_specs=[a_spec, b_spec],
        out_specs=pl.BlockSpec((tm, tn), lambda i, j, l: (i, j)), out_shape=_sds((m, n), out_dtype),
        scratch_shapes=[pltpu.VMEM((tm, tn), F32)] if nk > 1 else [],
        compiler_params=_cp(("parallel", "parallel", "arbitrary")),
    )(a, b)


def _nm_fn(is_lat, x, mod, g, shift, scale):
    return _rms(x, g) * (1.0 + _sel(mod, is_lat, scale)) + _sel(mod, is_lat, shift)


def _res_nm_fn(is_lat, x, br, modg, gate, mods, g, shift, scale):
    xn = x + _sel(modg, is_lat, gate) * br
    return xn, _nm_fn(is_lat, xn, mods, g, shift, scale)


def _nm_fwd(name, x, mod, g, shift, scale):
    t = x.shape[0]
    fn = lambda ids, xv, mv, gv: (_nm_fn(ids[0] >= NCB, xv, mv, gv, shift, scale),)
    return _rows_call(name, fn, (t // TM,), [(x, _rowspec(D)), (mod, _full((2, 6, D))), (g, _full((1, D)))],
                      [(_sds((t, D), BF16), _rowspec(D), False)])[0]


def _nm_bwd(name, x, mod, g, shift, scale, dx_res, dh):
    t = x.shape[0]

    def fn(ids, xv, mv, gv, dxr, dhv):
        _, vjp = jax.vjp(lambda a, b, c: _nm_fn(ids[0] >= NCB, a, b, c, shift, scale), xv, mv, gv)
        dx, dm, dg = vjp(dhv)
        return dx + dxr, dm, dg

    return _rows_call(name, fn, (t // TM,),
                      [(x, _rowspec(D)), (mod, _full((2, 6, D))), (g, _full((1, D))), (dx_res, _rowspec(D)), (dh, _rowspec(D))],
                      [(_sds((t, D), F32), _rowspec(D), False), (_sds((2, 6, D), F32), _full((2, 6, D)), True),
                       (_sds((1, D), F32), _full((1, D)), True)])


def _res_nm_fwd(name, x, br, modg, gate, mods, g, shift, scale):
    t = x.shape[0]
    fn = lambda ids, xv, bv, mg, ms, gv: _res_nm_fn(ids[0] >= NCB, xv, bv, mg, gate, ms, gv, shift, scale)
    return _rows_call(name, fn, (t // TM,),
                      [(x, _rowspec(D)), (br, _rowspec(D)), (modg, _full((2, 6, D))), (mods, _full((2, 6, D))), (g, _full((1, D)))],
                      [(_sds((t, D), F32), _rowspec(D), False), (_sds((t, D), BF16), _rowspec(D), False)])


def _res_nm_bwd(name, x, br, modg, gate, mods, g, shift, scale, dx_res, dh):
    t = x.shape[0]

    def fn(ids, xv, bv, mg, ms, gv, dxr, dhv):
        f = lambda a, b, c, d, e: _res_nm_fn(ids[0] >= NCB, a, b, c, gate, d, e, shift, scale)
        _, vjp = jax.vjp(f, xv, bv, mg, ms, gv)
        return vjp((dxr, dhv))

    m26 = (_sds((2, 6, D), F32), _full((2, 6, D)), True)
    return _rows_call(name, fn, (t // TM,),
                      [(x, _rowspec(D)), (br, _rowspec(D)), (modg, _full((2, 6, D))), (mods, _full((2, 6, D))), (g, _full((1, D))),
                       (dx_res, _rowspec(D)), (dh, _rowspec(D))],
                      [(_sds((t, D), F32), _rowspec(D), False), (_sds((t, D), BF16), _rowspec(D), False), m26, m26,
                       (_sds((1, D), F32), _full((1, D)), True)])


def _head(name, x_mid, f, mod, gf, tgt):
    tx = tgt.shape[0]

    def fn(ids, xv, fv, mv, gv, tv):
        def loss_fn(a, b, c, d):
            y = _rms(a + c[1, 5:6, :] * b, d)
            e = y - tv
            return 0.5 * jnp.sum(jnp.mean(e * e, axis=-1))
        loss, grads = jax.value_and_grad(loss_fn, argnums=(0, 1, 2, 3))(xv, fv, mv, gv)
        return (jnp.reshape(loss, (1, 1)),) + grads

    lat = pl.BlockSpec((TM, D), lambda i: (i + NCB, 0))
    return _rows_call(name, fn, (tx // TM,),
                      [(x_mid, lat), (f, lat), (mod, _full((2, 6, D))), (gf, _full((1, D))), (tgt, _rowspec(D))],
                      [(_sds((1, 1), F32), _full((1, 1)), True), (_sds((tx, D), F32), _rowspec(D), False),
                       (_sds((tx, D), BF16), _rowspec(D), False), (_sds((2, 6, D), F32), _full((2, 6, D)), True),
                       (_sds((1, D), F32), _full((1, D)), True)])


def _gmlp_fn(u, v, g, ws, bst):
    rows = []
    for r in range(u.shape[0] // 128):
        uu, vv = _gelu(u[128 * r:128 * r + 128]), _gelu(v[128 * r:128 * r + 128])
        cols = []
        for gi in range(4):
            sl = slice(128 * gi, 128 * gi + 128)
            f = bdot(ws[gi], _rms(vv[:, sl], g[:, sl]), 1, 0) + bst[:, gi:gi + 1]
            cols.append(uu[:, sl] * f)
        rows.append(jnp.concatenate(cols, axis=-1))
    return jnp.concatenate(rows, axis=0)


def _gmlp_ins(p, g, ws, bst):
    return [(p, _rowspec(GW, OFF["gu"])), (p, _rowspec(GW, OFF["gv"])), (g, _full((1, GW))),
            (ws, _full((4, 128, 128))), (bst, _full((128, 4)))]


def _gmlp_fwd(name, p, g, ws, bst):
    t = p.shape[0]
    return _rows_call(name, lambda ids, *a: (_gmlp_fn(*a),), (t // TM,), _gmlp_ins(p, g, ws, bst),
                      [(_sds((t, GW), BF16), _rowspec(GW), False)])[0]


def _gmlp_bwd(name, p, g, ws, bst, dgm):
    t = p.shape[0]

    def fn(ids, u, v, gv, wv, bv, dv):
        _, vjp = jax.vjp(_gmlp_fn, u, v, gv, wv, bv)
        return vjp(dv)

    return _rows_call(name, fn, (t // TM,), _gmlp_ins(p, g, ws, bst) + [(dgm, _rowspec(GW))],
                      [(_sds((t, GW), BF16), _rowspec(GW), False), (_sds((t, GW), BF16), _rowspec(GW), False),
                       (_sds((1, GW), F32), _full((1, GW)), True), (_sds((4, 128, 128), F32), _full((4, 128, 128)), True),
                       (_sds((128, 4), F32), _full((128, 4)), True)])


def _qk_fn(q, k, gq, gk, cos, sin, seg, perm):
    cq, sq = jnp.concatenate([cos] * 4, axis=-1), jnp.concatenate([sin] * 4, axis=-1)
    qn = q * lax.rsqrt(hdot(q * q, seg) + EPS) * gq
    kn = k * lax.rsqrt(hdot(k * k, seg[:128, :128]) + EPS) * gk
    qr = qn * cq + hdot(qn, perm) * sq
    kr = kn * cos + hdot(kn, perm[:128, :128]) * sin
    return qr * (HD ** -0.5), kr


def _qk_ins(p, gq, gk, cos, sin, seg, perm):
    return [(p, _rowspec(512, OFF["q"])), (p, _rowspec(128, OFF["k"])), (gq, _full((1, 512))), (gk, _full((1, 128))),
            (cos, _rowspec(128)), (sin, _rowspec(128)), (seg, _full((512, 512))), (perm, _full((512, 512)))]


def _qk_fwd(name, p, gq, gk, cos, sin, seg, perm):
    t = p.shape[0]
    fn = lambda ids, q, k, a, b, c, s, sg, pm, v: _qk_fn(q, k, a, b, c, s, sg, pm) + (v,)
    return _rows_call(name, fn, (t // TM,), _qk_ins(p, gq, gk, cos, sin, seg, perm) + [(p, _rowspec(128, OFF["v"]))],
                      [(_sds((t, 512), BF16), _rowspec(512), False), (_sds((t, 128), BF16), _rowspec(128), False),
                       (_sds((t, 128), BF16), _rowspec(128), False)])


def _qk_bwd(name, p, gq, gk, cos, sin, seg, perm, dqr, dkr):
    t = p.shape[0]

    def fn(ids, q, k, a, b, c, s, sg, pm, dq, dk):
        _, vjp = jax.vjp(lambda q_, k_, a_, b_: _qk_fn(q_, k_, a_, b_, c, s, sg, pm), q, k, a, b)
        return vjp((dq, dk))

    return _rows_call(name, fn, (t // TM,),
                      _qk_ins(p, gq, gk, cos, sin, seg, perm) + [(dqr, _rowspec(512)), (dkr, _rowspec(128))],
                      [(_sds((t, 512), BF16), _rowspec(512), False), (_sds((t, 128), BF16), _rowspec(128), False),
                       (_sds((1, 512), F32), _full((1, 512)), True), (_sds((1, 128), F32), _full((1, 128)), True)])


_ATT_TQ = 256
_ATT_TK = 768


def _attn_fwd(name, q, k, v):
    h, tq_all, _ = q.shape
    hkv, tk_all, _ = k.shape
    tq, tk = _tile(tq_all, _ATT_TQ), _tile(tk_all, _ATT_TK)
    nkc = tk_all // tk

    def body(q_ref, k_ref, v_ref, o_ref, lse_ref):
        qv = q_ref[...].reshape(QG * tq, HD)

        def step(j, carry):
            m, acc = carry
            off = pl.multiple_of(j * tk, tk)
            kk, vv = k_ref[0, pl.ds(off, tk), :], v_ref[0, pl.ds(off, tk), :]
            s = lax.dot_general(qv, kk, (((1,), (1,)), ((), ())), preferred_element_type=F32)
            m_new = jnp.maximum(m, jnp.max(s, axis=-1, keepdims=True))
            pr = jnp.exp(s - m_new)
            acc = jnp.exp(m - m_new) * acc + jnp.dot(pr.astype(BF16), vv, preferred_element_type=F32)
            return m_new, acc

        init = (jnp.full((QG * tq, 1), -jnp.inf, F32), jnp.zeros((QG * tq, 2 * HD), F32))
        m, acc = lax.fori_loop(0, nkc, step, init)
        l = acc[:, HD:HD + 1]
        o_ref[...] = (acc[:, :HD] / l).reshape(QG, tq, HD)
        lse_ref[...] = (m + jnp.log(l)).reshape(QG, tq, 1)

    kv_spec = pl.BlockSpec((1, tk_all, HD), lambda g, i: (g, 0, 0))
    v1_spec = pl.BlockSpec((1, tk_all, 2 * HD), lambda g, i: (g, 0, 0))
    qspec = pl.BlockSpec((QG, tq, HD), lambda g, i: (g, i, 0))
    return pl.pallas_call(
        body, name=name, grid=(hkv, tq_all // tq), in_specs=[qspec, kv_spec, v1_spec],
        out_specs=[qspec, pl.BlockSpec((QG, tq, 1), lambda g, i: (g, i, 0))],
        out_shape=[_sds((h, tq_all, HD), F32), _sds((h, tq_all, 1), F32)],
        compiler_params=_cp(("parallel", "parallel")),
    )(q, k, v)


def _attn_bwd(name, q, k, v, o, do, lse_row):
    h, tq_all, _ = q.shape
    hkv, tk_all, _ = k.shape
    tq, tk = _tile(tq_all, 1024), _tile(tk_all, 768)

    def body(q_ref, k_ref, v_ref, o_ref, do_ref, lse_ref, dq_ref, dk_ref, dv_ref, dl_ref):
        i, j = pl.program_id(1), pl.program_id(2)

        @pl.when(j == 0)
        def _():
            ones = jnp.ones((8, HD), F32)
            for g in range(QG):
                dl_ref[g] = hdot(ones, do_ref[g] * o_ref[g], 1, 1)

        kk, vv = k_ref[0], v_ref[0]
        dk_acc, dv_acc = jnp.zeros((tk, HD), F32), jnp.zeros((tk, HD), F32)
        for g in range(QG):
            qv, dob = q_ref[g], do_ref[g].astype(BF16)
            st = lax.dot_general(kk, qv, (((1,), (1,)), ((), ())), preferred_element_type=F32)
            pt = jnp.exp(st - lse_ref[g])
            dv_acc += jnp.dot(pt.astype(BF16), dob, preferred_element_type=F32)
            dpt = lax.dot_general(vv, dob, (((1,), (1,)), ((), ())), preferred_element_type=F32)
            dst = (pt * (dpt - dl_ref[g, 0:1, :])).astype(BF16)
            dk_acc += jnp.dot(dst, qv, preferred_element_type=F32)
            dq_part = lax.dot_general(dst, kk, (((0,), (0,)), ((), ())), preferred_element_type=F32)

            @pl.when(j == 0)
            def _():
                dq_ref[g] = dq_part

            @pl.when(j > 0)
            def _():
                dq_ref[g] += dq_part

        rows = pl.ds(pl.multiple_of(j * tk, tk), tk)

        @pl.when(i == 0)
        def _():
            dk_ref[0, rows, :] = dk_acc
            dv_ref[0, rows, :] = dv_acc

        @pl.when(i > 0)
        def _():
            dk_ref[0, rows, :] += dk_acc
            dv_ref[0, rows, :] += dv_acc

    ks = pl.BlockSpec((1, tk, HD), lambda g, i, j: (g, j, 0))
    qs = pl.BlockSpec((QG, tq, HD), lambda g, i, j: (g, i, 0))
    rs = pl.BlockSpec((QG, 1, tq), lambda g, i, j: (g, 0, i))
    full = pl.BlockSpec((1, tk_all, HD), lambda g, i, j: (g, 0, 0))
    return pl.pallas_call(
        body, name=name, grid=(hkv, tq_all // tq, tk_all // tk), in_specs=[qs, ks, ks, qs, qs, rs], out_specs=[qs, full, full],
        out_shape=[_sds((h, tq_all, HD), F32), _sds((hkv, tk_all, HD), F32), _sds((hkv, tk_all, HD), F32)],
        scratch_shapes=[pltpu.VMEM((QG, 8, tq), F32)],
        compiler_params=_cp(("parallel", "arbitrary", "arbitrary")),
    )(q, k, v, o, do, lse_row)


def _decay_fn(a, w2, b2):
    return _log_sigmoid(bdot(a, w2, 1, 0) + b2) / GLA_TAU


def _decay_fwd(name, p, w2, b2):
    t = p.shape[0]
    return _rows_call(name, lambda ids, a, w, b: (_decay_fn(a, w, b),), (t // TM,),
                      [(p, _rowspec(128, OFF["ab"])), (w2, _full((128, 512))), (b2, _full((1, 512)))],
                      [(_sds((t, 512), F32), _rowspec(512), False)])[0]


def _decay_bwd(name, p, w2, b2, gf, gb):
    t = p.shape[0]

    def fn(ids, a, w, b, qf, kf, vf, lf, qb, kb, vb, lb):
        _, vjp = jax.vjp(_decay_fn, a, w, b)
        return vjp(jnp.concatenate([lf, lb], axis=-1)) + (qf + qb, kf + kb, vf + vb)

    widths = (256, 256, 512, 256)
    return _rows_call(name, fn, (t // TM,),
                      [(p, _rowspec(128, OFF["ab"])), (w2, _full((128, 512))), (b2, _full((1, 512)))]
                      + [(g, _rowspec(w)) for g, w in zip(gf, widths)] + [(g, _rowspec(w)) for g, w in zip(gb, widths)],
                      [(_sds((t, 128), BF16), _rowspec(128), False), (_sds((128, 512), F32), _full((128, 512)), True),
                       (_sds((1, 512), F32), _full((1, 512)), True)]
                      + [(_sds((t, w), BF16), _rowspec(w), False) for w in widths[:3]])


def _gla_consts(reverse):
    r = lax.broadcasted_iota(jnp.int32, (GLA_CHUNK, GLA_CHUNK), 0)
    c = lax.broadcasted_iota(jnp.int32, (GLA_CHUNK, GLA_CHUNK), 1)
    trib = (r <= c) if reverse else (r >= c)
    br = lax.broadcasted_iota(jnp.int32, (GLA_QK, GLA_V), 0) // GLA_DK
    bc = lax.broadcasted_iota(jnp.int32, (GLA_QK, GLA_V), 1) // GLA_DV
    lane_head = lax.broadcasted_iota(jnp.int32, (1, GLA_QK), 1) // GLA_DK
    return trib, (br == bc).astype(F32), lane_head


def _gla_chunk(q, k, v, la, s_in, consts):
    trib, bd, lane_head = consts
    cum = hdot(trib.astype(F32), la)
    tot = jnp.sum(la, axis=0, keepdims=True)
    q_in = q * (GLA_DK ** -0.5) * jnp.exp(cum)
    k_in = k * jnp.exp(-cum)
    k_st = k * jnp.exp(tot - cum)
    outs = []
    for h in range(GLA_H):
        att = bdot(jnp.where(lane_head == h, q_in, 0.0), k_in, 1, 1)
        att = jnp.where(trib, att, 0.0)
        outs.append(bdot(att, v[:, GLA_DV * h:GLA_DV * (h + 1)], 1, 0))
    o = jnp.concatenate(outs, axis=-1) + bdot(q_in, s_in, 1, 0)
    decay = jnp.exp(hdot(la, jnp.ones((GLA_CHUNK, LANE), F32), 0, 0))
    s_out = jnp.concatenate([decay] * (GLA_V // LANE), axis=-1) * s_in + bdot(k_st, v, 0, 0) * bd
    return o, s_out


def _gla_order(nb, reverse, backward):
    if not reverse:
        return (lambda s: nb - 1 - s) if backward else (lambda s: s)
    if backward:
        return lambda s: jnp.where(s == nb - 1, 0, s + 1)
    return lambda s: jnp.where(s == 0, 0, nb - s)


_NCH = TM // GLA_CHUNK


def _gla_specs(nb, reverse, backward):
    order = _gla_order(nb, reverse, backward)
    col = lambda width, off: pl.BlockSpec((TM, width), lambda s, o=off // width: (order(s), o))
    state = pl.BlockSpec((_NCH, GLA_H, GLA_DK, GLA_DV), lambda s: (order(s), 0, 0, 0))
    qkvla = [col(256, OFF["glq"]), col(256, OFF["glk"]), col(512, OFF["glv"]), col(256, 256 * int(reverse))]
    return col, state, qkvla


def _gla_fwd(name, p, la):
    t = p.shape[0]
    nb = t // TM

    def body(*refs):
        ins, outs, scr = (refs[0:4], refs[4:8]), (refs[8:10], refs[10:12]), refs[12:14]

        @pl.when(pl.program_id(0) == 0)
        def _():
            for s_ref in scr:
                s_ref[...] = jnp.zeros_like(s_ref)

        for step in range(_NCH):
            for d in range(2):
                (q_ref, k_ref, v_ref, la_ref), (o_ref, sv_ref), s_ref = ins[d], outs[d], scr[d]
                c = _NCH - 1 - step if d else step
                rows = slice(GLA_CHUNK * c, GLA_CHUNK * (c + 1))
                s_in = s_ref[...]
                for h in range(GLA_H):
                    sv_ref[c, h] = s_in[GLA_DK * h:GLA_DK * (h + 1), GLA_DV * h:GLA_DV * (h + 1)]
                o, s_out = _gla_chunk(q_ref[rows, :], k_ref[rows, :], v_ref[rows, :], la_ref[rows, :], s_in,
                                      _gla_consts(bool(d)))
                o_ref[rows, :] = o
                s_ref[...] = s_out

    in_specs, out_specs, out_shape = [], [], []
    for d in range(2):
        col, state, qkvla = _gla_specs(nb, bool(d), False)
        in_specs += qkvla
        out_specs += [col(512, 0), state]
        out_shape += [_sds((t, GLA_V), F32), _sds((t // GLA_CHUNK, GLA_H, GLA_DK, GLA_DV), F32)]
    return pl.pallas_call(
        body, name=name, grid=(nb,), in_specs=in_specs, out_specs=out_specs, out_shape=out_shape,
        scratch_shapes=[pltpu.VMEM((GLA_QK, GLA_V), F32)] * 2, compiler_params=_cp(("arbitrary",)),
    )(p, p, p, la, p, p, p, la)


def _gla_bwd(name, p, la, sv_f, sv_b, do):
    t = p.shape[0]
    nb = t // TM

    def body(*refs):
        ins, outs, scr = (refs[0:6], refs[6:12]), (refs[12:16], refs[16:20]), refs[20:22]

        @pl.when(pl.program_id(0) == 0)
        def _():
            for ds_ref in scr:
                ds_ref[...] = jnp.zeros_like(ds_ref)

        zero = jnp.zeros((GLA_DK, GLA_DV), F32)
        for step in range(_NCH):
            for d in range(2):
                (q_ref, k_ref, v_ref, la_ref, sv_ref, do_ref), out_refs, ds_ref = ins[d], outs[d], scr[d]
                c = step if d else _NCH - 1 - step
                rows = slice(GLA_CHUNK * c, GLA_CHUNK * (c + 1))
                s_in = jnp.concatenate(
                    [jnp.concatenate([sv_ref[c, h] if hh == h else zero for hh in range(GLA_H)], axis=-1)
                     for h in range(GLA_H)], axis=0)
                consts = _gla_consts(bool(d))
                _, vjp = jax.vjp(lambda a, b, cc, dd, e: _gla_chunk(a, b, cc, dd, e, consts),
                                 q_ref[rows, :], k_ref[rows, :], v_ref[rows, :], la_ref[rows, :], s_in)
                grads = vjp((do_ref[rows, :], ds_ref[...]))
                for o_ref, g in zip(out_refs, grads[:4]):
                    o_ref[rows, :] = g
                ds_ref[...] = grads[4]

    ins, in_specs, out_specs, out_shape = [], [], [], []
    for d, sv in enumerate((sv_f, sv_b)):
        col, state, qkvla = _gla_specs(nb, bool(d), True)
        ins += [p, p, p, la, sv, do]
        in_specs += qkvla + [state, col(512, 0)]
        out_specs += [col(256, 0), col(256, 0), col(512, 0), col(256, 0)]
        out_shape += [_sds((t, GLA_QK), F32), _sds((t, GLA_QK), F32), _sds((t, GLA_V), F32), _sds((t, GLA_QK), F32)]
    return pl.pallas_call(
        body, name=name, grid=(nb,), in_specs=in_specs, out_specs=out_specs, out_shape=out_shape,
        scratch_shapes=[pltpu.VMEM((GLA_QK, GLA_V), F32)] * 2, compiler_params=_cp(("arbitrary",)),
    )(*ins)


def _gla_out_fn(of, ob, r, g):
    o = of + ob
    cols = [_rms(o[:, GLA_DV * h:GLA_DV * (h + 1)], g[:, GLA_DV * h:GLA_DV * (h + 1)]) for h in range(GLA_H)]
    return jnp.concatenate(cols, axis=-1) * jax.nn.silu(r)


def _gla_out_fwd(name, of, ob, p, g):
    t = p.shape[0]
    return _rows_call(name, lambda ids, *a: (_gla_out_fn(*a),), (t // TM,),
                      [(of, _rowspec(512)), (ob, _rowspec(512)), (p, _rowspec(512, OFF["gr"])), (g, _full((1, 512)))],
                      [(_sds((t, 512), BF16), _rowspec(512), False)])[0]


def _gla_out_bwd(name, of, ob, p, g, dgla):
    t = p.shape[0]

    def fn(ids, a, b, r, gv, dv):
        _, vjp = jax.vjp(_gla_out_fn, a, b, r, gv)
        do, _, dr, dg = vjp(dv)
        return do, dr, dg

    return _rows_call(name, fn, (t // TM,),
                      [(of, _rowspec(512)), (ob, _rowspec(512)), (p, _rowspec(512, OFF["gr"])), (g, _full((1, 512))),
                       (dgla, _rowspec(512))],
                      [(_sds((t, 512), F32), _rowspec(512), False), (_sds((t, 512), BF16), _rowspec(512), False),
                       (_sds((1, 512), F32), _full((1, 512)), True)])


_TMM = 384


def _merge_fwd(name, gm, att, gla, wa, wb, wc, p):
    t = p.shape[0]
    row = lambda w, off=0: pl.BlockSpec((_TMM, w), lambda i, o=off // w: (i, o))

    def fn(ids, a, b, c, wa_, wb_, wc_, ga, gb, gc):
        return (jax.nn.sigmoid(ga) * bdot(a, wa_, 1, 0) + jax.nn.sigmoid(gb) * bdot(b, wb_, 1, 0)
                + jax.nn.sigmoid(gc) * bdot(c, wc_, 1, 0),)

    return _rows_call(name, fn, (t // _TMM,),
                      [(gm, row(512)), (att, row(512)), (gla, row(512)), (wa, _full((512, D))), (wb, _full((512, D))),
                       (wc, _full((512, D))), (p, row(D, OFF["gA"])), (p, row(D, OFF["gB"])), (p, row(D, OFF["gC"]))],
                      [(_sds((t, D), BF16), row(D), False)])[0]


def _merge_bwd(name, gm, att, gla, wa, wb, wc, p, dmerged):
    t = p.shape[0]
    row = lambda w, off=0: pl.BlockSpec((_TMM, w), lambda i, o=off // w: (i, o))

    def fn(ids, a, b, c, wa_, wb_, wc_, ga, gb, gc, dm):
        outs_y, outs_g = [], []
        for br, w, g in ((a, wa_, ga), (b, wb_, gb), (c, wc_, gc)):
            s = jax.nn.sigmoid(g)
            outs_y.append(dm * s)
            outs_g.append(dm * bdot(br, w, 1, 0) * s * (1.0 - s))
        return tuple(outs_y) + tuple(outs_g)

    o = (_sds((t, D), BF16), row(D), False)
    return _rows_call(name, fn, (t // _TMM,),
                      [(gm, row(512)), (att, row(512)), (gla, row(512)), (wa, _full((512, D))), (wb, _full((512, D))),
                       (wc, _full((512, D))), (p, row(D, OFF["gA"])), (p, row(D, OFF["gB"])), (p, row(D, OFF["gC"])),
                       (dmerged, row(D))], [o] * 6)


_TNC = 1408
_NJ = FFN // _TNC


def _shift_rows(x, prev8, next8, vp, vn):
    n = x.shape[0]
    rid = lax.broadcasted_iota(jnp.int32, x.shape, 0)
    xp = jnp.where(rid == 0, jnp.where(vp, prev8[7:8, :], 0.0), pltpu.roll(x, 1, 0))
    xn = jnp.where(rid == n - 1, jnp.where(vn, next8[0:1, :], 0.0), pltpu.roll(x, n - 1, 0))
    return xp, xn


def _seq_edges(i, t):
    start, end = i * TM, (i + 1) * TM
    return jnp.logical_and(start != 0, start != TC), jnp.logical_and(end != TC, end != t)


def _halo_specs(t, colmap):
    r8 = TM // 8
    main = pl.BlockSpec((TM, _TNC), lambda j, i: (i, colmap(j)))
    prev = pl.BlockSpec((8, _TNC), lambda j, i: (jnp.maximum(i * r8 - 1, 0), colmap(j)))
    nxt = pl.BlockSpec((8, _TNC), lambda j, i: (jnp.minimum((i + 1) * r8, t // 8 - 1), colmap(j)))
    return [main, prev, nxt]


def _conv3(x, xp, xn, w, b=None):
    y = xp * w[0:1, :] + x * w[1:2, :] + xn * w[2:3, :]
    return y if b is None else b + y


def _conv_fwd(name, a, cw, cb):
    t = a.shape[0]

    def fn(ids, ag, agp, agn, av, avp, avn, wg, wv, bg, bv):
        vp, vn = _seq_edges(ids[1], t)
        cg = _conv3(ag, *_shift_rows(ag, agp, agn, vp, vn), wg, bg)
        cv = _conv3(av, *_shift_rows(av, avp, avn, vp, vn), wv, bv)
        return (jax.nn.silu(cg) * cv,)

    gcol, vcol = (lambda j: j), (lambda j: j + _NJ)
    wspec = lambda cm: pl.BlockSpec((3, _TNC), lambda j, i: (0, cm(j)))
    bspec = lambda cm: pl.BlockSpec((1, _TNC), lambda j, i: (0, cm(j)))
    ins = [(a, s) for s in _halo_specs(t, gcol) + _halo_specs(t, vcol)]
    ins += [(cw, wspec(gcol)), (cw, wspec(vcol)), (cb, bspec(gcol)), (cb, bspec(vcol))]
    return _rows_call(name, fn, (_NJ, t // TM), ins,
                      [(_sds((t, FFN), BF16), pl.BlockSpec((TM, _TNC), lambda j, i: (i, j)), False)])[0]


def _conv_bwd(name, a, cw, cb, dact):
    t = a.shape[0]
    n = TM + 16

    def fn(ids, ag, agp, agn, av, avp, avn, dv, dvp, dvn, wg, wv, bg, bv):
        vp, vn = _seq_edges(ids[1], t)
        ext = lambda x, xp, xn: jnp.concatenate([jnp.where(vp, xp, 0.0), x, jnp.where(vn, xn, 0.0)], axis=0)
        up, dn = (lambda x: pltpu.roll(x, 1, 0)), (lambda x: pltpu.roll(x, n - 1, 0))
        eg, ev, ed = ext(ag, agp, agn), ext(av, avp, avn), ext(dv, dvp, dvn)
        cg = _conv3(eg, up(eg), dn(eg), wg, bg)
        cv = _conv3(ev, up(ev), dn(ev), wv, bv)
        s = jax.nn.sigmoid(cg)
        is_v = ids[0] >= _NJ
        dc = jnp.where(is_v, ed * cg * s, ed * cv * s * (1.0 + cg * (1.0 - s)))
        w, x = jnp.where(is_v, wv, wg), jnp.where(is_v, ev, eg)
        main = lambda y: y[8:8 + TM]
        da = main(dn(dc) * w[0:1, :] + dc * w[1:2, :] + up(dc) * w[2:3, :])
        dcm = main(dc)
        sums = [jnp.sum(dcm * main(y), axis=0, keepdims=True) for y in (up(x), x, dn(x))]
        rid = lax.broadcasted_iota(jnp.int32, (3, x.shape[1]), 0)
        dw = jnp.where(rid == 0, sums[0], jnp.where(rid == 1, sums[1], sums[2]))
        return da, dw, jnp.sum(dcm, axis=0, keepdims=True)

    gcol, vcol = (lambda j: j % _NJ), (lambda j: j % _NJ + _NJ)
    wspec = lambda cm: pl.BlockSpec((3, _TNC), lambda j, i: (0, cm(j)))
    bspec = lambda cm: pl.BlockSpec((1, _TNC), lambda j, i: (0, cm(j)))
    ins = [(a, s) for s in _halo_specs(t, gcol) + _halo_specs(t, vcol)] + [(dact, s) for s in _halo_specs(t, gcol)]
    ins += [(cw, wspec(gcol)), (cw, wspec(vcol)), (cb, bspec(gcol)), (cb, bspec(vcol))]
    return _rows_call(name, fn, (2 * _NJ, t // TM), ins,
                      [(_sds((t, 2 * FFN), BF16), pl.BlockSpec((TM, _TNC), lambda j, i: (i, j)), False),
                       (_sds((3, 2 * FFN), F32), pl.BlockSpec((3, _TNC), lambda j, i: (0, j)), True),
                       (_sds((1, 2 * FFN), F32), pl.BlockSpec((1, _TNC), lambda j, i: (0, j)), True)])


_TNA = 512


def _adaln_fwd(name, cond, w, b):
    fn = lambda ids, cv, wv, bv: ((bdot(jax.nn.silu(cv), wv[0], 1, 0) + bv[0])[None],)
    return _rows_call(name, fn, (2, ADA_LOC // _TNA),
                      [(cond, _full((16, D))), (w, pl.BlockSpec((1, D, _TNA), lambda l, j: (l, 0, j))),
                       (b, pl.BlockSpec((1, 1, _TNA), lambda l, j: (l, 0, j)))],
                      [(_sds((2, 16, ADA_LOC), F32), pl.BlockSpec((1, 16, _TNA), lambda l, j: (l, 0, j)), False)])[0]


def _adaln_bwd(name, c8, cc8, w, dl, dc):
    def fn(ids, cv, ccv, wv, dlv, dcv):
        dcs = jnp.broadcast_to(jnp.sum(dcv[0], axis=0, keepdims=True), dcv[0].shape)
        dw = hdot(jax.nn.silu(cv), dlv[0], 0, 0) + hdot(jax.nn.silu(ccv), dcs, 0, 0)
        s = jax.nn.sigmoid(ccv)
        rid = lax.broadcasted_iota(jnp.int32, ccv.shape, 0)
        dcc = jnp.where(rid == 0, bdot(dcs, wv[0], 1, 1) * s * (1.0 + ccv * (1.0 - s)), 0.0)
        return dw[None], dcc

    dspec = pl.BlockSpec((1, 8, _TNA), lambda l, j: (l, 0, j))
    return _rows_call(name, fn, (2, ADA_LOC // _TNA),
                      [(c8, _full((8, D))), (cc8, _full((8, D))), (w, pl.BlockSpec((1, D, _TNA), lambda l, j: (l, 0, j))),
                       (dl, dspec), (dc, dspec)],
                      [(_sds((2, D, ADA_LOC), F32), pl.BlockSpec((1, D, _TNA), lambda l, j: (l, 0, j)), False),
                       (_sds((8, D), F32), _full((8, D)), True)], acc_axes=(0, 1))


def _adamw_fn(w, g, m, v):
    m = ADAM_B1 * m + (1.0 - ADAM_B1) * g
    v = ADAM_B2 * v + (1.0 - ADAM_B2) * (g * g)
    m_hat = m / (1.0 - ADAM_B1 ** ADAM_STEP)
    v_hat = v / (1.0 - ADAM_B2 ** ADAM_STEP)
    return -ADAM_LR * (m_hat / (jnp.sqrt(v_hat) + ADAM_EPS) + ADAM_WD * w), m, v


def _adamw(name, w, g, m, v):
    r, c = w.shape
    tr = _tile(r, max(8, (1 << 20) // (4 * c)), 8)
    spec = pl.BlockSpec((tr, c), lambda i: (i, 0))
    o = (_sds((r, c), F32), spec, False)
    return _rows_call(name, lambda ids, *a: _adamw_fn(*a), (r // tr,), [(x, spec) for x in (w, g, m, v)], [o, o, o],
                      sem=("parallel",))


def _coords():
    return lax.axis_index("x"), lax.axis_index("y"), lax.axis_index("c")


def _other_chips(x, y):
    return [(1 - x, y), (x, 1 - y), (1 - x, 1 - y)]


def _allgather_small(name, blk):
    m_per, n = blk.shape

    def body(x_ref, out_ref, send_sems, recv_sems, local_sem):
        x, y, c = _coords()
        me, sibling = (x, y, c), (x, y, 1 - c)
        chips = _other_chips(x, y)

        def rows(px, py, pc):
            return out_ref.at[pl.ds((4 * px + 2 * py + pc) * m_per, m_per), :]

        def copy(k, block, to, src=None):
            return pltpu.make_async_remote_copy(
                src_ref=rows(*block) if src is None else src, dst_ref=rows(*block), send_sem=send_sems.at[k],
                recv_sem=recv_sems.at[k], device_id=to, device_id_type=MESH)

        mine = pltpu.make_async_copy(x_ref, rows(*me), local_sem)
        mine.start()
        first = [copy(0, me, sibling, src=x_ref)]
        first += [copy(1 + j, me, (*chip, c), src=x_ref) for j, chip in enumerate(chips)]
        for cp in first:
            cp.start()
        passed = [copy(4 + j, (*chip, c), sibling) for j, chip in enumerate(chips)]
        for j, chip in enumerate(chips):
            copy(1 + j, (*chip, c), me).wait_recv()
            passed[j].start()
        copy(0, sibling, me).wait_recv()
        for j, chip in enumerate(chips):
            copy(4 + j, (*chip, 1 - c), me).wait_recv()
        for cp in first + passed:
            cp.wait_send()
        mine.wait()

    return pl.pallas_call(
        body, name=name, out_shape=_sds((N_DEV * m_per, n), blk.dtype),
        in_specs=[pl.BlockSpec(memory_space=pltpu.VMEM)], out_specs=pl.BlockSpec(memory_space=pltpu.VMEM),
        scratch_shapes=[pltpu.SemaphoreType.DMA((7,)), pltpu.SemaphoreType.DMA((7,)), pltpu.SemaphoreType.DMA],
        compiler_params=pltpu.CompilerParams(vmem_limit_bytes=VMEM_LIMIT),
    )(blk)


_ANY = pl.BlockSpec(memory_space=pl.ANY)


def _remote(src, dst, send_sems, recv_sems, s, to):
    return pltpu.make_async_remote_copy(src_ref=src, dst_ref=dst, send_sem=send_sems.at[s], recv_sem=recv_sems.at[s],
                                        device_id=to, device_id_type=MESH)


def _comm_call(name, body, ins, out_shapes, n_sems, n_local):
    return pl.pallas_call(
        body, name=name, out_shape=out_shapes, in_specs=[_ANY] * len(ins), out_specs=[_ANY] * len(out_shapes),
        scratch_shapes=[pltpu.SemaphoreType.DMA((n_sems,)), pltpu.SemaphoreType.DMA((n_sems,)),
                        pltpu.SemaphoreType.DMA((n_local,))],
    )(*ins)


def _allgather_layers(name, locs):
    n = len(locs)

    def body(*refs):
        ins, outs, (send_sems, recv_sems, local_sems) = refs[:n], refs[n:2 * n], refs[2 * n:]
        x, y, c = _coords()
        k = 2 * x + y
        sibling = (x, y, 1 - c)
        chips = _other_chips(x, y)
        first = [_remote(ins[t].at[c], outs[t].at[k, c], send_sems, recv_sems, 6 * t + j, (*chip, c))
                 for t in range(n) for j, chip in enumerate(chips)]
        for cp in first:
            cp.start()
        passed = []
        for t in range(n):
            for j, (cx, cy) in enumerate(chips):
                there = outs[t].at[2 * cx + cy, c]
                _remote(there, there, send_sems, recv_sems, 6 * t + j, sibling).wait_recv()
                passed.append(_remote(there, there, send_sems, recv_sems, 6 * t + 3 + j, sibling))
                passed[-1].start()
        for t in range(n):
            for j, (cx, cy) in enumerate(chips):
                there = outs[t].at[2 * cx + cy, 1 - c]
                _remote(there, there, send_sems, recv_sems, 6 * t + 3 + j, sibling).wait_recv()
        for cp in first + passed:
            cp.wait_send()

    outs = _comm_call(name, body, locs, [_sds((N_CHIP,) + a.shape, a.dtype) for a in locs], 6 * n, 1)
    k = 2 * lax.axis_index("x") + lax.axis_index("y")
    return [lax.dynamic_update_slice_in_dim(o, a[None], k, axis=0) for o, a in zip(outs, locs)]


def _rs_pair_exchange(name, g0, g1):
    n = len(g0)

    def body(*refs):
        a0, a1, outs, (send_sems, recv_sems, _) = refs[:n], refs[n:2 * n], refs[2 * n:3 * n], refs[3 * n:]
        x, y, c = _coords()

        def run(srcs):
            cps = [_remote(srcs[t], outs[t], send_sems, recv_sems, t, (x, y, 1 - c)) for t in range(n)]
            for cp in cps:
                cp.start()
            for cp in cps:
                cp.wait()

        pl.when(c == 0)(lambda: run(a1))
        pl.when(c == 1)(lambda: run(a0))

    return _comm_call(name, body, list(g0) + list(g1), [_sds(a.shape, a.dtype) for a in g0], n, 1)


def _ew2d(name, fn, ins, out_dtype):
    shape = ins[0].shape
    r, c = _prod(shape[:-1]), shape[-1]
    tr = _tile(r, max(8, (1 << 20) // (4 * c)), 8)
    spec = pl.BlockSpec((tr, c), lambda i: (i, 0))
    out = _rows_call(name, lambda ids, *a: (fn(*a),), (r // tr,), [(a.reshape(r, c), spec) for a in ins],
                     [(_sds((r, c), out_dtype), spec, False)], sem=("parallel",))[0]
    return out.reshape(shape)


def _rs_chip_exchange(name, s1):
    n = len(s1)

    def body(*refs):
        ins, outs, (send_sems, recv_sems, local_sems) = refs[:n], refs[n:2 * n], refs[2 * n:]
        x, y, c = _coords()
        k = 2 * x + y
        chips = _other_chips(x, y)
        cps = [_remote(ins[t].at[2 * cx + cy], outs[t].at[k], send_sems, recv_sems, 3 * t + j, (cx, cy, c))
               for t in range(n) for j, (cx, cy) in enumerate(chips)]
        for cp in cps:
            cp.start()
        for t in range(n):
            for j, (cx, cy) in enumerate(chips):
                there = outs[t].at[2 * cx + cy]
                _remote(there, there, send_sems, recv_sems, 3 * t + j, (cx, cy, c)).wait_recv()
        for cp in cps:
            cp.wait_send()

    outs = _comm_call(name, body, s1, [_sds(a.shape, a.dtype) for a in s1], 3 * n, 1)
    k = 2 * lax.axis_index("x") + lax.axis_index("y")
    own = [lax.dynamic_index_in_dim(a, k, axis=0, keepdims=True) for a in s1]
    return [lax.dynamic_update_slice_in_dim(o, a, k, axis=0) for o, a in zip(outs, own)]


def _sum_slots(name, a):
    s, r, cdim = a.shape
    tr = _tile(r, 512, 8)

    def fn(ids, av):
        tot = av[0]
        for i in range(1, s):
            tot = tot + av[i]
        return (tot,)

    return _rows_call(name, fn, (r // tr,), [(a, pl.BlockSpec((s, tr, cdim), lambda i: (0, i, 0)))],
                      [(_sds((r, cdim), F32), pl.BlockSpec((tr, cdim), lambda i: (i, 0)), False)], sem=("parallel",))[0]


def _pair_allgather(name, halves):
    n = len(halves)

    def body(*refs):
        ins, outs, (send_sems, recv_sems, local_sems) = refs[:n], refs[n:2 * n], refs[2 * n:]
        x, y, c = _coords()
        cps = [_remote(ins[t], outs[t].at[c], send_sems, recv_sems, t, (x, y, 1 - c)) for t in range(n)]
        for cp in cps:
            cp.start()
        for t in range(n):
            _remote(ins[t], outs[t].at[1 - c], send_sems, recv_sems, t, (x, y, 1 - c)).wait_recv()
        for cp in cps:
            cp.wait_send()

    outs = _comm_call(name, body, halves, [_sds((2,) + a.shape, a.dtype) for a in halves], n, 1)
    return [lax.dynamic_update_slice_in_dim(o, a[None], lax.axis_index("c"), axis=0) for o, a in zip(outs, halves)]


def _reduce_scatter(g0, g1):
    n = len(g0)
    got = _rs_pair_exchange("rs_pair_exchange", g0, g1)
    keep = lambda a, b, r: jnp.where(lax.axis_index("c") == 0, a, b) + r
    s1 = [_ew2d("rs_pair_add_%d" % t, keep, [g0[t], g1[t], got[t]], BF16) for t in range(n)]
    slots = _rs_chip_exchange("rs_chip_exchange", s1)
    red = [_sum_slots("rs_chip_sum_%d" % t, a.reshape(N_CHIP, -1, a.shape[-1])).reshape(a.shape[1:])
           for t, a in enumerate(slots)]
    return _pair_allgather("rs_pair_allgather", red)


PACK_C = 1024
_SHARDED = (("w_in", 1), ("w_br_a", 1), ("w_br_b", 1), ("w_br_c", 1), ("w_out", 0), ("w_ffn_up", 1), ("w_ffn_down", 0))
_SHARDED_SMALL = (("conv_w", (3, 2 * FFN), 1), ("w_alpha2", (2, 16, GLA_QK), 2), ("b_alpha", (2, GLA_QK), 1))


def _prod(shape):
    n = 1
    for s in shape:
        n *= s
    return n


def _to_blocks(full, axis):
    shp = full.shape
    split = full.reshape(shp[:axis] + (N_CHIP, shp[axis] // N_CHIP) + shp[axis + 1:])
    return jnp.moveaxis(split, axis, 0)


def _from_blocks(blocks, axis):
    return jnp.concatenate([blocks[k] for k in range(N_CHIP)], axis=axis)


def _rope_tables(tx):
    pos = jnp.arange(tx, dtype=jnp.int32)
    inv_freq = 10000.0 ** (-jnp.arange(16, dtype=F32) / 16)
    ang_r = (pos // GRID_W).astype(F32)[:, None] * inv_freq
    ang_c = (pos % GRID_W).astype(F32)[:, None] * inv_freq
    ang = jnp.concatenate([ang_r, ang_r, ang_c, ang_c], axis=-1)
    sign = jnp.concatenate([-jnp.ones((16,), F32), jnp.ones((16,), F32)] * 2)
    cos = jnp.concatenate([jnp.ones((TC, HD), F32), jnp.cos(ang)], axis=0)
    sin = jnp.concatenate([jnp.zeros((TC, HD), F32), jnp.sin(ang) * sign], axis=0)
    return jnp.tile(cos, (1, 2)), jnp.tile(sin, (1, 2))


def _lane_consts():
    l = jnp.arange(512)
    seg = (l[:, None] // HD == l[None, :] // HD).astype(F32) / HD
    partner = jnp.where(l % 32 < 16, l + 16, l - 16)
    perm = (l[:, None] == partner[None, :]).astype(F32)
    return seg, perm


def _heads(a, n):
    return a.reshape(a.shape[0], n, HD).transpose(1, 0, 2)


def _unheads(a):
    return a.transpose(1, 0, 2).reshape(a.shape[1], a.shape[0] * HD)


def _gather_f32_shards(shards):
    sizes = [_prod(a.shape) for a in shards]
    flat = jnp.concatenate([a.reshape(-1) for a in shards] + [jnp.zeros((16 * PACK_C - sum(sizes),), F32)])
    got = _allgather_small("gather_f32_shards", flat.reshape(16, PACK_C)).reshape(N_CHIP, 2, 16 * PACK_C)[:, 0]
    out, o = {}, 0
    for (n, _, ax), a, sz in zip(_SHARDED_SMALL, shards, sizes):
        out[n] = jnp.concatenate([got[k, o:o + sz].reshape(a.shape) for k in range(N_CHIP)], axis=ax + 1)
        o += sz
    return out


def _layer_params(l, wfull, small):
    w2 = small["w_alpha2_full"][l]
    w2pad = jnp.zeros((128, 512), F32).at[0:16, 0:256].set(w2[0]).at[16:32, 256:512].set(w2[1])
    full = {n: _from_blocks(wfull[n][:, l], ax) for n, ax in _SHARDED}
    return dict(
        w_in=_to_new_cols(full["w_in"]), wa=full["w_br_a"], wb=full["w_br_b"], wc=full["w_br_c"],
        w_out=full["w_out"], w_up=full["w_ffn_up"], w_down=full["w_ffn_down"],
        cw=small["conv_w_full"][l], cb=small["conv_b"][l][None], w2=w2pad,
        b2=small["b_alpha_full"][l].reshape(1, 512),
        g1=small["norm1_g"][l][None], g2=small["norm2_g"][l][None], gq=jnp.tile(small["q_norm_g"][l], 8)[None],
        gk=jnp.tile(small["k_norm_g"][l], 2)[None], ggm=small["gmlp_norm_g"][l][None], ws=small["w_spatial"][l],
        bst=small["b_spatial"][l].T, ggl=small["gla_norm_g"][l][None])


def _layer_fwd(l, last, x, h1, mod, P, tabs):
    cos, sin, seg, perm = tabs
    n = "l%d_" % l
    s = dict(x=x, h1=h1)
    p = _mm(n + "in_proj", h1, P["w_in"], "nn", F32, tm_t=768, tn_t=2176)
    s["p"] = p
    s["gm"] = _gmlp_fwd(n + "gmlp", p, P["ggm"], P["ws"], P["bst"])
    qr, kr, vb = _qk_fwd(n + "qk_prep", p, P["gq"], P["gk"], cos, sin, seg, perm)
    qh, kh, vh = _heads(qr, NQ), _heads(kr, NKV), _heads(vb, NKV)
    s["qh"], s["kh"], s["vh"] = qh, kh, vh
    one_hot = (jnp.arange(HD) == 0).astype(BF16)
    v1 = jnp.concatenate([vh, jnp.broadcast_to(one_hot, vh.shape)], axis=-1)
    ox, lse_x = _attn_fwd(n + "attn_x", qh[:, TC:], kh, v1)
    s["ox"], s["lse_x"] = ox, lse_x
    if last:
        oc = jnp.zeros((NQ, TC, HD), F32)
    else:
        oc, lse_c = _attn_fwd(n + "attn_c", qh[:, :TC], kh[:, :TC], v1[:, :TC])
        s["oc"], s["lse_c"] = oc, lse_c
    s["att"] = _unheads(jnp.concatenate([oc, ox], axis=1)).astype(BF16)
    la = _decay_fwd(n + "gla_decay", p, P["w2"], P["b2"])
    s["la"] = la
    s["of"], s["sf"], s["ob"], s["sb"] = _gla_fwd(n + "gla_scan", p, la)
    s["gla"] = _gla_out_fwd(n + "gla_out", s["of"], s["ob"], p, P["ggl"])
    s["merged"] = _merge_fwd(n + "merge", s["gm"], s["att"], s["gla"], P["wa"], P["wb"], P["wc"], p)
    s["mix"] = _mm(n + "out_proj", s["merged"], P["w_out"], "nn", F32)
    s["x_mid"], s["h2"] = _res_nm_fwd(n + "res1_norm2", x, s["mix"], mod, 2, mod, P["g2"], 3, 4)
    s["a"] = _mm(n + "ffn_up", s["h2"], P["w_up"], "nn", F32)
    s["act"] = _conv_fwd(n + "conv_gate", s["a"], P["cw"], P["cb"])
    s["f"] = _mm(n + "ffn_down", s["act"], P["w_down"], "nn", F32)
    return s


def _layer_bwd(l, last, s, mod, P, tabs, dx_mid, df, gw):
    cos, sin, seg, perm = tabs
    n = "l%d_b_" % l
    t = dx_mid.shape[0]
    p = s["p"]
    gw["w_ffn_down"] = _mm(n + "ffn_down_w", s["act"], df, "tn", F32, tm_t=1408)
    dact = _mm(n + "ffn_down_x", df, P["w_down"], "nt", F32)
    da, gw["conv_w"], dcb = _conv_bwd(n + "conv_gate", s["a"], P["cw"], P["cb"], dact)
    gw["conv_b"] = dcb[0]
    gw["w_ffn_up"] = _mm(n + "ffn_up_w", s["h2"], da, "tn", F32)
    dh2 = _mm(n + "ffn_up_x", da, P["w_up"], "nt", F32)
    dx, dmix, dmod_a, dmod_b, dg2 = _res_nm_bwd(n + "res1_norm2", s["x"], s["mix"], mod, 2, mod, P["g2"], 3, 4, dx_mid, dh2)
    dmod = dmod_a + dmod_b
    gw["norm2_g"] = dg2[0]
    gw["w_out"] = _mm(n + "out_proj_w", s["merged"], dmix, "tn", F32)
    dmerged = _mm(n + "out_proj_x", dmix, P["w_out"], "nt", F32)
    dya, dyb, dyc, dga, dgb, dgc = _merge_bwd(n + "merge", s["gm"], s["att"], s["gla"], P["wa"], P["wb"], P["wc"], p, dmerged)
    gw["w_br_a"] = _mm(n + "br_a_w", s["gm"], dya, "tn", F32)
    gw["w_br_b"] = _mm(n + "br_b_w", s["att"], dyb, "tn", F32)
    gw["w_br_c"] = _mm(n + "br_c_w", s["gla"], dyc, "tn", F32)
    dgm = _mm(n + "br_a_x", dya, P["wa"], "nt", F32)
    datt = _mm(n + "br_b_x", dyb, P["wb"], "nt", F32)
    dgla = _mm(n + "br_c_x", dyc, P["wc"], "nt", F32)
    du, dv_g, dggm, dws, dbst = _gmlp_bwd(n + "gmlp", p, P["ggm"], P["ws"], P["bst"], dgm)
    gw["gmlp_norm_g"], gw["w_spatial"], gw["b_spatial"] = dggm[0], dws, dbst.T
    doh = _heads(datt, NQ)
    qh, kh, vh = s["qh"], s["kh"], s["vh"]
    row = lambda a: a.reshape(a.shape[0], 1, a.shape[1])
    dqx, dkh, dvh = _attn_bwd(n + "attn_x", qh[:, TC:], kh, vh, s["ox"], doh[:, TC:], row(s["lse_x"]))
    if last:
        dqc = jnp.zeros((NQ, TC, HD), F32)
    else:
        dqc, dkc, dvc = _attn_bwd(n + "attn_c", qh[:, :TC], kh[:, :TC], vh[:, :TC], s["oc"], doh[:, :TC], row(s["lse_c"]))
        pad = jnp.zeros((NKV, t - TC, HD), F32)
        dkh = dkh + jnp.concatenate([dkc, pad], axis=1)
        dvh = dvh + jnp.concatenate([dvc, pad], axis=1)
    dqr = _unheads(jnp.concatenate([dqc, dqx], axis=1))
    dq, dk, dgq, dgk = _qk_bwd(n + "qk_prep", p, P["gq"], P["gk"], cos, sin, seg, perm, dqr, _unheads(dkh))
    gw["q_norm_g"], gw["k_norm_g"] = dgq.reshape(8, HD).sum(0), dgk.reshape(2, HD).sum(0)
    dv_att = _unheads(dvh).astype(BF16)
    do, dr, dggl = _gla_out_bwd(n + "gla_out", s["of"], s["ob"], p, P["ggl"], dgla)
    gw["gla_norm_g"] = dggl[0]
    scans = _gla_bwd(n + "gla_scan", p, s["la"], s["sf"], s["sb"], do)
    dab, dw2, db2, dglq, dglk, dglv = _decay_bwd(n + "gla_decay", p, P["w2"], P["b2"], scans[:4], scans[4:])
    gw["w_alpha2"] = jnp.stack([dw2[0:16, 0:256], dw2[16:32, 256:512]])
    gw["b_alpha"] = db2.reshape(2, 256)
    dp = jnp.concatenate([dga, dgb, dgc, du, dv_g, dq, dglv, dr, dglq, dglk, dk, dv_att, dab], axis=-1)
    gw["w_in"] = _to_ref_cols(_mm(n + "in_proj_w", s["h1"], dp, "tn", F32, tn_t=2176, tk_t=768))
    dh1 = _mm(n + "in_proj_x", dp, P["w_in"], "nt", F32, tk_t=2176)
    return dx, dh1, dmod


_SMALL = (("norm1_g", (2, D)), ("norm2_g", (2, D)), ("q_norm_g", (2, HD)), ("k_norm_g", (2, HD)), ("gmlp_norm_g", (2, GW)),
          ("gla_norm_g", (2, GLA_V)), ("w_spatial", (2, 4, 128, 128)), ("b_spatial", (2, 4, 128)), ("conv_b", (2, 2 * FFN)),
          ("final_norm_g", (D,))) + tuple((n, (2,) + s) for n, s, _ in _SHARDED_SMALL)
_SMALL_N = 2 * 2 * ADA_W + sum(_prod(s) for _, s in _SMALL)
_SMALL_R = -(-_SMALL_N // (PACK_C * 8)) * 8


def _mod_tables(c, c_ctx, w_ada, b_ada, k):
    x, y, cc = _coords()
    me = 4 * x + 2 * y + cc
    c_all = _allgather_small("gather_c", jnp.concatenate([c, jnp.zeros((7, D), F32)], axis=0))
    c8 = c_all.reshape(N_DEV, 8, D)[:, 0]
    cond = jnp.concatenate([c8, c_ctx[None], jnp.zeros((7, D), F32)], axis=0)
    b_loc = lax.dynamic_slice_in_dim(b_ada, k * ADA_LOC, ADA_LOC, axis=1)[:, None, :]
    m_loc = _adaln_fwd("adaln", cond, w_ada, b_loc)
    m_all = _allgather_small("gather_mod", m_loc.reshape(32, ADA_LOC)).reshape(N_CHIP, 2, 2, 16, ADA_LOC)[:, 0]
    m_all = m_all.transpose(1, 2, 0, 3).reshape(2, 16, ADA_W)
    rows = jnp.stack([m_all[:, 8], lax.dynamic_index_in_dim(m_all, me, axis=1, keepdims=False)], axis=1)
    return rows.reshape(2, 2, 6, D), c8


def _step(x, c, ctx, c_ctx, W, tgt):
    xc, yc, cc = _coords()
    k = 2 * xc + yc
    tx = x.shape[0]
    t = TC + tx
    small = {n: W[n] for n, _ in _SMALL}
    for n, a in _gather_f32_shards([W[n] for n, _, _ in _SHARDED_SMALL]).items():
        small[n + "_full"] = a

    gathered = _allgather_layers("gather_weights", [W[n].astype(BF16) for n, _ in _SHARDED])
    wfull = {n: a for (n, _), a in zip(_SHARDED, gathered)}
    mods, c8 = _mod_tables(c, c_ctx, W["w_ada"], W["b_ada"], k)
    tabs = _rope_tables(tx) + _lane_consts()
    params = [_layer_params(l, wfull, small) for l in range(2)]

    xs = jnp.concatenate([ctx, x], axis=0)
    h1 = _nm_fwd("l0_norm1", xs, mods[0], params[0]["g1"], 0, 1)
    s0 = _layer_fwd(0, False, xs, h1, mods[0], params[0], tabs)
    x1, h1b = _res_nm_fwd("l0_res2_norm1", s0["x_mid"], s0["f"], mods[0], 5, mods[1], params[1]["g1"], 0, 1)
    s1 = _layer_fwd(1, True, x1, h1b, mods[1], params[1], tabs)
    loss, dxm_l, df_l, dmod_head, dgf = _head("head", s1["x_mid"], s1["f"], mods[1], W["final_norm_g"][None], tgt)

    gws = [dict(), dict()]
    zc = lambda dt: jnp.zeros((TC, D), dt)
    dx1, dh1b, dmod1 = _layer_bwd(1, True, s1, mods[1], params[1], tabs, jnp.concatenate([zc(F32), dxm_l]),
                                  jnp.concatenate([zc(BF16), df_l]), gws[1])
    dxm0, df0, dmod0_g, dmod1_s, dg1b = _res_nm_bwd("l0_b_res2_norm1", s0["x_mid"], s0["f"], mods[0], 5, mods[1],
                                                    params[1]["g1"], 0, 1, dx1, dh1b)
    gws[1]["norm1_g"] = dg1b[0]
    dx0, dh1, dmod0 = _layer_bwd(0, False, s0, mods[0], params[0], tabs, dxm0, df0, gws[0])
    dxs, dmod0_s, dg1 = _nm_bwd("l0_b_norm1", xs, mods[0], params[0]["g1"], 0, 1, dx0, dh1)
    gws[0]["norm1_g"] = dg1[0]
    grad_x = dxs[TC:]
    dmods = jnp.stack([dmod0 + dmod0_g + dmod0_s, dmod1 + dmod1_s + dmod_head])

    stk = {n: jnp.stack([gws[0][n], gws[1][n]]) for n, _ in _SMALL if n != "final_norm_g"}
    stk["final_norm_g"] = dgf[0]
    flat = jnp.concatenate([dmods.reshape(-1)] + [stk[n].reshape(-1) for n, _ in _SMALL])
    flat = jnp.concatenate([flat, jnp.zeros((_SMALL_R * PACK_C - _SMALL_N,), F32)]).reshape(_SMALL_R, PACK_C)
    every = _allgather_small("gather_small_grads", flat).reshape(N_DEV, _SMALL_R, PACK_C)
    tot = _sum_slots("sum_small_grads", every).reshape(-1)
    grads, o = {}, 2 * 2 * ADA_W
    for n, shp in _SMALL:
        grads[n] = tot[o:o + _prod(shp)].reshape(shp)
        o += _prod(shp)
    grads["b_ada"] = tot[:2 * 2 * ADA_W].reshape(2, 2, ADA_W).sum(axis=1)

    dm_every = every[:, :2 * 2 * ADA_W // PACK_C].reshape(N_DEV, 2, 2, ADA_W)
    dm_loc = lax.dynamic_slice_in_dim(dm_every, k * ADA_LOC, ADA_LOC, axis=3).transpose(1, 2, 0, 3)
    cc8 = jnp.concatenate([c_ctx[None], jnp.zeros((7, D), F32)], axis=0)
    grads["w_ada"], dcc = _adaln_bwd("adaln_b", c8, cc8, W["w_ada"], dm_loc[:, 1], dm_loc[:, 0])
    dcc_every = _allgather_small("gather_dcctx", dcc * 0.5).reshape(N_DEV, 8, D)
    grads["c_ctx"] = _sum_slots("sum_dcctx", dcc_every)[0]

    for n, shp, ax in _SHARDED_SMALL:
        grads[n] = lax.dynamic_slice_in_dim(grads[n], k * (shp[ax] // N_CHIP), shp[ax] // N_CHIP, axis=ax + 1)
    red = _reduce_scatter(*[[_to_blocks(gws[l][n], ax) for n, ax in _SHARDED] for l in range(2)])
    grads.update({n: a for (n, _), a in zip(_SHARDED, red)})
    return loss[0, 0], grad_x, grads


_WEIGHTS = ("c_ctx", "w_ada", "b_ada", "norm1_g", "norm2_g", "w_in", "q_norm_g", "k_norm_g", "gmlp_norm_g", "w_spatial",
            "b_spatial", "w_alpha2", "b_alpha", "gla_norm_g", "w_br_a", "w_br_b", "w_br_c", "w_out", "w_ffn_up", "conv_w",
            "conv_b", "w_ffn_down", "final_norm_g")
_BIG = ("w_ada", "w_in", "w_br_a", "w_br_b", "w_br_c", "w_out", "w_ffn_up", "w_ffn_down")


def _update(W, G, M, V):
    delta, new_m, new_v = {}, {}, {}
    for n in _BIG:
        shp = W[n].shape
        two = lambda a: a.reshape(-1, shp[-1])
        d, m, v = _adamw("adamw_" + n, two(W[n]), two(G[n]), two(M[n]), two(V[n]))
        delta[n], new_m[n], new_v[n] = d.reshape(shp), m.reshape(shp), v.reshape(shp)
    rest = [n for n in _WEIGHTS if n not in _BIG]
    tot = sum(_prod(W[n].shape) for n in rest)
    rows = -(-tot // (PACK_C * 8)) * 8

    def cat(dct):
        flat = jnp.concatenate([dct[n].reshape(-1) for n in rest] + [jnp.zeros((rows * PACK_C - tot,), F32)])
        return flat.reshape(rows, PACK_C)

    outs = _adamw("adamw_small", cat(W), cat(G), cat(M), cat(V))
    o = 0
    for n in rest:
        sz, shp = _prod(W[n].shape), W[n].shape
        delta[n], new_m[n], new_v[n] = (a.reshape(-1)[o:o + sz].reshape(shp) for a in outs)
        o += sz
    return delta, new_m, new_v


def kernel(x, c, ctx, c_ctx, w_ada, b_ada, norm1_g, norm2_g, w_in, q_norm_g, k_norm_g, gmlp_norm_g, w_spatial, b_spatial, w_alpha2, b_alpha, gla_norm_g, w_br_a, w_br_b, w_br_c, w_out, w_ffn_up, conv_w, conv_b, w_ffn_down, final_norm_g, loss_target, m_c_ctx, m_w_ada, m_b_ada, m_norm1_g, m_norm2_g, m_w_in, m_q_norm_g, m_k_norm_g, m_gmlp_norm_g, m_w_spatial, m_b_spatial, m_w_alpha2, m_b_alpha, m_gla_norm_g, m_w_br_a, m_w_br_b, m_w_br_c, m_w_out, m_w_ffn_up, m_conv_w, m_conv_b, m_w_ffn_down, m_final_norm_g, v_c_ctx, v_w_ada, v_b_ada, v_norm1_g, v_norm2_g, v_w_in, v_q_norm_g, v_k_norm_g, v_gmlp_norm_g, v_w_spatial, v_b_spatial, v_w_alpha2, v_b_alpha, v_gla_norm_g, v_w_br_a, v_w_br_b, v_w_br_c, v_w_out, v_w_ffn_up, v_conv_w, v_conv_b, v_w_ffn_down, v_final_norm_g):
    W = dict(c_ctx=c_ctx, w_ada=w_ada, b_ada=b_ada, norm1_g=norm1_g, norm2_g=norm2_g, w_in=w_in, q_norm_g=q_norm_g,
             k_norm_g=k_norm_g, gmlp_norm_g=gmlp_norm_g, w_spatial=w_spatial, b_spatial=b_spatial, w_alpha2=w_alpha2,
             b_alpha=b_alpha, gla_norm_g=gla_norm_g, w_br_a=w_br_a, w_br_b=w_br_b, w_br_c=w_br_c, w_out=w_out,
             w_ffn_up=w_ffn_up, conv_w=conv_w, conv_b=conv_b, w_ffn_down=w_ffn_down, final_norm_g=final_norm_g)
    M = dict(c_ctx=m_c_ctx, w_ada=m_w_ada, b_ada=m_b_ada, norm1_g=m_norm1_g, norm2_g=m_norm2_g, w_in=m_w_in,
             q_norm_g=m_q_norm_g, k_norm_g=m_k_norm_g, gmlp_norm_g=m_gmlp_norm_g, w_spatial=m_w_spatial,
             b_spatial=m_b_spatial, w_alpha2=m_w_alpha2, b_alpha=m_b_alpha, gla_norm_g=m_gla_norm_g, w_br_a=m_w_br_a,
             w_br_b=m_w_br_b, w_br_c=m_w_br_c, w_out=m_w_out, w_ffn_up=m_w_ffn_up, conv_w=m_conv_w, conv_b=m_conv_b,
             w_ffn_down=m_w_ffn_down, final_norm_g=m_final_norm_g)
    V = dict(c_ctx=v_c_ctx, w_ada=v_w_ada, b_ada=v_b_ada, norm1_g=v_norm1_g, norm2_g=v_norm2_g, w_in=v_w_in,
             q_norm_g=v_q_norm_g, k_norm_g=v_k_norm_g, gmlp_norm_g=v_gmlp_norm_g, w_spatial=v_w_spatial,
             b_spatial=v_b_spatial, w_alpha2=v_w_alpha2, b_alpha=v_b_alpha, gla_norm_g=v_gla_norm_g, w_br_a=v_w_br_a,
             w_br_b=v_w_br_b, w_br_c=v_w_br_c, w_out=v_w_out, w_ffn_up=v_w_ffn_up, conv_w=v_conv_w, conv_b=v_conv_b,
             w_ffn_down=v_w_ffn_down, final_norm_g=v_final_norm_g)
    loss_local, grad_x, G = _step(x[0], c, ctx[0], c_ctx, W, loss_target[0])
    loss = lax.psum(loss_local, ("x", "y", "c"))
    delta, new_m, new_v = _update(W, G, M, V)
    return (loss, grad_x[None], *[G[n] for n in _WEIGHTS], *[delta[n] for n in _WEIGHTS],
            *[new_m[n] for n in _WEIGHTS], *[new_v[n] for n in _WEIGHTS])
```

```python
import functools

import jax
import jax.numpy as jnp
from jax import lax
from jax.experimental import pallas as pl
from jax.experimental.pallas import tpu as pltpu

F32 = jnp.float32
BF16 = jnp.bfloat16

D = 1024
TC = 256
GRID_W = 64
EPS = 1e-6
HD = 64
NQ = 8
NKV = 2
QG = NQ // NKV
GLA_H = 4
GLA_DK = 64
GLA_DV = 128
GLA_QK = 256
GLA_V = 512
GLA_CHUNK = 64
GLA_TAU = 16.0
GW = 512
FFN = 2816
IN_W = 6432
PW = 6528
ADA_W = 6 * D
N_CHIP = 4
N_DEV = 8
ADA_LOC = ADA_W // N_CHIP

ADAM_LR = 0.001
ADAM_B1 = 0.9
ADAM_B2 = 0.999
ADAM_EPS = 1e-08
ADAM_WD = 0.01
ADAM_STEP = 10

TM = 256
NCB = TC // TM
LANE = 128
VMEM_LIMIT = 48 * 1024 * 1024
MESH = pl.DeviceIdType.MESH

_COLS = (("gA", 3360, 1024), ("gB", 4384, 1024), ("gC", 5408, 1024), ("gu", 0, 512), ("gv", 512, 512),
         ("q", 1024, 512), ("glv", 2304, 512), ("gr", 2848, 512), ("glq", 1792, 256), ("glk", 2048, 256),
         ("k", 1536, 128), ("v", 1664, 128), ("ab", 2816, 32))
OFF = {}
_o = 0
for _n, _s, _w in _COLS:
    OFF[_n] = _o
    _o += max(_w, LANE)
assert _o == PW


def _to_new_cols(w):
    parts = [w[..., s:s + n] for _, s, n in _COLS]
    pad = jnp.zeros(w.shape[:-1] + (PW - IN_W,), w.dtype)
    return jnp.concatenate(parts + [pad], axis=-1)


def _to_ref_cols(w):
    by_start = sorted(_COLS, key=lambda t: t[1])
    return jnp.concatenate([w[..., OFF[n]:OFF[n] + wd] for n, _, wd in by_start], axis=-1)


def _tile(n, target, align=LANE):
    best = None
    t = align
    while t <= min(n, target):
        if n % t == 0:
            best = t
        t += align
    assert best is not None, (n, target, align)
    return best


def _cp(sem=None):
    return pltpu.CompilerParams(dimension_semantics=sem, vmem_limit_bytes=VMEM_LIMIT)


def _bdot_impl(a, b, ca, cb):
    return lax.dot_general(a.astype(BF16), b.astype(BF16), (((ca,), (cb,)), ((), ())),
                           preferred_element_type=F32)


@functools.partial(jax.custom_vjp, nondiff_argnums=(2, 3))
def bdot(a, b, ca, cb):
    return _bdot_impl(a, b, ca, cb)


def _bdot_fwd(a, b, ca, cb):
    return _bdot_impl(a, b, ca, cb), (a, b)


def _bdot_bwd(ca, cb, res, g):
    a, b = res
    da = bdot(g, b, 1, 1 - cb) if ca == 1 else bdot(b, g, 1 - cb, 1)
    db = bdot(a, g, 1 - ca, 0) if cb == 0 else bdot(g, a, 0, 1 - ca)
    return da.astype(a.dtype), db.astype(b.dtype)


bdot.defvjp(_bdot_fwd, _bdot_bwd)


def hdot(a, b, ca=1, cb=0):
    return lax.dot_general(a, b, (((ca,), (cb,)), ((), ())), precision=lax.Precision.HIGH,
                           preferred_element_type=F32)


def _rms(x, g):
    return x * lax.rsqrt(jnp.mean(x * x, axis=-1, keepdims=True) + EPS) * g


def _gelu(x):
    return 0.5 * x * (1.0 + jnp.tanh(0.7978845608028654 * (x + 0.044715 * (x * x * x))))


def _log_sigmoid(z):
    return jnp.minimum(z, 0.0) - jnp.log(1.0 + jnp.exp(-jnp.abs(z)))


def _sel(mod, is_lat, idx):
    return jnp.where(is_lat, mod[1, idx:idx + 1, :], mod[0, idx:idx + 1, :])


def _rows_call(name, fn, grid, ins, outs, acc_axes=None, sem=None):
    n_in = len(ins)
    flags = [o[2] for o in outs]
    if acc_axes is None:
        acc_axes = (len(grid) - 1,)

    def body(*refs):
        ids = tuple(pl.program_id(a) for a in range(len(grid)))
        res = fn(ids, *[r[...] for r in refs[:n_in]])
        for r, v, acc in zip(refs[n_in:], res, flags):
            if acc:
                first = functools.reduce(jnp.logical_and, [ids[a] == 0 for a in acc_axes])

                @pl.when(first)
                def _():
                    r[...] = jnp.zeros_like(r)
                r[...] += v.astype(r.dtype)
            else:
                r[...] = v.astype(r.dtype)

    return pl.pallas_call(
        body, name=name, grid=grid, in_specs=[s for _, s in ins], out_specs=[o[1] for o in outs],
        out_shape=[o[0] for o in outs],
        compiler_params=_cp(sem if sem is not None else ("arbitrary",) * len(grid)),
    )(*[a for a, _ in ins])


def _sds(shape, dtype):
    return jax.ShapeDtypeStruct(shape, dtype)


def _rowspec(width, off=0, tm=TM):
    assert off % width == 0
    return pl.BlockSpec((tm, width), lambda i, o=off // width: (i, o))


def _full(shape):
    nd = len(shape)
    return pl.BlockSpec(shape, lambda *a: (0,) * nd)


def _mm(name, a, b, mode, out_dtype, tm_t=1056, tn_t=1408, tk_t=1408):
    halves = a.ndim == 3 or b.ndim == 3
    if mode == "nn":
        (m, k), (_, n) = a.shape, b.shape
    elif mode == "nt":
        (m, k), (n, _) = a.shape[-2:], b.shape
        k *= a.ndim - 1
    else:
        (k, m), (_, n) = a.shape, b.shape[-2:]
        n *= b.ndim - 1
    tm = _tile(m, tm_t, 8 if m % LANE else LANE)
    tn = _tile(n // 2 if halves and mode == "tn" else n, tn_t)
    tk = _tile(k // 2 if halves and mode == "nt" else k, tk_t)
    nk = k // tk
    if mode == "nn":
        dims, a_spec, b_spec = ((1,), (0,)), pl.BlockSpec((tm, tk), lambda i, j, l: (i, l)), pl.BlockSpec((tk, tn), lambda i, j, l: (l, j))
    elif mode == "nt":
        dims, a_spec, b_spec = ((1,), (1,)), pl.BlockSpec((tm, tk), lambda i, j, l: (i, l)), pl.BlockSpec((tn, tk), lambda i, j, l: (j, l))
        if halves:
            a_spec = pl.BlockSpec((None, tm, tk), lambda i, j, l, h=nk // 2: (l // h, i, l % h))
    else:
        dims, a_spec, b_spec = ((0,), (0,)), pl.BlockSpec((tk, tm), lambda i, j, l: (l, i)), pl.BlockSpec((tk, tn), lambda i, j, l: (l, j))
        if halves:
            b_spec = pl.BlockSpec((None, tk, tn), lambda i, j, l, h=n // tn // 2: (j // h, l, j % h))

    def body(a_ref, b_ref, o_ref, *scratch):
        l = pl.program_id(2)
        part = lax.dot_general(a_ref[...].astype(BF16), b_ref[...].astype(BF16), (dims, ((), ())),
                               preferred_element_type=F32)
        if nk == 1:
            o_ref[...] = part.astype(o_ref.dtype)
            return
        acc_ref = scratch[0]

        @pl.when(l == 0)
        def _():
            acc_ref[...] = part

        @pl.when(l > 0)
        def _():
            acc_ref[...] += part

        @pl.when(l == nk - 1)
        def _():
            o_ref[...] = acc_ref[...].astype(o_ref.dtype)

    return pl.pallas_call(
        body, name=name, grid=(m // tm, n // tn, nk), in_specs=[a_spec, b_spec],
        out_specs=pl.BlockSpec((tm, tn), lambda i, j, l: (i, j)), out_shape=_sds((m, n), out_dtype),
        scratch_shapes=[pltpu.VMEM((tm, tn), F32)] if nk > 1 else [],
        compiler_params=_cp(("parallel", "parallel", "arbitrary")),
    )(a, b)


def _nm_fn(is_lat, x, mod, g, shift, scale):
    return _rms(x, g) * (1.0 + _sel(mod, is_lat, scale)) + _sel(mod, is_lat, shift)


def _res_nm_fn(is_lat, x, br, modg, gate, mods, g, shift, scale):
    xn = x + _sel(modg, is_lat, gate) * br
    return xn, _nm_fn(is_lat, xn, mods, g, shift, scale)


def _nm_fwd(name, x, mod, g, shift, scale):
    t = x.shape[0]
    fn = lambda ids, xv, mv, gv: (_nm_fn(ids[0] >= NCB, xv, mv, gv, shift, scale),)
    return _rows_call(name, fn, (t // TM,), [(x, _rowspec(D)), (mod, _full((2, 6, D))), (g, _full((1, D)))],
                      [(_sds((t, D), BF16), _rowspec(D), False)])[0]


def _nm_bwd(name, x, mod, g, shift, scale, dx_res, dh):
    t = x.shape[0]

    def fn(ids, xv, mv, gv, dxr, dhv):
        _, vjp = jax.vjp(lambda a, b, c: _nm_fn(ids[0] >= NCB, a, b, c, shift, scale), xv, mv, gv)
        dx, dm, dg = vjp(dhv)
        return dx + dxr, dm, dg

    return _rows_call(name, fn, (t // TM,),
                      [(x, _rowspec(D)), (mod, _full((2, 6, D))), (g, _full((1, D))), (dx_res, _rowspec(D)), (dh, _rowspec(D))],
                      [(_sds((t, D), F32), _rowspec(D), False), (_sds((2, 6, D), F32), _full((2, 6, D)), True),
                       (_sds((1, D), F32), _full((1, D)), True)])


def _res_nm_fwd(name, x, br, modg, gate, mods, g, shift, scale):
    t = x.shape[0]
    fn = lambda ids, xv, bv, mg, ms, gv: _res_nm_fn(ids[0] >= NCB, xv, bv, mg, gate, ms, gv, shift, scale)
    return _rows_call(name, fn, (t // TM,),
                      [(x, _rowspec(D)), (br, _rowspec(D)), (modg, _full((2, 6, D))), (mods, _full((2, 6, D))), (g, _full((1, D)))],
                      [(_sds((t, D), F32), _rowspec(D), False), (_sds((t, D), BF16), _rowspec(D), False)])


def _res_nm_bwd(name, x, br, modg, gate, mods, g, shift, scale, dx_res, dh):
    t = x.shape[0]

    def fn(ids, xv, bv, mg, ms, gv, dxr, dhv):
        f = lambda a, b, c, d, e: _res_nm_fn(ids[0] >= NCB, a, b, c, gate, d, e, shift, scale)
        _, vjp = jax.vjp(f, xv, bv, mg, ms, gv)
        return vjp((dxr, dhv))

    m26 = (_sds((2, 6, D), F32), _full((2, 6, D)), True)
    return _rows_call(name, fn, (t // TM,),
                      [(x, _rowspec(D)), (br, _rowspec(D)), (modg, _full((2, 6, D))), (mods, _full((2, 6, D))), (g, _full((1, D))),
                       (dx_res, _rowspec(D)), (dh, _rowspec(D))],
                      [(_sds((t, D), F32), _rowspec(D), False), (_sds((t, D), BF16), _rowspec(D), False), m26, m26,
                       (_sds((1, D), F32), _full((1, D)), True)])


def _head(name, x_mid, f, mod, gf, tgt):
    t = x_mid.shape[0]

    def fn(ids, xv, fv, mv, gv, tv):
        def loss_fn(a, b, c, d):
            y = _rms(a + c[1, 5:6, :] * b, d)
            e = y - tv
            return 0.5 * jnp.sum(jnp.mean(e * e, axis=-1))
        loss, grads = jax.value_and_grad(loss_fn, argnums=(0, 1, 2, 3))(xv, fv, mv, gv)
        return tuple(jnp.where(ids[0] >= NCB, v, 0.0) for v in (jnp.reshape(loss, (1, 1)),) + grads)

    return _rows_call(name, fn, (t // TM,),
                      [(x_mid, _rowspec(D)), (f, _rowspec(D)), (mod, _full((2, 6, D))), (gf, _full((1, D))),
                       (tgt, pl.BlockSpec((TM, D), lambda i: (jnp.maximum(i - NCB, 0), 0)))],
                      [(_sds((1, 1), F32), _full((1, 1)), True), (_sds((t, D), F32), _rowspec(D), False),
                       (_sds((t, D), BF16), _rowspec(D), False), (_sds((2, 6, D), F32), _full((2, 6, D)), True),
                       (_sds((1, D), F32), _full((1, D)), True)])


def _gmlp_fn(u, v, g, ws, bst):
    rows = []
    for r in range(u.shape[0] // 128):
        uu, vv = _gelu(u[128 * r:128 * r + 128]), _gelu(v[128 * r:128 * r + 128])
        cols = []
        for gi in range(4):
            sl = slice(128 * gi, 128 * gi + 128)
            f = bdot(ws[gi], _rms(vv[:, sl], g[:, sl]), 1, 0) + bst[:, gi:gi + 1]
            cols.append(uu[:, sl] * f)
        rows.append(jnp.concatenate(cols, axis=-1))
    return jnp.concatenate(rows, axis=0)


def _gmlp_ins(p, g, ws, bst):
    return [(p, _rowspec(GW, OFF["gu"])), (p, _rowspec(GW, OFF["gv"])), (g, _full((1, GW))),
            (ws, _full((4, 128, 128))), (bst, _full((128, 4)))]


def _gmlp_fwd(name, p, g, ws, bst):
    t = p.shape[0]
    return _rows_call(name, lambda ids, *a: (_gmlp_fn(*a),), (t // TM,), _gmlp_ins(p, g, ws, bst),
                      [(_sds((t, GW), BF16), _rowspec(GW), False)])[0]


def _gmlp_bwd(name, p, g, ws, bst, dgm):
    t = p.shape[0]

    def fn(ids, u, v, gv, wv, bv, dv):
        _, vjp = jax.vjp(_gmlp_fn, u, v, gv, wv, bv)
        return vjp(dv)

    return _rows_call(name, fn, (t // TM,), _gmlp_ins(p, g, ws, bst) + [(dgm, _rowspec(GW))],
                      [(_sds((t, GW), BF16), _rowspec(GW), False), (_sds((t, GW), BF16), _rowspec(GW), False),
                       (_sds((1, GW), F32), _full((1, GW)), True), (_sds((4, 128, 128), F32), _full((4, 128, 128)), True),
                       (_sds((128, 4), F32), _full((128, 4)), True)])


def _qk_fn(q, k, gq, gk, cos, sin, seg, perm):
    cq, sq = jnp.concatenate([cos] * 4, axis=-1), jnp.concatenate([sin] * 4, axis=-1)
    qn = q * lax.rsqrt(hdot(q * q, seg) + EPS) * gq
    kn = k * lax.rsqrt(hdot(k * k, seg[:128, :128]) + EPS) * gk
    qr = qn * cq + hdot(qn, perm) * sq
    kr = kn * cos + hdot(kn, perm[:128, :128]) * sin
    return qr * (HD ** -0.5), kr


def _qk_ins(p, gq, gk, cos, sin, seg, perm):
    return [(p, _rowspec(512, OFF["q"])), (p, _rowspec(128, OFF["k"])), (gq, _full((1, 512))), (gk, _full((1, 128))),
            (cos, _rowspec(128)), (sin, _rowspec(128)), (seg, _full((512, 512))), (perm, _full((512, 512)))]


def _qk_fwd(name, p, gq, gk, cos, sin, seg, perm):
    t = p.shape[0]
    fn = lambda ids, q, k, a, b, c, s, sg, pm, v: _qk_fn(q, k, a, b, c, s, sg, pm) + (v,)
    return _rows_call(name, fn, (t // TM,), _qk_ins(p, gq, gk, cos, sin, seg, perm) + [(p, _rowspec(128, OFF["v"]))],
                      [(_sds((t, 512), BF16), _rowspec(512), False), (_sds((t, 128), BF16), _rowspec(128), False),
                       (_sds((t, 128), BF16), _rowspec(128), False)])


def _qk_bwd(name, p, gq, gk, cos, sin, seg, perm, dqr, dkr):
    t = p.shape[0]

    def fn(ids, q, k, a, b, c, s, sg, pm, dq, dk):
        _, vjp = jax.vjp(lambda q_, k_, a_, b_: _qk_fn(q_, k_, a_, b_, c, s, sg, pm), q, k, a, b)
        return vjp((dq, dk))

    return _rows_call(name, fn, (t // TM,),
                      _qk_ins(p, gq, gk, cos, sin, seg, perm) + [(dqr, _rowspec(512)), (dkr, _rowspec(128))],
                      [(_sds((t, 512), BF16), _rowspec(512), False), (_sds((t, 128), BF16), _rowspec(128), False),
                       (_sds((1, 512), F32), _full((1, 512)), True), (_sds((1, 128), F32), _full((1, 128)), True)])


_ATT_TQ = 1024
_ATT_TK = 768


def _attn_fwd(name, q, k, v):
    h, tq_all, _ = q.shape
    hkv, tk_all, _ = k.shape
    tq, tk = _tile(tq_all, _ATT_TQ), _tile(tk_all, _ATT_TK)
    nkc = tk_all // tk

    def body(q_ref, k_ref, v_ref, o_ref, lse_ref):
        qv = q_ref[...].reshape(QG * tq, HD)

        def step(j, carry):
            m, acc = carry
            off = pl.multiple_of(j * tk, tk)
            kk, vv = k_ref[0, pl.ds(off, tk), :], v_ref[0, pl.ds(off, tk), :]
            s = lax.dot_general(qv, kk, (((1,), (1,)), ((), ())), preferred_element_type=F32)
            m_new = jnp.maximum(m, jnp.max(s, axis=-1, keepdims=True))
            pr = jnp.exp(s - m_new)
            acc = jnp.exp(m - m_new) * acc + jnp.dot(pr.astype(BF16), vv, preferred_element_type=F32)
            return m_new, acc

        init = (jnp.full((QG * tq, 1), -jnp.inf, F32), jnp.zeros((QG * tq, 2 * HD), F32))
        m, acc = lax.fori_loop(0, nkc, step, init)
        l = acc[:, HD:HD + 1]
        o_ref[...] = (acc[:, :HD] / l).reshape(QG, tq, HD)
        lse_ref[...] = (m + jnp.log(l)).reshape(QG, tq, 1)

    kv_spec = pl.BlockSpec((1, tk_all, HD), lambda g, i: (g, 0, 0))
    v1_spec = pl.BlockSpec((1, tk_all, 2 * HD), lambda g, i: (g, 0, 0))
    qspec = pl.BlockSpec((QG, tq, HD), lambda g, i: (g, i, 0))
    return pl.pallas_call(
        body, name=name, grid=(hkv, tq_all // tq), in_specs=[qspec, kv_spec, v1_spec],
        out_specs=[qspec, pl.BlockSpec((QG, tq, 1), lambda g, i: (g, i, 0))],
        out_shape=[_sds((h, tq_all, HD), F32), _sds((h, tq_all, 1), F32)],
        compiler_params=_cp(("parallel", "parallel")),
    )(q, k, v)


def _attn_bwd(name, q, k, v, o, do, lse_row):
    h, tq_all, _ = q.shape
    hkv, tk_all, _ = k.shape
    tq, tk = _tile(tq_all, 1024), _tile(tk_all, 1408)

    def body(q_ref, k_ref, v_ref, o_ref, do_ref, lse_ref, dq_ref, dk_ref, dv_ref, dl_ref):
        i, j = pl.program_id(1), pl.program_id(2)

        @pl.when(j == 0)
        def _():
            ones = jnp.ones((8, HD), F32)
            for g in range(QG):
                dl_ref[g] = hdot(ones, do_ref[g] * o_ref[g], 1, 1)

        kk, vv = k_ref[0], v_ref[0]
        dk_acc, dv_acc = jnp.zeros((tk, HD), F32), jnp.zeros((tk, HD), F32)
        for g in range(QG):
            qv, dob = q_ref[g], do_ref[g].astype(BF16)
            st = lax.dot_general(kk, qv, (((1,), (1,)), ((), ())), preferred_element_type=F32)
            pt = jnp.exp(st - lse_ref[g])
            dv_acc += jnp.dot(pt.astype(BF16), dob, preferred_element_type=F32)
            dpt = lax.dot_general(vv, dob, (((1,), (1,)), ((), ())), preferred_element_type=F32)
            dst = (pt * (dpt - dl_ref[g, 0:1, :])).astype(BF16)
            dk_acc += jnp.dot(dst, qv, preferred_element_type=F32)
            dq_part = lax.dot_general(dst, kk, (((0,), (0,)), ((), ())), preferred_element_type=F32)

            @pl.when(j == 0)
            def _():
                dq_ref[g] = dq_part

            @pl.when(j > 0)
            def _():
                dq_ref[g] += dq_part

        rows = pl.ds(pl.multiple_of(j * tk, tk), tk)

        @pl.when(i == 0)
        def _():
            dk_ref[0, rows, :] = dk_acc
            dv_ref[0, rows, :] = dv_acc

        @pl.when(i > 0)
        def _():
            dk_ref[0, rows, :] += dk_acc
            dv_ref[0, rows, :] += dv_acc

    ks = pl.BlockSpec((1, tk, HD), lambda g, i, j: (g, j, 0))
    qs = pl.BlockSpec((QG, tq, HD), lambda g, i, j: (g, i, 0))
    rs = pl.BlockSpec((QG, 1, tq), lambda g, i, j: (g, 0, i))
    full = pl.BlockSpec((1, tk_all, HD), lambda g, i, j: (g, 0, 0))
    return pl.pallas_call(
        body, name=name, grid=(hkv, tq_all // tq, tk_all // tk), in_specs=[qs, ks, ks, qs, qs, rs], out_specs=[qs, full, full],
        out_shape=[_sds((h, tq_all, HD), F32), _sds((hkv, tk_all, HD), F32), _sds((hkv, tk_all, HD), F32)],
        scratch_shapes=[pltpu.VMEM((QG, 8, tq), F32)],
        compiler_params=_cp(("parallel", "arbitrary", "arbitrary")),
    )(q, k, v, o, do, lse_row)


def _decay_fn(a, w2, b2):
    return _log_sigmoid(bdot(a, w2, 1, 0) + b2) / GLA_TAU


def _decay_fwd(name, p, w2, b2):
    t = p.shape[0]
    return _rows_call(name, lambda ids, a, w, b: (_decay_fn(a, w, b),), (t // TM,),
                      [(p, _rowspec(128, OFF["ab"])), (w2, _full((128, 512))), (b2, _full((1, 512)))],
                      [(_sds((t, 512), F32), _rowspec(512), False)])[0]


def _decay_bwd(name, p, w2, b2, gf, gb):
    t = p.shape[0]

    def fn(ids, a, w, b, qf, kf, vf, lf, qb, kb, vb, lb):
        _, vjp = jax.vjp(_decay_fn, a, w, b)
        return vjp(jnp.concatenate([lf, lb], axis=-1)) + (qf + qb, kf + kb, vf + vb)

    widths = (256, 256, 512, 256)
    return _rows_call(name, fn, (t // TM,),
                      [(p, _rowspec(128, OFF["ab"])), (w2, _full((128, 512))), (b2, _full((1, 512)))]
                      + [(g, _rowspec(w)) for g, w in zip(gf, widths)] + [(g, _rowspec(w)) for g, w in zip(gb, widths)],
                      [(_sds((t, 128), BF16), _rowspec(128), False), (_sds((128, 512), F32), _full((128, 512)), True),
                       (_sds((1, 512), F32), _full((1, 512)), True)]
                      + [(_sds((t, w), BF16), _rowspec(w), False) for w in widths[:3]])


def _gla_consts(reverse):
    r = lax.broadcasted_iota(jnp.int32, (GLA_CHUNK, GLA_CHUNK), 0)
    c = lax.broadcasted_iota(jnp.int32, (GLA_CHUNK, GLA_CHUNK), 1)
    trib = (r <= c) if reverse else (r >= c)
    br = lax.broadcasted_iota(jnp.int32, (GLA_QK, GLA_V), 0) // GLA_DK
    bc = lax.broadcasted_iota(jnp.int32, (GLA_QK, GLA_V), 1) // GLA_DV
    lane_head = lax.broadcasted_iota(jnp.int32, (1, GLA_QK), 1) // GLA_DK
    return trib, (br == bc).astype(F32), lane_head


def _gla_chunk(q, k, v, la, s_in, consts):
    trib, bd, lane_head = consts
    cum = hdot(trib.astype(F32), la)
    tot = jnp.sum(la, axis=0, keepdims=True)
    q_in = q * (GLA_DK ** -0.5) * jnp.exp(cum)
    k_in = k * jnp.exp(-cum)
    k_st = k * jnp.exp(tot - cum)
    outs = []
    for h in range(GLA_H):
        att = bdot(jnp.where(lane_head == h, q_in, 0.0), k_in, 1, 1)
        att = jnp.where(trib, att, 0.0)
        outs.append(bdot(att, v[:, GLA_DV * h:GLA_DV * (h + 1)], 1, 0))
    o = jnp.concatenate(outs, axis=-1) + bdot(q_in, s_in, 1, 0)
    decay = jnp.exp(hdot(la, jnp.ones((GLA_CHUNK, LANE), F32), 0, 0))
    s_out = jnp.concatenate([decay] * (GLA_V // LANE), axis=-1) * s_in + bdot(k_st, v, 0, 0) * bd
    return o, s_out


def _gla_order(nb, reverse, backward):
    if not reverse:
        return (lambda s: nb - 1 - s) if backward else (lambda s: s)
    if backward:
        return lambda s: jnp.where(s == nb - 1, 0, s + 1)
    return lambda s: jnp.where(s == 0, 0, nb - s)


_NCH = TM // GLA_CHUNK


def _gla_specs(nb, reverse, backward):
    order = _gla_order(nb, reverse, backward)
    col = lambda width, off: pl.BlockSpec((TM, width), lambda s, o=off // width: (order(s), o))
    state = pl.BlockSpec((_NCH, GLA_H, GLA_DK, GLA_DV), lambda s: (order(s), 0, 0, 0))
    qkvla = [col(256, OFF["glq"]), col(256, OFF["glk"]), col(512, OFF["glv"]), col(256, 256 * int(reverse))]
    return col, state, qkvla


def _gla_fwd(name, p, la):
    t = p.shape[0]
    nb = t // TM

    def body(*refs):
        ins, outs, scr = (refs[0:4], refs[4:8]), (refs[8:10], refs[10:12]), refs[12:14]

        @pl.when(pl.program_id(0) == 0)
        def _():
            for s_ref in scr:
                s_ref[...] = jnp.zeros_like(s_ref)

        for step in range(_NCH):
            for d in range(2):
                (q_ref, k_ref, v_ref, la_ref), (o_ref, sv_ref), s_ref = ins[d], outs[d], scr[d]
                c = _NCH - 1 - step if d else step
                rows = slice(GLA_CHUNK * c, GLA_CHUNK * (c + 1))
                s_in = s_ref[...]
                for h in range(GLA_H):
                    sv_ref[c, h] = s_in[GLA_DK * h:GLA_DK * (h + 1), GLA_DV * h:GLA_DV * (h + 1)]
                o, s_out = _gla_chunk(q_ref[rows, :], k_ref[rows, :], v_ref[rows, :], la_ref[rows, :], s_in,
                                      _gla_consts(bool(d)))
                o_ref[rows, :] = o
                s_ref[...] = s_out

    in_specs, out_specs, out_shape = [], [], []
    for d in range(2):
        col, state, qkvla = _gla_specs(nb, bool(d), False)
        in_specs += qkvla
        out_specs += [col(512, 0), state]
        out_shape += [_sds((t, GLA_V), F32), _sds((t // GLA_CHUNK, GLA_H, GLA_DK, GLA_DV), F32)]
    return pl.pallas_call(
        body, name=name, grid=(nb,), in_specs=in_specs, out_specs=out_specs, out_shape=out_shape,
        scratch_shapes=[pltpu.VMEM((GLA_QK, GLA_V), F32)] * 2, compiler_params=_cp(("arbitrary",)),
    )(p, p, p, la, p, p, p, la)


def _gla_bwd(name, p, la, sv_f, sv_b, do):
    t = p.shape[0]
    nb = t // TM

    def body(*refs):
        ins, outs, scr = (refs[0:6], refs[6:12]), (refs[12:16], refs[16:20]), refs[20:22]

        @pl.when(pl.program_id(0) == 0)
        def _():
            for ds_ref in scr:
                ds_ref[...] = jnp.zeros_like(ds_ref)

        zero = jnp.zeros((GLA_DK, GLA_DV), F32)
        for step in range(_NCH):
            for d in range(2):
                (q_ref, k_ref, v_ref, la_ref, sv_ref, do_ref), out_refs, ds_ref = ins[d], outs[d], scr[d]
                c = step if d else _NCH - 1 - step
                rows = slice(GLA_CHUNK * c, GLA_CHUNK * (c + 1))
                s_in = jnp.concatenate(
                    [jnp.concatenate([sv_ref[c, h] if hh == h else zero for hh in range(GLA_H)], axis=-1)
                     for h in range(GLA_H)], axis=0)
                consts = _gla_consts(bool(d))
                _, vjp = jax.vjp(lambda a, b, cc, dd, e: _gla_chunk(a, b, cc, dd, e, consts),
                                 q_ref[rows, :], k_ref[rows, :], v_ref[rows, :], la_ref[rows, :], s_in)
                grads = vjp((do_ref[rows, :], ds_ref[...]))
                for o_ref, g in zip(out_refs, grads[:4]):
                    o_ref[rows, :] = g
                ds_ref[...] = grads[4]

    ins, in_specs, out_specs, out_shape = [], [], [], []
    for d, sv in enumerate((sv_f, sv_b)):
        col, state, qkvla = _gla_specs(nb, bool(d), True)
        ins += [p, p, p, la, sv, do]
        in_specs += qkvla + [state, col(512, 0)]
        out_specs += [col(256, 0), col(256, 0), col(512, 0), col(256, 0)]
        out_shape += [_sds((t, GLA_QK), F32), _sds((t, GLA_QK), F32), _sds((t, GLA_V), F32), _sds((t, GLA_QK), F32)]
    return pl.pallas_call(
        body, name=name, grid=(nb,), in_specs=in_specs, out_specs=out_specs, out_shape=out_shape,
        scratch_shapes=[pltpu.VMEM((GLA_QK, GLA_V), F32)] * 2, compiler_params=_cp(("arbitrary",)),
    )(*ins)


def _gla_out_fn(of, ob, r, g):
    o = of + ob
    cols = [_rms(o[:, GLA_DV * h:GLA_DV * (h + 1)], g[:, GLA_DV * h:GLA_DV * (h + 1)]) for h in range(GLA_H)]
    return jnp.concatenate(cols, axis=-1) * jax.nn.silu(r)


def _gla_out_fwd(name, of, ob, p, g):
    t = p.shape[0]
    return _rows_call(name, lambda ids, *a: (_gla_out_fn(*a),), (t // TM,),
                      [(of, _rowspec(512)), (ob, _rowspec(512)), (p, _rowspec(512, OFF["gr"])), (g, _full((1, 512)))],
                      [(_sds((t, 512), BF16), _rowspec(512), False)])[0]


def _gla_out_bwd(name, of, ob, p, g, dgla):
    t = p.shape[0]

    def fn(ids, a, b, r, gv, dv):
        _, vjp = jax.vjp(_gla_out_fn, a, b, r, gv)
        do, _, dr, dg = vjp(dv)
        return do, dr, dg

    return _rows_call(name, fn, (t // TM,),
                      [(of, _rowspec(512)), (ob, _rowspec(512)), (p, _rowspec(512, OFF["gr"])), (g, _full((1, 512))),
                       (dgla, _rowspec(512))],
                      [(_sds((t, 512), F32), _rowspec(512), False), (_sds((t, 512), BF16), _rowspec(512), False),
                       (_sds((1, 512), F32), _full((1, 512)), True)])


_TMM = 384


def _merge_fwd(name, gm, att, gla, wa, wb, wc, p):
    t = p.shape[0]
    row = lambda w, off=0: pl.BlockSpec((_TMM, w), lambda i, o=off // w: (i, o))

    def fn(ids, a, b, c, wa_, wb_, wc_, ga, gb, gc):
        return (jax.nn.sigmoid(ga) * bdot(a, wa_, 1, 0) + jax.nn.sigmoid(gb) * bdot(b, wb_, 1, 0)
                + jax.nn.sigmoid(gc) * bdot(c, wc_, 1, 0),)

    return _rows_call(name, fn, (t // _TMM,),
                      [(gm, row(512)), (att, row(512)), (gla, row(512)), (wa, _full((512, D))), (wb, _full((512, D))),
                       (wc, _full((512, D))), (p, row(D, OFF["gA"])), (p, row(D, OFF["gB"])), (p, row(D, OFF["gC"]))],
                      [(_sds((t, D), BF16), row(D), False)])[0]


def _merge_bwd(name, gm, att, gla, wa, wb, wc, p, dmerged):
    t = p.shape[0]
    row = lambda w, off=0: pl.BlockSpec((_TMM, w), lambda i, o=off // w: (i, o))

    def fn(ids, a, b, c, wa_, wb_, wc_, ga, gb, gc, dm):
        outs_y, outs_g = [], []
        for br, w, g in ((a, wa_, ga), (b, wb_, gb), (c, wc_, gc)):
            s = jax.nn.sigmoid(g)
            outs_y.append(dm * s)
            outs_g.append(dm * bdot(br, w, 1, 0) * s * (1.0 - s))
        return tuple(outs_y) + tuple(outs_g)

    o = (_sds((t, D), BF16), row(D), False)
    return _rows_call(name, fn, (t // _TMM,),
                      [(gm, row(512)), (att, row(512)), (gla, row(512)), (wa, _full((512, D))), (wb, _full((512, D))),
                       (wc, _full((512, D))), (p, row(D, OFF["gA"])), (p, row(D, OFF["gB"])), (p, row(D, OFF["gC"])),
                       (dmerged, row(D))], [o] * 6)


_TNC = 1408
_NJ = FFN // _TNC


def _shift_rows(x, prev8, next8, vp, vn):
    n = x.shape[0]
    rid = lax.broadcasted_iota(jnp.int32, x.shape, 0)
    xp = jnp.where(rid == 0, jnp.where(vp, prev8[7:8, :], 0.0), pltpu.roll(x, 1, 0))
    xn = jnp.where(rid == n - 1, jnp.where(vn, next8[0:1, :], 0.0), pltpu.roll(x, n - 1, 0))
    return xp, xn


def _seq_edges(i, t):
    start, end = i * TM, (i + 1) * TM
    return jnp.logical_and(start != 0, start != TC), jnp.logical_and(end != TC, end != t)


def _halo_specs(t, colmap):
    r8 = TM // 8
    main = pl.BlockSpec((TM, _TNC), lambda j, i: (i, colmap(j)))
    prev = pl.BlockSpec((8, _TNC), lambda j, i: (jnp.maximum(i * r8 - 1, 0), colmap(j)))
    nxt = pl.BlockSpec((8, _TNC), lambda j, i: (jnp.minimum((i + 1) * r8, t // 8 - 1), colmap(j)))
    return [main, prev, nxt]


def _conv3(x, xp, xn, w, b=None):
    y = xp * w[0:1, :] + x * w[1:2, :] + xn * w[2:3, :]
    return y if b is None else b + y


def _conv_fwd(name, a, cw, cb):
    t = a.shape[0]

    def fn(ids, ag, agp, agn, av, avp, avn, wg, wv, bg, bv):
        vp, vn = _seq_edges(ids[1], t)
        cg = _conv3(ag, *_shift_rows(ag, agp, agn, vp, vn), wg, bg)
        cv = _conv3(av, *_shift_rows(av, avp, avn, vp, vn), wv, bv)
        return (jax.nn.silu(cg) * cv,)

    gcol, vcol = (lambda j: j), (lambda j: j + _NJ)
    wspec = lambda cm: pl.BlockSpec((3, _TNC), lambda j, i: (0, cm(j)))
    bspec = lambda cm: pl.BlockSpec((1, _TNC), lambda j, i: (0, cm(j)))
    ins = [(a, s) for s in _halo_specs(t, gcol) + _halo_specs(t, vcol)]
    ins += [(cw, wspec(gcol)), (cw, wspec(vcol)), (cb, bspec(gcol)), (cb, bspec(vcol))]
    return _rows_call(name, fn, (_NJ, t // TM), ins,
                      [(_sds((t, FFN), BF16), pl.BlockSpec((TM, _TNC), lambda j, i: (i, j)), False)])[0]


def _conv_bwd(name, a, cw, cb, dact):
    t = a.shape[0]
    n = TM + 16

    def fn(ids, ag, agp, agn, av, avp, avn, dv, dvp, dvn, wg, wv, bg, bv):
        vp, vn = _seq_edges(ids[1], t)
        ext = lambda x, xp, xn: jnp.concatenate([jnp.where(vp, xp, 0.0), x, jnp.where(vn, xn, 0.0)], axis=0)
        up, dn = (lambda x: pltpu.roll(x, 1, 0)), (lambda x: pltpu.roll(x, n - 1, 0))
        main = lambda y: y[8:8 + TM]
        eg, ev, ed = ext(ag, agp, agn), ext(av, avp, avn), ext(dv, dvp, dvn)
        cg = _conv3(eg, up(eg), dn(eg), wg, bg)
        cv = _conv3(ev, up(ev), dn(ev), wv, bv)
        s = jax.nn.sigmoid(cg)
        rid = lax.broadcasted_iota(jnp.int32, (3, eg.shape[1]), 0)
        das, dws, dbs = [], [], []
        for dc, w, x in ((ed * cv * s * (1.0 + cg * (1.0 - s)), wg, eg), (ed * cg * s, wv, ev)):
            das.append(main(dn(dc) * w[0:1, :] + dc * w[1:2, :] + up(dc) * w[2:3, :]))
            dcm = main(dc)
            sums = [jnp.sum(dcm * main(y), axis=0, keepdims=True) for y in (up(x), x, dn(x))]
            dws.append(jnp.where(rid == 0, sums[0], jnp.where(rid == 1, sums[1], sums[2])))
            dbs.append(jnp.sum(dcm, axis=0, keepdims=True))
        return jnp.stack(das), jnp.stack(dws), jnp.stack(dbs)

    gcol, vcol = (lambda j: j), (lambda j: j + _NJ)
    wspec = lambda cm: pl.BlockSpec((3, _TNC), lambda j, i: (0, cm(j)))
    bspec = lambda cm: pl.BlockSpec((1, _TNC), lambda j, i: (0, cm(j)))
    ins = [(a, s) for s in _halo_specs(t, gcol) + _halo_specs(t, vcol)] + [(dact, s) for s in _halo_specs(t, gcol)]
    ins += [(cw, wspec(gcol)), (cw, wspec(vcol)), (cb, bspec(gcol)), (cb, bspec(vcol))]
    return _rows_call(name, fn, (_NJ, t // TM), ins,
                      [(_sds((2, t, FFN), BF16), pl.BlockSpec((2, TM, _TNC), lambda j, i: (0, i, j)), False),
                       (_sds((2, 3, FFN), F32), pl.BlockSpec((2, 3, _TNC), lambda j, i: (0, 0, j)), True),
                       (_sds((2, 1, FFN), F32), pl.BlockSpec((2, 1, _TNC), lambda j, i: (0, 0, j)), True)])


_TNA = 512


def _adaln_fwd(name, cond, w, b):
    fn = lambda ids, cv, wv, bv: ((bdot(jax.nn.silu(cv), wv[0], 1, 0) + bv[0])[None],)
    return _rows_call(name, fn, (2, ADA_LOC // _TNA),
                      [(cond, _full((16, D))), (w, pl.BlockSpec((1, D, _TNA), lambda l, j: (l, 0, j))),
                       (b, pl.BlockSpec((1, 1, _TNA), lambda l, j: (l, 0, j)))],
                      [(_sds((2, 16, ADA_LOC), F32), pl.BlockSpec((1, 16, _TNA), lambda l, j: (l, 0, j)), False)])[0]


def _adaln_bwd(name, c8, cc8, w, dl, dc):
    def fn(ids, cv, ccv, wv, dlv, dcv):
        dcs = jnp.broadcast_to(jnp.sum(dcv[0], axis=0, keepdims=True), dcv[0].shape)
        dw = hdot(jax.nn.silu(cv), dlv[0], 0, 0) + hdot(jax.nn.silu(ccv), dcs, 0, 0)
        s = jax.nn.sigmoid(ccv)
        rid = lax.broadcasted_iota(jnp.int32, ccv.shape, 0)
        dcc = jnp.where(rid == 0, bdot(dcs, wv[0], 1, 1) * s * (1.0 + ccv * (1.0 - s)), 0.0)
        return dw[None], dcc

    dspec = pl.BlockSpec((1, 8, _TNA), lambda l, j: (l, 0, j))
    return _rows_call(name, fn, (2, ADA_LOC // _TNA),
                      [(c8, _full((8, D))), (cc8, _full((8, D))), (w, pl.BlockSpec((1, D, _TNA), lambda l, j: (l, 0, j))),
                       (dl, dspec), (dc, dspec)],
                      [(_sds((2, D, ADA_LOC), F32), pl.BlockSpec((1, D, _TNA), lambda l, j: (l, 0, j)), False),
                       (_sds((8, D), F32), _full((8, D)), True)], acc_axes=(0, 1))


def _adamw_fn(w, g, m, v):
    m = ADAM_B1 * m + (1.0 - ADAM_B1) * g
    v = ADAM_B2 * v + (1.0 - ADAM_B2) * (g * g)
    m_hat = m / (1.0 - ADAM_B1 ** ADAM_STEP)
    v_hat = v / (1.0 - ADAM_B2 ** ADAM_STEP)
    return -ADAM_LR * (m_hat / (jnp.sqrt(v_hat) + ADAM_EPS) + ADAM_WD * w), m, v


def _adamw(name, w, g, m, v):
    r, c = w.shape
    tr = _tile(r, max(8, (1 << 20) // (4 * c)), 8)
    spec = pl.BlockSpec((tr, c), lambda i: (i, 0))
    o = (_sds((r, c), F32), spec, False)
    return _rows_call(name, lambda ids, *a: _adamw_fn(*a), (r // tr,), [(x, spec) for x in (w, g, m, v)], [o, o, o],
                      sem=("parallel",))


def _coords():
    return lax.axis_index("x"), lax.axis_index("y"), lax.axis_index("c")


def _other_chips(x, y):
    return [(1 - x, y), (x, 1 - y), (1 - x, 1 - y)]


def _allgather_small(name, blk):
    m_per, n = blk.shape

    def body(x_ref, out_ref, send_sems, recv_sems, local_sem):
        x, y, c = _coords()
        me, sibling = (x, y, c), (x, y, 1 - c)
        chips = _other_chips(x, y)

        def rows(px, py, pc):
            return out_ref.at[pl.ds((4 * px + 2 * py + pc) * m_per, m_per), :]

        def copy(k, block, to, src=None):
            return pltpu.make_async_remote_copy(
                src_ref=rows(*block) if src is None else src, dst_ref=rows(*block), send_sem=send_sems.at[k],
                recv_sem=recv_sems.at[k], device_id=to, device_id_type=MESH)

        mine = pltpu.make_async_copy(x_ref, rows(*me), local_sem)
        mine.start()
        first = [copy(0, me, sibling, src=x_ref)]
        first += [copy(1 + j, me, (*chip, c), src=x_ref) for j, chip in enumerate(chips)]
        for cp in first:
            cp.start()
        passed = [copy(4 + j, (*chip, c), sibling) for j, chip in enumerate(chips)]
        for j, chip in enumerate(chips):
            copy(1 + j, (*chip, c), me).wait_recv()
            passed[j].start()
        copy(0, sibling, me).wait_recv()
        for j, chip in enumerate(chips):
            copy(4 + j, (*chip, 1 - c), me).wait_recv()
        for cp in first + passed:
            cp.wait_send()
        mine.wait()

    return pl.pallas_call(
        body, name=name, out_shape=_sds((N_DEV * m_per, n), blk.dtype),
        in_specs=[pl.BlockSpec(memory_space=pltpu.VMEM)], out_specs=pl.BlockSpec(memory_space=pltpu.VMEM),
        scratch_shapes=[pltpu.SemaphoreType.DMA((7,)), pltpu.SemaphoreType.DMA((7,)), pltpu.SemaphoreType.DMA],
        compiler_params=pltpu.CompilerParams(vmem_limit_bytes=VMEM_LIMIT),
    )(blk)


_ANY = pl.BlockSpec(memory_space=pl.ANY)


def _remote(src, dst, send_sems, recv_sems, s, to):
    return pltpu.make_async_remote_copy(src_ref=src, dst_ref=dst, send_sem=send_sems.at[s], recv_sem=recv_sems.at[s],
                                        device_id=to, device_id_type=MESH)


def _comm_call(name, body, ins, out_shapes, n_sems, n_local):
    return pl.pallas_call(
        body, name=name, out_shape=out_shapes, in_specs=[_ANY] * len(ins), out_specs=[_ANY] * len(out_shapes),
        scratch_shapes=[pltpu.SemaphoreType.DMA((n_sems,)), pltpu.SemaphoreType.DMA((n_sems,)),
                        pltpu.SemaphoreType.DMA((n_local,))],
    )(*ins)


def _allgather_layers(name, locs):
    n = len(locs)

    def body(*refs):
        ins, outs, (send_sems, recv_sems, local_sems) = refs[:n], refs[n:2 * n], refs[2 * n:]
        x, y, c = _coords()
        k = 2 * x + y
        sibling = (x, y, 1 - c)
        chips = _other_chips(x, y)
        first = [_remote(ins[t].at[c], outs[t].at[k, c], send_sems, recv_sems, 6 * t + j, (*chip, c))
                 for t in range(n) for j, chip in enumerate(chips)]
        for cp in first:
            cp.start()
        passed = []
        for t in range(n):
            for j, (cx, cy) in enumerate(chips):
                there = outs[t].at[2 * cx + cy, c]
                _remote(there, there, send_sems, recv_sems, 6 * t + j, sibling).wait_recv()
                passed.append(_remote(there, there, send_sems, recv_sems, 6 * t + 3 + j, sibling))
                passed[-1].start()
        for t in range(n):
            for j, (cx, cy) in enumerate(chips):
                there = outs[t].at[2 * cx + cy, 1 - c]
                _remote(there, there, send_sems, recv_sems, 6 * t + 3 + j, sibling).wait_recv()
        for cp in first + passed:
            cp.wait_send()

    outs = _comm_call(name, body, locs, [_sds((N_CHIP,) + a.shape, a.dtype) for a in locs], 6 * n, 1)
    k = 2 * lax.axis_index("x") + lax.axis_index("y")
    return [lax.dynamic_update_slice_in_dim(o, a[None], k, axis=0) for o, a in zip(outs, locs)]


def _rs_pair_exchange(name, g0, g1):
    n = len(g0)

    def body(*refs):
        a0, a1, outs, (send_sems, recv_sems, _) = refs[:n], refs[n:2 * n], refs[2 * n:3 * n], refs[3 * n:]
        x, y, c = _coords()

        def run(srcs):
            cps = [_remote(srcs[t], outs[t], send_sems, recv_sems, t, (x, y, 1 - c)) for t in range(n)]
            for cp in cps:
                cp.start()
            for cp in cps:
                cp.wait()

        pl.when(c == 0)(lambda: run(a1))
        pl.when(c == 1)(lambda: run(a0))

    return _comm_call(name, body, list(g0) + list(g1), [_sds(a.shape, a.dtype) for a in g0], n, 1)


def _ew2d(name, fn, ins, out_dtype):
    shape = ins[0].shape
    r, c = _prod(shape[:-1]), shape[-1]
    tr = _tile(r, max(8, (1 << 20) // (4 * c)), 8)
    spec = pl.BlockSpec((tr, c), lambda i: (i, 0))
    out = _rows_call(name, lambda ids, *a: (fn(*a),), (r // tr,), [(a.reshape(r, c), spec) for a in ins],
                     [(_sds((r, c), out_dtype), spec, False)], sem=("parallel",))[0]
    return out.reshape(shape)


def _rs_chip_exchange(name, s1):
    n = len(s1)

    def body(*refs):
        ins, outs, (send_sems, recv_sems, local_sems) = refs[:n], refs[n:2 * n], refs[2 * n:]
        x, y, c = _coords()
        k = 2 * x + y
        chips = _other_chips(x, y)
        cps = [_remote(ins[t].at[2 * cx + cy], outs[t].at[k], send_sems, recv_sems, 3 * t + j, (cx, cy, c))
               for t in range(n) for j, (cx, cy) in enumerate(chips)]
        for cp in cps:
            cp.start()
        for t in range(n):
            for j, (cx, cy) in enumerate(chips):
                there = outs[t].at[2 * cx + cy]
                _remote(there, there, send_sems, recv_sems, 3 * t + j, (cx, cy, c)).wait_recv()
        for cp in cps:
            cp.wait_send()

    outs = _comm_call(name, body, s1, [_sds(a.shape, a.dtype) for a in s1], 3 * n, 1)
    k = 2 * lax.axis_index("x") + lax.axis_index("y")
    own = [lax.dynamic_index_in_dim(a, k, axis=0, keepdims=True) for a in s1]
    return [lax.dynamic_update_slice_in_dim(o, a, k, axis=0) for o, a in zip(outs, own)]


def _sum_slots(name, a):
    s, r, cdim = a.shape
    tr = _tile(r, 512, 8)

    def fn(ids, av):
        tot = av[0]
        for i in range(1, s):
            tot = tot + av[i]
        return (tot,)

    return _rows_call(name, fn, (r // tr,), [(a, pl.BlockSpec((s, tr, cdim), lambda i: (0, i, 0)))],
                      [(_sds((r, cdim), F32), pl.BlockSpec((tr, cdim), lambda i: (i, 0)), False)], sem=("parallel",))[0]


def _pair_allgather(name, halves):
    n = len(halves)

    def body(*refs):
        ins, outs, (send_sems, recv_sems, local_sems) = refs[:n], refs[n:2 * n], refs[2 * n:]
        x, y, c = _coords()
        cps = [_remote(ins[t], outs[t].at[c], send_sems, recv_sems, t, (x, y, 1 - c)) for t in range(n)]
        for cp in cps:
            cp.start()
        for t in range(n):
            _remote(ins[t], outs[t].at[1 - c], send_sems, recv_sems, t, (x, y, 1 - c)).wait_recv()
        for cp in cps:
            cp.wait_send()

    outs = _comm_call(name, body, halves, [_sds((2,) + a.shape, a.dtype) for a in halves], n, 1)
    return [lax.dynamic_update_slice_in_dim(o, a[None], lax.axis_index("c"), axis=0) for o, a in zip(outs, halves)]


def _reduce_scatter(g0, g1):
    n = len(g0)
    got = _rs_pair_exchange("rs_pair_exchange", g0, g1)
    keep = lambda a, b, r: jnp.where(lax.axis_index("c") == 0, a, b) + r
    s1 = [_ew2d("rs_pair_add_%d" % t, keep, [g0[t], g1[t], got[t]], BF16) for t in range(n)]
    slots = _rs_chip_exchange("rs_chip_exchange", s1)
    red = [_sum_slots("rs_chip_sum_%d" % t, a.reshape(N_CHIP, -1, a.shape[-1])).reshape(a.shape[1:])
           for t, a in enumerate(slots)]
    return _pair_allgather("rs_pair_allgather", red)


PACK_C = 1024
_SHARDED = (("w_in", 1), ("w_br_a", 1), ("w_br_b", 1), ("w_br_c", 1), ("w_out", 0), ("w_ffn_up", 1), ("w_ffn_down", 0))
_SHARDED_SMALL = (("conv_w", (3, 2 * FFN), 1), ("w_alpha2", (2, 16, GLA_QK), 2), ("b_alpha", (2, GLA_QK), 1))


def _prod(shape):
    n = 1
    for s in shape:
        n *= s
    return n


def _to_blocks(full, axis):
    shp = full.shape
    split = full.reshape(shp[:axis] + (N_CHIP, shp[axis] // N_CHIP) + shp[axis + 1:])
    return jnp.moveaxis(split, axis, 0)


def _from_blocks(blocks, axis):
    return jnp.concatenate([blocks[k] for k in range(N_CHIP)], axis=axis)


def _rope_tables(tx):
    pos = jnp.arange(tx, dtype=jnp.int32)
    inv_freq = 10000.0 ** (-jnp.arange(16, dtype=F32) / 16)
    ang_r = (pos // GRID_W).astype(F32)[:, None] * inv_freq
    ang_c = (pos % GRID_W).astype(F32)[:, None] * inv_freq
    ang = jnp.concatenate([ang_r, ang_r, ang_c, ang_c], axis=-1)
    sign = jnp.concatenate([-jnp.ones((16,), F32), jnp.ones((16,), F32)] * 2)
    cos = jnp.concatenate([jnp.ones((TC, HD), F32), jnp.cos(ang)], axis=0)
    sin = jnp.concatenate([jnp.zeros((TC, HD), F32), jnp.sin(ang) * sign], axis=0)
    return jnp.tile(cos, (1, 2)), jnp.tile(sin, (1, 2))


def _lane_consts():
    l = jnp.arange(512)
    seg = (l[:, None] // HD == l[None, :] // HD).astype(F32) / HD
    partner = jnp.where(l % 32 < 16, l + 16, l - 16)
    perm = (l[:, None] == partner[None, :]).astype(F32)
    return seg, perm


def _heads(a, n):
    return a.reshape(a.shape[0], n, HD).transpose(1, 0, 2)


def _unheads(a):
    return a.transpose(1, 0, 2).reshape(a.shape[1], a.shape[0] * HD)


def _gather_f32_shards(shards):
    sizes = [_prod(a.shape) for a in shards]
    flat = jnp.concatenate([a.reshape(-1) for a in shards] + [jnp.zeros((16 * PACK_C - sum(sizes),), F32)])
    got = _allgather_small("gather_f32_shards", flat.reshape(16, PACK_C)).reshape(N_CHIP, 2, 16 * PACK_C)[:, 0]
    out, o = {}, 0
    for (n, _, ax), a, sz in zip(_SHARDED_SMALL, shards, sizes):
        out[n] = jnp.concatenate([got[k, o:o + sz].reshape(a.shape) for k in range(N_CHIP)], axis=ax + 1)
        o += sz
    return out


def _layer_params(l, wfull, small):
    w2 = small["w_alpha2_full"][l]
    w2pad = jnp.zeros((128, 512), F32).at[0:16, 0:256].set(w2[0]).at[16:32, 256:512].set(w2[1])
    full = {n: _from_blocks(wfull[n][:, l], ax) for n, ax in _SHARDED}
    return dict(
        w_in=_to_new_cols(full["w_in"]), wa=full["w_br_a"], wb=full["w_br_b"], wc=full["w_br_c"],
        w_out=full["w_out"], w_up=full["w_ffn_up"], w_down=full["w_ffn_down"],
        cw=small["conv_w_full"][l], cb=small["conv_b"][l][None], w2=w2pad,
        b2=small["b_alpha_full"][l].reshape(1, 512),
        g1=small["norm1_g"][l][None], g2=small["norm2_g"][l][None], gq=jnp.tile(small["q_norm_g"][l], 8)[None],
        gk=jnp.tile(small["k_norm_g"][l], 2)[None], ggm=small["gmlp_norm_g"][l][None], ws=small["w_spatial"][l],
        bst=small["b_spatial"][l].T, ggl=small["gla_norm_g"][l][None])


def _layer_fwd(l, last, x, h1, mod, P, tabs):
    cos, sin, seg, perm = tabs
    n = "l%d_" % l
    s = dict(x=x, h1=h1)
    p = _mm(n + "in_proj", h1, P["w_in"], "nn", F32, tm_t=768, tn_t=2176)
    s["p"] = p
    s["gm"] = _gmlp_fwd(n + "gmlp", p, P["ggm"], P["ws"], P["bst"])
    qr, kr, vb = _qk_fwd(n + "qk_prep", p, P["gq"], P["gk"], cos, sin, seg, perm)
    qh, kh, vh = _heads(qr, NQ), _heads(kr, NKV), _heads(vb, NKV)
    s["qh"], s["kh"], s["vh"] = qh, kh, vh
    one_hot = (jnp.arange(HD) == 0).astype(BF16)
    v1 = jnp.concatenate([vh, jnp.broadcast_to(one_hot, vh.shape)], axis=-1)
    ox, lse_x = _attn_fwd(n + "attn_x", qh[:, TC:], kh, v1)
    s["ox"], s["lse_x"] = ox, lse_x
    if last:
        oc = jnp.zeros((NQ, TC, HD), F32)
    else:
        oc, lse_c = _attn_fwd(n + "attn_c", qh[:, :TC], kh[:, :TC], v1[:, :TC])
        s["oc"], s["lse_c"] = oc, lse_c
    s["att"] = _unheads(jnp.concatenate([oc, ox], axis=1)).astype(BF16)
    la = _decay_fwd(n + "gla_decay", p, P["w2"], P["b2"])
    s["la"] = la
    s["of"], s["sf"], s["ob"], s["sb"] = _gla_fwd(n + "gla_scan", p, la)
    s["gla"] = _gla_out_fwd(n + "gla_out", s["of"], s["ob"], p, P["ggl"])
    s["merged"] = _merge_fwd(n + "merge", s["gm"], s["att"], s["gla"], P["wa"], P["wb"], P["wc"], p)
    s["mix"] = _mm(n + "out_proj", s["merged"], P["w_out"], "nn", F32)
    s["x_mid"], s["h2"] = _res_nm_fwd(n + "res1_norm2", x, s["mix"], mod, 2, mod, P["g2"], 3, 4)
    s["a"] = _mm(n + "ffn_up", s["h2"], P["w_up"], "nn", F32)
    s["act"] = _conv_fwd(n + "conv_gate", s["a"], P["cw"], P["cb"])
    s["f"] = _mm(n + "ffn_down", s["act"], P["w_down"], "nn", F32)
    return s


def _layer_bwd(l, last, s, mod, P, tabs, dx_mid, df, gw):
    cos, sin, seg, perm = tabs
    n = "l%d_b_" % l
    t = dx_mid.shape[0]
    p = s["p"]
    gw["w_ffn_down"] = _mm(n + "ffn_down_w", s["act"], df, "tn", F32, tm_t=1408)
    dact = _mm(n + "ffn_down_x", df, P["w_down"], "nt", F32)
    da, dcw, dcb = _conv_bwd(n + "conv_gate", s["a"], P["cw"], P["cb"], dact)
    gw["conv_w"], gw["conv_b"] = dcw.transpose(1, 0, 2).reshape(3, 2 * FFN), dcb.reshape(2 * FFN)
    gw["w_ffn_up"] = _mm(n + "ffn_up_w", s["h2"], da, "tn", F32)
    dh2 = _mm(n + "ffn_up_x", da, P["w_up"], "nt", F32)
    dx, dmix, dmod_a, dmod_b, dg2 = _res_nm_bwd(n + "res1_norm2", s["x"], s["mix"], mod, 2, mod, P["g2"], 3, 4, dx_mid, dh2)
    dmod = dmod_a + dmod_b
    gw["norm2_g"] = dg2[0]
    gw["w_out"] = _mm(n + "out_proj_w", s["merged"], dmix, "tn", F32)
    dmerged = _mm(n + "out_proj_x", dmix, P["w_out"], "nt", F32)
    dya, dyb, dyc, dga, dgb, dgc = _merge_bwd(n + "merge", s["gm"], s["att"], s["gla"], P["wa"], P["wb"], P["wc"], p, dmerged)
    gw["w_br_a"] = _mm(n + "br_a_w", s["gm"], dya, "tn", F32)
    gw["w_br_b"] = _mm(n + "br_b_w", s["att"], dyb, "tn", F32)
    gw["w_br_c"] = _mm(n + "br_c_w", s["gla"], dyc, "tn", F32)
    dgm = _mm(n + "br_a_x", dya, P["wa"], "nt", F32)
    datt = _mm(n + "br_b_x", dyb, P["wb"], "nt", F32)
    dgla = _mm(n + "br_c_x", dyc, P["wc"], "nt", F32)
    du, dv_g, dggm, dws, dbst = _gmlp_bwd(n + "gmlp", p, P["ggm"], P["ws"], P["bst"], dgm)
    gw["gmlp_norm_g"], gw["w_spatial"], gw["b_spatial"] = dggm[0], dws, dbst.T
    doh = _heads(datt, NQ)
    qh, kh, vh = s["qh"], s["kh"], s["vh"]
    row = lambda a: a.reshape(a.shape[0], 1, a.shape[1])
    dqx, dkh, dvh = _attn_bwd(n + "attn_x", qh[:, TC:], kh, vh, s["ox"], doh[:, TC:], row(s["lse_x"]))
    if last:
        dqc = jnp.zeros((NQ, TC, HD), F32)
    else:
        dqc, dkc, dvc = _attn_bwd(n + "attn_c", qh[:, :TC], kh[:, :TC], vh[:, :TC], s["oc"], doh[:, :TC], row(s["lse_c"]))
        pad = jnp.zeros((NKV, t - TC, HD), F32)
        dkh = dkh + jnp.concatenate([dkc, pad], axis=1)
        dvh = dvh + jnp.concatenate([dvc, pad], axis=1)
    dqr = _unheads(jnp.concatenate([dqc, dqx], axis=1))
    dq, dk, dgq, dgk = _qk_bwd(n + "qk_prep", p, P["gq"], P["gk"], cos, sin, seg, perm, dqr, _unheads(dkh))
    gw["q_norm_g"], gw["k_norm_g"] = dgq.reshape(8, HD).sum(0), dgk.reshape(2, HD).sum(0)
    dv_att = _unheads(dvh).astype(BF16)
    do, dr, dggl = _gla_out_bwd(n + "gla_out", s["of"], s["ob"], p, P["ggl"], dgla)
    gw["gla_norm_g"] = dggl[0]
    scans = _gla_bwd(n + "gla_scan", p, s["la"], s["sf"], s["sb"], do)
    dab, dw2, db2, dglq, dglk, dglv = _decay_bwd(n + "gla_decay", p, P["w2"], P["b2"], scans[:4], scans[4:])
    gw["w_alpha2"] = jnp.stack([dw2[0:16, 0:256], dw2[16:32, 256:512]])
    gw["b_alpha"] = db2.reshape(2, 256)
    dp = jnp.concatenate([dga, dgb, dgc, du, dv_g, dq, dglv, dr, dglq, dglk, dk, dv_att, dab], axis=-1)
    gw["w_in"] = _to_ref_cols(_mm(n + "in_proj_w", s["h1"], dp, "tn", F32, tn_t=2176, tk_t=768))
    dh1 = _mm(n + "in_proj_x", dp, P["w_in"], "nt", F32, tk_t=2176)
    return dx, dh1, dmod


_SMALL = (("norm1_g", (2, D)), ("norm2_g", (2, D)), ("q_norm_g", (2, HD)), ("k_norm_g", (2, HD)), ("gmlp_norm_g", (2, GW)),
          ("gla_norm_g", (2, GLA_V)), ("w_spatial", (2, 4, 128, 128)), ("b_spatial", (2, 4, 128)), ("conv_b", (2, 2 * FFN)),
          ("final_norm_g", (D,))) + tuple((n, (2,) + s) for n, s, _ in _SHARDED_SMALL)
_SMALL_N = 2 * 2 * ADA_W + sum(_prod(s) for _, s in _SMALL)
_SMALL_R = -(-_SMALL_N // (PACK_C * 8)) * 8


def _mod_tables(c, c_ctx, w_ada, b_ada, k):
    x, y, cc = _coords()
    me = 4 * x + 2 * y + cc
    c_all = _allgather_small("gather_c", jnp.concatenate([c, jnp.zeros((7, D), F32)], axis=0))
    c8 = c_all.reshape(N_DEV, 8, D)[:, 0]
    cond = jnp.concatenate([c8, c_ctx[None], jnp.zeros((7, D), F32)], axis=0)
    b_loc = lax.dynamic_slice_in_dim(b_ada, k * ADA_LOC, ADA_LOC, axis=1)[:, None, :]
    m_loc = _adaln_fwd("adaln", cond, w_ada, b_loc)
    m_all = _allgather_small("gather_mod", m_loc.reshape(32, ADA_LOC)).reshape(N_CHIP, 2, 2, 16, ADA_LOC)[:, 0]
    m_all = m_all.transpose(1, 2, 0, 3).reshape(2, 16, ADA_W)
    rows = jnp.stack([m_all[:, 8], lax.dynamic_index_in_dim(m_all, me, axis=1, keepdims=False)], axis=1)
    return rows.reshape(2, 2, 6, D), c8


def _step(x, c, ctx, c_ctx, W, tgt):
    xc, yc, cc = _coords()
    k = 2 * xc + yc
    tx = x.shape[0]
    t = TC + tx
    small = {n: W[n] for n, _ in _SMALL}
    for n, a in _gather_f32_shards([W[n] for n, _, _ in _SHARDED_SMALL]).items():
        small[n + "_full"] = a

    gathered = _allgather_layers("gather_weights", [W[n].astype(BF16) for n, _ in _SHARDED])
    wfull = {n: a for (n, _), a in zip(_SHARDED, gathered)}
    mods, c8 = _mod_tables(c, c_ctx, W["w_ada"], W["b_ada"], k)
    tabs = _rope_tables(tx) + _lane_consts()
    params = [_layer_params(l, wfull, small) for l in range(2)]

    xs = jnp.concatenate([ctx, x], axis=0)
    h1 = _nm_fwd("l0_norm1", xs, mods[0], params[0]["g1"], 0, 1)
    s0 = _layer_fwd(0, False, xs, h1, mods[0], params[0], tabs)
    x1, h1b = _res_nm_fwd("l0_res2_norm1", s0["x_mid"], s0["f"], mods[0], 5, mods[1], params[1]["g1"], 0, 1)
    s1 = _layer_fwd(1, True, x1, h1b, mods[1], params[1], tabs)
    loss, dxm_l, df_l, dmod_head, dgf = _head("head", s1["x_mid"], s1["f"], mods[1], W["final_norm_g"][None], tgt)

    gws = [dict(), dict()]
    dx1, dh1b, dmod1 = _layer_bwd(1, True, s1, mods[1], params[1], tabs, dxm_l, df_l, gws[1])
    dxm0, df0, dmod0_g, dmod1_s, dg1b = _res_nm_bwd("l0_b_res2_norm1", s0["x_mid"], s0["f"], mods[0], 5, mods[1],
                                                    params[1]["g1"], 0, 1, dx1, dh1b)
    gws[1]["norm1_g"] = dg1b[0]
    dx0, dh1, dmod0 = _layer_bwd(0, False, s0, mods[0], params[0], tabs, dxm0, df0, gws[0])
    dxs, dmod0_s, dg1 = _nm_bwd("l0_b_norm1", xs, mods[0], params[0]["g1"], 0, 1, dx0, dh1)
    gws[0]["norm1_g"] = dg1[0]
    grad_x = dxs[TC:]
    dmods = jnp.stack([dmod0 + dmod0_g + dmod0_s, dmod1 + dmod1_s + dmod_head])

    stk = {n: jnp.stack([gws[0][n], gws[1][n]]) for n, _ in _SMALL if n != "final_norm_g"}
    stk["final_norm_g"] = dgf[0]
    flat = jnp.concatenate([dmods.reshape(-1)] + [stk[n].reshape(-1) for n, _ in _SMALL])
    flat = jnp.concatenate([flat, jnp.zeros((_SMALL_R * PACK_C - _SMALL_N,), F32)]).reshape(_SMALL_R, PACK_C)
    every = _allgather_small("gather_small_grads", flat).reshape(N_DEV, _SMALL_R, PACK_C)
    tot = _sum_slots("sum_small_grads", every).reshape(-1)
    grads, o = {}, 2 * 2 * ADA_W
    for n, shp in _SMALL:
        grads[n] = tot[o:o + _prod(shp)].reshape(shp)
        o += _prod(shp)
    grads["b_ada"] = tot[:2 * 2 * ADA_W].reshape(2, 2, ADA_W).sum(axis=1)

    dm_every = every[:, :2 * 2 * ADA_W // PACK_C].reshape(N_DEV, 2, 2, ADA_W)
    dm_loc = lax.dynamic_slice_in_dim(dm_every, k * ADA_LOC, ADA_LOC, axis=3).transpose(1, 2, 0, 3)
    cc8 = jnp.concatenate([c_ctx[None], jnp.zeros((7, D), F32)], axis=0)
    grads["w_ada"], dcc = _adaln_bwd("adaln_b", c8, cc8, W["w_ada"], dm_loc[:, 1], dm_loc[:, 0])
    dcc_every = _allgather_small("gather_dcctx", dcc * 0.5).reshape(N_DEV, 8, D)
    grads["c_ctx"] = _sum_slots("sum_dcctx", dcc_every)[0]

    for n, shp, ax in _SHARDED_SMALL:
        grads[n] = lax.dynamic_slice_in_dim(grads[n], k * (shp[ax] // N_CHIP), shp[ax] // N_CHIP, axis=ax + 1)
    red = _reduce_scatter(*[[_to_blocks(gws[l][n], ax) for n, ax in _SHARDED] for l in range(2)])
    grads.update({n: a for (n, _), a in zip(_SHARDED, red)})
    return loss[0, 0], grad_x, grads


_WEIGHTS = ("c_ctx", "w_ada", "b_ada", "norm1_g", "norm2_g", "w_in", "q_norm_g", "k_norm_g", "gmlp_norm_g", "w_spatial",
            "b_spatial", "w_alpha2", "b_alpha", "gla_norm_g", "w_br_a", "w_br_b", "w_br_c", "w_out", "w_ffn_up", "conv_w",
            "conv_b", "w_ffn_down", "final_norm_g")
_BIG = ("w_ada", "w_in", "w_br_a", "w_br_b", "w_br_c", "w_out", "w_ffn_up", "w_ffn_down")


def _update(W, G, M, V):
    delta, new_m, new_v = {}, {}, {}
    for n in _BIG:
        shp = W[n].shape
        two = lambda a: a.reshape(-1, shp[-1])
        d, m, v = _adamw("adamw_" + n, two(W[n]), two(G[n]), two(M[n]), two(V[n]))
        delta[n], new_m[n], new_v[n] = d.reshape(shp), m.reshape(shp), v.reshape(shp)
    rest = [n for n in _WEIGHTS if n not in _BIG]
    tot = sum(_prod(W[n].shape) for n in rest)
    rows = -(-tot // (PACK_C * 8)) * 8

    def cat(dct):
        flat = jnp.concatenate([dct[n].reshape(-1) for n in rest] + [jnp.zeros((rows * PACK_C - tot,), F32)])
        return flat.reshape(rows, PACK_C)

    outs = _adamw("adamw_small", cat(W), cat(G), cat(M), cat(V))
    o = 0
    for n in rest:
        sz, shp = _prod(W[n].shape), W[n].shape
        delta[n], new_m[n], new_v[n] = (a.reshape(-1)[o:o + sz].reshape(shp) for a in outs)
        o += sz
    return delta, new_m, new_v


def kernel(x, c, ctx, c_ctx, w_ada, b_ada, norm1_g, norm2_g, w_in, q_norm_g, k_norm_g, gmlp_norm_g, w_spatial, b_spatial, w_alpha2, b_alpha, gla_norm_g, w_br_a, w_br_b, w_br_c, w_out, w_ffn_up, conv_w, conv_b, w_ffn_down, final_norm_g, loss_target, m_c_ctx, m_w_ada, m_b_ada, m_norm1_g, m_norm2_g, m_w_in, m_q_norm_g, m_k_norm_g, m_gmlp_norm_g, m_w_spatial, m_b_spatial, m_w_alpha2, m_b_alpha, m_gla_norm_g, m_w_br_a, m_w_br_b, m_w_br_c, m_w_out, m_w_ffn_up, m_conv_w, m_conv_b, m_w_ffn_down, m_final_norm_g, v_c_ctx, v_w_ada, v_b_ada, v_norm1_g, v_norm2_g, v_w_in, v_q_norm_g, v_k_norm_g, v_gmlp_norm_g, v_w_spatial, v_b_spatial, v_w_alpha2, v_b_alpha, v_gla_norm_g, v_w_br_a, v_w_br_b, v_w_br_c, v_w_out, v_w_ffn_up, v_conv_w, v_conv_b, v_w_ffn_down, v_final_norm_g):
    W = dict(c_ctx=c_ctx, w_ada=w_ada, b_ada=b_ada, norm1_g=norm1_g, norm2_g=norm2_g, w_in=w_in, q_norm_g=q_norm_g,
             k_norm_g=k_norm_g, gmlp_norm_g=gmlp_norm_g, w_spatial=w_spatial, b_spatial=b_spatial, w_alpha2=w_alpha2,
             b_alpha=b_alpha, gla_norm_g=gla_norm_g, w_br_a=w_br_a, w_br_b=w_br_b, w_br_c=w_br_c, w_out=w_out,
             w_ffn_up=w_ffn_up, conv_w=conv_w, conv_b=conv_b, w_ffn_down=w_ffn_down, final_norm_g=final_norm_g)
    M = dict(c_ctx=m_c_ctx, w_ada=m_w_ada, b_ada=m_b_ada, norm1_g=m_norm1_g, norm2_g=m_norm2_g, w_in=m_w_in,
             q_norm_g=m_q_norm_g, k_norm_g=m_k_norm_g, gmlp_norm_g=m_gmlp_norm_g, w_spatial=m_w_spatial,
             b_spatial=m_b_spatial, w_alpha2=m_w_alpha2, b_alpha=m_b_alpha, gla_norm_g=m_gla_norm_g, w_br_a=m_w_br_a,
             w_br_b=m_w_br_b, w_br_c=m_w_br_c, w_out=m_w_out, w_ffn_up=m_w_ffn_up, conv_w=m_conv_w, conv_b=m_conv_b,
             w_ffn_down=m_w_ffn_down, final_norm_g=m_final_norm_g)
    V = dict(c_ctx=v_c_ctx, w_ada=v_w_ada, b_ada=v_b_ada, norm1_g=v_norm1_g, norm2_g=v_norm2_g, w_in=v_w_in,
             q_norm_g=v_q_norm_g, k_norm_g=v_k_norm_g, gmlp_norm_g=v_gmlp_norm_g, w_spatial=v_w_spatial,
             b_spatial=v_b_spatial, w_alpha2=v_w_alpha2, b_alpha=v_b_alpha, gla_norm_g=v_gla_norm_g, w_br_a=v_w_br_a,
             w_br_b=v_w_br_b, w_br_c=v_w_br_c, w_out=v_w_out, w_ffn_up=v_w_ffn_up, conv_w=v_conv_w, conv_b=v_conv_b,
             w_ffn_down=v_w_ffn_down, final_norm_g=v_final_norm_g)
    loss_local, grad_x, G = _step(x[0], c, ctx[0], c_ctx, W, loss_target[0])
    loss = lax.psum(loss_local, ("x", "y", "c"))
    delta, new_m, new_v = _update(W, G, M, V)
    return (loss, grad_x[None], *[G[n] for n in _WEIGHTS], *[delta[n] for n in _WEIGHTS],
            *[new_m[n] for n in _WEIGHTS], *[new_v[n] for n in _WEIGHTS])
```

```python
import functools

import jax
import jax.numpy as jnp
from jax import lax
from jax.experimental import pallas as pl
from jax.experimental.pallas import tpu as pltpu

F32 = jnp.float32
BF16 = jnp.bfloat16

D = 1024
TC = 256
GRID_W = 64
EPS = 1e-6
HD = 64
NQ = 8
NKV = 2
QG = NQ // NKV
GLA_H = 4
GLA_DK = 64
GLA_DV = 128
GLA_QK = 256
GLA_V = 512
GLA_CHUNK = 64
GLA_TAU = 16.0
GW = 512
FFN = 2816
IN_W = 6432
PW = 6528
ADA_W = 6 * D
N_CHIP = 4
N_DEV = 8
ADA_LOC = ADA_W // N_CHIP

ADAM_LR = 0.001
ADAM_B1 = 0.9
ADAM_B2 = 0.999
ADAM_EPS = 1e-08
ADAM_WD = 0.01
ADAM_STEP = 10

TM = 256
NCB = TC // TM
LANE = 128
VMEM_LIMIT = 48 * 1024 * 1024
MESH = pl.DeviceIdType.MESH

_COLS = (("gA", 3360, 1024), ("gB", 4384, 1024), ("gC", 5408, 1024), ("gu", 0, 512), ("gv", 512, 512),
         ("q", 1024, 512), ("glv", 2304, 512), ("gr", 2848, 512), ("glq", 1792, 256), ("glk", 2048, 256),
         ("k", 1536, 128), ("v", 1664, 128), ("ab", 2816, 32))
OFF = {}
_o = 0
for _n, _s, _w in _COLS:
    OFF[_n] = _o
    _o += max(_w, LANE)
assert _o == PW


def _to_new_cols(w):
    parts = [w[..., s:s + n] for _, s, n in _COLS]
    pad = jnp.zeros(w.shape[:-1] + (PW - IN_W,), w.dtype)
    return jnp.concatenate(parts + [pad], axis=-1)


def _to_ref_cols(w):
    by_start = sorted(_COLS, key=lambda t: t[1])
    return jnp.concatenate([w[..., OFF[n]:OFF[n] + wd] for n, _, wd in by_start], axis=-1)


def _tile(n, target, align=LANE):
    best = None
    t = align
    while t <= min(n, target):
        if n % t == 0:
            best = t
        t += align
    assert best is not None, (n, target, align)
    return best


def _cp(sem=None):
    return pltpu.CompilerParams(dimension_semantics=sem, vmem_limit_bytes=VMEM_LIMIT)


def _bdot_impl(a, b, ca, cb):
    return lax.dot_general(a.astype(BF16), b.astype(BF16), (((ca,), (cb,)), ((), ())),
                           preferred_element_type=F32)


@functools.partial(jax.custom_vjp, nondiff_argnums=(2, 3))
def bdot(a, b, ca, cb):
    return _bdot_impl(a, b, ca, cb)


def _bdot_fwd(a, b, ca, cb):
    return _bdot_impl(a, b, ca, cb), (a, b)


def _bdot_bwd(ca, cb, res, g):
    a, b = res
    da = bdot(g, b, 1, 1 - cb) if ca == 1 else bdot(b, g, 1 - cb, 1)
    db = bdot(a, g, 1 - ca, 0) if cb == 0 else bdot(g, a, 0, 1 - ca)
    return da.astype(a.dtype), db.astype(b.dtype)


bdot.defvjp(_bdot_fwd, _bdot_bwd)


def hdot(a, b, ca=1, cb=0):
    return lax.dot_general(a, b, (((ca,), (cb,)), ((), ())), precision=lax.Precision.HIGH,
                           preferred_element_type=F32)


def _rms(x, g):
    return x * lax.rsqrt(jnp.mean(x * x, axis=-1, keepdims=True) + EPS) * g


def _gelu(x):
    return 0.5 * x * (1.0 + jnp.tanh(0.7978845608028654 * (x + 0.044715 * (x * x * x))))


def _log_sigmoid(z):
    return jnp.minimum(z, 0.0) - jnp.log(1.0 + jnp.exp(-jnp.abs(z)))


def _sel(mod, is_lat, idx):
    return jnp.where(is_lat, mod[1, idx:idx + 1, :], mod[0, idx:idx + 1, :])


def _rows_call(name, fn, grid, ins, outs, acc_axes=None, sem=None):
    n_in = len(ins)
    flags = [o[2] for o in outs]
    if acc_axes is None:
        acc_axes = (len(grid) - 1,)

    def body(*refs):
        ids = tuple(pl.program_id(a) for a in range(len(grid)))
        res = fn(ids, *[r[...] for r in refs[:n_in]])
        for r, v, acc in zip(refs[n_in:], res, flags):
            if acc:
                first = functools.reduce(jnp.logical_and, [ids[a] == 0 for a in acc_axes])

                @pl.when(first)
                def _():
                    r[...] = jnp.zeros_like(r)
                r[...] += v.astype(r.dtype)
            else:
                r[...] = v.astype(r.dtype)

    return pl.pallas_call(
        body, name=name, grid=grid, in_specs=[s for _, s in ins], out_specs=[o[1] for o in outs],
        out_shape=[o[0] for o in outs],
        compiler_params=_cp(sem if sem is not None else ("arbitrary",) * len(grid)),
    )(*[a for a, _ in ins])


def _sds(shape, dtype):
    return jax.ShapeDtypeStruct(shape, dtype)


def _rowspec(width, off=0, tm=TM):
    assert off % width == 0
    return pl.BlockSpec((tm, width), lambda i, o=off // width: (i, o))


def _full(shape):
    nd = len(shape)
    return pl.BlockSpec(shape, lambda *a: (0,) * nd)


def _mm(name, a, b, mode, out_dtype, tm_t=1056, tn_t=1408, tk_t=1408, chip_blocks=False, j_outer=False):
    halves = a.ndim == 3 or b.ndim == 3
    if mode == "nn":
        (m, k), (_, n) = a.shape, b.shape
    elif mode == "nt":
        (m, k), (n, _) = a.shape[-2:], b.shape
        k *= a.ndim - 1
    else:
        (k, m), (_, n) = a.shape, b.shape[-2:]
        n *= b.ndim - 1
    tm = _tile(m, tm_t, 8 if m % LANE else LANE)
    tn = _tile(n // 2 if halves and mode == "tn" else n, tn_t)
    tk = _tile(k // 2 if halves and mode == "nt" else k, tk_t)
    nk = k // tk
    if mode == "nn":
        dims, a_spec, b_spec = ((1,), (0,)), pl.BlockSpec((tm, tk), lambda i, j, l: (i, l)), pl.BlockSpec((tk, tn), lambda i, j, l: (l, j))
    elif mode == "nt":
        dims, a_spec, b_spec = ((1,), (1,)), pl.BlockSpec((tm, tk), lambda i, j, l: (i, l)), pl.BlockSpec((tn, tk), lambda i, j, l: (j, l))
        if halves:
            a_spec = pl.BlockSpec((None, tm, tk), lambda i, j, l, h=nk // 2: (l // h, i, l % h))
    else:
        dims, a_spec, b_spec = ((0,), (0,)), pl.BlockSpec((tk, tm), lambda i, j, l: (l, i)), pl.BlockSpec((tk, tn), lambda i, j, l: (l, j))
        if halves:
            b_spec = pl.BlockSpec((None, tk, tn), lambda i, j, l, h=n // tn // 2: (j // h, l, j % h))

    def body(a_ref, b_ref, o_ref, *scratch):
        l = pl.program_id(2)
        part = lax.dot_general(a_ref[...].astype(BF16), b_ref[...].astype(BF16), (dims, ((), ())),
                               preferred_element_type=F32)
        if nk == 1:
            o_ref[...] = part.astype(o_ref.dtype)
            return
        acc_ref = scratch[0]

        @pl.when(l == 0)
        def _():
            acc_ref[...] = part

        @pl.when(l > 0)
        def _():
            acc_ref[...] += part

        @pl.when(l == nk - 1)
        def _():
            o_ref[...] = acc_ref[...].astype(o_ref.dtype)

    o_spec, o_shape = pl.BlockSpec((tm, tn), lambda i, j, l: (i, j)), _sds((m, n), out_dtype)
    if chip_blocks:
        assert tn * N_CHIP == n and tm == m
        o_spec, o_shape = pl.BlockSpec((None, tm, tn), lambda i, j, l: (j, 0, 0)), _sds((N_CHIP, m, tn), out_dtype)
    grid = (m // tm, n // tn, nk)
    if j_outer:
        swap = lambda spec: pl.BlockSpec(spec.block_shape, lambda j, i, l, f=spec.index_map: f(i, j, l))
        a_spec, b_spec, o_spec, grid = swap(a_spec), swap(b_spec), swap(o_spec), (n // tn, m // tm, nk)
    return pl.pallas_call(
        body, name=name, grid=grid, in_specs=[a_spec, b_spec], out_specs=o_spec, out_shape=o_shape,
        scratch_shapes=[pltpu.VMEM((tm, tn), F32)] if nk > 1 else [],
        compiler_params=_cp(("parallel", "parallel", "arbitrary")),
    )(a, b)


def _nm_fn(is_lat, x, mod, g, shift, scale):
    return _rms(x, g) * (1.0 + _sel(mod, is_lat, scale)) + _sel(mod, is_lat, shift)


def _res_nm_fn(is_lat, x, br, modg, gate, mods, g, shift, scale):
    xn = x + _sel(modg, is_lat, gate) * br
    return xn, _nm_fn(is_lat, xn, mods, g, shift, scale)


def _nm_fwd(name, x, mod, g, shift, scale):
    t = x.shape[0]
    fn = lambda ids, xv, mv, gv: (_nm_fn(ids[0] >= NCB, xv, mv, gv, shift, scale),)
    return _rows_call(name, fn, (t // TM,), [(x, _rowspec(D)), (mod, _full((2, 6, D))), (g, _full((1, D)))],
                      [(_sds((t, D), BF16), _rowspec(D), False)])[0]


def _nm_bwd(name, x, mod, g, shift, scale, dx_res, dh):
    t = x.shape[0]

    def fn(ids, xv, mv, gv, dxr, dhv):
        _, vjp = jax.vjp(lambda a, b, c: _nm_fn(ids[0] >= NCB, a, b, c, shift, scale), xv, mv, gv)
        dx, dm, dg = vjp(dhv)
        return dx + dxr, dm, dg

    lat = pl.BlockSpec((TM, D), lambda i: (jnp.maximum(i - NCB, 0), 0))
    return _rows_call(name, fn, (t // TM,),
                      [(x, _rowspec(D)), (mod, _full((2, 6, D))), (g, _full((1, D))), (dx_res, _rowspec(D)), (dh, _rowspec(D))],
                      [(_sds((t - TC, D), F32), lat, False), (_sds((2, 6, D), F32), _full((2, 6, D)), True),
                       (_sds((1, D), F32), _full((1, D)), True)])


def _res_nm_fwd(name, x, br, modg, gate, mods, g, shift, scale):
    t = x.shape[0]
    fn = lambda ids, xv, bv, mg, ms, gv: _res_nm_fn(ids[0] >= NCB, xv, bv, mg, gate, ms, gv, shift, scale)
    return _rows_call(name, fn, (t // TM,),
                      [(x, _rowspec(D)), (br, _rowspec(D)), (modg, _full((2, 6, D))), (mods, _full((2, 6, D))), (g, _full((1, D)))],
                      [(_sds((t, D), F32), _rowspec(D), False), (_sds((t, D), BF16), _rowspec(D), False)])


def _res_nm_bwd(name, x, br, modg, gate, mods, g, shift, scale, dx_res, dh):
    t = x.shape[0]

    def fn(ids, xv, bv, mg, ms, gv, dxr, dhv):
        f = lambda a, b, c, d, e: _res_nm_fn(ids[0] >= NCB, a, b, c, gate, d, e, shift, scale)
        _, vjp = jax.vjp(f, xv, bv, mg, ms, gv)
        return vjp((dxr, dhv))

    m26 = (_sds((2, 6, D), F32), _full((2, 6, D)), True)
    return _rows_call(name, fn, (t // TM,),
                      [(x, _rowspec(D)), (br, _rowspec(D)), (modg, _full((2, 6, D))), (mods, _full((2, 6, D))), (g, _full((1, D))),
                       (dx_res, _rowspec(D)), (dh, _rowspec(D))],
                      [(_sds((t, D), F32), _rowspec(D), False), (_sds((t, D), BF16), _rowspec(D), False), m26, m26,
                       (_sds((1, D), F32), _full((1, D)), True)])


def _head(name, x_mid, f, mod, gf, tgt):
    t = x_mid.shape[0]

    def fn(ids, xv, fv, mv, gv, tv):
        def loss_fn(a, b, c, d):
            y = _rms(a + c[1, 5:6, :] * b, d)
            e = y - tv
            return 0.5 * jnp.sum(jnp.mean(e * e, axis=-1))
        loss, grads = jax.value_and_grad(loss_fn, argnums=(0, 1, 2, 3))(xv, fv, mv, gv)
        return tuple(jnp.where(ids[0] >= NCB, v, 0.0) for v in (jnp.reshape(loss, (1, 1)),) + grads)

    return _rows_call(name, fn, (t // TM,),
                      [(x_mid, _rowspec(D)), (f, _rowspec(D)), (mod, _full((2, 6, D))), (gf, _full((1, D))),
                       (tgt, pl.BlockSpec((TM, D), lambda i: (jnp.maximum(i - NCB, 0), 0)))],
                      [(_sds((1, 1), F32), _full((1, 1)), True), (_sds((t, D), F32), _rowspec(D), False),
                       (_sds((t, D), BF16), _rowspec(D), False), (_sds((2, 6, D), F32), _full((2, 6, D)), True),
                       (_sds((1, D), F32), _full((1, D)), True)])


def _gmlp_fn(u, v, g, ws, bst):
    rows = []
    for r in range(u.shape[0] // 128):
        uu, vv = _gelu(u[128 * r:128 * r + 128]), _gelu(v[128 * r:128 * r + 128])
        cols = []
        for gi in range(4):
            sl = slice(128 * gi, 128 * gi + 128)
            f = bdot(ws[gi], _rms(vv[:, sl], g[:, sl]), 1, 0) + bst[:, gi:gi + 1]
            cols.append(uu[:, sl] * f)
        rows.append(jnp.concatenate(cols, axis=-1))
    return jnp.concatenate(rows, axis=0)


def _gmlp_ins(p, g, ws, bst):
    return [(p, _rowspec(GW, OFF["gu"])), (p, _rowspec(GW, OFF["gv"])), (g, _full((1, GW))),
            (ws, _full((4, 128, 128))), (bst, _full((128, 4)))]


def _gmlp_fwd(name, p, g, ws, bst):
    t = p.shape[0]
    return _rows_call(name, lambda ids, *a: (_gmlp_fn(*a),), (t // TM,), _gmlp_ins(p, g, ws, bst),
                      [(_sds((t, GW), BF16), _rowspec(GW), False)])[0]


def _gmlp_bwd(name, p, g, ws, bst, dgm):
    t = p.shape[0]

    def fn(ids, u, v, gv, wv, bv, dv):
        _, vjp = jax.vjp(_gmlp_fn, u, v, gv, wv, bv)
        return vjp(dv)

    return _rows_call(name, fn, (t // TM,), _gmlp_ins(p, g, ws, bst) + [(dgm, _rowspec(GW))],
                      [(_sds((t, GW), BF16), _rowspec(GW), False), (_sds((t, GW), BF16), _rowspec(GW), False),
                       (_sds((1, GW), F32), _full((1, GW)), True), (_sds((4, 128, 128), F32), _full((4, 128, 128)), True),
                       (_sds((128, 4), F32), _full((128, 4)), True)])


def _qk_fn(q, k, gq, gk, cos, sin, seg, perm):
    cq, sq = jnp.concatenate([cos] * 4, axis=-1), jnp.concatenate([sin] * 4, axis=-1)
    qn = q * lax.rsqrt(hdot(q * q, seg) + EPS) * gq
    kn = k * lax.rsqrt(hdot(k * k, seg[:128, :128]) + EPS) * gk
    qr = qn * cq + hdot(qn, perm) * sq
    kr = kn * cos + hdot(kn, perm[:128, :128]) * sin
    return qr * (HD ** -0.5), kr


def _qk_ins(p, gq, gk, cos, sin, seg, perm):
    return [(p, _rowspec(512, OFF["q"])), (p, _rowspec(128, OFF["k"])), (gq, _full((1, 512))), (gk, _full((1, 128))),
            (cos, _rowspec(128)), (sin, _rowspec(128)), (seg, _full((512, 512))), (perm, _full((512, 512)))]


def _qk_fwd(name, p, gq, gk, cos, sin, seg, perm):
    t = p.shape[0]
    fn = lambda ids, q, k, a, b, c, s, sg, pm, v: _qk_fn(q, k, a, b, c, s, sg, pm) + (v,)
    return _rows_call(name, fn, (t // TM,), _qk_ins(p, gq, gk, cos, sin, seg, perm) + [(p, _rowspec(128, OFF["v"]))],
                      [(_sds((t, 512), BF16), _rowspec(512), False), (_sds((t, 128), BF16), _rowspec(128), False),
                       (_sds((t, 128), BF16), _rowspec(128), False)])


def _qk_bwd(name, p, gq, gk, cos, sin, seg, perm, dqr, dkr):
    t = p.shape[0]

    def fn(ids, q, k, a, b, c, s, sg, pm, dq, dk):
        _, vjp = jax.vjp(lambda q_, k_, a_, b_: _qk_fn(q_, k_, a_, b_, c, s, sg, pm), q, k, a, b)
        return vjp((dq, dk))

    return _rows_call(name, fn, (t // TM,),
                      _qk_ins(p, gq, gk, cos, sin, seg, perm) + [(dqr, _rowspec(512)), (dkr, _rowspec(128))],
                      [(_sds((t, 512), BF16), _rowspec(512), False), (_sds((t, 128), BF16), _rowspec(128), False),
                       (_sds((1, 512), F32), _full((1, 512)), True), (_sds((1, 128), F32), _full((1, 128)), True)])


_ATT_TQ = 1024
_ATT_TK = 768


def _attn_fwd(name, q, k, v):
    h, tq_all, _ = q.shape
    hkv, tk_all, _ = k.shape
    tq, tk = _tile(tq_all, _ATT_TQ), _tile(tk_all, _ATT_TK)
    nkc = tk_all // tk

    def body(q_ref, k_ref, v_ref, o_ref, lse_ref):
        qv = q_ref[...].reshape(QG * tq, HD)

        def step(j, carry):
            m, acc = carry
            off = pl.multiple_of(j * tk, tk)
            kk, vv = k_ref[0, pl.ds(off, tk), :], v_ref[0, pl.ds(off, tk), :]
            s = lax.dot_general(qv, kk, (((1,), (1,)), ((), ())), preferred_element_type=F32)
            m_new = jnp.maximum(m, jnp.max(s, axis=-1, keepdims=True))
            pr = jnp.exp(s - m_new)
            acc = jnp.exp(m - m_new) * acc + jnp.dot(pr.astype(BF16), vv, preferred_element_type=F32)
            return m_new, acc

        init = (jnp.full((QG * tq, 1), -jnp.inf, F32), jnp.zeros((QG * tq, 2 * HD), F32))
        m, acc = lax.fori_loop(0, nkc, step, init)
        l = acc[:, HD:HD + 1]
        o_ref[...] = (acc[:, :HD] / l).reshape(QG, tq, HD)
        lse_ref[...] = (m + jnp.log(l)).reshape(QG, tq, 1)

    kv_spec = pl.BlockSpec((1, tk_all, HD), lambda g, i: (g, 0, 0))
    v1_spec = pl.BlockSpec((1, tk_all, 2 * HD), lambda g, i: (g, 0, 0))
    qspec = pl.BlockSpec((QG, tq, HD), lambda g, i: (g, i, 0))
    return pl.pallas_call(
        body, name=name, grid=(hkv, tq_all // tq), in_specs=[qspec, kv_spec, v1_spec],
        out_specs=[qspec, pl.BlockSpec((QG, tq, 1), lambda g, i: (g, i, 0))],
        out_shape=[_sds((h, tq_all, HD), F32), _sds((h, tq_all, 1), F32)],
        compiler_params=_cp(("parallel", "parallel")),
    )(q, k, v)


def _attn_bwd(name, q, k, v, o, do, lse_row):
    h, tq_all, _ = q.shape
    hkv, tk_all, _ = k.shape
    tq, tk = _tile(tq_all, 1024), _tile(tk_all, 1408)

    def body(q_ref, k_ref, v_ref, o_ref, do_ref, lse_ref, dq_ref, dk_ref, dv_ref, dl_ref):
        i, j = pl.program_id(1), pl.program_id(2)

        @pl.when(j == 0)
        def _():
            ones = jnp.ones((8, HD), F32)
            for g in range(QG):
                dl_ref[g] = hdot(ones, do_ref[g] * o_ref[g], 1, 1)

        kk, vv = k_ref[0], v_ref[0]
        dk_acc, dv_acc = jnp.zeros((tk, HD), F32), jnp.zeros((tk, HD), F32)
        for g in range(QG):
            qv, dob = q_ref[g], do_ref[g].astype(BF16)
            st = lax.dot_general(kk, qv, (((1,), (1,)), ((), ())), preferred_element_type=F32)
            pt = jnp.exp(st - lse_ref[g])
            dv_acc += jnp.dot(pt.astype(BF16), dob, preferred_element_type=F32)
            dpt = lax.dot_general(vv, dob, (((1,), (1,)), ((), ())), preferred_element_type=F32)
            dst = (pt * (dpt - dl_ref[g, 0:1, :])).astype(BF16)
            dk_acc += jnp.dot(dst, qv, preferred_element_type=F32)
            dq_part = lax.dot_general(dst, kk, (((0,), (0,)), ((), ())), preferred_element_type=F32)

            @pl.when(j == 0)
            def _():
                dq_ref[g] = dq_part

            @pl.when(j > 0)
            def _():
                dq_ref[g] += dq_part

        rows = pl.ds(pl.multiple_of(j * tk, tk), tk)

        @pl.when(i == 0)
        def _():
            dk_ref[0, rows, :] = dk_acc
            dv_ref[0, rows, :] = dv_acc

        @pl.when(i > 0)
        def _():
            dk_ref[0, rows, :] += dk_acc
            dv_ref[0, rows, :] += dv_acc

    ks = pl.BlockSpec((1, tk, HD), lambda g, i, j: (g, j, 0))
    qs = pl.BlockSpec((QG, tq, HD), lambda g, i, j: (g, i, 0))
    rs = pl.BlockSpec((QG, 1, tq), lambda g, i, j: (g, 0, i))
    full = pl.BlockSpec((1, tk_all, HD), lambda g, i, j: (g, 0, 0))
    return pl.pallas_call(
        body, name=name, grid=(hkv, tq_all // tq, tk_all // tk), in_specs=[qs, ks, ks, qs, qs, rs], out_specs=[qs, full, full],
        out_shape=[_sds((h, tq_all, HD), F32), _sds((hkv, tk_all, HD), F32), _sds((hkv, tk_all, HD), F32)],
        scratch_shapes=[pltpu.VMEM((QG, 8, tq), F32)],
        compiler_params=_cp(("parallel", "arbitrary", "arbitrary")),
    )(q, k, v, o, do, lse_row)


def _decay_fn(a, w2, b2):
    return _log_sigmoid(bdot(a, w2, 1, 0) + b2) / GLA_TAU


def _decay_fwd(name, p, w2, b2):
    t = p.shape[0]
    return _rows_call(name, lambda ids, a, w, b: (_decay_fn(a, w, b),), (t // TM,),
                      [(p, _rowspec(128, OFF["ab"])), (w2, _full((128, 512))), (b2, _full((1, 512)))],
                      [(_sds((t, 512), F32), _rowspec(512), False)])[0]


def _decay_bwd(name, p, w2, b2, gf, gb):
    t = p.shape[0]

    def fn(ids, a, w, b, qf, kf, vf, lf, qb, kb, vb, lb):
        _, vjp = jax.vjp(_decay_fn, a, w, b)
        return vjp(jnp.concatenate([lf, lb], axis=-1)) + (qf + qb, kf + kb, vf + vb)

    widths = (256, 256, 512, 256)
    return _rows_call(name, fn, (t // TM,),
                      [(p, _rowspec(128, OFF["ab"])), (w2, _full((128, 512))), (b2, _full((1, 512)))]
                      + [(g, _rowspec(w)) for g, w in zip(gf, widths)] + [(g, _rowspec(w)) for g, w in zip(gb, widths)],
                      [(_sds((t, 128), BF16), _rowspec(128), False), (_sds((128, 512), F32), _full((128, 512)), True),
                       (_sds((1, 512), F32), _full((1, 512)), True)]
                      + [(_sds((t, w), BF16), _rowspec(w), False) for w in widths[:3]])


def _gla_consts(reverse):
    r = lax.broadcasted_iota(jnp.int32, (GLA_CHUNK, GLA_CHUNK), 0)
    c = lax.broadcasted_iota(jnp.int32, (GLA_CHUNK, GLA_CHUNK), 1)
    trib = (r <= c) if reverse else (r >= c)
    br = lax.broadcasted_iota(jnp.int32, (GLA_QK, GLA_V), 0) // GLA_DK
    bc = lax.broadcasted_iota(jnp.int32, (GLA_QK, GLA_V), 1) // GLA_DV
    lane_head = lax.broadcasted_iota(jnp.int32, (1, GLA_QK), 1) // GLA_DK
    return trib, (br == bc).astype(F32), lane_head


def _gla_chunk(q, k, v, la, s_in, consts):
    trib, bd, lane_head = consts
    cum = hdot(trib.astype(F32), la)
    tot = jnp.sum(la, axis=0, keepdims=True)
    q_in = q * (GLA_DK ** -0.5) * jnp.exp(cum)
    k_in = k * jnp.exp(-cum)
    k_st = k * jnp.exp(tot - cum)
    outs = []
    for h in range(GLA_H):
        att = bdot(jnp.where(lane_head == h, q_in, 0.0), k_in, 1, 1)
        att = jnp.where(trib, att, 0.0)
        outs.append(bdot(att, v[:, GLA_DV * h:GLA_DV * (h + 1)], 1, 0))
    o = jnp.concatenate(outs, axis=-1) + bdot(q_in, s_in, 1, 0)
    decay = jnp.exp(hdot(la, jnp.ones((GLA_CHUNK, LANE), F32), 0, 0))
    s_out = jnp.concatenate([decay] * (GLA_V // LANE), axis=-1) * s_in + bdot(k_st, v, 0, 0) * bd
    return o, s_out


def _gla_order(nb, reverse, backward):
    if not reverse:
        return (lambda s: nb - 1 - s) if backward else (lambda s: s)
    if backward:
        return lambda s: jnp.where(s == nb - 1, 0, s + 1)
    return lambda s: jnp.where(s == 0, 0, nb - s)


_NCH = TM // GLA_CHUNK


def _gla_specs(nb, reverse, backward):
    order = _gla_order(nb, reverse, backward)
    col = lambda width, off: pl.BlockSpec((TM, width), lambda s, o=off // width: (order(s), o))
    state = pl.BlockSpec((_NCH, GLA_H, GLA_DK, GLA_DV), lambda s: (order(s), 0, 0, 0))
    qkvla = [col(256, OFF["glq"]), col(256, OFF["glk"]), col(512, OFF["glv"]), col(256, 256 * int(reverse))]
    return col, state, qkvla


def _gla_fwd(name, p, la):
    t = p.shape[0]
    nb = t // TM

    def body(*refs):
        ins, outs, scr = (refs[0:4], refs[4:8]), (refs[8:10], refs[10:12]), refs[12:14]

        @pl.when(pl.program_id(0) == 0)
        def _():
            for s_ref in scr:
                s_ref[...] = jnp.zeros_like(s_ref)

        for step in range(_NCH):
            for d in range(2):
                (q_ref, k_ref, v_ref, la_ref), (o_ref, sv_ref), s_ref = ins[d], outs[d], scr[d]
                c = _NCH - 1 - step if d else step
                rows = slice(GLA_CHUNK * c, GLA_CHUNK * (c + 1))
                s_in = s_ref[...]
                for h in range(GLA_H):
                    sv_ref[c, h] = s_in[GLA_DK * h:GLA_DK * (h + 1), GLA_DV * h:GLA_DV * (h + 1)]
                o, s_out = _gla_chunk(q_ref[rows, :], k_ref[rows, :], v_ref[rows, :], la_ref[rows, :], s_in,
                                      _gla_consts(bool(d)))
                o_ref[rows, :] = o
                s_ref[...] = s_out

    in_specs, out_specs, out_shape = [], [], []
    for d in range(2):
        col, state, qkvla = _gla_specs(nb, bool(d), False)
        in_specs += qkvla
        out_specs += [col(512, 0), state]
        out_shape += [_sds((t, GLA_V), F32), _sds((t // GLA_CHUNK, GLA_H, GLA_DK, GLA_DV), F32)]
    return pl.pallas_call(
        body, name=name, grid=(nb,), in_specs=in_specs, out_specs=out_specs, out_shape=out_shape,
        scratch_shapes=[pltpu.VMEM((GLA_QK, GLA_V), F32)] * 2, compiler_params=_cp(("arbitrary",)),
    )(p, p, p, la, p, p, p, la)


def _gla_bwd(name, p, la, sv_f, sv_b, do):
    t = p.shape[0]
    nb = t // TM

    def body(*refs):
        ins, outs, scr = (refs[0:6], refs[6:12]), (refs[12:16], refs[16:20]), refs[20:22]

        @pl.when(pl.program_id(0) == 0)
        def _():
            for ds_ref in scr:
                ds_ref[...] = jnp.zeros_like(ds_ref)

        zero = jnp.zeros((GLA_DK, GLA_DV), F32)
        for step in range(_NCH):
            for d in range(2):
                (q_ref, k_ref, v_ref, la_ref, sv_ref, do_ref), out_refs, ds_ref = ins[d], outs[d], scr[d]
                c = step if d else _NCH - 1 - step
                rows = slice(GLA_CHUNK * c, GLA_CHUNK * (c + 1))
                s_in = jnp.concatenate(
                    [jnp.concatenate([sv_ref[c, h] if hh == h else zero for hh in range(GLA_H)], axis=-1)
                     for h in range(GLA_H)], axis=0)
                consts = _gla_consts(bool(d))
                _, vjp = jax.vjp(lambda a, b, cc, dd, e: _gla_chunk(a, b, cc, dd, e, consts),
                                 q_ref[rows, :], k_ref[rows, :], v_ref[rows, :], la_ref[rows, :], s_in)
                grads = vjp((do_ref[rows, :], ds_ref[...]))
                for o_ref, g in zip(out_refs, grads[:4]):
                    o_ref[rows, :] = g
                ds_ref[...] = grads[4]

    ins, in_specs, out_specs, out_shape = [], [], [], []
    for d, sv in enumerate((sv_f, sv_b)):
        col, state, qkvla = _gla_specs(nb, bool(d), True)
        ins += [p, p, p, la, sv, do]
        in_specs += qkvla + [state, col(512, 0)]
        out_specs += [col(256, 0), col(256, 0), col(512, 0), col(256, 0)]
        out_shape += [_sds((t, GLA_QK), F32), _sds((t, GLA_QK), F32), _sds((t, GLA_V), F32), _sds((t, GLA_QK), F32)]
    return pl.pallas_call(
        body, name=name, grid=(nb,), in_specs=in_specs, out_specs=out_specs, out_shape=out_shape,
        scratch_shapes=[pltpu.VMEM((GLA_QK, GLA_V), F32)] * 2, compiler_params=_cp(("arbitrary",)),
    )(*ins)


def _gla_out_fn(of, ob, r, g):
    o = of + ob
    cols = [_rms(o[:, GLA_DV * h:GLA_DV * (h + 1)], g[:, GLA_DV * h:GLA_DV * (h + 1)]) for h in range(GLA_H)]
    return jnp.concatenate(cols, axis=-1) * jax.nn.silu(r)


def _gla_out_fwd(name, of, ob, p, g):
    t = p.shape[0]
    return _rows_call(name, lambda ids, *a: (_gla_out_fn(*a),), (t // TM,),
                      [(of, _rowspec(512)), (ob, _rowspec(512)), (p, _rowspec(512, OFF["gr"])), (g, _full((1, 512)))],
                      [(_sds((t, 512), BF16), _rowspec(512), False)])[0]


def _gla_out_bwd(name, of, ob, p, g, dgla):
    t = p.shape[0]

    def fn(ids, a, b, r, gv, dv):
        _, vjp = jax.vjp(_gla_out_fn, a, b, r, gv)
        do, _, dr, dg = vjp(dv)
        return do, dr, dg

    return _rows_call(name, fn, (t // TM,),
                      [(of, _rowspec(512)), (ob, _rowspec(512)), (p, _rowspec(512, OFF["gr"])), (g, _full((1, 512))),
                       (dgla, _rowspec(512))],
                      [(_sds((t, 512), F32), _rowspec(512), False), (_sds((t, 512), BF16), _rowspec(512), False),
                       (_sds((1, 512), F32), _full((1, 512)), True)])


_TMM = 384


def _merge_fwd(name, gm, att, gla, wa, wb, wc, p):
    t = p.shape[0]
    row = lambda w, off=0: pl.BlockSpec((_TMM, w), lambda i, o=off // w: (i, o))

    def fn(ids, a, b, c, wa_, wb_, wc_, ga, gb, gc):
        return (jax.nn.sigmoid(ga) * bdot(a, wa_, 1, 0) + jax.nn.sigmoid(gb) * bdot(b, wb_, 1, 0)
                + jax.nn.sigmoid(gc) * bdot(c, wc_, 1, 0),)

    return _rows_call(name, fn, (t // _TMM,),
                      [(gm, row(512)), (att, row(512)), (gla, row(512)), (wa, _full((512, D))), (wb, _full((512, D))),
                       (wc, _full((512, D))), (p, row(D, OFF["gA"])), (p, row(D, OFF["gB"])), (p, row(D, OFF["gC"]))],
                      [(_sds((t, D), BF16), row(D), False)])[0]


def _merge_bwd(name, gm, att, gla, wa, wb, wc, p, dmerged):
    t = p.shape[0]
    row = lambda w, off=0: pl.BlockSpec((_TMM, w), lambda i, o=off // w: (i, o))

    def fn(ids, a, b, c, wa_, wb_, wc_, ga, gb, gc, dm):
        outs_y, outs_g = [], []
        for br, w, g in ((a, wa_, ga), (b, wb_, gb), (c, wc_, gc)):
            s = jax.nn.sigmoid(g)
            outs_y.append(dm * s)
            outs_g.append(dm * bdot(br, w, 1, 0) * s * (1.0 - s))
        return tuple(outs_y) + tuple(outs_g)

    o = (_sds((t, D), BF16), row(D), False)
    return _rows_call(name, fn, (t // _TMM,),
                      [(gm, row(512)), (att, row(512)), (gla, row(512)), (wa, _full((512, D))), (wb, _full((512, D))),
                       (wc, _full((512, D))), (p, row(D, OFF["gA"])), (p, row(D, OFF["gB"])), (p, row(D, OFF["gC"])),
                       (dmerged, row(D))], [o] * 6)


_TNC = 1408
_NJ = FFN // _TNC


def _shift_rows(x, prev8, next8, vp, vn):
    n = x.shape[0]
    rid = lax.broadcasted_iota(jnp.int32, x.shape, 0)
    xp = jnp.where(rid == 0, jnp.where(vp, prev8[7:8, :], 0.0), pltpu.roll(x, 1, 0))
    xn = jnp.where(rid == n - 1, jnp.where(vn, next8[0:1, :], 0.0), pltpu.roll(x, n - 1, 0))
    return xp, xn


def _seq_edges(i, t):
    start, end = i * TM, (i + 1) * TM
    return jnp.logical_and(start != 0, start != TC), jnp.logical_and(end != TC, end != t)


def _halo_specs(t, colmap):
    r8 = TM // 8
    main = pl.BlockSpec((TM, _TNC), lambda j, i: (i, colmap(j)))
    prev = pl.BlockSpec((8, _TNC), lambda j, i: (jnp.maximum(i * r8 - 1, 0), colmap(j)))
    nxt = pl.BlockSpec((8, _TNC), lambda j, i: (jnp.minimum((i + 1) * r8, t // 8 - 1), colmap(j)))
    return [main, prev, nxt]


def _conv3(x, xp, xn, w, b=None):
    y = xp * w[0:1, :] + x * w[1:2, :] + xn * w[2:3, :]
    return y if b is None else b + y


def _conv_fwd(name, a, cw, cb):
    t = a.shape[0]

    def fn(ids, ag, agp, agn, av, avp, avn, wg, wv, bg, bv):
        vp, vn = _seq_edges(ids[1], t)
        cg = _conv3(ag, *_shift_rows(ag, agp, agn, vp, vn), wg, bg)
        cv = _conv3(av, *_shift_rows(av, avp, avn, vp, vn), wv, bv)
        return (jax.nn.silu(cg) * cv,)

    gcol, vcol = (lambda j: j), (lambda j: j + _NJ)
    wspec = lambda cm: pl.BlockSpec((3, _TNC), lambda j, i: (0, cm(j)))
    bspec = lambda cm: pl.BlockSpec((1, _TNC), lambda j, i: (0, cm(j)))
    ins = [(a, s) for s in _halo_specs(t, gcol) + _halo_specs(t, vcol)]
    ins += [(cw, wspec(gcol)), (cw, wspec(vcol)), (cb, bspec(gcol)), (cb, bspec(vcol))]
    return _rows_call(name, fn, (_NJ, t // TM), ins,
                      [(_sds((t, FFN), BF16), pl.BlockSpec((TM, _TNC), lambda j, i: (i, j)), False)])[0]


def _conv_bwd(name, a, cw, cb, dact):
    t = a.shape[0]
    n = TM + 16

    def fn(ids, ag, agp, agn, av, avp, avn, dv, dvp, dvn, wg, wv, bg, bv):
        vp, vn = _seq_edges(ids[1], t)
        ext = lambda x, xp, xn: jnp.concatenate([jnp.where(vp, xp, 0.0), x, jnp.where(vn, xn, 0.0)], axis=0)
        up, dn = (lambda x: pltpu.roll(x, 1, 0)), (lambda x: pltpu.roll(x, n - 1, 0))
        main = lambda y: y[8:8 + TM]
        eg, ev, ed = ext(ag, agp, agn), ext(av, avp, avn), ext(dv, dvp, dvn)
        cg = _conv3(eg, up(eg), dn(eg), wg, bg)
        cv = _conv3(ev, up(ev), dn(ev), wv, bv)
        s = jax.nn.sigmoid(cg)
        rid = lax.broadcasted_iota(jnp.int32, (3, eg.shape[1]), 0)
        das, dws, dbs = [], [], []
        for dc, w, x in ((ed * cv * s * (1.0 + cg * (1.0 - s)), wg, eg), (ed * cg * s, wv, ev)):
            das.append(main(dn(dc) * w[0:1, :] + dc * w[1:2, :] + up(dc) * w[2:3, :]))
            dcm = main(dc)
            sums = [jnp.sum(dcm * main(y), axis=0, keepdims=True) for y in (up(x), x, dn(x))]
            dws.append(jnp.where(rid == 0, sums[0], jnp.where(rid == 1, sums[1], sums[2])))
            dbs.append(jnp.sum(dcm, axis=0, keepdims=True))
        return jnp.stack(das), jnp.stack(dws), jnp.stack(dbs)

    gcol, vcol = (lambda j: j), (lambda j: j + _NJ)
    wspec = lambda cm: pl.BlockSpec((3, _TNC), lambda j, i: (0, cm(j)))
    bspec = lambda cm: pl.BlockSpec((1, _TNC), lambda j, i: (0, cm(j)))
    ins = [(a, s) for s in _halo_specs(t, gcol) + _halo_specs(t, vcol)] + [(dact, s) for s in _halo_specs(t, gcol)]
    ins += [(cw, wspec(gcol)), (cw, wspec(vcol)), (cb, bspec(gcol)), (cb, bspec(vcol))]
    return _rows_call(name, fn, (_NJ, t // TM), ins,
                      [(_sds((2, t, FFN), BF16), pl.BlockSpec((2, TM, _TNC), lambda j, i: (0, i, j)), False),
                       (_sds((2, 3, FFN), F32), pl.BlockSpec((2, 3, _TNC), lambda j, i: (0, 0, j)), True),
                       (_sds((2, 1, FFN), F32), pl.BlockSpec((2, 1, _TNC), lambda j, i: (0, 0, j)), True)])


_TNA = 512


def _adaln_fwd(name, cond, w, b):
    fn = lambda ids, cv, wv, bv: ((bdot(jax.nn.silu(cv), wv[0], 1, 0) + bv[0])[None],)
    return _rows_call(name, fn, (2, ADA_LOC // _TNA),
                      [(cond, _full((16, D))), (w, pl.BlockSpec((1, D, _TNA), lambda l, j: (l, 0, j))),
                       (b, pl.BlockSpec((1, 1, _TNA), lambda l, j: (l, 0, j)))],
                      [(_sds((2, 16, ADA_LOC), F32), pl.BlockSpec((1, 16, _TNA), lambda l, j: (l, 0, j)), False)])[0]


def _adaln_bwd(name, c8, cc8, w, dl, dc):
    def fn(ids, cv, ccv, wv, dlv, dcv):
        dcs = jnp.broadcast_to(jnp.sum(dcv[0], axis=0, keepdims=True), dcv[0].shape)
        dw = hdot(jax.nn.silu(cv), dlv[0], 0, 0) + hdot(jax.nn.silu(ccv), dcs, 0, 0)
        s = jax.nn.sigmoid(ccv)
        rid = lax.broadcasted_iota(jnp.int32, ccv.shape, 0)
        dcc = jnp.where(rid == 0, bdot(dcs, wv[0], 1, 1) * s * (1.0 + ccv * (1.0 - s)), 0.0)
        return dw[None], dcc

    dspec = pl.BlockSpec((1, 8, _TNA), lambda l, j: (l, 0, j))
    return _rows_call(name, fn, (2, ADA_LOC // _TNA),
                      [(c8, _full((8, D))), (cc8, _full((8, D))), (w, pl.BlockSpec((1, D, _TNA), lambda l, j: (l, 0, j))),
                       (dl, dspec), (dc, dspec)],
                      [(_sds((2, D, ADA_LOC), F32), pl.BlockSpec((1, D, _TNA), lambda l, j: (l, 0, j)), False),
                       (_sds((8, D), F32), _full((8, D)), True)], acc_axes=(0, 1))


def _adamw_fn(w, g, m, v):
    m = ADAM_B1 * m + (1.0 - ADAM_B1) * g
    v = ADAM_B2 * v + (1.0 - ADAM_B2) * (g * g)
    m_hat = m / (1.0 - ADAM_B1 ** ADAM_STEP)
    v_hat = v / (1.0 - ADAM_B2 ** ADAM_STEP)
    return -ADAM_LR * (m_hat / (jnp.sqrt(v_hat) + ADAM_EPS) + ADAM_WD * w), m, v


def _adamw(name, w, g, m, v):
    l, r, c = w.shape
    tr = _tile(r, max(8, (1 << 20) // (4 * c)), 8)
    spec = pl.BlockSpec((None, tr, c), lambda i, j: (i, j, 0))
    o = (_sds((l, r, c), F32), spec, False)
    return _rows_call(name, lambda ids, *a: _adamw_fn(*a), (l, r // tr), [(x, spec) for x in (w, g, m, v)], [o, o, o],
                      sem=("parallel", "parallel"))


def _coords():
    return lax.axis_index("x"), lax.axis_index("y"), lax.axis_index("c")


def _other_chips(x, y):
    return [(1 - x, y), (x, 1 - y), (1 - x, 1 - y)]


def _allgather_small(name, blk):
    m_per, n = blk.shape

    def body(x_ref, out_ref, send_sems, recv_sems, local_sem):
        x, y, c = _coords()
        me, sibling = (x, y, c), (x, y, 1 - c)
        chips = _other_chips(x, y)

        def rows(px, py, pc):
            return out_ref.at[pl.ds((4 * px + 2 * py + pc) * m_per, m_per), :]

        def copy(k, block, to, src=None):
            return pltpu.make_async_remote_copy(
                src_ref=rows(*block) if src is None else src, dst_ref=rows(*block), send_sem=send_sems.at[k],
                recv_sem=recv_sems.at[k], device_id=to, device_id_type=MESH)

        mine = pltpu.make_async_copy(x_ref, rows(*me), local_sem)
        mine.start()
        first = [copy(0, me, sibling, src=x_ref)]
        first += [copy(1 + j, me, (*chip, c), src=x_ref) for j, chip in enumerate(chips)]
        for cp in first:
            cp.start()
        passed = [copy(4 + j, (*chip, c), sibling) for j, chip in enumerate(chips)]
        for j, chip in enumerate(chips):
            copy(1 + j, (*chip, c), me).wait_recv()
            passed[j].start()
        copy(0, sibling, me).wait_recv()
        for j, chip in enumerate(chips):
            copy(4 + j, (*chip, 1 - c), me).wait_recv()
        for cp in first + passed:
            cp.wait_send()
        mine.wait()

    return pl.pallas_call(
        body, name=name, out_shape=_sds((N_DEV * m_per, n), blk.dtype),
        in_specs=[pl.BlockSpec(memory_space=pltpu.VMEM)], out_specs=pl.BlockSpec(memory_space=pltpu.VMEM),
        scratch_shapes=[pltpu.SemaphoreType.DMA((7,)), pltpu.SemaphoreType.DMA((7,)), pltpu.SemaphoreType.DMA],
        compiler_params=pltpu.CompilerParams(vmem_limit_bytes=VMEM_LIMIT),
    )(blk)


_ANY = pl.BlockSpec(memory_space=pl.ANY)


def _remote(src, dst, send_sems, recv_sems, s, to):
    return pltpu.make_async_remote_copy(src_ref=src, dst_ref=dst, send_sem=send_sems.at[s], recv_sem=recv_sems.at[s],
                                        device_id=to, device_id_type=MESH)


def _comm_call(name, body, ins, out_shapes, n_sems, n_local):
    return pl.pallas_call(
        body, name=name, out_shape=out_shapes, in_specs=[_ANY] * len(ins), out_specs=[_ANY] * len(out_shapes),
        scratch_shapes=[pltpu.SemaphoreType.DMA((n_sems,)), pltpu.SemaphoreType.DMA((n_sems,)),
                        pltpu.SemaphoreType.DMA((n_local,))],
    )(*ins)


def _allgather_layers(name, locs):
    n = len(locs)

    def body(*refs):
        ins, outs, (send_sems, recv_sems, local_sems) = refs[:n], refs[n:2 * n], refs[2 * n:]
        x, y, c = _coords()
        k = 2 * x + y
        sibling = (x, y, 1 - c)
        chips = _other_chips(x, y)
        first = [_remote(ins[t].at[c], outs[t].at[k, c], send_sems, recv_sems, 6 * t + j, (*chip, c))
                 for t in range(n) for j, chip in enumerate(chips)]
        for cp in first:
            cp.start()
        passed = []
        for t in range(n):
            for j, (cx, cy) in enumerate(chips):
                there = outs[t].at[2 * cx + cy, c]
                _remote(there, there, send_sems, recv_sems, 6 * t + j, sibling).wait_recv()
                passed.append(_remote(there, there, send_sems, recv_sems, 6 * t + 3 + j, sibling))
                passed[-1].start()
        for t in range(n):
            for j, (cx, cy) in enumerate(chips):
                there = outs[t].at[2 * cx + cy, 1 - c]
                _remote(there, there, send_sems, recv_sems, 6 * t + 3 + j, sibling).wait_recv()
        for cp in first + passed:
            cp.wait_send()

    outs = _comm_call(name, body, locs, [_sds((N_CHIP,) + a.shape, a.dtype) for a in locs], 6 * n, 1)
    k = 2 * lax.axis_index("x") + lax.axis_index("y")
    return [lax.dynamic_update_slice_in_dim(o, a[None], k, axis=0) for o, a in zip(outs, locs)]


def _rs_pair_exchange(name, g0, g1):
    n = len(g0)

    def body(*refs):
        a0, a1, outs, (send_sems, recv_sems, _) = refs[:n], refs[n:2 * n], refs[2 * n:3 * n], refs[3 * n:]
        x, y, c = _coords()

        def run(srcs):
            cps = [_remote(srcs[t], outs[t], send_sems, recv_sems, t, (x, y, 1 - c)) for t in range(n)]
            for cp in cps:
                cp.start()
            for cp in cps:
                cp.wait()

        pl.when(c == 0)(lambda: run(a1))
        pl.when(c == 1)(lambda: run(a0))

    return _comm_call(name, body, list(g0) + list(g1), [_sds(a.shape, a.dtype) for a in g0], n, 1)


def _ew2d(name, fn, ins, out_dtype):
    shape = ins[0].shape
    r, c = _prod(shape[:-1]), shape[-1]
    tr = _tile(r, max(8, (1 << 20) // (4 * c)), 8)
    spec = pl.BlockSpec((tr, c), lambda i: (i, 0))
    out = _rows_call(name, lambda ids, *a: (fn(*a),), (r // tr,), [(a.reshape(r, c), spec) for a in ins],
                     [(_sds((r, c), out_dtype), spec, False)], sem=("parallel",))[0]
    return out.reshape(shape)


def _rs_chip_exchange(name, s1):
    n = len(s1)

    def body(*refs):
        ins, outs, (send_sems, recv_sems, local_sems) = refs[:n], refs[n:2 * n], refs[2 * n:]
        x, y, c = _coords()
        k = 2 * x + y
        chips = _other_chips(x, y)
        cps = [_remote(ins[t].at[2 * cx + cy], outs[t].at[k], send_sems, recv_sems, 3 * t + j, (cx, cy, c))
               for t in range(n) for j, (cx, cy) in enumerate(chips)]
        for cp in cps:
            cp.start()
        for t in range(n):
            for j, (cx, cy) in enumerate(chips):
                there = outs[t].at[2 * cx + cy]
                _remote(there, there, send_sems, recv_sems, 3 * t + j, (cx, cy, c)).wait_recv()
        for cp in cps:
            cp.wait_send()

    outs = _comm_call(name, body, s1, [_sds(a.shape, a.dtype) for a in s1], 3 * n, 1)
    k = 2 * lax.axis_index("x") + lax.axis_index("y")
    own = [lax.dynamic_index_in_dim(a, k, axis=0, keepdims=True) for a in s1]
    return [lax.dynamic_update_slice_in_dim(o, a, k, axis=0) for o, a in zip(outs, own)]


def _sum_slots(name, a):
    s, r, cdim = a.shape
    tr = _tile(r, 512, 8)

    def fn(ids, av):
        tot = av[0]
        for i in range(1, s):
            tot = tot + av[i]
        return (tot,)

    return _rows_call(name, fn, (r // tr,), [(a, pl.BlockSpec((s, tr, cdim), lambda i: (0, i, 0)))],
                      [(_sds((r, cdim), F32), pl.BlockSpec((tr, cdim), lambda i: (i, 0)), False)], sem=("parallel",))[0]


def _pair_allgather(name, halves):
    n = len(halves)

    def body(*refs):
        ins, outs, (send_sems, recv_sems, local_sems) = refs[:n], refs[n:2 * n], refs[2 * n:]
        x, y, c = _coords()
        cps = [_remote(ins[t], outs[t].at[c], send_sems, recv_sems, t, (x, y, 1 - c)) for t in range(n)]
        for cp in cps:
            cp.start()
        for t in range(n):
            _remote(ins[t], outs[t].at[1 - c], send_sems, recv_sems, t, (x, y, 1 - c)).wait_recv()
        for cp in cps:
            cp.wait_send()

    outs = _comm_call(name, body, halves, [_sds((2,) + a.shape, a.dtype) for a in halves], n, 1)
    return [lax.dynamic_update_slice_in_dim(o, a[None], lax.axis_index("c"), axis=0) for o, a in zip(outs, halves)]


def _reduce_scatter(g0, g1):
    n = len(g0)
    got = _rs_pair_exchange("rs_pair_exchange", g0, g1)
    keep = lambda a, b, r: jnp.where(lax.axis_index("c") == 0, a, b) + r
    s1 = [_ew2d("rs_pair_add_%d" % t, keep, [g0[t], g1[t], got[t]], BF16) for t in range(n)]
    slots = _rs_chip_exchange("rs_chip_exchange", s1)
    red = [_sum_slots("rs_chip_sum_%d" % t, a.reshape(N_CHIP, -1, a.shape[-1])).reshape(a.shape[1:])
           for t, a in enumerate(slots)]
    return _pair_allgather("rs_pair_allgather", red)


PACK_C = 1024
_SHARDED = (("w_in", 1), ("w_br_a", 1), ("w_br_b", 1), ("w_br_c", 1), ("w_out", 0), ("w_ffn_up", 1), ("w_ffn_down", 0))
_SHARDED_SMALL = (("conv_w", (3, 2 * FFN), 1), ("w_alpha2", (2, 16, GLA_QK), 2), ("b_alpha", (2, GLA_QK), 1))


def _prod(shape):
    n = 1
    for s in shape:
        n *= s
    return n


def _to_blocks(full, axis):
    shp = full.shape
    split = full.reshape(shp[:axis] + (N_CHIP, shp[axis] // N_CHIP) + shp[axis + 1:])
    return jnp.moveaxis(split, axis, 0)


def _from_blocks(blocks, axis):
    return jnp.concatenate([blocks[k] for k in range(N_CHIP)], axis=axis)


def _rope_tables(tx):
    pos = jnp.arange(tx, dtype=jnp.int32)
    inv_freq = 10000.0 ** (-jnp.arange(16, dtype=F32) / 16)
    ang_r = (pos // GRID_W).astype(F32)[:, None] * inv_freq
    ang_c = (pos % GRID_W).astype(F32)[:, None] * inv_freq
    ang = jnp.concatenate([ang_r, ang_r, ang_c, ang_c], axis=-1)
    sign = jnp.concatenate([-jnp.ones((16,), F32), jnp.ones((16,), F32)] * 2)
    cos = jnp.concatenate([jnp.ones((TC, HD), F32), jnp.cos(ang)], axis=0)
    sin = jnp.concatenate([jnp.zeros((TC, HD), F32), jnp.sin(ang) * sign], axis=0)
    return jnp.tile(cos, (1, 2)), jnp.tile(sin, (1, 2))


def _lane_consts():
    l = jnp.arange(512)
    seg = (l[:, None] // HD == l[None, :] // HD).astype(F32) / HD
    partner = jnp.where(l % 32 < 16, l + 16, l - 16)
    perm = (l[:, None] == partner[None, :]).astype(F32)
    return seg, perm


def _heads(a, n):
    return a.reshape(a.shape[0], n, HD).transpose(1, 0, 2)


def _unheads(a):
    return a.transpose(1, 0, 2).reshape(a.shape[1], a.shape[0] * HD)


def _gather_f32_shards(shards):
    sizes = [_prod(a.shape) for a in shards]
    flat = jnp.concatenate([a.reshape(-1) for a in shards] + [jnp.zeros((16 * PACK_C - sum(sizes),), F32)])
    got = _allgather_small("gather_f32_shards", flat.reshape(16, PACK_C)).reshape(N_CHIP, 2, 16 * PACK_C)[:, 0]
    out, o = {}, 0
    for (n, _, ax), a, sz in zip(_SHARDED_SMALL, shards, sizes):
        out[n] = jnp.concatenate([got[k, o:o + sz].reshape(a.shape) for k in range(N_CHIP)], axis=ax + 1)
        o += sz
    return out


def _layer_params(l, wfull, small):
    w2 = small["w_alpha2_full"][l]
    w2pad = jnp.zeros((128, 512), F32).at[0:16, 0:256].set(w2[0]).at[16:32, 256:512].set(w2[1])
    full = {n: _from_blocks(wfull[n][:, l], ax) for n, ax in _SHARDED}
    return dict(
        w_in=_to_new_cols(full["w_in"]), wa=full["w_br_a"], wb=full["w_br_b"], wc=full["w_br_c"],
        w_out=full["w_out"], w_up=full["w_ffn_up"], w_down=full["w_ffn_down"],
        cw=small["conv_w_full"][l], cb=small["conv_b"][l][None], w2=w2pad,
        b2=small["b_alpha_full"][l].reshape(1, 512),
        g1=small["norm1_g"][l][None], g2=small["norm2_g"][l][None], gq=jnp.tile(small["q_norm_g"][l], 8)[None],
        gk=jnp.tile(small["k_norm_g"][l], 2)[None], ggm=small["gmlp_norm_g"][l][None], ws=small["w_spatial"][l],
        bst=small["b_spatial"][l].T, ggl=small["gla_norm_g"][l][None])


def _layer_fwd(l, last, x, h1, mod, P, tabs):
    cos, sin, seg, perm = tabs
    n = "l%d_" % l
    s = dict(x=x, h1=h1)
    p = _mm(n + "in_proj", h1, P["w_in"], "nn", F32, tm_t=768, tn_t=2176, j_outer=True)
    s["p"] = p
    s["gm"] = _gmlp_fwd(n + "gmlp", p, P["ggm"], P["ws"], P["bst"])
    qr, kr, vb = _qk_fwd(n + "qk_prep", p, P["gq"], P["gk"], cos, sin, seg, perm)
    qx, qc, kh, vh = _heads(qr[TC:], NQ), _heads(qr[:TC], NQ), _heads(kr, NKV), _heads(vb, NKV)
    s["qx"], s["qc"], s["kh"], s["vh"] = qx, qc, kh, vh
    one_hot = (jnp.arange(HD) == 0).astype(BF16)
    v1 = jnp.concatenate([vh, jnp.broadcast_to(one_hot, vh.shape)], axis=-1)
    ox, lse_x = _attn_fwd(n + "attn_x", qx, kh, v1)
    s["ox"], s["lse_x"] = ox, lse_x
    if last:
        oc = jnp.zeros((NQ, TC, HD), F32)
    else:
        oc, lse_c = _attn_fwd(n + "attn_c", qc, kh[:, :TC], v1[:, :TC])
        s["oc"], s["lse_c"] = oc, lse_c
    s["att"] = jnp.concatenate([_unheads(oc), _unheads(ox)], axis=0).astype(BF16)
    la = _decay_fwd(n + "gla_decay", p, P["w2"], P["b2"])
    s["la"] = la
    s["of"], s["sf"], s["ob"], s["sb"] = _gla_fwd(n + "gla_scan", p, la)
    s["gla"] = _gla_out_fwd(n + "gla_out", s["of"], s["ob"], p, P["ggl"])
    s["merged"] = _merge_fwd(n + "merge", s["gm"], s["att"], s["gla"], P["wa"], P["wb"], P["wc"], p)
    s["mix"] = _mm(n + "out_proj", s["merged"], P["w_out"], "nn", F32)
    s["x_mid"], s["h2"] = _res_nm_fwd(n + "res1_norm2", x, s["mix"], mod, 2, mod, P["g2"], 3, 4)
    s["a"] = _mm(n + "ffn_up", s["h2"], P["w_up"], "nn", F32, j_outer=True)
    s["act"] = _conv_fwd(n + "conv_gate", s["a"], P["cw"], P["cb"])
    s["f"] = _mm(n + "ffn_down", s["act"], P["w_down"], "nn", F32)
    return s


def _layer_bwd(l, last, s, mod, P, tabs, dx_mid, df, gw):
    cos, sin, seg, perm = tabs
    n = "l%d_b_" % l
    t = dx_mid.shape[0]
    p = s["p"]
    gw["w_ffn_down"] = _mm(n + "ffn_down_w", s["act"], df, "tn", F32, tm_t=1408)
    dact = _mm(n + "ffn_down_x", df, P["w_down"], "nt", F32)
    da, dcw, dcb = _conv_bwd(n + "conv_gate", s["a"], P["cw"], P["cb"], dact)
    gw["conv_w"], gw["conv_b"] = dcw.transpose(1, 0, 2).reshape(3, 2 * FFN), dcb.reshape(2 * FFN)
    gw["w_ffn_up"] = _mm(n + "ffn_up_w", s["h2"], da, "tn", F32, chip_blocks=True)
    dh2 = _mm(n + "ffn_up_x", da, P["w_up"], "nt", F32)
    dx, dmix, dmod_a, dmod_b, dg2 = _res_nm_bwd(n + "res1_norm2", s["x"], s["mix"], mod, 2, mod, P["g2"], 3, 4, dx_mid, dh2)
    dmod = dmod_a + dmod_b
    gw["norm2_g"] = dg2[0]
    gw["w_out"] = _mm(n + "out_proj_w", s["merged"], dmix, "tn", F32)
    dmerged = _mm(n + "out_proj_x", dmix, P["w_out"], "nt", F32)
    dya, dyb, dyc, dga, dgb, dgc = _merge_bwd(n + "merge", s["gm"], s["att"], s["gla"], P["wa"], P["wb"], P["wc"], p, dmerged)
    gw["w_br_a"] = _mm(n + "br_a_w", s["gm"], dya, "tn", F32)
    gw["w_br_b"] = _mm(n + "br_b_w", s["att"], dyb, "tn", F32)
    gw["w_br_c"] = _mm(n + "br_c_w", s["gla"], dyc, "tn", F32)
    dgm = _mm(n + "br_a_x", dya, P["wa"], "nt", F32)
    datt = _mm(n + "br_b_x", dyb, P["wb"], "nt", F32)
    dgla = _mm(n + "br_c_x", dyc, P["wc"], "nt", F32)
    du, dv_g, dggm, dws, dbst = _gmlp_bwd(n + "gmlp", p, P["ggm"], P["ws"], P["bst"], dgm)
    gw["gmlp_norm_g"], gw["w_spatial"], gw["b_spatial"] = dggm[0], dws, dbst.T
    kh, vh = s["kh"], s["vh"]
    row = lambda a: a.reshape(a.shape[0], 1, a.shape[1])
    dqx, dkh, dvh = _attn_bwd(n + "attn_x", s["qx"], kh, vh, s["ox"], _heads(datt[TC:], NQ), row(s["lse_x"]))
    if last:
        dqc = jnp.zeros((NQ, TC, HD), F32)
    else:
        dqc, dkc, dvc = _attn_bwd(n + "attn_c", s["qc"], kh[:, :TC], vh[:, :TC], s["oc"], _heads(datt[:TC], NQ),
                                  row(s["lse_c"]))
        pad = jnp.zeros((NKV, t - TC, HD), F32)
        dkh = dkh + jnp.concatenate([dkc, pad], axis=1)
        dvh = dvh + jnp.concatenate([dvc, pad], axis=1)
    dqr = jnp.concatenate([_unheads(dqc), _unheads(dqx)], axis=0)
    dq, dk, dgq, dgk = _qk_bwd(n + "qk_prep", p, P["gq"], P["gk"], cos, sin, seg, perm, dqr, _unheads(dkh))
    gw["q_norm_g"], gw["k_norm_g"] = dgq.reshape(8, HD).sum(0), dgk.reshape(2, HD).sum(0)
    dv_att = _unheads(dvh).astype(BF16)
    do, dr, dggl = _gla_out_bwd(n + "gla_out", s["of"], s["ob"], p, P["ggl"], dgla)
    gw["gla_norm_g"] = dggl[0]
    scans = _gla_bwd(n + "gla_scan", p, s["la"], s["sf"], s["sb"], do)
    dab, dw2, db2, dglq, dglk, dglv = _decay_bwd(n + "gla_decay", p, P["w2"], P["b2"], scans[:4], scans[4:])
    gw["w_alpha2"] = jnp.stack([dw2[0:16, 0:256], dw2[16:32, 256:512]])
    gw["b_alpha"] = db2.reshape(2, 256)
    dp = jnp.concatenate([dga, dgb, dgc, du, dv_g, dq, dglv, dr, dglq, dglk, dk, dv_att, dab], axis=-1)
    gw["w_in"] = _to_ref_cols(_mm(n + "in_proj_w", s["h1"], dp, "tn", F32, tn_t=2176, tk_t=768))
    dh1 = _mm(n + "in_proj_x", dp, P["w_in"], "nt", F32, tk_t=2176)
    return dx, dh1, dmod


_SMALL = (("norm1_g", (2, D)), ("norm2_g", (2, D)), ("q_norm_g", (2, HD)), ("k_norm_g", (2, HD)), ("gmlp_norm_g", (2, GW)),
          ("gla_norm_g", (2, GLA_V)), ("w_spatial", (2, 4, 128, 128)), ("b_spatial", (2, 4, 128)), ("conv_b", (2, 2 * FFN)),
          ("final_norm_g", (D,))) + tuple((n, (2,) + s) for n, s, _ in _SHARDED_SMALL)
_SMALL_N = 2 * 2 * ADA_W + sum(_prod(s) for _, s in _SMALL)
_SMALL_R = -(-_SMALL_N // (PACK_C * 8)) * 8


def _mod_tables(c, c_ctx, w_ada, b_ada, k):
    x, y, cc = _coords()
    me = 4 * x + 2 * y + cc
    c_all = _allgather_small("gather_c", jnp.concatenate([c, jnp.zeros((7, D), F32)], axis=0))
    c8 = c_all.reshape(N_DEV, 8, D)[:, 0]
    cond = jnp.concatenate([c8, c_ctx[None], jnp.zeros((7, D), F32)], axis=0)
    b_loc = lax.dynamic_slice_in_dim(b_ada, k * ADA_LOC, ADA_LOC, axis=1)[:, None, :]
    m_loc = _adaln_fwd("adaln", cond, w_ada, b_loc)
    m_all = _allgather_small("gather_mod", m_loc.reshape(32, ADA_LOC)).reshape(N_CHIP, 2, 2, 16, ADA_LOC)[:, 0]
    m_all = m_all.transpose(1, 2, 0, 3).reshape(2, 16, ADA_W)
    rows = jnp.stack([m_all[:, 8], lax.dynamic_index_in_dim(m_all, me, axis=1, keepdims=False)], axis=1)
    return rows.reshape(2, 2, 6, D), c8


def _step(x, c, ctx, c_ctx, W, tgt):
    xc, yc, cc = _coords()
    k = 2 * xc + yc
    tx = x.shape[0]
    t = TC + tx
    small = {n: W[n] for n, _ in _SMALL}
    for n, a in _gather_f32_shards([W[n] for n, _, _ in _SHARDED_SMALL]).items():
        small[n + "_full"] = a

    gathered = _allgather_layers("gather_weights", [W[n].astype(BF16) for n, _ in _SHARDED])
    wfull = {n: a for (n, _), a in zip(_SHARDED, gathered)}
    mods, c8 = _mod_tables(c, c_ctx, W["w_ada"], W["b_ada"], k)
    tabs = _rope_tables(tx) + _lane_consts()
    params = [_layer_params(l, wfull, small) for l in range(2)]

    xs = jnp.concatenate([ctx, x], axis=0)
    h1 = _nm_fwd("l0_norm1", xs, mods[0], params[0]["g1"], 0, 1)
    s0 = _layer_fwd(0, False, xs, h1, mods[0], params[0], tabs)
    x1, h1b = _res_nm_fwd("l0_res2_norm1", s0["x_mid"], s0["f"], mods[0], 5, mods[1], params[1]["g1"], 0, 1)
    s1 = _layer_fwd(1, True, x1, h1b, mods[1], params[1], tabs)
    loss, dxm_l, df_l, dmod_head, dgf = _head("head", s1["x_mid"], s1["f"], mods[1], W["final_norm_g"][None], tgt)

    gws = [dict(), dict()]
    dx1, dh1b, dmod1 = _layer_bwd(1, True, s1, mods[1], params[1], tabs, dxm_l, df_l, gws[1])
    dxm0, df0, dmod0_g, dmod1_s, dg1b = _res_nm_bwd("l0_b_res2_norm1", s0["x_mid"], s0["f"], mods[0], 5, mods[1],
                                                    params[1]["g1"], 0, 1, dx1, dh1b)
    gws[1]["norm1_g"] = dg1b[0]
    dx0, dh1, dmod0 = _layer_bwd(0, False, s0, mods[0], params[0], tabs, dxm0, df0, gws[0])
    grad_x, dmod0_s, dg1 = _nm_bwd("l0_b_norm1", xs, mods[0], params[0]["g1"], 0, 1, dx0, dh1)
    gws[0]["norm1_g"] = dg1[0]
    dmods = jnp.stack([dmod0 + dmod0_g + dmod0_s, dmod1 + dmod1_s + dmod_head])

    stk = {n: jnp.stack([gws[0][n], gws[1][n]]) for n, _ in _SMALL if n != "final_norm_g"}
    stk["final_norm_g"] = dgf[0]
    flat = jnp.concatenate([dmods.reshape(-1)] + [stk[n].reshape(-1) for n, _ in _SMALL])
    flat = jnp.concatenate([flat, jnp.zeros((_SMALL_R * PACK_C - _SMALL_N,), F32)]).reshape(_SMALL_R, PACK_C)
    every = _allgather_small("gather_small_grads", flat).reshape(N_DEV, _SMALL_R, PACK_C)
    tot = _sum_slots("sum_small_grads", every).reshape(-1)
    grads, o = {}, 2 * 2 * ADA_W
    for n, shp in _SMALL:
        grads[n] = tot[o:o + _prod(shp)].reshape(shp)
        o += _prod(shp)
    grads["b_ada"] = tot[:2 * 2 * ADA_W].reshape(2, 2, ADA_W).sum(axis=1)

    dm_every = every[:, :2 * 2 * ADA_W // PACK_C].reshape(N_DEV, 2, 2, ADA_W)
    dm_loc = lax.dynamic_slice_in_dim(dm_every, k * ADA_LOC, ADA_LOC, axis=3).transpose(1, 2, 0, 3)
    cc8 = jnp.concatenate([c_ctx[None], jnp.zeros((7, D), F32)], axis=0)
    grads["w_ada"], dcc = _adaln_bwd("adaln_b", c8, cc8, W["w_ada"], dm_loc[:, 1], dm_loc[:, 0])
    dcc_every = _allgather_small("gather_dcctx", dcc * 0.5).reshape(N_DEV, 8, D)
    grads["c_ctx"] = _sum_slots("sum_dcctx", dcc_every)[0]

    for n, shp, ax in _SHARDED_SMALL:
        grads[n] = lax.dynamic_slice_in_dim(grads[n], k * (shp[ax] // N_CHIP), shp[ax] // N_CHIP, axis=ax + 1)
    blocks = lambda g, n, ax: g if n == "w_ffn_up" else _to_blocks(g, ax)
    red = _reduce_scatter(*[[blocks(gws[l][n], n, ax) for n, ax in _SHARDED] for l in range(2)])
    grads.update({n: a for (n, _), a in zip(_SHARDED, red)})
    return loss[0, 0], grad_x, grads


_WEIGHTS = ("c_ctx", "w_ada", "b_ada", "norm1_g", "norm2_g", "w_in", "q_norm_g", "k_norm_g", "gmlp_norm_g", "w_spatial",
            "b_spatial", "w_alpha2", "b_alpha", "gla_norm_g", "w_br_a", "w_br_b", "w_br_c", "w_out", "w_ffn_up", "conv_w",
            "conv_b", "w_ffn_down", "final_norm_g")
_BIG = ("w_ada", "w_in", "w_br_a", "w_br_b", "w_br_c", "w_out", "w_ffn_up", "w_ffn_down")


def _update(W, G, M, V):
    delta, new_m, new_v = {}, {}, {}
    for n in _BIG:
        delta[n], new_m[n], new_v[n] = _adamw("adamw_" + n, W[n], G[n], M[n], V[n])
    rest = [n for n in _WEIGHTS if n not in _BIG]
    tot = sum(_prod(W[n].shape) for n in rest)
    rows = -(-tot // (PACK_C * 8)) * 8

    def cat(dct):
        flat = jnp.concatenate([dct[n].reshape(-1) for n in rest] + [jnp.zeros((rows * PACK_C - tot,), F32)])
        return flat.reshape(1, rows, PACK_C)

    outs = _adamw("adamw_small", cat(W), cat(G), cat(M), cat(V))
    o = 0
    for n in rest:
        sz, shp = _prod(W[n].shape), W[n].shape
        delta[n], new_m[n], new_v[n] = (a.reshape(-1)[o:o + sz].reshape(shp) for a in outs)
        o += sz
    return delta, new_m, new_v


def kernel(x, c, ctx, c_ctx, w_ada, b_ada, norm1_g, norm2_g, w_in, q_norm_g, k_norm_g, gmlp_norm_g, w_spatial, b_spatial, w_alpha2, b_alpha, gla_norm_g, w_br_a, w_br_b, w_br_c, w_out, w_ffn_up, conv_w, conv_b, w_ffn_down, final_norm_g, loss_target, m_c_ctx, m_w_ada, m_b_ada, m_norm1_g, m_norm2_g, m_w_in, m_q_norm_g, m_k_norm_g, m_gmlp_norm_g, m_w_spatial, m_b_spatial, m_w_alpha2, m_b_alpha, m_gla_norm_g, m_w_br_a, m_w_br_b, m_w_br_c, m_w_out, m_w_ffn_up, m_conv_w, m_conv_b, m_w_ffn_down, m_final_norm_g, v_c_ctx, v_w_ada, v_b_ada, v_norm1_g, v_norm2_g, v_w_in, v_q_norm_g, v_k_norm_g, v_gmlp_norm_g, v_w_spatial, v_b_spatial, v_w_alpha2, v_b_alpha, v_gla_norm_g, v_w_br_a, v_w_br_b, v_w_br_c, v_w_out, v_w_ffn_up, v_conv_w, v_conv_b, v_w_ffn_down, v_final_norm_g):
    W = dict(c_ctx=c_ctx, w_ada=w_ada, b_ada=b_ada, norm1_g=norm1_g, norm2_g=norm2_g, w_in=w_in, q_norm_g=q_norm_g,
             k_norm_g=k_norm_g, gmlp_norm_g=gmlp_norm_g, w_spatial=w_spatial, b_spatial=b_spatial, w_alpha2=w_alpha2,
             b_alpha=b_alpha, gla_norm_g=gla_norm_g, w_br_a=w_br_a, w_br_b=w_br_b, w_br_c=w_br_c, w_out=w_out,
             w_ffn_up=w_ffn_up, conv_w=conv_w, conv_b=conv_b, w_ffn_down=w_ffn_down, final_norm_g=final_norm_g)
    M = dict(c_ctx=m_c_ctx, w_ada=m_w_ada, b_ada=m_b_ada, norm1_g=m_norm1_g, norm2_g=m_norm2_g, w_in=m_w_in,
             q_norm_g=m_q_norm_g, k_norm_g=m_k_norm_g, gmlp_norm_g=m_gmlp_norm_g, w_spatial=m_w_spatial,
             b_spatial=m_b_spatial, w_alpha2=m_w_alpha2, b_alpha=m_b_alpha, gla_norm_g=m_gla_norm_g, w_br_a=m_w_br_a,
             w_br_b=m_w_br_b, w_br_c=m_w_br_c, w_out=m_w_out, w_ffn_up=m_w_ffn_up, conv_w=m_conv_w, conv_b=m_conv_b,
             w_ffn_down=m_w_ffn_down, final_norm_g=m_final_norm_g)
    V = dict(c_ctx=v_c_ctx, w_ada=v_w_ada, b_ada=v_b_ada, norm1_g=v_norm1_g, norm2_g=v_norm2_g, w_in=v_w_in,
             q_norm_g=v_q_norm_g, k_norm_g=v_k_norm_g, gmlp_norm_g=v_gmlp_norm_g, w_spatial=v_w_spatial,
             b_spatial=v_b_spatial, w_alpha2=v_w_alpha2, b_alpha=v_b_alpha, gla_norm_g=v_gla_norm_g, w_br_a=v_w_br_a,
             w_br_b=v_w_br_b, w_br_c=v_w_br_c, w_out=v_w_out, w_ffn_up=v_w_ffn_up, conv_w=v_conv_w, conv_b=v_conv_b,
             w_ffn_down=v_w_ffn_down, final_norm_g=v_final_norm_g)
    loss_local, grad_x, G = _step(x[0], c, ctx[0], c_ctx, W, loss_target[0])
    loss = lax.psum(loss_local, ("x", "y", "c"))
    delta, new_m, new_v = _update(W, G, M, V)
    return (loss, grad_x[None], *[G[n] for n in _WEIGHTS], *[delta[n] for n in _WEIGHTS],
            *[new_m[n] for n in _WEIGHTS], *[new_v[n] for n in _WEIGHTS])
```

```python
import functools

import jax
import jax.numpy as jnp
from jax import lax
from jax.experimental import pallas as pl
from jax.experimental.pallas import tpu as pltpu

F32 = jnp.float32
BF16 = jnp.bfloat16

D = 1024
TC = 256
GRID_W = 64
EPS = 1e-6
HD = 64
NQ = 8
NKV = 2
QG = NQ // NKV
GLA_H = 4
GLA_DK = 64
GLA_DV = 128
GLA_QK = 256
GLA_V = 512
GLA_CHUNK = 64
GLA_TAU = 16.0
GW = 512
FFN = 2816
IN_W = 6432
PW = 6528
ADA_W = 6 * D
N_CHIP = 4
N_DEV = 8
ADA_LOC = ADA_W // N_CHIP

ADAM_LR = 0.001
ADAM_B1 = 0.9
ADAM_B2 = 0.999
ADAM_EPS = 1e-08
ADAM_WD = 0.01
ADAM_STEP = 10

TM = 256
NCB = TC // TM
LANE = 128
VMEM_LIMIT = 48 * 1024 * 1024
MESH = pl.DeviceIdType.MESH

_COLS = (("gA", 3360, 1024), ("gB", 4384, 1024), ("gC", 5408, 1024), ("gu", 0, 512), ("gv", 512, 512),
         ("q", 1024, 512), ("glv", 2304, 512), ("gr", 2848, 512), ("glq", 1792, 256), ("glk", 2048, 256),
         ("k", 1536, 128), ("v", 1664, 128), ("ab", 2816, 32))
OFF = {}
_o = 0
for _n, _s, _w in _COLS:
    OFF[_n] = _o
    _o += max(_w, LANE)
assert _o == PW


def _to_new_cols(w):
    parts = [w[..., s:s + n] for _, s, n in _COLS]
    pad = jnp.zeros(w.shape[:-1] + (PW - IN_W,), w.dtype)
    return jnp.concatenate(parts + [pad], axis=-1)


def _to_ref_cols(w):
    by_start = sorted(_COLS, key=lambda t: t[1])
    return jnp.concatenate([w[..., OFF[n]:OFF[n] + wd] for n, _, wd in by_start], axis=-1)


def _tile(n, target, align=LANE):
    best = None
    t = align
    while t <= min(n, target):
        if n % t == 0:
            best = t
        t += align
    assert best is not None, (n, target, align)
    return best


def _cp(sem=None):
    return pltpu.CompilerParams(dimension_semantics=sem, vmem_limit_bytes=VMEM_LIMIT)


def _bdot_impl(a, b, ca, cb):
    return lax.dot_general(a.astype(BF16), b.astype(BF16), (((ca,), (cb,)), ((), ())),
                           preferred_element_type=F32)


@functools.partial(jax.custom_vjp, nondiff_argnums=(2, 3))
def bdot(a, b, ca, cb):
    return _bdot_impl(a, b, ca, cb)


def _bdot_fwd(a, b, ca, cb):
    return _bdot_impl(a, b, ca, cb), (a, b)


def _bdot_bwd(ca, cb, res, g):
    a, b = res
    da = bdot(g, b, 1, 1 - cb) if ca == 1 else bdot(b, g, 1 - cb, 1)
    db = bdot(a, g, 1 - ca, 0) if cb == 0 else bdot(g, a, 0, 1 - ca)
    return da.astype(a.dtype), db.astype(b.dtype)


bdot.defvjp(_bdot_fwd, _bdot_bwd)


def hdot(a, b, ca=1, cb=0):
    return lax.dot_general(a, b, (((ca,), (cb,)), ((), ())), precision=lax.Precision.HIGH,
                           preferred_element_type=F32)


def _rms(x, g):
    return x * lax.rsqrt(jnp.mean(x * x, axis=-1, keepdims=True) + EPS) * g


def _gelu(x):
    return 0.5 * x * (1.0 + jnp.tanh(0.7978845608028654 * (x + 0.044715 * (x * x * x))))


def _log_sigmoid(z):
    return jnp.minimum(z, 0.0) - jnp.log(1.0 + jnp.exp(-jnp.abs(z)))


def _sel(mod, is_lat, idx):
    return jnp.where(is_lat, mod[1, idx:idx + 1, :], mod[0, idx:idx + 1, :])


def _rows_call(name, fn, grid, ins, outs, acc_axes=None, sem=None):
    n_in = len(ins)
    flags = [o[2] for o in outs]
    if acc_axes is None:
        acc_axes = (len(grid) - 1,)

    def body(*refs):
        ids = tuple(pl.program_id(a) for a in range(len(grid)))
        res = fn(ids, *[r[...] for r in refs[:n_in]])
        for r, v, acc in zip(refs[n_in:], res, flags):
            if acc:
                first = functools.reduce(jnp.logical_and, [ids[a] == 0 for a in acc_axes])

                @pl.when(first)
                def _():
                    r[...] = jnp.zeros_like(r)
                r[...] += v.astype(r.dtype)
            else:
                r[...] = v.astype(r.dtype)

    return pl.pallas_call(
        body, name=name, grid=grid, in_specs=[s for _, s in ins], out_specs=[o[1] for o in outs],
        out_shape=[o[0] for o in outs],
        compiler_params=_cp(sem if sem is not None else ("arbitrary",) * len(grid)),
    )(*[a for a, _ in ins])


def _sds(shape, dtype):
    return jax.ShapeDtypeStruct(shape, dtype)


def _rowspec(width, off=0, tm=TM):
    assert off % width == 0
    return pl.BlockSpec((tm, width), lambda i, o=off // width: (i, o))


def _full(shape):
    nd = len(shape)
    return pl.BlockSpec(shape, lambda *a: (0,) * nd)


def _mm(name, a, b, mode, out_dtype, tm_t=1056, tn_t=1408, tk_t=1408, chip_blocks=False, j_outer=False):
    halves = a.ndim == 3 or b.ndim == 3
    if mode == "nn":
        (m, k), (_, n) = a.shape, b.shape
    elif mode == "nt":
        (m, k), (n, _) = a.shape[-2:], b.shape
        k *= a.ndim - 1
    else:
        (k, m), (_, n) = a.shape, b.shape[-2:]
        n *= b.ndim - 1
    tm = _tile(m, tm_t, 8 if m % LANE else LANE)
    tn = _tile(n // 2 if halves and mode == "tn" else n, tn_t)
    tk = _tile(k // 2 if halves and mode == "nt" else k, tk_t)
    nk = k // tk
    if mode == "nn":
        dims, a_spec, b_spec = ((1,), (0,)), pl.BlockSpec((tm, tk), lambda i, j, l: (i, l)), pl.BlockSpec((tk, tn), lambda i, j, l: (l, j))
    elif mode == "nt":
        dims, a_spec, b_spec = ((1,), (1,)), pl.BlockSpec((tm, tk), lambda i, j, l: (i, l)), pl.BlockSpec((tn, tk), lambda i, j, l: (j, l))
        if halves:
            a_spec = pl.BlockSpec((None, tm, tk), lambda i, j, l, h=nk // 2: (l // h, i, l % h))
    else:
        dims, a_spec, b_spec = ((0,), (0,)), pl.BlockSpec((tk, tm), lambda i, j, l: (l, i)), pl.BlockSpec((tk, tn), lambda i, j, l: (l, j))
        if halves:
            b_spec = pl.BlockSpec((None, tk, tn), lambda i, j, l, h=n // tn // 2: (j // h, l, j % h))

    def body(a_ref, b_ref, o_ref, *scratch):
        l = pl.program_id(2)
        part = lax.dot_general(a_ref[...].astype(BF16), b_ref[...].astype(BF16), (dims, ((), ())),
                               preferred_element_type=F32)
        if nk == 1:
            o_ref[...] = part.astype(o_ref.dtype)
            return
        acc_ref = scratch[0]

        @pl.when(l == 0)
        def _():
            acc_ref[...] = part

        @pl.when(l > 0)
        def _():
            acc_ref[...] += part

        @pl.when(l == nk - 1)
        def _():
            o_ref[...] = acc_ref[...].astype(o_ref.dtype)

    o_spec, o_shape = pl.BlockSpec((tm, tn), lambda i, j, l: (i, j)), _sds((m, n), out_dtype)
    if chip_blocks:
        assert tn * N_CHIP == n and tm == m
        o_spec, o_shape = pl.BlockSpec((None, tm, tn), lambda i, j, l: (j, 0, 0)), _sds((N_CHIP, m, tn), out_dtype)
    grid = (m // tm, n // tn, nk)
    if j_outer:
        swap = lambda spec: pl.BlockSpec(spec.block_shape, lambda j, i, l, f=spec.index_map: f(i, j, l))
        a_spec, b_spec, o_spec, grid = swap(a_spec), swap(b_spec), swap(o_spec), (n // tn, m // tm, nk)
    return pl.pallas_call(
        body, name=name, grid=grid, in_specs=[a_spec, b_spec], out_specs=o_spec, out_shape=o_shape,
        scratch_shapes=[pltpu.VMEM((tm, tn), F32)] if nk > 1 else [],
        compiler_params=_cp(("parallel", "parallel", "arbitrary")),
    )(a, b)


def _nm_fn(is_lat, x, mod, g, shift, scale):
    return _rms(x, g) * (1.0 + _sel(mod, is_lat, scale)) + _sel(mod, is_lat, shift)


def _res_nm_fn(is_lat, x, br, modg, gate, mods, g, shift, scale):
    xn = x + _sel(modg, is_lat, gate) * br
    return xn, _nm_fn(is_lat, xn, mods, g, shift, scale)


def _nm_fwd(name, x, mod, g, shift, scale):
    t = x.shape[0]
    fn = lambda ids, xv, mv, gv: (_nm_fn(ids[0] >= NCB, xv, mv, gv, shift, scale),)
    return _rows_call(name, fn, (t // TM,), [(x, _rowspec(D)), (mod, _full((2, 6, D))), (g, _full((1, D)))],
                      [(_sds((t, D), BF16), _rowspec(D), False)])[0]


def _nm_bwd(name, x, mod, g, shift, scale, dx_res, dh):
    t = x.shape[0]

    def fn(ids, xv, mv, gv, dxr, dhv):
        _, vjp = jax.vjp(lambda a, b, c: _nm_fn(ids[0] >= NCB, a, b, c, shift, scale), xv, mv, gv)
        dx, dm, dg = vjp(dhv)
        return dx + dxr, dm, dg

    lat = pl.BlockSpec((TM, D), lambda i: (jnp.maximum(i - NCB, 0), 0))
    return _rows_call(name, fn, (t // TM,),
                      [(x, _rowspec(D)), (mod, _full((2, 6, D))), (g, _full((1, D))), (dx_res, _rowspec(D)), (dh, _rowspec(D))],
                      [(_sds((t - TC, D), F32), lat, False), (_sds((2, 6, D), F32), _full((2, 6, D)), True),
                       (_sds((1, D), F32), _full((1, D)), True)])


def _res_nm_fwd(name, x, br, modg, gate, mods, g, shift, scale):
    t = x.shape[0]
    fn = lambda ids, xv, bv, mg, ms, gv: _res_nm_fn(ids[0] >= NCB, xv, bv, mg, gate, ms, gv, shift, scale)
    return _rows_call(name, fn, (t // TM,),
                      [(x, _rowspec(D)), (br, _rowspec(D)), (modg, _full((2, 6, D))), (mods, _full((2, 6, D))), (g, _full((1, D)))],
                      [(_sds((t, D), F32), _rowspec(D), False), (_sds((t, D), BF16), _rowspec(D), False)])


def _res_nm_bwd(name, x, br, modg, gate, mods, g, shift, scale, dx_res, dh):
    t = x.shape[0]

    def fn(ids, xv, bv, mg, ms, gv, dxr, dhv):
        f = lambda a, b, c, d, e: _res_nm_fn(ids[0] >= NCB, a, b, c, gate, d, e, shift, scale)
        _, vjp = jax.vjp(f, xv, bv, mg, ms, gv)
        return vjp((dxr, dhv))

    m26 = (_sds((2, 6, D), F32), _full((2, 6, D)), True)
    return _rows_call(name, fn, (t // TM,),
                      [(x, _rowspec(D)), (br, _rowspec(D)), (modg, _full((2, 6, D))), (mods, _full((2, 6, D))), (g, _full((1, D))),
                       (dx_res, _rowspec(D)), (dh, _rowspec(D))],
                      [(_sds((t, D), F32), _rowspec(D), False), (_sds((t, D), BF16), _rowspec(D), False), m26, m26,
                       (_sds((1, D), F32), _full((1, D)), True)])


def _head(name, x_mid, f, mod, gf, tgt):
    t = x_mid.shape[0]

    def fn(ids, xv, fv, mv, gv, tv):
        def loss_fn(a, b, c, d):
            y = _rms(a + c[1, 5:6, :] * b, d)
            e = y - tv
            return 0.5 * jnp.sum(jnp.mean(e * e, axis=-1))
        loss, grads = jax.value_and_grad(loss_fn, argnums=(0, 1, 2, 3))(xv, fv, mv, gv)
        return tuple(jnp.where(ids[0] >= NCB, v, 0.0) for v in (jnp.reshape(loss, (1, 1)),) + grads)

    return _rows_call(name, fn, (t // TM,),
                      [(x_mid, _rowspec(D)), (f, _rowspec(D)), (mod, _full((2, 6, D))), (gf, _full((1, D))),
                       (tgt, pl.BlockSpec((TM, D), lambda i: (jnp.maximum(i - NCB, 0), 0)))],
                      [(_sds((1, 1), F32), _full((1, 1)), True), (_sds((t, D), F32), _rowspec(D), False),
                       (_sds((t, D), BF16), _rowspec(D), False), (_sds((2, 6, D), F32), _full((2, 6, D)), True),
                       (_sds((1, D), F32), _full((1, D)), True)])


def _gmlp_fn(u, v, g, ws, bst):
    rows = []
    u, v = u.astype(F32), v.astype(F32)
    for r in range(u.shape[0] // 128):
        uu, vv = _gelu(u[128 * r:128 * r + 128]), _gelu(v[128 * r:128 * r + 128])
        cols = []
        for gi in range(4):
            sl = slice(128 * gi, 128 * gi + 128)
            f = bdot(ws[gi], _rms(vv[:, sl], g[:, sl]), 1, 0) + bst[:, gi:gi + 1]
            cols.append(uu[:, sl] * f)
        rows.append(jnp.concatenate(cols, axis=-1))
    return jnp.concatenate(rows, axis=0)


def _gmlp_ins(p, g, ws, bst):
    return [(p, _rowspec(GW, OFF["gu"])), (p, _rowspec(GW, OFF["gv"])), (g, _full((1, GW))),
            (ws, _full((4, 128, 128))), (bst, _full((128, 4)))]


def _gmlp_fwd(name, p, g, ws, bst):
    t = p.shape[0]
    return _rows_call(name, lambda ids, *a: (_gmlp_fn(*a),), (t // TM,), _gmlp_ins(p, g, ws, bst),
                      [(_sds((t, GW), BF16), _rowspec(GW), False)])[0]


def _gmlp_bwd(name, p, g, ws, bst, dgm):
    t = p.shape[0]

    def fn(ids, u, v, gv, wv, bv, dv):
        _, vjp = jax.vjp(_gmlp_fn, u, v, gv, wv, bv)
        return vjp(dv)

    return _rows_call(name, fn, (t // TM,), _gmlp_ins(p, g, ws, bst) + [(dgm, _rowspec(GW))],
                      [(_sds((t, GW), BF16), _rowspec(GW), False), (_sds((t, GW), BF16), _rowspec(GW), False),
                       (_sds((1, GW), F32), _full((1, GW)), True), (_sds((4, 128, 128), F32), _full((4, 128, 128)), True),
                       (_sds((128, 4), F32), _full((128, 4)), True)])


def _qk_fn(q, k, gq, gk, cos, sin, seg, perm):
    cq, sq = jnp.concatenate([cos] * 4, axis=-1), jnp.concatenate([sin] * 4, axis=-1)
    q, k = q.astype(F32), k.astype(F32)
    qn = q * lax.rsqrt(hdot(q * q, seg) + EPS) * gq
    kn = k * lax.rsqrt(hdot(k * k, seg[:128, :128]) + EPS) * gk
    qr = qn * cq + hdot(qn, perm) * sq
    kr = kn * cos + hdot(kn, perm[:128, :128]) * sin
    return qr * (HD ** -0.5), kr


def _qk_ins(p, gq, gk, cos, sin, seg, perm):
    return [(p, _rowspec(512, OFF["q"])), (p, _rowspec(128, OFF["k"])), (gq, _full((1, 512))), (gk, _full((1, 128))),
            (cos, _rowspec(128)), (sin, _rowspec(128)), (seg, _full((512, 512))), (perm, _full((512, 512)))]


def _qk_fwd(name, p, gq, gk, cos, sin, seg, perm):
    t = p.shape[0]
    fn = lambda ids, q, k, a, b, c, s, sg, pm, v: _qk_fn(q, k, a, b, c, s, sg, pm) + (v,)
    return _rows_call(name, fn, (t // TM,), _qk_ins(p, gq, gk, cos, sin, seg, perm) + [(p, _rowspec(128, OFF["v"]))],
                      [(_sds((t, 512), BF16), _rowspec(512), False), (_sds((t, 128), BF16), _rowspec(128), False),
                       (_sds((t, 128), BF16), _rowspec(128), False)])


def _qk_bwd(name, p, gq, gk, cos, sin, seg, perm, dqr, dkr):
    t = p.shape[0]

    def fn(ids, q, k, a, b, c, s, sg, pm, dq, dk):
        _, vjp = jax.vjp(lambda q_, k_, a_, b_: _qk_fn(q_, k_, a_, b_, c, s, sg, pm), q, k, a, b)
        return vjp((dq, dk))

    return _rows_call(name, fn, (t // TM,),
                      _qk_ins(p, gq, gk, cos, sin, seg, perm) + [(dqr, _rowspec(512)), (dkr, _rowspec(128))],
                      [(_sds((t, 512), BF16), _rowspec(512), False), (_sds((t, 128), BF16), _rowspec(128), False),
                       (_sds((1, 512), F32), _full((1, 512)), True), (_sds((1, 128), F32), _full((1, 128)), True)])


_ATT_TQ = 1024
_ATT_TK = 768


def _attn_fwd(name, q, k, v):
    h, tq_all, _ = q.shape
    hkv, tk_all, _ = k.shape
    tq, tk = _tile(tq_all, _ATT_TQ), _tile(tk_all, _ATT_TK)
    nkc = tk_all // tk

    def body(q_ref, k_ref, v_ref, o_ref, lse_ref):
        qv = q_ref[...].reshape(QG * tq, HD)

        def step(j, carry):
            m, acc = carry
            off = pl.multiple_of(j * tk, tk)
            kk, vv = k_ref[0, pl.ds(off, tk), :], v_ref[0, pl.ds(off, tk), :]
            s = lax.dot_general(qv, kk, (((1,), (1,)), ((), ())), preferred_element_type=F32)
            m_new = jnp.maximum(m, jnp.max(s, axis=-1, keepdims=True))
            pr = jnp.exp(s - m_new)
            acc = jnp.exp(m - m_new) * acc + jnp.dot(pr.astype(BF16), vv, preferred_element_type=F32)
            return m_new, acc

        init = (jnp.full((QG * tq, 1), -jnp.inf, F32), jnp.zeros((QG * tq, 2 * HD), F32))
        m, acc = lax.fori_loop(0, nkc, step, init)
        l = acc[:, HD:HD + 1]
        o_ref[...] = (acc[:, :HD] / l).reshape(QG, tq, HD)
        lse_ref[...] = (m + jnp.log(l)).reshape(QG, tq, 1)

    kv_spec = pl.BlockSpec((1, tk_all, HD), lambda g, i: (g, 0, 0))
    v1_spec = pl.BlockSpec((1, tk_all, 2 * HD), lambda g, i: (g, 0, 0))
    qspec = pl.BlockSpec((QG, tq, HD), lambda g, i: (g, i, 0))
    return pl.pallas_call(
        body, name=name, grid=(hkv, tq_all // tq), in_specs=[qspec, kv_spec, v1_spec],
        out_specs=[qspec, pl.BlockSpec((QG, tq, 1), lambda g, i: (g, i, 0))],
        out_shape=[_sds((h, tq_all, HD), F32), _sds((h, tq_all, 1), F32)],
        compiler_params=_cp(("parallel", "parallel")),
    )(q, k, v)


def _attn_bwd(name, q, k, v, o, do, lse_row):
    h, tq_all, _ = q.shape
    hkv, tk_all, _ = k.shape
    tq, tk = _tile(tq_all, 1024), _tile(tk_all, 1408)

    def body(q_ref, k_ref, v_ref, o_ref, do_ref, lse_ref, dq_ref, dk_ref, dv_ref, dl_ref):
        i, j = pl.program_id(1), pl.program_id(2)

        @pl.when(j == 0)
        def _():
            ones = jnp.ones((8, HD), F32)
            for g in range(QG):
                dl_ref[g] = hdot(ones, do_ref[g] * o_ref[g], 1, 1)

        kk, vv = k_ref[0], v_ref[0]
        dk_acc, dv_acc = jnp.zeros((tk, HD), F32), jnp.zeros((tk, HD), F32)
        for g in range(QG):
            qv, dob = q_ref[g], do_ref[g].astype(BF16)
            st = lax.dot_general(kk, qv, (((1,), (1,)), ((), ())), preferred_element_type=F32)
            pt = jnp.exp(st - lse_ref[g])
            dv_acc += jnp.dot(pt.astype(BF16), dob, preferred_element_type=F32)
            dpt = lax.dot_general(vv, dob, (((1,), (1,)), ((), ())), preferred_element_type=F32)
            dst = (pt * (dpt - dl_ref[g, 0:1, :])).astype(BF16)
            dk_acc += jnp.dot(dst, qv, preferred_element_type=F32)
            dq_part = lax.dot_general(dst, kk, (((0,), (0,)), ((), ())), preferred_element_type=F32)

            @pl.when(j == 0)
            def _():
                dq_ref[g] = dq_part

            @pl.when(j > 0)
            def _():
                dq_ref[g] += dq_part

        rows = pl.ds(pl.multiple_of(j * tk, tk), tk)

        @pl.when(i == 0)
        def _():
            dk_ref[0, rows, :] = dk_acc
            dv_ref[0, rows, :] = dv_acc

        @pl.when(i > 0)
        def _():
            dk_ref[0, rows, :] += dk_acc
            dv_ref[0, rows, :] += dv_acc

    ks = pl.BlockSpec((1, tk, HD), lambda g, i, j: (g, j, 0))
    qs = pl.BlockSpec((QG, tq, HD), lambda g, i, j: (g, i, 0))
    rs = pl.BlockSpec((QG, 1, tq), lambda g, i, j: (g, 0, i))
    full = pl.BlockSpec((1, tk_all, HD), lambda g, i, j: (g, 0, 0))
    return pl.pallas_call(
        body, name=name, grid=(hkv, tq_all // tq, tk_all // tk), in_specs=[qs, ks, ks, qs, qs, rs], out_specs=[qs, full, full],
        out_shape=[_sds((h, tq_all, HD), F32), _sds((hkv, tk_all, HD), F32), _sds((hkv, tk_all, HD), F32)],
        scratch_shapes=[pltpu.VMEM((QG, 8, tq), F32)],
        compiler_params=_cp(("parallel", "arbitrary", "arbitrary")),
    )(q, k, v, o, do, lse_row)


def _decay_fn(a, w2, b2):
    return _log_sigmoid(bdot(a, w2, 1, 0) + b2) / GLA_TAU


def _decay_fwd(name, p, w2, b2):
    t = p.shape[0]
    return _rows_call(name, lambda ids, a, w, b: (_decay_fn(a, w, b),), (t // TM,),
                      [(p, _rowspec(128, OFF["ab"])), (w2, _full((128, 512))), (b2, _full((1, 512)))],
                      [(_sds((t, 512), F32), _rowspec(512), False)])[0]


def _decay_bwd(name, p, w2, b2, gf, gb):
    t = p.shape[0]

    def fn(ids, a, w, b, qf, kf, vf, lf, qb, kb, vb, lb):
        _, vjp = jax.vjp(_decay_fn, a, w, b)
        return vjp(jnp.concatenate([lf, lb], axis=-1)) + (qf + qb, kf + kb, vf + vb)

    widths = (256, 256, 512, 256)
    return _rows_call(name, fn, (t // TM,),
                      [(p, _rowspec(128, OFF["ab"])), (w2, _full((128, 512))), (b2, _full((1, 512)))]
                      + [(g, _rowspec(w)) for g, w in zip(gf, widths)] + [(g, _rowspec(w)) for g, w in zip(gb, widths)],
                      [(_sds((t, 128), BF16), _rowspec(128), False), (_sds((128, 512), F32), _full((128, 512)), True),
                       (_sds((1, 512), F32), _full((1, 512)), True)]
                      + [(_sds((t, w), BF16), _rowspec(w), False) for w in widths[:3]])


def _gla_consts(reverse):
    r = lax.broadcasted_iota(jnp.int32, (GLA_CHUNK, GLA_CHUNK), 0)
    c = lax.broadcasted_iota(jnp.int32, (GLA_CHUNK, GLA_CHUNK), 1)
    trib = (r <= c) if reverse else (r >= c)
    br = lax.broadcasted_iota(jnp.int32, (GLA_QK, GLA_V), 0) // GLA_DK
    bc = lax.broadcasted_iota(jnp.int32, (GLA_QK, GLA_V), 1) // GLA_DV
    lane_head = lax.broadcasted_iota(jnp.int32, (1, GLA_QK), 1) // GLA_DK
    return trib, (br == bc).astype(F32), lane_head


def _gla_chunk(q, k, v, la, s_in, consts):
    trib, bd, lane_head = consts
    q, k = q.astype(F32), k.astype(F32)
    cum = hdot(trib.astype(F32), la)
    tot = jnp.sum(la, axis=0, keepdims=True)
    q_in = q * (GLA_DK ** -0.5) * jnp.exp(cum)
    k_in = k * jnp.exp(-cum)
    k_st = k * jnp.exp(tot - cum)
    outs = []
    for h in range(GLA_H):
        att = bdot(jnp.where(lane_head == h, q_in, 0.0), k_in, 1, 1)
        att = jnp.where(trib, att, 0.0)
        outs.append(bdot(att, v[:, GLA_DV * h:GLA_DV * (h + 1)], 1, 0))
    o = jnp.concatenate(outs, axis=-1) + bdot(q_in, s_in, 1, 0)
    decay = jnp.exp(hdot(la, jnp.ones((GLA_CHUNK, LANE), F32), 0, 0))
    s_out = jnp.concatenate([decay] * (GLA_V // LANE), axis=-1) * s_in + bdot(k_st, v, 0, 0) * bd
    return o, s_out


def _gla_order(nb, reverse, backward):
    if not reverse:
        return (lambda s: nb - 1 - s) if backward else (lambda s: s)
    if backward:
        return lambda s: jnp.where(s == nb - 1, 0, s + 1)
    return lambda s: jnp.where(s == 0, 0, nb - s)


_NCH = TM // GLA_CHUNK


def _gla_specs(nb, reverse, backward):
    order = _gla_order(nb, reverse, backward)
    col = lambda width, off: pl.BlockSpec((TM, width), lambda s, o=off // width: (order(s), o))
    state = pl.BlockSpec((_NCH, GLA_H, GLA_DK, GLA_DV), lambda s: (order(s), 0, 0, 0))
    qkvla = [col(256, OFF["glq"]), col(256, OFF["glk"]), col(512, OFF["glv"]), col(256, 256 * int(reverse))]
    return col, state, qkvla


def _gla_fwd(name, p, la):
    t = p.shape[0]
    nb = t // TM

    def body(*refs):
        ins, outs, scr = (refs[0:4], refs[4:8]), (refs[8:10], refs[10:12]), refs[12:14]

        @pl.when(pl.program_id(0) == 0)
        def _():
            for s_ref in scr:
                s_ref[...] = jnp.zeros_like(s_ref)

        for step in range(_NCH):
            for d in range(2):
                (q_ref, k_ref, v_ref, la_ref), (o_ref, sv_ref), s_ref = ins[d], outs[d], scr[d]
                c = _NCH - 1 - step if d else step
                rows = slice(GLA_CHUNK * c, GLA_CHUNK * (c + 1))
                s_in = s_ref[...]
                for h in range(GLA_H):
                    sv_ref[c, h] = s_in[GLA_DK * h:GLA_DK * (h + 1), GLA_DV * h:GLA_DV * (h + 1)]
                o, s_out = _gla_chunk(q_ref[rows, :], k_ref[rows, :], v_ref[rows, :], la_ref[rows, :], s_in,
                                      _gla_consts(bool(d)))
                o_ref[rows, :] = o
                s_ref[...] = s_out

    in_specs, out_specs, out_shape = [], [], []
    for d in range(2):
        col, state, qkvla = _gla_specs(nb, bool(d), False)
        in_specs += qkvla
        out_specs += [col(512, 0), state]
        out_shape += [_sds((t, GLA_V), F32), _sds((t // GLA_CHUNK, GLA_H, GLA_DK, GLA_DV), F32)]
    return pl.pallas_call(
        body, name=name, grid=(nb,), in_specs=in_specs, out_specs=out_specs, out_shape=out_shape,
        scratch_shapes=[pltpu.VMEM((GLA_QK, GLA_V), F32)] * 2, compiler_params=_cp(("arbitrary",)),
    )(p, p, p, la, p, p, p, la)


def _gla_bwd(name, p, la, sv_f, sv_b, do):
    t = p.shape[0]
    nb = t // TM

    def body(*refs):
        ins, outs, scr = (refs[0:6], refs[6:12]), (refs[12:16], refs[16:20]), refs[20:22]

        @pl.when(pl.program_id(0) == 0)
        def _():
            for ds_ref in scr:
                ds_ref[...] = jnp.zeros_like(ds_ref)

        zero = jnp.zeros((GLA_DK, GLA_DV), F32)
        for step in range(_NCH):
            for d in range(2):
                (q_ref, k_ref, v_ref, la_ref, sv_ref, do_ref), out_refs, ds_ref = ins[d], outs[d], scr[d]
                c = step if d else _NCH - 1 - step
                rows = slice(GLA_CHUNK * c, GLA_CHUNK * (c + 1))
                s_in = jnp.concatenate(
                    [jnp.concatenate([sv_ref[c, h] if hh == h else zero for hh in range(GLA_H)], axis=-1)
                     for h in range(GLA_H)], axis=0)
                consts = _gla_consts(bool(d))
                _, vjp = jax.vjp(lambda a, b, cc, dd, e: _gla_chunk(a, b, cc, dd, e, consts),
                                 q_ref[rows, :], k_ref[rows, :], v_ref[rows, :], la_ref[rows, :], s_in)
                grads = vjp((do_ref[rows, :], ds_ref[...]))
                for o_ref, g in zip(out_refs, grads[:4]):
                    o_ref[rows, :] = g.astype(o_ref.dtype)
                ds_ref[...] = grads[4]

    ins, in_specs, out_specs, out_shape = [], [], [], []
    for d, sv in enumerate((sv_f, sv_b)):
        col, state, qkvla = _gla_specs(nb, bool(d), True)
        ins += [p, p, p, la, sv, do]
        in_specs += qkvla + [state, col(512, 0)]
        out_specs += [col(256, 0), col(256, 0), col(512, 0), col(256, 0)]
        out_shape += [_sds((t, GLA_QK), F32), _sds((t, GLA_QK), F32), _sds((t, GLA_V), F32), _sds((t, GLA_QK), F32)]
    return pl.pallas_call(
        body, name=name, grid=(nb,), in_specs=in_specs, out_specs=out_specs, out_shape=out_shape,
        scratch_shapes=[pltpu.VMEM((GLA_QK, GLA_V), F32)] * 2, compiler_params=_cp(("arbitrary",)),
    )(*ins)


def _gla_out_fn(of, ob, r, g):
    o = of + ob
    cols = [_rms(o[:, GLA_DV * h:GLA_DV * (h + 1)], g[:, GLA_DV * h:GLA_DV * (h + 1)]) for h in range(GLA_H)]
    return jnp.concatenate(cols, axis=-1) * jax.nn.silu(r.astype(F32))


def _gla_out_fwd(name, of, ob, p, g):
    t = p.shape[0]
    return _rows_call(name, lambda ids, *a: (_gla_out_fn(*a),), (t // TM,),
                      [(of, _rowspec(512)), (ob, _rowspec(512)), (p, _rowspec(512, OFF["gr"])), (g, _full((1, 512)))],
                      [(_sds((t, 512), BF16), _rowspec(512), False)])[0]


def _gla_out_bwd(name, of, ob, p, g, dgla):
    t = p.shape[0]

    def fn(ids, a, b, r, gv, dv):
        _, vjp = jax.vjp(_gla_out_fn, a, b, r, gv)
        do, _, dr, dg = vjp(dv)
        return do, dr, dg

    return _rows_call(name, fn, (t // TM,),
                      [(of, _rowspec(512)), (ob, _rowspec(512)), (p, _rowspec(512, OFF["gr"])), (g, _full((1, 512))),
                       (dgla, _rowspec(512))],
                      [(_sds((t, 512), F32), _rowspec(512), False), (_sds((t, 512), BF16), _rowspec(512), False),
                       (_sds((1, 512), F32), _full((1, 512)), True)])


_TMM = 384


def _merge_fwd(name, gm, att, gla, wa, wb, wc, p):
    t = p.shape[0]
    row = lambda w, off=0: pl.BlockSpec((_TMM, w), lambda i, o=off // w: (i, o))

    def fn(ids, a, b, c, wa_, wb_, wc_, ga, gb, gc):
        ga, gb, gc = ga.astype(F32), gb.astype(F32), gc.astype(F32)
        return (jax.nn.sigmoid(ga) * bdot(a, wa_, 1, 0) + jax.nn.sigmoid(gb) * bdot(b, wb_, 1, 0)
                + jax.nn.sigmoid(gc) * bdot(c, wc_, 1, 0),)

    return _rows_call(name, fn, (t // _TMM,),
                      [(gm, row(512)), (att, row(512)), (gla, row(512)), (wa, _full((512, D))), (wb, _full((512, D))),
                       (wc, _full((512, D))), (p, row(D, OFF["gA"])), (p, row(D, OFF["gB"])), (p, row(D, OFF["gC"]))],
                      [(_sds((t, D), BF16), row(D), False)])[0]


def _merge_bwd(name, gm, att, gla, wa, wb, wc, p, dmerged):
    t = p.shape[0]
    row = lambda w, off=0: pl.BlockSpec((_TMM, w), lambda i, o=off // w: (i, o))

    def fn(ids, a, b, c, wa_, wb_, wc_, ga, gb, gc, dm):
        ga, gb, gc = ga.astype(F32), gb.astype(F32), gc.astype(F32)
        outs_y, outs_g = [], []
        for br, w, g in ((a, wa_, ga), (b, wb_, gb), (c, wc_, gc)):
            s = jax.nn.sigmoid(g)
            outs_y.append(dm * s)
            outs_g.append(dm * bdot(br, w, 1, 0) * s * (1.0 - s))
        return tuple(outs_y) + tuple(outs_g)

    o = (_sds((t, D), BF16), row(D), False)
    return _rows_call(name, fn, (t // _TMM,),
                      [(gm, row(512)), (att, row(512)), (gla, row(512)), (wa, _full((512, D))), (wb, _full((512, D))),
                       (wc, _full((512, D))), (p, row(D, OFF["gA"])), (p, row(D, OFF["gB"])), (p, row(D, OFF["gC"])),
                       (dmerged, row(D))], [o] * 6)


_TNC = 1408
_NJ = FFN // _TNC


HALO = 16


def _shift_rows(x, prev, nxt, vp, vn):
    n = x.shape[0]
    rid = lax.broadcasted_iota(jnp.int32, x.shape, 0)
    xp = jnp.where(rid == 0, jnp.where(vp, prev[HALO - 1:HALO, :], 0.0), pltpu.roll(x, 1, 0))
    xn = jnp.where(rid == n - 1, jnp.where(vn, nxt[0:1, :], 0.0), pltpu.roll(x, n - 1, 0))
    return xp, xn


def _seq_edges(i, t):
    start, end = i * TM, (i + 1) * TM
    return jnp.logical_and(start != 0, start != TC), jnp.logical_and(end != TC, end != t)


def _halo_specs(t, colmap):
    r = TM // HALO
    main = pl.BlockSpec((TM, _TNC), lambda j, i: (i, colmap(j)))
    prev = pl.BlockSpec((HALO, _TNC), lambda j, i: (jnp.maximum(i * r - 1, 0), colmap(j)))
    nxt = pl.BlockSpec((HALO, _TNC), lambda j, i: (jnp.minimum((i + 1) * r, t // HALO - 1), colmap(j)))
    return [main, prev, nxt]


def _conv3(x, xp, xn, w, b=None):
    y = xp * w[0:1, :] + x * w[1:2, :] + xn * w[2:3, :]
    return y if b is None else b + y


def _conv_fwd(name, a, cw, cb):
    t = a.shape[0]

    def fn(ids, ag, agp, agn, av, avp, avn, wg, wv, bg, bv):
        vp, vn = _seq_edges(ids[1], t)
        ag, agp, agn, av, avp, avn = (z.astype(F32) for z in (ag, agp, agn, av, avp, avn))
        cg = _conv3(ag, *_shift_rows(ag, agp, agn, vp, vn), wg, bg)
        cv = _conv3(av, *_shift_rows(av, avp, avn, vp, vn), wv, bv)
        return (jax.nn.silu(cg) * cv,)

    gcol, vcol = (lambda j: j), (lambda j: j + _NJ)
    wspec = lambda cm: pl.BlockSpec((3, _TNC), lambda j, i: (0, cm(j)))
    bspec = lambda cm: pl.BlockSpec((1, _TNC), lambda j, i: (0, cm(j)))
    ins = [(a, s) for s in _halo_specs(t, gcol) + _halo_specs(t, vcol)]
    ins += [(cw, wspec(gcol)), (cw, wspec(vcol)), (cb, bspec(gcol)), (cb, bspec(vcol))]
    return _rows_call(name, fn, (_NJ, t // TM), ins,
                      [(_sds((t, FFN), BF16), pl.BlockSpec((TM, _TNC), lambda j, i: (i, j)), False)])[0]


def _conv_bwd(name, a, cw, cb, dact):
    t = a.shape[0]
    n = TM + 2 * HALO

    def fn(ids, ag, agp, agn, av, avp, avn, dv, dvp, dvn, wg, wv, bg, bv):
        vp, vn = _seq_edges(ids[1], t)
        ag, agp, agn, av, avp, avn = (z.astype(F32) for z in (ag, agp, agn, av, avp, avn))
        ext = lambda x, xp, xn: jnp.concatenate([jnp.where(vp, xp, 0.0), x, jnp.where(vn, xn, 0.0)], axis=0)
        up, dn = (lambda x: pltpu.roll(x, 1, 0)), (lambda x: pltpu.roll(x, n - 1, 0))
        main = lambda y: y[HALO:HALO + TM]
        eg, ev, ed = ext(ag, agp, agn), ext(av, avp, avn), ext(dv, dvp, dvn)
        cg = _conv3(eg, up(eg), dn(eg), wg, bg)
        cv = _conv3(ev, up(ev), dn(ev), wv, bv)
        s = jax.nn.sigmoid(cg)
        rid = lax.broadcasted_iota(jnp.int32, (3, eg.shape[1]), 0)
        das, dws, dbs = [], [], []
        for dc, w, x in ((ed * cv * s * (1.0 + cg * (1.0 - s)), wg, eg), (ed * cg * s, wv, ev)):
            das.append(main(dn(dc) * w[0:1, :] + dc * w[1:2, :] + up(dc) * w[2:3, :]))
            dcm = main(dc)
            sums = [jnp.sum(dcm * main(y), axis=0, keepdims=True) for y in (up(x), x, dn(x))]
            dws.append(jnp.where(rid == 0, sums[0], jnp.where(rid == 1, sums[1], sums[2])))
            dbs.append(jnp.sum(dcm, axis=0, keepdims=True))
        return jnp.stack(das), jnp.stack(dws), jnp.stack(dbs)

    gcol, vcol = (lambda j: j), (lambda j: j + _NJ)
    wspec = lambda cm: pl.BlockSpec((3, _TNC), lambda j, i: (0, cm(j)))
    bspec = lambda cm: pl.BlockSpec((1, _TNC), lambda j, i: (0, cm(j)))
    ins = [(a, s) for s in _halo_specs(t, gcol) + _halo_specs(t, vcol)] + [(dact, s) for s in _halo_specs(t, gcol)]
    ins += [(cw, wspec(gcol)), (cw, wspec(vcol)), (cb, bspec(gcol)), (cb, bspec(vcol))]
    return _rows_call(name, fn, (_NJ, t // TM), ins,
                      [(_sds((2, t, FFN), BF16), pl.BlockSpec((2, TM, _TNC), lambda j, i: (0, i, j)), False),
                       (_sds((2, 3, FFN), F32), pl.BlockSpec((2, 3, _TNC), lambda j, i: (0, 0, j)), True),
                       (_sds((2, 1, FFN), F32), pl.BlockSpec((2, 1, _TNC), lambda j, i: (0, 0, j)), True)])


_TNA = 512


def _adaln_fwd(name, cond, w, b):
    fn = lambda ids, cv, wv, bv: ((bdot(jax.nn.silu(cv), wv[0], 1, 0) + bv[0])[None],)
    return _rows_call(name, fn, (2, ADA_LOC // _TNA),
                      [(cond, _full((16, D))), (w, pl.BlockSpec((1, D, _TNA), lambda l, j: (l, 0, j))),
                       (b, pl.BlockSpec((1, 1, _TNA), lambda l, j: (l, 0, j)))],
                      [(_sds((2, 16, ADA_LOC), F32), pl.BlockSpec((1, 16, _TNA), lambda l, j: (l, 0, j)), False)])[0]


def _adaln_bwd(name, c8, cc8, w, dl, dc):
    def fn(ids, cv, ccv, wv, dlv, dcv):
        dcs = jnp.broadcast_to(jnp.sum(dcv[0], axis=0, keepdims=True), dcv[0].shape)
        dw = hdot(jax.nn.silu(cv), dlv[0], 0, 0) + hdot(jax.nn.silu(ccv), dcs, 0, 0)
        s = jax.nn.sigmoid(ccv)
        rid = lax.broadcasted_iota(jnp.int32, ccv.shape, 0)
        dcc = jnp.where(rid == 0, bdot(dcs, wv[0], 1, 1) * s * (1.0 + ccv * (1.0 - s)), 0.0)
        return dw[None], dcc

    dspec = pl.BlockSpec((1, 8, _TNA), lambda l, j: (l, 0, j))
    return _rows_call(name, fn, (2, ADA_LOC // _TNA),
                      [(c8, _full((8, D))), (cc8, _full((8, D))), (w, pl.BlockSpec((1, D, _TNA), lambda l, j: (l, 0, j))),
                       (dl, dspec), (dc, dspec)],
                      [(_sds((2, D, ADA_LOC), F32), pl.BlockSpec((1, D, _TNA), lambda l, j: (l, 0, j)), False),
                       (_sds((8, D), F32), _full((8, D)), True)], acc_axes=(0, 1))


def _adamw_fn(w, g, m, v):
    m = ADAM_B1 * m + (1.0 - ADAM_B1) * g
    v = ADAM_B2 * v + (1.0 - ADAM_B2) * (g * g)
    m_hat = m / (1.0 - ADAM_B1 ** ADAM_STEP)
    v_hat = v / (1.0 - ADAM_B2 ** ADAM_STEP)
    return -ADAM_LR * (m_hat / (jnp.sqrt(v_hat) + ADAM_EPS) + ADAM_WD * w), m, v


def _adamw(name, w, g, m, v):
    l, r, c = w.shape
    tr = _tile(r, max(8, (1 << 20) // (4 * c)), 8)
    spec = pl.BlockSpec((None, tr, c), lambda i, j: (i, j, 0))
    o = (_sds((l, r, c), F32), spec, False)
    return _rows_call(name, lambda ids, *a: _adamw_fn(*a), (l, r // tr), [(x, spec) for x in (w, g, m, v)], [o, o, o],
                      sem=("parallel", "parallel"))


def _coords():
    return lax.axis_index("x"), lax.axis_index("y"), lax.axis_index("c")


def _other_chips(x, y):
    return [(1 - x, y), (x, 1 - y), (1 - x, 1 - y)]


def _allgather_small(name, blk):
    m_per, n = blk.shape

    def body(x_ref, out_ref, send_sems, recv_sems, local_sem):
        x, y, c = _coords()
        me, sibling = (x, y, c), (x, y, 1 - c)
        chips = _other_chips(x, y)

        def rows(px, py, pc):
            return out_ref.at[pl.ds((4 * px + 2 * py + pc) * m_per, m_per), :]

        def copy(k, block, to, src=None):
            return pltpu.make_async_remote_copy(
                src_ref=rows(*block) if src is None else src, dst_ref=rows(*block), send_sem=send_sems.at[k],
                recv_sem=recv_sems.at[k], device_id=to, device_id_type=MESH)

        mine = pltpu.make_async_copy(x_ref, rows(*me), local_sem)
        mine.start()
        first = [copy(0, me, sibling, src=x_ref)]
        first += [copy(1 + j, me, (*chip, c), src=x_ref) for j, chip in enumerate(chips)]
        for cp in first:
            cp.start()
        passed = [copy(4 + j, (*chip, c), sibling) for j, chip in enumerate(chips)]
        for j, chip in enumerate(chips):
            copy(1 + j, (*chip, c), me).wait_recv()
            passed[j].start()
        copy(0, sibling, me).wait_recv()
        for j, chip in enumerate(chips):
            copy(4 + j, (*chip, 1 - c), me).wait_recv()
        for cp in first + passed:
            cp.wait_send()
        mine.wait()

    return pl.pallas_call(
        body, name=name, out_shape=_sds((N_DEV * m_per, n), blk.dtype),
        in_specs=[pl.BlockSpec(memory_space=pltpu.VMEM)], out_specs=pl.BlockSpec(memory_space=pltpu.VMEM),
        scratch_shapes=[pltpu.SemaphoreType.DMA((7,)), pltpu.SemaphoreType.DMA((7,)), pltpu.SemaphoreType.DMA],
        compiler_params=pltpu.CompilerParams(vmem_limit_bytes=VMEM_LIMIT),
    )(blk)


_ANY = pl.BlockSpec(memory_space=pl.ANY)


def _remote(src, dst, send_sems, recv_sems, s, to):
    return pltpu.make_async_remote_copy(src_ref=src, dst_ref=dst, send_sem=send_sems.at[s], recv_sem=recv_sems.at[s],
                                        device_id=to, device_id_type=MESH)


def _comm_call(name, body, ins, out_shapes, n_sems, n_local):
    return pl.pallas_call(
        body, name=name, out_shape=out_shapes, in_specs=[_ANY] * len(ins), out_specs=[_ANY] * len(out_shapes),
        scratch_shapes=[pltpu.SemaphoreType.DMA((n_sems,)), pltpu.SemaphoreType.DMA((n_sems,)),
                        pltpu.SemaphoreType.DMA((n_local,))],
    )(*ins)


def _allgather_layers(name, locs):
    n = len(locs)

    def body(*refs):
        ins, outs, (send_sems, recv_sems, local_sems) = refs[:n], refs[n:2 * n], refs[2 * n:]
        x, y, c = _coords()
        k = 2 * x + y
        sibling = (x, y, 1 - c)
        chips = _other_chips(x, y)
        first = [_remote(ins[t].at[c], outs[t].at[k, c], send_sems, recv_sems, 6 * t + j, (*chip, c))
                 for t in range(n) for j, chip in enumerate(chips)]
        for cp in first:
            cp.start()
        passed = []
        for t in range(n):
            for j, (cx, cy) in enumerate(chips):
                there = outs[t].at[2 * cx + cy, c]
                _remote(there, there, send_sems, recv_sems, 6 * t + j, sibling).wait_recv()
                passed.append(_remote(there, there, send_sems, recv_sems, 6 * t + 3 + j, sibling))
                passed[-1].start()
        for t in range(n):
            for j, (cx, cy) in enumerate(chips):
                there = outs[t].at[2 * cx + cy, 1 - c]
                _remote(there, there, send_sems, recv_sems, 6 * t + 3 + j, sibling).wait_recv()
        for cp in first + passed:
            cp.wait_send()

    outs = _comm_call(name, body, locs, [_sds((N_CHIP,) + a.shape, a.dtype) for a in locs], 6 * n, 1)
    k = 2 * lax.axis_index("x") + lax.axis_index("y")
    return [lax.dynamic_update_slice_in_dim(o, a[None], k, axis=0) for o, a in zip(outs, locs)]


def _rs_pair_exchange(name, g0, g1):
    n = len(g0)

    def body(*refs):
        a0, a1, outs, (send_sems, recv_sems, _) = refs[:n], refs[n:2 * n], refs[2 * n:3 * n], refs[3 * n:]
        x, y, c = _coords()

        def run(srcs):
            cps = [_remote(srcs[t], outs[t], send_sems, recv_sems, t, (x, y, 1 - c)) for t in range(n)]
            for cp in cps:
                cp.start()
            for cp in cps:
                cp.wait()

        pl.when(c == 0)(lambda: run(a1))
        pl.when(c == 1)(lambda: run(a0))

    return _comm_call(name, body, list(g0) + list(g1), [_sds(a.shape, a.dtype) for a in g0], n, 1)


def _ew2d(name, fn, ins, out_dtype):
    shape = ins[0].shape
    r, c = _prod(shape[:-1]), shape[-1]
    tr = _tile(r, max(8, (1 << 20) // (4 * c)), 8)
    spec = pl.BlockSpec((tr, c), lambda i: (i, 0))
    out = _rows_call(name, lambda ids, *a: (fn(*a),), (r // tr,), [(a.reshape(r, c), spec) for a in ins],
                     [(_sds((r, c), out_dtype), spec, False)], sem=("parallel",))[0]
    return out.reshape(shape)


def _rs_chip_exchange(name, s1):
    n = len(s1)

    def body(*refs):
        ins, outs, (send_sems, recv_sems, local_sems) = refs[:n], refs[n:2 * n], refs[2 * n:]
        x, y, c = _coords()
        k = 2 * x + y
        chips = _other_chips(x, y)
        cps = [_remote(ins[t].at[2 * cx + cy], outs[t].at[k], send_sems, recv_sems, 3 * t + j, (cx, cy, c))
               for t in range(n) for j, (cx, cy) in enumerate(chips)]
        for cp in cps:
            cp.start()
        for t in range(n):
            for j, (cx, cy) in enumerate(chips):
                there = outs[t].at[2 * cx + cy]
                _remote(there, there, send_sems, recv_sems, 3 * t + j, (cx, cy, c)).wait_recv()
        for cp in cps:
            cp.wait_send()

    outs = _comm_call(name, body, s1, [_sds(a.shape, a.dtype) for a in s1], 3 * n, 1)
    k = 2 * lax.axis_index("x") + lax.axis_index("y")
    own = [lax.dynamic_index_in_dim(a, k, axis=0, keepdims=True) for a in s1]
    return [lax.dynamic_update_slice_in_dim(o, a, k, axis=0) for o, a in zip(outs, own)]


def _sum_slots(name, a):
    s, r, cdim = a.shape
    tr = _tile(r, 512, 8)

    def fn(ids, av):
        tot = av[0]
        for i in range(1, s):
            tot = tot + av[i]
        return (tot,)

    return _rows_call(name, fn, (r // tr,), [(a, pl.BlockSpec((s, tr, cdim), lambda i: (0, i, 0)))],
                      [(_sds((r, cdim), F32), pl.BlockSpec((tr, cdim), lambda i: (i, 0)), False)], sem=("parallel",))[0]


def _pair_allgather(name, halves):
    n = len(halves)

    def body(*refs):
        ins, outs, (send_sems, recv_sems, local_sems) = refs[:n], refs[n:2 * n], refs[2 * n:]
        x, y, c = _coords()
        cps = [_remote(ins[t], outs[t].at[c], send_sems, recv_sems, t, (x, y, 1 - c)) for t in range(n)]
        for cp in cps:
            cp.start()
        for t in range(n):
            _remote(ins[t], outs[t].at[1 - c], send_sems, recv_sems, t, (x, y, 1 - c)).wait_recv()
        for cp in cps:
            cp.wait_send()

    outs = _comm_call(name, body, halves, [_sds((2,) + a.shape, a.dtype) for a in halves], n, 1)
    return [lax.dynamic_update_slice_in_dim(o, a[None], lax.axis_index("c"), axis=0) for o, a in zip(outs, halves)]


def _reduce_scatter(g0, g1):
    n = len(g0)
    got = _rs_pair_exchange("rs_pair_exchange", g0, g1)
    keep = lambda a, b, r: jnp.where(lax.axis_index("c") == 0, a, b) + r
    s1 = [_ew2d("rs_pair_add_%d" % t, keep, [g0[t], g1[t], got[t]], BF16) for t in range(n)]
    slots = _rs_chip_exchange("rs_chip_exchange", s1)
    red = [_sum_slots("rs_chip_sum_%d" % t, a.reshape(N_CHIP, -1, a.shape[-1])).reshape(a.shape[1:])
           for t, a in enumerate(slots)]
    return _pair_allgather("rs_pair_allgather", red)


PACK_C = 1024
_SHARDED = (("w_in", 1), ("w_br_a", 1), ("w_br_b", 1), ("w_br_c", 1), ("w_out", 0), ("w_ffn_up", 1), ("w_ffn_down", 0))
_SHARDED_SMALL = (("conv_w", (3, 2 * FFN), 1), ("w_alpha2", (2, 16, GLA_QK), 2), ("b_alpha", (2, GLA_QK), 1))


def _prod(shape):
    n = 1
    for s in shape:
        n *= s
    return n


def _to_blocks(full, axis):
    shp = full.shape
    split = full.reshape(shp[:axis] + (N_CHIP, shp[axis] // N_CHIP) + shp[axis + 1:])
    return jnp.moveaxis(split, axis, 0)


def _from_blocks(blocks, axis):
    return jnp.concatenate([blocks[k] for k in range(N_CHIP)], axis=axis)


def _rope_tables(tx):
    pos = jnp.arange(tx, dtype=jnp.int32)
    inv_freq = 10000.0 ** (-jnp.arange(16, dtype=F32) / 16)
    ang_r = (pos // GRID_W).astype(F32)[:, None] * inv_freq
    ang_c = (pos % GRID_W).astype(F32)[:, None] * inv_freq
    ang = jnp.concatenate([ang_r, ang_r, ang_c, ang_c], axis=-1)
    sign = jnp.concatenate([-jnp.ones((16,), F32), jnp.ones((16,), F32)] * 2)
    cos = jnp.concatenate([jnp.ones((TC, HD), F32), jnp.cos(ang)], axis=0)
    sin = jnp.concatenate([jnp.zeros((TC, HD), F32), jnp.sin(ang) * sign], axis=0)
    return jnp.tile(cos, (1, 2)), jnp.tile(sin, (1, 2))


def _lane_consts():
    l = jnp.arange(512)
    seg = (l[:, None] // HD == l[None, :] // HD).astype(F32) / HD
    partner = jnp.where(l % 32 < 16, l + 16, l - 16)
    perm = (l[:, None] == partner[None, :]).astype(F32)
    return seg, perm


def _heads(a, n):
    return a.reshape(a.shape[0], n, HD).transpose(1, 0, 2)


def _unheads(a):
    return a.transpose(1, 0, 2).reshape(a.shape[1], a.shape[0] * HD)


def _gather_f32_shards(shards):
    sizes = [_prod(a.shape) for a in shards]
    flat = jnp.concatenate([a.reshape(-1) for a in shards] + [jnp.zeros((16 * PACK_C - sum(sizes),), F32)])
    got = _allgather_small("gather_f32_shards", flat.reshape(16, PACK_C)).reshape(N_CHIP, 2, 16 * PACK_C)[:, 0]
    out, o = {}, 0
    for (n, _, ax), a, sz in zip(_SHARDED_SMALL, shards, sizes):
        out[n] = jnp.concatenate([got[k, o:o + sz].reshape(a.shape) for k in range(N_CHIP)], axis=ax + 1)
        o += sz
    return out


def _layer_params(l, wfull, small):
    w2 = small["w_alpha2_full"][l]
    w2pad = jnp.zeros((128, 512), F32).at[0:16, 0:256].set(w2[0]).at[16:32, 256:512].set(w2[1])
    full = {n: _from_blocks(wfull[n][:, l], ax) for n, ax in _SHARDED}
    return dict(
        w_in=_to_new_cols(full["w_in"]), wa=full["w_br_a"], wb=full["w_br_b"], wc=full["w_br_c"],
        w_out=full["w_out"], w_up=full["w_ffn_up"], w_down=full["w_ffn_down"],
        cw=small["conv_w_full"][l], cb=small["conv_b"][l][None], w2=w2pad,
        b2=small["b_alpha_full"][l].reshape(1, 512),
        g1=small["norm1_g"][l][None], g2=small["norm2_g"][l][None], gq=jnp.tile(small["q_norm_g"][l], 8)[None],
        gk=jnp.tile(small["k_norm_g"][l], 2)[None], ggm=small["gmlp_norm_g"][l][None], ws=small["w_spatial"][l],
        bst=small["b_spatial"][l].T, ggl=small["gla_norm_g"][l][None])


def _layer_fwd(l, last, x, h1, mod, P, tabs):
    cos, sin, seg, perm = tabs
    n = "l%d_" % l
    s = dict(x=x, h1=h1)
    p = _mm(n + "in_proj", h1, P["w_in"], "nn", BF16, tm_t=768, tn_t=2176, j_outer=True)
    s["p"] = p
    s["gm"] = _gmlp_fwd(n + "gmlp", p, P["ggm"], P["ws"], P["bst"])
    qr, kr, vb = _qk_fwd(n + "qk_prep", p, P["gq"], P["gk"], cos, sin, seg, perm)
    qx, qc, kh, vh = _heads(qr[TC:], NQ), _heads(qr[:TC], NQ), _heads(kr, NKV), _heads(vb, NKV)
    s["qx"], s["qc"], s["kh"], s["vh"] = qx, qc, kh, vh
    one_hot = (jnp.arange(HD) == 0).astype(BF16)
    v1 = jnp.concatenate([vh, jnp.broadcast_to(one_hot, vh.shape)], axis=-1)
    ox, lse_x = _attn_fwd(n + "attn_x", qx, kh, v1)
    s["ox"], s["lse_x"] = ox, lse_x
    if last:
        oc = jnp.zeros((NQ, TC, HD), F32)
    else:
        oc, lse_c = _attn_fwd(n + "attn_c", qc, kh[:, :TC], v1[:, :TC])
        s["oc"], s["lse_c"] = oc, lse_c
    s["att"] = jnp.concatenate([_unheads(oc), _unheads(ox)], axis=0).astype(BF16)
    la = _decay_fwd(n + "gla_decay", p, P["w2"], P["b2"])
    s["la"] = la
    s["of"], s["sf"], s["ob"], s["sb"] = _gla_fwd(n + "gla_scan", p, la)
    s["gla"] = _gla_out_fwd(n + "gla_out", s["of"], s["ob"], p, P["ggl"])
    s["merged"] = _merge_fwd(n + "merge", s["gm"], s["att"], s["gla"], P["wa"], P["wb"], P["wc"], p)
    s["mix"] = _mm(n + "out_proj", s["merged"], P["w_out"], "nn", F32)
    s["x_mid"], s["h2"] = _res_nm_fwd(n + "res1_norm2", x, s["mix"], mod, 2, mod, P["g2"], 3, 4)
    s["a"] = _mm(n + "ffn_up", s["h2"], P["w_up"], "nn", BF16, j_outer=True)
    s["act"] = _conv_fwd(n + "conv_gate", s["a"], P["cw"], P["cb"])
    s["f"] = _mm(n + "ffn_down", s["act"], P["w_down"], "nn", F32)
    return s


def _layer_bwd(l, last, s, mod, P, tabs, dx_mid, df, gw):
    cos, sin, seg, perm = tabs
    n = "l%d_b_" % l
    t = dx_mid.shape[0]
    p = s["p"]
    gw["w_ffn_down"] = _mm(n + "ffn_down_w", s["act"], df, "tn", F32, tm_t=1408)
    dact = _mm(n + "ffn_down_x", df, P["w_down"], "nt", F32)
    da, dcw, dcb = _conv_bwd(n + "conv_gate", s["a"], P["cw"], P["cb"], dact)
    gw["conv_w"], gw["conv_b"] = dcw.transpose(1, 0, 2).reshape(3, 2 * FFN), dcb.reshape(2 * FFN)
    gw["w_ffn_up"] = _mm(n + "ffn_up_w", s["h2"], da, "tn", F32, chip_blocks=True)
    dh2 = _mm(n + "ffn_up_x", da, P["w_up"], "nt", F32)
    dx, dmix, dmod_a, dmod_b, dg2 = _res_nm_bwd(n + "res1_norm2", s["x"], s["mix"], mod, 2, mod, P["g2"], 3, 4, dx_mid, dh2)
    dmod = dmod_a + dmod_b
    gw["norm2_g"] = dg2[0]
    gw["w_out"] = _mm(n + "out_proj_w", s["merged"], dmix, "tn", F32)
    dmerged = _mm(n + "out_proj_x", dmix, P["w_out"], "nt", F32)
    dya, dyb, dyc, dga, dgb, dgc = _merge_bwd(n + "merge", s["gm"], s["att"], s["gla"], P["wa"], P["wb"], P["wc"], p, dmerged)
    gw["w_br_a"] = _mm(n + "br_a_w", s["gm"], dya, "tn", F32)
    gw["w_br_b"] = _mm(n + "br_b_w", s["att"], dyb, "tn", F32)
    gw["w_br_c"] = _mm(n + "br_c_w", s["gla"], dyc, "tn", F32)
    dgm = _mm(n + "br_a_x", dya, P["wa"], "nt", F32)
    datt = _mm(n + "br_b_x", dyb, P["wb"], "nt", F32)
    dgla = _mm(n + "br_c_x", dyc, P["wc"], "nt", F32)
    du, dv_g, dggm, dws, dbst = _gmlp_bwd(n + "gmlp", p, P["ggm"], P["ws"], P["bst"], dgm)
    gw["gmlp_norm_g"], gw["w_spatial"], gw["b_spatial"] = dggm[0], dws, dbst.T
    kh, vh = s["kh"], s["vh"]
    row = lambda a: a.reshape(a.shape[0], 1, a.shape[1])
    dqx, dkh, dvh = _attn_bwd(n + "attn_x", s["qx"], kh, vh, s["ox"], _heads(datt[TC:], NQ), row(s["lse_x"]))
    if last:
        dqc = jnp.zeros((NQ, TC, HD), F32)
    else:
        dqc, dkc, dvc = _attn_bwd(n + "attn_c", s["qc"], kh[:, :TC], vh[:, :TC], s["oc"], _heads(datt[:TC], NQ),
                                  row(s["lse_c"]))
        pad = jnp.zeros((NKV, t - TC, HD), F32)
        dkh = dkh + jnp.concatenate([dkc, pad], axis=1)
        dvh = dvh + jnp.concatenate([dvc, pad], axis=1)
    dqr = jnp.concatenate([_unheads(dqc), _unheads(dqx)], axis=0)
    dq, dk, dgq, dgk = _qk_bwd(n + "qk_prep", p, P["gq"], P["gk"], cos, sin, seg, perm, dqr, _unheads(dkh))
    gw["q_norm_g"], gw["k_norm_g"] = dgq.reshape(8, HD).sum(0), dgk.reshape(2, HD).sum(0)
    dv_att = _unheads(dvh).astype(BF16)
    do, dr, dggl = _gla_out_bwd(n + "gla_out", s["of"], s["ob"], p, P["ggl"], dgla)
    gw["gla_norm_g"] = dggl[0]
    scans = _gla_bwd(n + "gla_scan", p, s["la"], s["sf"], s["sb"], do)
    dab, dw2, db2, dglq, dglk, dglv = _decay_bwd(n + "gla_decay", p, P["w2"], P["b2"], scans[:4], scans[4:])
    gw["w_alpha2"] = jnp.stack([dw2[0:16, 0:256], dw2[16:32, 256:512]])
    gw["b_alpha"] = db2.reshape(2, 256)
    dp = jnp.concatenate([dga, dgb, dgc, du, dv_g, dq, dglv, dr, dglq, dglk, dk, dv_att, dab], axis=-1)
    gw["w_in"] = _to_ref_cols(_mm(n + "in_proj_w", s["h1"], dp, "tn", F32, tn_t=2176, tk_t=768))
    dh1 = _mm(n + "in_proj_x", dp, P["w_in"], "nt", F32, tk_t=2176)
    return dx, dh1, dmod


_SMALL = (("norm1_g", (2, D)), ("norm2_g", (2, D)), ("q_norm_g", (2, HD)), ("k_norm_g", (2, HD)), ("gmlp_norm_g", (2, GW)),
          ("gla_norm_g", (2, GLA_V)), ("w_spatial", (2, 4, 128, 128)), ("b_spatial", (2, 4, 128)), ("conv_b", (2, 2 * FFN)),
          ("final_norm_g", (D,))) + tuple((n, (2,) + s) for n, s, _ in _SHARDED_SMALL)
_SMALL_N = 2 * 2 * ADA_W + sum(_prod(s) for _, s in _SMALL)
_SMALL_R = -(-_SMALL_N // (PACK_C * 8)) * 8


def _mod_tables(c, c_ctx, w_ada, b_ada, k):
    x, y, cc = _coords()
    me = 4 * x + 2 * y + cc
    c_all = _allgather_small("gather_c", jnp.concatenate([c, jnp.zeros((7, D), F32)], axis=0))
    c8 = c_all.reshape(N_DEV, 8, D)[:, 0]
    cond = jnp.concatenate([c8, c_ctx[None], jnp.zeros((7, D), F32)], axis=0)
    b_loc = lax.dynamic_slice_in_dim(b_ada, k * ADA_LOC, ADA_LOC, axis=1)[:, None, :]
    m_loc = _adaln_fwd("adaln", cond, w_ada, b_loc)
    m_all = _allgather_small("gather_mod", m_loc.reshape(32, ADA_LOC)).reshape(N_CHIP, 2, 2, 16, ADA_LOC)[:, 0]
    m_all = m_all.transpose(1, 2, 0, 3).reshape(2, 16, ADA_W)
    rows = jnp.stack([m_all[:, 8], lax.dynamic_index_in_dim(m_all, me, axis=1, keepdims=False)], axis=1)
    return rows.reshape(2, 2, 6, D), c8


def _step(x, c, ctx, c_ctx, W, tgt):
    xc, yc, cc = _coords()
    k = 2 * xc + yc
    tx = x.shape[0]
    t = TC + tx
    small = {n: W[n] for n, _ in _SMALL}
    for n, a in _gather_f32_shards([W[n] for n, _, _ in _SHARDED_SMALL]).items():
        small[n + "_full"] = a

    gathered = _allgather_layers("gather_weights", [W[n].astype(BF16) for n, _ in _SHARDED])
    wfull = {n: a for (n, _), a in zip(_SHARDED, gathered)}
    mods, c8 = _mod_tables(c, c_ctx, W["w_ada"], W["b_ada"], k)
    tabs = _rope_tables(tx) + _lane_consts()
    params = [_layer_params(l, wfull, small) for l in range(2)]

    xs = jnp.concatenate([ctx, x], axis=0)
    h1 = _nm_fwd("l0_norm1", xs, mods[0], params[0]["g1"], 0, 1)
    s0 = _layer_fwd(0, False, xs, h1, mods[0], params[0], tabs)
    x1, h1b = _res_nm_fwd("l0_res2_norm1", s0["x_mid"], s0["f"], mods[0], 5, mods[1], params[1]["g1"], 0, 1)
    s1 = _layer_fwd(1, True, x1, h1b, mods[1], params[1], tabs)
    loss, dxm_l, df_l, dmod_head, dgf = _head("head", s1["x_mid"], s1["f"], mods[1], W["final_norm_g"][None], tgt)

    gws = [dict(), dict()]
    dx1, dh1b, dmod1 = _layer_bwd(1, True, s1, mods[1], params[1], tabs, dxm_l, df_l, gws[1])
    dxm0, df0, dmod0_g, dmod1_s, dg1b = _res_nm_bwd("l0_b_res2_norm1", s0["x_mid"], s0["f"], mods[0], 5, mods[1],
                                                    params[1]["g1"], 0, 1, dx1, dh1b)
    gws[1]["norm1_g"] = dg1b[0]
    dx0, dh1, dmod0 = _layer_bwd(0, False, s0, mods[0], params[0], tabs, dxm0, df0, gws[0])
    grad_x, dmod0_s, dg1 = _nm_bwd("l0_b_norm1", xs, mods[0], params[0]["g1"], 0, 1, dx0, dh1)
    gws[0]["norm1_g"] = dg1[0]
    dmods = jnp.stack([dmod0 + dmod0_g + dmod0_s, dmod1 + dmod1_s + dmod_head])

    stk = {n: jnp.stack([gws[0][n], gws[1][n]]) for n, _ in _SMALL if n != "final_norm_g"}
    stk["final_norm_g"] = dgf[0]
    flat = jnp.concatenate([dmods.reshape(-1)] + [stk[n].reshape(-1) for n, _ in _SMALL])
    flat = jnp.concatenate([flat, jnp.zeros((_SMALL_R * PACK_C - _SMALL_N,), F32)]).reshape(_SMALL_R, PACK_C)
    every = _allgather_small("gather_small_grads", flat).reshape(N_DEV, _SMALL_R, PACK_C)
    tot = _sum_slots("sum_small_grads", every).reshape(-1)
    grads, o = {}, 2 * 2 * ADA_W
    for n, shp in _SMALL:
        grads[n] = tot[o:o + _prod(shp)].reshape(shp)
        o += _prod(shp)
    grads["b_ada"] = tot[:2 * 2 * ADA_W].reshape(2, 2, ADA_W).sum(axis=1)

    dm_every = every[:, :2 * 2 * ADA_W // PACK_C].reshape(N_DEV, 2, 2, ADA_W)
    dm_loc = lax.dynamic_slice_in_dim(dm_every, k * ADA_LOC, ADA_LOC, axis=3).transpose(1, 2, 0, 3)
    cc8 = jnp.concatenate([c_ctx[None], jnp.zeros((7, D), F32)], axis=0)
    grads["w_ada"], dcc = _adaln_bwd("adaln_b", c8, cc8, W["w_ada"], dm_loc[:, 1], dm_loc[:, 0])
    dcc_every = _allgather_small("gather_dcctx", dcc * 0.5).reshape(N_DEV, 8, D)
    grads["c_ctx"] = _sum_slots("sum_dcctx", dcc_every)[0]

    for n, shp, ax in _SHARDED_SMALL:
        grads[n] = lax.dynamic_slice_in_dim(grads[n], k * (shp[ax] // N_CHIP), shp[ax] // N_CHIP, axis=ax + 1)
    blocks = lambda g, n, ax: g if n == "w_ffn_up" else _to_blocks(g, ax)
    red = _reduce_scatter(*[[blocks(gws[l][n], n, ax) for n, ax in _SHARDED] for l in range(2)])
    grads.update({n: a for (n, _), a in zip(_SHARDED, red)})
    return loss[0, 0], grad_x, grads


_WEIGHTS = ("c_ctx", "w_ada", "b_ada", "norm1_g", "norm2_g", "w_in", "q_norm_g", "k_norm_g", "gmlp_norm_g", "w_spatial",
            "b_spatial", "w_alpha2", "b_alpha", "gla_norm_g", "w_br_a", "w_br_b", "w_br_c", "w_out", "w_ffn_up", "conv_w",
            "conv_b", "w_ffn_down", "final_norm_g")
_BIG = ("w_ada", "w_in", "w_br_a", "w_br_b", "w_br_c", "w_out", "w_ffn_up", "w_ffn_down")


def _update(W, G, M, V):
    delta, new_m, new_v = {}, {}, {}
    for n in _BIG:
        delta[n], new_m[n], new_v[n] = _adamw("adamw_" + n, W[n], G[n], M[n], V[n])
    rest = [n for n in _WEIGHTS if n not in _BIG]
    tot = sum(_prod(W[n].shape) for n in rest)
    rows = -(-tot // (PACK_C * 8)) * 8

    def cat(dct):
        flat = jnp.concatenate([dct[n].reshape(-1) for n in rest] + [jnp.zeros((rows * PACK_C - tot,), F32)])
        return flat.reshape(1, rows, PACK_C)

    outs = _adamw("adamw_small", cat(W), cat(G), cat(M), cat(V))
    o = 0
    for n in rest:
        sz, shp = _prod(W[n].shape), W[n].shape
        delta[n], new_m[n], new_v[n] = (a.reshape(-1)[o:o + sz].reshape(shp) for a in outs)
        o += sz
    return delta, new_m, new_v


def kernel(x, c, ctx, c_ctx, w_ada, b_ada, norm1_g, norm2_g, w_in, q_norm_g, k_norm_g, gmlp_norm_g, w_spatial, b_spatial, w_alpha2, b_alpha, gla_norm_g, w_br_a, w_br_b, w_br_c, w_out, w_ffn_up, conv_w, conv_b, w_ffn_down, final_norm_g, loss_target, m_c_ctx, m_w_ada, m_b_ada, m_norm1_g, m_norm2_g, m_w_in, m_q_norm_g, m_k_norm_g, m_gmlp_norm_g, m_w_spatial, m_b_spatial, m_w_alpha2, m_b_alpha, m_gla_norm_g, m_w_br_a, m_w_br_b, m_w_br_c, m_w_out, m_w_ffn_up, m_conv_w, m_conv_b, m_w_ffn_down, m_final_norm_g, v_c_ctx, v_w_ada, v_b_ada, v_norm1_g, v_norm2_g, v_w_in, v_q_norm_g, v_k_norm_g, v_gmlp_norm_g, v_w_spatial, v_b_spatial, v_w_alpha2, v_b_alpha, v_gla_norm_g, v_w_br_a, v_w_br_b, v_w_br_c, v_w_out, v_w_ffn_up, v_conv_w, v_conv_b, v_w_ffn_down, v_final_norm_g):
    W = dict(c_ctx=c_ctx, w_ada=w_ada, b_ada=b_ada, norm1_g=norm1_g, norm2_g=norm2_g, w_in=w_in, q_norm_g=q_norm_g,
             k_norm_g=k_norm_g, gmlp_norm_g=gmlp_norm_g, w_spatial=w_spatial, b_spatial=b_spatial, w_alpha2=w_alpha2,
             b_alpha=b_alpha, gla_norm_g=gla_norm_g, w_br_a=w_br_a, w_br_b=w_br_b, w_br_c=w_br_c, w_out=w_out,
             w_ffn_up=w_ffn_up, conv_w=conv_w, conv_b=conv_b, w_ffn_down=w_ffn_down, final_norm_g=final_norm_g)
    M = dict(c_ctx=m_c_ctx, w_ada=m_w_ada, b_ada=m_b_ada, norm1_g=m_norm1_g, norm2_g=m_norm2_g, w_in=m_w_in,
             q_norm_g=m_q_norm_g, k_norm_g=m_k_norm_g, gmlp_norm_g=m_gmlp_norm_g, w_spatial=m_w_spatial,
             b_spatial=m_b_spatial, w_alpha2=m_w_alpha2, b_alpha=m_b_alpha, gla_norm_g=m_gla_norm_g, w_br_a=m_w_br_a,
             w_br_b=m_w_br_b, w_br_c=m_w_br_c, w_out=m_w_out, w_ffn_up=m_w_ffn_up, conv_w=m_conv_w, conv_b=m_conv_b,
             w_ffn_down=m_w_ffn_down, final_norm_g=m_final_norm_g)
    V = dict(c_ctx=v_c_ctx, w_ada=v_w_ada, b_ada=v_b_ada, norm1_g=v_norm1_g, norm2_g=v_norm2_g, w_in=v_w_in,
             q_norm_g=v_q_norm_g, k_norm_g=v_k_norm_g, gmlp_norm_g=v_gmlp_norm_g, w_spatial=v_w_spatial,
             b_spatial=v_b_spatial, w_alpha2=v_w_alpha2, b_alpha=v_b_alpha, gla_norm_g=v_gla_norm_g, w_br_a=v_w_br_a,
             w_br_b=v_w_br_b, w_br_c=v_w_br_c, w_out=v_w_out, w_ffn_up=v_w_ffn_up, conv_w=v_conv_w, conv_b=v_conv_b,
             w_ffn_down=v_w_ffn_down, final_norm_g=v_final_norm_g)
    loss_local, grad_x, G = _step(x[0], c, ctx[0], c_ctx, W, loss_target[0])
    loss = lax.psum(loss_local, ("x", "y", "c"))
    delta, new_m, new_v = _update(W, G, M, V)
    return (loss, grad_x[None], *[G[n] for n in _WEIGHTS], *[delta[n] for n in _WEIGHTS],
            *[new_m[n] for n in _WEIGHTS], *[new_v[n] for n in _WEIGHTS])
```

```python
import functools

import jax
import jax.numpy as jnp
from jax import lax
from jax.experimental import pallas as pl
from jax.experimental.pallas import tpu as pltpu

F32 = jnp.float32
BF16 = jnp.bfloat16

D = 1024
TC = 256
GRID_W = 64
EPS = 1e-6
HD = 64
NQ = 8
NKV = 2
QG = NQ // NKV
GLA_H = 4
GLA_DK = 64
GLA_DV = 128
GLA_QK = 256
GLA_V = 512
GLA_CHUNK = 64
GLA_TAU = 16.0
GW = 512
FFN = 2816
IN_W = 6432
PW = 6528
ADA_W = 6 * D
N_CHIP = 4
N_DEV = 8
ADA_LOC = ADA_W // N_CHIP

ADAM_LR = 0.001
ADAM_B1 = 0.9
ADAM_B2 = 0.999
ADAM_EPS = 1e-08
ADAM_WD = 0.01
ADAM_STEP = 10

TM = 256
NCB = TC // TM
LANE = 128
VMEM_LIMIT = 48 * 1024 * 1024
MESH = pl.DeviceIdType.MESH

_COLS = (("gA", 3360, 1024), ("gB", 4384, 1024), ("gC", 5408, 1024), ("gu", 0, 512), ("gv", 512, 512),
         ("q", 1024, 512), ("glv", 2304, 512), ("gr", 2848, 512), ("glq", 1792, 256), ("glk", 2048, 256),
         ("k", 1536, 128), ("v", 1664, 128), ("ab", 2816, 32))
OFF = {}
_o = 0
for _n, _s, _w in _COLS:
    OFF[_n] = _o
    _o += max(_w, LANE)
assert _o == PW


def _to_new_cols(w):
    parts = [w[..., s:s + n] for _, s, n in _COLS]
    pad = jnp.zeros(w.shape[:-1] + (PW - IN_W,), w.dtype)
    return jnp.concatenate(parts + [pad], axis=-1)


def _to_ref_cols(w):
    by_start = sorted(_COLS, key=lambda t: t[1])
    return jnp.concatenate([w[..., OFF[n]:OFF[n] + wd] for n, _, wd in by_start], axis=-1)


def _tile(n, target, align=LANE):
    best = None
    t = align
    while t <= min(n, target):
        if n % t == 0:
            best = t
        t += align
    assert best is not None, (n, target, align)
    return best


def _cp(sem=None):
    return pltpu.CompilerParams(dimension_semantics=sem, vmem_limit_bytes=VMEM_LIMIT)


def _bdot_impl(a, b, ca, cb):
    return lax.dot_general(a.astype(BF16), b.astype(BF16), (((ca,), (cb,)), ((), ())),
                           preferred_element_type=F32)


@functools.partial(jax.custom_vjp, nondiff_argnums=(2, 3))
def bdot(a, b, ca, cb):
    return _bdot_impl(a, b, ca, cb)


def _bdot_fwd(a, b, ca, cb):
    return _bdot_impl(a, b, ca, cb), (a, b)


def _bdot_bwd(ca, cb, res, g):
    a, b = res
    da = bdot(g, b, 1, 1 - cb) if ca == 1 else bdot(b, g, 1 - cb, 1)
    db = bdot(a, g, 1 - ca, 0) if cb == 0 else bdot(g, a, 0, 1 - ca)
    return da.astype(a.dtype), db.astype(b.dtype)


bdot.defvjp(_bdot_fwd, _bdot_bwd)


def hdot(a, b, ca=1, cb=0):
    return lax.dot_general(a, b, (((ca,), (cb,)), ((), ())), precision=lax.Precision.HIGH,
                           preferred_element_type=F32)


def _rms(x, g):
    return x * lax.rsqrt(jnp.mean(x * x, axis=-1, keepdims=True) + EPS) * g


def _gelu(x):
    return 0.5 * x * (1.0 + jnp.tanh(0.7978845608028654 * (x + 0.044715 * (x * x * x))))


def _log_sigmoid(z):
    return jnp.minimum(z, 0.0) - jnp.log(1.0 + jnp.exp(-jnp.abs(z)))


def _sel(mod, is_lat, idx):
    return jnp.where(is_lat, mod[1, idx:idx + 1, :], mod[0, idx:idx + 1, :])


def _rows_call(name, fn, grid, ins, outs, acc_axes=None, sem=None):
    n_in = len(ins)
    flags = [o[2] for o in outs]
    if acc_axes is None:
        acc_axes = (len(grid) - 1,)

    def body(*refs):
        ids = tuple(pl.program_id(a) for a in range(len(grid)))
        res = fn(ids, *[r[...] for r in refs[:n_in]])
        for r, v, acc in zip(refs[n_in:], res, flags):
            if acc:
                first = functools.reduce(jnp.logical_and, [ids[a] == 0 for a in acc_axes])

                @pl.when(first)
                def _():
                    r[...] = jnp.zeros_like(r)
                r[...] += v.astype(r.dtype)
            else:
                r[...] = v.astype(r.dtype)

    return pl.pallas_call(
        body, name=name, grid=grid, in_specs=[s for _, s in ins], out_specs=[o[1] for o in outs],
        out_shape=[o[0] for o in outs],
        compiler_params=_cp(sem if sem is not None else ("arbitrary",) * len(grid)),
    )(*[a for a, _ in ins])


def _sds(shape, dtype):
    return jax.ShapeDtypeStruct(shape, dtype)


def _rowspec(width, off=0, tm=TM):
    assert off % width == 0
    return pl.BlockSpec((tm, width), lambda i, o=off // width: (i, o))


def _full(shape):
    nd = len(shape)
    return pl.BlockSpec(shape, lambda *a: (0,) * nd)


def _mm(name, a, b, mode, out_dtype, tm_t=1056, tn_t=1408, tk_t=1408, chip_blocks=False, j_outer=False):
    halves = a.ndim == 3 or b.ndim == 3
    if mode == "nn":
        (m, k), (_, n) = a.shape, b.shape
    elif mode == "nt":
        (m, k), (n, _) = a.shape[-2:], b.shape
        k *= a.ndim - 1
    else:
        (k, m), (_, n) = a.shape, b.shape[-2:]
        n *= b.ndim - 1
    tm = _tile(m, tm_t, 8 if m % LANE else LANE)
    tn = _tile(n // 2 if halves and mode == "tn" else n, tn_t)
    tk = _tile(k // 2 if halves and mode == "nt" else k, tk_t)
    nk = k // tk
    if mode == "nn":
        dims, a_spec, b_spec = ((1,), (0,)), pl.BlockSpec((tm, tk), lambda i, j, l: (i, l)), pl.BlockSpec((tk, tn), lambda i, j, l: (l, j))
    elif mode == "nt":
        dims, a_spec, b_spec = ((1,), (1,)), pl.BlockSpec((tm, tk), lambda i, j, l: (i, l)), pl.BlockSpec((tn, tk), lambda i, j, l: (j, l))
        if halves:
            a_spec = pl.BlockSpec((None, tm, tk), lambda i, j, l, h=nk // 2: (l // h, i, l % h))
    else:
        dims, a_spec, b_spec = ((0,), (0,)), pl.BlockSpec((tk, tm), lambda i, j, l: (l, i)), pl.BlockSpec((tk, tn), lambda i, j, l: (l, j))
        if halves:
            b_spec = pl.BlockSpec((None, tk, tn), lambda i, j, l, h=n // tn // 2: (j // h, l, j % h))

    def body(a_ref, b_ref, o_ref, *scratch):
        l = pl.program_id(2)
        part = lax.dot_general(a_ref[...].astype(BF16), b_ref[...].astype(BF16), (dims, ((), ())),
                               preferred_element_type=F32)
        if nk == 1:
            o_ref[...] = part.astype(o_ref.dtype)
            return
        acc_ref = scratch[0]

        @pl.when(l == 0)
        def _():
            acc_ref[...] = part

        @pl.when(l > 0)
        def _():
            acc_ref[...] += part

        @pl.when(l == nk - 1)
        def _():
            o_ref[...] = acc_ref[...].astype(o_ref.dtype)

    o_spec, o_shape = pl.BlockSpec((tm, tn), lambda i, j, l: (i, j)), _sds((m, n), out_dtype)
    if chip_blocks:
        assert tn * N_CHIP == n and tm == m
        o_spec, o_shape = pl.BlockSpec((None, tm, tn), lambda i, j, l: (j, 0, 0)), _sds((N_CHIP, m, tn), out_dtype)
    grid = (m // tm, n // tn, nk)
    if j_outer:
        swap = lambda spec: pl.BlockSpec(spec.block_shape, lambda j, i, l, f=spec.index_map: f(i, j, l))
        a_spec, b_spec, o_spec, grid = swap(a_spec), swap(b_spec), swap(o_spec), (n // tn, m // tm, nk)
    return pl.pallas_call(
        body, name=name, grid=grid, in_specs=[a_spec, b_spec], out_specs=o_spec, out_shape=o_shape,
        scratch_shapes=[pltpu.VMEM((tm, tn), F32)] if nk > 1 else [],
        compiler_params=_cp(("parallel", "parallel", "arbitrary")),
    )(a, b)


def _nm_fn(is_lat, x, mod, g, shift, scale):
    return _rms(x, g) * (1.0 + _sel(mod, is_lat, scale)) + _sel(mod, is_lat, shift)


def _res_nm_fn(is_lat, x, br, modg, gate, mods, g, shift, scale):
    xn = x + _sel(modg, is_lat, gate) * br
    return xn, _nm_fn(is_lat, xn, mods, g, shift, scale)


def _nm_fwd(name, x, mod, g, shift, scale):
    t = x.shape[0]
    fn = lambda ids, xv, mv, gv: (_nm_fn(ids[0] >= NCB, xv, mv, gv, shift, scale),)
    return _rows_call(name, fn, (t // TM,), [(x, _rowspec(D)), (mod, _full((2, 6, D))), (g, _full((1, D)))],
                      [(_sds((t, D), BF16), _rowspec(D), False)])[0]


def _nm_bwd(name, x, mod, g, shift, scale, dx_res, dh):
    t = x.shape[0]

    def fn(ids, xv, mv, gv, dxr, dhv):
        _, vjp = jax.vjp(lambda a, b, c: _nm_fn(ids[0] >= NCB, a, b, c, shift, scale), xv, mv, gv)
        dx, dm, dg = vjp(dhv)
        return dx + dxr, dm, dg

    lat = pl.BlockSpec((TM, D), lambda i: (jnp.maximum(i - NCB, 0), 0))
    return _rows_call(name, fn, (t // TM,),
                      [(x, _rowspec(D)), (mod, _full((2, 6, D))), (g, _full((1, D))), (dx_res, _rowspec(D)), (dh, _rowspec(D))],
                      [(_sds((t - TC, D), F32), lat, False), (_sds((2, 6, D), F32), _full((2, 6, D)), True),
                       (_sds((1, D), F32), _full((1, D)), True)])


def _res_nm_fwd(name, x, br, modg, gate, mods, g, shift, scale):
    t = x.shape[0]
    fn = lambda ids, xv, bv, mg, ms, gv: _res_nm_fn(ids[0] >= NCB, xv, bv, mg, gate, ms, gv, shift, scale)
    return _rows_call(name, fn, (t // TM,),
                      [(x, _rowspec(D)), (br, _rowspec(D)), (modg, _full((2, 6, D))), (mods, _full((2, 6, D))), (g, _full((1, D)))],
                      [(_sds((t, D), F32), _rowspec(D), False), (_sds((t, D), BF16), _rowspec(D), False)])


def _res_nm_bwd(name, x, br, modg, gate, mods, g, shift, scale, dx_res, dh):
    t = x.shape[0]

    def fn(ids, xv, bv, mg, ms, gv, dxr, dhv):
        f = lambda a, b, c, d, e: _res_nm_fn(ids[0] >= NCB, a, b, c, gate, d, e, shift, scale)
        _, vjp = jax.vjp(f, xv, bv, mg, ms, gv)
        return vjp((dxr, dhv))

    m26 = (_sds((2, 6, D), F32), _full((2, 6, D)), True)
    return _rows_call(name, fn, (t // TM,),
                      [(x, _rowspec(D)), (br, _rowspec(D)), (modg, _full((2, 6, D))), (mods, _full((2, 6, D))), (g, _full((1, D))),
                       (dx_res, _rowspec(D)), (dh, _rowspec(D))],
                      [(_sds((t, D), F32), _rowspec(D), False), (_sds((t, D), BF16), _rowspec(D), False), m26, m26,
                       (_sds((1, D), F32), _full((1, D)), True)])


def _head(name, x_mid, f, mod, gf, tgt):
    t = x_mid.shape[0]

    def fn(ids, xv, fv, mv, gv, tv):
        def loss_fn(a, b, c, d):
            y = _rms(a + c[1, 5:6, :] * b, d)
            e = y - tv
            return 0.5 * jnp.sum(jnp.mean(e * e, axis=-1))
        loss, grads = jax.value_and_grad(loss_fn, argnums=(0, 1, 2, 3))(xv, fv, mv, gv)
        return tuple(jnp.where(ids[0] >= NCB, v, 0.0) for v in (jnp.reshape(loss, (1, 1)),) + grads)

    return _rows_call(name, fn, (t // TM,),
                      [(x_mid, _rowspec(D)), (f, _rowspec(D)), (mod, _full((2, 6, D))), (gf, _full((1, D))),
                       (tgt, pl.BlockSpec((TM, D), lambda i: (jnp.maximum(i - NCB, 0), 0)))],
                      [(_sds((1, 1), F32), _full((1, 1)), True), (_sds((t, D), F32), _rowspec(D), False),
                       (_sds((t, D), BF16), _rowspec(D), False), (_sds((2, 6, D), F32), _full((2, 6, D)), True),
                       (_sds((1, D), F32), _full((1, D)), True)])


def _gmlp_fn(u, v, g, ws, bst):
    rows = []
    u, v = u.astype(F32), v.astype(F32)
    for r in range(u.shape[0] // 128):
        uu, vv = _gelu(u[128 * r:128 * r + 128]), _gelu(v[128 * r:128 * r + 128])
        cols = []
        for gi in range(4):
            sl = slice(128 * gi, 128 * gi + 128)
            f = bdot(ws[gi], _rms(vv[:, sl], g[:, sl]), 1, 0) + bst[:, gi:gi + 1]
            cols.append(uu[:, sl] * f)
        rows.append(jnp.concatenate(cols, axis=-1))
    return jnp.concatenate(rows, axis=0)


def _gmlp_ins(p, g, ws, bst):
    return [(p, _rowspec(GW, OFF["gu"])), (p, _rowspec(GW, OFF["gv"])), (g, _full((1, GW))),
            (ws, _full((4, 128, 128))), (bst, _full((128, 4)))]


def _gmlp_fwd(name, p, g, ws, bst):
    t = p.shape[0]
    return _rows_call(name, lambda ids, *a: (_gmlp_fn(*a),), (t // TM,), _gmlp_ins(p, g, ws, bst),
                      [(_sds((t, GW), BF16), _rowspec(GW), False)])[0]


def _gmlp_bwd(name, p, g, ws, bst, dgm):
    t = p.shape[0]

    def fn(ids, u, v, gv, wv, bv, dv):
        _, vjp = jax.vjp(_gmlp_fn, u, v, gv, wv, bv)
        return vjp(dv)

    return _rows_call(name, fn, (t // TM,), _gmlp_ins(p, g, ws, bst) + [(dgm, _rowspec(GW))],
                      [(_sds((t, GW), BF16), _rowspec(GW), False), (_sds((t, GW), BF16), _rowspec(GW), False),
                       (_sds((1, GW), F32), _full((1, GW)), True), (_sds((4, 128, 128), F32), _full((4, 128, 128)), True),
                       (_sds((128, 4), F32), _full((128, 4)), True)])


def _qk_fn(q, k, gq, gk, cos, sin, seg, perm):
    cq, sq = jnp.concatenate([cos] * 4, axis=-1), jnp.concatenate([sin] * 4, axis=-1)
    q, k = q.astype(F32), k.astype(F32)
    qn = q * lax.rsqrt(hdot(q * q, seg) + EPS) * gq
    kn = k * lax.rsqrt(hdot(k * k, seg[:128, :128]) + EPS) * gk
    qr = qn * cq + hdot(qn, perm) * sq
    kr = kn * cos + hdot(kn, perm[:128, :128]) * sin
    return qr * (HD ** -0.5), kr


def _qk_ins(p, gq, gk, cos, sin, seg, perm):
    return [(p, _rowspec(512, OFF["q"])), (p, _rowspec(128, OFF["k"])), (gq, _full((1, 512))), (gk, _full((1, 128))),
            (cos, _rowspec(128)), (sin, _rowspec(128)), (seg, _full((512, 512))), (perm, _full((512, 512)))]


def _qk_fwd(name, p, gq, gk, cos, sin, seg, perm):
    t = p.shape[0]
    fn = lambda ids, q, k, a, b, c, s, sg, pm, v: _qk_fn(q, k, a, b, c, s, sg, pm) + (v,)
    return _rows_call(name, fn, (t // TM,), _qk_ins(p, gq, gk, cos, sin, seg, perm) + [(p, _rowspec(128, OFF["v"]))],
                      [(_sds((t, 512), BF16), _rowspec(512), False), (_sds((t, 128), BF16), _rowspec(128), False),
                       (_sds((t, 128), BF16), _rowspec(128), False)])


def _qk_bwd(name, p, gq, gk, cos, sin, seg, perm, dqr, dkr):
    t = p.shape[0]

    def fn(ids, q, k, a, b, c, s, sg, pm, dq, dk):
        _, vjp = jax.vjp(lambda q_, k_, a_, b_: _qk_fn(q_, k_, a_, b_, c, s, sg, pm), q, k, a, b)
        return vjp((dq, dk))

    return _rows_call(name, fn, (t // TM,),
                      _qk_ins(p, gq, gk, cos, sin, seg, perm) + [(dqr, _rowspec(512)), (dkr, _rowspec(128))],
                      [(_sds((t, 512), BF16), _rowspec(512), False), (_sds((t, 128), BF16), _rowspec(128), False),
                       (_sds((1, 512), F32), _full((1, 512)), True), (_sds((1, 128), F32), _full((1, 128)), True)])


_ATT_TQ = 1024
_ATT_TK = 768


def _attn_fwd(name, q, k, v):
    h, tq_all, _ = q.shape
    hkv, tk_all, _ = k.shape
    tq, tk = _tile(tq_all, _ATT_TQ), _tile(tk_all, _ATT_TK)
    nkc = tk_all // tk

    def body(q_ref, k_ref, v_ref, o_ref, lse_ref):
        qv = q_ref[...].reshape(QG * tq, HD)

        def step(j, carry):
            m, acc = carry
            off = pl.multiple_of(j * tk, tk)
            kk, vv = k_ref[0, pl.ds(off, tk), :], v_ref[0, pl.ds(off, tk), :]
            s = lax.dot_general(qv, kk, (((1,), (1,)), ((), ())), preferred_element_type=F32)
            m_new = jnp.maximum(m, jnp.max(s, axis=-1, keepdims=True))
            pr = jnp.exp(s - m_new)
            acc = jnp.exp(m - m_new) * acc + jnp.dot(pr.astype(BF16), vv, preferred_element_type=F32)
            return m_new, acc

        init = (jnp.full((QG * tq, 1), -jnp.inf, F32), jnp.zeros((QG * tq, 2 * HD), F32))
        m, acc = lax.fori_loop(0, nkc, step, init)
        l = acc[:, HD:HD + 1]
        o_ref[...] = (acc[:, :HD] / l).reshape(QG, tq, HD)
        lse_ref[...] = (m + jnp.log(l)).reshape(QG, tq, 1)

    kv_spec = pl.BlockSpec((1, tk_all, HD), lambda g, i: (g, 0, 0))
    v1_spec = pl.BlockSpec((1, tk_all, 2 * HD), lambda g, i: (g, 0, 0))
    qspec = pl.BlockSpec((QG, tq, HD), lambda g, i: (g, i, 0))
    return pl.pallas_call(
        body, name=name, grid=(hkv, tq_all // tq), in_specs=[qspec, kv_spec, v1_spec],
        out_specs=[qspec, pl.BlockSpec((QG, tq, 1), lambda g, i: (g, i, 0))],
        out_shape=[_sds((h, tq_all, HD), F32), _sds((h, tq_all, 1), F32)],
        compiler_params=_cp(("parallel", "parallel")),
    )(q, k, v)


def _attn_bwd(name, q, k, v, o, do, lse_row):
    h, tq_all, _ = q.shape
    hkv, tk_all, _ = k.shape
    tq, tk = _tile(tq_all, 1024), _tile(tk_all, 1408)

    def body(q_ref, k_ref, v_ref, o_ref, do_ref, lse_ref, dq_ref, dk_ref, dv_ref, dl_ref):
        i, j = pl.program_id(1), pl.program_id(2)

        @pl.when(j == 0)
        def _():
            ones = jnp.ones((8, HD), F32)
            for g in range(QG):
                dl_ref[g] = hdot(ones, do_ref[g] * o_ref[g], 1, 1)

        kk, vv = k_ref[0], v_ref[0]
        dk_acc, dv_acc = jnp.zeros((tk, HD), F32), jnp.zeros((tk, HD), F32)
        for g in range(QG):
            qv, dob = q_ref[g], do_ref[g].astype(BF16)
            st = lax.dot_general(kk, qv, (((1,), (1,)), ((), ())), preferred_element_type=F32)
            pt = jnp.exp(st - lse_ref[g])
            dv_acc += jnp.dot(pt.astype(BF16), dob, preferred_element_type=F32)
            dpt = lax.dot_general(vv, dob, (((1,), (1,)), ((), ())), preferred_element_type=F32)
            dst = (pt * (dpt - dl_ref[g, 0:1, :])).astype(BF16)
            dk_acc += jnp.dot(dst, qv, preferred_element_type=F32)
            dq_part = lax.dot_general(dst, kk, (((0,), (0,)), ((), ())), preferred_element_type=F32)

            @pl.when(j == 0)
            def _():
                dq_ref[g] = dq_part

            @pl.when(j > 0)
            def _():
                dq_ref[g] += dq_part

        rows = pl.ds(pl.multiple_of(j * tk, tk), tk)

        @pl.when(i == 0)
        def _():
            dk_ref[0, rows, :] = dk_acc
            dv_ref[0, rows, :] = dv_acc

        @pl.when(i > 0)
        def _():
            dk_ref[0, rows, :] += dk_acc
            dv_ref[0, rows, :] += dv_acc

    ks = pl.BlockSpec((1, tk, HD), lambda g, i, j: (g, j, 0))
    qs = pl.BlockSpec((QG, tq, HD), lambda g, i, j: (g, i, 0))
    rs = pl.BlockSpec((QG, 1, tq), lambda g, i, j: (g, 0, i))
    full = pl.BlockSpec((1, tk_all, HD), lambda g, i, j: (g, 0, 0))
    return pl.pallas_call(
        body, name=name, grid=(hkv, tq_all // tq, tk_all // tk), in_specs=[qs, ks, ks, qs, qs, rs], out_specs=[qs, full, full],
        out_shape=[_sds((h, tq_all, HD), F32), _sds((hkv, tk_all, HD), F32), _sds((hkv, tk_all, HD), F32)],
        scratch_shapes=[pltpu.VMEM((QG, 8, tq), F32)],
        compiler_params=_cp(("parallel", "arbitrary", "arbitrary")),
    )(q, k, v, o, do, lse_row)


def _decay_fn(a, w2, b2):
    return _log_sigmoid(bdot(a, w2, 1, 0) + b2) / GLA_TAU


def _decay_fwd(name, p, w2, b2):
    t = p.shape[0]
    return _rows_call(name, lambda ids, a, w, b: (_decay_fn(a, w, b),), (t // TM,),
                      [(p, _rowspec(128, OFF["ab"])), (w2, _full((128, 512))), (b2, _full((1, 512)))],
                      [(_sds((t, 512), F32), _rowspec(512), False)])[0]


def _decay_bwd(name, p, w2, b2, gf, gb):
    t = p.shape[0]

    def fn(ids, a, w, b, qf, kf, vf, lf, qb, kb, vb, lb):
        _, vjp = jax.vjp(_decay_fn, a, w, b)
        return vjp(jnp.concatenate([lf, lb], axis=-1)) + (qf + qb, kf + kb, vf + vb)

    widths = (256, 256, 512, 256)
    return _rows_call(name, fn, (t // TM,),
                      [(p, _rowspec(128, OFF["ab"])), (w2, _full((128, 512))), (b2, _full((1, 512)))]
                      + [(g, _rowspec(w)) for g, w in zip(gf, widths)] + [(g, _rowspec(w)) for g, w in zip(gb, widths)],
                      [(_sds((t, 128), BF16), _rowspec(128), False), (_sds((128, 512), F32), _full((128, 512)), True),
                       (_sds((1, 512), F32), _full((1, 512)), True)]
                      + [(_sds((t, w), BF16), _rowspec(w), False) for w in widths[:3]])


def _gla_consts(reverse):
    r = lax.broadcasted_iota(jnp.int32, (GLA_CHUNK, GLA_CHUNK), 0)
    c = lax.broadcasted_iota(jnp.int32, (GLA_CHUNK, GLA_CHUNK), 1)
    trib = (r <= c) if reverse else (r >= c)
    br = lax.broadcasted_iota(jnp.int32, (GLA_QK, GLA_V), 0) // GLA_DK
    bc = lax.broadcasted_iota(jnp.int32, (GLA_QK, GLA_V), 1) // GLA_DV
    lane_head = lax.broadcasted_iota(jnp.int32, (1, GLA_QK), 1) // GLA_DK
    return trib, (br == bc).astype(F32), lane_head


def _gla_chunk(q, k, v, la, s_in, consts):
    trib, bd, lane_head = consts
    q, k = q.astype(F32), k.astype(F32)
    cum = hdot(trib.astype(F32), la)
    tot = jnp.sum(la, axis=0, keepdims=True)
    q_in = q * (GLA_DK ** -0.5) * jnp.exp(cum)
    k_in = k * jnp.exp(-cum)
    k_st = k * jnp.exp(tot - cum)
    outs = []
    for h in range(GLA_H):
        att = bdot(jnp.where(lane_head == h, q_in, 0.0), k_in, 1, 1)
        att = jnp.where(trib, att, 0.0)
        outs.append(bdot(att, v[:, GLA_DV * h:GLA_DV * (h + 1)], 1, 0))
    o = jnp.concatenate(outs, axis=-1) + bdot(q_in, s_in, 1, 0)
    decay = jnp.exp(hdot(la, jnp.ones((GLA_CHUNK, LANE), F32), 0, 0))
    s_out = jnp.concatenate([decay] * (GLA_V // LANE), axis=-1) * s_in + bdot(k_st, v, 0, 0) * bd
    return o, s_out


def _gla_order(nb, reverse, backward):
    if not reverse:
        return (lambda s: nb - 1 - s) if backward else (lambda s: s)
    if backward:
        return lambda s: jnp.where(s == nb - 1, 0, s + 1)
    return lambda s: jnp.where(s == 0, 0, nb - s)


_NCH = TM // GLA_CHUNK


def _gla_specs(nb, reverse, backward):
    order = _gla_order(nb, reverse, backward)
    col = lambda width, off: pl.BlockSpec((TM, width), lambda s, o=off // width: (order(s), o))
    state = pl.BlockSpec((_NCH, GLA_H, GLA_DK, GLA_DV), lambda s: (order(s), 0, 0, 0))
    qkvla = [col(256, OFF["glq"]), col(256, OFF["glk"]), col(512, OFF["glv"]), col(256, 256 * int(reverse))]
    return col, state, qkvla


def _gla_fwd(name, p, la):
    t = p.shape[0]
    nb = t // TM

    def body(*refs):
        ins, outs, scr = (refs[0:4], refs[4:8]), (refs[8:10], refs[10:12]), refs[12:14]

        @pl.when(pl.program_id(0) == 0)
        def _():
            for s_ref in scr:
                s_ref[...] = jnp.zeros_like(s_ref)

        for step in range(_NCH):
            for d in range(2):
                (q_ref, k_ref, v_ref, la_ref), (o_ref, sv_ref), s_ref = ins[d], outs[d], scr[d]
                c = _NCH - 1 - step if d else step
                rows = slice(GLA_CHUNK * c, GLA_CHUNK * (c + 1))
                s_in = s_ref[...]
                for h in range(GLA_H):
                    sv_ref[c, h] = s_in[GLA_DK * h:GLA_DK * (h + 1), GLA_DV * h:GLA_DV * (h + 1)]
                o, s_out = _gla_chunk(q_ref[rows, :], k_ref[rows, :], v_ref[rows, :], la_ref[rows, :], s_in,
                                      _gla_consts(bool(d)))
                o_ref[rows, :] = o
                s_ref[...] = s_out

    in_specs, out_specs, out_shape = [], [], []
    for d in range(2):
        col, state, qkvla = _gla_specs(nb, bool(d), False)
        in_specs += qkvla
        out_specs += [col(512, 0), state]
        out_shape += [_sds((t, GLA_V), F32), _sds((t // GLA_CHUNK, GLA_H, GLA_DK, GLA_DV), F32)]
    return pl.pallas_call(
        body, name=name, grid=(nb,), in_specs=in_specs, out_specs=out_specs, out_shape=out_shape,
        scratch_shapes=[pltpu.VMEM((GLA_QK, GLA_V), F32)] * 2, compiler_params=_cp(("arbitrary",)),
    )(p, p, p, la, p, p, p, la)


def _gla_bwd(name, p, la, sv_f, sv_b, do):
    t = p.shape[0]
    nb = t // TM

    def body(*refs):
        ins, outs, scr = (refs[0:6], refs[6:12]), (refs[12:16], refs[16:20]), refs[20:22]

        @pl.when(pl.program_id(0) == 0)
        def _():
            for ds_ref in scr:
                ds_ref[...] = jnp.zeros_like(ds_ref)

        zero = jnp.zeros((GLA_DK, GLA_DV), F32)
        for step in range(_NCH):
            for d in range(2):
                (q_ref, k_ref, v_ref, la_ref, sv_ref, do_ref), out_refs, ds_ref = ins[d], outs[d], scr[d]
                c = step if d else _NCH - 1 - step
                rows = slice(GLA_CHUNK * c, GLA_CHUNK * (c + 1))
                s_in = jnp.concatenate(
                    [jnp.concatenate([sv_ref[c, h] if hh == h else zero for hh in range(GLA_H)], axis=-1)
                     for h in range(GLA_H)], axis=0)
                consts = _gla_consts(bool(d))
                _, vjp = jax.vjp(lambda a, b, cc, dd, e: _gla_chunk(a, b, cc, dd, e, consts),
                                 q_ref[rows, :], k_ref[rows, :], v_ref[rows, :], la_ref[rows, :], s_in)
                grads = vjp((do_ref[rows, :], ds_ref[...]))
                for o_ref, g in zip(out_refs, grads[:4]):
                    o_ref[rows, :] = g.astype(o_ref.dtype)
                ds_ref[...] = grads[4]

    ins, in_specs, out_specs, out_shape = [], [], [], []
    for d, sv in enumerate((sv_f, sv_b)):
        col, state, qkvla = _gla_specs(nb, bool(d), True)
        ins += [p, p, p, la, sv, do]
        in_specs += qkvla + [state, col(512, 0)]
        out_specs += [col(256, 0), col(256, 0), col(512, 0), col(256, 0)]
        out_shape += [_sds((t, GLA_QK), F32), _sds((t, GLA_QK), F32), _sds((t, GLA_V), F32), _sds((t, GLA_QK), F32)]
    return pl.pallas_call(
        body, name=name, grid=(nb,), in_specs=in_specs, out_specs=out_specs, out_shape=out_shape,
        scratch_shapes=[pltpu.VMEM((GLA_QK, GLA_V), F32)] * 2, compiler_params=_cp(("arbitrary",)),
    )(*ins)


def _gla_out_fn(of, ob, r, g):
    o = of + ob
    cols = [_rms(o[:, GLA_DV * h:GLA_DV * (h + 1)], g[:, GLA_DV * h:GLA_DV * (h + 1)]) for h in range(GLA_H)]
    return jnp.concatenate(cols, axis=-1) * jax.nn.silu(r.astype(F32))


def _gla_out_fwd(name, of, ob, p, g):
    t = p.shape[0]
    return _rows_call(name, lambda ids, *a: (_gla_out_fn(*a),), (t // TM,),
                      [(of, _rowspec(512)), (ob, _rowspec(512)), (p, _rowspec(512, OFF["gr"])), (g, _full((1, 512)))],
                      [(_sds((t, 512), BF16), _rowspec(512), False)])[0]


def _gla_out_bwd(name, of, ob, p, g, dgla):
    t = p.shape[0]

    def fn(ids, a, b, r, gv, dv):
        _, vjp = jax.vjp(_gla_out_fn, a, b, r, gv)
        do, _, dr, dg = vjp(dv)
        return do, dr, dg

    return _rows_call(name, fn, (t // TM,),
                      [(of, _rowspec(512)), (ob, _rowspec(512)), (p, _rowspec(512, OFF["gr"])), (g, _full((1, 512))),
                       (dgla, _rowspec(512))],
                      [(_sds((t, 512), F32), _rowspec(512), False), (_sds((t, 512), BF16), _rowspec(512), False),
                       (_sds((1, 512), F32), _full((1, 512)), True)])


_TMM = 384


def _merge_fwd(name, gm, att, gla, wa, wb, wc, p):
    t = p.shape[0]
    row = lambda w, off=0: pl.BlockSpec((_TMM, w), lambda i, o=off // w: (i, o))

    def fn(ids, a, b, c, wa_, wb_, wc_, ga, gb, gc):
        ga, gb, gc = ga.astype(F32), gb.astype(F32), gc.astype(F32)
        return (jax.nn.sigmoid(ga) * bdot(a, wa_, 1, 0) + jax.nn.sigmoid(gb) * bdot(b, wb_, 1, 0)
                + jax.nn.sigmoid(gc) * bdot(c, wc_, 1, 0),)

    return _rows_call(name, fn, (t // _TMM,),
                      [(gm, row(512)), (att, row(512)), (gla, row(512)), (wa, _full((512, D))), (wb, _full((512, D))),
                       (wc, _full((512, D))), (p, row(D, OFF["gA"])), (p, row(D, OFF["gB"])), (p, row(D, OFF["gC"]))],
                      [(_sds((t, D), BF16), row(D), False)])[0]


def _merge_bwd(name, gm, att, gla, wa, wb, wc, p, dmerged):
    t = p.shape[0]
    row = lambda w, off=0: pl.BlockSpec((_TMM, w), lambda i, o=off // w: (i, o))

    def fn(ids, a, b, c, wa_, wb_, wc_, ga, gb, gc, dm):
        ga, gb, gc = ga.astype(F32), gb.astype(F32), gc.astype(F32)
        outs_y, outs_g = [], []
        for br, w, g in ((a, wa_, ga), (b, wb_, gb), (c, wc_, gc)):
            s = jax.nn.sigmoid(g)
            outs_y.append(dm * s)
            outs_g.append(dm * bdot(br, w, 1, 0) * s * (1.0 - s))
        return tuple(outs_y) + tuple(outs_g)

    o = (_sds((t, D), BF16), row(D), False)
    return _rows_call(name, fn, (t // _TMM,),
                      [(gm, row(512)), (att, row(512)), (gla, row(512)), (wa, _full((512, D))), (wb, _full((512, D))),
                       (wc, _full((512, D))), (p, row(D, OFF["gA"])), (p, row(D, OFF["gB"])), (p, row(D, OFF["gC"])),
                       (dmerged, row(D))], [o] * 6)


_TNC = 1408
_NJ = FFN // _TNC


HALO = 16


def _shift_rows(x, prev, nxt, vp, vn):
    n = x.shape[0]
    rid = lax.broadcasted_iota(jnp.int32, x.shape, 0)
    xp = jnp.where(rid == 0, jnp.where(vp, prev[HALO - 1:HALO, :], 0.0), pltpu.roll(x, 1, 0))
    xn = jnp.where(rid == n - 1, jnp.where(vn, nxt[0:1, :], 0.0), pltpu.roll(x, n - 1, 0))
    return xp, xn


def _seq_edges(i, t):
    start, end = i * TM, (i + 1) * TM
    return jnp.logical_and(start != 0, start != TC), jnp.logical_and(end != TC, end != t)


def _halo_specs(t, colmap):
    r = TM // HALO
    main = pl.BlockSpec((TM, _TNC), lambda j, i: (i, colmap(j)))
    prev = pl.BlockSpec((HALO, _TNC), lambda j, i: (jnp.maximum(i * r - 1, 0), colmap(j)))
    nxt = pl.BlockSpec((HALO, _TNC), lambda j, i: (jnp.minimum((i + 1) * r, t // HALO - 1), colmap(j)))
    return [main, prev, nxt]


def _conv3(x, xp, xn, w, b=None):
    y = xp * w[0:1, :] + x * w[1:2, :] + xn * w[2:3, :]
    return y if b is None else b + y


def _conv_fwd(name, a, cw, cb):
    t = a.shape[0]

    def fn(ids, ag, agp, agn, av, avp, avn, wg, wv, bg, bv):
        vp, vn = _seq_edges(ids[1], t)
        ag, agp, agn, av, avp, avn = (z.astype(F32) for z in (ag, agp, agn, av, avp, avn))
        cg = _conv3(ag, *_shift_rows(ag, agp, agn, vp, vn), wg, bg)
        cv = _conv3(av, *_shift_rows(av, avp, avn, vp, vn), wv, bv)
        return (jax.nn.silu(cg) * cv,)

    gcol, vcol = (lambda j: j), (lambda j: j + _NJ)
    wspec = lambda cm: pl.BlockSpec((3, _TNC), lambda j, i: (0, cm(j)))
    bspec = lambda cm: pl.BlockSpec((1, _TNC), lambda j, i: (0, cm(j)))
    ins = [(a, s) for s in _halo_specs(t, gcol) + _halo_specs(t, vcol)]
    ins += [(cw, wspec(gcol)), (cw, wspec(vcol)), (cb, bspec(gcol)), (cb, bspec(vcol))]
    return _rows_call(name, fn, (_NJ, t // TM), ins,
                      [(_sds((t, FFN), BF16), pl.BlockSpec((TM, _TNC), lambda j, i: (i, j)), False)])[0]


def _conv_bwd(name, a, cw, cb, dact):
    t = a.shape[0]
    n = TM + 2 * HALO

    def fn(ids, ag, agp, agn, av, avp, avn, dv, dvp, dvn, wg, wv, bg, bv):
        vp, vn = _seq_edges(ids[1], t)
        ag, agp, agn, av, avp, avn = (z.astype(F32) for z in (ag, agp, agn, av, avp, avn))
        ext = lambda x, xp, xn: jnp.concatenate([jnp.where(vp, xp, 0.0), x, jnp.where(vn, xn, 0.0)], axis=0)
        up, dn = (lambda x: pltpu.roll(x, 1, 0)), (lambda x: pltpu.roll(x, n - 1, 0))
        main = lambda y: y[HALO:HALO + TM]
        eg, ev, ed = ext(ag, agp, agn), ext(av, avp, avn), ext(dv, dvp, dvn)
        cg = _conv3(eg, up(eg), dn(eg), wg, bg)
        cv = _conv3(ev, up(ev), dn(ev), wv, bv)
        s = jax.nn.sigmoid(cg)
        rid = lax.broadcasted_iota(jnp.int32, (3, eg.shape[1]), 0)
        das, dws, dbs = [], [], []
        for dc, w, x in ((ed * cv * s * (1.0 + cg * (1.0 - s)), wg, eg), (ed * cg * s, wv, ev)):
            shifted = [main(dn(dc)), main(dc), main(up(dc))]
            das.append(shifted[0] * w[0:1, :] + shifted[1] * w[1:2, :] + shifted[2] * w[2:3, :])
            sums = [jnp.sum(y * main(x), axis=0, keepdims=True) for y in shifted]
            dws.append(jnp.where(rid == 0, sums[0], jnp.where(rid == 1, sums[1], sums[2])))
            dbs.append(jnp.sum(shifted[1], axis=0, keepdims=True))
        return jnp.stack(das), jnp.stack(dws), jnp.stack(dbs)

    gcol, vcol = (lambda j: j), (lambda j: j + _NJ)
    wspec = lambda cm: pl.BlockSpec((3, _TNC), lambda j, i: (0, cm(j)))
    bspec = lambda cm: pl.BlockSpec((1, _TNC), lambda j, i: (0, cm(j)))
    ins = [(a, s) for s in _halo_specs(t, gcol) + _halo_specs(t, vcol)] + [(dact, s) for s in _halo_specs(t, gcol)]
    ins += [(cw, wspec(gcol)), (cw, wspec(vcol)), (cb, bspec(gcol)), (cb, bspec(vcol))]
    return _rows_call(name, fn, (_NJ, t // TM), ins,
                      [(_sds((2, t, FFN), BF16), pl.BlockSpec((2, TM, _TNC), lambda j, i: (0, i, j)), False),
                       (_sds((2, 3, FFN), F32), pl.BlockSpec((2, 3, _TNC), lambda j, i: (0, 0, j)), True),
                       (_sds((2, 1, FFN), F32), pl.BlockSpec((2, 1, _TNC), lambda j, i: (0, 0, j)), True)])


_TNA = 512


def _adaln_fwd(name, cond, w, b):
    fn = lambda ids, cv, wv, bv: ((bdot(jax.nn.silu(cv), wv[0], 1, 0) + bv[0])[None],)
    return _rows_call(name, fn, (2, ADA_LOC // _TNA),
                      [(cond, _full((16, D))), (w, pl.BlockSpec((1, D, _TNA), lambda l, j: (l, 0, j))),
                       (b, pl.BlockSpec((1, 1, _TNA), lambda l, j: (l, 0, j)))],
                      [(_sds((2, 16, ADA_LOC), F32), pl.BlockSpec((1, 16, _TNA), lambda l, j: (l, 0, j)), False)])[0]


def _adaln_bwd(name, c8, cc8, w, dl, dc):
    def fn(ids, cv, ccv, wv, dlv, dcv):
        dcs = jnp.broadcast_to(jnp.sum(dcv[0], axis=0, keepdims=True), dcv[0].shape)
        dw = hdot(jax.nn.silu(cv), dlv[0], 0, 0) + hdot(jax.nn.silu(ccv), dcs, 0, 0)
        s = jax.nn.sigmoid(ccv)
        rid = lax.broadcasted_iota(jnp.int32, ccv.shape, 0)
        dcc = jnp.where(rid == 0, bdot(dcs, wv[0], 1, 1) * s * (1.0 + ccv * (1.0 - s)), 0.0)
        return dw[None], dcc

    dspec = pl.BlockSpec((1, 8, _TNA), lambda l, j: (l, 0, j))
    return _rows_call(name, fn, (2, ADA_LOC // _TNA),
                      [(c8, _full((8, D))), (cc8, _full((8, D))), (w, pl.BlockSpec((1, D, _TNA), lambda l, j: (l, 0, j))),
                       (dl, dspec), (dc, dspec)],
                      [(_sds((2, D, ADA_LOC), F32), pl.BlockSpec((1, D, _TNA), lambda l, j: (l, 0, j)), False),
                       (_sds((8, D), F32), _full((8, D)), True)], acc_axes=(0, 1))


def _adamw_fn(w, g, m, v):
    m = ADAM_B1 * m + (1.0 - ADAM_B1) * g
    v = ADAM_B2 * v + (1.0 - ADAM_B2) * (g * g)
    m_hat = m / (1.0 - ADAM_B1 ** ADAM_STEP)
    v_hat = v / (1.0 - ADAM_B2 ** ADAM_STEP)
    return -ADAM_LR * (m_hat / (jnp.sqrt(v_hat) + ADAM_EPS) + ADAM_WD * w), m, v


def _adamw(name, w, g, m, v):
    l, r, c = w.shape
    tr = _tile(r, max(8, (1 << 20) // (4 * c)), 8)
    spec = pl.BlockSpec((None, tr, c), lambda i, j: (i, j, 0))
    o = (_sds((l, r, c), F32), spec, False)
    return _rows_call(name, lambda ids, *a: _adamw_fn(*a), (l, r // tr), [(x, spec) for x in (w, g, m, v)], [o, o, o],
                      sem=("parallel", "parallel"))


def _coords():
    return lax.axis_index("x"), lax.axis_index("y"), lax.axis_index("c")


def _other_chips(x, y):
    return [(1 - x, y), (x, 1 - y), (1 - x, 1 - y)]


def _allgather_small(name, blk):
    m_per, n = blk.shape

    def body(x_ref, out_ref, send_sems, recv_sems, local_sem):
        x, y, c = _coords()
        me, sibling = (x, y, c), (x, y, 1 - c)
        chips = _other_chips(x, y)

        def rows(px, py, pc):
            return out_ref.at[pl.ds((4 * px + 2 * py + pc) * m_per, m_per), :]

        def copy(k, block, to, src=None):
            return pltpu.make_async_remote_copy(
                src_ref=rows(*block) if src is None else src, dst_ref=rows(*block), send_sem=send_sems.at[k],
                recv_sem=recv_sems.at[k], device_id=to, device_id_type=MESH)

        mine = pltpu.make_async_copy(x_ref, rows(*me), local_sem)
        mine.start()
        first = [copy(0, me, sibling, src=x_ref)]
        first += [copy(1 + j, me, (*chip, c), src=x_ref) for j, chip in enumerate(chips)]
        for cp in first:
            cp.start()
        passed = [copy(4 + j, (*chip, c), sibling) for j, chip in enumerate(chips)]
        for j, chip in enumerate(chips):
            copy(1 + j, (*chip, c), me).wait_recv()
            passed[j].start()
        copy(0, sibling, me).wait_recv()
        for j, chip in enumerate(chips):
            copy(4 + j, (*chip, 1 - c), me).wait_recv()
        for cp in first + passed:
            cp.wait_send()
        mine.wait()

    return pl.pallas_call(
        body, name=name, out_shape=_sds((N_DEV * m_per, n), blk.dtype),
        in_specs=[pl.BlockSpec(memory_space=pltpu.VMEM)], out_specs=pl.BlockSpec(memory_space=pltpu.VMEM),
        scratch_shapes=[pltpu.SemaphoreType.DMA((7,)), pltpu.SemaphoreType.DMA((7,)), pltpu.SemaphoreType.DMA],
        compiler_params=pltpu.CompilerParams(vmem_limit_bytes=VMEM_LIMIT),
    )(blk)


_ANY = pl.BlockSpec(memory_space=pl.ANY)


def _remote(src, dst, send_sems, recv_sems, s, to):
    return pltpu.make_async_remote_copy(src_ref=src, dst_ref=dst, send_sem=send_sems.at[s], recv_sem=recv_sems.at[s],
                                        device_id=to, device_id_type=MESH)


def _comm_call(name, body, ins, out_shapes, n_sems, n_local):
    return pl.pallas_call(
        body, name=name, out_shape=out_shapes, in_specs=[_ANY] * len(ins), out_specs=[_ANY] * len(out_shapes),
        scratch_shapes=[pltpu.SemaphoreType.DMA((n_sems,)), pltpu.SemaphoreType.DMA((n_sems,)),
                        pltpu.SemaphoreType.DMA((n_local,))],
    )(*ins)


def _allgather_layers(name, locs):
    n = len(locs)

    def body(*refs):
        ins, outs, (send_sems, recv_sems, local_sems) = refs[:n], refs[n:2 * n], refs[2 * n:]
        x, y, c = _coords()
        k = 2 * x + y
        sibling = (x, y, 1 - c)
        chips = _other_chips(x, y)
        first = [_remote(ins[t].at[c], outs[t].at[k, c], send_sems, recv_sems, 6 * t + j, (*chip, c))
                 for t in range(n) for j, chip in enumerate(chips)]
        for cp in first:
            cp.start()
        passed = []
        for t in range(n):
            for j, (cx, cy) in enumerate(chips):
                there = outs[t].at[2 * cx + cy, c]
                _remote(there, there, send_sems, recv_sems, 6 * t + j, sibling).wait_recv()
                passed.append(_remote(there, there, send_sems, recv_sems, 6 * t + 3 + j, sibling))
                passed[-1].start()
        for t in range(n):
            for j, (cx, cy) in enumerate(chips):
                there = outs[t].at[2 * cx + cy, 1 - c]
                _remote(there, there, send_sems, recv_sems, 6 * t + 3 + j, sibling).wait_recv()
        for cp in first + passed:
            cp.wait_send()

    outs = _comm_call(name, body, locs, [_sds((N_CHIP,) + a.shape, a.dtype) for a in locs], 6 * n, 1)
    k = 2 * lax.axis_index("x") + lax.axis_index("y")
    return [lax.dynamic_update_slice_in_dim(o, a[None], k, axis=0) for o, a in zip(outs, locs)]


def _rs_pair_exchange(name, g0, g1):
    n = len(g0)

    def body(*refs):
        a0, a1, outs, (send_sems, recv_sems, _) = refs[:n], refs[n:2 * n], refs[2 * n:3 * n], refs[3 * n:]
        x, y, c = _coords()

        def run(srcs):
            cps = [_remote(srcs[t], outs[t], send_sems, recv_sems, t, (x, y, 1 - c)) for t in range(n)]
            for cp in cps:
                cp.start()
            for cp in cps:
                cp.wait()

        pl.when(c == 0)(lambda: run(a1))
        pl.when(c == 1)(lambda: run(a0))

    return _comm_call(name, body, list(g0) + list(g1), [_sds(a.shape, a.dtype) for a in g0], n, 1)


def _ew2d(name, fn, ins, out_dtype):
    shape = ins[0].shape
    r, c = _prod(shape[:-1]), shape[-1]
    tr = _tile(r, max(8, (1 << 20) // (4 * c)), 8)
    spec = pl.BlockSpec((tr, c), lambda i: (i, 0))
    out = _rows_call(name, lambda ids, *a: (fn(*a),), (r // tr,), [(a.reshape(r, c), spec) for a in ins],
                     [(_sds((r, c), out_dtype), spec, False)], sem=("parallel",))[0]
    return out.reshape(shape)


def _rs_chip_exchange(name, s1):
    n = len(s1)

    def body(*refs):
        ins, outs, (send_sems, recv_sems, local_sems) = refs[:n], refs[n:2 * n], refs[2 * n:]
        x, y, c = _coords()
        k = 2 * x + y
        chips = _other_chips(x, y)
        cps = [_remote(ins[t].at[2 * cx + cy], outs[t].at[k], send_sems, recv_sems, 3 * t + j, (cx, cy, c))
               for t in range(n) for j, (cx, cy) in enumerate(chips)]
        for cp in cps:
            cp.start()
        for t in range(n):
            for j, (cx, cy) in enumerate(chips):
                there = outs[t].at[2 * cx + cy]
                _remote(there, there, send_sems, recv_sems, 3 * t + j, (cx, cy, c)).wait_recv()
        for cp in cps:
            cp.wait_send()

    outs = _comm_call(name, body, s1, [_sds(a.shape, a.dtype) for a in s1], 3 * n, 1)
    k = 2 * lax.axis_index("x") + lax.axis_index("y")
    own = [lax.dynamic_index_in_dim(a, k, axis=0, keepdims=True) for a in s1]
    return [lax.dynamic_update_slice_in_dim(o, a, k, axis=0) for o, a in zip(outs, own)]


def _sum_slots(name, a):
    s, r, cdim = a.shape
    tr = _tile(r, 512, 8)

    def fn(ids, av):
        tot = av[0]
        for i in range(1, s):
            tot = tot + av[i]
        return (tot,)

    return _rows_call(name, fn, (r // tr,), [(a, pl.BlockSpec((s, tr, cdim), lambda i: (0, i, 0)))],
                      [(_sds((r, cdim), F32), pl.BlockSpec((tr, cdim), lambda i: (i, 0)), False)], sem=("parallel",))[0]


def _pair_allgather(name, halves):
    n = len(halves)

    def body(*refs):
        ins, outs, (send_sems, recv_sems, local_sems) = refs[:n], refs[n:2 * n], refs[2 * n:]
        x, y, c = _coords()
        cps = [_remote(ins[t], outs[t].at[c], send_sems, recv_sems, t, (x, y, 1 - c)) for t in range(n)]
        for cp in cps:
            cp.start()
        for t in range(n):
            _remote(ins[t], outs[t].at[1 - c], send_sems, recv_sems, t, (x, y, 1 - c)).wait_recv()
        for cp in cps:
            cp.wait_send()

    outs = _comm_call(name, body, halves, [_sds((2,) + a.shape, a.dtype) for a in halves], n, 1)
    return [lax.dynamic_update_slice_in_dim(o, a[None], lax.axis_index("c"), axis=0) for o, a in zip(outs, halves)]


def _reduce_scatter(g0, g1):
    n = len(g0)
    got = _rs_pair_exchange("rs_pair_exchange", g0, g1)
    keep = lambda a, b, r: jnp.where(lax.axis_index("c") == 0, a, b) + r
    s1 = [_ew2d("rs_pair_add_%d" % t, keep, [g0[t], g1[t], got[t]], BF16) for t in range(n)]
    slots = _rs_chip_exchange("rs_chip_exchange", s1)
    red = [_sum_slots("rs_chip_sum_%d" % t, a.reshape(N_CHIP, -1, a.shape[-1])).reshape(a.shape[1:])
           for t, a in enumerate(slots)]
    return _pair_allgather("rs_pair_allgather", red)


PACK_C = 1024
_SHARDED = (("w_in", 1), ("w_br_a", 1), ("w_br_b", 1), ("w_br_c", 1), ("w_out", 0), ("w_ffn_up", 1), ("w_ffn_down", 0))
_SHARDED_SMALL = (("conv_w", (3, 2 * FFN), 1), ("w_alpha2", (2, 16, GLA_QK), 2), ("b_alpha", (2, GLA_QK), 1))


def _prod(shape):
    n = 1
    for s in shape:
        n *= s
    return n


def _to_blocks(full, axis):
    shp = full.shape
    split = full.reshape(shp[:axis] + (N_CHIP, shp[axis] // N_CHIP) + shp[axis + 1:])
    return jnp.moveaxis(split, axis, 0)


def _from_blocks(blocks, axis):
    return jnp.concatenate([blocks[k] for k in range(N_CHIP)], axis=axis)


def _rope_tables(tx):
    pos = jnp.arange(tx, dtype=jnp.int32)
    inv_freq = 10000.0 ** (-jnp.arange(16, dtype=F32) / 16)
    ang_r = (pos // GRID_W).astype(F32)[:, None] * inv_freq
    ang_c = (pos % GRID_W).astype(F32)[:, None] * inv_freq
    ang = jnp.concatenate([ang_r, ang_r, ang_c, ang_c], axis=-1)
    sign = jnp.concatenate([-jnp.ones((16,), F32), jnp.ones((16,), F32)] * 2)
    cos = jnp.concatenate([jnp.ones((TC, HD), F32), jnp.cos(ang)], axis=0)
    sin = jnp.concatenate([jnp.zeros((TC, HD), F32), jnp.sin(ang) * sign], axis=0)
    return jnp.tile(cos, (1, 2)), jnp.tile(sin, (1, 2))


def _lane_consts():
    l = jnp.arange(512)
    seg = (l[:, None] // HD == l[None, :] // HD).astype(F32) / HD
    partner = jnp.where(l % 32 < 16, l + 16, l - 16)
    perm = (l[:, None] == partner[None, :]).astype(F32)
    return seg, perm


def _heads(a, n):
    return a.reshape(a.shape[0], n, HD).transpose(1, 0, 2)


def _unheads(a):
    return a.transpose(1, 0, 2).reshape(a.shape[1], a.shape[0] * HD)


def _gather_f32_shards(shards):
    sizes = [_prod(a.shape) for a in shards]
    flat = jnp.concatenate([a.reshape(-1) for a in shards] + [jnp.zeros((16 * PACK_C - sum(sizes),), F32)])
    got = _allgather_small("gather_f32_shards", flat.reshape(16, PACK_C)).reshape(N_CHIP, 2, 16 * PACK_C)[:, 0]
    out, o = {}, 0
    for (n, _, ax), a, sz in zip(_SHARDED_SMALL, shards, sizes):
        out[n] = jnp.concatenate([got[k, o:o + sz].reshape(a.shape) for k in range(N_CHIP)], axis=ax + 1)
        o += sz
    return out


def _layer_params(l, wfull, small):
    w2 = small["w_alpha2_full"][l]
    w2pad = jnp.zeros((128, 512), F32).at[0:16, 0:256].set(w2[0]).at[16:32, 256:512].set(w2[1])
    full = {n: _from_blocks(wfull[n][:, l], ax) for n, ax in _SHARDED}
    return dict(
        w_in=_to_new_cols(full["w_in"]), wa=full["w_br_a"], wb=full["w_br_b"], wc=full["w_br_c"],
        w_out=full["w_out"], w_up=full["w_ffn_up"], w_down=full["w_ffn_down"],
        cw=small["conv_w_full"][l], cb=small["conv_b"][l][None], w2=w2pad,
        b2=small["b_alpha_full"][l].reshape(1, 512),
        g1=small["norm1_g"][l][None], g2=small["norm2_g"][l][None], gq=jnp.tile(small["q_norm_g"][l], 8)[None],
        gk=jnp.tile(small["k_norm_g"][l], 2)[None], ggm=small["gmlp_norm_g"][l][None], ws=small["w_spatial"][l],
        bst=small["b_spatial"][l].T, ggl=small["gla_norm_g"][l][None])


def _layer_fwd(l, last, x, h1, mod, P, tabs):
    cos, sin, seg, perm = tabs
    n = "l%d_" % l
    s = dict(x=x, h1=h1)
    p = _mm(n + "in_proj", h1, P["w_in"], "nn", BF16, tm_t=768, tn_t=2176, j_outer=True)
    s["p"] = p
    s["gm"] = _gmlp_fwd(n + "gmlp", p, P["ggm"], P["ws"], P["bst"])
    qr, kr, vb = _qk_fwd(n + "qk_prep", p, P["gq"], P["gk"], cos, sin, seg, perm)
    qx, qc, kh, vh = _heads(qr[TC:], NQ), _heads(qr[:TC], NQ), _heads(kr, NKV), _heads(vb, NKV)
    s["qx"], s["qc"], s["kh"], s["vh"] = qx, qc, kh, vh
    one_hot = (jnp.arange(HD) == 0).astype(BF16)
    v1 = jnp.concatenate([vh, jnp.broadcast_to(one_hot, vh.shape)], axis=-1)
    ox, lse_x = _attn_fwd(n + "attn_x", qx, kh, v1)
    s["ox"], s["lse_x"] = ox, lse_x
    if last:
        oc = jnp.zeros((NQ, TC, HD), F32)
    else:
        oc, lse_c = _attn_fwd(n + "attn_c", qc, kh[:, :TC], v1[:, :TC])
        s["oc"], s["lse_c"] = oc, lse_c
    s["att"] = jnp.concatenate([_unheads(oc), _unheads(ox)], axis=0).astype(BF16)
    la = _decay_fwd(n + "gla_decay", p, P["w2"], P["b2"])
    s["la"] = la
    s["of"], s["sf"], s["ob"], s["sb"] = _gla_fwd(n + "gla_scan", p, la)
    s["gla"] = _gla_out_fwd(n + "gla_out", s["of"], s["ob"], p, P["ggl"])
    s["merged"] = _merge_fwd(n + "merge", s["gm"], s["att"], s["gla"], P["wa"], P["wb"], P["wc"], p)
    s["mix"] = _mm(n + "out_proj", s["merged"], P["w_out"], "nn", F32)
    s["x_mid"], s["h2"] = _res_nm_fwd(n + "res1_norm2", x, s["mix"], mod, 2, mod, P["g2"], 3, 4)
    s["a"] = _mm(n + "ffn_up", s["h2"], P["w_up"], "nn", BF16, j_outer=True)
    s["act"] = _conv_fwd(n + "conv_gate", s["a"], P["cw"], P["cb"])
    s["f"] = _mm(n + "ffn_down", s["act"], P["w_down"], "nn", F32)
    return s


def _layer_bwd(l, last, s, mod, P, tabs, dx_mid, df, gw):
    cos, sin, seg, perm = tabs
    n = "l%d_b_" % l
    t = dx_mid.shape[0]
    p = s["p"]
    gw["w_ffn_down"] = _mm(n + "ffn_down_w", s["act"], df, "tn", F32, tm_t=1408)
    dact = _mm(n + "ffn_down_x", df, P["w_down"], "nt", F32)
    da, dcw, dcb = _conv_bwd(n + "conv_gate", s["a"], P["cw"], P["cb"], dact)
    gw["conv_w"], gw["conv_b"] = dcw.transpose(1, 0, 2).reshape(3, 2 * FFN), dcb.reshape(2 * FFN)
    gw["w_ffn_up"] = _mm(n + "ffn_up_w", s["h2"], da, "tn", F32, chip_blocks=True)
    dh2 = _mm(n + "ffn_up_x", da, P["w_up"], "nt", F32)
    dx, dmix, dmod_a, dmod_b, dg2 = _res_nm_bwd(n + "res1_norm2", s["x"], s["mix"], mod, 2, mod, P["g2"], 3, 4, dx_mid, dh2)
    dmod = dmod_a + dmod_b
    gw["norm2_g"] = dg2[0]
    gw["w_out"] = _mm(n + "out_proj_w", s["merged"], dmix, "tn", F32)
    dmerged = _mm(n + "out_proj_x", dmix, P["w_out"], "nt", F32)
    dya, dyb, dyc, dga, dgb, dgc = _merge_bwd(n + "merge", s["gm"], s["att"], s["gla"], P["wa"], P["wb"], P["wc"], p, dmerged)
    gw["w_br_a"] = _mm(n + "br_a_w", s["gm"], dya, "tn", F32)
    gw["w_br_b"] = _mm(n + "br_b_w", s["att"], dyb, "tn", F32)
    gw["w_br_c"] = _mm(n + "br_c_w", s["gla"], dyc, "tn", F32)
    dgm = _mm(n + "br_a_x", dya, P["wa"], "nt", F32)
    datt = _mm(n + "br_b_x", dyb, P["wb"], "nt", F32)
    dgla = _mm(n + "br_c_x", dyc, P["wc"], "nt", F32)
    du, dv_g, dggm, dws, dbst = _gmlp_bwd(n + "gmlp", p, P["ggm"], P["ws"], P["bst"], dgm)
    gw["gmlp_norm_g"], gw["w_spatial"], gw["b_spatial"] = dggm[0], dws, dbst.T
    kh, vh = s["kh"], s["vh"]
    row = lambda a: a.reshape(a.shape[0], 1, a.shape[1])
    dqx, dkh, dvh = _attn_bwd(n + "attn_x", s["qx"], kh, vh, s["ox"], _heads(datt[TC:], NQ), row(s["lse_x"]))
    if last:
        dqc = jnp.zeros((NQ, TC, HD), F32)
    else:
        dqc, dkc, dvc = _attn_bwd(n + "attn_c", s["qc"], kh[:, :TC], vh[:, :TC], s["oc"], _heads(datt[:TC], NQ),
                                  row(s["lse_c"]))
        pad = jnp.zeros((NKV, t - TC, HD), F32)
        dkh = dkh + jnp.concatenate([dkc, pad], axis=1)
        dvh = dvh + jnp.concatenate([dvc, pad], axis=1)
    dqr = jnp.concatenate([_unheads(dqc), _unheads(dqx)], axis=0)
    dq, dk, dgq, dgk = _qk_bwd(n + "qk_prep", p, P["gq"], P["gk"], cos, sin, seg, perm, dqr, _unheads(dkh))
    gw["q_norm_g"], gw["k_norm_g"] = dgq.reshape(8, HD).sum(0), dgk.reshape(2, HD).sum(0)
    dv_att = _unheads(dvh).astype(BF16)
    do, dr, dggl = _gla_out_bwd(n + "gla_out", s["of"], s["ob"], p, P["ggl"], dgla)
    gw["gla_norm_g"] = dggl[0]
    scans = _gla_bwd(n + "gla_scan", p, s["la"], s["sf"], s["sb"], do)
    dab, dw2, db2, dglq, dglk, dglv = _decay_bwd(n + "gla_decay", p, P["w2"], P["b2"], scans[:4], scans[4:])
    gw["w_alpha2"] = jnp.stack([dw2[0:16, 0:256], dw2[16:32, 256:512]])
    gw["b_alpha"] = db2.reshape(2, 256)
    dp = jnp.concatenate([dga, dgb, dgc, du, dv_g, dq, dglv, dr, dglq, dglk, dk, dv_att, dab], axis=-1)
    gw["w_in"] = _to_ref_cols(_mm(n + "in_proj_w", s["h1"], dp, "tn", F32, tn_t=2176, tk_t=768))
    dh1 = _mm(n + "in_proj_x", dp, P["w_in"], "nt", F32, tk_t=2176)
    return dx, dh1, dmod


_SMALL = (("norm1_g", (2, D)), ("norm2_g", (2, D)), ("q_norm_g", (2, HD)), ("k_norm_g", (2, HD)), ("gmlp_norm_g", (2, GW)),
          ("gla_norm_g", (2, GLA_V)), ("w_spatial", (2, 4, 128, 128)), ("b_spatial", (2, 4, 128)), ("conv_b", (2, 2 * FFN)),
          ("final_norm_g", (D,))) + tuple((n, (2,) + s) for n, s, _ in _SHARDED_SMALL)
_SMALL_N = 2 * 2 * ADA_W + sum(_prod(s) for _, s in _SMALL)
_SMALL_R = -(-_SMALL_N // (PACK_C * 8)) * 8


def _mod_tables(c, c_ctx, w_ada, b_ada, k):
    x, y, cc = _coords()
    me = 4 * x + 2 * y + cc
    c_all = _allgather_small("gather_c", jnp.concatenate([c, jnp.zeros((7, D), F32)], axis=0))
    c8 = c_all.reshape(N_DEV, 8, D)[:, 0]
    cond = jnp.concatenate([c8, c_ctx[None], jnp.zeros((7, D), F32)], axis=0)
    b_loc = lax.dynamic_slice_in_dim(b_ada, k * ADA_LOC, ADA_LOC, axis=1)[:, None, :]
    m_loc = _adaln_fwd("adaln", cond, w_ada, b_loc)
    m_all = _allgather_small("gather_mod", m_loc.reshape(32, ADA_LOC)).reshape(N_CHIP, 2, 2, 16, ADA_LOC)[:, 0]
    m_all = m_all.transpose(1, 2, 0, 3).reshape(2, 16, ADA_W)
    rows = jnp.stack([m_all[:, 8], lax.dynamic_index_in_dim(m_all, me, axis=1, keepdims=False)], axis=1)
    return rows.reshape(2, 2, 6, D), c8


def _step(x, c, ctx, c_ctx, W, tgt):
    xc, yc, cc = _coords()
    k = 2 * xc + yc
    tx = x.shape[0]
    t = TC + tx
    small = {n: W[n] for n, _ in _SMALL}
    for n, a in _gather_f32_shards([W[n] for n, _, _ in _SHARDED_SMALL]).items():
        small[n + "_full"] = a

    gathered = _allgather_layers("gather_weights", [W[n].astype(BF16) for n, _ in _SHARDED])
    wfull = {n: a for (n, _), a in zip(_SHARDED, gathered)}
    mods, c8 = _mod_tables(c, c_ctx, W["w_ada"], W["b_ada"], k)
    tabs = _rope_tables(tx) + _lane_consts()
    params = [_layer_params(l, wfull, small) for l in range(2)]

    xs = jnp.concatenate([ctx, x], axis=0)
    h1 = _nm_fwd("l0_norm1", xs, mods[0], params[0]["g1"], 0, 1)
    s0 = _layer_fwd(0, False, xs, h1, mods[0], params[0], tabs)
    x1, h1b = _res_nm_fwd("l0_res2_norm1", s0["x_mid"], s0["f"], mods[0], 5, mods[1], params[1]["g1"], 0, 1)
    s1 = _layer_fwd(1, True, x1, h1b, mods[1], params[1], tabs)
    loss, dxm_l, df_l, dmod_head, dgf = _head("head", s1["x_mid"], s1["f"], mods[1], W["final_norm_g"][None], tgt)

    gws = [dict(), dict()]
    dx1, dh1b, dmod1 = _layer_bwd(1, True, s1, mods[1], params[1], tabs, dxm_l, df_l, gws[1])
    dxm0, df0, dmod0_g, dmod1_s, dg1b = _res_nm_bwd("l0_b_res2_norm1", s0["x_mid"], s0["f"], mods[0], 5, mods[1],
                                                    params[1]["g1"], 0, 1, dx1, dh1b)
    gws[1]["norm1_g"] = dg1b[0]
    dx0, dh1, dmod0 = _layer_bwd(0, False, s0, mods[0], params[0], tabs, dxm0, df0, gws[0])
    grad_x, dmod0_s, dg1 = _nm_bwd("l0_b_norm1", xs, mods[0], params[0]["g1"], 0, 1, dx0, dh1)
    gws[0]["norm1_g"] = dg1[0]
    dmods = jnp.stack([dmod0 + dmod0_g + dmod0_s, dmod1 + dmod1_s + dmod_head])

    stk = {n: jnp.stack([gws[0][n], gws[1][n]]) for n, _ in _SMALL if n != "final_norm_g"}
    stk["final_norm_g"] = dgf[0]
    flat = jnp.concatenate([dmods.reshape(-1)] + [stk[n].reshape(-1) for n, _ in _SMALL])
    flat = jnp.concatenate([flat, jnp.zeros((_SMALL_R * PACK_C - _SMALL_N,), F32)]).reshape(_SMALL_R, PACK_C)
    every = _allgather_small("gather_small_grads", flat).reshape(N_DEV, _SMALL_R, PACK_C)
    tot = _sum_slots("sum_small_grads", every).reshape(-1)
    grads, o = {}, 2 * 2 * ADA_W
    for n, shp in _SMALL:
        grads[n] = tot[o:o + _prod(shp)].reshape(shp)
        o += _prod(shp)
    grads["b_ada"] = tot[:2 * 2 * ADA_W].reshape(2, 2, ADA_W).sum(axis=1)

    dm_every = every[:, :2 * 2 * ADA_W // PACK_C].reshape(N_DEV, 2, 2, ADA_W)
    dm_loc = lax.dynamic_slice_in_dim(dm_every, k * ADA_LOC, ADA_LOC, axis=3).transpose(1, 2, 0, 3)
    cc8 = jnp.concatenate([c_ctx[None], jnp.zeros((7, D), F32)], axis=0)
    grads["w_ada"], dcc = _adaln_bwd("adaln_b", c8, cc8, W["w_ada"], dm_loc[:, 1], dm_loc[:, 0])
    dcc_every = _allgather_small("gather_dcctx", dcc * 0.5).reshape(N_DEV, 8, D)
    grads["c_ctx"] = _sum_slots("sum_dcctx", dcc_every)[0]

    for n, shp, ax in _SHARDED_SMALL:
        grads[n] = lax.dynamic_slice_in_dim(grads[n], k * (shp[ax] // N_CHIP), shp[ax] // N_CHIP, axis=ax + 1)
    blocks = lambda g, n, ax: g if n == "w_ffn_up" else _to_blocks(g, ax)
    red = _reduce_scatter(*[[blocks(gws[l][n], n, ax) for n, ax in _SHARDED] for l in range(2)])
    grads.update({n: a for (n, _), a in zip(_SHARDED, red)})
    return loss[0, 0], grad_x, grads


_WEIGHTS = ("c_ctx", "w_ada", "b_ada", "norm1_g", "norm2_g", "w_in", "q_norm_g", "k_norm_g", "gmlp_norm_g", "w_spatial",
            "b_spatial", "w_alpha2", "b_alpha", "gla_norm_g", "w_br_a", "w_br_b", "w_br_c", "w_out", "w_ffn_up", "conv_w",
            "conv_b", "w_ffn_down", "final_norm_g")
_BIG = ("w_ada", "w_in", "w_br_a", "w_br_b", "w_br_c", "w_out", "w_ffn_up", "w_ffn_down")


def _update(W, G, M, V):
    delta, new_m, new_v = {}, {}, {}
    for n in _BIG:
        delta[n], new_m[n], new_v[n] = _adamw("adamw_" + n, W[n], G[n], M[n], V[n])
    rest = [n for n in _WEIGHTS if n not in _BIG]
    tot = sum(_prod(W[n].shape) for n in rest)
    rows = -(-tot // (PACK_C * 8)) * 8

    def cat(dct):
        flat = jnp.concatenate([dct[n].reshape(-1) for n in rest] + [jnp.zeros((rows * PACK_C - tot,), F32)])
        return flat.reshape(1, rows, PACK_C)

    outs = _adamw("adamw_small", cat(W), cat(G), cat(M), cat(V))
    o = 0
    for n in rest:
        sz, shp = _prod(W[n].shape), W[n].shape
        delta[n], new_m[n], new_v[n] = (a.reshape(-1)[o:o + sz].reshape(shp) for a in outs)
        o += sz
    return delta, new_m, new_v


def kernel(x, c, ctx, c_ctx, w_ada, b_ada, norm1_g, norm2_g, w_in, q_norm_g, k_norm_g, gmlp_norm_g, w_spatial, b_spatial, w_alpha2, b_alpha, gla_norm_g, w_br_a, w_br_b, w_br_c, w_out, w_ffn_up, conv_w, conv_b, w_ffn_down, final_norm_g, loss_target, m_c_ctx, m_w_ada, m_b_ada, m_norm1_g, m_norm2_g, m_w_in, m_q_norm_g, m_k_norm_g, m_gmlp_norm_g, m_w_spatial, m_b_spatial, m_w_alpha2, m_b_alpha, m_gla_norm_g, m_w_br_a, m_w_br_b, m_w_br_c, m_w_out, m_w_ffn_up, m_conv_w, m_conv_b, m_w_ffn_down, m_final_norm_g, v_c_ctx, v_w_ada, v_b_ada, v_norm1_g, v_norm2_g, v_w_in, v_q_norm_g, v_k_norm_g, v_gmlp_norm_g, v_w_spatial, v_b_spatial, v_w_alpha2, v_b_alpha, v_gla_norm_g, v_w_br_a, v_w_br_b, v_w_br_c, v_w_out, v_w_ffn_up, v_conv_w, v_conv_b, v_w_ffn_down, v_final_norm_g):
    W = dict(c_ctx=c_ctx, w_ada=w_ada, b_ada=b_ada, norm1_g=norm1_g, norm2_g=norm2_g, w_in=w_in, q_norm_g=q_norm_g,
             k_norm_g=k_norm_g, gmlp_norm_g=gmlp_norm_g, w_spatial=w_spatial, b_spatial=b_spatial, w_alpha2=w_alpha2,
             b_alpha=b_alpha, gla_norm_g=gla_norm_g, w_br_a=w_br_a, w_br_b=w_br_b, w_br_c=w_br_c, w_out=w_out,
             w_ffn_up=w_ffn_up, conv_w=conv_w, conv_b=conv_b, w_ffn_down=w_ffn_down, final_norm_g=final_norm_g)
    M = dict(c_ctx=m_c_ctx, w_ada=m_w_ada, b_ada=m_b_ada, norm1_g=m_norm1_g, norm2_g=m_norm2_g, w_in=m_w_in,
             q_norm_g=m_q_norm_g, k_norm_g=m_k_norm_g, gmlp_norm_g=m_gmlp_norm_g, w_spatial=m_w_spatial,
             b_spatial=m_b_spatial, w_alpha2=m_w_alpha2, b_alpha=m_b_alpha, gla_norm_g=m_gla_norm_g, w_br_a=m_w_br_a,
             w_br_b=m_w_br_b, w_br_c=m_w_br_c, w_out=m_w_out, w_ffn_up=m_w_ffn_up, conv_w=m_conv_w, conv_b=m_conv_b,
             w_ffn_down=m_w_ffn_down, final_norm_g=m_final_norm_g)
    V = dict(c_ctx=v_c_ctx, w_ada=v_w_ada, b_ada=v_b_ada, norm1_g=v_norm1_g, norm2_g=v_norm2_g, w_in=v_w_in,
             q_norm_g=v_q_norm_g, k_norm_g=v_k_norm_g, gmlp_norm_g=v_gmlp_norm_g, w_spatial=v_w_spatial,
             b_spatial=v_b_spatial, w_alpha2=v_w_alpha2, b_alpha=v_b_alpha, gla_norm_g=v_gla_norm_g, w_br_a=v_w_br_a,
             w_br_b=v_w_br_b, w_br_c=v_w_br_c, w_out=v_w_out, w_ffn_up=v_w_ffn_up, conv_w=v_conv_w, conv_b=v_conv_b,
             w_ffn_down=v_w_ffn_down, final_norm_g=v_final_norm_g)
    loss_local, grad_x, G = _step(x[0], c, ctx[0], c_ctx, W, loss_target[0])
    loss = lax.psum(loss_local, ("x", "y", "c"))
    delta, new_m, new_v = _update(W, G, M, V)
    return (loss, grad_x[None], *[G[n] for n in _WEIGHTS], *[delta[n] for n in _WEIGHTS],
            *[new_m[n] for n in _WEIGHTS], *[new_v[n] for n in _WEIGHTS])
```

```python
import functools

import jax
import jax.numpy as jnp
from jax import lax
from jax.experimental import pallas as pl
from jax.experimental.pallas import tpu as pltpu

F32 = jnp.float32
BF16 = jnp.bfloat16

D = 1024
TC = 256
GRID_W = 64
EPS = 1e-6
HD = 64
NQ = 8
NKV = 2
QG = NQ // NKV
GLA_H = 4
GLA_DK = 64
GLA_DV = 128
GLA_QK = 256
GLA_V = 512
GLA_CHUNK = 64
GLA_TAU = 16.0
GW = 512
FFN = 2816
IN_W = 6432
PW = 6528
ADA_W = 6 * D
N_CHIP = 4
N_DEV = 8
ADA_LOC = ADA_W // N_CHIP

ADAM_LR = 0.001
ADAM_B1 = 0.9
ADAM_B2 = 0.999
ADAM_EPS = 1e-08
ADAM_WD = 0.01
ADAM_STEP = 10

TM = 256
NCB = TC // TM
LANE = 128
VMEM_LIMIT = 48 * 1024 * 1024
MESH = pl.DeviceIdType.MESH

_COLS = (("gA", 3360, 1024), ("gB", 4384, 1024), ("gC", 5408, 1024), ("gu", 0, 512), ("gv", 512, 512),
         ("q", 1024, 512), ("glv", 2304, 512), ("gr", 2848, 512), ("glq", 1792, 256), ("glk", 2048, 256),
         ("k", 1536, 128), ("v", 1664, 128), ("ab", 2816, 32))
OFF = {}
_o = 0
for _n, _s, _w in _COLS:
    OFF[_n] = _o
    _o += max(_w, LANE)
assert _o == PW


def _to_new_cols(w):
    parts = [w[..., s:s + n] for _, s, n in _COLS]
    pad = jnp.zeros(w.shape[:-1] + (PW - IN_W,), w.dtype)
    return jnp.concatenate(parts + [pad], axis=-1)


def _to_ref_cols(w):
    by_start = sorted(_COLS, key=lambda t: t[1])
    return jnp.concatenate([w[..., OFF[n]:OFF[n] + wd] for n, _, wd in by_start], axis=-1)


def _tile(n, target, align=LANE):
    best = None
    t = align
    while t <= min(n, target):
        if n % t == 0:
            best = t
        t += align
    assert best is not None, (n, target, align)
    return best


def _cp(sem=None):
    return pltpu.CompilerParams(dimension_semantics=sem, vmem_limit_bytes=VMEM_LIMIT)


def _bdot_impl(a, b, ca, cb):
    return lax.dot_general(a.astype(BF16), b.astype(BF16), (((ca,), (cb,)), ((), ())),
                           preferred_element_type=F32)


@functools.partial(jax.custom_vjp, nondiff_argnums=(2, 3))
def bdot(a, b, ca, cb):
    return _bdot_impl(a, b, ca, cb)


def _bdot_fwd(a, b, ca, cb):
    return _bdot_impl(a, b, ca, cb), (a, b)


def _bdot_bwd(ca, cb, res, g):
    a, b = res
    da = bdot(g, b, 1, 1 - cb) if ca == 1 else bdot(b, g, 1 - cb, 1)
    db = bdot(a, g, 1 - ca, 0) if cb == 0 else bdot(g, a, 0, 1 - ca)
    return da.astype(a.dtype), db.astype(b.dtype)


bdot.defvjp(_bdot_fwd, _bdot_bwd)


def hdot(a, b, ca=1, cb=0):
    return lax.dot_general(a, b, (((ca,), (cb,)), ((), ())), precision=lax.Precision.HIGH,
                           preferred_element_type=F32)


def _rms(x, g):
    return x * lax.rsqrt(jnp.mean(x * x, axis=-1, keepdims=True) + EPS) * g


def _gelu(x):
    return 0.5 * x * (1.0 + jnp.tanh(0.7978845608028654 * (x + 0.044715 * (x * x * x))))


def _log_sigmoid(z):
    return jnp.minimum(z, 0.0) - jnp.log(1.0 + jnp.exp(-jnp.abs(z)))


def _sel(mod, is_lat, idx):
    return jnp.where(is_lat, mod[1, idx:idx + 1, :], mod[0, idx:idx + 1, :])


def _rows_call(name, fn, grid, ins, outs, acc_axes=None, sem=None):
    n_in = len(ins)
    flags = [o[2] for o in outs]
    if acc_axes is None:
        acc_axes = (len(grid) - 1,)

    def body(*refs):
        ids = tuple(pl.program_id(a) for a in range(len(grid)))
        res = fn(ids, *[r[...] for r in refs[:n_in]])
        for r, v, acc in zip(refs[n_in:], res, flags):
            if acc:
                first = functools.reduce(jnp.logical_and, [ids[a] == 0 for a in acc_axes])

                @pl.when(first)
                def _():
                    r[...] = jnp.zeros_like(r)
                r[...] += v.astype(r.dtype)
            else:
                r[...] = v.astype(r.dtype)

    return pl.pallas_call(
        body, name=name, grid=grid, in_specs=[s for _, s in ins], out_specs=[o[1] for o in outs],
        out_shape=[o[0] for o in outs],
        compiler_params=_cp(sem if sem is not None else ("arbitrary",) * len(grid)),
    )(*[a for a, _ in ins])


def _sds(shape, dtype):
    return jax.ShapeDtypeStruct(shape, dtype)


def _rowspec(width, off=0, tm=TM):
    assert off % width == 0
    return pl.BlockSpec((tm, width), lambda i, o=off // width: (i, o))


def _full(shape):
    nd = len(shape)
    return pl.BlockSpec(shape, lambda *a: (0,) * nd)


def _mm(name, a, b, mode, out_dtype, tm_t=1056, tn_t=1408, tk_t=1408, chip_blocks=False, j_outer=False):
    halves = a.ndim == 3 or b.ndim == 3
    if mode == "nn":
        (m, k), (_, n) = a.shape, b.shape
    elif mode == "nt":
        (m, k), (n, _) = a.shape[-2:], b.shape
        k *= a.ndim - 1
    else:
        (k, m), (_, n) = a.shape, b.shape[-2:]
        n *= b.ndim - 1
    tm = _tile(m, tm_t, 8 if m % LANE else LANE)
    tn = _tile(n // 2 if halves and mode == "tn" else n, tn_t)
    tk = _tile(k // 2 if halves and mode == "nt" else k, tk_t)
    nk = k // tk
    if mode == "nn":
        dims, a_spec, b_spec = ((1,), (0,)), pl.BlockSpec((tm, tk), lambda i, j, l: (i, l)), pl.BlockSpec((tk, tn), lambda i, j, l: (l, j))
    elif mode == "nt":
        dims, a_spec, b_spec = ((1,), (1,)), pl.BlockSpec((tm, tk), lambda i, j, l: (i, l)), pl.BlockSpec((tn, tk), lambda i, j, l: (j, l))
        if halves:
            a_spec = pl.BlockSpec((None, tm, tk), lambda i, j, l, h=nk // 2: (l // h, i, l % h))
    else:
        dims, a_spec, b_spec = ((0,), (0,)), pl.BlockSpec((tk, tm), lambda i, j, l: (l, i)), pl.BlockSpec((tk, tn), lambda i, j, l: (l, j))
        if halves:
            b_spec = pl.BlockSpec((None, tk, tn), lambda i, j, l, h=n // tn // 2: (j // h, l, j % h))

    def body(a_ref, b_ref, o_ref, *scratch):
        l = pl.program_id(2)
        part = lax.dot_general(a_ref[...].astype(BF16), b_ref[...].astype(BF16), (dims, ((), ())),
                               preferred_element_type=F32)
        if nk == 1:
            o_ref[...] = part.astype(o_ref.dtype)
            return
        acc_ref = scratch[0]

        @pl.when(l == 0)
        def _():
            acc_ref[...] = part

        @pl.when(l > 0)
        def _():
            acc_ref[...] += part

        @pl.when(l == nk - 1)
        def _():
            o_ref[...] = acc_ref[...].astype(o_ref.dtype)

    o_spec, o_shape = pl.BlockSpec((tm, tn), lambda i, j, l: (i, j)), _sds((m, n), out_dtype)
    if chip_blocks:
        assert tn * N_CHIP == n and tm == m
        o_spec, o_shape = pl.BlockSpec((None, tm, tn), lambda i, j, l: (j, 0, 0)), _sds((N_CHIP, m, tn), out_dtype)
    grid = (m // tm, n // tn, nk)
    if j_outer:
        swap = lambda spec: pl.BlockSpec(spec.block_shape, lambda j, i, l, f=spec.index_map: f(i, j, l))
        a_spec, b_spec, o_spec, grid = swap(a_spec), swap(b_spec), swap(o_spec), (n // tn, m // tm, nk)
    return pl.pallas_call(
        body, name=name, grid=grid, in_specs=[a_spec, b_spec], out_specs=o_spec, out_shape=o_shape,
        scratch_shapes=[pltpu.VMEM((tm, tn), F32)] if nk > 1 else [],
        compiler_params=_cp(("parallel", "parallel", "arbitrary")),
    )(a, b)


def _nm_fn(is_lat, x, mod, g, shift, scale):
    return _rms(x, g) * (1.0 + _sel(mod, is_lat, scale)) + _sel(mod, is_lat, shift)


def _res_nm_fn(is_lat, x, br, modg, gate, mods, g, shift, scale):
    xn = x + _sel(modg, is_lat, gate) * br
    return xn, _nm_fn(is_lat, xn, mods, g, shift, scale)


def _nm_fwd(name, x, mod, g, shift, scale):
    t = x.shape[0]
    fn = lambda ids, xv, mv, gv: (_nm_fn(ids[0] >= NCB, xv, mv, gv, shift, scale),)
    return _rows_call(name, fn, (t // TM,), [(x, _rowspec(D)), (mod, _full((2, 6, D))), (g, _full((1, D)))],
                      [(_sds((t, D), BF16), _rowspec(D), False)])[0]


def _nm_bwd(name, x, mod, g, shift, scale, dx_res, dh):
    t = x.shape[0]

    def fn(ids, xv, mv, gv, dxr, dhv):
        _, vjp = jax.vjp(lambda a, b, c: _nm_fn(ids[0] >= NCB, a, b, c, shift, scale), xv, mv, gv)
        dx, dm, dg = vjp(dhv)
        return dx + dxr, dm, dg

    lat = pl.BlockSpec((TM, D), lambda i: (jnp.maximum(i - NCB, 0), 0))
    return _rows_call(name, fn, (t // TM,),
                      [(x, _rowspec(D)), (mod, _full((2, 6, D))), (g, _full((1, D))), (dx_res, _rowspec(D)), (dh, _rowspec(D))],
                      [(_sds((t - TC, D), F32), lat, False), (_sds((2, 6, D), F32), _full((2, 6, D)), True),
                       (_sds((1, D), F32), _full((1, D)), True)])


def _res_nm_fwd(name, x, br, modg, gate, mods, g, shift, scale):
    t = x.shape[0]
    fn = lambda ids, xv, bv, mg, ms, gv: _res_nm_fn(ids[0] >= NCB, xv, bv, mg, gate, ms, gv, shift, scale)
    return _rows_call(name, fn, (t // TM,),
                      [(x, _rowspec(D)), (br, _rowspec(D)), (modg, _full((2, 6, D))), (mods, _full((2, 6, D))), (g, _full((1, D)))],
                      [(_sds((t, D), F32), _rowspec(D), False), (_sds((t, D), BF16), _rowspec(D), False)])


def _res_nm_bwd(name, x, br, modg, gate, mods, g, shift, scale, dx_res, dh):
    t = x.shape[0]

    def fn(ids, xv, bv, mg, ms, gv, dxr, dhv):
        f = lambda a, b, c, d, e: _res_nm_fn(ids[0] >= NCB, a, b, c, gate, d, e, shift, scale)
        _, vjp = jax.vjp(f, xv, bv, mg, ms, gv)
        return vjp((dxr, dhv))

    m26 = (_sds((2, 6, D), F32), _full((2, 6, D)), True)
    return _rows_call(name, fn, (t // TM,),
                      [(x, _rowspec(D)), (br, _rowspec(D)), (modg, _full((2, 6, D))), (mods, _full((2, 6, D))), (g, _full((1, D))),
                       (dx_res, _rowspec(D)), (dh, _rowspec(D))],
                      [(_sds((t, D), F32), _rowspec(D), False), (_sds((t, D), BF16), _rowspec(D), False), m26, m26,
                       (_sds((1, D), F32), _full((1, D)), True)])


def _head(name, x_mid, f, mod, gf, tgt):
    t = x_mid.shape[0]

    def fn(ids, xv, fv, mv, gv, tv):
        def loss_fn(a, b, c, d):
            y = _rms(a + c[1, 5:6, :] * b, d)
            e = y - tv
            return 0.5 * jnp.sum(jnp.mean(e * e, axis=-1))
        loss, grads = jax.value_and_grad(loss_fn, argnums=(0, 1, 2, 3))(xv, fv, mv, gv)
        return tuple(jnp.where(ids[0] >= NCB, v, 0.0) for v in (jnp.reshape(loss, (1, 1)),) + grads)

    return _rows_call(name, fn, (t // TM,),
                      [(x_mid, _rowspec(D)), (f, _rowspec(D)), (mod, _full((2, 6, D))), (gf, _full((1, D))),
                       (tgt, pl.BlockSpec((TM, D), lambda i: (jnp.maximum(i - NCB, 0), 0)))],
                      [(_sds((1, 1), F32), _full((1, 1)), True), (_sds((t, D), F32), _rowspec(D), False),
                       (_sds((t, D), BF16), _rowspec(D), False), (_sds((2, 6, D), F32), _full((2, 6, D)), True),
                       (_sds((1, D), F32), _full((1, D)), True)])


def _gmlp_fn(u, v, g, ws, bst):
    rows = []
    u, v = u.astype(F32), v.astype(F32)
    for r in range(u.shape[0] // 128):
        uu, vv = _gelu(u[128 * r:128 * r + 128]), _gelu(v[128 * r:128 * r + 128])
        cols = []
        for gi in range(4):
            sl = slice(128 * gi, 128 * gi + 128)
            f = bdot(ws[gi], _rms(vv[:, sl], g[:, sl]), 1, 0) + bst[:, gi:gi + 1]
            cols.append(uu[:, sl] * f)
        rows.append(jnp.concatenate(cols, axis=-1))
    return jnp.concatenate(rows, axis=0)


def _gmlp_ins(p, g, ws, bst):
    return [(p, _rowspec(GW, OFF["gu"])), (p, _rowspec(GW, OFF["gv"])), (g, _full((1, GW))),
            (ws, _full((4, 128, 128))), (bst, _full((128, 4)))]


def _gmlp_fwd(name, p, g, ws, bst):
    t = p.shape[0]
    return _rows_call(name, lambda ids, *a: (_gmlp_fn(*a),), (t // TM,), _gmlp_ins(p, g, ws, bst),
                      [(_sds((t, GW), BF16), _rowspec(GW), False)])[0]


def _gmlp_bwd(name, p, g, ws, bst, dgm):
    t = p.shape[0]

    def fn(ids, u, v, gv, wv, bv, dv):
        _, vjp = jax.vjp(_gmlp_fn, u, v, gv, wv, bv)
        return vjp(dv)

    return _rows_call(name, fn, (t // TM,), _gmlp_ins(p, g, ws, bst) + [(dgm, _rowspec(GW))],
                      [(_sds((t, GW), BF16), _rowspec(GW), False), (_sds((t, GW), BF16), _rowspec(GW), False),
                       (_sds((1, GW), F32), _full((1, GW)), True), (_sds((4, 128, 128), F32), _full((4, 128, 128)), True),
                       (_sds((128, 4), F32), _full((128, 4)), True)])


def _qk_fn(q, k, gq, gk, cos, sin, seg, perm):
    cq, sq = jnp.concatenate([cos] * 4, axis=-1), jnp.concatenate([sin] * 4, axis=-1)
    q, k = q.astype(F32), k.astype(F32)
    qn = q * lax.rsqrt(hdot(q * q, seg) + EPS) * gq
    kn = k * lax.rsqrt(hdot(k * k, seg[:128, :128]) + EPS) * gk
    qr = qn * cq + hdot(qn, perm) * sq
    kr = kn * cos + hdot(kn, perm[:128, :128]) * sin
    return qr * (HD ** -0.5), kr


def _qk_ins(p, gq, gk, cos, sin, seg, perm):
    return [(p, _rowspec(512, OFF["q"])), (p, _rowspec(128, OFF["k"])), (gq, _full((1, 512))), (gk, _full((1, 128))),
            (cos, _rowspec(128)), (sin, _rowspec(128)), (seg, _full((512, 512))), (perm, _full((512, 512)))]


def _qk_fwd(name, p, gq, gk, cos, sin, seg, perm):
    t = p.shape[0]
    fn = lambda ids, q, k, a, b, c, s, sg, pm, v: _qk_fn(q, k, a, b, c, s, sg, pm) + (v,)
    return _rows_call(name, fn, (t // TM,), _qk_ins(p, gq, gk, cos, sin, seg, perm) + [(p, _rowspec(128, OFF["v"]))],
                      [(_sds((t, 512), BF16), _rowspec(512), False), (_sds((t, 128), BF16), _rowspec(128), False),
                       (_sds((t, 128), BF16), _rowspec(128), False)])


def _qk_bwd(name, p, gq, gk, cos, sin, seg, perm, dqr, dkr):
    t = p.shape[0]

    def fn(ids, q, k, a, b, c, s, sg, pm, dq, dk):
        _, vjp = jax.vjp(lambda q_, k_, a_, b_: _qk_fn(q_, k_, a_, b_, c, s, sg, pm), q, k, a, b)
        return vjp((dq, dk))

    return _rows_call(name, fn, (t // TM,),
                      _qk_ins(p, gq, gk, cos, sin, seg, perm) + [(dqr, _rowspec(512)), (dkr, _rowspec(128))],
                      [(_sds((t, 512), BF16), _rowspec(512), False), (_sds((t, 128), BF16), _rowspec(128), False),
                       (_sds((1, 512), F32), _full((1, 512)), True), (_sds((1, 128), F32), _full((1, 128)), True)])


_ATT_TQ = 1024
_ATT_TK = 768


def _attn_fwd(name, q, k, v, gather=()):
    h, tq_all, _ = q.shape
    hkv, tk_all, _ = k.shape
    tq, tk = _tile(tq_all, _ATT_TQ), _tile(tk_all, _ATT_TK)
    nkc = tk_all // tk
    ng, nq = len(gather), tq_all // tq

    def body(*refs):
        q_ref, k_ref, v_ref = refs[:3]
        o_ref, lse_ref = refs[3 + ng:5 + ng]
        if ng:
            g_id, i_id = pl.program_id(0), pl.program_id(1)
            stage = lambda st: _gather_stage(st, refs[3:3 + ng], refs[5 + ng:5 + 2 * ng], *refs[5 + 2 * ng:])
            pl.when(jnp.logical_and(g_id == 0, i_id == 0))(lambda: stage(0))
            pl.when(jnp.logical_and(g_id == hkv - 1, i_id == 0))(lambda: stage(1))
        qv = q_ref[...].reshape(QG * tq, HD)

        def step(j, carry):
            m, acc = carry
            off = pl.multiple_of(j * tk, tk)
            kk, vv = k_ref[0, pl.ds(off, tk), :], v_ref[0, pl.ds(off, tk), :]
            s = lax.dot_general(qv, kk, (((1,), (1,)), ((), ())), preferred_element_type=F32)
            m_new = jnp.maximum(m, jnp.max(s, axis=-1, keepdims=True))
            pr = jnp.exp(s - m_new)
            acc = jnp.exp(m - m_new) * acc + jnp.dot(pr.astype(BF16), vv, preferred_element_type=F32)
            return m_new, acc

        init = (jnp.full((QG * tq, 1), -jnp.inf, F32), jnp.zeros((QG * tq, 2 * HD), F32))
        m, acc = lax.fori_loop(0, nkc, step, init)
        l = acc[:, HD:HD + 1]
        o_ref[...] = (acc[:, :HD] / l).reshape(QG, tq, HD)
        lse_ref[...] = (m + jnp.log(l)).reshape(QG, tq, 1)
        if ng:
            pl.when(jnp.logical_and(g_id == hkv - 1, i_id == nq - 1))(lambda: stage(2))

    kv_spec = pl.BlockSpec((1, tk_all, HD), lambda g, i: (g, 0, 0))
    v1_spec = pl.BlockSpec((1, tk_all, 2 * HD), lambda g, i: (g, 0, 0))
    qspec = pl.BlockSpec((QG, tq, HD), lambda g, i: (g, i, 0))
    sems = [pltpu.SemaphoreType.DMA((6 * ng,)), pltpu.SemaphoreType.DMA((6 * ng,))] if ng else []
    return pl.pallas_call(
        body, name=name, grid=(hkv, nq), in_specs=[qspec, kv_spec, v1_spec] + [_ANY] * ng,
        out_specs=[qspec, pl.BlockSpec((QG, tq, 1), lambda g, i: (g, i, 0))] + [_ANY] * ng,
        out_shape=[_sds((h, tq_all, HD), F32), _sds((h, tq_all, 1), F32)]
        + [_sds((N_CHIP,) + a.shape, a.dtype) for a in gather],
        scratch_shapes=sems, compiler_params=_cp(("arbitrary", "arbitrary")),
    )(q, k, v, *gather)


def _attn_bwd(name, q, k, v, o, do, lse_row):
    h, tq_all, _ = q.shape
    hkv, tk_all, _ = k.shape
    tq, tk = _tile(tq_all, 1024), _tile(tk_all, 1408)

    def body(q_ref, k_ref, v_ref, o_ref, do_ref, lse_ref, dq_ref, dk_ref, dv_ref, dl_ref):
        i, j = pl.program_id(1), pl.program_id(2)

        @pl.when(j == 0)
        def _():
            ones = jnp.ones((8, HD), F32)
            for g in range(QG):
                dl_ref[g] = hdot(ones, do_ref[g] * o_ref[g], 1, 1)

        kk, vv = k_ref[0], v_ref[0]
        dk_acc, dv_acc = jnp.zeros((tk, HD), F32), jnp.zeros((tk, HD), F32)
        for g in range(QG):
            qv, dob = q_ref[g], do_ref[g].astype(BF16)
            st = lax.dot_general(kk, qv, (((1,), (1,)), ((), ())), preferred_element_type=F32)
            pt = jnp.exp(st - lse_ref[g])
            dv_acc += jnp.dot(pt.astype(BF16), dob, preferred_element_type=F32)
            dpt = lax.dot_general(vv, dob, (((1,), (1,)), ((), ())), preferred_element_type=F32)
            dst = (pt * (dpt - dl_ref[g, 0:1, :])).astype(BF16)
            dk_acc += jnp.dot(dst, qv, preferred_element_type=F32)
            dq_part = lax.dot_general(dst, kk, (((0,), (0,)), ((), ())), preferred_element_type=F32)

            @pl.when(j == 0)
            def _():
                dq_ref[g] = dq_part

            @pl.when(j > 0)
            def _():
                dq_ref[g] += dq_part

        rows = pl.ds(pl.multiple_of(j * tk, tk), tk)

        @pl.when(i == 0)
        def _():
            dk_ref[0, rows, :] = dk_acc
            dv_ref[0, rows, :] = dv_acc

        @pl.when(i > 0)
        def _():
            dk_ref[0, rows, :] += dk_acc
            dv_ref[0, rows, :] += dv_acc

    ks = pl.BlockSpec((1, tk, HD), lambda g, i, j: (g, j, 0))
    qs = pl.BlockSpec((QG, tq, HD), lambda g, i, j: (g, i, 0))
    rs = pl.BlockSpec((QG, 1, tq), lambda g, i, j: (g, 0, i))
    full = pl.BlockSpec((1, tk_all, HD), lambda g, i, j: (g, 0, 0))
    return pl.pallas_call(
        body, name=name, grid=(hkv, tq_all // tq, tk_all // tk), in_specs=[qs, ks, ks, qs, qs, rs], out_specs=[qs, full, full],
        out_shape=[_sds((h, tq_all, HD), F32), _sds((hkv, tk_all, HD), F32), _sds((hkv, tk_all, HD), F32)],
        scratch_shapes=[pltpu.VMEM((QG, 8, tq), F32)],
        compiler_params=_cp(("parallel", "arbitrary", "arbitrary")),
    )(q, k, v, o, do, lse_row)


def _decay_fn(a, w2, b2):
    return _log_sigmoid(bdot(a, w2, 1, 0) + b2) / GLA_TAU


def _decay_fwd(name, p, w2, b2):
    t = p.shape[0]
    return _rows_call(name, lambda ids, a, w, b: (_decay_fn(a, w, b),), (t // TM,),
                      [(p, _rowspec(128, OFF["ab"])), (w2, _full((128, 512))), (b2, _full((1, 512)))],
                      [(_sds((t, 512), F32), _rowspec(512), False)])[0]


def _decay_bwd(name, p, w2, b2, gf, gb):
    t = p.shape[0]

    def fn(ids, a, w, b, qf, kf, vf, lf, qb, kb, vb, lb):
        _, vjp = jax.vjp(_decay_fn, a, w, b)
        return vjp(jnp.concatenate([lf, lb], axis=-1)) + (qf + qb, kf + kb, vf + vb)

    widths = (256, 256, 512, 256)
    return _rows_call(name, fn, (t // TM,),
                      [(p, _rowspec(128, OFF["ab"])), (w2, _full((128, 512))), (b2, _full((1, 512)))]
                      + [(g, _rowspec(w)) for g, w in zip(gf, widths)] + [(g, _rowspec(w)) for g, w in zip(gb, widths)],
                      [(_sds((t, 128), BF16), _rowspec(128), False), (_sds((128, 512), F32), _full((128, 512)), True),
                       (_sds((1, 512), F32), _full((1, 512)), True)]
                      + [(_sds((t, w), BF16), _rowspec(w), False) for w in widths[:3]])


def _gla_consts(reverse):
    r = lax.broadcasted_iota(jnp.int32, (GLA_CHUNK, GLA_CHUNK), 0)
    c = lax.broadcasted_iota(jnp.int32, (GLA_CHUNK, GLA_CHUNK), 1)
    trib = (r <= c) if reverse else (r >= c)
    br = lax.broadcasted_iota(jnp.int32, (GLA_QK, GLA_V), 0) // GLA_DK
    bc = lax.broadcasted_iota(jnp.int32, (GLA_QK, GLA_V), 1) // GLA_DV
    lane_head = lax.broadcasted_iota(jnp.int32, (1, GLA_QK), 1) // GLA_DK
    return trib, (br == bc).astype(F32), lane_head


def _gla_chunk(q, k, v, la, s_in, consts):
    trib, bd, lane_head = consts
    q, k = q.astype(F32), k.astype(F32)
    cum = hdot(trib.astype(F32), la)
    tot = jnp.sum(la, axis=0, keepdims=True)
    q_in = q * (GLA_DK ** -0.5) * jnp.exp(cum)
    k_in = k * jnp.exp(-cum)
    k_st = k * jnp.exp(tot - cum)
    outs = []
    for h in range(GLA_H):
        att = bdot(jnp.where(lane_head == h, q_in, 0.0), k_in, 1, 1)
        att = jnp.where(trib, att, 0.0)
        outs.append(bdot(att, v[:, GLA_DV * h:GLA_DV * (h + 1)], 1, 0))
    o = jnp.concatenate(outs, axis=-1) + bdot(q_in, s_in, 1, 0)
    decay = jnp.exp(hdot(la, jnp.ones((GLA_CHUNK, LANE), F32), 0, 0))
    s_out = jnp.concatenate([decay] * (GLA_V // LANE), axis=-1) * s_in + bdot(k_st, v, 0, 0) * bd
    return o, s_out


def _gla_order(nb, reverse, backward):
    if not reverse:
        return (lambda s: nb - 1 - s) if backward else (lambda s: s)
    if backward:
        return lambda s: jnp.where(s == nb - 1, 0, s + 1)
    return lambda s: jnp.where(s == 0, 0, nb - s)


_NCH = TM // GLA_CHUNK


def _gla_specs(nb, reverse, backward):
    order = _gla_order(nb, reverse, backward)
    col = lambda width, off: pl.BlockSpec((TM, width), lambda s, o=off // width: (order(s), o))
    state = pl.BlockSpec((_NCH, GLA_H, GLA_DK, GLA_DV), lambda s: (order(s), 0, 0, 0))
    qkvla = [col(256, OFF["glq"]), col(256, OFF["glk"]), col(512, OFF["glv"]), col(256, 256 * int(reverse))]
    return col, state, qkvla


def _gla_fwd(name, p, la):
    t = p.shape[0]
    nb = t // TM

    def body(*refs):
        ins, outs, scr = (refs[0:4], refs[4:8]), (refs[8:10], refs[10:12]), refs[12:14]

        @pl.when(pl.program_id(0) == 0)
        def _():
            for s_ref in scr:
                s_ref[...] = jnp.zeros_like(s_ref)

        for step in range(_NCH):
            for d in range(2):
                (q_ref, k_ref, v_ref, la_ref), (o_ref, sv_ref), s_ref = ins[d], outs[d], scr[d]
                c = _NCH - 1 - step if d else step
                rows = slice(GLA_CHUNK * c, GLA_CHUNK * (c + 1))
                s_in = s_ref[...]
                for h in range(GLA_H):
                    sv_ref[c, h] = s_in[GLA_DK * h:GLA_DK * (h + 1), GLA_DV * h:GLA_DV * (h + 1)]
                o, s_out = _gla_chunk(q_ref[rows, :], k_ref[rows, :], v_ref[rows, :], la_ref[rows, :], s_in,
                                      _gla_consts(bool(d)))
                o_ref[rows, :] = o
                s_ref[...] = s_out

    in_specs, out_specs, out_shape = [], [], []
    for d in range(2):
        col, state, qkvla = _gla_specs(nb, bool(d), False)
        in_specs += qkvla
        out_specs += [col(512, 0), state]
        out_shape += [_sds((t, GLA_V), F32), _sds((t // GLA_CHUNK, GLA_H, GLA_DK, GLA_DV), F32)]
    return pl.pallas_call(
        body, name=name, grid=(nb,), in_specs=in_specs, out_specs=out_specs, out_shape=out_shape,
        scratch_shapes=[pltpu.VMEM((GLA_QK, GLA_V), F32)] * 2, compiler_params=_cp(("arbitrary",)),
    )(p, p, p, la, p, p, p, la)


def _gla_bwd(name, p, la, sv_f, sv_b, do):
    t = p.shape[0]
    nb = t // TM

    def body(*refs):
        ins, outs, scr = (refs[0:6], refs[6:12]), (refs[12:16], refs[16:20]), refs[20:22]

        @pl.when(pl.program_id(0) == 0)
        def _():
            for ds_ref in scr:
                ds_ref[...] = jnp.zeros_like(ds_ref)

        zero = jnp.zeros((GLA_DK, GLA_DV), F32)
        for step in range(_NCH):
            for d in range(2):
                (q_ref, k_ref, v_ref, la_ref, sv_ref, do_ref), out_refs, ds_ref = ins[d], outs[d], scr[d]
                c = step if d else _NCH - 1 - step
                rows = slice(GLA_CHUNK * c, GLA_CHUNK * (c + 1))
                s_in = jnp.concatenate(
                    [jnp.concatenate([sv_ref[c, h] if hh == h else zero for hh in range(GLA_H)], axis=-1)
                     for h in range(GLA_H)], axis=0)
                consts = _gla_consts(bool(d))
                _, vjp = jax.vjp(lambda a, b, cc, dd, e: _gla_chunk(a, b, cc, dd, e, consts),
                                 q_ref[rows, :], k_ref[rows, :], v_ref[rows, :], la_ref[rows, :], s_in)
                grads = vjp((do_ref[rows, :], ds_ref[...]))
                for o_ref, g in zip(out_refs, grads[:4]):
                    o_ref[rows, :] = g.astype(o_ref.dtype)
                ds_ref[...] = grads[4]

    ins, in_specs, out_specs, out_shape = [], [], [], []
    for d, sv in enumerate((sv_f, sv_b)):
        col, state, qkvla = _gla_specs(nb, bool(d), True)
        ins += [p, p, p, la, sv, do]
        in_specs += qkvla + [state, col(512, 0)]
        out_specs += [col(256, 0), col(256, 0), col(512, 0), col(256, 0)]
        out_shape += [_sds((t, GLA_QK), F32), _sds((t, GLA_QK), F32), _sds((t, GLA_V), F32), _sds((t, GLA_QK), F32)]
    return pl.pallas_call(
        body, name=name, grid=(nb,), in_specs=in_specs, out_specs=out_specs, out_shape=out_shape,
        scratch_shapes=[pltpu.VMEM((GLA_QK, GLA_V), F32)] * 2, compiler_params=_cp(("arbitrary",)),
    )(*ins)


def _gla_out_fn(of, ob, r, g):
    o = of + ob
    cols = [_rms(o[:, GLA_DV * h:GLA_DV * (h + 1)], g[:, GLA_DV * h:GLA_DV * (h + 1)]) for h in range(GLA_H)]
    return jnp.concatenate(cols, axis=-1) * jax.nn.silu(r.astype(F32))


def _gla_out_fwd(name, of, ob, p, g):
    t = p.shape[0]
    return _rows_call(name, lambda ids, *a: (_gla_out_fn(*a),), (t // TM,),
                      [(of, _rowspec(512)), (ob, _rowspec(512)), (p, _rowspec(512, OFF["gr"])), (g, _full((1, 512)))],
                      [(_sds((t, 512), BF16), _rowspec(512), False)])[0]


def _gla_out_bwd(name, of, ob, p, g, dgla):
    t = p.shape[0]

    def fn(ids, a, b, r, gv, dv):
        _, vjp = jax.vjp(_gla_out_fn, a, b, r, gv)
        do, _, dr, dg = vjp(dv)
        return do, dr, dg

    return _rows_call(name, fn, (t // TM,),
                      [(of, _rowspec(512)), (ob, _rowspec(512)), (p, _rowspec(512, OFF["gr"])), (g, _full((1, 512))),
                       (dgla, _rowspec(512))],
                      [(_sds((t, 512), F32), _rowspec(512), False), (_sds((t, 512), BF16), _rowspec(512), False),
                       (_sds((1, 512), F32), _full((1, 512)), True)])


_TMM = 384


def _merge_fwd(name, gm, att, gla, wa, wb, wc, p):
    t = p.shape[0]
    row = lambda w, off=0: pl.BlockSpec((_TMM, w), lambda i, o=off // w: (i, o))

    def fn(ids, a, b, c, wa_, wb_, wc_, ga, gb, gc):
        ga, gb, gc = ga.astype(F32), gb.astype(F32), gc.astype(F32)
        return (jax.nn.sigmoid(ga) * bdot(a, wa_, 1, 0) + jax.nn.sigmoid(gb) * bdot(b, wb_, 1, 0)
                + jax.nn.sigmoid(gc) * bdot(c, wc_, 1, 0),)

    return _rows_call(name, fn, (t // _TMM,),
                      [(gm, row(512)), (att, row(512)), (gla, row(512)), (wa, _full((512, D))), (wb, _full((512, D))),
                       (wc, _full((512, D))), (p, row(D, OFF["gA"])), (p, row(D, OFF["gB"])), (p, row(D, OFF["gC"]))],
                      [(_sds((t, D), BF16), row(D), False)])[0]


def _merge_bwd(name, gm, att, gla, wa, wb, wc, p, dmerged):
    t = p.shape[0]
    row = lambda w, off=0: pl.BlockSpec((_TMM, w), lambda i, o=off // w: (i, o))

    def fn(ids, a, b, c, wa_, wb_, wc_, ga, gb, gc, dm):
        ga, gb, gc = ga.astype(F32), gb.astype(F32), gc.astype(F32)
        outs_y, outs_g = [], []
        for br, w, g in ((a, wa_, ga), (b, wb_, gb), (c, wc_, gc)):
            s = jax.nn.sigmoid(g)
            outs_y.append(dm * s)
            outs_g.append(dm * bdot(br, w, 1, 0) * s * (1.0 - s))
        return tuple(outs_y) + tuple(outs_g)

    o = (_sds((t, D), BF16), row(D), False)
    return _rows_call(name, fn, (t // _TMM,),
                      [(gm, row(512)), (att, row(512)), (gla, row(512)), (wa, _full((512, D))), (wb, _full((512, D))),
                       (wc, _full((512, D))), (p, row(D, OFF["gA"])), (p, row(D, OFF["gB"])), (p, row(D, OFF["gC"])),
                       (dmerged, row(D))], [o] * 6)


_TNC = 1408
_NJ = FFN // _TNC


HALO = 16


def _shift_rows(x, prev, nxt, vp, vn):
    n = x.shape[0]
    rid = lax.broadcasted_iota(jnp.int32, x.shape, 0)
    xp = jnp.where(rid == 0, jnp.where(vp, prev[HALO - 1:HALO, :], 0.0), pltpu.roll(x, 1, 0))
    xn = jnp.where(rid == n - 1, jnp.where(vn, nxt[0:1, :], 0.0), pltpu.roll(x, n - 1, 0))
    return xp, xn


def _seq_edges(i, t):
    start, end = i * TM, (i + 1) * TM
    return jnp.logical_and(start != 0, start != TC), jnp.logical_and(end != TC, end != t)


def _halo_specs(t, colmap):
    r = TM // HALO
    main = pl.BlockSpec((TM, _TNC), lambda j, i: (i, colmap(j)))
    prev = pl.BlockSpec((HALO, _TNC), lambda j, i: (jnp.maximum(i * r - 1, 0), colmap(j)))
    nxt = pl.BlockSpec((HALO, _TNC), lambda j, i: (jnp.minimum((i + 1) * r, t // HALO - 1), colmap(j)))
    return [main, prev, nxt]


def _conv3(x, xp, xn, w, b=None):
    y = xp * w[0:1, :] + x * w[1:2, :] + xn * w[2:3, :]
    return y if b is None else b + y


def _conv_fwd(name, a, cw, cb):
    t = a.shape[0]

    def fn(ids, ag, agp, agn, av, avp, avn, wg, wv, bg, bv):
        vp, vn = _seq_edges(ids[1], t)
        ag, agp, agn, av, avp, avn = (z.astype(F32) for z in (ag, agp, agn, av, avp, avn))
        cg = _conv3(ag, *_shift_rows(ag, agp, agn, vp, vn), wg, bg)
        cv = _conv3(av, *_shift_rows(av, avp, avn, vp, vn), wv, bv)
        return (jax.nn.silu(cg) * cv,)

    gcol, vcol = (lambda j: j), (lambda j: j + _NJ)
    wspec = lambda cm: pl.BlockSpec((3, _TNC), lambda j, i: (0, cm(j)))
    bspec = lambda cm: pl.BlockSpec((1, _TNC), lambda j, i: (0, cm(j)))
    ins = [(a, s) for s in _halo_specs(t, gcol) + _halo_specs(t, vcol)]
    ins += [(cw, wspec(gcol)), (cw, wspec(vcol)), (cb, bspec(gcol)), (cb, bspec(vcol))]
    return _rows_call(name, fn, (_NJ, t // TM), ins,
                      [(_sds((t, FFN), BF16), pl.BlockSpec((TM, _TNC), lambda j, i: (i, j)), False)])[0]


def _conv_bwd(name, a, cw, cb, dact):
    t = a.shape[0]
    n = TM + 2 * HALO

    def fn(ids, ag, agp, agn, av, avp, avn, dv, dvp, dvn, wg, wv, bg, bv):
        vp, vn = _seq_edges(ids[1], t)
        ag, agp, agn, av, avp, avn = (z.astype(F32) for z in (ag, agp, agn, av, avp, avn))
        ext = lambda x, xp, xn: jnp.concatenate([jnp.where(vp, xp, 0.0), x, jnp.where(vn, xn, 0.0)], axis=0)
        up, dn = (lambda x: pltpu.roll(x, 1, 0)), (lambda x: pltpu.roll(x, n - 1, 0))
        main = lambda y: y[HALO:HALO + TM]
        eg, ev, ed = ext(ag, agp, agn), ext(av, avp, avn), ext(dv, dvp, dvn)
        cg = _conv3(eg, up(eg), dn(eg), wg, bg)
        cv = _conv3(ev, up(ev), dn(ev), wv, bv)
        s = jax.nn.sigmoid(cg)
        rid = lax.broadcasted_iota(jnp.int32, (3, eg.shape[1]), 0)
        das, dws, dbs = [], [], []
        for dc, w, x in ((ed * cv * s * (1.0 + cg * (1.0 - s)), wg, eg), (ed * cg * s, wv, ev)):
            shifted = [main(dn(dc)), main(dc), main(up(dc))]
            das.append(shifted[0] * w[0:1, :] + shifted[1] * w[1:2, :] + shifted[2] * w[2:3, :])
            sums = [jnp.sum(y * main(x), axis=0, keepdims=True) for y in shifted]
            dws.append(jnp.where(rid == 0, sums[0], jnp.where(rid == 1, sums[1], sums[2])))
            dbs.append(jnp.sum(shifted[1], axis=0, keepdims=True))
        return jnp.stack(das), jnp.stack(dws), jnp.stack(dbs)

    gcol, vcol = (lambda j: j), (lambda j: j + _NJ)
    wspec = lambda cm: pl.BlockSpec((3, _TNC), lambda j, i: (0, cm(j)))
    bspec = lambda cm: pl.BlockSpec((1, _TNC), lambda j, i: (0, cm(j)))
    ins = [(a, s) for s in _halo_specs(t, gcol) + _halo_specs(t, vcol)] + [(dact, s) for s in _halo_specs(t, gcol)]
    ins += [(cw, wspec(gcol)), (cw, wspec(vcol)), (cb, bspec(gcol)), (cb, bspec(vcol))]
    return _rows_call(name, fn, (_NJ, t // TM), ins,
                      [(_sds((2, t, FFN), BF16), pl.BlockSpec((2, TM, _TNC), lambda j, i: (0, i, j)), False),
                       (_sds((2, 3, FFN), F32), pl.BlockSpec((2, 3, _TNC), lambda j, i: (0, 0, j)), True),
                       (_sds((2, 1, FFN), F32), pl.BlockSpec((2, 1, _TNC), lambda j, i: (0, 0, j)), True)])


_TNA = 512


def _adaln_fwd(name, cond, w, b):
    fn = lambda ids, cv, wv, bv: ((bdot(jax.nn.silu(cv), wv[0], 1, 0) + bv[0])[None],)
    return _rows_call(name, fn, (2, ADA_LOC // _TNA),
                      [(cond, _full((16, D))), (w, pl.BlockSpec((1, D, _TNA), lambda l, j: (l, 0, j))),
                       (b, pl.BlockSpec((1, 1, _TNA), lambda l, j: (l, 0, j)))],
                      [(_sds((2, 16, ADA_LOC), F32), pl.BlockSpec((1, 16, _TNA), lambda l, j: (l, 0, j)), False)])[0]


def _adaln_bwd(name, c8, cc8, w, dl, dc):
    def fn(ids, cv, ccv, wv, dlv, dcv):
        dcs = jnp.broadcast_to(jnp.sum(dcv[0], axis=0, keepdims=True), dcv[0].shape)
        dw = hdot(jax.nn.silu(cv), dlv[0], 0, 0) + hdot(jax.nn.silu(ccv), dcs, 0, 0)
        s = jax.nn.sigmoid(ccv)
        rid = lax.broadcasted_iota(jnp.int32, ccv.shape, 0)
        dcc = jnp.where(rid == 0, bdot(dcs, wv[0], 1, 1) * s * (1.0 + ccv * (1.0 - s)), 0.0)
        return dw[None], dcc

    dspec = pl.BlockSpec((1, 8, _TNA), lambda l, j: (l, 0, j))
    return _rows_call(name, fn, (2, ADA_LOC // _TNA),
                      [(c8, _full((8, D))), (cc8, _full((8, D))), (w, pl.BlockSpec((1, D, _TNA), lambda l, j: (l, 0, j))),
                       (dl, dspec), (dc, dspec)],
                      [(_sds((2, D, ADA_LOC), F32), pl.BlockSpec((1, D, _TNA), lambda l, j: (l, 0, j)), False),
                       (_sds((8, D), F32), _full((8, D)), True)], acc_axes=(0, 1))


def _adamw_fn(w, g, m, v):
    m = ADAM_B1 * m + (1.0 - ADAM_B1) * g
    v = ADAM_B2 * v + (1.0 - ADAM_B2) * (g * g)
    m_hat = m / (1.0 - ADAM_B1 ** ADAM_STEP)
    v_hat = v / (1.0 - ADAM_B2 ** ADAM_STEP)
    return -ADAM_LR * (m_hat / (jnp.sqrt(v_hat) + ADAM_EPS) + ADAM_WD * w), m, v


def _adamw(name, w, g, m, v):
    l, r, c = w.shape
    tr = _tile(r, max(8, (1 << 20) // (4 * c)), 8)
    spec = pl.BlockSpec((None, tr, c), lambda i, j: (i, j, 0))
    o = (_sds((l, r, c), F32), spec, False)
    return _rows_call(name, lambda ids, *a: _adamw_fn(*a), (l, r // tr), [(x, spec) for x in (w, g, m, v)], [o, o, o],
                      sem=("parallel", "parallel"))


def _coords():
    return lax.axis_index("x"), lax.axis_index("y"), lax.axis_index("c")


def _other_chips(x, y):
    return [(1 - x, y), (x, 1 - y), (1 - x, 1 - y)]


def _allgather_small(name, blk):
    m_per, n = blk.shape

    def body(x_ref, out_ref, send_sems, recv_sems, local_sem):
        x, y, c = _coords()
        me, sibling = (x, y, c), (x, y, 1 - c)
        chips = _other_chips(x, y)

        def rows(px, py, pc):
            return out_ref.at[pl.ds((4 * px + 2 * py + pc) * m_per, m_per), :]

        def copy(k, block, to, src=None):
            return pltpu.make_async_remote_copy(
                src_ref=rows(*block) if src is None else src, dst_ref=rows(*block), send_sem=send_sems.at[k],
                recv_sem=recv_sems.at[k], device_id=to, device_id_type=MESH)

        mine = pltpu.make_async_copy(x_ref, rows(*me), local_sem)
        mine.start()
        first = [copy(0, me, sibling, src=x_ref)]
        first += [copy(1 + j, me, (*chip, c), src=x_ref) for j, chip in enumerate(chips)]
        for cp in first:
            cp.start()
        passed = [copy(4 + j, (*chip, c), sibling) for j, chip in enumerate(chips)]
        for j, chip in enumerate(chips):
            copy(1 + j, (*chip, c), me).wait_recv()
            passed[j].start()
        copy(0, sibling, me).wait_recv()
        for j, chip in enumerate(chips):
            copy(4 + j, (*chip, 1 - c), me).wait_recv()
        for cp in first + passed:
            cp.wait_send()
        mine.wait()

    return pl.pallas_call(
        body, name=name, out_shape=_sds((N_DEV * m_per, n), blk.dtype),
        in_specs=[pl.BlockSpec(memory_space=pltpu.VMEM)], out_specs=pl.BlockSpec(memory_space=pltpu.VMEM),
        scratch_shapes=[pltpu.SemaphoreType.DMA((7,)), pltpu.SemaphoreType.DMA((7,)), pltpu.SemaphoreType.DMA],
        compiler_params=pltpu.CompilerParams(vmem_limit_bytes=VMEM_LIMIT),
    )(blk)


_ANY = pl.BlockSpec(memory_space=pl.ANY)


def _remote(src, dst, send_sems, recv_sems, s, to):
    return pltpu.make_async_remote_copy(src_ref=src, dst_ref=dst, send_sem=send_sems.at[s], recv_sem=recv_sems.at[s],
                                        device_id=to, device_id_type=MESH)


def _comm_call(name, body, ins, out_shapes, n_sems, n_local):
    return pl.pallas_call(
        body, name=name, out_shape=out_shapes, in_specs=[_ANY] * len(ins), out_specs=[_ANY] * len(out_shapes),
        scratch_shapes=[pltpu.SemaphoreType.DMA((n_sems,)), pltpu.SemaphoreType.DMA((n_sems,)),
                        pltpu.SemaphoreType.DMA((n_local,))],
    )(*ins)


def _gather_stage(stage, ins, outs, send_sems, recv_sems):
    n = len(ins)
    x, y, c = _coords()
    k = 2 * x + y
    sibling = (x, y, 1 - c)
    chips = _other_chips(x, y)
    first = [_remote(ins[t].at[c], outs[t].at[k, c], send_sems, recv_sems, 6 * t + j, (*chip, c))
             for t in range(n) for j, chip in enumerate(chips)]
    there = lambda t, j, half: outs[t].at[2 * chips[j][0] + chips[j][1], half]
    passed = [_remote(there(t, j, c), there(t, j, c), send_sems, recv_sems, 6 * t + 3 + j, sibling)
              for t in range(n) for j in range(3)]
    if stage == 0:
        for cp in first:
            cp.start()
    elif stage == 1:
        for t in range(n):
            for j in range(3):
                _remote(there(t, j, c), there(t, j, c), send_sems, recv_sems, 6 * t + j, sibling).wait_recv()
                passed[3 * t + j].start()
    else:
        for t in range(n):
            for j in range(3):
                _remote(there(t, j, 1 - c), there(t, j, 1 - c), send_sems, recv_sems, 6 * t + 3 + j, sibling).wait_recv()
        for cp in first + passed:
            cp.wait_send()


def _gather_own(outs, locs):
    k = 2 * lax.axis_index("x") + lax.axis_index("y")
    return [lax.dynamic_update_slice_in_dim(o, a[None], k, axis=0) for o, a in zip(outs, locs)]


def _allgather_layers(name, locs):
    n = len(locs)

    def body(*refs):
        ins, outs, (send_sems, recv_sems, _) = refs[:n], refs[n:2 * n], refs[2 * n:]
        for stage in range(3):
            _gather_stage(stage, ins, outs, send_sems, recv_sems)

    return _gather_own(_comm_call(name, body, locs, [_sds((N_CHIP,) + a.shape, a.dtype) for a in locs], 6 * n, 1), locs)


def _rs_pair_exchange(name, g0, g1):
    n = len(g0)

    def body(*refs):
        a0, a1, outs, (send_sems, recv_sems, _) = refs[:n], refs[n:2 * n], refs[2 * n:3 * n], refs[3 * n:]
        x, y, c = _coords()

        def run(srcs):
            cps = [_remote(srcs[t], outs[t], send_sems, recv_sems, t, (x, y, 1 - c)) for t in range(n)]
            for cp in cps:
                cp.start()
            for cp in cps:
                cp.wait()

        pl.when(c == 0)(lambda: run(a1))
        pl.when(c == 1)(lambda: run(a0))

    return _comm_call(name, body, list(g0) + list(g1), [_sds(a.shape, a.dtype) for a in g0], n, 1)


def _ew2d(name, fn, ins, out_dtype):
    shape = ins[0].shape
    r, c = _prod(shape[:-1]), shape[-1]
    tr = _tile(r, max(8, (1 << 20) // (4 * c)), 8)
    spec = pl.BlockSpec((tr, c), lambda i: (i, 0))
    out = _rows_call(name, lambda ids, *a: (fn(*a),), (r // tr,), [(a.reshape(r, c), spec) for a in ins],
                     [(_sds((r, c), out_dtype), spec, False)], sem=("parallel",))[0]
    return out.reshape(shape)


def _rs_chip_exchange(name, s1):
    n = len(s1)

    def body(*refs):
        ins, outs, (send_sems, recv_sems, local_sems) = refs[:n], refs[n:2 * n], refs[2 * n:]
        x, y, c = _coords()
        k = 2 * x + y
        chips = _other_chips(x, y)
        cps = [_remote(ins[t].at[2 * cx + cy], outs[t].at[k], send_sems, recv_sems, 3 * t + j, (cx, cy, c))
               for t in range(n) for j, (cx, cy) in enumerate(chips)]
        for cp in cps:
            cp.start()
        for t in range(n):
            for j, (cx, cy) in enumerate(chips):
                there = outs[t].at[2 * cx + cy]
                _remote(there, there, send_sems, recv_sems, 3 * t + j, (cx, cy, c)).wait_recv()
        for cp in cps:
            cp.wait_send()

    outs = _comm_call(name, body, s1, [_sds(a.shape, a.dtype) for a in s1], 3 * n, 1)
    k = 2 * lax.axis_index("x") + lax.axis_index("y")
    own = [lax.dynamic_index_in_dim(a, k, axis=0, keepdims=True) for a in s1]
    return [lax.dynamic_update_slice_in_dim(o, a, k, axis=0) for o, a in zip(outs, own)]


def _sum_slots(name, a):
    s, r, cdim = a.shape
    tr = _tile(r, 512, 8)

    def fn(ids, av):
        tot = av[0]
        for i in range(1, s):
            tot = tot + av[i]
        return (tot,)

    return _rows_call(name, fn, (r // tr,), [(a, pl.BlockSpec((s, tr, cdim), lambda i: (0, i, 0)))],
                      [(_sds((r, cdim), F32), pl.BlockSpec((tr, cdim), lambda i: (i, 0)), False)], sem=("parallel",))[0]


def _pair_allgather(name, halves):
    n = len(halves)

    def body(*refs):
        ins, outs, (send_sems, recv_sems, local_sems) = refs[:n], refs[n:2 * n], refs[2 * n:]
        x, y, c = _coords()
        cps = [_remote(ins[t], outs[t].at[c], send_sems, recv_sems, t, (x, y, 1 - c)) for t in range(n)]
        for cp in cps:
            cp.start()
        for t in range(n):
            _remote(ins[t], outs[t].at[1 - c], send_sems, recv_sems, t, (x, y, 1 - c)).wait_recv()
        for cp in cps:
            cp.wait_send()

    outs = _comm_call(name, body, halves, [_sds((2,) + a.shape, a.dtype) for a in halves], n, 1)
    return [lax.dynamic_update_slice_in_dim(o, a[None], lax.axis_index("c"), axis=0) for o, a in zip(outs, halves)]


def _reduce_scatter(g0, g1):
    n = len(g0)
    got = _rs_pair_exchange("rs_pair_exchange", g0, g1)
    keep = lambda a, b, r: jnp.where(lax.axis_index("c") == 0, a, b) + r
    s1 = [_ew2d("rs_pair_add_%d" % t, keep, [g0[t], g1[t], got[t]], BF16) for t in range(n)]
    slots = _rs_chip_exchange("rs_chip_exchange", s1)
    red = [_sum_slots("rs_chip_sum_%d" % t, a.reshape(N_CHIP, -1, a.shape[-1])).reshape(a.shape[1:])
           for t, a in enumerate(slots)]
    return _pair_allgather("rs_pair_allgather", red)


PACK_C = 1024
_SHARDED = (("w_in", 1), ("w_br_a", 1), ("w_br_b", 1), ("w_br_c", 1), ("w_out", 0), ("w_ffn_up", 1), ("w_ffn_down", 0))
_SHARDED_SMALL = (("conv_w", (3, 2 * FFN), 1), ("w_alpha2", (2, 16, GLA_QK), 2), ("b_alpha", (2, GLA_QK), 1))


def _prod(shape):
    n = 1
    for s in shape:
        n *= s
    return n


def _to_blocks(full, axis):
    shp = full.shape
    split = full.reshape(shp[:axis] + (N_CHIP, shp[axis] // N_CHIP) + shp[axis + 1:])
    return jnp.moveaxis(split, axis, 0)


def _from_blocks(blocks, axis):
    return jnp.concatenate([blocks[k] for k in range(N_CHIP)], axis=axis)


def _rope_tables(tx):
    pos = jnp.arange(tx, dtype=jnp.int32)
    inv_freq = 10000.0 ** (-jnp.arange(16, dtype=F32) / 16)
    ang_r = (pos // GRID_W).astype(F32)[:, None] * inv_freq
    ang_c = (pos % GRID_W).astype(F32)[:, None] * inv_freq
    ang = jnp.concatenate([ang_r, ang_r, ang_c, ang_c], axis=-1)
    sign = jnp.concatenate([-jnp.ones((16,), F32), jnp.ones((16,), F32)] * 2)
    cos = jnp.concatenate([jnp.ones((TC, HD), F32), jnp.cos(ang)], axis=0)
    sin = jnp.concatenate([jnp.zeros((TC, HD), F32), jnp.sin(ang) * sign], axis=0)
    return jnp.tile(cos, (1, 2)), jnp.tile(sin, (1, 2))


def _lane_consts():
    l = jnp.arange(512)
    seg = (l[:, None] // HD == l[None, :] // HD).astype(F32) / HD
    partner = jnp.where(l % 32 < 16, l + 16, l - 16)
    perm = (l[:, None] == partner[None, :]).astype(F32)
    return seg, perm


def _heads(a, n):
    return a.reshape(a.shape[0], n, HD).transpose(1, 0, 2)


def _unheads(a):
    return a.transpose(1, 0, 2).reshape(a.shape[1], a.shape[0] * HD)


def _gather_f32_shards(shards):
    sizes = [_prod(a.shape) for a in shards]
    flat = jnp.concatenate([a.reshape(-1) for a in shards] + [jnp.zeros((16 * PACK_C - sum(sizes),), F32)])
    got = _allgather_small("gather_f32_shards", flat.reshape(16, PACK_C)).reshape(N_CHIP, 2, 16 * PACK_C)[:, 0]
    out, o = {}, 0
    for (n, _, ax), a, sz in zip(_SHARDED_SMALL, shards, sizes):
        out[n] = jnp.concatenate([got[k, o:o + sz].reshape(a.shape) for k in range(N_CHIP)], axis=ax + 1)
        o += sz
    return out


def _halves(a):
    return a.reshape(2, a.shape[0] // 2, a.shape[1])


def _layer_shards(W, l):
    return [_halves(W[n][l].astype(BF16)) for n, _ in _SHARDED]


def _layer_params(l, gathered, small):
    w2 = small["w_alpha2_full"][l]
    w2pad = jnp.zeros((128, 512), F32).at[0:16, 0:256].set(w2[0]).at[16:32, 256:512].set(w2[1])
    full = {n: _from_blocks(g.reshape(N_CHIP, 2 * g.shape[2], g.shape[3]), ax) for (n, ax), g in zip(_SHARDED, gathered)}
    return dict(
        w_in=_to_new_cols(full["w_in"]), wa=full["w_br_a"], wb=full["w_br_b"], wc=full["w_br_c"],
        w_out=full["w_out"], w_up=full["w_ffn_up"], w_down=full["w_ffn_down"],
        cw=small["conv_w_full"][l], cb=small["conv_b"][l][None], w2=w2pad,
        b2=small["b_alpha_full"][l].reshape(1, 512),
        g1=small["norm1_g"][l][None], g2=small["norm2_g"][l][None], gq=jnp.tile(small["q_norm_g"][l], 8)[None],
        gk=jnp.tile(small["k_norm_g"][l], 2)[None], ggm=small["gmlp_norm_g"][l][None], ws=small["w_spatial"][l],
        bst=small["b_spatial"][l].T, ggl=small["gla_norm_g"][l][None])


def _layer_fwd(l, last, x, h1, mod, P, tabs, gather=()):
    cos, sin, seg, perm = tabs
    n = "l%d_" % l
    s = dict(x=x, h1=h1)
    p = _mm(n + "in_proj", h1, P["w_in"], "nn", BF16, tm_t=768, tn_t=2176, j_outer=True)
    s["p"] = p
    s["gm"] = _gmlp_fwd(n + "gmlp", p, P["ggm"], P["ws"], P["bst"])
    qr, kr, vb = _qk_fwd(n + "qk_prep", p, P["gq"], P["gk"], cos, sin, seg, perm)
    qx, qc, kh, vh = _heads(qr[TC:], NQ), _heads(qr[:TC], NQ), _heads(kr, NKV), _heads(vb, NKV)
    s["qx"], s["qc"], s["kh"], s["vh"] = qx, qc, kh, vh
    one_hot = (jnp.arange(HD) == 0).astype(BF16)
    v1 = jnp.concatenate([vh, jnp.broadcast_to(one_hot, vh.shape)], axis=-1)
    ox, lse_x, *s["gathered"] = _attn_fwd(n + "attn_x", qx, kh, v1, gather)
    s["ox"], s["lse_x"] = ox, lse_x
    if last:
        oc = jnp.zeros((NQ, TC, HD), F32)
    else:
        oc, lse_c = _attn_fwd(n + "attn_c", qc, kh[:, :TC], v1[:, :TC])
        s["oc"], s["lse_c"] = oc, lse_c
    s["att"] = jnp.concatenate([_unheads(oc), _unheads(ox)], axis=0).astype(BF16)
    la = _decay_fwd(n + "gla_decay", p, P["w2"], P["b2"])
    s["la"] = la
    s["of"], s["sf"], s["ob"], s["sb"] = _gla_fwd(n + "gla_scan", p, la)
    s["gla"] = _gla_out_fwd(n + "gla_out", s["of"], s["ob"], p, P["ggl"])
    s["merged"] = _merge_fwd(n + "merge", s["gm"], s["att"], s["gla"], P["wa"], P["wb"], P["wc"], p)
    s["mix"] = _mm(n + "out_proj", s["merged"], P["w_out"], "nn", F32)
    s["x_mid"], s["h2"] = _res_nm_fwd(n + "res1_norm2", x, s["mix"], mod, 2, mod, P["g2"], 3, 4)
    s["a"] = _mm(n + "ffn_up", s["h2"], P["w_up"], "nn", BF16, j_outer=True)
    s["act"] = _conv_fwd(n + "conv_gate", s["a"], P["cw"], P["cb"])
    s["f"] = _mm(n + "ffn_down", s["act"], P["w_down"], "nn", F32)
    return s


def _layer_bwd(l, last, s, mod, P, tabs, dx_mid, df, gw):
    cos, sin, seg, perm = tabs
    n = "l%d_b_" % l
    t = dx_mid.shape[0]
    p = s["p"]
    gw["w_ffn_down"] = _mm(n + "ffn_down_w", s["act"], df, "tn", F32, tm_t=1408)
    dact = _mm(n + "ffn_down_x", df, P["w_down"], "nt", F32)
    da, dcw, dcb = _conv_bwd(n + "conv_gate", s["a"], P["cw"], P["cb"], dact)
    gw["conv_w"], gw["conv_b"] = dcw.transpose(1, 0, 2).reshape(3, 2 * FFN), dcb.reshape(2 * FFN)
    gw["w_ffn_up"] = _mm(n + "ffn_up_w", s["h2"], da, "tn", F32, chip_blocks=True)
    dh2 = _mm(n + "ffn_up_x", da, P["w_up"], "nt", F32)
    dx, dmix, dmod_a, dmod_b, dg2 = _res_nm_bwd(n + "res1_norm2", s["x"], s["mix"], mod, 2, mod, P["g2"], 3, 4, dx_mid, dh2)
    dmod = dmod_a + dmod_b
    gw["norm2_g"] = dg2[0]
    gw["w_out"] = _mm(n + "out_proj_w", s["merged"], dmix, "tn", F32)
    dmerged = _mm(n + "out_proj_x", dmix, P["w_out"], "nt", F32)
    dya, dyb, dyc, dga, dgb, dgc = _merge_bwd(n + "merge", s["gm"], s["att"], s["gla"], P["wa"], P["wb"], P["wc"], p, dmerged)
    gw["w_br_a"] = _mm(n + "br_a_w", s["gm"], dya, "tn", F32)
    gw["w_br_b"] = _mm(n + "br_b_w", s["att"], dyb, "tn", F32)
    gw["w_br_c"] = _mm(n + "br_c_w", s["gla"], dyc, "tn", F32)
    dgm = _mm(n + "br_a_x", dya, P["wa"], "nt", F32)
    datt = _mm(n + "br_b_x", dyb, P["wb"], "nt", F32)
    dgla = _mm(n + "br_c_x", dyc, P["wc"], "nt", F32)
    du, dv_g, dggm, dws, dbst = _gmlp_bwd(n + "gmlp", p, P["ggm"], P["ws"], P["bst"], dgm)
    gw["gmlp_norm_g"], gw["w_spatial"], gw["b_spatial"] = dggm[0], dws, dbst.T
    kh, vh = s["kh"], s["vh"]
    row = lambda a: a.reshape(a.shape[0], 1, a.shape[1])
    dqx, dkh, dvh = _attn_bwd(n + "attn_x", s["qx"], kh, vh, s["ox"], _heads(datt[TC:], NQ), row(s["lse_x"]))
    if last:
        dqc = jnp.zeros((NQ, TC, HD), F32)
    else:
        dqc, dkc, dvc = _attn_bwd(n + "attn_c", s["qc"], kh[:, :TC], vh[:, :TC], s["oc"], _heads(datt[:TC], NQ),
                                  row(s["lse_c"]))
        pad = jnp.zeros((NKV, t - TC, HD), F32)
        dkh = dkh + jnp.concatenate([dkc, pad], axis=1)
        dvh = dvh + jnp.concatenate([dvc, pad], axis=1)
    dqr = jnp.concatenate([_unheads(dqc), _unheads(dqx)], axis=0)
    dq, dk, dgq, dgk = _qk_bwd(n + "qk_prep", p, P["gq"], P["gk"], cos, sin, seg, perm, dqr, _unheads(dkh))
    gw["q_norm_g"], gw["k_norm_g"] = dgq.reshape(8, HD).sum(0), dgk.reshape(2, HD).sum(0)
    dv_att = _unheads(dvh).astype(BF16)
    do, dr, dggl = _gla_out_bwd(n + "gla_out", s["of"], s["ob"], p, P["ggl"], dgla)
    gw["gla_norm_g"] = dggl[0]
    scans = _gla_bwd(n + "gla_scan", p, s["la"], s["sf"], s["sb"], do)
    dab, dw2, db2, dglq, dglk, dglv = _decay_bwd(n + "gla_decay", p, P["w2"], P["b2"], scans[:4], scans[4:])
    gw["w_alpha2"] = jnp.stack([dw2[0:16, 0:256], dw2[16:32, 256:512]])
    gw["b_alpha"] = db2.reshape(2, 256)
    dp = jnp.concatenate([dga, dgb, dgc, du, dv_g, dq, dglv, dr, dglq, dglk, dk, dv_att, dab], axis=-1)
    gw["w_in"] = _to_ref_cols(_mm(n + "in_proj_w", s["h1"], dp, "tn", F32, tn_t=2176, tk_t=768))
    dh1 = _mm(n + "in_proj_x", dp, P["w_in"], "nt", F32, tk_t=2176)
    return dx, dh1, dmod


_SMALL = (("norm1_g", (2, D)), ("norm2_g", (2, D)), ("q_norm_g", (2, HD)), ("k_norm_g", (2, HD)), ("gmlp_norm_g", (2, GW)),
          ("gla_norm_g", (2, GLA_V)), ("w_spatial", (2, 4, 128, 128)), ("b_spatial", (2, 4, 128)), ("conv_b", (2, 2 * FFN)),
          ("final_norm_g", (D,))) + tuple((n, (2,) + s) for n, s, _ in _SHARDED_SMALL)
_SMALL_N = 2 * 2 * ADA_W + sum(_prod(s) for _, s in _SMALL)
_SMALL_R = -(-_SMALL_N // (PACK_C * 8)) * 8


def _mod_tables(c, c_ctx, w_ada, b_ada, k):
    x, y, cc = _coords()
    me = 4 * x + 2 * y + cc
    c_all = _allgather_small("gather_c", jnp.concatenate([c, jnp.zeros((7, D), F32)], axis=0))
    c8 = c_all.reshape(N_DEV, 8, D)[:, 0]
    cond = jnp.concatenate([c8, c_ctx[None], jnp.zeros((7, D), F32)], axis=0)
    b_loc = lax.dynamic_slice_in_dim(b_ada, k * ADA_LOC, ADA_LOC, axis=1)[:, None, :]
    m_loc = _adaln_fwd("adaln", cond, w_ada, b_loc)
    m_all = _allgather_small("gather_mod", m_loc.reshape(32, ADA_LOC)).reshape(N_CHIP, 2, 2, 16, ADA_LOC)[:, 0]
    m_all = m_all.transpose(1, 2, 0, 3).reshape(2, 16, ADA_W)
    rows = jnp.stack([m_all[:, 8], lax.dynamic_index_in_dim(m_all, me, axis=1, keepdims=False)], axis=1)
    return rows.reshape(2, 2, 6, D), c8


def _step(x, c, ctx, c_ctx, W, tgt):
    xc, yc, cc = _coords()
    k = 2 * xc + yc
    tx = x.shape[0]
    t = TC + tx
    small = {n: W[n] for n, _ in _SMALL}
    for n, a in _gather_f32_shards([W[n] for n, _, _ in _SHARDED_SMALL]).items():
        small[n + "_full"] = a

    shards = [_layer_shards(W, l) for l in range(2)]
    mods, c8 = _mod_tables(c, c_ctx, W["w_ada"], W["b_ada"], k)
    tabs = _rope_tables(tx) + _lane_consts()
    params = [_layer_params(0, _allgather_layers("gather_weights", shards[0]), small)]

    xs = jnp.concatenate([ctx, x], axis=0)
    h1 = _nm_fwd("l0_norm1", xs, mods[0], params[0]["g1"], 0, 1)
    s0 = _layer_fwd(0, False, xs, h1, mods[0], params[0], tabs, gather=shards[1])
    params.append(_layer_params(1, _gather_own(s0["gathered"], shards[1]), small))
    x1, h1b = _res_nm_fwd("l0_res2_norm1", s0["x_mid"], s0["f"], mods[0], 5, mods[1], params[1]["g1"], 0, 1)
    s1 = _layer_fwd(1, True, x1, h1b, mods[1], params[1], tabs)
    loss, dxm_l, df_l, dmod_head, dgf = _head("head", s1["x_mid"], s1["f"], mods[1], W["final_norm_g"][None], tgt)

    gws = [dict(), dict()]
    dx1, dh1b, dmod1 = _layer_bwd(1, True, s1, mods[1], params[1], tabs, dxm_l, df_l, gws[1])
    dxm0, df0, dmod0_g, dmod1_s, dg1b = _res_nm_bwd("l0_b_res2_norm1", s0["x_mid"], s0["f"], mods[0], 5, mods[1],
                                                    params[1]["g1"], 0, 1, dx1, dh1b)
    gws[1]["norm1_g"] = dg1b[0]
    dx0, dh1, dmod0 = _layer_bwd(0, False, s0, mods[0], params[0], tabs, dxm0, df0, gws[0])
    grad_x, dmod0_s, dg1 = _nm_bwd("l0_b_norm1", xs, mods[0], params[0]["g1"], 0, 1, dx0, dh1)
    gws[0]["norm1_g"] = dg1[0]
    dmods = jnp.stack([dmod0 + dmod0_g + dmod0_s, dmod1 + dmod1_s + dmod_head])

    stk = {n: jnp.stack([gws[0][n], gws[1][n]]) for n, _ in _SMALL if n != "final_norm_g"}
    stk["final_norm_g"] = dgf[0]
    flat = jnp.concatenate([dmods.reshape(-1)] + [stk[n].reshape(-1) for n, _ in _SMALL])
    flat = jnp.concatenate([flat, jnp.zeros((_SMALL_R * PACK_C - _SMALL_N,), F32)]).reshape(_SMALL_R, PACK_C)
    every = _allgather_small("gather_small_grads", flat).reshape(N_DEV, _SMALL_R, PACK_C)
    tot = _sum_slots("sum_small_grads", every).reshape(-1)
    grads, o = {}, 2 * 2 * ADA_W
    for n, shp in _SMALL:
        grads[n] = tot[o:o + _prod(shp)].reshape(shp)
        o += _prod(shp)
    grads["b_ada"] = tot[:2 * 2 * ADA_W].reshape(2, 2, ADA_W).sum(axis=1)

    dm_every = every[:, :2 * 2 * ADA_W // PACK_C].reshape(N_DEV, 2, 2, ADA_W)
    dm_loc = lax.dynamic_slice_in_dim(dm_every, k * ADA_LOC, ADA_LOC, axis=3).transpose(1, 2, 0, 3)
    cc8 = jnp.concatenate([c_ctx[None], jnp.zeros((7, D), F32)], axis=0)
    grads["w_ada"], dcc = _adaln_bwd("adaln_b", c8, cc8, W["w_ada"], dm_loc[:, 1], dm_loc[:, 0])
    dcc_every = _allgather_small("gather_dcctx", dcc * 0.5).reshape(N_DEV, 8, D)
    grads["c_ctx"] = _sum_slots("sum_dcctx", dcc_every)[0]

    for n, shp, ax in _SHARDED_SMALL:
        grads[n] = lax.dynamic_slice_in_dim(grads[n], k * (shp[ax] // N_CHIP), shp[ax] // N_CHIP, axis=ax + 1)
    blocks = lambda g, n, ax: g if n == "w_ffn_up" else _to_blocks(g, ax)
    red = _reduce_scatter(*[[blocks(gws[l][n], n, ax) for n, ax in _SHARDED] for l in range(2)])
    grads.update({n: a for (n, _), a in zip(_SHARDED, red)})
    return loss[0, 0], grad_x, grads


_WEIGHTS = ("c_ctx", "w_ada", "b_ada", "norm1_g", "norm2_g", "w_in", "q_norm_g", "k_norm_g", "gmlp_norm_g", "w_spatial",
            "b_spatial", "w_alpha2", "b_alpha", "gla_norm_g", "w_br_a", "w_br_b", "w_br_c", "w_out", "w_ffn_up", "conv_w",
            "conv_b", "w_ffn_down", "final_norm_g")
_BIG = ("w_ada", "w_in", "w_br_a", "w_br_b", "w_br_c", "w_out", "w_ffn_up", "w_ffn_down")


def _update(W, G, M, V):
    delta, new_m, new_v = {}, {}, {}
    for n in _BIG:
        delta[n], new_m[n], new_v[n] = _adamw("adamw_" + n, W[n], G[n], M[n], V[n])
    rest = [n for n in _WEIGHTS if n not in _BIG]
    tot = sum(_prod(W[n].shape) for n in rest)
    rows = -(-tot // (PACK_C * 8)) * 8

    def cat(dct):
        flat = jnp.concatenate([dct[n].reshape(-1) for n in rest] + [jnp.zeros((rows * PACK_C - tot,), F32)])
        return flat.reshape(1, rows, PACK_C)

    outs = _adamw("adamw_small", cat(W), cat(G), cat(M), cat(V))
    o = 0
    for n in rest:
        sz, shp = _prod(W[n].shape), W[n].shape
        delta[n], new_m[n], new_v[n] = (a.reshape(-1)[o:o + sz].reshape(shp) for a in outs)
        o += sz
    return delta, new_m, new_v


def kernel(x, c, ctx, c_ctx, w_ada, b_ada, norm1_g, norm2_g, w_in, q_norm_g, k_norm_g, gmlp_norm_g, w_spatial, b_spatial, w_alpha2, b_alpha, gla_norm_g, w_br_a, w_br_b, w_br_c, w_out, w_ffn_up, conv_w, conv_b, w_ffn_down, final_norm_g, loss_target, m_c_ctx, m_w_ada, m_b_ada, m_norm1_g, m_norm2_g, m_w_in, m_q_norm_g, m_k_norm_g, m_gmlp_norm_g, m_w_spatial, m_b_spatial, m_w_alpha2, m_b_alpha, m_gla_norm_g, m_w_br_a, m_w_br_b, m_w_br_c, m_w_out, m_w_ffn_up, m_conv_w, m_conv_b, m_w_ffn_down, m_final_norm_g, v_c_ctx, v_w_ada, v_b_ada, v_norm1_g, v_norm2_g, v_w_in, v_q_norm_g, v_k_norm_g, v_gmlp_norm_g, v_w_spatial, v_b_spatial, v_w_alpha2, v_b_alpha, v_gla_norm_g, v_w_br_a, v_w_br_b, v_w_br_c, v_w_out, v_w_ffn_up, v_conv_w, v_conv_b, v_w_ffn_down, v_final_norm_g):
    W = dict(c_ctx=c_ctx, w_ada=w_ada, b_ada=b_ada, norm1_g=norm1_g, norm2_g=norm2_g, w_in=w_in, q_norm_g=q_norm_g,
             k_norm_g=k_norm_g, gmlp_norm_g=gmlp_norm_g, w_spatial=w_spatial, b_spatial=b_spatial, w_alpha2=w_alpha2,
             b_alpha=b_alpha, gla_norm_g=gla_norm_g, w_br_a=w_br_a, w_br_b=w_br_b, w_br_c=w_br_c, w_out=w_out,
             w_ffn_up=w_ffn_up, conv_w=conv_w, conv_b=conv_b, w_ffn_down=w_ffn_down, final_norm_g=final_norm_g)
    M = dict(c_ctx=m_c_ctx, w_ada=m_w_ada, b_ada=m_b_ada, norm1_g=m_norm1_g, norm2_g=m_norm2_g, w_in=m_w_in,
             q_norm_g=m_q_norm_g, k_norm_g=m_k_norm_g, gmlp_norm_g=m_gmlp_norm_g, w_spatial=m_w_spatial,
             b_spatial=m_b_spatial, w_alpha2=m_w_alpha2, b_alpha=m_b_alpha, gla_norm_g=m_gla_norm_g, w_br_a=m_w_br_a,
             w_br_b=m_w_br_b, w_br_c=m_w_br_c, w_out=m_w_out, w_ffn_up=m_w_ffn_up, conv_w=m_conv_w, conv_b=m_conv_b,
             w_ffn_down=m_w_ffn_down, final_norm_g=m_final_norm_g)
    V = dict(c_ctx=v_c_ctx, w_ada=v_w_ada, b_ada=v_b_ada, norm1_g=v_norm1_g, norm2_g=v_norm2_g, w_in=v_w_in,
             q_norm_g=v_q_norm_g, k_norm_g=v_k_norm_g, gmlp_norm_g=v_gmlp_norm_g, w_spatial=v_w_spatial,
             b_spatial=v_b_spatial, w_alpha2=v_w_alpha2, b_alpha=v_b_alpha, gla_norm_g=v_gla_norm_g, w_br_a=v_w_br_a,
             w_br_b=v_w_br_b, w_br_c=v_w_br_c, w_out=v_w_out, w_ffn_up=v_w_ffn_up, conv_w=v_conv_w, conv_b=v_conv_b,
             w_ffn_down=v_w_ffn_down, final_norm_g=v_final_norm_g)
    loss_local, grad_x, G = _step(x[0], c, ctx[0], c_ctx, W, loss_target[0])
    loss = lax.psum(loss_local, ("x", "y", "c"))
    delta, new_m, new_v = _update(W, G, M, V)
    return (loss, grad_x[None], *[G[n] for n in _WEIGHTS], *[delta[n] for n in _WEIGHTS],
            *[new_m[n] for n in _WEIGHTS], *[new_v[n] for n in _WEIGHTS])
```

```python
import functools

import jax
import jax.numpy as jnp
from jax import lax
from jax.experimental import pallas as pl
from jax.experimental.pallas import tpu as pltpu

F32 = jnp.float32
BF16 = jnp.bfloat16

D = 1024
TC = 256
GRID_W = 64
EPS = 1e-6
HD = 64
NQ = 8
NKV = 2
QG = NQ // NKV
GLA_H = 4
GLA_DK = 64
GLA_DV = 128
GLA_QK = 256
GLA_V = 512
GLA_CHUNK = 64
GLA_TAU = 16.0
GW = 512
FFN = 2816
IN_W = 6432
PW = 6528
ADA_W = 6 * D
N_CHIP = 4
N_DEV = 8
ADA_LOC = ADA_W // N_CHIP

ADAM_LR = 0.001
ADAM_B1 = 0.9
ADAM_B2 = 0.999
ADAM_EPS = 1e-08
ADAM_WD = 0.01
ADAM_STEP = 10

TM = 256
NCB = TC // TM
LANE = 128
VMEM_LIMIT = 56 * 1024 * 1024
MESH = pl.DeviceIdType.MESH

_COLS = (("gA", 3360, 1024), ("gB", 4384, 1024), ("gC", 5408, 1024), ("gu", 0, 512), ("gv", 512, 512),
         ("q", 1024, 512), ("glv", 2304, 512), ("gr", 2848, 512), ("glq", 1792, 256), ("glk", 2048, 256),
         ("k", 1536, 128), ("v", 1664, 128), ("ab", 2816, 32))
OFF = {}
_o = 0
for _n, _s, _w in _COLS:
    OFF[_n] = _o
    _o += max(_w, LANE)
assert _o == PW


def _to_new_cols(w):
    parts = [w[..., s:s + n] for _, s, n in _COLS]
    pad = jnp.zeros(w.shape[:-1] + (PW - IN_W,), w.dtype)
    return jnp.concatenate(parts + [pad], axis=-1)


def _to_ref_cols(w):
    by_start = sorted(_COLS, key=lambda t: t[1])
    return jnp.concatenate([w[..., OFF[n]:OFF[n] + wd] for n, _, wd in by_start], axis=-1)


def _tile(n, target, align=LANE):
    best = None
    t = align
    while t <= min(n, target):
        if n % t == 0:
            best = t
        t += align
    assert best is not None, (n, target, align)
    return best


def _cp(sem=None):
    return pltpu.CompilerParams(dimension_semantics=sem, vmem_limit_bytes=VMEM_LIMIT)


def _bdot_impl(a, b, ca, cb):
    return lax.dot_general(a.astype(BF16), b.astype(BF16), (((ca,), (cb,)), ((), ())),
                           preferred_element_type=F32)


@functools.partial(jax.custom_vjp, nondiff_argnums=(2, 3))
def bdot(a, b, ca, cb):
    return _bdot_impl(a, b, ca, cb)


def _bdot_fwd(a, b, ca, cb):
    return _bdot_impl(a, b, ca, cb), (a, b)


def _bdot_bwd(ca, cb, res, g):
    a, b = res
    da = bdot(g, b, 1, 1 - cb) if ca == 1 else bdot(b, g, 1 - cb, 1)
    db = bdot(a, g, 1 - ca, 0) if cb == 0 else bdot(g, a, 0, 1 - ca)
    return da.astype(a.dtype), db.astype(b.dtype)


bdot.defvjp(_bdot_fwd, _bdot_bwd)


def hdot(a, b, ca=1, cb=0):
    return lax.dot_general(a, b, (((ca,), (cb,)), ((), ())), precision=lax.Precision.HIGH,
                           preferred_element_type=F32)


def _rms(x, g):
    return x * lax.rsqrt(jnp.mean(x * x, axis=-1, keepdims=True) + EPS) * g


def _gelu(x):
    return 0.5 * x * (1.0 + jnp.tanh(0.7978845608028654 * (x + 0.044715 * (x * x * x))))


def _log_sigmoid(z):
    return jnp.minimum(z, 0.0) - jnp.log(1.0 + jnp.exp(-jnp.abs(z)))


def _sel(mod, is_lat, idx):
    return jnp.where(is_lat, mod[1, idx:idx + 1, :], mod[0, idx:idx + 1, :])


def _rows_call(name, fn, grid, ins, outs, acc_axes=None, sem=None):
    n_in = len(ins)
    flags = [o[2] for o in outs]
    if acc_axes is None:
        acc_axes = (len(grid) - 1,)

    def body(*refs):
        ids = tuple(pl.program_id(a) for a in range(len(grid)))
        res = fn(ids, *[r[...] for r in refs[:n_in]])
        for r, v, acc in zip(refs[n_in:], res, flags):
            if acc:
                first = functools.reduce(jnp.logical_and, [ids[a] == 0 for a in acc_axes])

                @pl.when(first)
                def _():
                    r[...] = jnp.zeros_like(r)
                r[...] += v.astype(r.dtype)
            else:
                r[...] = v.astype(r.dtype)

    return pl.pallas_call(
        body, name=name, grid=grid, in_specs=[s for _, s in ins], out_specs=[o[1] for o in outs],
        out_shape=[o[0] for o in outs],
        compiler_params=_cp(sem if sem is not None else ("arbitrary",) * len(grid)),
    )(*[a for a, _ in ins])


def _sds(shape, dtype):
    return jax.ShapeDtypeStruct(shape, dtype)


def _rowspec(width, off=0, tm=TM):
    assert off % width == 0
    return pl.BlockSpec((tm, width), lambda i, o=off // width: (i, o))


def _full(shape):
    nd = len(shape)
    return pl.BlockSpec(shape, lambda *a: (0,) * nd)


def _mm(name, a, b, mode, out_dtype, tm_t=1056, tn_t=1408, tk_t=1408, chip_blocks=False, j_outer=False):
    halves = a.ndim == 3 or b.ndim == 3
    if mode == "nn":
        (m, k), (_, n) = a.shape, b.shape
    elif mode == "nt":
        (m, k), (n, _) = a.shape[-2:], b.shape
        k *= a.ndim - 1
    else:
        (k, m), (_, n) = a.shape, b.shape[-2:]
        n *= b.ndim - 1
    tm = _tile(m, tm_t, 8 if m % LANE else LANE)
    tn = _tile(n // 2 if halves and mode == "tn" else n, tn_t)
    tk = _tile(k // 2 if halves and mode == "nt" else k, tk_t)
    nk = k // tk
    if mode == "nn":
        dims, a_spec, b_spec = ((1,), (0,)), pl.BlockSpec((tm, tk), lambda i, j, l: (i, l)), pl.BlockSpec((tk, tn), lambda i, j, l: (l, j))
    elif mode == "nt":
        dims, a_spec, b_spec = ((1,), (1,)), pl.BlockSpec((tm, tk), lambda i, j, l: (i, l)), pl.BlockSpec((tn, tk), lambda i, j, l: (j, l))
        if halves:
            a_spec = pl.BlockSpec((None, tm, tk), lambda i, j, l, h=nk // 2: (l // h, i, l % h))
    else:
        dims, a_spec, b_spec = ((0,), (0,)), pl.BlockSpec((tk, tm), lambda i, j, l: (l, i)), pl.BlockSpec((tk, tn), lambda i, j, l: (l, j))
        if halves:
            b_spec = pl.BlockSpec((None, tk, tn), lambda i, j, l, h=n // tn // 2: (j // h, l, j % h))

    def body(a_ref, b_ref, o_ref, *scratch):
        l = pl.program_id(2)
        part = lax.dot_general(a_ref[...].astype(BF16), b_ref[...].astype(BF16), (dims, ((), ())),
                               preferred_element_type=F32)
        if nk == 1:
            o_ref[...] = part.astype(o_ref.dtype)
            return
        acc_ref = scratch[0]

        @pl.when(l == 0)
        def _():
            acc_ref[...] = part

        @pl.when(l > 0)
        def _():
            acc_ref[...] += part

        @pl.when(l == nk - 1)
        def _():
            o_ref[...] = acc_ref[...].astype(o_ref.dtype)

    o_spec, o_shape = pl.BlockSpec((tm, tn), lambda i, j, l: (i, j)), _sds((m, n), out_dtype)
    if chip_blocks:
        assert tn * N_CHIP == n and tm == m
        o_spec, o_shape = pl.BlockSpec((None, tm, tn), lambda i, j, l: (j, 0, 0)), _sds((N_CHIP, m, tn), out_dtype)
    grid = (m // tm, n // tn, nk)
    if j_outer:
        swap = lambda spec: pl.BlockSpec(spec.block_shape, lambda j, i, l, f=spec.index_map: f(i, j, l))
        a_spec, b_spec, o_spec, grid = swap(a_spec), swap(b_spec), swap(o_spec), (n // tn, m // tm, nk)
    return pl.pallas_call(
        body, name=name, grid=grid, in_specs=[a_spec, b_spec], out_specs=o_spec, out_shape=o_shape,
        scratch_shapes=[pltpu.VMEM((tm, tn), F32)] if nk > 1 else [],
        compiler_params=_cp(("parallel", "parallel", "arbitrary")),
    )(a, b)


def _nm_fn(is_lat, x, mod, g, shift, scale):
    return _rms(x, g) * (1.0 + _sel(mod, is_lat, scale)) + _sel(mod, is_lat, shift)


def _res_nm_fn(is_lat, x, br, modg, gate, mods, g, shift, scale):
    xn = x + _sel(modg, is_lat, gate) * br
    return xn, _nm_fn(is_lat, xn, mods, g, shift, scale)


def _nm_fwd(name, x, mod, g, shift, scale):
    t = x.shape[0]
    fn = lambda ids, xv, mv, gv: (_nm_fn(ids[0] >= NCB, xv, mv, gv, shift, scale),)
    return _rows_call(name, fn, (t // TM,), [(x, _rowspec(D)), (mod, _full((2, 6, D))), (g, _full((1, D)))],
                      [(_sds((t, D), BF16), _rowspec(D), False)])[0]


def _nm_bwd(name, x, mod, g, shift, scale, dx_res, dh):
    t = x.shape[0]

    def fn(ids, xv, mv, gv, dxr, dhv):
        _, vjp = jax.vjp(lambda a, b, c: _nm_fn(ids[0] >= NCB, a, b, c, shift, scale), xv, mv, gv)
        dx, dm, dg = vjp(dhv)
        return dx + dxr, dm, dg

    lat = pl.BlockSpec((TM, D), lambda i: (jnp.maximum(i - NCB, 0), 0))
    return _rows_call(name, fn, (t // TM,),
                      [(x, _rowspec(D)), (mod, _full((2, 6, D))), (g, _full((1, D))), (dx_res, _rowspec(D)), (dh, _rowspec(D))],
                      [(_sds((t - TC, D), F32), lat, False), (_sds((2, 6, D), F32), _full((2, 6, D)), True),
                       (_sds((1, D), F32), _full((1, D)), True)])


def _res_nm_fwd(name, x, br, modg, gate, mods, g, shift, scale):
    t = x.shape[0]
    fn = lambda ids, xv, bv, mg, ms, gv: _res_nm_fn(ids[0] >= NCB, xv, bv, mg, gate, ms, gv, shift, scale)
    return _rows_call(name, fn, (t // TM,),
                      [(x, _rowspec(D)), (br, _rowspec(D)), (modg, _full((2, 6, D))), (mods, _full((2, 6, D))), (g, _full((1, D)))],
                      [(_sds((t, D), F32), _rowspec(D), False), (_sds((t, D), BF16), _rowspec(D), False)])


def _res_nm_bwd(name, x, br, modg, gate, mods, g, shift, scale, dx_res, dh):
    t = x.shape[0]

    def fn(ids, xv, bv, mg, ms, gv, dxr, dhv):
        f = lambda a, b, c, d, e: _res_nm_fn(ids[0] >= NCB, a, b, c, gate, d, e, shift, scale)
        _, vjp = jax.vjp(f, xv, bv, mg, ms, gv)
        return vjp((dxr, dhv))

    m26 = (_sds((2, 6, D), F32), _full((2, 6, D)), True)
    return _rows_call(name, fn, (t // TM,),
                      [(x, _rowspec(D)), (br, _rowspec(D)), (modg, _full((2, 6, D))), (mods, _full((2, 6, D))), (g, _full((1, D))),
                       (dx_res, _rowspec(D)), (dh, _rowspec(D))],
                      [(_sds((t, D), F32), _rowspec(D), False), (_sds((t, D), BF16), _rowspec(D), False), m26, m26,
                       (_sds((1, D), F32), _full((1, D)), True)])


def _head(name, x_mid, f, mod, gf, tgt):
    t = x_mid.shape[0]

    def fn(ids, xv, fv, mv, gv, tv):
        def loss_fn(a, b, c, d):
            y = _rms(a + c[1, 5:6, :] * b, d)
            e = y - tv
            return 0.5 * jnp.sum(jnp.mean(e * e, axis=-1))
        loss, grads = jax.value_and_grad(loss_fn, argnums=(0, 1, 2, 3))(xv, fv, mv, gv)
        return tuple(jnp.where(ids[0] >= NCB, v, 0.0) for v in (jnp.reshape(loss, (1, 1)),) + grads)

    return _rows_call(name, fn, (t // TM,),
                      [(x_mid, _rowspec(D)), (f, _rowspec(D)), (mod, _full((2, 6, D))), (gf, _full((1, D))),
                       (tgt, pl.BlockSpec((TM, D), lambda i: (jnp.maximum(i - NCB, 0), 0)))],
                      [(_sds((1, 1), F32), _full((1, 1)), True), (_sds((t, D), F32), _rowspec(D), False),
                       (_sds((t, D), BF16), _rowspec(D), False), (_sds((2, 6, D), F32), _full((2, 6, D)), True),
                       (_sds((1, D), F32), _full((1, D)), True)])


def _gmlp_fn(u, v, g, ws, bst):
    rows = []
    u, v = u.astype(F32), v.astype(F32)
    for r in range(u.shape[0] // 128):
        uu, vv = _gelu(u[128 * r:128 * r + 128]), _gelu(v[128 * r:128 * r + 128])
        cols = []
        for gi in range(4):
            sl = slice(128 * gi, 128 * gi + 128)
            f = bdot(ws[gi], _rms(vv[:, sl], g[:, sl]), 1, 0) + bst[:, gi:gi + 1]
            cols.append(uu[:, sl] * f)
        rows.append(jnp.concatenate(cols, axis=-1))
    return jnp.concatenate(rows, axis=0)


def _gmlp_ins(p, g, ws, bst):
    return [(p, _rowspec(GW, OFF["gu"])), (p, _rowspec(GW, OFF["gv"])), (g, _full((1, GW))),
            (ws, _full((4, 128, 128))), (bst, _full((128, 4)))]


def _gmlp_fwd(name, p, g, ws, bst):
    t = p.shape[0]
    return _rows_call(name, lambda ids, *a: (_gmlp_fn(*a),), (t // TM,), _gmlp_ins(p, g, ws, bst),
                      [(_sds((t, GW), BF16), _rowspec(GW), False)])[0]


def _gmlp_bwd(name, p, g, ws, bst, dgm):
    t = p.shape[0]

    def fn(ids, u, v, gv, wv, bv, dv):
        _, vjp = jax.vjp(_gmlp_fn, u, v, gv, wv, bv)
        return vjp(dv)

    return _rows_call(name, fn, (t // TM,), _gmlp_ins(p, g, ws, bst) + [(dgm, _rowspec(GW))],
                      [(_sds((t, GW), BF16), _rowspec(GW), False), (_sds((t, GW), BF16), _rowspec(GW), False),
                       (_sds((1, GW), F32), _full((1, GW)), True), (_sds((4, 128, 128), F32), _full((4, 128, 128)), True),
                       (_sds((128, 4), F32), _full((128, 4)), True)])


def _qk_fn(q, k, gq, gk, cos, sin, seg, perm):
    cq, sq = jnp.concatenate([cos] * 4, axis=-1), jnp.concatenate([sin] * 4, axis=-1)
    q, k = q.astype(F32), k.astype(F32)
    qn = q * lax.rsqrt(hdot(q * q, seg) + EPS) * gq
    kn = k * lax.rsqrt(hdot(k * k, seg[:128, :128]) + EPS) * gk
    qr = qn * cq + hdot(qn, perm) * sq
    kr = kn * cos + hdot(kn, perm[:128, :128]) * sin
    return qr * (HD ** -0.5), kr


def _qk_ins(p, gq, gk, cos, sin, seg, perm):
    return [(p, _rowspec(512, OFF["q"])), (p, _rowspec(128, OFF["k"])), (gq, _full((1, 512))), (gk, _full((1, 128))),
            (cos, _rowspec(128)), (sin, _rowspec(128)), (seg, _full((512, 512))), (perm, _full((512, 512)))]


def _qk_fwd(name, p, gq, gk, cos, sin, seg, perm):
    t = p.shape[0]
    fn = lambda ids, q, k, a, b, c, s, sg, pm, v: _qk_fn(q, k, a, b, c, s, sg, pm) + (v,)
    return _rows_call(name, fn, (t // TM,), _qk_ins(p, gq, gk, cos, sin, seg, perm) + [(p, _rowspec(128, OFF["v"]))],
                      [(_sds((t, 512), BF16), _rowspec(512), False), (_sds((t, 128), BF16), _rowspec(128), False),
                       (_sds((t, 128), BF16), _rowspec(128), False)])


def _qk_bwd(name, p, gq, gk, cos, sin, seg, perm, dqr, dkr):
    t = p.shape[0]

    def fn(ids, q, k, a, b, c, s, sg, pm, dq, dk):
        _, vjp = jax.vjp(lambda q_, k_, a_, b_: _qk_fn(q_, k_, a_, b_, c, s, sg, pm), q, k, a, b)
        return vjp((dq, dk))

    return _rows_call(name, fn, (t // TM,),
                      _qk_ins(p, gq, gk, cos, sin, seg, perm) + [(dqr, _rowspec(512)), (dkr, _rowspec(128))],
                      [(_sds((t, 512), BF16), _rowspec(512), False), (_sds((t, 128), BF16), _rowspec(128), False),
                       (_sds((1, 512), F32), _full((1, 512)), True), (_sds((1, 128), F32), _full((1, 128)), True)])


_ATT_TQ = 1024
_ATT_TK = 768


def _attn_fwd(name, q, k, v, gather=()):
    h, tq_all, _ = q.shape
    hkv, tk_all, _ = k.shape
    tq, tk = _tile(tq_all, _ATT_TQ), _tile(tk_all, _ATT_TK)
    nkc = tk_all // tk
    ng, nq = len(gather), tq_all // tq

    def body(*refs):
        q_ref, k_ref, v_ref = refs[:3]
        o_ref, lse_ref = refs[3 + ng:5 + ng]
        if ng:
            g_id, i_id = pl.program_id(0), pl.program_id(1)
            stage = lambda st: _gather_stage(st, refs[3:3 + ng], refs[5 + ng:5 + 2 * ng], *refs[5 + 2 * ng:])
            pl.when(jnp.logical_and(g_id == 0, i_id == 0))(lambda: stage(0))
            pl.when(jnp.logical_and(g_id == hkv - 1, i_id == 0))(lambda: stage(1))
        qv = q_ref[...].reshape(QG * tq, HD)

        def step(j, carry):
            m, acc = carry
            off = pl.multiple_of(j * tk, tk)
            kk, vv = k_ref[0, pl.ds(off, tk), :], v_ref[0, pl.ds(off, tk), :]
            s = lax.dot_general(qv, kk, (((1,), (1,)), ((), ())), preferred_element_type=F32)
            m_new = jnp.maximum(m, jnp.max(s, axis=-1, keepdims=True))
            pr = jnp.exp(s - m_new)
            acc = jnp.exp(m - m_new) * acc + jnp.dot(pr.astype(BF16), vv, preferred_element_type=F32)
            return m_new, acc

        init = (jnp.full((QG * tq, 1), -jnp.inf, F32), jnp.zeros((QG * tq, 2 * HD), F32))
        m, acc = lax.fori_loop(0, nkc, step, init)
        l = acc[:, HD:HD + 1]
        o_ref[...] = (acc[:, :HD] / l).reshape(QG, tq, HD)
        lse_ref[...] = (m + jnp.log(l)).reshape(QG, tq, 1)
        if ng:
            pl.when(jnp.logical_and(g_id == hkv - 1, i_id == nq - 1))(lambda: stage(2))

    kv_spec = pl.BlockSpec((1, tk_all, HD), lambda g, i: (g, 0, 0))
    v1_spec = pl.BlockSpec((1, tk_all, 2 * HD), lambda g, i: (g, 0, 0))
    qspec = pl.BlockSpec((QG, tq, HD), lambda g, i: (g, i, 0))
    sems = [pltpu.SemaphoreType.DMA((6 * ng,)), pltpu.SemaphoreType.DMA((6 * ng,))] if ng else []
    return pl.pallas_call(
        body, name=name, grid=(hkv, nq), in_specs=[qspec, kv_spec, v1_spec] + [_ANY] * ng,
        out_specs=[qspec, pl.BlockSpec((QG, tq, 1), lambda g, i: (g, i, 0))] + [_ANY] * ng,
        out_shape=[_sds((h, tq_all, HD), F32), _sds((h, tq_all, 1), F32)]
        + [_sds((N_CHIP,) + a.shape, a.dtype) for a in gather],
        scratch_shapes=sems, compiler_params=_cp(("arbitrary", "arbitrary")),
    )(q, k, v, *gather)


def _attn_bwd(name, q, k, v, o, do, lse_row, exchange=()):
    h, tq_all, _ = q.shape
    hkv, tk_all, _ = k.shape
    tq, tk = _tile(tq_all, 1024), _tile(tk_all, 1408)
    ne, nq, nk = len(exchange), tq_all // tq, tk_all // tk

    def body(*refs):
        q_ref, k_ref, v_ref, o_ref, do_ref, lse_ref = refs[:6]
        dq_ref, dk_ref, dv_ref = refs[6 + ne:9 + ne]
        dl_ref = refs[9 + 2 * ne]
        g_id, i, j = pl.program_id(0), pl.program_id(1), pl.program_id(2)
        if ne:
            stage = lambda st: _exchange_stage(st, refs[6:6 + ne], refs[9 + ne:9 + 2 * ne], *refs[10 + 2 * ne:])
            first = functools.reduce(jnp.logical_and, [g_id == 0, i == 0, j == 0])
            pl.when(first)(lambda: stage(0))

        @pl.when(j == 0)
        def _():
            ones = jnp.ones((8, HD), F32)
            for g in range(QG):
                dl_ref[g] = hdot(ones, do_ref[g] * o_ref[g], 1, 1)

        kk, vv = k_ref[0], v_ref[0]
        dk_acc, dv_acc = jnp.zeros((tk, HD), F32), jnp.zeros((tk, HD), F32)
        for g in range(QG):
            qv, dob = q_ref[g], do_ref[g].astype(BF16)
            st = lax.dot_general(kk, qv, (((1,), (1,)), ((), ())), preferred_element_type=F32)
            pt = jnp.exp(st - lse_ref[g])
            dv_acc += jnp.dot(pt.astype(BF16), dob, preferred_element_type=F32)
            dpt = lax.dot_general(vv, dob, (((1,), (1,)), ((), ())), preferred_element_type=F32)
            dst = (pt * (dpt - dl_ref[g, 0:1, :])).astype(BF16)
            dk_acc += jnp.dot(dst, qv, preferred_element_type=F32)
            dq_part = lax.dot_general(dst, kk, (((0,), (0,)), ((), ())), preferred_element_type=F32)

            @pl.when(j == 0)
            def _():
                dq_ref[g] = dq_part

            @pl.when(j > 0)
            def _():
                dq_ref[g] += dq_part

        rows = pl.ds(pl.multiple_of(j * tk, tk), tk)

        @pl.when(i == 0)
        def _():
            dk_ref[0, rows, :] = dk_acc
            dv_ref[0, rows, :] = dv_acc

        @pl.when(i > 0)
        def _():
            dk_ref[0, rows, :] += dk_acc
            dv_ref[0, rows, :] += dv_acc

        if ne:
            last = functools.reduce(jnp.logical_and, [g_id == hkv - 1, i == nq - 1, j == nk - 1])
            pl.when(last)(lambda: stage(1))

    ks = pl.BlockSpec((1, tk, HD), lambda g, i, j: (g, j, 0))
    qs = pl.BlockSpec((QG, tq, HD), lambda g, i, j: (g, i, 0))
    rs = pl.BlockSpec((QG, 1, tq), lambda g, i, j: (g, 0, i))
    full = pl.BlockSpec((1, tk_all, HD), lambda g, i, j: (g, 0, 0))
    sems = [pltpu.SemaphoreType.DMA((3 * ne,)), pltpu.SemaphoreType.DMA((3 * ne,))] if ne else []
    return pl.pallas_call(
        body, name=name, grid=(hkv, nq, nk), in_specs=[qs, ks, ks, qs, qs, rs] + [_ANY] * ne,
        out_specs=[qs, full, full] + [_ANY] * ne,
        out_shape=[_sds((h, tq_all, HD), F32), _sds((hkv, tk_all, HD), F32), _sds((hkv, tk_all, HD), F32)]
        + [_sds(a.shape, a.dtype) for a in exchange],
        scratch_shapes=[pltpu.VMEM((QG, 8, tq), F32)] + sems,
        compiler_params=_cp(("arbitrary", "arbitrary", "arbitrary")),
    )(q, k, v, o, do, lse_row, *exchange)


def _decay_fn(a, w2, b2):
    return _log_sigmoid(bdot(a, w2, 1, 0) + b2) / GLA_TAU


def _decay_fwd(name, p, w2, b2):
    t = p.shape[0]
    return _rows_call(name, lambda ids, a, w, b: (_decay_fn(a, w, b),), (t // TM,),
                      [(p, _rowspec(128, OFF["ab"])), (w2, _full((128, 512))), (b2, _full((1, 512)))],
                      [(_sds((t, 512), F32), _rowspec(512), False)])[0]


def _decay_bwd(name, p, w2, b2, gf, gb):
    t = p.shape[0]

    def fn(ids, a, w, b, qf, kf, vf, lf, qb, kb, vb, lb):
        _, vjp = jax.vjp(_decay_fn, a, w, b)
        return vjp(jnp.concatenate([lf, lb], axis=-1)) + (qf + qb, kf + kb, vf + vb)

    widths = (256, 256, 512, 256)
    return _rows_call(name, fn, (t // TM,),
                      [(p, _rowspec(128, OFF["ab"])), (w2, _full((128, 512))), (b2, _full((1, 512)))]
                      + [(g, _rowspec(w)) for g, w in zip(gf, widths)] + [(g, _rowspec(w)) for g, w in zip(gb, widths)],
                      [(_sds((t, 128), BF16), _rowspec(128), False), (_sds((128, 512), F32), _full((128, 512)), True),
                       (_sds((1, 512), F32), _full((1, 512)), True)]
                      + [(_sds((t, w), BF16), _rowspec(w), False) for w in widths[:3]])


def _gla_consts(reverse):
    r = lax.broadcasted_iota(jnp.int32, (GLA_CHUNK, GLA_CHUNK), 0)
    c = lax.broadcasted_iota(jnp.int32, (GLA_CHUNK, GLA_CHUNK), 1)
    trib = (r <= c) if reverse else (r >= c)
    br = lax.broadcasted_iota(jnp.int32, (GLA_QK, GLA_V), 0) // GLA_DK
    bc = lax.broadcasted_iota(jnp.int32, (GLA_QK, GLA_V), 1) // GLA_DV
    lane_head = lax.broadcasted_iota(jnp.int32, (1, GLA_QK), 1) // GLA_DK
    return trib, (br == bc).astype(F32), lane_head


def _gla_chunk(q, k, v, la, s_in, consts):
    trib, bd, lane_head = consts
    q, k = q.astype(F32), k.astype(F32)
    cum = hdot(trib.astype(F32), la)
    tot = jnp.sum(la, axis=0, keepdims=True)
    q_in = q * (GLA_DK ** -0.5) * jnp.exp(cum)
    k_in = k * jnp.exp(-cum)
    k_st = k * jnp.exp(tot - cum)
    outs = []
    for h in range(GLA_H):
        att = bdot(jnp.where(lane_head == h, q_in, 0.0), k_in, 1, 1)
        att = jnp.where(trib, att, 0.0)
        outs.append(bdot(att, v[:, GLA_DV * h:GLA_DV * (h + 1)], 1, 0))
    o = jnp.concatenate(outs, axis=-1) + bdot(q_in, s_in, 1, 0)
    decay = jnp.exp(hdot(la, jnp.ones((GLA_CHUNK, LANE), F32), 0, 0))
    s_out = jnp.concatenate([decay] * (GLA_V // LANE), axis=-1) * s_in + bdot(k_st, v, 0, 0) * bd
    return o, s_out


def _gla_order(nb, reverse, backward):
    if not reverse:
        return (lambda s: nb - 1 - s) if backward else (lambda s: s)
    if backward:
        return lambda s: jnp.where(s == nb - 1, 0, s + 1)
    return lambda s: jnp.where(s == 0, 0, nb - s)


_NCH = TM // GLA_CHUNK


def _gla_specs(nb, reverse, backward):
    order = _gla_order(nb, reverse, backward)
    col = lambda width, off: pl.BlockSpec((TM, width), lambda s, o=off // width: (order(s), o))
    state = pl.BlockSpec((_NCH, GLA_H, GLA_DK, GLA_DV), lambda s: (order(s), 0, 0, 0))
    qkvla = [col(256, OFF["glq"]), col(256, OFF["glk"]), col(512, OFF["glv"]), col(256, 256 * int(reverse))]
    return col, state, qkvla


def _gla_fwd(name, p, la):
    t = p.shape[0]
    nb = t // TM

    def body(*refs):
        ins, outs, scr = (refs[0:4], refs[4:8]), (refs[8:10], refs[10:12]), refs[12:14]

        @pl.when(pl.program_id(0) == 0)
        def _():
            for s_ref in scr:
                s_ref[...] = jnp.zeros_like(s_ref)

        for step in range(_NCH):
            for d in range(2):
                (q_ref, k_ref, v_ref, la_ref), (o_ref, sv_ref), s_ref = ins[d], outs[d], scr[d]
                c = _NCH - 1 - step if d else step
                rows = slice(GLA_CHUNK * c, GLA_CHUNK * (c + 1))
                s_in = s_ref[...]
                for h in range(GLA_H):
                    sv_ref[c, h] = s_in[GLA_DK * h:GLA_DK * (h + 1), GLA_DV * h:GLA_DV * (h + 1)]
                o, s_out = _gla_chunk(q_ref[rows, :], k_ref[rows, :], v_ref[rows, :], la_ref[rows, :], s_in,
                                      _gla_consts(bool(d)))
                o_ref[rows, :] = o
                s_ref[...] = s_out

    in_specs, out_specs, out_shape = [], [], []
    for d in range(2):
        col, state, qkvla = _gla_specs(nb, bool(d), False)
        in_specs += qkvla
        out_specs += [col(512, 0), state]
        out_shape += [_sds((t, GLA_V), F32), _sds((t // GLA_CHUNK, GLA_H, GLA_DK, GLA_DV), F32)]
    return pl.pallas_call(
        body, name=name, grid=(nb,), in_specs=in_specs, out_specs=out_specs, out_shape=out_shape,
        scratch_shapes=[pltpu.VMEM((GLA_QK, GLA_V), F32)] * 2, compiler_params=_cp(("arbitrary",)),
    )(p, p, p, la, p, p, p, la)


def _gla_bwd(name, p, la, sv_f, sv_b, do):
    t = p.shape[0]
    nb = t // TM

    def body(*refs):
        ins, outs, scr = (refs[0:6], refs[6:12]), (refs[12:16], refs[16:20]), refs[20:22]

        @pl.when(pl.program_id(0) == 0)
        def _():
            for ds_ref in scr:
                ds_ref[...] = jnp.zeros_like(ds_ref)

        zero = jnp.zeros((GLA_DK, GLA_DV), F32)
        for step in range(_NCH):
            for d in range(2):
                (q_ref, k_ref, v_ref, la_ref, sv_ref, do_ref), out_refs, ds_ref = ins[d], outs[d], scr[d]
                c = step if d else _NCH - 1 - step
                rows = slice(GLA_CHUNK * c, GLA_CHUNK * (c + 1))
                s_in = jnp.concatenate(
                    [jnp.concatenate([sv_ref[c, h] if hh == h else zero for hh in range(GLA_H)], axis=-1)
                     for h in range(GLA_H)], axis=0)
                consts = _gla_consts(bool(d))
                _, vjp = jax.vjp(lambda a, b, cc, dd, e: _gla_chunk(a, b, cc, dd, e, consts),
                                 q_ref[rows, :], k_ref[rows, :], v_ref[rows, :], la_ref[rows, :], s_in)
                grads = vjp((do_ref[rows, :], ds_ref[...]))
                for o_ref, g in zip(out_refs, grads[:4]):
                    o_ref[rows, :] = g.astype(o_ref.dtype)
                ds_ref[...] = grads[4]

    ins, in_specs, out_specs, out_shape = [], [], [], []
    for d, sv in enumerate((sv_f, sv_b)):
        col, state, qkvla = _gla_specs(nb, bool(d), True)
        ins += [p, p, p, la, sv, do]
        in_specs += qkvla + [state, col(512, 0)]
        out_specs += [col(256, 0), col(256, 0), col(512, 0), col(256, 0)]
        out_shape += [_sds((t, GLA_QK), F32), _sds((t, GLA_QK), F32), _sds((t, GLA_V), F32), _sds((t, GLA_QK), F32)]
    return pl.pallas_call(
        body, name=name, grid=(nb,), in_specs=in_specs, out_specs=out_specs, out_shape=out_shape,
        scratch_shapes=[pltpu.VMEM((GLA_QK, GLA_V), F32)] * 2, compiler_params=_cp(("arbitrary",)),
    )(*ins)


def _gla_out_fn(of, ob, r, g):
    o = of + ob
    cols = [_rms(o[:, GLA_DV * h:GLA_DV * (h + 1)], g[:, GLA_DV * h:GLA_DV * (h + 1)]) for h in range(GLA_H)]
    return jnp.concatenate(cols, axis=-1) * jax.nn.silu(r.astype(F32))


def _gla_out_fwd(name, of, ob, p, g):
    t = p.shape[0]
    return _rows_call(name, lambda ids, *a: (_gla_out_fn(*a),), (t // TM,),
                      [(of, _rowspec(512)), (ob, _rowspec(512)), (p, _rowspec(512, OFF["gr"])), (g, _full((1, 512)))],
                      [(_sds((t, 512), BF16), _rowspec(512), False)])[0]


def _gla_out_bwd(name, of, ob, p, g, dgla):
    t = p.shape[0]

    def fn(ids, a, b, r, gv, dv):
        _, vjp = jax.vjp(_gla_out_fn, a, b, r, gv)
        do, _, dr, dg = vjp(dv)
        return do, dr, dg

    return _rows_call(name, fn, (t // TM,),
                      [(of, _rowspec(512)), (ob, _rowspec(512)), (p, _rowspec(512, OFF["gr"])), (g, _full((1, 512))),
                       (dgla, _rowspec(512))],
                      [(_sds((t, 512), F32), _rowspec(512), False), (_sds((t, 512), BF16), _rowspec(512), False),
                       (_sds((1, 512), F32), _full((1, 512)), True)])


_TMM = 384


def _merge_fwd(name, gm, att, gla, wa, wb, wc, p):
    t = p.shape[0]
    row = lambda w, off=0: pl.BlockSpec((_TMM, w), lambda i, o=off // w: (i, o))

    def fn(ids, a, b, c, wa_, wb_, wc_, ga, gb, gc):
        ga, gb, gc = ga.astype(F32), gb.astype(F32), gc.astype(F32)
        return (jax.nn.sigmoid(ga) * bdot(a, wa_, 1, 0) + jax.nn.sigmoid(gb) * bdot(b, wb_, 1, 0)
                + jax.nn.sigmoid(gc) * bdot(c, wc_, 1, 0),)

    return _rows_call(name, fn, (t // _TMM,),
                      [(gm, row(512)), (att, row(512)), (gla, row(512)), (wa, _full((512, D))), (wb, _full((512, D))),
                       (wc, _full((512, D))), (p, row(D, OFF["gA"])), (p, row(D, OFF["gB"])), (p, row(D, OFF["gC"]))],
                      [(_sds((t, D), BF16), row(D), False)])[0]


def _merge_bwd(name, gm, att, gla, wa, wb, wc, p, dmerged):
    t = p.shape[0]
    row = lambda w, off=0: pl.BlockSpec((_TMM, w), lambda i, o=off // w: (i, o))

    def fn(ids, a, b, c, wa_, wb_, wc_, ga, gb, gc, dm):
        ga, gb, gc = ga.astype(F32), gb.astype(F32), gc.astype(F32)
        outs_y, outs_g = [], []
        for br, w, g in ((a, wa_, ga), (b, wb_, gb), (c, wc_, gc)):
            s = jax.nn.sigmoid(g)
            outs_y.append(dm * s)
            outs_g.append(dm * bdot(br, w, 1, 0) * s * (1.0 - s))
        return tuple(outs_y) + tuple(outs_g)

    o = (_sds((t, D), BF16), row(D), False)
    return _rows_call(name, fn, (t // _TMM,),
                      [(gm, row(512)), (att, row(512)), (gla, row(512)), (wa, _full((512, D))), (wb, _full((512, D))),
                       (wc, _full((512, D))), (p, row(D, OFF["gA"])), (p, row(D, OFF["gB"])), (p, row(D, OFF["gC"])),
                       (dmerged, row(D))], [o] * 6)


_TNC = 1408
_NJ = FFN // _TNC


HALO = 16


def _shift_rows(x, prev, nxt, vp, vn):
    n = x.shape[0]
    rid = lax.broadcasted_iota(jnp.int32, x.shape, 0)
    xp = jnp.where(rid == 0, jnp.where(vp, prev[HALO - 1:HALO, :], 0.0), pltpu.roll(x, 1, 0))
    xn = jnp.where(rid == n - 1, jnp.where(vn, nxt[0:1, :], 0.0), pltpu.roll(x, n - 1, 0))
    return xp, xn


def _seq_edges(i, t):
    start, end = i * TM, (i + 1) * TM
    return jnp.logical_and(start != 0, start != TC), jnp.logical_and(end != TC, end != t)


def _halo_specs(t, colmap):
    r = TM // HALO
    main = pl.BlockSpec((TM, _TNC), lambda j, i: (i, colmap(j)))
    prev = pl.BlockSpec((HALO, _TNC), lambda j, i: (jnp.maximum(i * r - 1, 0), colmap(j)))
    nxt = pl.BlockSpec((HALO, _TNC), lambda j, i: (jnp.minimum((i + 1) * r, t // HALO - 1), colmap(j)))
    return [main, prev, nxt]


def _conv3(x, xp, xn, w, b=None):
    y = xp * w[0:1, :] + x * w[1:2, :] + xn * w[2:3, :]
    return y if b is None else b + y


def _conv_fwd(name, a, cw, cb):
    t = a.shape[0]

    def fn(ids, ag, agp, agn, av, avp, avn, wg, wv, bg, bv):
        vp, vn = _seq_edges(ids[1], t)
        ag, agp, agn, av, avp, avn = (z.astype(F32) for z in (ag, agp, agn, av, avp, avn))
        cg = _conv3(ag, *_shift_rows(ag, agp, agn, vp, vn), wg, bg)
        cv = _conv3(av, *_shift_rows(av, avp, avn, vp, vn), wv, bv)
        return (jax.nn.silu(cg) * cv,)

    gcol, vcol = (lambda j: j), (lambda j: j + _NJ)
    wspec = lambda cm: pl.BlockSpec((3, _TNC), lambda j, i: (0, cm(j)))
    bspec = lambda cm: pl.BlockSpec((1, _TNC), lambda j, i: (0, cm(j)))
    ins = [(a, s) for s in _halo_specs(t, gcol) + _halo_specs(t, vcol)]
    ins += [(cw, wspec(gcol)), (cw, wspec(vcol)), (cb, bspec(gcol)), (cb, bspec(vcol))]
    return _rows_call(name, fn, (_NJ, t // TM), ins,
                      [(_sds((t, FFN), BF16), pl.BlockSpec((TM, _TNC), lambda j, i: (i, j)), False)])[0]


def _conv_bwd(name, a, cw, cb, dact):
    t = a.shape[0]
    n = TM + 2 * HALO

    def fn(ids, ag, agp, agn, av, avp, avn, dv, dvp, dvn, wg, wv, bg, bv):
        vp, vn = _seq_edges(ids[1], t)
        ag, agp, agn, av, avp, avn = (z.astype(F32) for z in (ag, agp, agn, av, avp, avn))
        ext = lambda x, xp, xn: jnp.concatenate([jnp.where(vp, xp, 0.0), x, jnp.where(vn, xn, 0.0)], axis=0)
        up, dn = (lambda x: pltpu.roll(x, 1, 0)), (lambda x: pltpu.roll(x, n - 1, 0))
        main = lambda y: y[HALO:HALO + TM]
        eg, ev, ed = ext(ag, agp, agn), ext(av, avp, avn), ext(dv, dvp, dvn)
        cg = _conv3(eg, up(eg), dn(eg), wg, bg)
        cv = _conv3(ev, up(ev), dn(ev), wv, bv)
        s = jax.nn.sigmoid(cg)
        rid = lax.broadcasted_iota(jnp.int32, (3, eg.shape[1]), 0)
        das, dws, dbs = [], [], []
        for dc, w, x in ((ed * cv * s * (1.0 + cg * (1.0 - s)), wg, eg), (ed * cg * s, wv, ev)):
            shifted = [main(dn(dc)), main(dc), main(up(dc))]
            das.append(shifted[0] * w[0:1, :] + shifted[1] * w[1:2, :] + shifted[2] * w[2:3, :])
            sums = [jnp.sum(y * main(x), axis=0, keepdims=True) for y in shifted]
            dws.append(jnp.where(rid == 0, sums[0], jnp.where(rid == 1, sums[1], sums[2])))
            dbs.append(jnp.sum(shifted[1], axis=0, keepdims=True))
        return jnp.stack(das), jnp.stack(dws), jnp.stack(dbs)

    gcol, vcol = (lambda j: j), (lambda j: j + _NJ)
    wspec = lambda cm: pl.BlockSpec((3, _TNC), lambda j, i: (0, cm(j)))
    bspec = lambda cm: pl.BlockSpec((1, _TNC), lambda j, i: (0, cm(j)))
    ins = [(a, s) for s in _halo_specs(t, gcol) + _halo_specs(t, vcol)] + [(dact, s) for s in _halo_specs(t, gcol)]
    ins += [(cw, wspec(gcol)), (cw, wspec(vcol)), (cb, bspec(gcol)), (cb, bspec(vcol))]
    return _rows_call(name, fn, (_NJ, t // TM), ins,
                      [(_sds((2, t, FFN), BF16), pl.BlockSpec((2, TM, _TNC), lambda j, i: (0, i, j)), False),
                       (_sds((2, 3, FFN), F32), pl.BlockSpec((2, 3, _TNC), lambda j, i: (0, 0, j)), True),
                       (_sds((2, 1, FFN), F32), pl.BlockSpec((2, 1, _TNC), lambda j, i: (0, 0, j)), True)])


_TNA = 512


def _adaln_fwd(name, cond, w, b):
    fn = lambda ids, cv, wv, bv: ((bdot(jax.nn.silu(cv), wv[0], 1, 0) + bv[0])[None],)
    return _rows_call(name, fn, (2, ADA_LOC // _TNA),
                      [(cond, _full((16, D))), (w, pl.BlockSpec((1, D, _TNA), lambda l, j: (l, 0, j))),
                       (b, pl.BlockSpec((1, 1, _TNA), lambda l, j: (l, 0, j)))],
                      [(_sds((2, 16, ADA_LOC), F32), pl.BlockSpec((1, 16, _TNA), lambda l, j: (l, 0, j)), False)])[0]


def _adaln_bwd(name, c8, cc8, w, dl, dc):
    def fn(ids, cv, ccv, wv, dlv, dcv):
        dcs = jnp.broadcast_to(jnp.sum(dcv[0], axis=0, keepdims=True), dcv[0].shape)
        dw = hdot(jax.nn.silu(cv), dlv[0], 0, 0) + hdot(jax.nn.silu(ccv), dcs, 0, 0)
        s = jax.nn.sigmoid(ccv)
        rid = lax.broadcasted_iota(jnp.int32, ccv.shape, 0)
        dcc = jnp.where(rid == 0, bdot(dcs, wv[0], 1, 1) * s * (1.0 + ccv * (1.0 - s)), 0.0)
        return dw[None], dcc

    dspec = pl.BlockSpec((1, 8, _TNA), lambda l, j: (l, 0, j))
    return _rows_call(name, fn, (2, ADA_LOC // _TNA),
                      [(c8, _full((8, D))), (cc8, _full((8, D))), (w, pl.BlockSpec((1, D, _TNA), lambda l, j: (l, 0, j))),
                       (dl, dspec), (dc, dspec)],
                      [(_sds((2, D, ADA_LOC), F32), pl.BlockSpec((1, D, _TNA), lambda l, j: (l, 0, j)), False),
                       (_sds((8, D), F32), _full((8, D)), True)], acc_axes=(0, 1))


def _adamw_fn(w, g, m, v):
    m = ADAM_B1 * m + (1.0 - ADAM_B1) * g
    v = ADAM_B2 * v + (1.0 - ADAM_B2) * (g * g)
    m_hat = m / (1.0 - ADAM_B1 ** ADAM_STEP)
    v_hat = v / (1.0 - ADAM_B2 ** ADAM_STEP)
    return -ADAM_LR * (m_hat / (jnp.sqrt(v_hat) + ADAM_EPS) + ADAM_WD * w), m, v


def _adamw(name, w, g, m, v):
    l, r, c = w.shape
    tr = _tile(r, max(8, (1 << 20) // (4 * c)), 8)
    spec = pl.BlockSpec((None, tr, c), lambda i, j: (i, j, 0))
    o = (_sds((l, r, c), F32), spec, False)
    return _rows_call(name, lambda ids, *a: _adamw_fn(*a), (l, r // tr), [(x, spec) for x in (w, g, m, v)], [o, o, o],
                      sem=("parallel", "parallel"))


def _coords():
    return lax.axis_index("x"), lax.axis_index("y"), lax.axis_index("c")


def _other_chips(x, y):
    return [(1 - x, y), (x, 1 - y), (1 - x, 1 - y)]


def _allgather_small(name, blk):
    m_per, n = blk.shape

    def body(x_ref, out_ref, send_sems, recv_sems, local_sem):
        x, y, c = _coords()
        me, sibling = (x, y, c), (x, y, 1 - c)
        chips = _other_chips(x, y)

        def rows(px, py, pc):
            return out_ref.at[pl.ds((4 * px + 2 * py + pc) * m_per, m_per), :]

        def copy(k, block, to, src=None):
            return pltpu.make_async_remote_copy(
                src_ref=rows(*block) if src is None else src, dst_ref=rows(*block), send_sem=send_sems.at[k],
                recv_sem=recv_sems.at[k], device_id=to, device_id_type=MESH)

        mine = pltpu.make_async_copy(x_ref, rows(*me), local_sem)
        mine.start()
        first = [copy(0, me, sibling, src=x_ref)]
        first += [copy(1 + j, me, (*chip, c), src=x_ref) for j, chip in enumerate(chips)]
        for cp in first:
            cp.start()
        passed = [copy(4 + j, (*chip, c), sibling) for j, chip in enumerate(chips)]
        for j, chip in enumerate(chips):
            copy(1 + j, (*chip, c), me).wait_recv()
            passed[j].start()
        copy(0, sibling, me).wait_recv()
        for j, chip in enumerate(chips):
            copy(4 + j, (*chip, 1 - c), me).wait_recv()
        for cp in first + passed:
            cp.wait_send()
        mine.wait()

    return pl.pallas_call(
        body, name=name, out_shape=_sds((N_DEV * m_per, n), blk.dtype),
        in_specs=[pl.BlockSpec(memory_space=pltpu.VMEM)], out_specs=pl.BlockSpec(memory_space=pltpu.VMEM),
        scratch_shapes=[pltpu.SemaphoreType.DMA((7,)), pltpu.SemaphoreType.DMA((7,)), pltpu.SemaphoreType.DMA],
        compiler_params=pltpu.CompilerParams(vmem_limit_bytes=VMEM_LIMIT),
    )(blk)


_ANY = pl.BlockSpec(memory_space=pl.ANY)


def _remote(src, dst, send_sems, recv_sems, s, to):
    return pltpu.make_async_remote_copy(src_ref=src, dst_ref=dst, send_sem=send_sems.at[s], recv_sem=recv_sems.at[s],
                                        device_id=to, device_id_type=MESH)


def _comm_call(name, body, ins, out_shapes, n_sems, n_local):
    return pl.pallas_call(
        body, name=name, out_shape=out_shapes, in_specs=[_ANY] * len(ins), out_specs=[_ANY] * len(out_shapes),
        scratch_shapes=[pltpu.SemaphoreType.DMA((n_sems,)), pltpu.SemaphoreType.DMA((n_sems,)),
                        pltpu.SemaphoreType.DMA((n_local,))],
    )(*ins)


def _gather_stage(stage, ins, outs, send_sems, recv_sems):
    n = len(ins)
    x, y, c = _coords()
    k = 2 * x + y
    sibling = (x, y, 1 - c)
    chips = _other_chips(x, y)
    first = [_remote(ins[t].at[c], outs[t].at[k, c], send_sems, recv_sems, 6 * t + j, (*chip, c))
             for t in range(n) for j, chip in enumerate(chips)]
    there = lambda t, j, half: outs[t].at[2 * chips[j][0] + chips[j][1], half]
    passed = [_remote(there(t, j, c), there(t, j, c), send_sems, recv_sems, 6 * t + 3 + j, sibling)
              for t in range(n) for j in range(3)]
    if stage == 0:
        for cp in first:
            cp.start()
    elif stage == 1:
        for t in range(n):
            for j in range(3):
                _remote(there(t, j, c), there(t, j, c), send_sems, recv_sems, 6 * t + j, sibling).wait_recv()
                passed[3 * t + j].start()
    else:
        for t in range(n):
            for j in range(3):
                _remote(there(t, j, 1 - c), there(t, j, 1 - c), send_sems, recv_sems, 6 * t + 3 + j, sibling).wait_recv()
        for cp in first + passed:
            cp.wait_send()


def _gather_own(outs, locs):
    k = 2 * lax.axis_index("x") + lax.axis_index("y")
    return [lax.dynamic_update_slice_in_dim(o, a[None], k, axis=0) for o, a in zip(outs, locs)]


def _allgather_layers(name, locs):
    n = len(locs)

    def body(*refs):
        ins, outs, (send_sems, recv_sems, _) = refs[:n], refs[n:2 * n], refs[2 * n:]
        for stage in range(3):
            _gather_stage(stage, ins, outs, send_sems, recv_sems)

    return _gather_own(_comm_call(name, body, locs, [_sds((N_CHIP,) + a.shape, a.dtype) for a in locs], 6 * n, 1), locs)


def _rs_pair_exchange(name, gs):
    n = len(gs)

    def body(*refs):
        ins, outs, (send_sems, recv_sems, _) = refs[:n], refs[n:2 * n], refs[2 * n:]
        x, y, c = _coords()
        cps = [_remote(ins[t].at[kk, 1 - c], outs[t].at[kk], send_sems, recv_sems, N_CHIP * t + kk, (x, y, 1 - c))
               for t in range(n) for kk in range(N_CHIP)]
        for cp in cps:
            cp.start()
        for cp in cps:
            cp.wait()

    return _comm_call(name, body, gs, [_sds((N_CHIP,) + a.shape[2:], a.dtype) for a in gs], N_CHIP * n, 1)


def _rs_pair_add(name, g, got):
    _, _, r, c = g.shape
    tr = _tile(r, max(16, (1 << 20) // (4 * c)), 16)
    half = lambda h: pl.BlockSpec((None, None, tr, c), lambda kk, i: (kk, h, i, 0))
    spec = pl.BlockSpec((None, tr, c), lambda kk, i: (kk, i, 0))
    fn = lambda ids, a0, a1, rr: (jnp.where(lax.axis_index("c") == 0, a0, a1) + rr,)
    return _rows_call(name, fn, (N_CHIP, r // tr), [(g, half(0)), (g, half(1)), (got, spec)],
                      [(_sds((N_CHIP, r, c), BF16), spec, False)], sem=("parallel", "parallel"))[0]


def _exchange_stage(stage, ins, outs, send_sems, recv_sems):
    n = len(ins)
    x, y, c = _coords()
    k = 2 * x + y
    chips = _other_chips(x, y)
    cps = [_remote(ins[t].at[2 * cx + cy], outs[t].at[k], send_sems, recv_sems, 3 * t + j, (cx, cy, c))
           for t in range(n) for j, (cx, cy) in enumerate(chips)]
    if stage == 0:
        for cp in cps:
            cp.start()
    else:
        for t in range(n):
            for j, (cx, cy) in enumerate(chips):
                there = outs[t].at[2 * cx + cy]
                _remote(there, there, send_sems, recv_sems, 3 * t + j, (cx, cy, c)).wait_recv()
        for cp in cps:
            cp.wait_send()


def _exchange_own(outs, s1):
    k = 2 * lax.axis_index("x") + lax.axis_index("y")
    own = [lax.dynamic_index_in_dim(a, k, axis=0, keepdims=True) for a in s1]
    return [lax.dynamic_update_slice_in_dim(o, a, k, axis=0) for o, a in zip(outs, own)]


def _rs_chip_exchange(name, s1):
    n = len(s1)

    def body(*refs):
        ins, outs, (send_sems, recv_sems, _) = refs[:n], refs[n:2 * n], refs[2 * n:]
        for stage in range(2):
            _exchange_stage(stage, ins, outs, send_sems, recv_sems)

    return _exchange_own(_comm_call(name, body, s1, [_sds(a.shape, a.dtype) for a in s1], 3 * n, 1), s1)


def _sum_slots(name, a):
    s, r, cdim = a.shape
    tr = _tile(r, 512, 8)

    def fn(ids, av):
        tot = av[0]
        for i in range(1, s):
            tot = tot + av[i]
        return (tot,)

    return _rows_call(name, fn, (r // tr,), [(a, pl.BlockSpec((s, tr, cdim), lambda i: (0, i, 0)))],
                      [(_sds((r, cdim), F32), pl.BlockSpec((tr, cdim), lambda i: (i, 0)), False)], sem=("parallel",))[0]


def _pair_allgather(name, red0, red1):
    n = len(red0)

    def body(*refs):
        ins, outs, (send_sems, recv_sems, _) = (refs[:n], refs[n:2 * n]), refs[2 * n:3 * n], refs[3 * n:]
        x, y, c = _coords()
        cps = [_remote(ins[l][t], outs[t].at[l, c], send_sems, recv_sems, 2 * t + l, (x, y, 1 - c))
               for t in range(n) for l in range(2)]
        for cp in cps:
            cp.start()
        for t in range(n):
            for l in range(2):
                _remote(ins[l][t], outs[t].at[l, 1 - c], send_sems, recv_sems, 2 * t + l, (x, y, 1 - c)).wait_recv()
        for cp in cps:
            cp.wait_send()

    outs = _comm_call(name, body, list(red0) + list(red1), [_sds((2, 2) + a.shape, a.dtype) for a in red0], 2 * n, 1)
    c = lax.axis_index("c")
    own = [jnp.stack([a, b])[:, None] for a, b in zip(red0, red1)]
    return [lax.dynamic_update_slice_in_dim(o, a, c, axis=1) for o, a in zip(outs, own)]


def _rs_front(tag, gs):
    halves = [g.reshape(N_CHIP, 2, g.shape[1] // 2, g.shape[2]) for g in gs]
    got = _rs_pair_exchange(tag + "rs_pair_exchange", halves)
    return [_rs_pair_add(tag + "rs_pair_add_%d" % t, h, r) for t, (h, r) in enumerate(zip(halves, got))]


def _rs_sum(tag, slots):
    return [_sum_slots(tag + "rs_chip_sum_%d" % t, a) for t, a in enumerate(slots)]


PACK_C = 1024
_SHARDED = (("w_in", 1), ("w_br_a", 1), ("w_br_b", 1), ("w_br_c", 1), ("w_out", 0), ("w_ffn_up", 1), ("w_ffn_down", 0))
_SHARDED_SMALL = (("conv_w", (3, 2 * FFN), 1), ("w_alpha2", (2, 16, GLA_QK), 2), ("b_alpha", (2, GLA_QK), 1))


def _prod(shape):
    n = 1
    for s in shape:
        n *= s
    return n


def _to_blocks(full, axis):
    shp = full.shape
    split = full.reshape(shp[:axis] + (N_CHIP, shp[axis] // N_CHIP) + shp[axis + 1:])
    return jnp.moveaxis(split, axis, 0)


def _from_blocks(blocks, axis):
    return jnp.concatenate([blocks[k] for k in range(N_CHIP)], axis=axis)


def _rope_tables(tx):
    pos = jnp.arange(tx, dtype=jnp.int32)
    inv_freq = 10000.0 ** (-jnp.arange(16, dtype=F32) / 16)
    ang_r = (pos // GRID_W).astype(F32)[:, None] * inv_freq
    ang_c = (pos % GRID_W).astype(F32)[:, None] * inv_freq
    ang = jnp.concatenate([ang_r, ang_r, ang_c, ang_c], axis=-1)
    sign = jnp.concatenate([-jnp.ones((16,), F32), jnp.ones((16,), F32)] * 2)
    cos = jnp.concatenate([jnp.ones((TC, HD), F32), jnp.cos(ang)], axis=0)
    sin = jnp.concatenate([jnp.zeros((TC, HD), F32), jnp.sin(ang) * sign], axis=0)
    return jnp.tile(cos, (1, 2)), jnp.tile(sin, (1, 2))


def _lane_consts():
    l = jnp.arange(512)
    seg = (l[:, None] // HD == l[None, :] // HD).astype(F32) / HD
    partner = jnp.where(l % 32 < 16, l + 16, l - 16)
    perm = (l[:, None] == partner[None, :]).astype(F32)
    return seg, perm


def _heads(a, n):
    return a.reshape(a.shape[0], n, HD).transpose(1, 0, 2)


def _unheads(a):
    return a.transpose(1, 0, 2).reshape(a.shape[1], a.shape[0] * HD)


def _gather_f32_shards(shards):
    sizes = [_prod(a.shape) for a in shards]
    flat = jnp.concatenate([a.reshape(-1) for a in shards] + [jnp.zeros((16 * PACK_C - sum(sizes),), F32)])
    got = _allgather_small("gather_f32_shards", flat.reshape(16, PACK_C)).reshape(N_CHIP, 2, 16 * PACK_C)[:, 0]
    out, o = {}, 0
    for (n, _, ax), a, sz in zip(_SHARDED_SMALL, shards, sizes):
        out[n] = jnp.concatenate([got[k, o:o + sz].reshape(a.shape) for k in range(N_CHIP)], axis=ax + 1)
        o += sz
    return out


def _halves(a):
    return a.reshape(2, a.shape[0] // 2, a.shape[1])


def _layer_shards(W, l):
    return [_halves(W[n][l].astype(BF16)) for n, _ in _SHARDED]


def _layer_params(l, gathered, small):
    w2 = small["w_alpha2_full"][l]
    w2pad = jnp.zeros((128, 512), F32).at[0:16, 0:256].set(w2[0]).at[16:32, 256:512].set(w2[1])
    full = {n: _from_blocks(g.reshape(N_CHIP, 2 * g.shape[2], g.shape[3]), ax) for (n, ax), g in zip(_SHARDED, gathered)}
    return dict(
        w_in=_to_new_cols(full["w_in"]), wa=full["w_br_a"], wb=full["w_br_b"], wc=full["w_br_c"],
        w_out=full["w_out"], w_up=full["w_ffn_up"], w_down=full["w_ffn_down"],
        cw=small["conv_w_full"][l], cb=small["conv_b"][l][None], w2=w2pad,
        b2=small["b_alpha_full"][l].reshape(1, 512),
        g1=small["norm1_g"][l][None], g2=small["norm2_g"][l][None], gq=jnp.tile(small["q_norm_g"][l], 8)[None],
        gk=jnp.tile(small["k_norm_g"][l], 2)[None], ggm=small["gmlp_norm_g"][l][None], ws=small["w_spatial"][l],
        bst=small["b_spatial"][l].T, ggl=small["gla_norm_g"][l][None])


def _layer_fwd(l, last, x, h1, mod, P, tabs, gather=()):
    cos, sin, seg, perm = tabs
    n = "l%d_" % l
    s = dict(x=x, h1=h1)
    p = _mm(n + "in_proj", h1, P["w_in"], "nn", BF16, tm_t=768, tn_t=2176, j_outer=True)
    s["p"] = p
    s["gm"] = _gmlp_fwd(n + "gmlp", p, P["ggm"], P["ws"], P["bst"])
    qr, kr, vb = _qk_fwd(n + "qk_prep", p, P["gq"], P["gk"], cos, sin, seg, perm)
    qx, qc, kh, vh = _heads(qr[TC:], NQ), _heads(qr[:TC], NQ), _heads(kr, NKV), _heads(vb, NKV)
    s["qx"], s["qc"], s["kh"], s["vh"] = qx, qc, kh, vh
    one_hot = (jnp.arange(HD) == 0).astype(BF16)
    v1 = jnp.concatenate([vh, jnp.broadcast_to(one_hot, vh.shape)], axis=-1)
    ox, lse_x, *s["gathered"] = _attn_fwd(n + "attn_x", qx, kh, v1, gather)
    s["ox"], s["lse_x"] = ox, lse_x
    if last:
        oc = jnp.zeros((NQ, TC, HD), F32)
    else:
        oc, lse_c = _attn_fwd(n + "attn_c", qc, kh[:, :TC], v1[:, :TC])
        s["oc"], s["lse_c"] = oc, lse_c
    s["att"] = jnp.concatenate([_unheads(oc), _unheads(ox)], axis=0).astype(BF16)
    la = _decay_fwd(n + "gla_decay", p, P["w2"], P["b2"])
    s["la"] = la
    s["of"], s["sf"], s["ob"], s["sb"] = _gla_fwd(n + "gla_scan", p, la)
    s["gla"] = _gla_out_fwd(n + "gla_out", s["of"], s["ob"], p, P["ggl"])
    s["merged"] = _merge_fwd(n + "merge", s["gm"], s["att"], s["gla"], P["wa"], P["wb"], P["wc"], p)
    s["mix"] = _mm(n + "out_proj", s["merged"], P["w_out"], "nn", F32)
    s["x_mid"], s["h2"] = _res_nm_fwd(n + "res1_norm2", x, s["mix"], mod, 2, mod, P["g2"], 3, 4)
    s["a"] = _mm(n + "ffn_up", s["h2"], P["w_up"], "nn", BF16, j_outer=True)
    s["act"] = _conv_fwd(n + "conv_gate", s["a"], P["cw"], P["cb"])
    s["f"] = _mm(n + "ffn_down", s["act"], P["w_down"], "nn", F32)
    return s


def _layer_bwd(l, last, s, mod, P, tabs, dx_mid, df, gw, exchange=()):
    cos, sin, seg, perm = tabs
    n = "l%d_b_" % l
    t = dx_mid.shape[0]
    p = s["p"]
    gw["w_ffn_down"] = _mm(n + "ffn_down_w", s["act"], df, "tn", F32, tm_t=1408)
    dact = _mm(n + "ffn_down_x", df, P["w_down"], "nt", F32)
    da, dcw, dcb = _conv_bwd(n + "conv_gate", s["a"], P["cw"], P["cb"], dact)
    gw["conv_w"], gw["conv_b"] = dcw.transpose(1, 0, 2).reshape(3, 2 * FFN), dcb.reshape(2 * FFN)
    gw["w_ffn_up"] = _mm(n + "ffn_up_w", s["h2"], da, "tn", F32, chip_blocks=True)
    dh2 = _mm(n + "ffn_up_x", da, P["w_up"], "nt", F32)
    dx, dmix, dmod_a, dmod_b, dg2 = _res_nm_bwd(n + "res1_norm2", s["x"], s["mix"], mod, 2, mod, P["g2"], 3, 4, dx_mid, dh2)
    dmod = dmod_a + dmod_b
    gw["norm2_g"] = dg2[0]
    gw["w_out"] = _mm(n + "out_proj_w", s["merged"], dmix, "tn", F32)
    dmerged = _mm(n + "out_proj_x", dmix, P["w_out"], "nt", F32)
    dya, dyb, dyc, dga, dgb, dgc = _merge_bwd(n + "merge", s["gm"], s["att"], s["gla"], P["wa"], P["wb"], P["wc"], p, dmerged)
    gw["w_br_a"] = _mm(n + "br_a_w", s["gm"], dya, "tn", F32)
    gw["w_br_b"] = _mm(n + "br_b_w", s["att"], dyb, "tn", F32)
    gw["w_br_c"] = _mm(n + "br_c_w", s["gla"], dyc, "tn", F32)
    dgm = _mm(n + "br_a_x", dya, P["wa"], "nt", F32)
    datt = _mm(n + "br_b_x", dyb, P["wb"], "nt", F32)
    dgla = _mm(n + "br_c_x", dyc, P["wc"], "nt", F32)
    du, dv_g, dggm, dws, dbst = _gmlp_bwd(n + "gmlp", p, P["ggm"], P["ws"], P["bst"], dgm)
    gw["gmlp_norm_g"], gw["w_spatial"], gw["b_spatial"] = dggm[0], dws, dbst.T
    kh, vh = s["kh"], s["vh"]
    row = lambda a: a.reshape(a.shape[0], 1, a.shape[1])
    dqx, dkh, dvh, *gw["exchanged"] = _attn_bwd(n + "attn_x", s["qx"], kh, vh, s["ox"], _heads(datt[TC:], NQ),
                                                row(s["lse_x"]), exchange)
    if last:
        dqc = jnp.zeros((NQ, TC, HD), F32)
    else:
        dqc, dkc, dvc = _attn_bwd(n + "attn_c", s["qc"], kh[:, :TC], vh[:, :TC], s["oc"], _heads(datt[:TC], NQ),
                                  row(s["lse_c"]))
        pad = jnp.zeros((NKV, t - TC, HD), F32)
        dkh = dkh + jnp.concatenate([dkc, pad], axis=1)
        dvh = dvh + jnp.concatenate([dvc, pad], axis=1)
    dqr = jnp.concatenate([_unheads(dqc), _unheads(dqx)], axis=0)
    dq, dk, dgq, dgk = _qk_bwd(n + "qk_prep", p, P["gq"], P["gk"], cos, sin, seg, perm, dqr, _unheads(dkh))
    gw["q_norm_g"], gw["k_norm_g"] = dgq.reshape(8, HD).sum(0), dgk.reshape(2, HD).sum(0)
    dv_att = _unheads(dvh).astype(BF16)
    do, dr, dggl = _gla_out_bwd(n + "gla_out", s["of"], s["ob"], p, P["ggl"], dgla)
    gw["gla_norm_g"] = dggl[0]
    scans = _gla_bwd(n + "gla_scan", p, s["la"], s["sf"], s["sb"], do)
    dab, dw2, db2, dglq, dglk, dglv = _decay_bwd(n + "gla_decay", p, P["w2"], P["b2"], scans[:4], scans[4:])
    gw["w_alpha2"] = jnp.stack([dw2[0:16, 0:256], dw2[16:32, 256:512]])
    gw["b_alpha"] = db2.reshape(2, 256)
    dp = jnp.concatenate([dga, dgb, dgc, du, dv_g, dq, dglv, dr, dglq, dglk, dk, dv_att, dab], axis=-1)
    gw["w_in"] = _to_ref_cols(_mm(n + "in_proj_w", s["h1"], dp, "tn", F32, tn_t=2176, tk_t=768))
    dh1 = _mm(n + "in_proj_x", dp, P["w_in"], "nt", F32, tk_t=2176)
    return dx, dh1, dmod


_SMALL = (("norm1_g", (2, D)), ("norm2_g", (2, D)), ("q_norm_g", (2, HD)), ("k_norm_g", (2, HD)), ("gmlp_norm_g", (2, GW)),
          ("gla_norm_g", (2, GLA_V)), ("w_spatial", (2, 4, 128, 128)), ("b_spatial", (2, 4, 128)), ("conv_b", (2, 2 * FFN)),
          ("final_norm_g", (D,))) + tuple((n, (2,) + s) for n, s, _ in _SHARDED_SMALL)
_SMALL_N = 2 * 2 * ADA_W + sum(_prod(s) for _, s in _SMALL)
_SMALL_R = -(-_SMALL_N // (PACK_C * 8)) * 8


def _mod_tables(c, c_ctx, w_ada, b_ada, k):
    x, y, cc = _coords()
    me = 4 * x + 2 * y + cc
    c_all = _allgather_small("gather_c", jnp.concatenate([c, jnp.zeros((7, D), F32)], axis=0))
    c8 = c_all.reshape(N_DEV, 8, D)[:, 0]
    cond = jnp.concatenate([c8, c_ctx[None], jnp.zeros((7, D), F32)], axis=0)
    b_loc = lax.dynamic_slice_in_dim(b_ada, k * ADA_LOC, ADA_LOC, axis=1)[:, None, :]
    m_loc = _adaln_fwd("adaln", cond, w_ada, b_loc)
    m_all = _allgather_small("gather_mod", m_loc.reshape(32, ADA_LOC)).reshape(N_CHIP, 2, 2, 16, ADA_LOC)[:, 0]
    m_all = m_all.transpose(1, 2, 0, 3).reshape(2, 16, ADA_W)
    rows = jnp.stack([m_all[:, 8], lax.dynamic_index_in_dim(m_all, me, axis=1, keepdims=False)], axis=1)
    return rows.reshape(2, 2, 6, D), c8


def _step(x, c, ctx, c_ctx, W, tgt):
    xc, yc, cc = _coords()
    k = 2 * xc + yc
    tx = x.shape[0]
    t = TC + tx
    small = {n: W[n] for n, _ in _SMALL}
    for n, a in _gather_f32_shards([W[n] for n, _, _ in _SHARDED_SMALL]).items():
        small[n + "_full"] = a

    shards = [_layer_shards(W, l) for l in range(2)]
    mods, c8 = _mod_tables(c, c_ctx, W["w_ada"], W["b_ada"], k)
    tabs = _rope_tables(tx) + _lane_consts()
    params = [_layer_params(0, _allgather_layers("gather_weights", shards[0]), small)]

    xs = jnp.concatenate([ctx, x], axis=0)
    h1 = _nm_fwd("l0_norm1", xs, mods[0], params[0]["g1"], 0, 1)
    s0 = _layer_fwd(0, False, xs, h1, mods[0], params[0], tabs, gather=shards[1])
    params.append(_layer_params(1, _gather_own(s0["gathered"], shards[1]), small))
    x1, h1b = _res_nm_fwd("l0_res2_norm1", s0["x_mid"], s0["f"], mods[0], 5, mods[1], params[1]["g1"], 0, 1)
    s1 = _layer_fwd(1, True, x1, h1b, mods[1], params[1], tabs)
    loss, dxm_l, df_l, dmod_head, dgf = _head("head", s1["x_mid"], s1["f"], mods[1], W["final_norm_g"][None], tgt)

    gws = [dict(), dict()]
    dx1, dh1b, dmod1 = _layer_bwd(1, True, s1, mods[1], params[1], tabs, dxm_l, df_l, gws[1])
    dxm0, df0, dmod0_g, dmod1_s, dg1b = _res_nm_bwd("l0_b_res2_norm1", s0["x_mid"], s0["f"], mods[0], 5, mods[1],
                                                    params[1]["g1"], 0, 1, dx1, dh1b)
    gws[1]["norm1_g"] = dg1b[0]
    blocks = lambda g, n, ax: g if n == "w_ffn_up" else _to_blocks(g, ax)
    sums1 = _rs_front("l1_", [blocks(gws[1][n], n, ax) for n, ax in _SHARDED])
    dx0, dh1, dmod0 = _layer_bwd(0, False, s0, mods[0], params[0], tabs, dxm0, df0, gws[0], exchange=sums1)
    red1 = _rs_sum("l1_", _exchange_own(gws[0]["exchanged"], sums1))
    grad_x, dmod0_s, dg1 = _nm_bwd("l0_b_norm1", xs, mods[0], params[0]["g1"], 0, 1, dx0, dh1)
    gws[0]["norm1_g"] = dg1[0]
    dmods = jnp.stack([dmod0 + dmod0_g + dmod0_s, dmod1 + dmod1_s + dmod_head])

    stk = {n: jnp.stack([gws[0][n], gws[1][n]]) for n, _ in _SMALL if n != "final_norm_g"}
    stk["final_norm_g"] = dgf[0]
    flat = jnp.concatenate([dmods.reshape(-1)] + [stk[n].reshape(-1) for n, _ in _SMALL])
    flat = jnp.concatenate([flat, jnp.zeros((_SMALL_R * PACK_C - _SMALL_N,), F32)]).reshape(_SMALL_R, PACK_C)
    every = _allgather_small("gather_small_grads", flat).reshape(N_DEV, _SMALL_R, PACK_C)
    tot = _sum_slots("sum_small_grads", every).reshape(-1)
    grads, o = {}, 2 * 2 * ADA_W
    for n, shp in _SMALL:
        grads[n] = tot[o:o + _prod(shp)].reshape(shp)
        o += _prod(shp)
    grads["b_ada"] = tot[:2 * 2 * ADA_W].reshape(2, 2, ADA_W).sum(axis=1)

    dm_every = every[:, :2 * 2 * ADA_W // PACK_C].reshape(N_DEV, 2, 2, ADA_W)
    dm_loc = lax.dynamic_slice_in_dim(dm_every, k * ADA_LOC, ADA_LOC, axis=3).transpose(1, 2, 0, 3)
    cc8 = jnp.concatenate([c_ctx[None], jnp.zeros((7, D), F32)], axis=0)
    grads["w_ada"], dcc = _adaln_bwd("adaln_b", c8, cc8, W["w_ada"], dm_loc[:, 1], dm_loc[:, 0])
    dcc_every = _allgather_small("gather_dcctx", dcc * 0.5).reshape(N_DEV, 8, D)
    grads["c_ctx"] = _sum_slots("sum_dcctx", dcc_every)[0]

    for n, shp, ax in _SHARDED_SMALL:
        grads[n] = lax.dynamic_slice_in_dim(grads[n], k * (shp[ax] // N_CHIP), shp[ax] // N_CHIP, axis=ax + 1)
    sums0 = _rs_front("l0_", [blocks(gws[0][n], n, ax) for n, ax in _SHARDED])
    red0 = _rs_sum("l0_", _rs_chip_exchange("l0_rs_chip_exchange", sums0))
    for (n, _), a in zip(_SHARDED, _pair_allgather("rs_pair_allgather", red0, red1)):
        grads[n] = a.reshape(2, 2 * a.shape[2], a.shape[3])
    return loss[0, 0], grad_x, grads


_WEIGHTS = ("c_ctx", "w_ada", "b_ada", "norm1_g", "norm2_g", "w_in", "q_norm_g", "k_norm_g", "gmlp_norm_g", "w_spatial",
            "b_spatial", "w_alpha2", "b_alpha", "gla_norm_g", "w_br_a", "w_br_b", "w_br_c", "w_out", "w_ffn_up", "conv_w",
            "conv_b", "w_ffn_down", "final_norm_g")
_BIG = ("w_ada", "w_in", "w_br_a", "w_br_b", "w_br_c", "w_out", "w_ffn_up", "w_ffn_down")


def _update(W, G, M, V):
    delta, new_m, new_v = {}, {}, {}
    for n in _BIG:
        delta[n], new_m[n], new_v[n] = _adamw("adamw_" + n, W[n], G[n], M[n], V[n])
    rest = [n for n in _WEIGHTS if n not in _BIG]
    tot = sum(_prod(W[n].shape) for n in rest)
    rows = -(-tot // (PACK_C * 8)) * 8

    def cat(dct):
        flat = jnp.concatenate([dct[n].reshape(-1) for n in rest] + [jnp.zeros((rows * PACK_C - tot,), F32)])
        return flat.reshape(1, rows, PACK_C)

    outs = _adamw("adamw_small", cat(W), cat(G), cat(M), cat(V))
    o = 0
    for n in rest:
        sz, shp = _prod(W[n].shape), W[n].shape
        delta[n], new_m[n], new_v[n] = (a.reshape(-1)[o:o + sz].reshape(shp) for a in outs)
        o += sz
    return delta, new_m, new_v


def kernel(x, c, ctx, c_ctx, w_ada, b_ada, norm1_g, norm2_g, w_in, q_norm_g, k_norm_g, gmlp_norm_g, w_spatial, b_spatial, w_alpha2, b_alpha, gla_norm_g, w_br_a, w_br_b, w_br_c, w_out, w_ffn_up, conv_w, conv_b, w_ffn_down, final_norm_g, loss_target, m_c_ctx, m_w_ada, m_b_ada, m_norm1_g, m_norm2_g, m_w_in, m_q_norm_g, m_k_norm_g, m_gmlp_norm_g, m_w_spatial, m_b_spatial, m_w_alpha2, m_b_alpha, m_gla_norm_g, m_w_br_a, m_w_br_b, m_w_br_c, m_w_out, m_w_ffn_up, m_conv_w, m_conv_b, m_w_ffn_down, m_final_norm_g, v_c_ctx, v_w_ada, v_b_ada, v_norm1_g, v_norm2_g, v_w_in, v_q_norm_g, v_k_norm_g, v_gmlp_norm_g, v_w_spatial, v_b_spatial, v_w_alpha2, v_b_alpha, v_gla_norm_g, v_w_br_a, v_w_br_b, v_w_br_c, v_w_out, v_w_ffn_up, v_conv_w, v_conv_b, v_w_ffn_down, v_final_norm_g):
    W = dict(c_ctx=c_ctx, w_ada=w_ada, b_ada=b_ada, norm1_g=norm1_g, norm2_g=norm2_g, w_in=w_in, q_norm_g=q_norm_g,
             k_norm_g=k_norm_g, gmlp_norm_g=gmlp_norm_g, w_spatial=w_spatial, b_spatial=b_spatial, w_alpha2=w_alpha2,
             b_alpha=b_alpha, gla_norm_g=gla_norm_g, w_br_a=w_br_a, w_br_b=w_br_b, w_br_c=w_br_c, w_out=w_out,
             w_ffn_up=w_ffn_up, conv_w=conv_w, conv_b=conv_b, w_ffn_down=w_ffn_down, final_norm_g=final_norm_g)
    M = dict(c_ctx=m_c_ctx, w_ada=m_w_ada, b_ada=m_b_ada, norm1_g=m_norm1_g, norm2_g=m_norm2_g, w_in=m_w_in,
             q_norm_g=m_q_norm_g, k_norm_g=m_k_norm_g, gmlp_norm_g=m_gmlp_norm_g, w_spatial=m_w_spatial,
             b_spatial=m_b_spatial, w_alpha2=m_w_alpha2, b_alpha=m_b_alpha, gla_norm_g=m_gla_norm_g, w_br_a=m_w_br_a,
             w_br_b=m_w_br_b, w_br_c=m_w_br_c, w_out=m_w_out, w_ffn_up=m_w_ffn_up, conv_w=m_conv_w, conv_b=m_conv_b,
             w_ffn_down=m_w_ffn_down, final_norm_g=m_final_norm_g)
    V = dict(c_ctx=v_c_ctx, w_ada=v_w_ada, b_ada=v_b_ada, norm1_g=v_norm1_g, norm2_g=v_norm2_g, w_in=v_w_in,
             q_norm_g=v_q_norm_g, k_norm_g=v_k_norm_g, gmlp_norm_g=v_gmlp_norm_g, w_spatial=v_w_spatial,
             b_spatial=v_b_spatial, w_alpha2=v_w_alpha2, b_alpha=v_b_alpha, gla_norm_g=v_gla_norm_g, w_br_a=v_w_br_a,
             w_br_b=v_w_br_b, w_br_c=v_w_br_c, w_out=v_w_out, w_ffn_up=v_w_ffn_up, conv_w=v_conv_w, conv_b=v_conv_b,
             w_ffn_down=v_w_ffn_down, final_norm_g=v_final_norm_g)
    loss_local, grad_x, G = _step(x[0], c, ctx[0], c_ctx, W, loss_target[0])
    loss = lax.psum(loss_local, ("x", "y", "c"))
    delta, new_m, new_v = _update(W, G, M, V)
    return (loss, grad_x[None], *[G[n] for n in _WEIGHTS], *[delta[n] for n in _WEIGHTS],
            *[new_m[n] for n in _WEIGHTS], *[new_v[n] for n in _WEIGHTS])
```

```python
import functools

import jax
import jax.numpy as jnp
from jax import lax
from jax.experimental import pallas as pl
from jax.experimental.pallas import tpu as pltpu

F32 = jnp.float32
BF16 = jnp.bfloat16

D = 1024
TC = 256
GRID_W = 64
EPS = 1e-6
HD = 64
NQ = 8
NKV = 2
QG = NQ // NKV
GLA_H = 4
GLA_DK = 64
GLA_DV = 128
GLA_QK = 256
GLA_V = 512
GLA_CHUNK = 64
GLA_TAU = 16.0
GW = 512
FFN = 2816
IN_W = 6432
PW = 6528
ADA_W = 6 * D
N_CHIP = 4
N_DEV = 8
ADA_LOC = ADA_W // N_CHIP

ADAM_LR = 0.001
ADAM_B1 = 0.9
ADAM_B2 = 0.999
ADAM_EPS = 1e-08
ADAM_WD = 0.01
ADAM_STEP = 10

TM = 256
NCB = TC // TM
LANE = 128
VMEM_LIMIT = 48 * 1024 * 1024
VMEM_LIMIT_ATTN_BWD = 56 * 1024 * 1024
MESH = pl.DeviceIdType.MESH

_COLS = (("gA", 3360, 1024), ("gB", 4384, 1024), ("gC", 5408, 1024), ("gu", 0, 512), ("gv", 512, 512),
         ("q", 1024, 512), ("glv", 2304, 512), ("gr", 2848, 512), ("glq", 1792, 256), ("glk", 2048, 256),
         ("k", 1536, 128), ("v", 1664, 128), ("ab", 2816, 32))
OFF = {}
_o = 0
for _n, _s, _w in _COLS:
    OFF[_n] = _o
    _o += max(_w, LANE)
assert _o == PW


def _to_new_cols(w):
    parts = [w[..., s:s + n] for _, s, n in _COLS]
    pad = jnp.zeros(w.shape[:-1] + (PW - IN_W,), w.dtype)
    return jnp.concatenate(parts + [pad], axis=-1)


def _to_ref_cols(w):
    by_start = sorted(_COLS, key=lambda t: t[1])
    return jnp.concatenate([w[..., OFF[n]:OFF[n] + wd] for n, _, wd in by_start], axis=-1)


def _tile(n, target, align=LANE):
    best = None
    t = align
    while t <= min(n, target):
        if n % t == 0:
            best = t
        t += align
    assert best is not None, (n, target, align)
    return best


def _cp(sem=None, vmem=VMEM_LIMIT):
    return pltpu.CompilerParams(dimension_semantics=sem, vmem_limit_bytes=vmem)


def _bdot_impl(a, b, ca, cb):
    return lax.dot_general(a.astype(BF16), b.astype(BF16), (((ca,), (cb,)), ((), ())),
                           preferred_element_type=F32)


@functools.partial(jax.custom_vjp, nondiff_argnums=(2, 3))
def bdot(a, b, ca, cb):
    return _bdot_impl(a, b, ca, cb)


def _bdot_fwd(a, b, ca, cb):
    return _bdot_impl(a, b, ca, cb), (a, b)


def _bdot_bwd(ca, cb, res, g):
    a, b = res
    da = bdot(g, b, 1, 1 - cb) if ca == 1 else bdot(b, g, 1 - cb, 1)
    db = bdot(a, g, 1 - ca, 0) if cb == 0 else bdot(g, a, 0, 1 - ca)
    return da.astype(a.dtype), db.astype(b.dtype)


bdot.defvjp(_bdot_fwd, _bdot_bwd)


def hdot(a, b, ca=1, cb=0):
    return lax.dot_general(a, b, (((ca,), (cb,)), ((), ())), precision=lax.Precision.HIGH,
                           preferred_element_type=F32)


def _rms(x, g):
    return x * lax.rsqrt(jnp.mean(x * x, axis=-1, keepdims=True) + EPS) * g


def _gelu(x):
    return 0.5 * x * (1.0 + jnp.tanh(0.7978845608028654 * (x + 0.044715 * (x * x * x))))


def _log_sigmoid(z):
    return jnp.minimum(z, 0.0) - jnp.log(1.0 + jnp.exp(-jnp.abs(z)))


def _sel(mod, is_lat, idx):
    return jnp.where(is_lat, mod[1, idx:idx + 1, :], mod[0, idx:idx + 1, :])


def _rows_call(name, fn, grid, ins, outs, acc_axes=None, sem=None):
    n_in = len(ins)
    flags = [o[2] for o in outs]
    if acc_axes is None:
        acc_axes = (len(grid) - 1,)

    def body(*refs):
        ids = tuple(pl.program_id(a) for a in range(len(grid)))
        res = fn(ids, *[r[...] for r in refs[:n_in]])
        for r, v, acc in zip(refs[n_in:], res, flags):
            if acc:
                first = functools.reduce(jnp.logical_and, [ids[a] == 0 for a in acc_axes])

                @pl.when(first)
                def _():
                    r[...] = jnp.zeros_like(r)
                r[...] += v.astype(r.dtype)
            else:
                r[...] = v.astype(r.dtype)

    return pl.pallas_call(
        body, name=name, grid=grid, in_specs=[s for _, s in ins], out_specs=[o[1] for o in outs],
        out_shape=[o[0] for o in outs],
        compiler_params=_cp(sem if sem is not None else ("arbitrary",) * len(grid)),
    )(*[a for a, _ in ins])


def _sds(shape, dtype):
    return jax.ShapeDtypeStruct(shape, dtype)


def _rowspec(width, off=0, tm=TM):
    assert off % width == 0
    return pl.BlockSpec((tm, width), lambda i, o=off // width: (i, o))


def _full(shape):
    nd = len(shape)
    return pl.BlockSpec(shape, lambda *a: (0,) * nd)


def _mm(name, a, b, mode, out_dtype, tm_t=1056, tn_t=1408, tk_t=1408, chip_blocks=False, j_outer=False):
    halves = a.ndim == 3 or b.ndim == 3
    if mode == "nn":
        (m, k), (_, n) = a.shape, b.shape
    elif mode == "nt":
        (m, k), (n, _) = a.shape[-2:], b.shape
        k *= a.ndim - 1
    else:
        (k, m), (_, n) = a.shape, b.shape[-2:]
        n *= b.ndim - 1
    tm = _tile(m, tm_t, 8 if m % LANE else LANE)
    tn = _tile(n // 2 if halves and mode == "tn" else n, tn_t)
    tk = _tile(k // 2 if halves and mode == "nt" else k, tk_t)
    nk = k // tk
    if mode == "nn":
        dims, a_spec, b_spec = ((1,), (0,)), pl.BlockSpec((tm, tk), lambda i, j, l: (i, l)), pl.BlockSpec((tk, tn), lambda i, j, l: (l, j))
    elif mode == "nt":
        dims, a_spec, b_spec = ((1,), (1,)), pl.BlockSpec((tm, tk), lambda i, j, l: (i, l)), pl.BlockSpec((tn, tk), lambda i, j, l: (j, l))
        if halves:
            a_spec = pl.BlockSpec((None, tm, tk), lambda i, j, l, h=nk // 2: (l // h, i, l % h))
    else:
        dims, a_spec, b_spec = ((0,), (0,)), pl.BlockSpec((tk, tm), lambda i, j, l: (l, i)), pl.BlockSpec((tk, tn), lambda i, j, l: (l, j))
        if halves:
            b_spec = pl.BlockSpec((None, tk, tn), lambda i, j, l, h=n // tn // 2: (j // h, l, j % h))

    def body(a_ref, b_ref, o_ref, *scratch):
        l = pl.program_id(2)
        part = lax.dot_general(a_ref[...].astype(BF16), b_ref[...].astype(BF16), (dims, ((), ())),
                               preferred_element_type=F32)
        if nk == 1:
            o_ref[...] = part.astype(o_ref.dtype)
            return
        acc_ref = scratch[0]

        @pl.when(l == 0)
        def _():
            acc_ref[...] = part

        @pl.when(l > 0)
        def _():
            acc_ref[...] += part

        @pl.when(l == nk - 1)
        def _():
            o_ref[...] = acc_ref[...].astype(o_ref.dtype)

    o_spec, o_shape = pl.BlockSpec((tm, tn), lambda i, j, l: (i, j)), _sds((m, n), out_dtype)
    if chip_blocks:
        assert tn * N_CHIP == n and tm == m
        o_spec, o_shape = pl.BlockSpec((None, tm, tn), lambda i, j, l: (j, 0, 0)), _sds((N_CHIP, m, tn), out_dtype)
    grid = (m // tm, n // tn, nk)
    if j_outer:
        swap = lambda spec: pl.BlockSpec(spec.block_shape, lambda j, i, l, f=spec.index_map: f(i, j, l))
        a_spec, b_spec, o_spec, grid = swap(a_spec), swap(b_spec), swap(o_spec), (n // tn, m // tm, nk)
    return pl.pallas_call(
        body, name=name, grid=grid, in_specs=[a_spec, b_spec], out_specs=o_spec, out_shape=o_shape,
        scratch_shapes=[pltpu.VMEM((tm, tn), F32)] if nk > 1 else [],
        compiler_params=_cp(("parallel", "parallel", "arbitrary")),
    )(a, b)


def _nm_fn(is_lat, x, mod, g, shift, scale):
    return _rms(x, g) * (1.0 + _sel(mod, is_lat, scale)) + _sel(mod, is_lat, shift)


def _res_nm_fn(is_lat, x, br, modg, gate, mods, g, shift, scale):
    xn = x + _sel(modg, is_lat, gate) * br
    return xn, _nm_fn(is_lat, xn, mods, g, shift, scale)


def _nm_fwd(name, x, mod, g, shift, scale):
    t = x.shape[0]
    fn = lambda ids, xv, mv, gv: (_nm_fn(ids[0] >= NCB, xv, mv, gv, shift, scale),)
    return _rows_call(name, fn, (t // TM,), [(x, _rowspec(D)), (mod, _full((2, 6, D))), (g, _full((1, D)))],
                      [(_sds((t, D), BF16), _rowspec(D), False)])[0]


def _nm_bwd(name, x, mod, g, shift, scale, dx_res, dh):
    t = x.shape[0]

    def fn(ids, xv, mv, gv, dxr, dhv):
        _, vjp = jax.vjp(lambda a, b, c: _nm_fn(ids[0] >= NCB, a, b, c, shift, scale), xv, mv, gv)
        dx, dm, dg = vjp(dhv)
        return dx + dxr, dm, dg

    lat = pl.BlockSpec((TM, D), lambda i: (jnp.maximum(i - NCB, 0), 0))
    return _rows_call(name, fn, (t // TM,),
                      [(x, _rowspec(D)), (mod, _full((2, 6, D))), (g, _full((1, D))), (dx_res, _rowspec(D)), (dh, _rowspec(D))],
                      [(_sds((t - TC, D), F32), lat, False), (_sds((2, 6, D), F32), _full((2, 6, D)), True),
                       (_sds((1, D), F32), _full((1, D)), True)])


def _res_nm_fwd(name, x, br, modg, gate, mods, g, shift, scale):
    t = x.shape[0]
    fn = lambda ids, xv, bv, mg, ms, gv: _res_nm_fn(ids[0] >= NCB, xv, bv, mg, gate, ms, gv, shift, scale)
    return _rows_call(name, fn, (t // TM,),
                      [(x, _rowspec(D)), (br, _rowspec(D)), (modg, _full((2, 6, D))), (mods, _full((2, 6, D))), (g, _full((1, D)))],
                      [(_sds((t, D), F32), _rowspec(D), False), (_sds((t, D), BF16), _rowspec(D), False)])


def _res_nm_bwd(name, x, br, modg, gate, mods, g, shift, scale, dx_res, dh):
    t = x.shape[0]

    def fn(ids, xv, bv, mg, ms, gv, dxr, dhv):
        f = lambda a, b, c, d, e: _res_nm_fn(ids[0] >= NCB, a, b, c, gate, d, e, shift, scale)
        _, vjp = jax.vjp(f, xv, bv, mg, ms, gv)
        return vjp((dxr, dhv))

    m26 = (_sds((2, 6, D), F32), _full((2, 6, D)), True)
    return _rows_call(name, fn, (t // TM,),
                      [(x, _rowspec(D)), (br, _rowspec(D)), (modg, _full((2, 6, D))), (mods, _full((2, 6, D))), (g, _full((1, D))),
                       (dx_res, _rowspec(D)), (dh, _rowspec(D))],
                      [(_sds((t, D), F32), _rowspec(D), False), (_sds((t, D), BF16), _rowspec(D), False), m26, m26,
                       (_sds((1, D), F32), _full((1, D)), True)])


def _head(name, x_mid, f, mod, gf, tgt):
    t = x_mid.shape[0]

    def fn(ids, xv, fv, mv, gv, tv):
        def loss_fn(a, b, c, d):
            y = _rms(a + c[1, 5:6, :] * b, d)
            e = y - tv
            return 0.5 * jnp.sum(jnp.mean(e * e, axis=-1))
        loss, grads = jax.value_and_grad(loss_fn, argnums=(0, 1, 2, 3))(xv, fv, mv, gv)
        return tuple(jnp.where(ids[0] >= NCB, v, 0.0) for v in (jnp.reshape(loss, (1, 1)),) + grads)

    return _rows_call(name, fn, (t // TM,),
                      [(x_mid, _rowspec(D)), (f, _rowspec(D)), (mod, _full((2, 6, D))), (gf, _full((1, D))),
                       (tgt, pl.BlockSpec((TM, D), lambda i: (jnp.maximum(i - NCB, 0), 0)))],
                      [(_sds((1, 1), F32), _full((1, 1)), True), (_sds((t, D), F32), _rowspec(D), False),
                       (_sds((t, D), BF16), _rowspec(D), False), (_sds((2, 6, D), F32), _full((2, 6, D)), True),
                       (_sds((1, D), F32), _full((1, D)), True)])


def _gmlp_fn(u, v, g, ws, bst):
    rows = []
    u, v = u.astype(F32), v.astype(F32)
    for r in range(u.shape[0] // 128):
        uu, vv = _gelu(u[128 * r:128 * r + 128]), _gelu(v[128 * r:128 * r + 128])
        cols = []
        for gi in range(4):
            sl = slice(128 * gi, 128 * gi + 128)
            f = bdot(ws[gi], _rms(vv[:, sl], g[:, sl]), 1, 0) + bst[:, gi:gi + 1]
            cols.append(uu[:, sl] * f)
        rows.append(jnp.concatenate(cols, axis=-1))
    return jnp.concatenate(rows, axis=0)


def _gmlp_ins(p, g, ws, bst):
    return [(p, _rowspec(GW, OFF["gu"])), (p, _rowspec(GW, OFF["gv"])), (g, _full((1, GW))),
            (ws, _full((4, 128, 128))), (bst, _full((128, 4)))]


def _gmlp_fwd(name, p, g, ws, bst):
    t = p.shape[0]
    return _rows_call(name, lambda ids, *a: (_gmlp_fn(*a),), (t // TM,), _gmlp_ins(p, g, ws, bst),
                      [(_sds((t, GW), BF16), _rowspec(GW), False)])[0]


def _gmlp_bwd(name, p, g, ws, bst, dgm):
    t = p.shape[0]

    def fn(ids, u, v, gv, wv, bv, dv):
        _, vjp = jax.vjp(_gmlp_fn, u, v, gv, wv, bv)
        return vjp(dv)

    return _rows_call(name, fn, (t // TM,), _gmlp_ins(p, g, ws, bst) + [(dgm, _rowspec(GW))],
                      [(_sds((t, GW), BF16), _rowspec(GW), False), (_sds((t, GW), BF16), _rowspec(GW), False),
                       (_sds((1, GW), F32), _full((1, GW)), True), (_sds((4, 128, 128), F32), _full((4, 128, 128)), True),
                       (_sds((128, 4), F32), _full((128, 4)), True)])


def _qk_fn(q, k, gq, gk, cos, sin, seg, perm):
    cq, sq = jnp.concatenate([cos] * 4, axis=-1), jnp.concatenate([sin] * 4, axis=-1)
    q, k = q.astype(F32), k.astype(F32)
    qn = q * lax.rsqrt(hdot(q * q, seg) + EPS) * gq
    kn = k * lax.rsqrt(hdot(k * k, seg[:128, :128]) + EPS) * gk
    qr = qn * cq + hdot(qn, perm) * sq
    kr = kn * cos + hdot(kn, perm[:128, :128]) * sin
    return qr * (HD ** -0.5), kr


def _qk_ins(p, gq, gk, cos, sin, seg, perm):
    return [(p, _rowspec(512, OFF["q"])), (p, _rowspec(128, OFF["k"])), (gq, _full((1, 512))), (gk, _full((1, 128))),
            (cos, _rowspec(128)), (sin, _rowspec(128)), (seg, _full((512, 512))), (perm, _full((512, 512)))]


def _qk_fwd(name, p, gq, gk, cos, sin, seg, perm):
    t = p.shape[0]
    fn = lambda ids, q, k, a, b, c, s, sg, pm, v: _qk_fn(q, k, a, b, c, s, sg, pm) + (v,)
    return _rows_call(name, fn, (t // TM,), _qk_ins(p, gq, gk, cos, sin, seg, perm) + [(p, _rowspec(128, OFF["v"]))],
                      [(_sds((t, 512), BF16), _rowspec(512), False), (_sds((t, 128), BF16), _rowspec(128), False),
                       (_sds((t, 128), BF16), _rowspec(128), False)])


def _qk_bwd(name, p, gq, gk, cos, sin, seg, perm, dqr, dkr):
    t = p.shape[0]

    def fn(ids, q, k, a, b, c, s, sg, pm, dq, dk):
        _, vjp = jax.vjp(lambda q_, k_, a_, b_: _qk_fn(q_, k_, a_, b_, c, s, sg, pm), q, k, a, b)
        return vjp((dq, dk))

    return _rows_call(name, fn, (t // TM,),
                      _qk_ins(p, gq, gk, cos, sin, seg, perm) + [(dqr, _rowspec(512)), (dkr, _rowspec(128))],
                      [(_sds((t, 512), BF16), _rowspec(512), False), (_sds((t, 128), BF16), _rowspec(128), False),
                       (_sds((1, 512), F32), _full((1, 512)), True), (_sds((1, 128), F32), _full((1, 128)), True)])


_ATT_TQ = 1024
_ATT_TK = 768


def _attn_fwd(name, q, k, v, gather=()):
    h, tq_all, _ = q.shape
    hkv, tk_all, _ = k.shape
    tq, tk = _tile(tq_all, _ATT_TQ), _tile(tk_all, _ATT_TK)
    nkc = tk_all // tk
    ng, nq = len(gather), tq_all // tq

    def body(*refs):
        q_ref, k_ref, v_ref = refs[:3]
        o_ref, lse_ref = refs[3 + ng:5 + ng]
        if ng:
            g_id, i_id = pl.program_id(0), pl.program_id(1)
            stage = lambda st: _gather_stage(st, refs[3:3 + ng], refs[5 + ng:5 + 2 * ng], *refs[5 + 2 * ng:])
            pl.when(jnp.logical_and(g_id == 0, i_id == 0))(lambda: stage(0))
            pl.when(jnp.logical_and(g_id == hkv - 1, i_id == 0))(lambda: stage(1))
        qv = q_ref[...].reshape(QG * tq, HD)

        def step(j, carry):
            m, acc = carry
            off = pl.multiple_of(j * tk, tk)
            kk, vv = k_ref[0, pl.ds(off, tk), :], v_ref[0, pl.ds(off, tk), :]
            s = lax.dot_general(qv, kk, (((1,), (1,)), ((), ())), preferred_element_type=F32)
            m_new = jnp.maximum(m, jnp.max(s, axis=-1, keepdims=True))
            pr = jnp.exp(s - m_new)
            acc = jnp.exp(m - m_new) * acc + jnp.dot(pr.astype(BF16), vv, preferred_element_type=F32)
            return m_new, acc

        init = (jnp.full((QG * tq, 1), -jnp.inf, F32), jnp.zeros((QG * tq, 2 * HD), F32))
        m, acc = lax.fori_loop(0, nkc, step, init)
        l = acc[:, HD:HD + 1]
        o_ref[...] = (acc[:, :HD] / l).reshape(QG, tq, HD)
        lse_ref[...] = (m + jnp.log(l)).reshape(QG, tq, 1)
        if ng:
            pl.when(jnp.logical_and(g_id == hkv - 1, i_id == nq - 1))(lambda: stage(2))

    kv_spec = pl.BlockSpec((1, tk_all, HD), lambda g, i: (g, 0, 0))
    v1_spec = pl.BlockSpec((1, tk_all, 2 * HD), lambda g, i: (g, 0, 0))
    qspec = pl.BlockSpec((QG, tq, HD), lambda g, i: (g, i, 0))
    sems = [pltpu.SemaphoreType.DMA((6 * ng,)), pltpu.SemaphoreType.DMA((6 * ng,))] if ng else []
    return pl.pallas_call(
        body, name=name, grid=(hkv, nq), in_specs=[qspec, kv_spec, v1_spec] + [_ANY] * ng,
        out_specs=[qspec, pl.BlockSpec((QG, tq, 1), lambda g, i: (g, i, 0))] + [_ANY] * ng,
        out_shape=[_sds((h, tq_all, HD), F32), _sds((h, tq_all, 1), F32)]
        + [_sds((N_CHIP,) + a.shape, a.dtype) for a in gather],
        scratch_shapes=sems, compiler_params=_cp(("arbitrary", "arbitrary")),
    )(q, k, v, *gather)


def _attn_bwd(name, q, k, v, o, do, lse_row, exchange=()):
    h, tq_all, _ = q.shape
    hkv, tk_all, _ = k.shape
    tq, tk = _tile(tq_all, 1024), _tile(tk_all, 1408)
    ne, nq, nk = len(exchange), tq_all // tq, tk_all // tk

    def body(*refs):
        q_ref, k_ref, v_ref, o_ref, do_ref, lse_ref = refs[:6]
        dq_ref, dk_ref, dv_ref = refs[6 + ne:9 + ne]
        dl_ref = refs[9 + 2 * ne]
        g_id, i, j = pl.program_id(0), pl.program_id(1), pl.program_id(2)
        if ne:
            stage = lambda st: _exchange_stage(st, refs[6:6 + ne], refs[9 + ne:9 + 2 * ne], *refs[10 + 2 * ne:])
            first = functools.reduce(jnp.logical_and, [g_id == 0, i == 0, j == 0])
            pl.when(first)(lambda: stage(0))

        @pl.when(j == 0)
        def _():
            ones = jnp.ones((8, HD), F32)
            for g in range(QG):
                dl_ref[g] = hdot(ones, do_ref[g] * o_ref[g], 1, 1)

        kk, vv = k_ref[0], v_ref[0]
        dk_acc, dv_acc = jnp.zeros((tk, HD), F32), jnp.zeros((tk, HD), F32)
        for g in range(QG):
            qv, dob = q_ref[g], do_ref[g].astype(BF16)
            st = lax.dot_general(kk, qv, (((1,), (1,)), ((), ())), preferred_element_type=F32)
            pt = jnp.exp(st - lse_ref[g])
            dv_acc += jnp.dot(pt.astype(BF16), dob, preferred_element_type=F32)
            dpt = lax.dot_general(vv, dob, (((1,), (1,)), ((), ())), preferred_element_type=F32)
            dst = (pt * (dpt - dl_ref[g, 0:1, :])).astype(BF16)
            dk_acc += jnp.dot(dst, qv, preferred_element_type=F32)
            dq_part = lax.dot_general(dst, kk, (((0,), (0,)), ((), ())), preferred_element_type=F32)

            @pl.when(j == 0)
            def _():
                dq_ref[g] = dq_part

            @pl.when(j > 0)
            def _():
                dq_ref[g] += dq_part

        rows = pl.ds(pl.multiple_of(j * tk, tk), tk)

        @pl.when(i == 0)
        def _():
            dk_ref[0, rows, :] = dk_acc
            dv_ref[0, rows, :] = dv_acc

        @pl.when(i > 0)
        def _():
            dk_ref[0, rows, :] += dk_acc
            dv_ref[0, rows, :] += dv_acc

        if ne:
            last = functools.reduce(jnp.logical_and, [g_id == hkv - 1, i == nq - 1, j == nk - 1])
            pl.when(last)(lambda: stage(1))

    ks = pl.BlockSpec((1, tk, HD), lambda g, i, j: (g, j, 0))
    qs = pl.BlockSpec((QG, tq, HD), lambda g, i, j: (g, i, 0))
    rs = pl.BlockSpec((QG, 1, tq), lambda g, i, j: (g, 0, i))
    full = pl.BlockSpec((1, tk_all, HD), lambda g, i, j: (g, 0, 0))
    sems = [pltpu.SemaphoreType.DMA((3 * ne,)), pltpu.SemaphoreType.DMA((3 * ne,))] if ne else []
    return pl.pallas_call(
        body, name=name, grid=(hkv, nq, nk), in_specs=[qs, ks, ks, qs, qs, rs] + [_ANY] * ne,
        out_specs=[qs, full, full] + [_ANY] * ne,
        out_shape=[_sds((h, tq_all, HD), F32), _sds((hkv, tk_all, HD), F32), _sds((hkv, tk_all, HD), F32)]
        + [_sds(a.shape, a.dtype) for a in exchange],
        scratch_shapes=[pltpu.VMEM((QG, 8, tq), F32)] + sems,
        compiler_params=_cp(("arbitrary", "arbitrary", "arbitrary"), VMEM_LIMIT_ATTN_BWD),
    )(q, k, v, o, do, lse_row, *exchange)


def _decay_fn(a, w2, b2):
    return _log_sigmoid(bdot(a, w2, 1, 0) + b2) / GLA_TAU


def _decay_fwd(name, p, w2, b2):
    t = p.shape[0]
    return _rows_call(name, lambda ids, a, w, b: (_decay_fn(a, w, b),), (t // TM,),
                      [(p, _rowspec(128, OFF["ab"])), (w2, _full((128, 512))), (b2, _full((1, 512)))],
                      [(_sds((t, 512), F32), _rowspec(512), False)])[0]


def _decay_bwd(name, p, w2, b2, gf, gb):
    t = p.shape[0]

    def fn(ids, a, w, b, qf, kf, vf, lf, qb, kb, vb, lb):
        _, vjp = jax.vjp(_decay_fn, a, w, b)
        return vjp(jnp.concatenate([lf, lb], axis=-1)) + (qf + qb, kf + kb, vf + vb)

    widths = (256, 256, 512, 256)
    return _rows_call(name, fn, (t // TM,),
                      [(p, _rowspec(128, OFF["ab"])), (w2, _full((128, 512))), (b2, _full((1, 512)))]
                      + [(g, _rowspec(w)) for g, w in zip(gf, widths)] + [(g, _rowspec(w)) for g, w in zip(gb, widths)],
                      [(_sds((t, 128), BF16), _rowspec(128), False), (_sds((128, 512), F32), _full((128, 512)), True),
                       (_sds((1, 512), F32), _full((1, 512)), True)]
                      + [(_sds((t, w), BF16), _rowspec(w), False) for w in widths[:3]])


def _gla_consts(reverse):
    r = lax.broadcasted_iota(jnp.int32, (GLA_CHUNK, GLA_CHUNK), 0)
    c = lax.broadcasted_iota(jnp.int32, (GLA_CHUNK, GLA_CHUNK), 1)
    trib = (r <= c) if reverse else (r >= c)
    br = lax.broadcasted_iota(jnp.int32, (GLA_QK, GLA_V), 0) // GLA_DK
    bc = lax.broadcasted_iota(jnp.int32, (GLA_QK, GLA_V), 1) // GLA_DV
    lane_head = lax.broadcasted_iota(jnp.int32, (1, GLA_QK), 1) // GLA_DK
    return trib, (br == bc).astype(F32), lane_head


def _gla_chunk(q, k, v, la, s_in, consts):
    trib, bd, lane_head = consts
    q, k = q.astype(F32), k.astype(F32)
    cum = hdot(trib.astype(F32), la)
    tot = jnp.sum(la, axis=0, keepdims=True)
    q_in = q * (GLA_DK ** -0.5) * jnp.exp(cum)
    k_in = k * jnp.exp(-cum)
    k_st = k * jnp.exp(tot - cum)
    outs = []
    for h in range(GLA_H):
        att = bdot(jnp.where(lane_head == h, q_in, 0.0), k_in, 1, 1)
        att = jnp.where(trib, att, 0.0)
        outs.append(bdot(att, v[:, GLA_DV * h:GLA_DV * (h + 1)], 1, 0))
    o = jnp.concatenate(outs, axis=-1) + bdot(q_in, s_in, 1, 0)
    decay = jnp.exp(hdot(la, jnp.ones((GLA_CHUNK, LANE), F32), 0, 0))
    s_out = jnp.concatenate([decay] * (GLA_V // LANE), axis=-1) * s_in + bdot(k_st, v, 0, 0) * bd
    return o, s_out


def _gla_order(nb, reverse, backward):
    if not reverse:
        return (lambda s: nb - 1 - s) if backward else (lambda s: s)
    if backward:
        return lambda s: jnp.where(s == nb - 1, 0, s + 1)
    return lambda s: jnp.where(s == 0, 0, nb - s)


_NCH = TM // GLA_CHUNK


def _gla_specs(nb, reverse, backward):
    order = _gla_order(nb, reverse, backward)
    col = lambda width, off: pl.BlockSpec((TM, width), lambda s, o=off // width: (order(s), o))
    state = pl.BlockSpec((_NCH, GLA_H, GLA_DK, GLA_DV), lambda s: (order(s), 0, 0, 0))
    qkvla = [col(256, OFF["glq"]), col(256, OFF["glk"]), col(512, OFF["glv"]), col(256, 256 * int(reverse))]
    return col, state, qkvla


def _gla_fwd(name, p, la):
    t = p.shape[0]
    nb = t // TM

    def body(*refs):
        ins, outs, scr = (refs[0:4], refs[4:8]), (refs[8:10], refs[10:12]), refs[12:14]

        @pl.when(pl.program_id(0) == 0)
        def _():
            for s_ref in scr:
                s_ref[...] = jnp.zeros_like(s_ref)

        for step in range(_NCH):
            for d in range(2):
                (q_ref, k_ref, v_ref, la_ref), (o_ref, sv_ref), s_ref = ins[d], outs[d], scr[d]
                c = _NCH - 1 - step if d else step
                rows = slice(GLA_CHUNK * c, GLA_CHUNK * (c + 1))
                s_in = s_ref[...]
                for h in range(GLA_H):
                    sv_ref[c, h] = s_in[GLA_DK * h:GLA_DK * (h + 1), GLA_DV * h:GLA_DV * (h + 1)]
                o, s_out = _gla_chunk(q_ref[rows, :], k_ref[rows, :], v_ref[rows, :], la_ref[rows, :], s_in,
                                      _gla_consts(bool(d)))
                o_ref[rows, :] = o
                s_ref[...] = s_out

    in_specs, out_specs, out_shape = [], [], []
    for d in range(2):
        col, state, qkvla = _gla_specs(nb, bool(d), False)
        in_specs += qkvla
        out_specs += [col(512, 0), state]
        out_shape += [_sds((t, GLA_V), F32), _sds((t // GLA_CHUNK, GLA_H, GLA_DK, GLA_DV), F32)]
    return pl.pallas_call(
        body, name=name, grid=(nb,), in_specs=in_specs, out_specs=out_specs, out_shape=out_shape,
        scratch_shapes=[pltpu.VMEM((GLA_QK, GLA_V), F32)] * 2, compiler_params=_cp(("arbitrary",)),
    )(p, p, p, la, p, p, p, la)


def _gla_bwd(name, p, la, sv_f, sv_b, do):
    t = p.shape[0]
    nb = t // TM

    def body(*refs):
        ins, outs, scr = (refs[0:6], refs[6:12]), (refs[12:16], refs[16:20]), refs[20:22]

        @pl.when(pl.program_id(0) == 0)
        def _():
            for ds_ref in scr:
                ds_ref[...] = jnp.zeros_like(ds_ref)

        zero = jnp.zeros((GLA_DK, GLA_DV), F32)
        for step in range(_NCH):
            for d in range(2):
                (q_ref, k_ref, v_ref, la_ref, sv_ref, do_ref), out_refs, ds_ref = ins[d], outs[d], scr[d]
                c = step if d else _NCH - 1 - step
                rows = slice(GLA_CHUNK * c, GLA_CHUNK * (c + 1))
                s_in = jnp.concatenate(
                    [jnp.concatenate([sv_ref[c, h] if hh == h else zero for hh in range(GLA_H)], axis=-1)
                     for h in range(GLA_H)], axis=0)
                consts = _gla_consts(bool(d))
                _, vjp = jax.vjp(lambda a, b, cc, dd, e: _gla_chunk(a, b, cc, dd, e, consts),
                                 q_ref[rows, :], k_ref[rows, :], v_ref[rows, :], la_ref[rows, :], s_in)
                grads = vjp((do_ref[rows, :], ds_ref[...]))
                for o_ref, g in zip(out_refs, grads[:4]):
                    o_ref[rows, :] = g.astype(o_ref.dtype)
                ds_ref[...] = grads[4]

    ins, in_specs, out_specs, out_shape = [], [], [], []
    for d, sv in enumerate((sv_f, sv_b)):
        col, state, qkvla = _gla_specs(nb, bool(d), True)
        ins += [p, p, p, la, sv, do]
        in_specs += qkvla + [state, col(512, 0)]
        out_specs += [col(256, 0), col(256, 0), col(512, 0), col(256, 0)]
        out_shape += [_sds((t, GLA_QK), F32), _sds((t, GLA_QK), F32), _sds((t, GLA_V), F32), _sds((t, GLA_QK), F32)]
    return pl.pallas_call(
        body, name=name, grid=(nb,), in_specs=in_specs, out_specs=out_specs, out_shape=out_shape,
        scratch_shapes=[pltpu.VMEM((GLA_QK, GLA_V), F32)] * 2, compiler_params=_cp(("arbitrary",)),
    )(*ins)


def _gla_out_fn(of, ob, r, g):
    o = of + ob
    cols = [_rms(o[:, GLA_DV * h:GLA_DV * (h + 1)], g[:, GLA_DV * h:GLA_DV * (h + 1)]) for h in range(GLA_H)]
    return jnp.concatenate(cols, axis=-1) * jax.nn.silu(r.astype(F32))


def _gla_out_fwd(name, of, ob, p, g):
    t = p.shape[0]
    return _rows_call(name, lambda ids, *a: (_gla_out_fn(*a),), (t // TM,),
                      [(of, _rowspec(512)), (ob, _rowspec(512)), (p, _rowspec(512, OFF["gr"])), (g, _full((1, 512)))],
                      [(_sds((t, 512), BF16), _rowspec(512), False)])[0]


def _gla_out_bwd(name, of, ob, p, g, dgla):
    t = p.shape[0]

    def fn(ids, a, b, r, gv, dv):
        _, vjp = jax.vjp(_gla_out_fn, a, b, r, gv)
        do, _, dr, dg = vjp(dv)
        return do, dr, dg

    return _rows_call(name, fn, (t // TM,),
                      [(of, _rowspec(512)), (ob, _rowspec(512)), (p, _rowspec(512, OFF["gr"])), (g, _full((1, 512))),
                       (dgla, _rowspec(512))],
                      [(_sds((t, 512), F32), _rowspec(512), False), (_sds((t, 512), BF16), _rowspec(512), False),
                       (_sds((1, 512), F32), _full((1, 512)), True)])


_TMM = 384


def _merge_fwd(name, gm, att, gla, wa, wb, wc, p):
    t = p.shape[0]
    row = lambda w, off=0: pl.BlockSpec((_TMM, w), lambda i, o=off // w: (i, o))

    def fn(ids, a, b, c, wa_, wb_, wc_, ga, gb, gc):
        ga, gb, gc = ga.astype(F32), gb.astype(F32), gc.astype(F32)
        return (jax.nn.sigmoid(ga) * bdot(a, wa_, 1, 0) + jax.nn.sigmoid(gb) * bdot(b, wb_, 1, 0)
                + jax.nn.sigmoid(gc) * bdot(c, wc_, 1, 0),)

    return _rows_call(name, fn, (t // _TMM,),
                      [(gm, row(512)), (att, row(512)), (gla, row(512)), (wa, _full((512, D))), (wb, _full((512, D))),
                       (wc, _full((512, D))), (p, row(D, OFF["gA"])), (p, row(D, OFF["gB"])), (p, row(D, OFF["gC"]))],
                      [(_sds((t, D), BF16), row(D), False)])[0]


def _merge_bwd(name, gm, att, gla, wa, wb, wc, p, dmerged):
    t = p.shape[0]
    row = lambda w, off=0: pl.BlockSpec((_TMM, w), lambda i, o=off // w: (i, o))

    def fn(ids, a, b, c, wa_, wb_, wc_, ga, gb, gc, dm):
        ga, gb, gc = ga.astype(F32), gb.astype(F32), gc.astype(F32)
        outs_y, outs_g = [], []
        for br, w, g in ((a, wa_, ga), (b, wb_, gb), (c, wc_, gc)):
            s = jax.nn.sigmoid(g)
            outs_y.append(dm * s)
            outs_g.append(dm * bdot(br, w, 1, 0) * s * (1.0 - s))
        return tuple(outs_y) + tuple(outs_g)

    o = (_sds((t, D), BF16), row(D), False)
    return _rows_call(name, fn, (t // _TMM,),
                      [(gm, row(512)), (att, row(512)), (gla, row(512)), (wa, _full((512, D))), (wb, _full((512, D))),
                       (wc, _full((512, D))), (p, row(D, OFF["gA"])), (p, row(D, OFF["gB"])), (p, row(D, OFF["gC"])),
                       (dmerged, row(D))], [o] * 6)


_TNC = 1408
_NJ = FFN // _TNC


HALO = 16


def _shift_rows(x, prev, nxt, vp, vn):
    n = x.shape[0]
    rid = lax.broadcasted_iota(jnp.int32, x.shape, 0)
    xp = jnp.where(rid == 0, jnp.where(vp, prev[HALO - 1:HALO, :], 0.0), pltpu.roll(x, 1, 0))
    xn = jnp.where(rid == n - 1, jnp.where(vn, nxt[0:1, :], 0.0), pltpu.roll(x, n - 1, 0))
    return xp, xn


def _seq_edges(i, t):
    start, end = i * TM, (i + 1) * TM
    return jnp.logical_and(start != 0, start != TC), jnp.logical_and(end != TC, end != t)


def _halo_specs(t, colmap):
    r = TM // HALO
    main = pl.BlockSpec((TM, _TNC), lambda j, i: (i, colmap(j)))
    prev = pl.BlockSpec((HALO, _TNC), lambda j, i: (jnp.maximum(i * r - 1, 0), colmap(j)))
    nxt = pl.BlockSpec((HALO, _TNC), lambda j, i: (jnp.minimum((i + 1) * r, t // HALO - 1), colmap(j)))
    return [main, prev, nxt]


def _conv3(x, xp, xn, w, b=None):
    y = xp * w[0:1, :] + x * w[1:2, :] + xn * w[2:3, :]
    return y if b is None else b + y


def _conv_fwd(name, a, cw, cb):
    t = a.shape[0]

    def fn(ids, ag, agp, agn, av, avp, avn, wg, wv, bg, bv):
        vp, vn = _seq_edges(ids[1], t)
        ag, agp, agn, av, avp, avn = (z.astype(F32) for z in (ag, agp, agn, av, avp, avn))
        cg = _conv3(ag, *_shift_rows(ag, agp, agn, vp, vn), wg, bg)
        cv = _conv3(av, *_shift_rows(av, avp, avn, vp, vn), wv, bv)
        return (jax.nn.silu(cg) * cv,)

    gcol, vcol = (lambda j: j), (lambda j: j + _NJ)
    wspec = lambda cm: pl.BlockSpec((3, _TNC), lambda j, i: (0, cm(j)))
    bspec = lambda cm: pl.BlockSpec((1, _TNC), lambda j, i: (0, cm(j)))
    ins = [(a, s) for s in _halo_specs(t, gcol) + _halo_specs(t, vcol)]
    ins += [(cw, wspec(gcol)), (cw, wspec(vcol)), (cb, bspec(gcol)), (cb, bspec(vcol))]
    return _rows_call(name, fn, (_NJ, t // TM), ins,
                      [(_sds((t, FFN), BF16), pl.BlockSpec((TM, _TNC), lambda j, i: (i, j)), False)])[0]


def _conv_bwd(name, a, cw, cb, dact):
    t = a.shape[0]
    n = TM + 2 * HALO

    def fn(ids, ag, agp, agn, av, avp, avn, dv, dvp, dvn, wg, wv, bg, bv):
        vp, vn = _seq_edges(ids[1], t)
        ag, agp, agn, av, avp, avn = (z.astype(F32) for z in (ag, agp, agn, av, avp, avn))
        ext = lambda x, xp, xn: jnp.concatenate([jnp.where(vp, xp, 0.0), x, jnp.where(vn, xn, 0.0)], axis=0)
        up, dn = (lambda x: pltpu.roll(x, 1, 0)), (lambda x: pltpu.roll(x, n - 1, 0))
        main = lambda y: y[HALO:HALO + TM]
        eg, ev, ed = ext(ag, agp, agn), ext(av, avp, avn), ext(dv, dvp, dvn)
        cg = _conv3(eg, up(eg), dn(eg), wg, bg)
        cv = _conv3(ev, up(ev), dn(ev), wv, bv)
        s = jax.nn.sigmoid(cg)
        rid = lax.broadcasted_iota(jnp.int32, (3, eg.shape[1]), 0)
        das, dws, dbs = [], [], []
        for dc, w, x in ((ed * cv * s * (1.0 + cg * (1.0 - s)), wg, eg), (ed * cg * s, wv, ev)):
            shifted = [main(dn(dc)), main(dc), main(up(dc))]
            das.append(shifted[0] * w[0:1, :] + shifted[1] * w[1:2, :] + shifted[2] * w[2:3, :])
            sums = [jnp.sum(y * main(x), axis=0, keepdims=True) for y in shifted]
            dws.append(jnp.where(rid == 0, sums[0], jnp.where(rid == 1, sums[1], sums[2])))
            dbs.append(jnp.sum(shifted[1], axis=0, keepdims=True))
        return jnp.stack(das), jnp.stack(dws), jnp.stack(dbs)

    gcol, vcol = (lambda j: j), (lambda j: j + _NJ)
    wspec = lambda cm: pl.BlockSpec((3, _TNC), lambda j, i: (0, cm(j)))
    bspec = lambda cm: pl.BlockSpec((1, _TNC), lambda j, i: (0, cm(j)))
    ins = [(a, s) for s in _halo_specs(t, gcol) + _halo_specs(t, vcol)] + [(dact, s) for s in _halo_specs(t, gcol)]
    ins += [(cw, wspec(gcol)), (cw, wspec(vcol)), (cb, bspec(gcol)), (cb, bspec(vcol))]
    return _rows_call(name, fn, (_NJ, t // TM), ins,
                      [(_sds((2, t, FFN), BF16), pl.BlockSpec((2, TM, _TNC), lambda j, i: (0, i, j)), False),
                       (_sds((2, 3, FFN), F32), pl.BlockSpec((2, 3, _TNC), lambda j, i: (0, 0, j)), True),
                       (_sds((2, 1, FFN), F32), pl.BlockSpec((2, 1, _TNC), lambda j, i: (0, 0, j)), True)])


_TNA = 512


def _adaln_fwd(name, cond, w, b):
    fn = lambda ids, cv, wv, bv: ((bdot(jax.nn.silu(cv), wv[0], 1, 0) + bv[0])[None],)
    return _rows_call(name, fn, (2, ADA_LOC // _TNA),
                      [(cond, _full((16, D))), (w, pl.BlockSpec((1, D, _TNA), lambda l, j: (l, 0, j))),
                       (b, pl.BlockSpec((1, 1, _TNA), lambda l, j: (l, 0, j)))],
                      [(_sds((2, 16, ADA_LOC), F32), pl.BlockSpec((1, 16, _TNA), lambda l, j: (l, 0, j)), False)])[0]


def _adaln_bwd(name, c8, cc8, w, dl, dc):
    def fn(ids, cv, ccv, wv, dlv, dcv):
        dcs = jnp.broadcast_to(jnp.sum(dcv[0], axis=0, keepdims=True), dcv[0].shape)
        dw = hdot(jax.nn.silu(cv), dlv[0], 0, 0) + hdot(jax.nn.silu(ccv), dcs, 0, 0)
        s = jax.nn.sigmoid(ccv)
        rid = lax.broadcasted_iota(jnp.int32, ccv.shape, 0)
        dcc = jnp.where(rid == 0, bdot(dcs, wv[0], 1, 1) * s * (1.0 + ccv * (1.0 - s)), 0.0)
        return dw[None], dcc

    dspec = pl.BlockSpec((1, 8, _TNA), lambda l, j: (l, 0, j))
    return _rows_call(name, fn, (2, ADA_LOC // _TNA),
                      [(c8, _full((8, D))), (cc8, _full((8, D))), (w, pl.BlockSpec((1, D, _TNA), lambda l, j: (l, 0, j))),
                       (dl, dspec), (dc, dspec)],
                      [(_sds((2, D, ADA_LOC), F32), pl.BlockSpec((1, D, _TNA), lambda l, j: (l, 0, j)), False),
                       (_sds((8, D), F32), _full((8, D)), True)], acc_axes=(0, 1))


def _adamw_fn(w, g, m, v):
    m = ADAM_B1 * m + (1.0 - ADAM_B1) * g
    v = ADAM_B2 * v + (1.0 - ADAM_B2) * (g * g)
    m_hat = m / (1.0 - ADAM_B1 ** ADAM_STEP)
    v_hat = v / (1.0 - ADAM_B2 ** ADAM_STEP)
    return -ADAM_LR * (m_hat / (jnp.sqrt(v_hat) + ADAM_EPS) + ADAM_WD * w), m, v


def _adamw(name, w, g, m, v):
    l, r, c = w.shape
    tr = _tile(r, max(8, (1 << 20) // (4 * c)), 8)
    spec = pl.BlockSpec((None, tr, c), lambda i, j: (i, j, 0))
    o = (_sds((l, r, c), F32), spec, False)
    return _rows_call(name, lambda ids, *a: _adamw_fn(*a), (l, r // tr), [(x, spec) for x in (w, g, m, v)], [o, o, o],
                      sem=("parallel", "parallel"))


def _coords():
    return lax.axis_index("x"), lax.axis_index("y"), lax.axis_index("c")


def _other_chips(x, y):
    return [(1 - x, y), (x, 1 - y), (1 - x, 1 - y)]


def _allgather_small(name, blk):
    m_per, n = blk.shape

    def body(x_ref, out_ref, send_sems, recv_sems, local_sem):
        x, y, c = _coords()
        me, sibling = (x, y, c), (x, y, 1 - c)
        chips = _other_chips(x, y)

        def rows(px, py, pc):
            return out_ref.at[pl.ds((4 * px + 2 * py + pc) * m_per, m_per), :]

        def copy(k, block, to, src=None):
            return pltpu.make_async_remote_copy(
                src_ref=rows(*block) if src is None else src, dst_ref=rows(*block), send_sem=send_sems.at[k],
                recv_sem=recv_sems.at[k], device_id=to, device_id_type=MESH)

        mine = pltpu.make_async_copy(x_ref, rows(*me), local_sem)
        mine.start()
        first = [copy(0, me, sibling, src=x_ref)]
        first += [copy(1 + j, me, (*chip, c), src=x_ref) for j, chip in enumerate(chips)]
        for cp in first:
            cp.start()
        passed = [copy(4 + j, (*chip, c), sibling) for j, chip in enumerate(chips)]
        for j, chip in enumerate(chips):
            copy(1 + j, (*chip, c), me).wait_recv()
            passed[j].start()
        copy(0, sibling, me).wait_recv()
        for j, chip in enumerate(chips):
            copy(4 + j, (*chip, 1 - c), me).wait_recv()
        for cp in first + passed:
            cp.wait_send()
        mine.wait()

    return pl.pallas_call(
        body, name=name, out_shape=_sds((N_DEV * m_per, n), blk.dtype),
        in_specs=[pl.BlockSpec(memory_space=pltpu.VMEM)], out_specs=pl.BlockSpec(memory_space=pltpu.VMEM),
        scratch_shapes=[pltpu.SemaphoreType.DMA((7,)), pltpu.SemaphoreType.DMA((7,)), pltpu.SemaphoreType.DMA],
        compiler_params=pltpu.CompilerParams(vmem_limit_bytes=VMEM_LIMIT),
    )(blk)


_ANY = pl.BlockSpec(memory_space=pl.ANY)


def _remote(src, dst, send_sems, recv_sems, s, to):
    return pltpu.make_async_remote_copy(src_ref=src, dst_ref=dst, send_sem=send_sems.at[s], recv_sem=recv_sems.at[s],
                                        device_id=to, device_id_type=MESH)


def _comm_call(name, body, ins, out_shapes, n_sems, n_local):
    return pl.pallas_call(
        body, name=name, out_shape=out_shapes, in_specs=[_ANY] * len(ins), out_specs=[_ANY] * len(out_shapes),
        scratch_shapes=[pltpu.SemaphoreType.DMA((n_sems,)), pltpu.SemaphoreType.DMA((n_sems,)),
                        pltpu.SemaphoreType.DMA((n_local,))],
    )(*ins)


def _gather_stage(stage, ins, outs, send_sems, recv_sems):
    n = len(ins)
    x, y, c = _coords()
    k = 2 * x + y
    sibling = (x, y, 1 - c)
    chips = _other_chips(x, y)
    first = [_remote(ins[t].at[c], outs[t].at[k, c], send_sems, recv_sems, 6 * t + j, (*chip, c))
             for t in range(n) for j, chip in enumerate(chips)]
    there = lambda t, j, half: outs[t].at[2 * chips[j][0] + chips[j][1], half]
    passed = [_remote(there(t, j, c), there(t, j, c), send_sems, recv_sems, 6 * t + 3 + j, sibling)
              for t in range(n) for j in range(3)]
    if stage == 0:
        for cp in first:
            cp.start()
    elif stage == 1:
        for t in range(n):
            for j in range(3):
                _remote(there(t, j, c), there(t, j, c), send_sems, recv_sems, 6 * t + j, sibling).wait_recv()
                passed[3 * t + j].start()
    else:
        for t in range(n):
            for j in range(3):
                _remote(there(t, j, 1 - c), there(t, j, 1 - c), send_sems, recv_sems, 6 * t + 3 + j, sibling).wait_recv()
        for cp in first + passed:
            cp.wait_send()


def _gather_own(outs, locs):
    k = 2 * lax.axis_index("x") + lax.axis_index("y")
    return [lax.dynamic_update_slice_in_dim(o, a[None], k, axis=0) for o, a in zip(outs, locs)]


def _allgather_layers(name, locs):
    n = len(locs)

    def body(*refs):
        ins, outs, (send_sems, recv_sems, _) = refs[:n], refs[n:2 * n], refs[2 * n:]
        for stage in range(3):
            _gather_stage(stage, ins, outs, send_sems, recv_sems)

    return _gather_own(_comm_call(name, body, locs, [_sds((N_CHIP,) + a.shape, a.dtype) for a in locs], 6 * n, 1), locs)


def _rs_pair_exchange(name, gs):
    n = len(gs)

    def body(*refs):
        ins, outs, (send_sems, recv_sems, _) = refs[:n], refs[n:2 * n], refs[2 * n:]
        x, y, c = _coords()
        cps = [_remote(ins[t].at[kk, 1 - c], outs[t].at[kk], send_sems, recv_sems, N_CHIP * t + kk, (x, y, 1 - c))
               for t in range(n) for kk in range(N_CHIP)]
        for cp in cps:
            cp.start()
        for cp in cps:
            cp.wait()

    return _comm_call(name, body, gs, [_sds((N_CHIP,) + a.shape[2:], a.dtype) for a in gs], N_CHIP * n, 1)


def _rs_pair_add(name, g, got):
    _, _, r, c = g.shape
    tr = _tile(r, max(16, (1 << 20) // (4 * c)), 16)
    core = jnp.reshape(lax.axis_index("c"), (1,)).astype(jnp.int32)

    def body(core_ref, a_ref, b_ref, o_ref):
        o_ref[...] = (a_ref[...] + b_ref[...]).astype(o_ref.dtype)

    spec = pl.BlockSpec((None, tr, c), lambda kk, i, cr: (kk, i, 0))
    return pl.pallas_call(
        body, name=name, out_shape=_sds((N_CHIP, r, c), BF16),
        grid_spec=pltpu.PrefetchScalarGridSpec(
            num_scalar_prefetch=1, grid=(N_CHIP, r // tr),
            in_specs=[pl.BlockSpec((None, None, tr, c), lambda kk, i, cr: (kk, cr[0], i, 0)), spec], out_specs=spec),
        compiler_params=_cp(("parallel", "parallel")),
    )(core, g, got)


def _exchange_stage(stage, ins, outs, send_sems, recv_sems):
    n = len(ins)
    x, y, c = _coords()
    k = 2 * x + y
    chips = _other_chips(x, y)
    cps = [_remote(ins[t].at[2 * cx + cy], outs[t].at[k], send_sems, recv_sems, 3 * t + j, (cx, cy, c))
           for t in range(n) for j, (cx, cy) in enumerate(chips)]
    if stage == 0:
        for cp in cps:
            cp.start()
    else:
        for t in range(n):
            for j, (cx, cy) in enumerate(chips):
                there = outs[t].at[2 * cx + cy]
                _remote(there, there, send_sems, recv_sems, 3 * t + j, (cx, cy, c)).wait_recv()
        for cp in cps:
            cp.wait_send()


def _exchange_own(outs, s1):
    k = 2 * lax.axis_index("x") + lax.axis_index("y")
    own = [lax.dynamic_index_in_dim(a, k, axis=0, keepdims=True) for a in s1]
    return [lax.dynamic_update_slice_in_dim(o, a, k, axis=0) for o, a in zip(outs, own)]


def _rs_chip_exchange(name, s1):
    n = len(s1)

    def body(*refs):
        ins, outs, (send_sems, recv_sems, _) = refs[:n], refs[n:2 * n], refs[2 * n:]
        for stage in range(2):
            _exchange_stage(stage, ins, outs, send_sems, recv_sems)

    return _exchange_own(_comm_call(name, body, s1, [_sds(a.shape, a.dtype) for a in s1], 3 * n, 1), s1)


def _sum_slots(name, a):
    s, r, cdim = a.shape
    tr = _tile(r, 512, 8)

    def fn(ids, av):
        tot = av[0]
        for i in range(1, s):
            tot = tot + av[i]
        return (tot,)

    return _rows_call(name, fn, (r // tr,), [(a, pl.BlockSpec((s, tr, cdim), lambda i: (0, i, 0)))],
                      [(_sds((r, cdim), F32), pl.BlockSpec((tr, cdim), lambda i: (i, 0)), False)], sem=("parallel",))[0]


def _pair_allgather(name, red0, red1):
    n = len(red0)

    def body(*refs):
        ins, outs, (send_sems, recv_sems, _) = (refs[:n], refs[n:2 * n]), refs[2 * n:3 * n], refs[3 * n:]
        x, y, c = _coords()
        cps = [_remote(ins[l][t], outs[t].at[l, c], send_sems, recv_sems, 2 * t + l, (x, y, 1 - c))
               for t in range(n) for l in range(2)]
        for cp in cps:
            cp.start()
        for t in range(n):
            for l in range(2):
                _remote(ins[l][t], outs[t].at[l, 1 - c], send_sems, recv_sems, 2 * t + l, (x, y, 1 - c)).wait_recv()
        for cp in cps:
            cp.wait_send()

    outs = _comm_call(name, body, list(red0) + list(red1), [_sds((2, 2) + a.shape, a.dtype) for a in red0], 2 * n, 1)
    c = lax.axis_index("c")
    own = [jnp.stack([a, b])[:, None] for a, b in zip(red0, red1)]
    return [lax.dynamic_update_slice_in_dim(o, a, c, axis=1) for o, a in zip(outs, own)]


def _rs_front(tag, gs):
    halves = [g.reshape(N_CHIP, 2, g.shape[1] // 2, g.shape[2]) for g in gs]
    got = _rs_pair_exchange(tag + "rs_pair_exchange", halves)
    return [_rs_pair_add(tag + "rs_pair_add_%d" % t, h, r) for t, (h, r) in enumerate(zip(halves, got))]


def _rs_sum(tag, slots):
    return [_sum_slots(tag + "rs_chip_sum_%d" % t, a) for t, a in enumerate(slots)]


PACK_C = 1024
_SHARDED = (("w_in", 1), ("w_br_a", 1), ("w_br_b", 1), ("w_br_c", 1), ("w_out", 0), ("w_ffn_up", 1), ("w_ffn_down", 0))
_SHARDED_SMALL = (("conv_w", (3, 2 * FFN), 1), ("w_alpha2", (2, 16, GLA_QK), 2), ("b_alpha", (2, GLA_QK), 1))


def _prod(shape):
    n = 1
    for s in shape:
        n *= s
    return n


def _to_blocks(full, axis):
    shp = full.shape
    split = full.reshape(shp[:axis] + (N_CHIP, shp[axis] // N_CHIP) + shp[axis + 1:])
    return jnp.moveaxis(split, axis, 0)


def _from_blocks(blocks, axis):
    return jnp.concatenate([blocks[k] for k in range(N_CHIP)], axis=axis)


def _rope_tables(tx):
    pos = jnp.arange(tx, dtype=jnp.int32)
    inv_freq = 10000.0 ** (-jnp.arange(16, dtype=F32) / 16)
    ang_r = (pos // GRID_W).astype(F32)[:, None] * inv_freq
    ang_c = (pos % GRID_W).astype(F32)[:, None] * inv_freq
    ang = jnp.concatenate([ang_r, ang_r, ang_c, ang_c], axis=-1)
    sign = jnp.concatenate([-jnp.ones((16,), F32), jnp.ones((16,), F32)] * 2)
    cos = jnp.concatenate([jnp.ones((TC, HD), F32), jnp.cos(ang)], axis=0)
    sin = jnp.concatenate([jnp.zeros((TC, HD), F32), jnp.sin(ang) * sign], axis=0)
    return jnp.tile(cos, (1, 2)), jnp.tile(sin, (1, 2))


def _lane_consts():
    l = jnp.arange(512)
    seg = (l[:, None] // HD == l[None, :] // HD).astype(F32) / HD
    partner = jnp.where(l % 32 < 16, l + 16, l - 16)
    perm = (l[:, None] == partner[None, :]).astype(F32)
    return seg, perm


def _heads(a, n):
    return a.reshape(a.shape[0], n, HD).transpose(1, 0, 2)


def _unheads(a):
    return a.transpose(1, 0, 2).reshape(a.shape[1], a.shape[0] * HD)


def _gather_f32_shards(shards):
    sizes = [_prod(a.shape) for a in shards]
    flat = jnp.concatenate([a.reshape(-1) for a in shards] + [jnp.zeros((16 * PACK_C - sum(sizes),), F32)])
    got = _allgather_small("gather_f32_shards", flat.reshape(16, PACK_C)).reshape(N_CHIP, 2, 16 * PACK_C)[:, 0]
    out, o = {}, 0
    for (n, _, ax), a, sz in zip(_SHARDED_SMALL, shards, sizes):
        out[n] = jnp.concatenate([got[k, o:o + sz].reshape(a.shape) for k in range(N_CHIP)], axis=ax + 1)
        o += sz
    return out


def _halves(a):
    return a.reshape(2, a.shape[0] // 2, a.shape[1])


def _layer_shards(W, l):
    return [_halves(W[n][l].astype(BF16)) for n, _ in _SHARDED]


def _layer_params(l, gathered, small):
    w2 = small["w_alpha2_full"][l]
    w2pad = jnp.zeros((128, 512), F32).at[0:16, 0:256].set(w2[0]).at[16:32, 256:512].set(w2[1])
    full = {n: _from_blocks(g.reshape(N_CHIP, 2 * g.shape[2], g.shape[3]), ax) for (n, ax), g in zip(_SHARDED, gathered)}
    return dict(
        w_in=_to_new_cols(full["w_in"]), wa=full["w_br_a"], wb=full["w_br_b"], wc=full["w_br_c"],
        w_out=full["w_out"], w_up=full["w_ffn_up"], w_down=full["w_ffn_down"],
        cw=small["conv_w_full"][l], cb=small["conv_b"][l][None], w2=w2pad,
        b2=small["b_alpha_full"][l].reshape(1, 512),
        g1=small["norm1_g"][l][None], g2=small["norm2_g"][l][None], gq=jnp.tile(small["q_norm_g"][l], 8)[None],
        gk=jnp.tile(small["k_norm_g"][l], 2)[None], ggm=small["gmlp_norm_g"][l][None], ws=small["w_spatial"][l],
        bst=small["b_spatial"][l].T, ggl=small["gla_norm_g"][l][None])


def _layer_fwd(l, last, x, h1, mod, P, tabs, gather=()):
    cos, sin, seg, perm = tabs
    n = "l%d_" % l
    s = dict(x=x, h1=h1)
    p = _mm(n + "in_proj", h1, P["w_in"], "nn", BF16, tm_t=768, tn_t=2176, j_outer=True)
    s["p"] = p
    s["gm"] = _gmlp_fwd(n + "gmlp", p, P["ggm"], P["ws"], P["bst"])
    qr, kr, vb = _qk_fwd(n + "qk_prep", p, P["gq"], P["gk"], cos, sin, seg, perm)
    qx, qc, kh, vh = _heads(qr[TC:], NQ), _heads(qr[:TC], NQ), _heads(kr, NKV), _heads(vb, NKV)
    s["qx"], s["qc"], s["kh"], s["vh"] = qx, qc, kh, vh
    one_hot = (jnp.arange(HD) == 0).astype(BF16)
    v1 = jnp.concatenate([vh, jnp.broadcast_to(one_hot, vh.shape)], axis=-1)
    ox, lse_x, *s["gathered"] = _attn_fwd(n + "attn_x", qx, kh, v1, gather)
    s["ox"], s["lse_x"] = ox, lse_x
    if last:
        oc = jnp.zeros((NQ, TC, HD), F32)
    else:
        oc, lse_c = _attn_fwd(n + "attn_c", qc, kh[:, :TC], v1[:, :TC])
        s["oc"], s["lse_c"] = oc, lse_c
    s["att"] = jnp.concatenate([_unheads(oc), _unheads(ox)], axis=0).astype(BF16)
    la = _decay_fwd(n + "gla_decay", p, P["w2"], P["b2"])
    s["la"] = la
    s["of"], s["sf"], s["ob"], s["sb"] = _gla_fwd(n + "gla_scan", p, la)
    s["gla"] = _gla_out_fwd(n + "gla_out", s["of"], s["ob"], p, P["ggl"])
    s["merged"] = _merge_fwd(n + "merge", s["gm"], s["att"], s["gla"], P["wa"], P["wb"], P["wc"], p)
    s["mix"] = _mm(n + "out_proj", s["merged"], P["w_out"], "nn", F32)
    s["x_mid"], s["h2"] = _res_nm_fwd(n + "res1_norm2", x, s["mix"], mod, 2, mod, P["g2"], 3, 4)
    s["a"] = _mm(n + "ffn_up", s["h2"], P["w_up"], "nn", BF16, j_outer=True)
    s["act"] = _conv_fwd(n + "conv_gate", s["a"], P["cw"], P["cb"])
    s["f"] = _mm(n + "ffn_down", s["act"], P["w_down"], "nn", F32)
    return s


def _layer_bwd(l, last, s, mod, P, tabs, dx_mid, df, gw, exchange=()):
    cos, sin, seg, perm = tabs
    n = "l%d_b_" % l
    t = dx_mid.shape[0]
    p = s["p"]
    gw["w_ffn_down"] = _mm(n + "ffn_down_w", s["act"], df, "tn", F32, tm_t=1408)
    dact = _mm(n + "ffn_down_x", df, P["w_down"], "nt", F32)
    da, dcw, dcb = _conv_bwd(n + "conv_gate", s["a"], P["cw"], P["cb"], dact)
    gw["conv_w"], gw["conv_b"] = dcw.transpose(1, 0, 2).reshape(3, 2 * FFN), dcb.reshape(2 * FFN)
    gw["w_ffn_up"] = _mm(n + "ffn_up_w", s["h2"], da, "tn", F32, chip_blocks=True)
    dh2 = _mm(n + "ffn_up_x", da, P["w_up"], "nt", F32)
    dx, dmix, dmod_a, dmod_b, dg2 = _res_nm_bwd(n + "res1_norm2", s["x"], s["mix"], mod, 2, mod, P["g2"], 3, 4, dx_mid, dh2)
    dmod = dmod_a + dmod_b
    gw["norm2_g"] = dg2[0]
    gw["w_out"] = _mm(n + "out_proj_w", s["merged"], dmix, "tn", F32)
    dmerged = _mm(n + "out_proj_x", dmix, P["w_out"], "nt", F32)
    dya, dyb, dyc, dga, dgb, dgc = _merge_bwd(n + "merge", s["gm"], s["att"], s["gla"], P["wa"], P["wb"], P["wc"], p, dmerged)
    gw["w_br_a"] = _mm(n + "br_a_w", s["gm"], dya, "tn", F32)
    gw["w_br_b"] = _mm(n + "br_b_w", s["att"], dyb, "tn", F32)
    gw["w_br_c"] = _mm(n + "br_c_w", s["gla"], dyc, "tn", F32)
    dgm = _mm(n + "br_a_x", dya, P["wa"], "nt", F32)
    datt = _mm(n + "br_b_x", dyb, P["wb"], "nt", F32)
    dgla = _mm(n + "br_c_x", dyc, P["wc"], "nt", F32)
    du, dv_g, dggm, dws, dbst = _gmlp_bwd(n + "gmlp", p, P["ggm"], P["ws"], P["bst"], dgm)
    gw["gmlp_norm_g"], gw["w_spatial"], gw["b_spatial"] = dggm[0], dws, dbst.T
    kh, vh = s["kh"], s["vh"]
    row = lambda a: a.reshape(a.shape[0], 1, a.shape[1])
    dqx, dkh, dvh, *gw["exchanged"] = _attn_bwd(n + "attn_x", s["qx"], kh, vh, s["ox"], _heads(datt[TC:], NQ),
                                                row(s["lse_x"]), exchange)
    if last:
        dqc = jnp.zeros((NQ, TC, HD), F32)
    else:
        dqc, dkc, dvc = _attn_bwd(n + "attn_c", s["qc"], kh[:, :TC], vh[:, :TC], s["oc"], _heads(datt[:TC], NQ),
                                  row(s["lse_c"]))
        pad = jnp.zeros((NKV, t - TC, HD), F32)
        dkh = dkh + jnp.concatenate([dkc, pad], axis=1)
        dvh = dvh + jnp.concatenate([dvc, pad], axis=1)
    dqr = jnp.concatenate([_unheads(dqc), _unheads(dqx)], axis=0)
    dq, dk, dgq, dgk = _qk_bwd(n + "qk_prep", p, P["gq"], P["gk"], cos, sin, seg, perm, dqr, _unheads(dkh))
    gw["q_norm_g"], gw["k_norm_g"] = dgq.reshape(8, HD).sum(0), dgk.reshape(2, HD).sum(0)
    dv_att = _unheads(dvh).astype(BF16)
    do, dr, dggl = _gla_out_bwd(n + "gla_out", s["of"], s["ob"], p, P["ggl"], dgla)
    gw["gla_norm_g"] = dggl[0]
    scans = _gla_bwd(n + "gla_scan", p, s["la"], s["sf"], s["sb"], do)
    dab, dw2, db2, dglq, dglk, dglv = _decay_bwd(n + "gla_decay", p, P["w2"], P["b2"], scans[:4], scans[4:])
    gw["w_alpha2"] = jnp.stack([dw2[0:16, 0:256], dw2[16:32, 256:512]])
    gw["b_alpha"] = db2.reshape(2, 256)
    dp = jnp.concatenate([dga, dgb, dgc, du, dv_g, dq, dglv, dr, dglq, dglk, dk, dv_att, dab], axis=-1)
    gw["w_in"] = _to_ref_cols(_mm(n + "in_proj_w", s["h1"], dp, "tn", F32, tn_t=2176, tk_t=768))
    dh1 = _mm(n + "in_proj_x", dp, P["w_in"], "nt", F32, tk_t=2176)
    return dx, dh1, dmod


_SMALL = (("norm1_g", (2, D)), ("norm2_g", (2, D)), ("q_norm_g", (2, HD)), ("k_norm_g", (2, HD)), ("gmlp_norm_g", (2, GW)),
          ("gla_norm_g", (2, GLA_V)), ("w_spatial", (2, 4, 128, 128)), ("b_spatial", (2, 4, 128)), ("conv_b", (2, 2 * FFN)),
          ("final_norm_g", (D,))) + tuple((n, (2,) + s) for n, s, _ in _SHARDED_SMALL)
_SMALL_N = 2 * 2 * ADA_W + sum(_prod(s) for _, s in _SMALL)
_SMALL_R = -(-_SMALL_N // (PACK_C * 8)) * 8


def _mod_tables(c, c_ctx, w_ada, b_ada, k):
    x, y, cc = _coords()
    me = 4 * x + 2 * y + cc
    c_all = _allgather_small("gather_c", jnp.concatenate([c, jnp.zeros((7, D), F32)], axis=0))
    c8 = c_all.reshape(N_DEV, 8, D)[:, 0]
    cond = jnp.concatenate([c8, c_ctx[None], jnp.zeros((7, D), F32)], axis=0)
    b_loc = lax.dynamic_slice_in_dim(b_ada, k * ADA_LOC, ADA_LOC, axis=1)[:, None, :]
    m_loc = _adaln_fwd("adaln", cond, w_ada, b_loc)
    m_all = _allgather_small("gather_mod", m_loc.reshape(32, ADA_LOC)).reshape(N_CHIP, 2, 2, 16, ADA_LOC)[:, 0]
    m_all = m_all.transpose(1, 2, 0, 3).reshape(2, 16, ADA_W)
    rows = jnp.stack([m_all[:, 8], lax.dynamic_index_in_dim(m_all, me, axis=1, keepdims=False)], axis=1)
    return rows.reshape(2, 2, 6, D), c8


def _step(x, c, ctx, c_ctx, W, tgt):
    xc, yc, cc = _coords()
    k = 2 * xc + yc
    tx = x.shape[0]
    t = TC + tx
    small = {n: W[n] for n, _ in _SMALL}
    for n, a in _gather_f32_shards([W[n] for n, _, _ in _SHARDED_SMALL]).items():
        small[n + "_full"] = a

    shards = [_layer_shards(W, l) for l in range(2)]
    mods, c8 = _mod_tables(c, c_ctx, W["w_ada"], W["b_ada"], k)
    tabs = _rope_tables(tx) + _lane_consts()
    params = [_layer_params(0, _allgather_layers("gather_weights", shards[0]), small)]

    xs = jnp.concatenate([ctx, x], axis=0)
    h1 = _nm_fwd("l0_norm1", xs, mods[0], params[0]["g1"], 0, 1)
    s0 = _layer_fwd(0, False, xs, h1, mods[0], params[0], tabs, gather=shards[1])
    params.append(_layer_params(1, _gather_own(s0["gathered"], shards[1]), small))
    x1, h1b = _res_nm_fwd("l0_res2_norm1", s0["x_mid"], s0["f"], mods[0], 5, mods[1], params[1]["g1"], 0, 1)
    s1 = _layer_fwd(1, True, x1, h1b, mods[1], params[1], tabs)
    loss, dxm_l, df_l, dmod_head, dgf = _head("head", s1["x_mid"], s1["f"], mods[1], W["final_norm_g"][None], tgt)

    gws = [dict(), dict()]
    dx1, dh1b, dmod1 = _layer_bwd(1, True, s1, mods[1], params[1], tabs, dxm_l, df_l, gws[1])
    dxm0, df0, dmod0_g, dmod1_s, dg1b = _res_nm_bwd("l0_b_res2_norm1", s0["x_mid"], s0["f"], mods[0], 5, mods[1],
                                                    params[1]["g1"], 0, 1, dx1, dh1b)
    gws[1]["norm1_g"] = dg1b[0]
    blocks = lambda g, n, ax: g if n == "w_ffn_up" else _to_blocks(g, ax)
    sums1 = _rs_front("l1_", [blocks(gws[1][n], n, ax) for n, ax in _SHARDED])
    dx0, dh1, dmod0 = _layer_bwd(0, False, s0, mods[0], params[0], tabs, dxm0, df0, gws[0], exchange=sums1)
    red1 = _rs_sum("l1_", _exchange_own(gws[0]["exchanged"], sums1))
    grad_x, dmod0_s, dg1 = _nm_bwd("l0_b_norm1", xs, mods[0], params[0]["g1"], 0, 1, dx0, dh1)
    gws[0]["norm1_g"] = dg1[0]
    dmods = jnp.stack([dmod0 + dmod0_g + dmod0_s, dmod1 + dmod1_s + dmod_head])

    stk = {n: jnp.stack([gws[0][n], gws[1][n]]) for n, _ in _SMALL if n != "final_norm_g"}
    stk["final_norm_g"] = dgf[0]
    flat = jnp.concatenate([dmods.reshape(-1)] + [stk[n].reshape(-1) for n, _ in _SMALL])
    flat = jnp.concatenate([flat, jnp.zeros((_SMALL_R * PACK_C - _SMALL_N,), F32)]).reshape(_SMALL_R, PACK_C)
    every = _allgather_small("gather_small_grads", flat).reshape(N_DEV, _SMALL_R, PACK_C)
    tot = _sum_slots("sum_small_grads", every).reshape(-1)
    grads, o = {}, 2 * 2 * ADA_W
    for n, shp in _SMALL:
        grads[n] = tot[o:o + _prod(shp)].reshape(shp)
        o += _prod(shp)
    grads["b_ada"] = tot[:2 * 2 * ADA_W].reshape(2, 2, ADA_W).sum(axis=1)

    dm_every = every[:, :2 * 2 * ADA_W // PACK_C].reshape(N_DEV, 2, 2, ADA_W)
    dm_loc = lax.dynamic_slice_in_dim(dm_every, k * ADA_LOC, ADA_LOC, axis=3).transpose(1, 2, 0, 3)
    cc8 = jnp.concatenate([c_ctx[None], jnp.zeros((7, D), F32)], axis=0)
    grads["w_ada"], dcc = _adaln_bwd("adaln_b", c8, cc8, W["w_ada"], dm_loc[:, 1], dm_loc[:, 0])
    dcc_every = _allgather_small("gather_dcctx", dcc * 0.5).reshape(N_DEV, 8, D)
    grads["c_ctx"] = _sum_slots("sum_dcctx", dcc_every)[0]

    for n, shp, ax in _SHARDED_SMALL:
        grads[n] = lax.dynamic_slice_in_dim(grads[n], k * (shp[ax] // N_CHIP), shp[ax] // N_CHIP, axis=ax + 1)
    sums0 = _rs_front("l0_", [blocks(gws[0][n], n, ax) for n, ax in _SHARDED])
    red0 = _rs_sum("l0_", _rs_chip_exchange("l0_rs_chip_exchange", sums0))
    for (n, _), a in zip(_SHARDED, _pair_allgather("rs_pair_allgather", red0, red1)):
        grads[n] = a.reshape(2, 2 * a.shape[2], a.shape[3])
    return loss[0, 0], grad_x, grads


_WEIGHTS = ("c_ctx", "w_ada", "b_ada", "norm1_g", "norm2_g", "w_in", "q_norm_g", "k_norm_g", "gmlp_norm_g", "w_spatial",
            "b_spatial", "w_alpha2", "b_alpha", "gla_norm_g", "w_br_a", "w_br_b", "w_br_c", "w_out", "w_ffn_up", "conv_w",
            "conv_b", "w_ffn_down", "final_norm_g")
_BIG = ("w_ada", "w_in", "w_br_a", "w_br_b", "w_br_c", "w_out", "w_ffn_up", "w_ffn_down")


def _update(W, G, M, V):
    delta, new_m, new_v = {}, {}, {}
    for n in _BIG:
        delta[n], new_m[n], new_v[n] = _adamw("adamw_" + n, W[n], G[n], M[n], V[n])
    rest = [n for n in _WEIGHTS if n not in _BIG]
    tot = sum(_prod(W[n].shape) for n in rest)
    rows = -(-tot // (PACK_C * 8)) * 8

    def cat(dct):
        flat = jnp.concatenate([dct[n].reshape(-1) for n in rest] + [jnp.zeros((rows * PACK_C - tot,), F32)])
        return flat.reshape(1, rows, PACK_C)

    outs = _adamw("adamw_small", cat(W), cat(G), cat(M), cat(V))
    o = 0
    for n in rest:
        sz, shp = _prod(W[n].shape), W[n].shape
        delta[n], new_m[n], new_v[n] = (a.reshape(-1)[o:o + sz].reshape(shp) for a in outs)
        o += sz
    return delta, new_m, new_v


def kernel(x, c, ctx, c_ctx, w_ada, b_ada, norm1_g, norm2_g, w_in, q_norm_g, k_norm_g, gmlp_norm_g, w_spatial, b_spatial, w_alpha2, b_alpha, gla_norm_g, w_br_a, w_br_b, w_br_c, w_out, w_ffn_up, conv_w, conv_b, w_ffn_down, final_norm_g, loss_target, m_c_ctx, m_w_ada, m_b_ada, m_norm1_g, m_norm2_g, m_w_in, m_q_norm_g, m_k_norm_g, m_gmlp_norm_g, m_w_spatial, m_b_spatial, m_w_alpha2, m_b_alpha, m_gla_norm_g, m_w_br_a, m_w_br_b, m_w_br_c, m_w_out, m_w_ffn_up, m_conv_w, m_conv_b, m_w_ffn_down, m_final_norm_g, v_c_ctx, v_w_ada, v_b_ada, v_norm1_g, v_norm2_g, v_w_in, v_q_norm_g, v_k_norm_g, v_gmlp_norm_g, v_w_spatial, v_b_spatial, v_w_alpha2, v_b_alpha, v_gla_norm_g, v_w_br_a, v_w_br_b, v_w_br_c, v_w_out, v_w_ffn_up, v_conv_w, v_conv_b, v_w_ffn_down, v_final_norm_g):
    W = dict(c_ctx=c_ctx, w_ada=w_ada, b_ada=b_ada, norm1_g=norm1_g, norm2_g=norm2_g, w_in=w_in, q_norm_g=q_norm_g,
             k_norm_g=k_norm_g, gmlp_norm_g=gmlp_norm_g, w_spatial=w_spatial, b_spatial=b_spatial, w_alpha2=w_alpha2,
             b_alpha=b_alpha, gla_norm_g=gla_norm_g, w_br_a=w_br_a, w_br_b=w_br_b, w_br_c=w_br_c, w_out=w_out,
             w_ffn_up=w_ffn_up, conv_w=conv_w, conv_b=conv_b, w_ffn_down=w_ffn_down, final_norm_g=final_norm_g)
    M = dict(c_ctx=m_c_ctx, w_ada=m_w_ada, b_ada=m_b_ada, norm1_g=m_norm1_g, norm2_g=m_norm2_g, w_in=m_w_in,
             q_norm_g=m_q_norm_g, k_norm_g=m_k_norm_g, gmlp_norm_g=m_gmlp_norm_g, w_spatial=m_w_spatial,
             b_spatial=m_b_spatial, w_alpha2=m_w_alpha2, b_alpha=m_b_alpha, gla_norm_g=m_gla_norm_g, w_br_a=m_w_br_a,
             w_br_b=m_w_br_b, w_br_c=m_w_br_c, w_out=m_w_out, w_ffn_up=m_w_ffn_up, conv_w=m_conv_w, conv_b=m_conv_b,
             w_ffn_down=m_w_ffn_down, final_norm_g=m_final_norm_g)
    V = dict(c_ctx=v_c_ctx, w_ada=v_w_ada, b_ada=v_b_ada, norm1_g=v_norm1_g, norm2_g=v_norm2_g, w_in=v_w_in,
             q_norm_g=v_q_norm_g, k_norm_g=v_k_norm_g, gmlp_norm_g=v_gmlp_norm_g, w_spatial=v_w_spatial,
             b_spatial=v_b_spatial, w_alpha2=v_w_alpha2, b_alpha=v_b_alpha, gla_norm_g=v_gla_norm_g, w_br_a=v_w_br_a,
             w_br_b=v_w_br_b, w_br_c=v_w_br_c, w_out=v_w_out, w_ffn_up=v_w_ffn_up, conv_w=v_conv_w, conv_b=v_conv_b,
             w_ffn_down=v_w_ffn_down, final_norm_g=v_final_norm_g)
    loss_local, grad_x, G = _step(x[0], c, ctx[0], c_ctx, W, loss_target[0])
    loss = lax.psum(loss_local, ("x", "y", "c"))
    delta, new_m, new_v = _update(W, G, M, V)
    return (loss, grad_x[None], *[G[n] for n in _WEIGHTS], *[delta[n] for n in _WEIGHTS],
            *[new_m[n] for n in _WEIGHTS], *[new_v[n] for n in _WEIGHTS])
```

```python
import functools

import jax
import jax.numpy as jnp
from jax import lax
from jax.experimental import pallas as pl
from jax.experimental.pallas import tpu as pltpu

F32 = jnp.float32
BF16 = jnp.bfloat16

D = 1024
TC = 256
GRID_W = 64
EPS = 1e-6
HD = 64
NQ = 8
NKV = 2
QG = NQ // NKV
GLA_H = 4
GLA_DK = 64
GLA_DV = 128
GLA_QK = 256
GLA_V = 512
GLA_CHUNK = 64
GLA_TAU = 16.0
GW = 512
FFN = 2816
IN_W = 6432
PW = 6528
ADA_W = 6 * D
N_CHIP = 4
N_DEV = 8
ADA_LOC = ADA_W // N_CHIP

ADAM_LR = 0.001
ADAM_B1 = 0.9
ADAM_B2 = 0.999
ADAM_EPS = 1e-08
ADAM_WD = 0.01
ADAM_STEP = 10

TM = 256
NCB = TC // TM
LANE = 128
VMEM_LIMIT = 48 * 1024 * 1024
VMEM_LIMIT_ATTN_BWD = 56 * 1024 * 1024
MESH = pl.DeviceIdType.MESH

_COLS = (("gA", 3360, 1024), ("gB", 4384, 1024), ("gC", 5408, 1024), ("gu", 0, 512), ("gv", 512, 512),
         ("q", 1024, 512), ("glv", 2304, 512), ("gr", 2848, 512), ("glq", 1792, 256), ("glk", 2048, 256),
         ("k", 1536, 128), ("v", 1664, 128), ("ab", 2816, 32))
OFF = {}
_o = 0
for _n, _s, _w in _COLS:
    OFF[_n] = _o
    _o += max(_w, LANE)
assert _o == PW


def _to_new_cols(w):
    parts = [w[..., s:s + n] for _, s, n in _COLS]
    pad = jnp.zeros(w.shape[:-1] + (PW - IN_W,), w.dtype)
    return jnp.concatenate(parts + [pad], axis=-1)


def _to_ref_cols(w):
    by_start = sorted(_COLS, key=lambda t: t[1])
    return jnp.concatenate([w[..., OFF[n]:OFF[n] + wd] for n, _, wd in by_start], axis=-1)


def _tile(n, target, align=LANE):
    best = None
    t = align
    while t <= min(n, target):
        if n % t == 0:
            best = t
        t += align
    assert best is not None, (n, target, align)
    return best


def _cp(sem=None, vmem=VMEM_LIMIT):
    return pltpu.CompilerParams(dimension_semantics=sem, vmem_limit_bytes=vmem)


def _bdot_impl(a, b, ca, cb):
    return lax.dot_general(a.astype(BF16), b.astype(BF16), (((ca,), (cb,)), ((), ())),
                           preferred_element_type=F32)


@functools.partial(jax.custom_vjp, nondiff_argnums=(2, 3))
def bdot(a, b, ca, cb):
    return _bdot_impl(a, b, ca, cb)


def _bdot_fwd(a, b, ca, cb):
    return _bdot_impl(a, b, ca, cb), (a, b)


def _bdot_bwd(ca, cb, res, g):
    a, b = res
    da = bdot(g, b, 1, 1 - cb) if ca == 1 else bdot(b, g, 1 - cb, 1)
    db = bdot(a, g, 1 - ca, 0) if cb == 0 else bdot(g, a, 0, 1 - ca)
    return da.astype(a.dtype), db.astype(b.dtype)


bdot.defvjp(_bdot_fwd, _bdot_bwd)


def hdot(a, b, ca=1, cb=0):
    return lax.dot_general(a, b, (((ca,), (cb,)), ((), ())), precision=lax.Precision.HIGH,
                           preferred_element_type=F32)


def _rms(x, g):
    return x * lax.rsqrt(jnp.mean(x * x, axis=-1, keepdims=True) + EPS) * g


def _gelu(x):
    return 0.5 * x * (1.0 + jnp.tanh(0.7978845608028654 * (x + 0.044715 * (x * x * x))))


def _log_sigmoid(z):
    return jnp.minimum(z, 0.0) - jnp.log(1.0 + jnp.exp(-jnp.abs(z)))


def _sel(mod, is_lat, idx):
    return jnp.where(is_lat, mod[1, idx:idx + 1, :], mod[0, idx:idx + 1, :])


def _rows_call(name, fn, grid, ins, outs, acc_axes=None, sem=None):
    n_in = len(ins)
    flags = [o[2] for o in outs]
    if acc_axes is None:
        acc_axes = (len(grid) - 1,)

    def body(*refs):
        ids = tuple(pl.program_id(a) for a in range(len(grid)))
        res = fn(ids, *[r[...] for r in refs[:n_in]])
        for r, v, acc in zip(refs[n_in:], res, flags):
            if acc:
                first = functools.reduce(jnp.logical_and, [ids[a] == 0 for a in acc_axes])

                @pl.when(first)
                def _():
                    r[...] = jnp.zeros_like(r)
                r[...] += v.astype(r.dtype)
            else:
                r[...] = v.astype(r.dtype)

    return pl.pallas_call(
        body, name=name, grid=grid, in_specs=[s for _, s in ins], out_specs=[o[1] for o in outs],
        out_shape=[o[0] for o in outs],
        compiler_params=_cp(sem if sem is not None else ("arbitrary",) * len(grid)),
    )(*[a for a, _ in ins])


def _sds(shape, dtype):
    return jax.ShapeDtypeStruct(shape, dtype)


def _rowspec(width, off=0, tm=TM):
    assert off % width == 0
    return pl.BlockSpec((tm, width), lambda i, o=off // width: (i, o))


def _full(shape):
    nd = len(shape)
    return pl.BlockSpec(shape, lambda *a: (0,) * nd)


def _mm(name, a, b, mode, out_dtype, tm_t=1056, tn_t=1408, tk_t=1408, chip_blocks=False, j_outer=False):
    halves = a.ndim == 3 or b.ndim == 3
    if mode == "nn":
        (m, k), (_, n) = a.shape, b.shape
    elif mode == "nt":
        (m, k), (n, _) = a.shape[-2:], b.shape
        k *= a.ndim - 1
    else:
        (k, m), (_, n) = a.shape, b.shape[-2:]
        n *= b.ndim - 1
    tm = _tile(m, tm_t, 8 if m % LANE else LANE)
    tn = _tile(n // 2 if halves and mode == "tn" else n, tn_t)
    tk = _tile(k // 2 if halves and mode == "nt" else k, tk_t)
    nk = k // tk
    if mode == "nn":
        dims, a_spec, b_spec = ((1,), (0,)), pl.BlockSpec((tm, tk), lambda i, j, l: (i, l)), pl.BlockSpec((tk, tn), lambda i, j, l: (l, j))
    elif mode == "nt":
        dims, a_spec, b_spec = ((1,), (1,)), pl.BlockSpec((tm, tk), lambda i, j, l: (i, l)), pl.BlockSpec((tn, tk), lambda i, j, l: (j, l))
        if halves:
            a_spec = pl.BlockSpec((None, tm, tk), lambda i, j, l, h=nk // 2: (l // h, i, l % h))
    else:
        dims, a_spec, b_spec = ((0,), (0,)), pl.BlockSpec((tk, tm), lambda i, j, l: (l, i)), pl.BlockSpec((tk, tn), lambda i, j, l: (l, j))
        if halves:
            b_spec = pl.BlockSpec((None, tk, tn), lambda i, j, l, h=n // tn // 2: (j // h, l, j % h))

    def body(a_ref, b_ref, o_ref, *scratch):
        l = pl.program_id(2)
        part = lax.dot_general(a_ref[...].astype(BF16), b_ref[...].astype(BF16), (dims, ((), ())),
                               preferred_element_type=F32)
        if nk == 1:
            o_ref[...] = part.astype(o_ref.dtype)
            return
        acc_ref = scratch[0]

        @pl.when(l == 0)
        def _():
            acc_ref[...] = part

        @pl.when(l > 0)
        def _():
            acc_ref[...] += part

        @pl.when(l == nk - 1)
        def _():
            o_ref[...] = acc_ref[...].astype(o_ref.dtype)

    o_spec, o_shape = pl.BlockSpec((tm, tn), lambda i, j, l: (i, j)), _sds((m, n), out_dtype)
    if chip_blocks:
        assert tn * N_CHIP == n and tm == m
        o_spec, o_shape = pl.BlockSpec((None, tm, tn), lambda i, j, l: (j, 0, 0)), _sds((N_CHIP, m, tn), out_dtype)
    grid = (m // tm, n // tn, nk)
    if j_outer:
        swap = lambda spec: pl.BlockSpec(spec.block_shape, lambda j, i, l, f=spec.index_map: f(i, j, l))
        a_spec, b_spec, o_spec, grid = swap(a_spec), swap(b_spec), swap(o_spec), (n // tn, m // tm, nk)
    return pl.pallas_call(
        body, name=name, grid=grid, in_specs=[a_spec, b_spec], out_specs=o_spec, out_shape=o_shape,
        scratch_shapes=[pltpu.VMEM((tm, tn), F32)] if nk > 1 else [],
        compiler_params=_cp(("parallel", "parallel", "arbitrary")),
    )(a, b)


def _nm_fn(is_lat, x, mod, g, shift, scale):
    return _rms(x, g) * (1.0 + _sel(mod, is_lat, scale)) + _sel(mod, is_lat, shift)


def _res_nm_fn(is_lat, x, br, modg, gate, mods, g, shift, scale):
    xn = x + _sel(modg, is_lat, gate) * br
    return xn, _nm_fn(is_lat, xn, mods, g, shift, scale)


def _nm_fwd(name, x, mod, g, shift, scale):
    t = x.shape[0]
    fn = lambda ids, xv, mv, gv: (_nm_fn(ids[0] >= NCB, xv, mv, gv, shift, scale),)
    return _rows_call(name, fn, (t // TM,), [(x, _rowspec(D)), (mod, _full((2, 6, D))), (g, _full((1, D)))],
                      [(_sds((t, D), BF16), _rowspec(D), False)])[0]


def _nm_bwd(name, x, mod, g, shift, scale, dx_res, dh):
    t = x.shape[0]

    def fn(ids, xv, mv, gv, dxr, dhv):
        _, vjp = jax.vjp(lambda a, b, c: _nm_fn(ids[0] >= NCB, a, b, c, shift, scale), xv, mv, gv)
        dx, dm, dg = vjp(dhv)
        return dx + dxr, dm, dg

    lat = pl.BlockSpec((TM, D), lambda i: (jnp.maximum(i - NCB, 0), 0))
    return _rows_call(name, fn, (t // TM,),
                      [(x, _rowspec(D)), (mod, _full((2, 6, D))), (g, _full((1, D))), (dx_res, _rowspec(D)), (dh, _rowspec(D))],
                      [(_sds((t - TC, D), F32), lat, False), (_sds((2, 6, D), F32), _full((2, 6, D)), True),
                       (_sds((1, D), F32), _full((1, D)), True)])


def _res_nm_fwd(name, x, br, modg, gate, mods, g, shift, scale):
    t = x.shape[0]
    fn = lambda ids, xv, bv, mg, ms, gv: _res_nm_fn(ids[0] >= NCB, xv, bv, mg, gate, ms, gv, shift, scale)
    return _rows_call(name, fn, (t // TM,),
                      [(x, _rowspec(D)), (br, _rowspec(D)), (modg, _full((2, 6, D))), (mods, _full((2, 6, D))), (g, _full((1, D)))],
                      [(_sds((t, D), F32), _rowspec(D), False), (_sds((t, D), BF16), _rowspec(D), False)])


def _res_nm_bwd(name, x, br, modg, gate, mods, g, shift, scale, dx_res, dh):
    t = x.shape[0]

    def fn(ids, xv, bv, mg, ms, gv, dxr, dhv):
        f = lambda a, b, c, d, e: _res_nm_fn(ids[0] >= NCB, a, b, c, gate, d, e, shift, scale)
        _, vjp = jax.vjp(f, xv, bv, mg, ms, gv)
        return vjp((dxr, dhv))

    m26 = (_sds((2, 6, D), F32), _full((2, 6, D)), True)
    return _rows_call(name, fn, (t // TM,),
                      [(x, _rowspec(D)), (br, _rowspec(D)), (modg, _full((2, 6, D))), (mods, _full((2, 6, D))), (g, _full((1, D))),
                       (dx_res, _rowspec(D)), (dh, _rowspec(D))],
                      [(_sds((t, D), F32), _rowspec(D), False), (_sds((t, D), BF16), _rowspec(D), False), m26, m26,
                       (_sds((1, D), F32), _full((1, D)), True)])


def _head(name, x_mid, f, mod, gf, tgt):
    t = x_mid.shape[0]

    def fn(ids, xv, fv, mv, gv, tv):
        def loss_fn(a, b, c, d):
            y = _rms(a + c[1, 5:6, :] * b, d)
            e = y - tv
            return 0.5 * jnp.sum(jnp.mean(e * e, axis=-1))
        loss, grads = jax.value_and_grad(loss_fn, argnums=(0, 1, 2, 3))(xv, fv, mv, gv)
        return tuple(jnp.where(ids[0] >= NCB, v, 0.0) for v in (jnp.reshape(loss, (1, 1)),) + grads)

    return _rows_call(name, fn, (t // TM,),
                      [(x_mid, _rowspec(D)), (f, _rowspec(D)), (mod, _full((2, 6, D))), (gf, _full((1, D))),
                       (tgt, pl.BlockSpec((TM, D), lambda i: (jnp.maximum(i - NCB, 0), 0)))],
                      [(_sds((1, 1), F32), _full((1, 1)), True), (_sds((t, D), F32), _rowspec(D), False),
                       (_sds((t, D), BF16), _rowspec(D), False), (_sds((2, 6, D), F32), _full((2, 6, D)), True),
                       (_sds((1, D), F32), _full((1, D)), True)])


def _gmlp_fn(u, v, g, ws, bst):
    rows = []
    u, v = u.astype(F32), v.astype(F32)
    for r in range(u.shape[0] // 128):
        uu, vv = _gelu(u[128 * r:128 * r + 128]), _gelu(v[128 * r:128 * r + 128])
        cols = []
        for gi in range(4):
            sl = slice(128 * gi, 128 * gi + 128)
            f = bdot(ws[gi], _rms(vv[:, sl], g[:, sl]), 1, 0) + bst[:, gi:gi + 1]
            cols.append(uu[:, sl] * f)
        rows.append(jnp.concatenate(cols, axis=-1))
    return jnp.concatenate(rows, axis=0)


def _gmlp_ins(p, g, ws, bst):
    return [(p, _rowspec(GW, OFF["gu"])), (p, _rowspec(GW, OFF["gv"])), (g, _full((1, GW))),
            (ws, _full((4, 128, 128))), (bst, _full((128, 4)))]


def _gmlp_fwd(name, p, g, ws, bst):
    t = p.shape[0]
    return _rows_call(name, lambda ids, *a: (_gmlp_fn(*a),), (t // TM,), _gmlp_ins(p, g, ws, bst),
                      [(_sds((t, GW), BF16), _rowspec(GW), False)])[0]


def _gmlp_bwd(name, p, g, ws, bst, dgm):
    t = p.shape[0]

    def fn(ids, u, v, gv, wv, bv, dv):
        _, vjp = jax.vjp(_gmlp_fn, u, v, gv, wv, bv)
        return vjp(dv)

    return _rows_call(name, fn, (t // TM,), _gmlp_ins(p, g, ws, bst) + [(dgm, _rowspec(GW))],
                      [(_sds((t, GW), BF16), _rowspec(GW), False), (_sds((t, GW), BF16), _rowspec(GW), False),
                       (_sds((1, GW), F32), _full((1, GW)), True), (_sds((4, 128, 128), F32), _full((4, 128, 128)), True),
                       (_sds((128, 4), F32), _full((128, 4)), True)])


def _qk_fn(q, k, gq, gk, cos, sin, seg, perm):
    cq, sq = jnp.concatenate([cos] * 4, axis=-1), jnp.concatenate([sin] * 4, axis=-1)
    q, k = q.astype(F32), k.astype(F32)
    qn = q * lax.rsqrt(hdot(q * q, seg) + EPS) * gq
    kn = k * lax.rsqrt(hdot(k * k, seg[:128, :128]) + EPS) * gk
    qr = qn * cq + hdot(qn, perm) * sq
    kr = kn * cos + hdot(kn, perm[:128, :128]) * sin
    return qr * (HD ** -0.5), kr


def _qk_ins(p, gq, gk, cos, sin, seg, perm):
    return [(p, _rowspec(512, OFF["q"])), (p, _rowspec(128, OFF["k"])), (gq, _full((1, 512))), (gk, _full((1, 128))),
            (cos, _rowspec(128)), (sin, _rowspec(128)), (seg, _full((512, 512))), (perm, _full((512, 512)))]


def _qk_fwd(name, p, gq, gk, cos, sin, seg, perm):
    t = p.shape[0]
    fn = lambda ids, q, k, a, b, c, s, sg, pm, v: _qk_fn(q, k, a, b, c, s, sg, pm) + (v,)
    return _rows_call(name, fn, (t // TM,), _qk_ins(p, gq, gk, cos, sin, seg, perm) + [(p, _rowspec(128, OFF["v"]))],
                      [(_sds((t, 512), BF16), _rowspec(512), False), (_sds((t, 128), BF16), _rowspec(128), False),
                       (_sds((t, 128), BF16), _rowspec(128), False)])


def _qk_bwd(name, p, gq, gk, cos, sin, seg, perm, dqr, dkr):
    t = p.shape[0]

    def fn(ids, q, k, a, b, c, s, sg, pm, dq, dk):
        _, vjp = jax.vjp(lambda q_, k_, a_, b_: _qk_fn(q_, k_, a_, b_, c, s, sg, pm), q, k, a, b)
        return vjp((dq, dk))

    return _rows_call(name, fn, (t // TM,),
                      _qk_ins(p, gq, gk, cos, sin, seg, perm) + [(dqr, _rowspec(512)), (dkr, _rowspec(128))],
                      [(_sds((t, 512), BF16), _rowspec(512), False), (_sds((t, 128), BF16), _rowspec(128), False),
                       (_sds((1, 512), F32), _full((1, 512)), True), (_sds((1, 128), F32), _full((1, 128)), True)])


_ATT_TQ = 1024
_ATT_TK = 768


def _attn_fwd(name, q, k, v, gather=()):
    h, tq_all, _ = q.shape
    hkv, tk_all, _ = k.shape
    tq, tk = _tile(tq_all, _ATT_TQ), _tile(tk_all, _ATT_TK)
    nkc = tk_all // tk
    ng, nq = len(gather), tq_all // tq

    def body(*refs):
        q_ref, k_ref, v_ref = refs[:3]
        o_ref, lse_ref = refs[3 + ng:5 + ng]
        if ng:
            g_id, i_id = pl.program_id(0), pl.program_id(1)
            stage = lambda st: _gather_stage(st, refs[3:3 + ng], refs[5 + ng:5 + 2 * ng], *refs[5 + 2 * ng:])
            pl.when(jnp.logical_and(g_id == 0, i_id == 0))(lambda: stage(0))
            pl.when(jnp.logical_and(g_id == hkv - 1, i_id == 0))(lambda: stage(1))
        qv = q_ref[...].reshape(QG * tq, HD)

        def step(j, carry):
            m, acc = carry
            off = pl.multiple_of(j * tk, tk)
            kk, vv = k_ref[0, pl.ds(off, tk), :], v_ref[0, pl.ds(off, tk), :]
            s = lax.dot_general(qv, kk, (((1,), (1,)), ((), ())), preferred_element_type=F32)
            m_new = jnp.maximum(m, jnp.max(s, axis=-1, keepdims=True))
            pr = jnp.exp(s - m_new)
            acc = jnp.exp(m - m_new) * acc + jnp.dot(pr.astype(BF16), vv, preferred_element_type=F32)
            return m_new, acc

        init = (jnp.full((QG * tq, 1), -jnp.inf, F32), jnp.zeros((QG * tq, 2 * HD), F32))
        m, acc = lax.fori_loop(0, nkc, step, init)
        l = acc[:, HD:HD + 1]
        o_ref[...] = (acc[:, :HD] / l).reshape(QG, tq, HD)
        lse_ref[...] = (m + jnp.log(l)).reshape(QG, tq, 1)
        if ng:
            pl.when(jnp.logical_and(g_id == hkv - 1, i_id == nq - 1))(lambda: stage(2))

    kv_spec = pl.BlockSpec((1, tk_all, HD), lambda g, i: (g, 0, 0))
    v1_spec = pl.BlockSpec((1, tk_all, 2 * HD), lambda g, i: (g, 0, 0))
    qspec = pl.BlockSpec((QG, tq, HD), lambda g, i: (g, i, 0))
    sems = [pltpu.SemaphoreType.DMA((6 * ng,)), pltpu.SemaphoreType.DMA((6 * ng,))] if ng else []
    return pl.pallas_call(
        body, name=name, grid=(hkv, nq), in_specs=[qspec, kv_spec, v1_spec] + [_ANY] * ng,
        out_specs=[qspec, pl.BlockSpec((QG, tq, 1), lambda g, i: (g, i, 0))] + [_ANY] * ng,
        out_shape=[_sds((h, tq_all, HD), F32), _sds((h, tq_all, 1), F32)]
        + [_sds((N_CHIP,) + a.shape, a.dtype) for a in gather],
        scratch_shapes=sems, compiler_params=_cp(("arbitrary", "arbitrary")),
    )(q, k, v, *gather)


def _attn_bwd(name, q, k, v, o, do, lse_row, exchange=()):
    h, tq_all, _ = q.shape
    hkv, tk_all, _ = k.shape
    tq, tk = _tile(tq_all, 1024), _tile(tk_all, 1408)
    ne, nq, nk = len(exchange), tq_all // tq, tk_all // tk

    def body(*refs):
        q_ref, k_ref, v_ref, o_ref, do_ref, lse_ref = refs[:6]
        dq_ref, dk_ref, dv_ref = refs[6 + ne:9 + ne]
        dl_ref = refs[9 + 2 * ne]
        g_id, i, j = pl.program_id(0), pl.program_id(1), pl.program_id(2)
        if ne:
            stage = lambda st: _exchange_stage(st, refs[6:6 + ne], refs[9 + ne:9 + 2 * ne], *refs[10 + 2 * ne:])
            first = functools.reduce(jnp.logical_and, [g_id == 0, i == 0, j == 0])
            pl.when(first)(lambda: stage(0))

        @pl.when(j == 0)
        def _():
            ones = jnp.ones((8, HD), F32)
            for g in range(QG):
                dl_ref[g] = hdot(ones, do_ref[g] * o_ref[g], 1, 1)

        kk, vv = k_ref[0], v_ref[0]
        dk_acc, dv_acc = jnp.zeros((tk, HD), F32), jnp.zeros((tk, HD), F32)
        for g in range(QG):
            qv, dob = q_ref[g], do_ref[g].astype(BF16)
            st = lax.dot_general(kk, qv, (((1,), (1,)), ((), ())), preferred_element_type=F32)
            pt = jnp.exp(st - lse_ref[g])
            dv_acc += jnp.dot(pt.astype(BF16), dob, preferred_element_type=F32)
            dpt = lax.dot_general(vv, dob, (((1,), (1,)), ((), ())), preferred_element_type=F32)
            dst = (pt * (dpt - dl_ref[g, 0:1, :])).astype(BF16)
            dk_acc += jnp.dot(dst, qv, preferred_element_type=F32)
            dq_part = lax.dot_general(dst, kk, (((0,), (0,)), ((), ())), preferred_element_type=F32)

            @pl.when(j == 0)
            def _():
                dq_ref[g] = dq_part

            @pl.when(j > 0)
            def _():
                dq_ref[g] += dq_part

        rows = pl.ds(pl.multiple_of(j * tk, tk), tk)

        @pl.when(i == 0)
        def _():
            dk_ref[0, rows, :] = dk_acc
            dv_ref[0, rows, :] = dv_acc

        @pl.when(i > 0)
        def _():
            dk_ref[0, rows, :] += dk_acc
            dv_ref[0, rows, :] += dv_acc

        if ne:
            last = functools.reduce(jnp.logical_and, [g_id == hkv - 1, i == nq - 1, j == nk - 1])
            pl.when(last)(lambda: stage(1))

    ks = pl.BlockSpec((1, tk, HD), lambda g, i, j: (g, j, 0))
    qs = pl.BlockSpec((QG, tq, HD), lambda g, i, j: (g, i, 0))
    rs = pl.BlockSpec((QG, 1, tq), lambda g, i, j: (g, 0, i))
    full = pl.BlockSpec((1, tk_all, HD), lambda g, i, j: (g, 0, 0))
    sems = [pltpu.SemaphoreType.DMA((3 * ne,)), pltpu.SemaphoreType.DMA((3 * ne,))] if ne else []
    return pl.pallas_call(
        body, name=name, grid=(hkv, nq, nk), in_specs=[qs, ks, ks, qs, qs, rs] + [_ANY] * ne,
        out_specs=[qs, full, full] + [_ANY] * ne,
        out_shape=[_sds((h, tq_all, HD), F32), _sds((hkv, tk_all, HD), F32), _sds((hkv, tk_all, HD), F32)]
        + [_sds(a.shape, a.dtype) for a in exchange],
        scratch_shapes=[pltpu.VMEM((QG, 8, tq), F32)] + sems,
        compiler_params=_cp(("arbitrary", "arbitrary", "arbitrary"), VMEM_LIMIT_ATTN_BWD),
    )(q, k, v, o, do, lse_row, *exchange)


def _decay_fn(a, w2, b2):
    return _log_sigmoid(bdot(a, w2, 1, 0) + b2) / GLA_TAU


def _decay_fwd(name, p, w2, b2):
    t = p.shape[0]
    return _rows_call(name, lambda ids, a, w, b: (_decay_fn(a, w, b),), (t // TM,),
                      [(p, _rowspec(128, OFF["ab"])), (w2, _full((128, 512))), (b2, _full((1, 512)))],
                      [(_sds((t, 512), F32), _rowspec(512), False)])[0]


def _decay_bwd(name, p, w2, b2, gf, gb):
    t = p.shape[0]

    def fn(ids, a, w, b, qf, kf, vf, lf, qb, kb, vb, lb):
        _, vjp = jax.vjp(_decay_fn, a, w, b)
        return vjp(jnp.concatenate([lf, lb], axis=-1)) + (qf + qb, kf + kb, vf + vb)

    widths = (256, 256, 512, 256)
    return _rows_call(name, fn, (t // TM,),
                      [(p, _rowspec(128, OFF["ab"])), (w2, _full((128, 512))), (b2, _full((1, 512)))]
                      + [(g, _rowspec(w)) for g, w in zip(gf, widths)] + [(g, _rowspec(w)) for g, w in zip(gb, widths)],
                      [(_sds((t, 128), BF16), _rowspec(128), False), (_sds((128, 512), F32), _full((128, 512)), True),
                       (_sds((1, 512), F32), _full((1, 512)), True)]
                      + [(_sds((t, w), BF16), _rowspec(w), False) for w in widths[:3]])


def _gla_consts(reverse):
    r = lax.broadcasted_iota(jnp.int32, (GLA_CHUNK, GLA_CHUNK), 0)
    c = lax.broadcasted_iota(jnp.int32, (GLA_CHUNK, GLA_CHUNK), 1)
    trib = (r <= c) if reverse else (r >= c)
    br = lax.broadcasted_iota(jnp.int32, (GLA_QK, GLA_V), 0) // GLA_DK
    bc = lax.broadcasted_iota(jnp.int32, (GLA_QK, GLA_V), 1) // GLA_DV
    lane_head = lax.broadcasted_iota(jnp.int32, (1, GLA_QK), 1) // GLA_DK
    return trib, (br == bc).astype(F32), lane_head


def _gla_chunk(q, k, v, la, s_in, consts):
    trib, bd, lane_head = consts
    q, k = q.astype(F32), k.astype(F32)
    cum = hdot(trib.astype(F32), la)
    tot = jnp.sum(la, axis=0, keepdims=True)
    q_in = q * (GLA_DK ** -0.5) * jnp.exp(cum)
    k_in = k * jnp.exp(-cum)
    k_st = k * jnp.exp(tot - cum)
    outs = []
    for h in range(GLA_H):
        att = bdot(jnp.where(lane_head == h, q_in, 0.0), k_in, 1, 1)
        att = jnp.where(trib, att, 0.0)
        outs.append(bdot(att, v[:, GLA_DV * h:GLA_DV * (h + 1)], 1, 0))
    o = jnp.concatenate(outs, axis=-1) + bdot(q_in, s_in, 1, 0)
    decay = jnp.exp(hdot(la, jnp.ones((GLA_CHUNK, LANE), F32), 0, 0))
    s_out = jnp.concatenate([decay] * (GLA_V // LANE), axis=-1) * s_in + bdot(k_st, v, 0, 0) * bd
    return o, s_out


def _gla_order(nb, reverse, backward):
    if not reverse:
        return (lambda s: nb - 1 - s) if backward else (lambda s: s)
    if backward:
        return lambda s: jnp.where(s == nb - 1, 0, s + 1)
    return lambda s: jnp.where(s == 0, 0, nb - s)


_NCH = TM // GLA_CHUNK


def _gla_specs(nb, reverse, backward):
    order = _gla_order(nb, reverse, backward)
    col = lambda width, off: pl.BlockSpec((TM, width), lambda s, o=off // width: (order(s), o))
    state = pl.BlockSpec((_NCH, GLA_H, GLA_DK, GLA_DV), lambda s: (order(s), 0, 0, 0))
    qkvla = [col(256, OFF["glq"]), col(256, OFF["glk"]), col(512, OFF["glv"]), col(256, 256 * int(reverse))]
    return col, state, qkvla


def _gla_fwd(name, p, la):
    t = p.shape[0]
    nb = t // TM

    def body(*refs):
        ins, outs, scr = (refs[0:4], refs[4:8]), (refs[8:10], refs[10:12]), refs[12:14]

        @pl.when(pl.program_id(0) == 0)
        def _():
            for s_ref in scr:
                s_ref[...] = jnp.zeros_like(s_ref)

        for step in range(_NCH):
            for d in range(2):
                (q_ref, k_ref, v_ref, la_ref), (o_ref, sv_ref), s_ref = ins[d], outs[d], scr[d]
                c = _NCH - 1 - step if d else step
                rows = slice(GLA_CHUNK * c, GLA_CHUNK * (c + 1))
                s_in = s_ref[...]
                for h in range(GLA_H):
                    sv_ref[c, h] = s_in[GLA_DK * h:GLA_DK * (h + 1), GLA_DV * h:GLA_DV * (h + 1)]
                o, s_out = _gla_chunk(q_ref[rows, :], k_ref[rows, :], v_ref[rows, :], la_ref[rows, :], s_in,
                                      _gla_consts(bool(d)))
                o_ref[rows, :] = o
                s_ref[...] = s_out

    in_specs, out_specs, out_shape = [], [], []
    for d in range(2):
        col, state, qkvla = _gla_specs(nb, bool(d), False)
        in_specs += qkvla
        out_specs += [col(512, 0), state]
        out_shape += [_sds((t, GLA_V), F32), _sds((t // GLA_CHUNK, GLA_H, GLA_DK, GLA_DV), F32)]
    return pl.pallas_call(
        body, name=name, grid=(nb,), in_specs=in_specs, out_specs=out_specs, out_shape=out_shape,
        scratch_shapes=[pltpu.VMEM((GLA_QK, GLA_V), F32)] * 2, compiler_params=_cp(("arbitrary",)),
    )(p, p, p, la, p, p, p, la)


def _gla_bwd(name, p, la, sv_f, sv_b, do):
    t = p.shape[0]
    nb = t // TM

    def body(*refs):
        ins, outs, scr = (refs[0:6], refs[6:12]), (refs[12:16], refs[16:20]), refs[20:22]

        @pl.when(pl.program_id(0) == 0)
        def _():
            for ds_ref in scr:
                ds_ref[...] = jnp.zeros_like(ds_ref)

        zero = jnp.zeros((GLA_DK, GLA_DV), F32)
        for step in range(_NCH):
            for d in range(2):
                (q_ref, k_ref, v_ref, la_ref, sv_ref, do_ref), out_refs, ds_ref = ins[d], outs[d], scr[d]
                c = step if d else _NCH - 1 - step
                rows = slice(GLA_CHUNK * c, GLA_CHUNK * (c + 1))
                s_in = jnp.concatenate(
                    [jnp.concatenate([sv_ref[c, h] if hh == h else zero for hh in range(GLA_H)], axis=-1)
                     for h in range(GLA_H)], axis=0)
                consts = _gla_consts(bool(d))
                _, vjp = jax.vjp(lambda a, b, cc, dd, e: _gla_chunk(a, b, cc, dd, e, consts),
                                 q_ref[rows, :], k_ref[rows, :], v_ref[rows, :], la_ref[rows, :], s_in)
                grads = vjp((do_ref[rows, :], ds_ref[...]))
                for o_ref, g in zip(out_refs, grads[:4]):
                    o_ref[rows, :] = g.astype(o_ref.dtype)
                ds_ref[...] = grads[4]

    ins, in_specs, out_specs, out_shape = [], [], [], []
    for d, sv in enumerate((sv_f, sv_b)):
        col, state, qkvla = _gla_specs(nb, bool(d), True)
        ins += [p, p, p, la, sv, do]
        in_specs += qkvla + [state, col(512, 0)]
        out_specs += [col(256, 0), col(256, 0), col(512, 0), col(256, 0)]
        out_shape += [_sds((t, GLA_QK), F32), _sds((t, GLA_QK), F32), _sds((t, GLA_V), F32), _sds((t, GLA_QK), F32)]
    return pl.pallas_call(
        body, name=name, grid=(nb,), in_specs=in_specs, out_specs=out_specs, out_shape=out_shape,
        scratch_shapes=[pltpu.VMEM((GLA_QK, GLA_V), F32)] * 2, compiler_params=_cp(("arbitrary",)),
    )(*ins)


def _gla_out_fn(of, ob, r, g):
    o = of + ob
    cols = [_rms(o[:, GLA_DV * h:GLA_DV * (h + 1)], g[:, GLA_DV * h:GLA_DV * (h + 1)]) for h in range(GLA_H)]
    return jnp.concatenate(cols, axis=-1) * jax.nn.silu(r.astype(F32))


def _gla_out_fwd(name, of, ob, p, g):
    t = p.shape[0]
    return _rows_call(name, lambda ids, *a: (_gla_out_fn(*a),), (t // TM,),
                      [(of, _rowspec(512)), (ob, _rowspec(512)), (p, _rowspec(512, OFF["gr"])), (g, _full((1, 512)))],
                      [(_sds((t, 512), BF16), _rowspec(512), False)])[0]


def _gla_out_bwd(name, of, ob, p, g, dgla):
    t = p.shape[0]

    def fn(ids, a, b, r, gv, dv):
        _, vjp = jax.vjp(_gla_out_fn, a, b, r, gv)
        do, _, dr, dg = vjp(dv)
        return do, dr, dg

    return _rows_call(name, fn, (t // TM,),
                      [(of, _rowspec(512)), (ob, _rowspec(512)), (p, _rowspec(512, OFF["gr"])), (g, _full((1, 512))),
                       (dgla, _rowspec(512))],
                      [(_sds((t, 512), F32), _rowspec(512), False), (_sds((t, 512), BF16), _rowspec(512), False),
                       (_sds((1, 512), F32), _full((1, 512)), True)])


_TMM = 384


def _merge_fwd(name, gm, att, gla, wa, wb, wc, p):
    t = p.shape[0]
    row = lambda w, off=0: pl.BlockSpec((_TMM, w), lambda i, o=off // w: (i, o))

    def fn(ids, a, b, c, wa_, wb_, wc_, ga, gb, gc):
        ga, gb, gc = ga.astype(F32), gb.astype(F32), gc.astype(F32)
        return (jax.nn.sigmoid(ga) * bdot(a, wa_, 1, 0) + jax.nn.sigmoid(gb) * bdot(b, wb_, 1, 0)
                + jax.nn.sigmoid(gc) * bdot(c, wc_, 1, 0),)

    return _rows_call(name, fn, (t // _TMM,),
                      [(gm, row(512)), (att, row(512)), (gla, row(512)), (wa, _full((512, D))), (wb, _full((512, D))),
                       (wc, _full((512, D))), (p, row(D, OFF["gA"])), (p, row(D, OFF["gB"])), (p, row(D, OFF["gC"]))],
                      [(_sds((t, D), BF16), row(D), False)])[0]


def _merge_bwd(name, gm, att, gla, wa, wb, wc, p, dmerged):
    t = p.shape[0]
    row = lambda w, off=0: pl.BlockSpec((_TMM, w), lambda i, o=off // w: (i, o))

    def fn(ids, a, b, c, wa_, wb_, wc_, ga, gb, gc, dm):
        ga, gb, gc = ga.astype(F32), gb.astype(F32), gc.astype(F32)
        outs_y, outs_g = [], []
        for br, w, g in ((a, wa_, ga), (b, wb_, gb), (c, wc_, gc)):
            s = jax.nn.sigmoid(g)
            outs_y.append(dm * s)
            outs_g.append(dm * bdot(br, w, 1, 0) * s * (1.0 - s))
        return tuple(outs_y) + tuple(outs_g)

    o = (_sds((t, D), BF16), row(D), False)
    return _rows_call(name, fn, (t // _TMM,),
                      [(gm, row(512)), (att, row(512)), (gla, row(512)), (wa, _full((512, D))), (wb, _full((512, D))),
                       (wc, _full((512, D))), (p, row(D, OFF["gA"])), (p, row(D, OFF["gB"])), (p, row(D, OFF["gC"])),
                       (dmerged, row(D))], [o] * 6)


_TNC = 1408
_NJ = FFN // _TNC


HALO = 16


def _shift_rows(x, prev, nxt, vp, vn):
    n = x.shape[0]
    rid = lax.broadcasted_iota(jnp.int32, x.shape, 0)
    xp = jnp.where(rid == 0, jnp.where(vp, prev[HALO - 1:HALO, :], 0.0), pltpu.roll(x, 1, 0))
    xn = jnp.where(rid == n - 1, jnp.where(vn, nxt[0:1, :], 0.0), pltpu.roll(x, n - 1, 0))
    return xp, xn


def _seq_edges(i, t):
    start, end = i * TM, (i + 1) * TM
    return jnp.logical_and(start != 0, start != TC), jnp.logical_and(end != TC, end != t)


def _halo_specs(t, colmap):
    r = TM // HALO
    main = pl.BlockSpec((TM, _TNC), lambda j, i: (i, colmap(j)))
    prev = pl.BlockSpec((HALO, _TNC), lambda j, i: (jnp.maximum(i * r - 1, 0), colmap(j)))
    nxt = pl.BlockSpec((HALO, _TNC), lambda j, i: (jnp.minimum((i + 1) * r, t // HALO - 1), colmap(j)))
    return [main, prev, nxt]


def _conv3(x, xp, xn, w, b=None):
    y = xp * w[0:1, :] + x * w[1:2, :] + xn * w[2:3, :]
    return y if b is None else b + y


def _conv_fwd(name, a, cw, cb):
    t = a.shape[0]

    def fn(ids, ag, agp, agn, av, avp, avn, wg, wv, bg, bv):
        vp, vn = _seq_edges(ids[1], t)
        ag, agp, agn, av, avp, avn = (z.astype(F32) for z in (ag, agp, agn, av, avp, avn))
        cg = _conv3(ag, *_shift_rows(ag, agp, agn, vp, vn), wg, bg)
        cv = _conv3(av, *_shift_rows(av, avp, avn, vp, vn), wv, bv)
        return (jax.nn.silu(cg) * cv,)

    gcol, vcol = (lambda j: j), (lambda j: j + _NJ)
    wspec = lambda cm: pl.BlockSpec((3, _TNC), lambda j, i: (0, cm(j)))
    bspec = lambda cm: pl.BlockSpec((1, _TNC), lambda j, i: (0, cm(j)))
    ins = [(a, s) for s in _halo_specs(t, gcol) + _halo_specs(t, vcol)]
    ins += [(cw, wspec(gcol)), (cw, wspec(vcol)), (cb, bspec(gcol)), (cb, bspec(vcol))]
    return _rows_call(name, fn, (_NJ, t // TM), ins,
                      [(_sds((t, FFN), BF16), pl.BlockSpec((TM, _TNC), lambda j, i: (i, j)), False)])[0]


def _conv_bwd(name, a, cw, cb, dact):
    t = a.shape[0]
    n = TM + 2 * HALO

    def fn(ids, ag, agp, agn, av, avp, avn, dv, dvp, dvn, wg, wv, bg, bv):
        vp, vn = _seq_edges(ids[1], t)
        ag, agp, agn, av, avp, avn = (z.astype(F32) for z in (ag, agp, agn, av, avp, avn))
        ext = lambda x, xp, xn: jnp.concatenate([jnp.where(vp, xp, 0.0), x, jnp.where(vn, xn, 0.0)], axis=0)
        up, dn = (lambda x: pltpu.roll(x, 1, 0)), (lambda x: pltpu.roll(x, n - 1, 0))
        main = lambda y: y[HALO:HALO + TM]
        eg, ev, ed = ext(ag, agp, agn), ext(av, avp, avn), ext(dv, dvp, dvn)
        cg = _conv3(eg, up(eg), dn(eg), wg, bg)
        cv = _conv3(ev, up(ev), dn(ev), wv, bv)
        s = jax.nn.sigmoid(cg)
        rid = lax.broadcasted_iota(jnp.int32, (3, eg.shape[1]), 0)
        das, dws, dbs = [], [], []
        for dc, w, x in ((ed * cv * s * (1.0 + cg * (1.0 - s)), wg, eg), (ed * cg * s, wv, ev)):
            shifted = [main(dn(dc)), main(dc), main(up(dc))]
            das.append(shifted[0] * w[0:1, :] + shifted[1] * w[1:2, :] + shifted[2] * w[2:3, :])
            sums = [jnp.sum(y * main(x), axis=0, keepdims=True) for y in shifted]
            dws.append(jnp.where(rid == 0, sums[0], jnp.where(rid == 1, sums[1], sums[2])))
            dbs.append(jnp.sum(shifted[1], axis=0, keepdims=True))
        return jnp.stack(das), jnp.stack(dws), jnp.stack(dbs)

    gcol, vcol = (lambda j: j), (lambda j: j + _NJ)
    wspec = lambda cm: pl.BlockSpec((3, _TNC), lambda j, i: (0, cm(j)))
    bspec = lambda cm: pl.BlockSpec((1, _TNC), lambda j, i: (0, cm(j)))
    ins = [(a, s) for s in _halo_specs(t, gcol) + _halo_specs(t, vcol)] + [(dact, s) for s in _halo_specs(t, gcol)]
    ins += [(cw, wspec(gcol)), (cw, wspec(vcol)), (cb, bspec(gcol)), (cb, bspec(vcol))]
    return _rows_call(name, fn, (_NJ, t // TM), ins,
                      [(_sds((2, t, FFN), BF16), pl.BlockSpec((2, TM, _TNC), lambda j, i: (0, i, j)), False),
                       (_sds((2, 3, FFN), F32), pl.BlockSpec((2, 3, _TNC), lambda j, i: (0, 0, j)), True),
                       (_sds((2, 1, FFN), F32), pl.BlockSpec((2, 1, _TNC), lambda j, i: (0, 0, j)), True)])


_TNA = 512


def _adaln_fwd(name, cond, w, b):
    fn = lambda ids, cv, wv, bv: ((bdot(jax.nn.silu(cv), wv[0], 1, 0) + bv[0])[None],)
    return _rows_call(name, fn, (2, ADA_LOC // _TNA),
                      [(cond, _full((16, D))), (w, pl.BlockSpec((1, D, _TNA), lambda l, j: (l, 0, j))),
                       (b, pl.BlockSpec((1, 1, _TNA), lambda l, j: (l, 0, j)))],
                      [(_sds((2, 16, ADA_LOC), F32), pl.BlockSpec((1, 16, _TNA), lambda l, j: (l, 0, j)), False)])[0]


def _adaln_bwd(name, c8, cc8, w, dl, dc):
    def fn(ids, cv, ccv, wv, dlv, dcv):
        dcs = jnp.broadcast_to(jnp.sum(dcv[0], axis=0, keepdims=True), dcv[0].shape)
        dw = hdot(jax.nn.silu(cv), dlv[0], 0, 0) + hdot(jax.nn.silu(ccv), dcs, 0, 0)
        s = jax.nn.sigmoid(ccv)
        rid = lax.broadcasted_iota(jnp.int32, ccv.shape, 0)
        dcc = jnp.where(rid == 0, bdot(dcs, wv[0], 1, 1) * s * (1.0 + ccv * (1.0 - s)), 0.0)
        return dw[None], dcc

    dspec = pl.BlockSpec((1, 8, _TNA), lambda l, j: (l, 0, j))
    return _rows_call(name, fn, (2, ADA_LOC // _TNA),
                      [(c8, _full((8, D))), (cc8, _full((8, D))), (w, pl.BlockSpec((1, D, _TNA), lambda l, j: (l, 0, j))),
                       (dl, dspec), (dc, dspec)],
                      [(_sds((2, D, ADA_LOC), F32), pl.BlockSpec((1, D, _TNA), lambda l, j: (l, 0, j)), False),
                       (_sds((8, D), F32), _full((8, D)), True)], acc_axes=(0, 1))


def _adamw_fn(w, g, m, v):
    m = ADAM_B1 * m + (1.0 - ADAM_B1) * g
    v = ADAM_B2 * v + (1.0 - ADAM_B2) * (g * g)
    m_hat = m / (1.0 - ADAM_B1 ** ADAM_STEP)
    v_hat = v / (1.0 - ADAM_B2 ** ADAM_STEP)
    return -ADAM_LR * (m_hat / (jnp.sqrt(v_hat) + ADAM_EPS) + ADAM_WD * w), m, v


def _adamw(name, w, g, m, v):
    l, r, c = w.shape
    tr = _tile(r, max(8, (1 << 20) // (4 * c)), 8)
    spec = pl.BlockSpec((None, tr, c), lambda i, j: (i, j, 0))
    o = (_sds((l, r, c), F32), spec, False)
    return _rows_call(name, lambda ids, *a: _adamw_fn(*a), (l, r // tr), [(x, spec) for x in (w, g, m, v)], [o, o, o],
                      sem=("parallel", "parallel"))


def _coords():
    return lax.axis_index("x"), lax.axis_index("y"), lax.axis_index("c")


def _other_chips(x, y):
    return [(1 - x, y), (x, 1 - y), (1 - x, 1 - y)]


def _allgather_small(name, blk):
    m_per, n = blk.shape

    def body(x_ref, out_ref, send_sems, recv_sems, local_sem):
        x, y, c = _coords()
        me, sibling = (x, y, c), (x, y, 1 - c)
        chips = _other_chips(x, y)

        def rows(px, py, pc):
            return out_ref.at[pl.ds((4 * px + 2 * py + pc) * m_per, m_per), :]

        def copy(k, block, to, src=None):
            return pltpu.make_async_remote_copy(
                src_ref=rows(*block) if src is None else src, dst_ref=rows(*block), send_sem=send_sems.at[k],
                recv_sem=recv_sems.at[k], device_id=to, device_id_type=MESH)

        mine = pltpu.make_async_copy(x_ref, rows(*me), local_sem)
        mine.start()
        first = [copy(0, me, sibling, src=x_ref)]
        first += [copy(1 + j, me, (*chip, c), src=x_ref) for j, chip in enumerate(chips)]
        for cp in first:
            cp.start()
        passed = [copy(4 + j, (*chip, c), sibling) for j, chip in enumerate(chips)]
        for j, chip in enumerate(chips):
            copy(1 + j, (*chip, c), me).wait_recv()
            passed[j].start()
        copy(0, sibling, me).wait_recv()
        for j, chip in enumerate(chips):
            copy(4 + j, (*chip, 1 - c), me).wait_recv()
        for cp in first + passed:
            cp.wait_send()
        mine.wait()

    return pl.pallas_call(
        body, name=name, out_shape=_sds((N_DEV * m_per, n), blk.dtype),
        in_specs=[pl.BlockSpec(memory_space=pltpu.VMEM)], out_specs=pl.BlockSpec(memory_space=pltpu.VMEM),
        scratch_shapes=[pltpu.SemaphoreType.DMA((7,)), pltpu.SemaphoreType.DMA((7,)), pltpu.SemaphoreType.DMA],
        compiler_params=pltpu.CompilerParams(vmem_limit_bytes=VMEM_LIMIT),
    )(blk)


_ANY = pl.BlockSpec(memory_space=pl.ANY)


def _remote(src, dst, send_sems, recv_sems, s, to):
    return pltpu.make_async_remote_copy(src_ref=src, dst_ref=dst, send_sem=send_sems.at[s], recv_sem=recv_sems.at[s],
                                        device_id=to, device_id_type=MESH)


def _comm_call(name, body, ins, out_shapes, n_sems, n_local):
    return pl.pallas_call(
        body, name=name, out_shape=out_shapes, in_specs=[_ANY] * len(ins), out_specs=[_ANY] * len(out_shapes),
        scratch_shapes=[pltpu.SemaphoreType.DMA((n_sems,)), pltpu.SemaphoreType.DMA((n_sems,)),
                        pltpu.SemaphoreType.DMA((n_local,))],
    )(*ins)


def _gather_stage(stage, ins, outs, send_sems, recv_sems):
    n = len(ins)
    x, y, c = _coords()
    k = 2 * x + y
    sibling = (x, y, 1 - c)
    chips = _other_chips(x, y)
    first = [_remote(ins[t].at[c], outs[t].at[k, c], send_sems, recv_sems, 6 * t + j, (*chip, c))
             for t in range(n) for j, chip in enumerate(chips)]
    there = lambda t, j, half: outs[t].at[2 * chips[j][0] + chips[j][1], half]
    passed = [_remote(there(t, j, c), there(t, j, c), send_sems, recv_sems, 6 * t + 3 + j, sibling)
              for t in range(n) for j in range(3)]
    if stage == 0:
        for cp in first:
            cp.start()
    elif stage == 1:
        for t in range(n):
            for j in range(3):
                _remote(there(t, j, c), there(t, j, c), send_sems, recv_sems, 6 * t + j, sibling).wait_recv()
                passed[3 * t + j].start()
    else:
        for t in range(n):
            for j in range(3):
                _remote(there(t, j, 1 - c), there(t, j, 1 - c), send_sems, recv_sems, 6 * t + 3 + j, sibling).wait_recv()
        for cp in first + passed:
            cp.wait_send()


def _gather_own(outs, locs):
    k = 2 * lax.axis_index("x") + lax.axis_index("y")
    return [lax.dynamic_update_slice_in_dim(o, a[None], k, axis=0) for o, a in zip(outs, locs)]


def _allgather_layers(name, locs):
    n = len(locs)

    def body(*refs):
        ins, outs, (send_sems, recv_sems, _) = refs[:n], refs[n:2 * n], refs[2 * n:]
        for stage in range(3):
            _gather_stage(stage, ins, outs, send_sems, recv_sems)

    return _gather_own(_comm_call(name, body, locs, [_sds((N_CHIP,) + a.shape, a.dtype) for a in locs], 6 * n, 1), locs)


def _rs_pair_exchange(name, gs):
    n = len(gs)

    def body(*refs):
        ins, outs, (send_sems, recv_sems, _) = refs[:n], refs[n:2 * n], refs[2 * n:]
        x, y, c = _coords()
        cps = [_remote(ins[t].at[kk, 1 - c], outs[t].at[kk], send_sems, recv_sems, N_CHIP * t + kk, (x, y, 1 - c))
               for t in range(n) for kk in range(N_CHIP)]
        for cp in cps:
            cp.start()
        for cp in cps:
            cp.wait()

    return _comm_call(name, body, gs, [_sds((N_CHIP,) + a.shape[2:], a.dtype) for a in gs], N_CHIP * n, 1)


def _rs_pair_add(name, g, got):
    _, _, r, c = g.shape
    tr = _tile(r, max(16, (1 << 20) // (4 * c)), 16)
    core = jnp.reshape(lax.axis_index("c"), (1,)).astype(jnp.int32)

    def body(core_ref, a_ref, b_ref, o_ref):
        o_ref[...] = (a_ref[...] + b_ref[...]).astype(o_ref.dtype)

    spec = pl.BlockSpec((None, tr, c), lambda kk, i, cr: (kk, i, 0))
    return pl.pallas_call(
        body, name=name, out_shape=_sds((N_CHIP, r, c), BF16),
        grid_spec=pltpu.PrefetchScalarGridSpec(
            num_scalar_prefetch=1, grid=(N_CHIP, r // tr),
            in_specs=[pl.BlockSpec((None, None, tr, c), lambda kk, i, cr: (kk, cr[0], i, 0)), spec], out_specs=spec),
        compiler_params=_cp(("parallel", "parallel")),
    )(core, g, got)


def _exchange_stage(stage, ins, outs, send_sems, recv_sems):
    n = len(ins)
    x, y, c = _coords()
    k = 2 * x + y
    chips = _other_chips(x, y)
    cps = [_remote(ins[t].at[2 * cx + cy], outs[t].at[k], send_sems, recv_sems, 3 * t + j, (cx, cy, c))
           for t in range(n) for j, (cx, cy) in enumerate(chips)]
    if stage == 0:
        for cp in cps:
            cp.start()
    else:
        for t in range(n):
            for j, (cx, cy) in enumerate(chips):
                there = outs[t].at[2 * cx + cy]
                _remote(there, there, send_sems, recv_sems, 3 * t + j, (cx, cy, c)).wait_recv()
        for cp in cps:
            cp.wait_send()


def _exchange_own(outs, s1):
    k = 2 * lax.axis_index("x") + lax.axis_index("y")
    own = [lax.dynamic_index_in_dim(a, k, axis=0, keepdims=True) for a in s1]
    return [lax.dynamic_update_slice_in_dim(o, a, k, axis=0) for o, a in zip(outs, own)]


def _rs_chip_exchange(name, s1):
    n = len(s1)

    def body(*refs):
        ins, outs, (send_sems, recv_sems, _) = refs[:n], refs[n:2 * n], refs[2 * n:]
        for stage in range(2):
            _exchange_stage(stage, ins, outs, send_sems, recv_sems)

    return _exchange_own(_comm_call(name, body, s1, [_sds(a.shape, a.dtype) for a in s1], 3 * n, 1), s1)


def _sum_slots(name, a):
    s, r, cdim = a.shape
    tr = _tile(r, 512, 8)

    def fn(ids, av):
        tot = av[0]
        for i in range(1, s):
            tot = tot + av[i]
        return (tot,)

    return _rows_call(name, fn, (r // tr,), [(a, pl.BlockSpec((s, tr, cdim), lambda i: (0, i, 0)))],
                      [(_sds((r, cdim), F32), pl.BlockSpec((tr, cdim), lambda i: (i, 0)), False)], sem=("parallel",))[0]


def _pair_allgather(name, red0, red1):
    n = len(red0)

    def body(*refs):
        ins, outs, (send_sems, recv_sems, _) = (refs[:n], refs[n:2 * n]), refs[2 * n:3 * n], refs[3 * n:]
        x, y, c = _coords()
        cps = [_remote(ins[l][t], outs[t].at[l, c], send_sems, recv_sems, 2 * t + l, (x, y, 1 - c))
               for t in range(n) for l in range(2)]
        for cp in cps:
            cp.start()
        for t in range(n):
            for l in range(2):
                _remote(ins[l][t], outs[t].at[l, 1 - c], send_sems, recv_sems, 2 * t + l, (x, y, 1 - c)).wait_recv()
        for cp in cps:
            cp.wait_send()

    outs = _comm_call(name, body, list(red0) + list(red1), [_sds((2, 2) + a.shape, a.dtype) for a in red0], 2 * n, 1)
    c = lax.axis_index("c")
    own = [jnp.stack([a, b])[:, None] for a, b in zip(red0, red1)]
    return [lax.dynamic_update_slice_in_dim(o, a, c, axis=1) for o, a in zip(outs, own)]


def _rs_front(tag, gs):
    halves = [g.reshape(N_CHIP, 2, g.shape[1] // 2, g.shape[2]) for g in gs]
    got = _rs_pair_exchange(tag + "rs_pair_exchange", halves)
    return [_rs_pair_add(tag + "rs_pair_add_%d" % t, h, r) for t, (h, r) in enumerate(zip(halves, got))]


def _rs_sum(tag, slots):
    return [_sum_slots(tag + "rs_chip_sum_%d" % t, a) for t, a in enumerate(slots)]


PACK_C = 1024
_SHARDED = (("w_in", 1), ("w_br_a", 1), ("w_br_b", 1), ("w_br_c", 1), ("w_out", 0), ("w_ffn_up", 1), ("w_ffn_down", 0))
_SHARDED_SMALL = (("conv_w", (3, 2 * FFN), 1), ("w_alpha2", (2, 16, GLA_QK), 2), ("b_alpha", (2, GLA_QK), 1))


def _prod(shape):
    n = 1
    for s in shape:
        n *= s
    return n


def _to_blocks(full, axis):
    shp = full.shape
    split = full.reshape(shp[:axis] + (N_CHIP, shp[axis] // N_CHIP) + shp[axis + 1:])
    return jnp.moveaxis(split, axis, 0)


def _from_blocks(blocks, axis):
    return jnp.concatenate([blocks[k] for k in range(N_CHIP)], axis=axis)


def _rope_tables(tx):
    pos = jnp.arange(tx, dtype=jnp.int32)
    inv_freq = 10000.0 ** (-jnp.arange(16, dtype=F32) / 16)
    ang_r = (pos // GRID_W).astype(F32)[:, None] * inv_freq
    ang_c = (pos % GRID_W).astype(F32)[:, None] * inv_freq
    ang = jnp.concatenate([ang_r, ang_r, ang_c, ang_c], axis=-1)
    sign = jnp.concatenate([-jnp.ones((16,), F32), jnp.ones((16,), F32)] * 2)
    cos = jnp.concatenate([jnp.ones((TC, HD), F32), jnp.cos(ang)], axis=0)
    sin = jnp.concatenate([jnp.zeros((TC, HD), F32), jnp.sin(ang) * sign], axis=0)
    return jnp.tile(cos, (1, 2)), jnp.tile(sin, (1, 2))


def _lane_consts():
    l = jnp.arange(512)
    seg = (l[:, None] // HD == l[None, :] // HD).astype(F32) / HD
    partner = jnp.where(l % 32 < 16, l + 16, l - 16)
    perm = (l[:, None] == partner[None, :]).astype(F32)
    return seg, perm


def _heads(a, n):
    return a.reshape(a.shape[0], n, HD).transpose(1, 0, 2)


def _unheads(a):
    return a.transpose(1, 0, 2).reshape(a.shape[1], a.shape[0] * HD)


def _gather_f32_shards(shards):
    sizes = [_prod(a.shape) for a in shards]
    flat = jnp.concatenate([a.reshape(-1) for a in shards] + [jnp.zeros((16 * PACK_C - sum(sizes),), F32)])
    got = _allgather_small("gather_f32_shards", flat.reshape(16, PACK_C)).reshape(N_CHIP, 2, 16 * PACK_C)[:, 0]
    out, o = {}, 0
    for (n, _, ax), a, sz in zip(_SHARDED_SMALL, shards, sizes):
        out[n] = jnp.concatenate([got[k, o:o + sz].reshape(a.shape) for k in range(N_CHIP)], axis=ax + 1)
        o += sz
    return out


def _halves(a):
    return a.reshape(2, a.shape[0] // 2, a.shape[1])


def _layer_shards(W, l):
    return [_halves(W[n][l].astype(BF16)) for n, _ in _SHARDED]


def _layer_params(l, gathered, small):
    w2 = small["w_alpha2_full"][l]
    w2pad = jnp.zeros((128, 512), F32).at[0:16, 0:256].set(w2[0]).at[16:32, 256:512].set(w2[1])
    full = {n: _from_blocks(g.reshape(N_CHIP, 2 * g.shape[2], g.shape[3]), ax)
            for (n, ax), g in zip(_SHARDED, gathered) if g is not None}
    keys = dict(w_br_a="wa", w_br_b="wb", w_br_c="wc", w_out="w_out", w_ffn_up="w_up", w_ffn_down="w_down")
    weights = {keys[n]: a for n, a in full.items() if n in keys}
    if "w_in" in full:
        weights["w_in"] = _to_new_cols(full["w_in"])
    return dict(
        weights,
        cw=small["conv_w_full"][l], cb=small["conv_b"][l][None], w2=w2pad,
        b2=small["b_alpha_full"][l].reshape(1, 512),
        g1=small["norm1_g"][l][None], g2=small["norm2_g"][l][None], gq=jnp.tile(small["q_norm_g"][l], 8)[None],
        gk=jnp.tile(small["k_norm_g"][l], 2)[None], ggm=small["gmlp_norm_g"][l][None], ws=small["w_spatial"][l],
        bst=small["b_spatial"][l].T, ggl=small["gla_norm_g"][l][None])


def _layer_fwd(l, last, x, h1, mod, P, tabs, gather=(), late=None):
    cos, sin, seg, perm = tabs
    n = "l%d_" % l
    s = dict(x=x, h1=h1)
    p = _mm(n + "in_proj", h1, P["w_in"], "nn", BF16, tm_t=768, tn_t=2176, j_outer=True)
    s["p"] = p
    s["gm"] = _gmlp_fwd(n + "gmlp", p, P["ggm"], P["ws"], P["bst"])
    qr, kr, vb = _qk_fwd(n + "qk_prep", p, P["gq"], P["gk"], cos, sin, seg, perm)
    qx, qc, kh, vh = _heads(qr[TC:], NQ), _heads(qr[:TC], NQ), _heads(kr, NKV), _heads(vb, NKV)
    s["qx"], s["qc"], s["kh"], s["vh"] = qx, qc, kh, vh
    one_hot = (jnp.arange(HD) == 0).astype(BF16)
    v1 = jnp.concatenate([vh, jnp.broadcast_to(one_hot, vh.shape)], axis=-1)
    ox, lse_x, *s["gathered"] = _attn_fwd(n + "attn_x", qx, kh, v1, gather)
    s["ox"], s["lse_x"] = ox, lse_x
    if late is not None:
        P = late(s["gathered"])
    s["P"] = P
    if last:
        oc = jnp.zeros((NQ, TC, HD), F32)
    else:
        oc, lse_c = _attn_fwd(n + "attn_c", qc, kh[:, :TC], v1[:, :TC])
        s["oc"], s["lse_c"] = oc, lse_c
    s["att"] = jnp.concatenate([_unheads(oc), _unheads(ox)], axis=0).astype(BF16)
    la = _decay_fwd(n + "gla_decay", p, P["w2"], P["b2"])
    s["la"] = la
    s["of"], s["sf"], s["ob"], s["sb"] = _gla_fwd(n + "gla_scan", p, la)
    s["gla"] = _gla_out_fwd(n + "gla_out", s["of"], s["ob"], p, P["ggl"])
    s["merged"] = _merge_fwd(n + "merge", s["gm"], s["att"], s["gla"], P["wa"], P["wb"], P["wc"], p)
    s["mix"] = _mm(n + "out_proj", s["merged"], P["w_out"], "nn", F32)
    s["x_mid"], s["h2"] = _res_nm_fwd(n + "res1_norm2", x, s["mix"], mod, 2, mod, P["g2"], 3, 4)
    s["a"] = _mm(n + "ffn_up", s["h2"], P["w_up"], "nn", BF16, j_outer=True)
    s["act"] = _conv_fwd(n + "conv_gate", s["a"], P["cw"], P["cb"])
    s["f"] = _mm(n + "ffn_down", s["act"], P["w_down"], "nn", F32)
    return s


def _layer_bwd(l, last, s, mod, P, tabs, dx_mid, df, gw, exchange=()):
    cos, sin, seg, perm = tabs
    n = "l%d_b_" % l
    t = dx_mid.shape[0]
    p = s["p"]
    gw["w_ffn_down"] = _mm(n + "ffn_down_w", s["act"], df, "tn", F32, tm_t=1408)
    dact = _mm(n + "ffn_down_x", df, P["w_down"], "nt", F32)
    da, dcw, dcb = _conv_bwd(n + "conv_gate", s["a"], P["cw"], P["cb"], dact)
    gw["conv_w"], gw["conv_b"] = dcw.transpose(1, 0, 2).reshape(3, 2 * FFN), dcb.reshape(2 * FFN)
    gw["w_ffn_up"] = _mm(n + "ffn_up_w", s["h2"], da, "tn", F32, chip_blocks=True)
    dh2 = _mm(n + "ffn_up_x", da, P["w_up"], "nt", F32)
    dx, dmix, dmod_a, dmod_b, dg2 = _res_nm_bwd(n + "res1_norm2", s["x"], s["mix"], mod, 2, mod, P["g2"], 3, 4, dx_mid, dh2)
    dmod = dmod_a + dmod_b
    gw["norm2_g"] = dg2[0]
    gw["w_out"] = _mm(n + "out_proj_w", s["merged"], dmix, "tn", F32)
    dmerged = _mm(n + "out_proj_x", dmix, P["w_out"], "nt", F32)
    dya, dyb, dyc, dga, dgb, dgc = _merge_bwd(n + "merge", s["gm"], s["att"], s["gla"], P["wa"], P["wb"], P["wc"], p, dmerged)
    gw["w_br_a"] = _mm(n + "br_a_w", s["gm"], dya, "tn", F32)
    gw["w_br_b"] = _mm(n + "br_b_w", s["att"], dyb, "tn", F32)
    gw["w_br_c"] = _mm(n + "br_c_w", s["gla"], dyc, "tn", F32)
    dgm = _mm(n + "br_a_x", dya, P["wa"], "nt", F32)
    datt = _mm(n + "br_b_x", dyb, P["wb"], "nt", F32)
    dgla = _mm(n + "br_c_x", dyc, P["wc"], "nt", F32)
    du, dv_g, dggm, dws, dbst = _gmlp_bwd(n + "gmlp", p, P["ggm"], P["ws"], P["bst"], dgm)
    gw["gmlp_norm_g"], gw["w_spatial"], gw["b_spatial"] = dggm[0], dws, dbst.T
    kh, vh = s["kh"], s["vh"]
    row = lambda a: a.reshape(a.shape[0], 1, a.shape[1])
    dqx, dkh, dvh, *gw["exchanged"] = _attn_bwd(n + "attn_x", s["qx"], kh, vh, s["ox"], _heads(datt[TC:], NQ),
                                                row(s["lse_x"]), exchange)
    if last:
        dqc = jnp.zeros((NQ, TC, HD), F32)
    else:
        dqc, dkc, dvc = _attn_bwd(n + "attn_c", s["qc"], kh[:, :TC], vh[:, :TC], s["oc"], _heads(datt[:TC], NQ),
                                  row(s["lse_c"]))
        pad = jnp.zeros((NKV, t - TC, HD), F32)
        dkh = dkh + jnp.concatenate([dkc, pad], axis=1)
        dvh = dvh + jnp.concatenate([dvc, pad], axis=1)
    dqr = jnp.concatenate([_unheads(dqc), _unheads(dqx)], axis=0)
    dq, dk, dgq, dgk = _qk_bwd(n + "qk_prep", p, P["gq"], P["gk"], cos, sin, seg, perm, dqr, _unheads(dkh))
    gw["q_norm_g"], gw["k_norm_g"] = dgq.reshape(8, HD).sum(0), dgk.reshape(2, HD).sum(0)
    dv_att = _unheads(dvh).astype(BF16)
    do, dr, dggl = _gla_out_bwd(n + "gla_out", s["of"], s["ob"], p, P["ggl"], dgla)
    gw["gla_norm_g"] = dggl[0]
    scans = _gla_bwd(n + "gla_scan", p, s["la"], s["sf"], s["sb"], do)
    dab, dw2, db2, dglq, dglk, dglv = _decay_bwd(n + "gla_decay", p, P["w2"], P["b2"], scans[:4], scans[4:])
    gw["w_alpha2"] = jnp.stack([dw2[0:16, 0:256], dw2[16:32, 256:512]])
    gw["b_alpha"] = db2.reshape(2, 256)
    dp = jnp.concatenate([dga, dgb, dgc, du, dv_g, dq, dglv, dr, dglq, dglk, dk, dv_att, dab], axis=-1)
    gw["w_in"] = _to_ref_cols(_mm(n + "in_proj_w", s["h1"], dp, "tn", F32, tn_t=2176, tk_t=768))
    dh1 = _mm(n + "in_proj_x", dp, P["w_in"], "nt", F32, tk_t=2176)
    return dx, dh1, dmod


_SMALL = (("norm1_g", (2, D)), ("norm2_g", (2, D)), ("q_norm_g", (2, HD)), ("k_norm_g", (2, HD)), ("gmlp_norm_g", (2, GW)),
          ("gla_norm_g", (2, GLA_V)), ("w_spatial", (2, 4, 128, 128)), ("b_spatial", (2, 4, 128)), ("conv_b", (2, 2 * FFN)),
          ("final_norm_g", (D,))) + tuple((n, (2,) + s) for n, s, _ in _SHARDED_SMALL)
_SMALL_N = 2 * 2 * ADA_W + sum(_prod(s) for _, s in _SMALL)
_SMALL_R = -(-_SMALL_N // (PACK_C * 8)) * 8


def _mod_tables(c, c_ctx, w_ada, b_ada, k):
    x, y, cc = _coords()
    me = 4 * x + 2 * y + cc
    c_all = _allgather_small("gather_c", jnp.concatenate([c, jnp.zeros((7, D), F32)], axis=0))
    c8 = c_all.reshape(N_DEV, 8, D)[:, 0]
    cond = jnp.concatenate([c8, c_ctx[None], jnp.zeros((7, D), F32)], axis=0)
    b_loc = lax.dynamic_slice_in_dim(b_ada, k * ADA_LOC, ADA_LOC, axis=1)[:, None, :]
    m_loc = _adaln_fwd("adaln", cond, w_ada, b_loc)
    m_all = _allgather_small("gather_mod", m_loc.reshape(32, ADA_LOC)).reshape(N_CHIP, 2, 2, 16, ADA_LOC)[:, 0]
    m_all = m_all.transpose(1, 2, 0, 3).reshape(2, 16, ADA_W)
    rows = jnp.stack([m_all[:, 8], lax.dynamic_index_in_dim(m_all, me, axis=1, keepdims=False)], axis=1)
    return rows.reshape(2, 2, 6, D), c8


def _step(x, c, ctx, c_ctx, W, tgt):
    xc, yc, cc = _coords()
    k = 2 * xc + yc
    tx = x.shape[0]
    t = TC + tx
    small = {n: W[n] for n, _ in _SMALL}
    for n, a in _gather_f32_shards([W[n] for n, _, _ in _SHARDED_SMALL]).items():
        small[n + "_full"] = a

    shards = [_layer_shards(W, l) for l in range(2)]
    mods, c8 = _mod_tables(c, c_ctx, W["w_ada"], W["b_ada"], k)
    tabs = _rope_tables(tx) + _lane_consts()
    n_w = len(_SHARDED)
    w_in0 = _allgather_layers("gather_weights", shards[0][:1])
    params = [_layer_params(0, w_in0 + [None] * (n_w - 1), small)]
    behind = shards[0][1:] + shards[1]
    arrived = []

    def late(got):
        arrived.extend(_gather_own(got, behind))
        return _layer_params(0, w_in0 + arrived[:n_w - 1], small)

    xs = jnp.concatenate([ctx, x], axis=0)
    h1 = _nm_fwd("l0_norm1", xs, mods[0], params[0]["g1"], 0, 1)
    s0 = _layer_fwd(0, False, xs, h1, mods[0], params[0], tabs, gather=behind, late=late)
    params[0] = s0["P"]
    params.append(_layer_params(1, arrived[n_w - 1:], small))
    x1, h1b = _res_nm_fwd("l0_res2_norm1", s0["x_mid"], s0["f"], mods[0], 5, mods[1], params[1]["g1"], 0, 1)
    s1 = _layer_fwd(1, True, x1, h1b, mods[1], params[1], tabs)
    loss, dxm_l, df_l, dmod_head, dgf = _head("head", s1["x_mid"], s1["f"], mods[1], W["final_norm_g"][None], tgt)

    gws = [dict(), dict()]
    dx1, dh1b, dmod1 = _layer_bwd(1, True, s1, mods[1], params[1], tabs, dxm_l, df_l, gws[1])
    dxm0, df0, dmod0_g, dmod1_s, dg1b = _res_nm_bwd("l0_b_res2_norm1", s0["x_mid"], s0["f"], mods[0], 5, mods[1],
                                                    params[1]["g1"], 0, 1, dx1, dh1b)
    gws[1]["norm1_g"] = dg1b[0]
    blocks = lambda g, n, ax: g if n == "w_ffn_up" else _to_blocks(g, ax)
    sums1 = _rs_front("l1_", [blocks(gws[1][n], n, ax) for n, ax in _SHARDED])
    dx0, dh1, dmod0 = _layer_bwd(0, False, s0, mods[0], params[0], tabs, dxm0, df0, gws[0], exchange=sums1)
    red1 = _rs_sum("l1_", _exchange_own(gws[0]["exchanged"], sums1))
    grad_x, dmod0_s, dg1 = _nm_bwd("l0_b_norm1", xs, mods[0], params[0]["g1"], 0, 1, dx0, dh1)
    gws[0]["norm1_g"] = dg1[0]
    dmods = jnp.stack([dmod0 + dmod0_g + dmod0_s, dmod1 + dmod1_s + dmod_head])

    stk = {n: jnp.stack([gws[0][n], gws[1][n]]) for n, _ in _SMALL if n != "final_norm_g"}
    stk["final_norm_g"] = dgf[0]
    flat = jnp.concatenate([dmods.reshape(-1)] + [stk[n].reshape(-1) for n, _ in _SMALL])
    flat = jnp.concatenate([flat, jnp.zeros((_SMALL_R * PACK_C - _SMALL_N,), F32)]).reshape(_SMALL_R, PACK_C)
    every = _allgather_small("gather_small_grads", flat).reshape(N_DEV, _SMALL_R, PACK_C)
    tot = _sum_slots("sum_small_grads", every).reshape(-1)
    grads, o = {}, 2 * 2 * ADA_W
    for n, shp in _SMALL:
        grads[n] = tot[o:o + _prod(shp)].reshape(shp)
        o += _prod(shp)
    grads["b_ada"] = tot[:2 * 2 * ADA_W].reshape(2, 2, ADA_W).sum(axis=1)

    dm_every = every[:, :2 * 2 * ADA_W // PACK_C].reshape(N_DEV, 2, 2, ADA_W)
    dm_loc = lax.dynamic_slice_in_dim(dm_every, k * ADA_LOC, ADA_LOC, axis=3).transpose(1, 2, 0, 3)
    cc8 = jnp.concatenate([c_ctx[None], jnp.zeros((7, D), F32)], axis=0)
    grads["w_ada"], dcc = _adaln_bwd("adaln_b", c8, cc8, W["w_ada"], dm_loc[:, 1], dm_loc[:, 0])
    dcc_every = _allgather_small("gather_dcctx", dcc * 0.5).reshape(N_DEV, 8, D)
    grads["c_ctx"] = _sum_slots("sum_dcctx", dcc_every)[0]

    for n, shp, ax in _SHARDED_SMALL:
        grads[n] = lax.dynamic_slice_in_dim(grads[n], k * (shp[ax] // N_CHIP), shp[ax] // N_CHIP, axis=ax + 1)
    sums0 = _rs_front("l0_", [blocks(gws[0][n], n, ax) for n, ax in _SHARDED])
    red0 = _rs_sum("l0_", _rs_chip_exchange("l0_rs_chip_exchange", sums0))
    for (n, _), a in zip(_SHARDED, _pair_allgather("rs_pair_allgather", red0, red1)):
        grads[n] = a.reshape(2, 2 * a.shape[2], a.shape[3])
    return loss[0, 0], grad_x, grads


_WEIGHTS = ("c_ctx", "w_ada", "b_ada", "norm1_g", "norm2_g", "w_in", "q_norm_g", "k_norm_g", "gmlp_norm_g", "w_spatial",
            "b_spatial", "w_alpha2", "b_alpha", "gla_norm_g", "w_br_a", "w_br_b", "w_br_c", "w_out", "w_ffn_up", "conv_w",
            "conv_b", "w_ffn_down", "final_norm_g")
_BIG = ("w_ada", "w_in", "w_br_a", "w_br_b", "w_br_c", "w_out", "w_ffn_up", "w_ffn_down")


def _update(W, G, M, V):
    delta, new_m, new_v = {}, {}, {}
    for n in _BIG:
        delta[n], new_m[n], new_v[n] = _adamw("adamw_" + n, W[n], G[n], M[n], V[n])
    rest = [n for n in _WEIGHTS if n not in _BIG]
    tot = sum(_prod(W[n].shape) for n in rest)
    rows = -(-tot // (PACK_C * 8)) * 8

    def cat(dct):
        flat = jnp.concatenate([dct[n].reshape(-1) for n in rest] + [jnp.zeros((rows * PACK_C - tot,), F32)])
        return flat.reshape(1, rows, PACK_C)

    outs = _adamw("adamw_small", cat(W), cat(G), cat(M), cat(V))
    o = 0
    for n in rest:
        sz, shp = _prod(W[n].shape), W[n].shape
        delta[n], new_m[n], new_v[n] = (a.reshape(-1)[o:o + sz].reshape(shp) for a in outs)
        o += sz
    return delta, new_m, new_v


def kernel(x, c, ctx, c_ctx, w_ada, b_ada, norm1_g, norm2_g, w_in, q_norm_g, k_norm_g, gmlp_norm_g, w_spatial, b_spatial, w_alpha2, b_alpha, gla_norm_g, w_br_a, w_br_b, w_br_c, w_out, w_ffn_up, conv_w, conv_b, w_ffn_down, final_norm_g, loss_target, m_c_ctx, m_w_ada, m_b_ada, m_norm1_g, m_norm2_g, m_w_in, m_q_norm_g, m_k_norm_g, m_gmlp_norm_g, m_w_spatial, m_b_spatial, m_w_alpha2, m_b_alpha, m_gla_norm_g, m_w_br_a, m_w_br_b, m_w_br_c, m_w_out, m_w_ffn_up, m_conv_w, m_conv_b, m_w_ffn_down, m_final_norm_g, v_c_ctx, v_w_ada, v_b_ada, v_norm1_g, v_norm2_g, v_w_in, v_q_norm_g, v_k_norm_g, v_gmlp_norm_g, v_w_spatial, v_b_spatial, v_w_alpha2, v_b_alpha, v_gla_norm_g, v_w_br_a, v_w_br_b, v_w_br_c, v_w_out, v_w_ffn_up, v_conv_w, v_conv_b, v_w_ffn_down, v_final_norm_g):
    W = dict(c_ctx=c_ctx, w_ada=w_ada, b_ada=b_ada, norm1_g=norm1_g, norm2_g=norm2_g, w_in=w_in, q_norm_g=q_norm_g,
             k_norm_g=k_norm_g, gmlp_norm_g=gmlp_norm_g, w_spatial=w_spatial, b_spatial=b_spatial, w_alpha2=w_alpha2,
             b_alpha=b_alpha, gla_norm_g=gla_norm_g, w_br_a=w_br_a, w_br_b=w_br_b, w_br_c=w_br_c, w_out=w_out,
             w_ffn_up=w_ffn_up, conv_w=conv_w, conv_b=conv_b, w_ffn_down=w_ffn_down, final_norm_g=final_norm_g)
    M = dict(c_ctx=m_c_ctx, w_ada=m_w_ada, b_ada=m_b_ada, norm1_g=m_norm1_g, norm2_g=m_norm2_g, w_in=m_w_in,
             q_norm_g=m_q_norm_g, k_norm_g=m_k_norm_g, gmlp_norm_g=m_gmlp_norm_g, w_spatial=m_w_spatial,
             b_spatial=m_b_spatial, w_alpha2=m_w_alpha2, b_alpha=m_b_alpha, gla_norm_g=m_gla_norm_g, w_br_a=m_w_br_a,
             w_br_b=m_w_br_b, w_br_c=m_w_br_c, w_out=m_w_out, w_ffn_up=m_w_ffn_up, conv_w=m_conv_w, conv_b=m_conv_b,
             w_ffn_down=m_w_ffn_down, final_norm_g=m_final_norm_g)
    V = dict(c_ctx=v_c_ctx, w_ada=v_w_ada, b_ada=v_b_ada, norm1_g=v_norm1_g, norm2_g=v_norm2_g, w_in=v_w_in,
             q_norm_g=v_q_norm_g, k_norm_g=v_k_norm_g, gmlp_norm_g=v_gmlp_norm_g, w_spatial=v_w_spatial,
             b_spatial=v_b_spatial, w_alpha2=v_w_alpha2, b_alpha=v_b_alpha, gla_norm_g=v_gla_norm_g, w_br_a=v_w_br_a,
             w_br_b=v_w_br_b, w_br_c=v_w_br_c, w_out=v_w_out, w_ffn_up=v_w_ffn_up, conv_w=v_conv_w, conv_b=v_conv_b,
             w_ffn_down=v_w_ffn_down, final_norm_g=v_final_norm_g)
    loss_local, grad_x, G = _step(x[0], c, ctx[0], c_ctx, W, loss_target[0])
    loss = lax.psum(loss_local, ("x", "y", "c"))
    delta, new_m, new_v = _update(W, G, M, V)
    return (loss, grad_x[None], *[G[n] for n in _WEIGHTS], *[delta[n] for n in _WEIGHTS],
            *[new_m[n] for n in _WEIGHTS], *[new_v[n] for n in _WEIGHTS])
```

```python
import functools

import jax
import jax.numpy as jnp
from jax import lax
from jax.experimental import pallas as pl
from jax.experimental.pallas import tpu as pltpu

F32 = jnp.float32
BF16 = jnp.bfloat16

D = 1024
TC = 256
GRID_W = 64
EPS = 1e-6
HD = 64
NQ = 8
NKV = 2
QG = NQ // NKV
GLA_H = 4
GLA_DK = 64
GLA_DV = 128
GLA_QK = 256
GLA_V = 512
GLA_CHUNK = 64
GLA_TAU = 16.0
GW = 512
FFN = 2816
IN_W = 6432
PW = 6528
ADA_W = 6 * D
N_CHIP = 4
N_DEV = 8
ADA_LOC = ADA_W // N_CHIP

ADAM_LR = 0.001
ADAM_B1 = 0.9
ADAM_B2 = 0.999
ADAM_EPS = 1e-08
ADAM_WD = 0.01
ADAM_STEP = 10

TM = 256
NCB = TC // TM
LANE = 128
VMEM_LIMIT = 48 * 1024 * 1024
VMEM_LIMIT_ATTN_BWD = 56 * 1024 * 1024
MESH = pl.DeviceIdType.MESH

_COLS = (("gA", 3360, 1024), ("gB", 4384, 1024), ("gC", 5408, 1024), ("gu", 0, 512), ("gv", 512, 512),
         ("q", 1024, 512), ("glv", 2304, 512), ("gr", 2848, 512), ("glq", 1792, 256), ("glk", 2048, 256),
         ("k", 1536, 128), ("v", 1664, 128), ("ab", 2816, 32))
OFF = {}
_o = 0
for _n, _s, _w in _COLS:
    OFF[_n] = _o
    _o += max(_w, LANE)
assert _o == PW


def _to_new_cols(w):
    parts = [w[..., s:s + n] for _, s, n in _COLS]
    pad = jnp.zeros(w.shape[:-1] + (PW - IN_W,), w.dtype)
    return jnp.concatenate(parts + [pad], axis=-1)


def _to_ref_cols(w):
    by_start = sorted(_COLS, key=lambda t: t[1])
    return jnp.concatenate([w[..., OFF[n]:OFF[n] + wd] for n, _, wd in by_start], axis=-1)


def _tile(n, target, align=LANE):
    best = None
    t = align
    while t <= min(n, target):
        if n % t == 0:
            best = t
        t += align
    assert best is not None, (n, target, align)
    return best


def _cp(sem=None, vmem=VMEM_LIMIT):
    return pltpu.CompilerParams(dimension_semantics=sem, vmem_limit_bytes=vmem)


def _bdot_impl(a, b, ca, cb):
    return lax.dot_general(a.astype(BF16), b.astype(BF16), (((ca,), (cb,)), ((), ())),
                           preferred_element_type=F32)


@functools.partial(jax.custom_vjp, nondiff_argnums=(2, 3))
def bdot(a, b, ca, cb):
    return _bdot_impl(a, b, ca, cb)


def _bdot_fwd(a, b, ca, cb):
    return _bdot_impl(a, b, ca, cb), (a, b)


def _bdot_bwd(ca, cb, res, g):
    a, b = res
    da = bdot(g, b, 1, 1 - cb) if ca == 1 else bdot(b, g, 1 - cb, 1)
    db = bdot(a, g, 1 - ca, 0) if cb == 0 else bdot(g, a, 0, 1 - ca)
    return da.astype(a.dtype), db.astype(b.dtype)


bdot.defvjp(_bdot_fwd, _bdot_bwd)


def hdot(a, b, ca=1, cb=0):
    return lax.dot_general(a, b, (((ca,), (cb,)), ((), ())), precision=lax.Precision.HIGH,
                           preferred_element_type=F32)


def _rms(x, g):
    return x * lax.rsqrt(jnp.mean(x * x, axis=-1, keepdims=True) + EPS) * g


def _gelu(x):
    return 0.5 * x * (1.0 + jnp.tanh(0.7978845608028654 * (x + 0.044715 * (x * x * x))))


def _log_sigmoid(z):
    return jnp.minimum(z, 0.0) - jnp.log(1.0 + jnp.exp(-jnp.abs(z)))


def _sel(mod, is_lat, idx):
    return jnp.where(is_lat, mod[1, idx:idx + 1, :], mod[0, idx:idx + 1, :])


def _rows_call(name, fn, grid, ins, outs, acc_axes=None, sem=None):
    n_in = len(ins)
    flags = [o[2] for o in outs]
    if acc_axes is None:
        acc_axes = (len(grid) - 1,)

    def body(*refs):
        ids = tuple(pl.program_id(a) for a in range(len(grid)))
        res = fn(ids, *[r[...] for r in refs[:n_in]])
        for r, v, acc in zip(refs[n_in:], res, flags):
            if acc:
                first = functools.reduce(jnp.logical_and, [ids[a] == 0 for a in acc_axes])

                @pl.when(first)
                def _():
                    r[...] = jnp.zeros_like(r)
                r[...] += v.astype(r.dtype)
            else:
                r[...] = v.astype(r.dtype)

    return pl.pallas_call(
        body, name=name, grid=grid, in_specs=[s for _, s in ins], out_specs=[o[1] for o in outs],
        out_shape=[o[0] for o in outs],
        compiler_params=_cp(sem if sem is not None else ("arbitrary",) * len(grid)),
    )(*[a for a, _ in ins])


def _sds(shape, dtype):
    return jax.ShapeDtypeStruct(shape, dtype)


def _rowspec(width, off=0, tm=TM):
    assert off % width == 0
    return pl.BlockSpec((tm, width), lambda i, o=off // width: (i, o))


def _full(shape):
    nd = len(shape)
    return pl.BlockSpec(shape, lambda *a: (0,) * nd)


def _mm(name, a, b, mode, out_dtype, tm_t=1056, tn_t=1408, tk_t=1408, chip_blocks=False, j_outer=False):
    halves = a.ndim == 3 or b.ndim == 3
    if mode == "nn":
        (m, k), (_, n) = a.shape, b.shape
    elif mode == "nt":
        (m, k), (n, _) = a.shape[-2:], b.shape
        k *= a.ndim - 1
    else:
        (k, m), (_, n) = a.shape, b.shape[-2:]
        n *= b.ndim - 1
    tm = _tile(m, tm_t, 8 if m % LANE else LANE)
    tn = _tile(n // 2 if halves and mode == "tn" else n, tn_t)
    tk = _tile(k // 2 if halves and mode == "nt" else k, tk_t)
    nk = k // tk
    if mode == "nn":
        dims, a_spec, b_spec = ((1,), (0,)), pl.BlockSpec((tm, tk), lambda i, j, l: (i, l)), pl.BlockSpec((tk, tn), lambda i, j, l: (l, j))
    elif mode == "nt":
        dims, a_spec, b_spec = ((1,), (1,)), pl.BlockSpec((tm, tk), lambda i, j, l: (i, l)), pl.BlockSpec((tn, tk), lambda i, j, l: (j, l))
        if halves:
            a_spec = pl.BlockSpec((None, tm, tk), lambda i, j, l, h=nk // 2: (l // h, i, l % h))
    else:
        dims, a_spec, b_spec = ((0,), (0,)), pl.BlockSpec((tk, tm), lambda i, j, l: (l, i)), pl.BlockSpec((tk, tn), lambda i, j, l: (l, j))
        if halves:
            b_spec = pl.BlockSpec((None, tk, tn), lambda i, j, l, h=n // tn // 2: (j // h, l, j % h))

    def body(a_ref, b_ref, o_ref, *scratch):
        l = pl.program_id(2)
        part = lax.dot_general(a_ref[...].astype(BF16), b_ref[...].astype(BF16), (dims, ((), ())),
                               preferred_element_type=F32)
        if nk == 1:
            o_ref[...] = part.astype(o_ref.dtype)
            return
        acc_ref = scratch[0]

        @pl.when(l == 0)
        def _():
            acc_ref[...] = part

        @pl.when(l > 0)
        def _():
            acc_ref[...] += part

        @pl.when(l == nk - 1)
        def _():
            o_ref[...] = acc_ref[...].astype(o_ref.dtype)

    o_spec, o_shape = pl.BlockSpec((tm, tn), lambda i, j, l: (i, j)), _sds((m, n), out_dtype)
    if chip_blocks:
        assert tn * N_CHIP == n and tm == m
        o_spec, o_shape = pl.BlockSpec((None, tm, tn), lambda i, j, l: (j, 0, 0)), _sds((N_CHIP, m, tn), out_dtype)
    grid = (m // tm, n // tn, nk)
    if j_outer:
        swap = lambda spec: pl.BlockSpec(spec.block_shape, lambda j, i, l, f=spec.index_map: f(i, j, l))
        a_spec, b_spec, o_spec, grid = swap(a_spec), swap(b_spec), swap(o_spec), (n // tn, m // tm, nk)
    return pl.pallas_call(
        body, name=name, grid=grid, in_specs=[a_spec, b_spec], out_specs=o_spec, out_shape=o_shape,
        scratch_shapes=[pltpu.VMEM((tm, tn), F32)] if nk > 1 else [],
        compiler_params=_cp(("parallel", "parallel", "arbitrary")),
    )(a, b)


def _nm_fn(is_lat, x, mod, g, shift, scale):
    return _rms(x, g) * (1.0 + _sel(mod, is_lat, scale)) + _sel(mod, is_lat, shift)


def _res_nm_fn(is_lat, x, br, modg, gate, mods, g, shift, scale):
    xn = x + _sel(modg, is_lat, gate) * br
    return xn, _nm_fn(is_lat, xn, mods, g, shift, scale)


def _nm_fwd(name, x, mod, g, shift, scale):
    t = x.shape[0]
    fn = lambda ids, xv, mv, gv: (_nm_fn(ids[0] >= NCB, xv, mv, gv, shift, scale),)
    return _rows_call(name, fn, (t // TM,), [(x, _rowspec(D)), (mod, _full((2, 6, D))), (g, _full((1, D)))],
                      [(_sds((t, D), BF16), _rowspec(D), False)])[0]


def _nm_bwd(name, x, mod, g, shift, scale, dx_res, dh):
    t = x.shape[0]

    def fn(ids, xv, mv, gv, dxr, dhv):
        _, vjp = jax.vjp(lambda a, b, c: _nm_fn(ids[0] >= NCB, a, b, c, shift, scale), xv, mv, gv)
        dx, dm, dg = vjp(dhv)
        return dx + dxr, dm, dg

    lat = pl.BlockSpec((TM, D), lambda i: (jnp.maximum(i - NCB, 0), 0))
    return _rows_call(name, fn, (t // TM,),
                      [(x, _rowspec(D)), (mod, _full((2, 6, D))), (g, _full((1, D))), (dx_res, _rowspec(D)), (dh, _rowspec(D))],
                      [(_sds((t - TC, D), F32), lat, False), (_sds((2, 6, D), F32), _full((2, 6, D)), True),
                       (_sds((1, D), F32), _full((1, D)), True)])


def _res_nm_fwd(name, x, br, modg, gate, mods, g, shift, scale):
    t = x.shape[0]
    fn = lambda ids, xv, bv, mg, ms, gv: _res_nm_fn(ids[0] >= NCB, xv, bv, mg, gate, ms, gv, shift, scale)
    return _rows_call(name, fn, (t // TM,),
                      [(x, _rowspec(D)), (br, _rowspec(D)), (modg, _full((2, 6, D))), (mods, _full((2, 6, D))), (g, _full((1, D)))],
                      [(_sds((t, D), F32), _rowspec(D), False), (_sds((t, D), BF16), _rowspec(D), False)])


def _res_nm_bwd(name, x, br, modg, gate, mods, g, shift, scale, dx_res, dh):
    t = x.shape[0]

    def fn(ids, xv, bv, mg, ms, gv, dxr, dhv):
        f = lambda a, b, c, d, e: _res_nm_fn(ids[0] >= NCB, a, b, c, gate, d, e, shift, scale)
        _, vjp = jax.vjp(f, xv, bv, mg, ms, gv)
        return vjp((dxr, dhv))

    m26 = (_sds((2, 6, D), F32), _full((2, 6, D)), True)
    return _rows_call(name, fn, (t // TM,),
                      [(x, _rowspec(D)), (br, _rowspec(D)), (modg, _full((2, 6, D))), (mods, _full((2, 6, D))), (g, _full((1, D))),
                       (dx_res, _rowspec(D)), (dh, _rowspec(D))],
                      [(_sds((t, D), F32), _rowspec(D), False), (_sds((t, D), BF16), _rowspec(D), False), m26, m26,
                       (_sds((1, D), F32), _full((1, D)), True)])


def _head(name, x_mid, f, mod, gf, tgt):
    t = x_mid.shape[0]

    def fn(ids, xv, fv, mv, gv, tv):
        def loss_fn(a, b, c, d):
            y = _rms(a + c[1, 5:6, :] * b, d)
            e = y - tv
            return 0.5 * jnp.sum(jnp.mean(e * e, axis=-1))
        loss, grads = jax.value_and_grad(loss_fn, argnums=(0, 1, 2, 3))(xv, fv, mv, gv)
        return tuple(jnp.where(ids[0] >= NCB, v, 0.0) for v in (jnp.reshape(loss, (1, 1)),) + grads)

    return _rows_call(name, fn, (t // TM,),
                      [(x_mid, _rowspec(D)), (f, _rowspec(D)), (mod, _full((2, 6, D))), (gf, _full((1, D))),
                       (tgt, pl.BlockSpec((TM, D), lambda i: (jnp.maximum(i - NCB, 0), 0)))],
                      [(_sds((1, 1), F32), _full((1, 1)), True), (_sds((t, D), F32), _rowspec(D), False),
                       (_sds((t, D), BF16), _rowspec(D), False), (_sds((2, 6, D), F32), _full((2, 6, D)), True),
                       (_sds((1, D), F32), _full((1, D)), True)])


def _gmlp_fn(u, v, g, ws, bst):
    rows = []
    u, v = u.astype(F32), v.astype(F32)
    for r in range(u.shape[0] // 128):
        uu, vv = _gelu(u[128 * r:128 * r + 128]), _gelu(v[128 * r:128 * r + 128])
        cols = []
        for gi in range(4):
            sl = slice(128 * gi, 128 * gi + 128)
            f = bdot(ws[gi], _rms(vv[:, sl], g[:, sl]), 1, 0) + bst[:, gi:gi + 1]
            cols.append(uu[:, sl] * f)
        rows.append(jnp.concatenate(cols, axis=-1))
    return jnp.concatenate(rows, axis=0)


def _gmlp_ins(p, g, ws, bst):
    return [(p, _rowspec(GW, OFF["gu"])), (p, _rowspec(GW, OFF["gv"])), (g, _full((1, GW))),
            (ws, _full((4, 128, 128))), (bst, _full((128, 4)))]


def _gmlp_fwd(name, p, g, ws, bst):
    t = p.shape[0]
    return _rows_call(name, lambda ids, *a: (_gmlp_fn(*a),), (t // TM,), _gmlp_ins(p, g, ws, bst),
                      [(_sds((t, GW), BF16), _rowspec(GW), False)])[0]


def _gmlp_bwd(name, p, g, ws, bst, dgm):
    t = p.shape[0]

    def fn(ids, u, v, gv, wv, bv, dv):
        _, vjp = jax.vjp(_gmlp_fn, u, v, gv, wv, bv)
        return vjp(dv)

    return _rows_call(name, fn, (t // TM,), _gmlp_ins(p, g, ws, bst) + [(dgm, _rowspec(GW))],
                      [(_sds((t, GW), BF16), _rowspec(GW), False), (_sds((t, GW), BF16), _rowspec(GW), False),
                       (_sds((1, GW), F32), _full((1, GW)), True), (_sds((4, 128, 128), F32), _full((4, 128, 128)), True),
                       (_sds((128, 4), F32), _full((128, 4)), True)])


def _qk_fn(q, k, gq, gk, cos, sin, seg, perm):
    cq, sq = jnp.concatenate([cos] * 4, axis=-1), jnp.concatenate([sin] * 4, axis=-1)
    q, k = q.astype(F32), k.astype(F32)
    qn = q * lax.rsqrt(hdot(q * q, seg) + EPS) * gq
    kn = k * lax.rsqrt(hdot(k * k, seg[:128, :128]) + EPS) * gk
    qr = qn * cq + hdot(qn, perm) * sq
    kr = kn * cos + hdot(kn, perm[:128, :128]) * sin
    return qr * (HD ** -0.5), kr


def _qk_ins(p, gq, gk, cos, sin, seg, perm):
    return [(p, _rowspec(512, OFF["q"])), (p, _rowspec(128, OFF["k"])), (gq, _full((1, 512))), (gk, _full((1, 128))),
            (cos, _rowspec(128)), (sin, _rowspec(128)), (seg, _full((512, 512))), (perm, _full((512, 512)))]


def _qk_fwd(name, p, gq, gk, cos, sin, seg, perm):
    t = p.shape[0]
    fn = lambda ids, q, k, a, b, c, s, sg, pm, v: _qk_fn(q, k, a, b, c, s, sg, pm) + (v,)
    return _rows_call(name, fn, (t // TM,), _qk_ins(p, gq, gk, cos, sin, seg, perm) + [(p, _rowspec(128, OFF["v"]))],
                      [(_sds((t, 512), BF16), _rowspec(512), False), (_sds((t, 128), BF16), _rowspec(128), False),
                       (_sds((t, 128), BF16), _rowspec(128), False)])


def _qk_bwd(name, p, gq, gk, cos, sin, seg, perm, dqr, dkr):
    t = p.shape[0]

    def fn(ids, q, k, a, b, c, s, sg, pm, dq, dk):
        _, vjp = jax.vjp(lambda q_, k_, a_, b_: _qk_fn(q_, k_, a_, b_, c, s, sg, pm), q, k, a, b)
        return vjp((dq, dk))

    return _rows_call(name, fn, (t // TM,),
                      _qk_ins(p, gq, gk, cos, sin, seg, perm) + [(dqr, _rowspec(512)), (dkr, _rowspec(128))],
                      [(_sds((t, 512), BF16), _rowspec(512), False), (_sds((t, 128), BF16), _rowspec(128), False),
                       (_sds((1, 512), F32), _full((1, 512)), True), (_sds((1, 128), F32), _full((1, 128)), True)])


_ATT_TQ = 1024
_ATT_TK = 768


def _attn_fwd(name, q, k, v, gather=()):
    h, tq_all, _ = q.shape
    hkv, tk_all, _ = k.shape
    tq, tk = _tile(tq_all, _ATT_TQ), _tile(tk_all, _ATT_TK)
    nkc = tk_all // tk
    ng, nq = len(gather), tq_all // tq

    def body(*refs):
        q_ref, k_ref, v_ref = refs[:3]
        o_ref, lse_ref = refs[3 + ng:5 + ng]
        if ng:
            g_id, i_id = pl.program_id(0), pl.program_id(1)
            stage = lambda st: _gather_stage(st, refs[3:3 + ng], refs[5 + ng:5 + 2 * ng], *refs[5 + 2 * ng:])
            pl.when(jnp.logical_and(g_id == 0, i_id == 0))(lambda: stage(0))
            pl.when(jnp.logical_and(g_id == hkv - 1, i_id == 0))(lambda: stage(1))
        qv = q_ref[...].reshape(QG * tq, HD)

        def step(j, carry):
            m, acc = carry
            off = pl.multiple_of(j * tk, tk)
            kk, vv = k_ref[0, pl.ds(off, tk), :], v_ref[0, pl.ds(off, tk), :]
            s = lax.dot_general(qv, kk, (((1,), (1,)), ((), ())), preferred_element_type=F32)
            m_new = jnp.maximum(m, jnp.max(s, axis=-1, keepdims=True))
            pr = jnp.exp(s - m_new)
            acc = jnp.exp(m - m_new) * acc + jnp.dot(pr.astype(BF16), vv, preferred_element_type=F32)
            return m_new, acc

        init = (jnp.full((QG * tq, 1), -jnp.inf, F32), jnp.zeros((QG * tq, 2 * HD), F32))
        m, acc = lax.fori_loop(0, nkc, step, init)
        l = acc[:, HD:HD + 1]
        o_ref[...] = (acc[:, :HD] / l).reshape(QG, tq, HD)
        lse_ref[...] = (m + jnp.log(l)).reshape(QG, tq, 1)
        if ng:
            pl.when(jnp.logical_and(g_id == hkv - 1, i_id == nq - 1))(lambda: stage(2))

    kv_spec = pl.BlockSpec((1, tk_all, HD), lambda g, i: (g, 0, 0))
    v1_spec = pl.BlockSpec((1, tk_all, 2 * HD), lambda g, i: (g, 0, 0))
    qspec = pl.BlockSpec((QG, tq, HD), lambda g, i: (g, i, 0))
    sems = [pltpu.SemaphoreType.DMA((6 * ng,)), pltpu.SemaphoreType.DMA((6 * ng,))] if ng else []
    return pl.pallas_call(
        body, name=name, grid=(hkv, nq), in_specs=[qspec, kv_spec, v1_spec] + [_ANY] * ng,
        out_specs=[qspec, pl.BlockSpec((QG, tq, 1), lambda g, i: (g, i, 0))] + [_ANY] * ng,
        out_shape=[_sds((h, tq_all, HD), F32), _sds((h, tq_all, 1), F32)]
        + [_sds((N_CHIP,) + a.shape, a.dtype) for a in gather],
        scratch_shapes=sems, compiler_params=_cp(("arbitrary", "arbitrary")),
    )(q, k, v, *gather)


def _attn_bwd(name, q, k, v, o, do, lse_row, exchange=()):
    h, tq_all, _ = q.shape
    hkv, tk_all, _ = k.shape
    tq, tk = _tile(tq_all, 1024), _tile(tk_all, 1408)
    ne, nq, nk = len(exchange), tq_all // tq, tk_all // tk

    def body(*refs):
        q_ref, k_ref, v_ref, o_ref, do_ref, lse_ref = refs[:6]
        dq_ref, dk_ref, dv_ref = refs[6 + ne:9 + ne]
        dl_ref = refs[9 + 2 * ne]
        g_id, i, j = pl.program_id(0), pl.program_id(1), pl.program_id(2)
        if ne:
            stage = lambda st: _exchange_stage(st, refs[6:6 + ne], refs[9 + ne:9 + 2 * ne], *refs[10 + 2 * ne:])
            first = functools.reduce(jnp.logical_and, [g_id == 0, i == 0, j == 0])
            pl.when(first)(lambda: stage(0))

        @pl.when(j == 0)
        def _():
            ones = jnp.ones((8, HD), F32)
            for g in range(QG):
                dl_ref[g] = hdot(ones, do_ref[g] * o_ref[g], 1, 1)

        kk, vv = k_ref[0], v_ref[0]
        dk_acc, dv_acc = jnp.zeros((tk, HD), F32), jnp.zeros((tk, HD), F32)
        for g in range(QG):
            qv, dob = q_ref[g], do_ref[g].astype(BF16)
            st = lax.dot_general(kk, qv, (((1,), (1,)), ((), ())), preferred_element_type=F32)
            pt = jnp.exp(st - lse_ref[g])
            dv_acc += jnp.dot(pt.astype(BF16), dob, preferred_element_type=F32)
            dpt = lax.dot_general(vv, dob, (((1,), (1,)), ((), ())), preferred_element_type=F32)
            dst = (pt * (dpt - dl_ref[g, 0:1, :])).astype(BF16)
            dk_acc += jnp.dot(dst, qv, preferred_element_type=F32)
            dq_part = lax.dot_general(dst, kk, (((0,), (0,)), ((), ())), preferred_element_type=F32)

            @pl.when(j == 0)
            def _():
                dq_ref[g] = dq_part

            @pl.when(j > 0)
            def _():
                dq_ref[g] += dq_part

        rows = pl.ds(pl.multiple_of(j * tk, tk), tk)

        @pl.when(i == 0)
        def _():
            dk_ref[0, rows, :] = dk_acc
            dv_ref[0, rows, :] = dv_acc

        @pl.when(i > 0)
        def _():
            dk_ref[0, rows, :] += dk_acc
            dv_ref[0, rows, :] += dv_acc

        if ne:
            last = functools.reduce(jnp.logical_and, [g_id == hkv - 1, i == nq - 1, j == nk - 1])
            pl.when(last)(lambda: stage(1))

    ks = pl.BlockSpec((1, tk, HD), lambda g, i, j: (g, j, 0))
    qs = pl.BlockSpec((QG, tq, HD), lambda g, i, j: (g, i, 0))
    rs = pl.BlockSpec((QG, 1, tq), lambda g, i, j: (g, 0, i))
    full = pl.BlockSpec((1, tk_all, HD), lambda g, i, j: (g, 0, 0))
    sems = [pltpu.SemaphoreType.DMA((3 * ne,)), pltpu.SemaphoreType.DMA((3 * ne,))] if ne else []
    return pl.pallas_call(
        body, name=name, grid=(hkv, nq, nk), in_specs=[qs, ks, ks, qs, qs, rs] + [_ANY] * ne,
        out_specs=[qs, full, full] + [_ANY] * ne,
        out_shape=[_sds((h, tq_all, HD), F32), _sds((hkv, tk_all, HD), F32), _sds((hkv, tk_all, HD), F32)]
        + [_sds(a.shape, a.dtype) for a in exchange],
        scratch_shapes=[pltpu.VMEM((QG, 8, tq), F32)] + sems,
        compiler_params=_cp(("arbitrary", "arbitrary", "arbitrary"), VMEM_LIMIT_ATTN_BWD),
    )(q, k, v, o, do, lse_row, *exchange)


def _decay_fn(a, w2, b2):
    return _log_sigmoid(bdot(a, w2, 1, 0) + b2) / GLA_TAU


def _decay_fwd(name, p, w2, b2):
    t = p.shape[0]
    return _rows_call(name, lambda ids, a, w, b: (_decay_fn(a, w, b),), (t // TM,),
                      [(p, _rowspec(128, OFF["ab"])), (w2, _full((128, 512))), (b2, _full((1, 512)))],
                      [(_sds((t, 512), F32), _rowspec(512), False)])[0]


def _decay_bwd(name, p, w2, b2, gf, gb):
    t = p.shape[0]

    def fn(ids, a, w, b, qf, kf, vf, lf, qb, kb, vb, lb):
        _, vjp = jax.vjp(_decay_fn, a, w, b)
        both = lambda f, b: f.astype(F32) + b.astype(F32)
        return vjp(jnp.concatenate([lf, lb], axis=-1)) + (both(qf, qb), both(kf, kb), both(vf, vb))

    widths = (256, 256, 512, 256)
    return _rows_call(name, fn, (t // TM,),
                      [(p, _rowspec(128, OFF["ab"])), (w2, _full((128, 512))), (b2, _full((1, 512)))]
                      + [(g, _rowspec(w)) for g, w in zip(gf, widths)] + [(g, _rowspec(w)) for g, w in zip(gb, widths)],
                      [(_sds((t, 128), BF16), _rowspec(128), False), (_sds((128, 512), F32), _full((128, 512)), True),
                       (_sds((1, 512), F32), _full((1, 512)), True)]
                      + [(_sds((t, w), BF16), _rowspec(w), False) for w in widths[:3]])


def _gla_consts(reverse):
    r = lax.broadcasted_iota(jnp.int32, (GLA_CHUNK, GLA_CHUNK), 0)
    c = lax.broadcasted_iota(jnp.int32, (GLA_CHUNK, GLA_CHUNK), 1)
    trib = (r <= c) if reverse else (r >= c)
    br = lax.broadcasted_iota(jnp.int32, (GLA_QK, GLA_V), 0) // GLA_DK
    bc = lax.broadcasted_iota(jnp.int32, (GLA_QK, GLA_V), 1) // GLA_DV
    lane_head = lax.broadcasted_iota(jnp.int32, (1, GLA_QK), 1) // GLA_DK
    return trib, (br == bc).astype(F32), lane_head


def _gla_chunk(q, k, v, la, s_in, consts):
    trib, bd, lane_head = consts
    q, k = q.astype(F32), k.astype(F32)
    cum = hdot(trib.astype(F32), la)
    tot = jnp.sum(la, axis=0, keepdims=True)
    q_in = q * (GLA_DK ** -0.5) * jnp.exp(cum)
    k_in = k * jnp.exp(-cum)
    k_st = k * jnp.exp(tot - cum)
    outs = []
    for h in range(GLA_H):
        att = bdot(jnp.where(lane_head == h, q_in, 0.0), k_in, 1, 1)
        att = jnp.where(trib, att, 0.0)
        outs.append(bdot(att, v[:, GLA_DV * h:GLA_DV * (h + 1)], 1, 0))
    o = jnp.concatenate(outs, axis=-1) + bdot(q_in, s_in, 1, 0)
    decay = jnp.exp(hdot(la, jnp.ones((GLA_CHUNK, LANE), F32), 0, 0))
    s_out = jnp.concatenate([decay] * (GLA_V // LANE), axis=-1) * s_in + bdot(k_st, v, 0, 0) * bd
    return o, s_out


def _gla_order(nb, reverse, backward):
    if not reverse:
        return (lambda s: nb - 1 - s) if backward else (lambda s: s)
    if backward:
        return lambda s: jnp.where(s == nb - 1, 0, s + 1)
    return lambda s: jnp.where(s == 0, 0, nb - s)


_NCH = TM // GLA_CHUNK


def _gla_specs(nb, reverse, backward):
    order = _gla_order(nb, reverse, backward)
    col = lambda width, off: pl.BlockSpec((TM, width), lambda s, o=off // width: (order(s), o))
    state = pl.BlockSpec((_NCH, GLA_H, GLA_DK, GLA_DV), lambda s: (order(s), 0, 0, 0))
    qkvla = [col(256, OFF["glq"]), col(256, OFF["glk"]), col(512, OFF["glv"]), col(256, 256 * int(reverse))]
    return col, state, qkvla


def _gla_fwd(name, p, la):
    t = p.shape[0]
    nb = t // TM

    def body(*refs):
        ins, outs, scr = (refs[0:4], refs[4:8]), (refs[8:10], refs[10:12]), refs[12:14]

        @pl.when(pl.program_id(0) == 0)
        def _():
            for s_ref in scr:
                s_ref[...] = jnp.zeros_like(s_ref)

        for step in range(_NCH):
            for d in range(2):
                (q_ref, k_ref, v_ref, la_ref), (o_ref, sv_ref), s_ref = ins[d], outs[d], scr[d]
                c = _NCH - 1 - step if d else step
                rows = slice(GLA_CHUNK * c, GLA_CHUNK * (c + 1))
                s_in = s_ref[...]
                for h in range(GLA_H):
                    sv_ref[c, h] = s_in[GLA_DK * h:GLA_DK * (h + 1), GLA_DV * h:GLA_DV * (h + 1)]
                o, s_out = _gla_chunk(q_ref[rows, :], k_ref[rows, :], v_ref[rows, :], la_ref[rows, :], s_in,
                                      _gla_consts(bool(d)))
                o_ref[rows, :] = o
                s_ref[...] = s_out

    in_specs, out_specs, out_shape = [], [], []
    for d in range(2):
        col, state, qkvla = _gla_specs(nb, bool(d), False)
        in_specs += qkvla
        out_specs += [col(512, 0), state]
        out_shape += [_sds((t, GLA_V), F32), _sds((t // GLA_CHUNK, GLA_H, GLA_DK, GLA_DV), F32)]
    return pl.pallas_call(
        body, name=name, grid=(nb,), in_specs=in_specs, out_specs=out_specs, out_shape=out_shape,
        scratch_shapes=[pltpu.VMEM((GLA_QK, GLA_V), F32)] * 2, compiler_params=_cp(("arbitrary",)),
    )(p, p, p, la, p, p, p, la)


def _gla_bwd(name, p, la, sv_f, sv_b, do):
    t = p.shape[0]
    nb = t // TM

    def body(*refs):
        ins, outs, scr = (refs[0:6], refs[6:12]), (refs[12:16], refs[16:20]), refs[20:22]

        @pl.when(pl.program_id(0) == 0)
        def _():
            for ds_ref in scr:
                ds_ref[...] = jnp.zeros_like(ds_ref)

        zero = jnp.zeros((GLA_DK, GLA_DV), F32)
        for step in range(_NCH):
            for d in range(2):
                (q_ref, k_ref, v_ref, la_ref, sv_ref, do_ref), out_refs, ds_ref = ins[d], outs[d], scr[d]
                c = step if d else _NCH - 1 - step
                rows = slice(GLA_CHUNK * c, GLA_CHUNK * (c + 1))
                s_in = jnp.concatenate(
                    [jnp.concatenate([sv_ref[c, h] if hh == h else zero for hh in range(GLA_H)], axis=-1)
                     for h in range(GLA_H)], axis=0)
                consts = _gla_consts(bool(d))
                _, vjp = jax.vjp(lambda a, b, cc, dd, e: _gla_chunk(a, b, cc, dd, e, consts),
                                 q_ref[rows, :], k_ref[rows, :], v_ref[rows, :], la_ref[rows, :], s_in)
                grads = vjp((do_ref[rows, :], ds_ref[...]))
                for o_ref, g in zip(out_refs, grads[:4]):
                    o_ref[rows, :] = g.astype(o_ref.dtype)
                ds_ref[...] = grads[4]

    ins, in_specs, out_specs, out_shape = [], [], [], []
    for d, sv in enumerate((sv_f, sv_b)):
        col, state, qkvla = _gla_specs(nb, bool(d), True)
        ins += [p, p, p, la, sv, do]
        in_specs += qkvla + [state, col(512, 0)]
        out_specs += [col(256, 0), col(256, 0), col(512, 0), col(256, 0)]
        out_shape += [_sds((t, GLA_QK), BF16), _sds((t, GLA_QK), BF16), _sds((t, GLA_V), BF16), _sds((t, GLA_QK), F32)]
    return pl.pallas_call(
        body, name=name, grid=(nb,), in_specs=in_specs, out_specs=out_specs, out_shape=out_shape,
        scratch_shapes=[pltpu.VMEM((GLA_QK, GLA_V), F32)] * 2, compiler_params=_cp(("arbitrary",)),
    )(*ins)


def _gla_out_fn(of, ob, r, g):
    o = of + ob
    cols = [_rms(o[:, GLA_DV * h:GLA_DV * (h + 1)], g[:, GLA_DV * h:GLA_DV * (h + 1)]) for h in range(GLA_H)]
    return jnp.concatenate(cols, axis=-1) * jax.nn.silu(r.astype(F32))


def _gla_out_fwd(name, of, ob, p, g):
    t = p.shape[0]
    return _rows_call(name, lambda ids, *a: (_gla_out_fn(*a),), (t // TM,),
                      [(of, _rowspec(512)), (ob, _rowspec(512)), (p, _rowspec(512, OFF["gr"])), (g, _full((1, 512)))],
                      [(_sds((t, 512), BF16), _rowspec(512), False)])[0]


def _gla_out_bwd(name, of, ob, p, g, dgla):
    t = p.shape[0]

    def fn(ids, a, b, r, gv, dv):
        _, vjp = jax.vjp(_gla_out_fn, a, b, r, gv)
        do, _, dr, dg = vjp(dv)
        return do, dr, dg

    return _rows_call(name, fn, (t // TM,),
                      [(of, _rowspec(512)), (ob, _rowspec(512)), (p, _rowspec(512, OFF["gr"])), (g, _full((1, 512))),
                       (dgla, _rowspec(512))],
                      [(_sds((t, 512), F32), _rowspec(512), False), (_sds((t, 512), BF16), _rowspec(512), False),
                       (_sds((1, 512), F32), _full((1, 512)), True)])


_TMM = 384


def _merge_fwd(name, gm, att, gla, wa, wb, wc, p):
    t = p.shape[0]
    row = lambda w, off=0: pl.BlockSpec((_TMM, w), lambda i, o=off // w: (i, o))

    def fn(ids, a, b, c, wa_, wb_, wc_, ga, gb, gc):
        ga, gb, gc = ga.astype(F32), gb.astype(F32), gc.astype(F32)
        return (jax.nn.sigmoid(ga) * bdot(a, wa_, 1, 0) + jax.nn.sigmoid(gb) * bdot(b, wb_, 1, 0)
                + jax.nn.sigmoid(gc) * bdot(c, wc_, 1, 0),)

    return _rows_call(name, fn, (t // _TMM,),
                      [(gm, row(512)), (att, row(512)), (gla, row(512)), (wa, _full((512, D))), (wb, _full((512, D))),
                       (wc, _full((512, D))), (p, row(D, OFF["gA"])), (p, row(D, OFF["gB"])), (p, row(D, OFF["gC"]))],
                      [(_sds((t, D), BF16), row(D), False)])[0]


def _merge_bwd(name, gm, att, gla, wa, wb, wc, p, dmerged):
    t = p.shape[0]
    row = lambda w, off=0: pl.BlockSpec((_TMM, w), lambda i, o=off // w: (i, o))

    def fn(ids, a, b, c, wa_, wb_, wc_, ga, gb, gc, dm):
        ga, gb, gc = ga.astype(F32), gb.astype(F32), gc.astype(F32)
        outs_y, outs_g = [], []
        for br, w, g in ((a, wa_, ga), (b, wb_, gb), (c, wc_, gc)):
            s = jax.nn.sigmoid(g)
            outs_y.append(dm * s)
            outs_g.append(dm * bdot(br, w, 1, 0) * s * (1.0 - s))
        return tuple(outs_y) + tuple(outs_g)

    o = (_sds((t, D), BF16), row(D), False)
    return _rows_call(name, fn, (t // _TMM,),
                      [(gm, row(512)), (att, row(512)), (gla, row(512)), (wa, _full((512, D))), (wb, _full((512, D))),
                       (wc, _full((512, D))), (p, row(D, OFF["gA"])), (p, row(D, OFF["gB"])), (p, row(D, OFF["gC"])),
                       (dmerged, row(D))], [o] * 6)


_TNC = 1408
_NJ = FFN // _TNC


HALO = 16


def _shift_rows(x, prev, nxt, vp, vn):
    n = x.shape[0]
    rid = lax.broadcasted_iota(jnp.int32, x.shape, 0)
    xp = jnp.where(rid == 0, jnp.where(vp, prev[HALO - 1:HALO, :], 0.0), pltpu.roll(x, 1, 0))
    xn = jnp.where(rid == n - 1, jnp.where(vn, nxt[0:1, :], 0.0), pltpu.roll(x, n - 1, 0))
    return xp, xn


def _seq_edges(i, t):
    start, end = i * TM, (i + 1) * TM
    return jnp.logical_and(start != 0, start != TC), jnp.logical_and(end != TC, end != t)


def _halo_specs(t, colmap):
    r = TM // HALO
    main = pl.BlockSpec((TM, _TNC), lambda j, i: (i, colmap(j)))
    prev = pl.BlockSpec((HALO, _TNC), lambda j, i: (jnp.maximum(i * r - 1, 0), colmap(j)))
    nxt = pl.BlockSpec((HALO, _TNC), lambda j, i: (jnp.minimum((i + 1) * r, t // HALO - 1), colmap(j)))
    return [main, prev, nxt]


def _conv3(x, xp, xn, w, b=None):
    y = xp * w[0:1, :] + x * w[1:2, :] + xn * w[2:3, :]
    return y if b is None else b + y


def _conv_fwd(name, a, cw, cb):
    t = a.shape[0]

    def fn(ids, ag, agp, agn, av, avp, avn, wg, wv, bg, bv):
        vp, vn = _seq_edges(ids[1], t)
        ag, agp, agn, av, avp, avn = (z.astype(F32) for z in (ag, agp, agn, av, avp, avn))
        cg = _conv3(ag, *_shift_rows(ag, agp, agn, vp, vn), wg, bg)
        cv = _conv3(av, *_shift_rows(av, avp, avn, vp, vn), wv, bv)
        return (jax.nn.silu(cg) * cv,)

    gcol, vcol = (lambda j: j), (lambda j: j + _NJ)
    wspec = lambda cm: pl.BlockSpec((3, _TNC), lambda j, i: (0, cm(j)))
    bspec = lambda cm: pl.BlockSpec((1, _TNC), lambda j, i: (0, cm(j)))
    ins = [(a, s) for s in _halo_specs(t, gcol) + _halo_specs(t, vcol)]
    ins += [(cw, wspec(gcol)), (cw, wspec(vcol)), (cb, bspec(gcol)), (cb, bspec(vcol))]
    return _rows_call(name, fn, (_NJ, t // TM), ins,
                      [(_sds((t, FFN), BF16), pl.BlockSpec((TM, _TNC), lambda j, i: (i, j)), False)])[0]


def _conv_bwd(name, a, cw, cb, dact):
    t = a.shape[0]
    n = TM + 2 * HALO

    def fn(ids, ag, agp, agn, av, avp, avn, dv, dvp, dvn, wg, wv, bg, bv):
        vp, vn = _seq_edges(ids[1], t)
        ag, agp, agn, av, avp, avn = (z.astype(F32) for z in (ag, agp, agn, av, avp, avn))
        ext = lambda x, xp, xn: jnp.concatenate([jnp.where(vp, xp, 0.0), x, jnp.where(vn, xn, 0.0)], axis=0)
        up, dn = (lambda x: pltpu.roll(x, 1, 0)), (lambda x: pltpu.roll(x, n - 1, 0))
        main = lambda y: y[HALO:HALO + TM]
        eg, ev, ed = ext(ag, agp, agn), ext(av, avp, avn), ext(dv, dvp, dvn)
        cg = _conv3(eg, up(eg), dn(eg), wg, bg)
        cv = _conv3(ev, up(ev), dn(ev), wv, bv)
        s = jax.nn.sigmoid(cg)
        rid = lax.broadcasted_iota(jnp.int32, (3, eg.shape[1]), 0)
        das, dws, dbs = [], [], []
        for dc, w, x in ((ed * cv * s * (1.0 + cg * (1.0 - s)), wg, eg), (ed * cg * s, wv, ev)):
            shifted = [main(dn(dc)), main(dc), main(up(dc))]
            das.append(shifted[0] * w[0:1, :] + shifted[1] * w[1:2, :] + shifted[2] * w[2:3, :])
            sums = [jnp.sum(y * main(x), axis=0, keepdims=True) for y in shifted]
            dws.append(jnp.where(rid == 0, sums[0], jnp.where(rid == 1, sums[1], sums[2])))
            dbs.append(jnp.sum(shifted[1], axis=0, keepdims=True))
        return jnp.stack(das), jnp.stack(dws), jnp.stack(dbs)

    gcol, vcol = (lambda j: j), (lambda j: j + _NJ)
    wspec = lambda cm: pl.BlockSpec((3, _TNC), lambda j, i: (0, cm(j)))
    bspec = lambda cm: pl.BlockSpec((1, _TNC), lambda j, i: (0, cm(j)))
    ins = [(a, s) for s in _halo_specs(t, gcol) + _halo_specs(t, vcol)] + [(dact, s) for s in _halo_specs(t, gcol)]
    ins += [(cw, wspec(gcol)), (cw, wspec(vcol)), (cb, bspec(gcol)), (cb, bspec(vcol))]
    return _rows_call(name, fn, (_NJ, t // TM), ins,
                      [(_sds((2, t, FFN), BF16), pl.BlockSpec((2, TM, _TNC), lambda j, i: (0, i, j)), False),
                       (_sds((2, 3, FFN), F32), pl.BlockSpec((2, 3, _TNC), lambda j, i: (0, 0, j)), True),
                       (_sds((2, 1, FFN), F32), pl.BlockSpec((2, 1, _TNC), lambda j, i: (0, 0, j)), True)])


_TNA = 512


def _adaln_fwd(name, cond, w, b):
    fn = lambda ids, cv, wv, bv: ((bdot(jax.nn.silu(cv), wv[0], 1, 0) + bv[0])[None],)
    return _rows_call(name, fn, (2, ADA_LOC // _TNA),
                      [(cond, _full((16, D))), (w, pl.BlockSpec((1, D, _TNA), lambda l, j: (l, 0, j))),
                       (b, pl.BlockSpec((1, 1, _TNA), lambda l, j: (l, 0, j)))],
                      [(_sds((2, 16, ADA_LOC), F32), pl.BlockSpec((1, 16, _TNA), lambda l, j: (l, 0, j)), False)])[0]


def _adaln_bwd(name, c8, cc8, w, dl, dc):
    def fn(ids, cv, ccv, wv, dlv, dcv):
        dcs = jnp.broadcast_to(jnp.sum(dcv[0], axis=0, keepdims=True), dcv[0].shape)
        dw = hdot(jax.nn.silu(cv), dlv[0], 0, 0) + hdot(jax.nn.silu(ccv), dcs, 0, 0)
        s = jax.nn.sigmoid(ccv)
        rid = lax.broadcasted_iota(jnp.int32, ccv.shape, 0)
        dcc = jnp.where(rid == 0, bdot(dcs, wv[0], 1, 1) * s * (1.0 + ccv * (1.0 - s)), 0.0)
        return dw[None], dcc

    dspec = pl.BlockSpec((1, 8, _TNA), lambda l, j: (l, 0, j))
    return _rows_call(name, fn, (2, ADA_LOC // _TNA),
                      [(c8, _full((8, D))), (cc8, _full((8, D))), (w, pl.BlockSpec((1, D, _TNA), lambda l, j: (l, 0, j))),
                       (dl, dspec), (dc, dspec)],
                      [(_sds((2, D, ADA_LOC), F32), pl.BlockSpec((1, D, _TNA), lambda l, j: (l, 0, j)), False),
                       (_sds((8, D), F32), _full((8, D)), True)], acc_axes=(0, 1))


def _adamw_fn(w, g, m, v):
    m = ADAM_B1 * m + (1.0 - ADAM_B1) * g
    v = ADAM_B2 * v + (1.0 - ADAM_B2) * (g * g)
    m_hat = m / (1.0 - ADAM_B1 ** ADAM_STEP)
    v_hat = v / (1.0 - ADAM_B2 ** ADAM_STEP)
    return -ADAM_LR * (m_hat / (jnp.sqrt(v_hat) + ADAM_EPS) + ADAM_WD * w), m, v


def _adamw(name, w, g, m, v):
    l, r, c = w.shape
    tr = _tile(r, max(8, (1 << 20) // (4 * c)), 8)
    spec = pl.BlockSpec((None, tr, c), lambda i, j: (i, j, 0))
    o = (_sds((l, r, c), F32), spec, False)
    return _rows_call(name, lambda ids, *a: _adamw_fn(*a), (l, r // tr), [(x, spec) for x in (w, g, m, v)], [o, o, o],
                      sem=("parallel", "parallel"))


def _coords():
    return lax.axis_index("x"), lax.axis_index("y"), lax.axis_index("c")


def _other_chips(x, y):
    return [(1 - x, y), (x, 1 - y), (1 - x, 1 - y)]


def _allgather_small(name, blk):
    m_per, n = blk.shape

    def body(x_ref, out_ref, send_sems, recv_sems, local_sem):
        x, y, c = _coords()
        me, sibling = (x, y, c), (x, y, 1 - c)
        chips = _other_chips(x, y)

        def rows(px, py, pc):
            return out_ref.at[pl.ds((4 * px + 2 * py + pc) * m_per, m_per), :]

        def copy(k, block, to, src=None):
            return pltpu.make_async_remote_copy(
                src_ref=rows(*block) if src is None else src, dst_ref=rows(*block), send_sem=send_sems.at[k],
                recv_sem=recv_sems.at[k], device_id=to, device_id_type=MESH)

        mine = pltpu.make_async_copy(x_ref, rows(*me), local_sem)
        mine.start()
        first = [copy(0, me, sibling, src=x_ref)]
        first += [copy(1 + j, me, (*chip, c), src=x_ref) for j, chip in enumerate(chips)]
        for cp in first:
            cp.start()
        passed = [copy(4 + j, (*chip, c), sibling) for j, chip in enumerate(chips)]
        for j, chip in enumerate(chips):
            copy(1 + j, (*chip, c), me).wait_recv()
            passed[j].start()
        copy(0, sibling, me).wait_recv()
        for j, chip in enumerate(chips):
            copy(4 + j, (*chip, 1 - c), me).wait_recv()
        for cp in first + passed:
            cp.wait_send()
        mine.wait()

    return pl.pallas_call(
        body, name=name, out_shape=_sds((N_DEV * m_per, n), blk.dtype),
        in_specs=[pl.BlockSpec(memory_space=pltpu.VMEM)], out_specs=pl.BlockSpec(memory_space=pltpu.VMEM),
        scratch_shapes=[pltpu.SemaphoreType.DMA((7,)), pltpu.SemaphoreType.DMA((7,)), pltpu.SemaphoreType.DMA],
        compiler_params=pltpu.CompilerParams(vmem_limit_bytes=VMEM_LIMIT),
    )(blk)


_ANY = pl.BlockSpec(memory_space=pl.ANY)


def _remote(src, dst, send_sems, recv_sems, s, to):
    return pltpu.make_async_remote_copy(src_ref=src, dst_ref=dst, send_sem=send_sems.at[s], recv_sem=recv_sems.at[s],
                                        device_id=to, device_id_type=MESH)


def _comm_call(name, body, ins, out_shapes, n_sems, n_local):
    return pl.pallas_call(
        body, name=name, out_shape=out_shapes, in_specs=[_ANY] * len(ins), out_specs=[_ANY] * len(out_shapes),
        scratch_shapes=[pltpu.SemaphoreType.DMA((n_sems,)), pltpu.SemaphoreType.DMA((n_sems,)),
                        pltpu.SemaphoreType.DMA((n_local,))],
    )(*ins)


def _gather_stage(stage, ins, outs, send_sems, recv_sems):
    n = len(ins)
    x, y, c = _coords()
    k = 2 * x + y
    sibling = (x, y, 1 - c)
    chips = _other_chips(x, y)
    first = [_remote(ins[t].at[c], outs[t].at[k, c], send_sems, recv_sems, 6 * t + j, (*chip, c))
             for t in range(n) for j, chip in enumerate(chips)]
    there = lambda t, j, half: outs[t].at[2 * chips[j][0] + chips[j][1], half]
    passed = [_remote(there(t, j, c), there(t, j, c), send_sems, recv_sems, 6 * t + 3 + j, sibling)
              for t in range(n) for j in range(3)]
    if stage == 0:
        for cp in first:
            cp.start()
    elif stage == 1:
        for t in range(n):
            for j in range(3):
                _remote(there(t, j, c), there(t, j, c), send_sems, recv_sems, 6 * t + j, sibling).wait_recv()
                passed[3 * t + j].start()
    else:
        for t in range(n):
            for j in range(3):
                _remote(there(t, j, 1 - c), there(t, j, 1 - c), send_sems, recv_sems, 6 * t + 3 + j, sibling).wait_recv()
        for cp in first + passed:
            cp.wait_send()


def _gather_own(outs, locs):
    k = 2 * lax.axis_index("x") + lax.axis_index("y")
    return [lax.dynamic_update_slice_in_dim(o, a[None], k, axis=0) for o, a in zip(outs, locs)]


def _allgather_layers(name, locs):
    n = len(locs)

    def body(*refs):
        ins, outs, (send_sems, recv_sems, _) = refs[:n], refs[n:2 * n], refs[2 * n:]
        for stage in range(3):
            _gather_stage(stage, ins, outs, send_sems, recv_sems)

    return _gather_own(_comm_call(name, body, locs, [_sds((N_CHIP,) + a.shape, a.dtype) for a in locs], 6 * n, 1), locs)


def _rs_pair_exchange(name, gs):
    n = len(gs)

    def body(*refs):
        ins, outs, (send_sems, recv_sems, _) = refs[:n], refs[n:2 * n], refs[2 * n:]
        x, y, c = _coords()
        cps = [_remote(ins[t].at[kk, 1 - c], outs[t].at[kk], send_sems, recv_sems, N_CHIP * t + kk, (x, y, 1 - c))
               for t in range(n) for kk in range(N_CHIP)]
        for cp in cps:
            cp.start()
        for cp in cps:
            cp.wait()

    return _comm_call(name, body, gs, [_sds((N_CHIP,) + a.shape[2:], a.dtype) for a in gs], N_CHIP * n, 1)


def _rs_pair_add(name, g, got):
    _, _, r, c = g.shape
    tr = _tile(r, max(16, (1 << 20) // (4 * c)), 16)
    core = jnp.reshape(lax.axis_index("c"), (1,)).astype(jnp.int32)

    def body(core_ref, a_ref, b_ref, o_ref):
        o_ref[...] = (a_ref[...] + b_ref[...]).astype(o_ref.dtype)

    spec = pl.BlockSpec((None, tr, c), lambda kk, i, cr: (kk, i, 0))
    return pl.pallas_call(
        body, name=name, out_shape=_sds((N_CHIP, r, c), BF16),
        grid_spec=pltpu.PrefetchScalarGridSpec(
            num_scalar_prefetch=1, grid=(N_CHIP, r // tr),
            in_specs=[pl.BlockSpec((None, None, tr, c), lambda kk, i, cr: (kk, cr[0], i, 0)), spec], out_specs=spec),
        compiler_params=_cp(("parallel", "parallel")),
    )(core, g, got)


def _exchange_stage(stage, ins, outs, send_sems, recv_sems):
    n = len(ins)
    x, y, c = _coords()
    k = 2 * x + y
    chips = _other_chips(x, y)
    cps = [_remote(ins[t].at[2 * cx + cy], outs[t].at[k], send_sems, recv_sems, 3 * t + j, (cx, cy, c))
           for t in range(n) for j, (cx, cy) in enumerate(chips)]
    if stage == 0:
        for cp in cps:
            cp.start()
    else:
        for t in range(n):
            for j, (cx, cy) in enumerate(chips):
                there = outs[t].at[2 * cx + cy]
                _remote(there, there, send_sems, recv_sems, 3 * t + j, (cx, cy, c)).wait_recv()
        for cp in cps:
            cp.wait_send()


def _exchange_own(outs, s1):
    k = 2 * lax.axis_index("x") + lax.axis_index("y")
    own = [lax.dynamic_index_in_dim(a, k, axis=0, keepdims=True) for a in s1]
    return [lax.dynamic_update_slice_in_dim(o, a, k, axis=0) for o, a in zip(outs, own)]


def _rs_chip_exchange(name, s1):
    n = len(s1)

    def body(*refs):
        ins, outs, (send_sems, recv_sems, _) = refs[:n], refs[n:2 * n], refs[2 * n:]
        for stage in range(2):
            _exchange_stage(stage, ins, outs, send_sems, recv_sems)

    return _exchange_own(_comm_call(name, body, s1, [_sds(a.shape, a.dtype) for a in s1], 3 * n, 1), s1)


def _sum_slots(name, a):
    s, r, cdim = a.shape
    tr = _tile(r, 512, 8)

    def fn(ids, av):
        tot = av[0]
        for i in range(1, s):
            tot = tot + av[i]
        return (tot,)

    return _rows_call(name, fn, (r // tr,), [(a, pl.BlockSpec((s, tr, cdim), lambda i: (0, i, 0)))],
                      [(_sds((r, cdim), F32), pl.BlockSpec((tr, cdim), lambda i: (i, 0)), False)], sem=("parallel",))[0]


def _pair_allgather(name, red0, red1):
    n = len(red0)

    def body(*refs):
        ins, outs, (send_sems, recv_sems, _) = (refs[:n], refs[n:2 * n]), refs[2 * n:3 * n], refs[3 * n:]
        x, y, c = _coords()
        cps = [_remote(ins[l][t], outs[t].at[l, c], send_sems, recv_sems, 2 * t + l, (x, y, 1 - c))
               for t in range(n) for l in range(2)]
        for cp in cps:
            cp.start()
        for t in range(n):
            for l in range(2):
                _remote(ins[l][t], outs[t].at[l, 1 - c], send_sems, recv_sems, 2 * t + l, (x, y, 1 - c)).wait_recv()
        for cp in cps:
            cp.wait_send()

    outs = _comm_call(name, body, list(red0) + list(red1), [_sds((2, 2) + a.shape, a.dtype) for a in red0], 2 * n, 1)
    c = lax.axis_index("c")
    own = [jnp.stack([a, b])[:, None] for a, b in zip(red0, red1)]
    return [lax.dynamic_update_slice_in_dim(o, a, c, axis=1) for o, a in zip(outs, own)]


def _rs_front(tag, gs):
    halves = [g.reshape(N_CHIP, 2, g.shape[1] // 2, g.shape[2]) for g in gs]
    got = _rs_pair_exchange(tag + "rs_pair_exchange", halves)
    return [_rs_pair_add(tag + "rs_pair_add_%d" % t, h, r) for t, (h, r) in enumerate(zip(halves, got))]


def _rs_sum(tag, slots):
    return [_sum_slots(tag + "rs_chip_sum_%d" % t, a) for t, a in enumerate(slots)]


PACK_C = 1024
_SHARDED = (("w_in", 1), ("w_br_a", 1), ("w_br_b", 1), ("w_br_c", 1), ("w_out", 0), ("w_ffn_up", 1), ("w_ffn_down", 0))
_SHARDED_SMALL = (("conv_w", (3, 2 * FFN), 1), ("w_alpha2", (2, 16, GLA_QK), 2), ("b_alpha", (2, GLA_QK), 1))


def _prod(shape):
    n = 1
    for s in shape:
        n *= s
    return n


def _to_blocks(full, axis):
    shp = full.shape
    split = full.reshape(shp[:axis] + (N_CHIP, shp[axis] // N_CHIP) + shp[axis + 1:])
    return jnp.moveaxis(split, axis, 0)


def _from_blocks(blocks, axis):
    return jnp.concatenate([blocks[k] for k in range(N_CHIP)], axis=axis)


def _rope_tables(tx):
    pos = jnp.arange(tx, dtype=jnp.int32)
    inv_freq = 10000.0 ** (-jnp.arange(16, dtype=F32) / 16)
    ang_r = (pos // GRID_W).astype(F32)[:, None] * inv_freq
    ang_c = (pos % GRID_W).astype(F32)[:, None] * inv_freq
    ang = jnp.concatenate([ang_r, ang_r, ang_c, ang_c], axis=-1)
    sign = jnp.concatenate([-jnp.ones((16,), F32), jnp.ones((16,), F32)] * 2)
    cos = jnp.concatenate([jnp.ones((TC, HD), F32), jnp.cos(ang)], axis=0)
    sin = jnp.concatenate([jnp.zeros((TC, HD), F32), jnp.sin(ang) * sign], axis=0)
    return jnp.tile(cos, (1, 2)), jnp.tile(sin, (1, 2))


def _lane_consts():
    l = jnp.arange(512)
    seg = (l[:, None] // HD == l[None, :] // HD).astype(F32) / HD
    partner = jnp.where(l % 32 < 16, l + 16, l - 16)
    perm = (l[:, None] == partner[None, :]).astype(F32)
    return seg, perm


def _heads(a, n):
    return a.reshape(a.shape[0], n, HD).transpose(1, 0, 2)


def _unheads(a):
    return a.transpose(1, 0, 2).reshape(a.shape[1], a.shape[0] * HD)


def _gather_f32_shards(shards):
    sizes = [_prod(a.shape) for a in shards]
    flat = jnp.concatenate([a.reshape(-1) for a in shards] + [jnp.zeros((16 * PACK_C - sum(sizes),), F32)])
    got = _allgather_small("gather_f32_shards", flat.reshape(16, PACK_C)).reshape(N_CHIP, 2, 16 * PACK_C)[:, 0]
    out, o = {}, 0
    for (n, _, ax), a, sz in zip(_SHARDED_SMALL, shards, sizes):
        out[n] = jnp.concatenate([got[k, o:o + sz].reshape(a.shape) for k in range(N_CHIP)], axis=ax + 1)
        o += sz
    return out


def _halves(a):
    return a.reshape(2, a.shape[0] // 2, a.shape[1])


def _layer_shards(W, l):
    return [_halves(W[n][l].astype(BF16)) for n, _ in _SHARDED]


def _layer_params(l, gathered, small):
    w2 = small["w_alpha2_full"][l]
    w2pad = jnp.zeros((128, 512), F32).at[0:16, 0:256].set(w2[0]).at[16:32, 256:512].set(w2[1])
    full = {n: _from_blocks(g.reshape(N_CHIP, 2 * g.shape[2], g.shape[3]), ax)
            for (n, ax), g in zip(_SHARDED, gathered) if g is not None}
    keys = dict(w_br_a="wa", w_br_b="wb", w_br_c="wc", w_out="w_out", w_ffn_up="w_up", w_ffn_down="w_down")
    weights = {keys[n]: a for n, a in full.items() if n in keys}
    if "w_in" in full:
        weights["w_in"] = _to_new_cols(full["w_in"])
    return dict(
        weights,
        cw=small["conv_w_full"][l], cb=small["conv_b"][l][None], w2=w2pad,
        b2=small["b_alpha_full"][l].reshape(1, 512),
        g1=small["norm1_g"][l][None], g2=small["norm2_g"][l][None], gq=jnp.tile(small["q_norm_g"][l], 8)[None],
        gk=jnp.tile(small["k_norm_g"][l], 2)[None], ggm=small["gmlp_norm_g"][l][None], ws=small["w_spatial"][l],
        bst=small["b_spatial"][l].T, ggl=small["gla_norm_g"][l][None])


def _layer_fwd(l, last, x, h1, mod, P, tabs, gather=(), late=None):
    cos, sin, seg, perm = tabs
    n = "l%d_" % l
    s = dict(x=x, h1=h1)
    p = _mm(n + "in_proj", h1, P["w_in"], "nn", BF16, tm_t=768, tn_t=2176, j_outer=True)
    s["p"] = p
    s["gm"] = _gmlp_fwd(n + "gmlp", p, P["ggm"], P["ws"], P["bst"])
    qr, kr, vb = _qk_fwd(n + "qk_prep", p, P["gq"], P["gk"], cos, sin, seg, perm)
    qx, qc, kh, vh = _heads(qr[TC:], NQ), _heads(qr[:TC], NQ), _heads(kr, NKV), _heads(vb, NKV)
    s["qx"], s["qc"], s["kh"], s["vh"] = qx, qc, kh, vh
    one_hot = (jnp.arange(HD) == 0).astype(BF16)
    v1 = jnp.concatenate([vh, jnp.broadcast_to(one_hot, vh.shape)], axis=-1)
    ox, lse_x, *s["gathered"] = _attn_fwd(n + "attn_x", qx, kh, v1, gather)
    s["ox"], s["lse_x"] = ox, lse_x
    if late is not None:
        P = late(s["gathered"])
    s["P"] = P
    if last:
        oc = jnp.zeros((NQ, TC, HD), F32)
    else:
        oc, lse_c = _attn_fwd(n + "attn_c", qc, kh[:, :TC], v1[:, :TC])
        s["oc"], s["lse_c"] = oc, lse_c
    s["att"] = jnp.concatenate([_unheads(oc), _unheads(ox)], axis=0).astype(BF16)
    la = _decay_fwd(n + "gla_decay", p, P["w2"], P["b2"])
    s["la"] = la
    s["of"], s["sf"], s["ob"], s["sb"] = _gla_fwd(n + "gla_scan", p, la)
    s["gla"] = _gla_out_fwd(n + "gla_out", s["of"], s["ob"], p, P["ggl"])
    s["merged"] = _merge_fwd(n + "merge", s["gm"], s["att"], s["gla"], P["wa"], P["wb"], P["wc"], p)
    s["mix"] = _mm(n + "out_proj", s["merged"], P["w_out"], "nn", F32)
    s["x_mid"], s["h2"] = _res_nm_fwd(n + "res1_norm2", x, s["mix"], mod, 2, mod, P["g2"], 3, 4)
    s["a"] = _mm(n + "ffn_up", s["h2"], P["w_up"], "nn", BF16, j_outer=True)
    s["act"] = _conv_fwd(n + "conv_gate", s["a"], P["cw"], P["cb"])
    s["f"] = _mm(n + "ffn_down", s["act"], P["w_down"], "nn", F32)
    return s


def _layer_bwd(l, last, s, mod, P, tabs, dx_mid, df, gw, exchange=()):
    cos, sin, seg, perm = tabs
    n = "l%d_b_" % l
    t = dx_mid.shape[0]
    p = s["p"]
    gw["w_ffn_down"] = _mm(n + "ffn_down_w", s["act"], df, "tn", F32, tm_t=1408)
    dact = _mm(n + "ffn_down_x", df, P["w_down"], "nt", F32)
    da, dcw, dcb = _conv_bwd(n + "conv_gate", s["a"], P["cw"], P["cb"], dact)
    gw["conv_w"], gw["conv_b"] = dcw.transpose(1, 0, 2).reshape(3, 2 * FFN), dcb.reshape(2 * FFN)
    gw["w_ffn_up"] = _mm(n + "ffn_up_w", s["h2"], da, "tn", F32, chip_blocks=True)
    dh2 = _mm(n + "ffn_up_x", da, P["w_up"], "nt", F32)
    dx, dmix, dmod_a, dmod_b, dg2 = _res_nm_bwd(n + "res1_norm2", s["x"], s["mix"], mod, 2, mod, P["g2"], 3, 4, dx_mid, dh2)
    dmod = dmod_a + dmod_b
    gw["norm2_g"] = dg2[0]
    gw["w_out"] = _mm(n + "out_proj_w", s["merged"], dmix, "tn", F32)
    dmerged = _mm(n + "out_proj_x", dmix, P["w_out"], "nt", F32)
    dya, dyb, dyc, dga, dgb, dgc = _merge_bwd(n + "merge", s["gm"], s["att"], s["gla"], P["wa"], P["wb"], P["wc"], p, dmerged)
    gw["w_br_a"] = _mm(n + "br_a_w", s["gm"], dya, "tn", F32)
    gw["w_br_b"] = _mm(n + "br_b_w", s["att"], dyb, "tn", F32)
    gw["w_br_c"] = _mm(n + "br_c_w", s["gla"], dyc, "tn", F32)
    dgm = _mm(n + "br_a_x", dya, P["wa"], "nt", F32)
    datt = _mm(n + "br_b_x", dyb, P["wb"], "nt", F32)
    dgla = _mm(n + "br_c_x", dyc, P["wc"], "nt", F32)
    du, dv_g, dggm, dws, dbst = _gmlp_bwd(n + "gmlp", p, P["ggm"], P["ws"], P["bst"], dgm)
    gw["gmlp_norm_g"], gw["w_spatial"], gw["b_spatial"] = dggm[0], dws, dbst.T
    kh, vh = s["kh"], s["vh"]
    row = lambda a: a.reshape(a.shape[0], 1, a.shape[1])
    dqx, dkh, dvh, *gw["exchanged"] = _attn_bwd(n + "attn_x", s["qx"], kh, vh, s["ox"], _heads(datt[TC:], NQ),
                                                row(s["lse_x"]), exchange)
    if last:
        dqc = jnp.zeros((NQ, TC, HD), F32)
    else:
        dqc, dkc, dvc = _attn_bwd(n + "attn_c", s["qc"], kh[:, :TC], vh[:, :TC], s["oc"], _heads(datt[:TC], NQ),
                                  row(s["lse_c"]))
        pad = jnp.zeros((NKV, t - TC, HD), F32)
        dkh = dkh + jnp.concatenate([dkc, pad], axis=1)
        dvh = dvh + jnp.concatenate([dvc, pad], axis=1)
    dqr = jnp.concatenate([_unheads(dqc), _unheads(dqx)], axis=0)
    dq, dk, dgq, dgk = _qk_bwd(n + "qk_prep", p, P["gq"], P["gk"], cos, sin, seg, perm, dqr, _unheads(dkh))
    gw["q_norm_g"], gw["k_norm_g"] = dgq.reshape(8, HD).sum(0), dgk.reshape(2, HD).sum(0)
    dv_att = _unheads(dvh).astype(BF16)
    do, dr, dggl = _gla_out_bwd(n + "gla_out", s["of"], s["ob"], p, P["ggl"], dgla)
    gw["gla_norm_g"] = dggl[0]
    scans = _gla_bwd(n + "gla_scan", p, s["la"], s["sf"], s["sb"], do)
    dab, dw2, db2, dglq, dglk, dglv = _decay_bwd(n + "gla_decay", p, P["w2"], P["b2"], scans[:4], scans[4:])
    gw["w_alpha2"] = jnp.stack([dw2[0:16, 0:256], dw2[16:32, 256:512]])
    gw["b_alpha"] = db2.reshape(2, 256)
    dp = jnp.concatenate([dga, dgb, dgc, du, dv_g, dq, dglv, dr, dglq, dglk, dk, dv_att, dab], axis=-1)
    gw["w_in"] = _to_ref_cols(_mm(n + "in_proj_w", s["h1"], dp, "tn", F32, tn_t=2176, tk_t=768))
    dh1 = _mm(n + "in_proj_x", dp, P["w_in"], "nt", F32, tk_t=2176)
    return dx, dh1, dmod


_SMALL = (("norm1_g", (2, D)), ("norm2_g", (2, D)), ("q_norm_g", (2, HD)), ("k_norm_g", (2, HD)), ("gmlp_norm_g", (2, GW)),
          ("gla_norm_g", (2, GLA_V)), ("w_spatial", (2, 4, 128, 128)), ("b_spatial", (2, 4, 128)), ("conv_b", (2, 2 * FFN)),
          ("final_norm_g", (D,))) + tuple((n, (2,) + s) for n, s, _ in _SHARDED_SMALL)
_SMALL_N = 2 * 2 * ADA_W + sum(_prod(s) for _, s in _SMALL)
_SMALL_R = -(-_SMALL_N // (PACK_C * 8)) * 8


def _mod_tables(c, c_ctx, w_ada, b_ada, k):
    x, y, cc = _coords()
    me = 4 * x + 2 * y + cc
    c_all = _allgather_small("gather_c", jnp.concatenate([c, jnp.zeros((7, D), F32)], axis=0))
    c8 = c_all.reshape(N_DEV, 8, D)[:, 0]
    cond = jnp.concatenate([c8, c_ctx[None], jnp.zeros((7, D), F32)], axis=0)
    b_loc = lax.dynamic_slice_in_dim(b_ada, k * ADA_LOC, ADA_LOC, axis=1)[:, None, :]
    m_loc = _adaln_fwd("adaln", cond, w_ada, b_loc)
    m_all = _allgather_small("gather_mod", m_loc.reshape(32, ADA_LOC)).reshape(N_CHIP, 2, 2, 16, ADA_LOC)[:, 0]
    m_all = m_all.transpose(1, 2, 0, 3).reshape(2, 16, ADA_W)
    rows = jnp.stack([m_all[:, 8], lax.dynamic_index_in_dim(m_all, me, axis=1, keepdims=False)], axis=1)
    return rows.reshape(2, 2, 6, D), c8


def _step(x, c, ctx, c_ctx, W, tgt):
    xc, yc, cc = _coords()
    k = 2 * xc + yc
    tx = x.shape[0]
    t = TC + tx
    small = {n: W[n] for n, _ in _SMALL}
    for n, a in _gather_f32_shards([W[n] for n, _, _ in _SHARDED_SMALL]).items():
        small[n + "_full"] = a

    shards = [_layer_shards(W, l) for l in range(2)]
    mods, c8 = _mod_tables(c, c_ctx, W["w_ada"], W["b_ada"], k)
    tabs = _rope_tables(tx) + _lane_consts()
    n_w = len(_SHARDED)
    w_in0 = _allgather_layers("gather_weights", shards[0][:1])
    params = [_layer_params(0, w_in0 + [None] * (n_w - 1), small)]
    behind = shards[0][1:] + shards[1]
    arrived = []

    def late(got):
        arrived.extend(_gather_own(got, behind))
        return _layer_params(0, w_in0 + arrived[:n_w - 1], small)

    xs = jnp.concatenate([ctx, x], axis=0)
    h1 = _nm_fwd("l0_norm1", xs, mods[0], params[0]["g1"], 0, 1)
    s0 = _layer_fwd(0, False, xs, h1, mods[0], params[0], tabs, gather=behind, late=late)
    params[0] = s0["P"]
    params.append(_layer_params(1, arrived[n_w - 1:], small))
    x1, h1b = _res_nm_fwd("l0_res2_norm1", s0["x_mid"], s0["f"], mods[0], 5, mods[1], params[1]["g1"], 0, 1)
    s1 = _layer_fwd(1, True, x1, h1b, mods[1], params[1], tabs)
    loss, dxm_l, df_l, dmod_head, dgf = _head("head", s1["x_mid"], s1["f"], mods[1], W["final_norm_g"][None], tgt)

    gws = [dict(), dict()]
    dx1, dh1b, dmod1 = _layer_bwd(1, True, s1, mods[1], params[1], tabs, dxm_l, df_l, gws[1])
    dxm0, df0, dmod0_g, dmod1_s, dg1b = _res_nm_bwd("l0_b_res2_norm1", s0["x_mid"], s0["f"], mods[0], 5, mods[1],
                                                    params[1]["g1"], 0, 1, dx1, dh1b)
    gws[1]["norm1_g"] = dg1b[0]
    blocks = lambda g, n, ax: g if n == "w_ffn_up" else _to_blocks(g, ax)
    sums1 = _rs_front("l1_", [blocks(gws[1][n], n, ax) for n, ax in _SHARDED])
    dx0, dh1, dmod0 = _layer_bwd(0, False, s0, mods[0], params[0], tabs, dxm0, df0, gws[0], exchange=sums1)
    red1 = _rs_sum("l1_", _exchange_own(gws[0]["exchanged"], sums1))
    grad_x, dmod0_s, dg1 = _nm_bwd("l0_b_norm1", xs, mods[0], params[0]["g1"], 0, 1, dx0, dh1)
    gws[0]["norm1_g"] = dg1[0]
    dmods = jnp.stack([dmod0 + dmod0_g + dmod0_s, dmod1 + dmod1_s + dmod_head])

    stk = {n: jnp.stack([gws[0][n], gws[1][n]]) for n, _ in _SMALL if n != "final_norm_g"}
    stk["final_norm_g"] = dgf[0]
    flat = jnp.concatenate([dmods.reshape(-1)] + [stk[n].reshape(-1) for n, _ in _SMALL])
    flat = jnp.concatenate([flat, jnp.zeros((_SMALL_R * PACK_C - _SMALL_N,), F32)]).reshape(_SMALL_R, PACK_C)
    every = _allgather_small("gather_small_grads", flat).reshape(N_DEV, _SMALL_R, PACK_C)
    tot = _sum_slots("sum_small_grads", every).reshape(-1)
    grads, o = {}, 2 * 2 * ADA_W
    for n, shp in _SMALL:
        grads[n] = tot[o:o + _prod(shp)].reshape(shp)
        o += _prod(shp)
    grads["b_ada"] = tot[:2 * 2 * ADA_W].reshape(2, 2, ADA_W).sum(axis=1)

    dm_every = every[:, :2 * 2 * ADA_W // PACK_C].reshape(N_DEV, 2, 2, ADA_W)
    dm_loc = lax.dynamic_slice_in_dim(dm_every, k * ADA_LOC, ADA_LOC, axis=3).transpose(1, 2, 0, 3)
    cc8 = jnp.concatenate([c_ctx[None], jnp.zeros((7, D), F32)], axis=0)
    grads["w_ada"], dcc = _adaln_bwd("adaln_b", c8, cc8, W["w_ada"], dm_loc[:, 1], dm_loc[:, 0])
    dcc_every = _allgather_small("gather_dcctx", dcc * 0.5).reshape(N_DEV, 8, D)
    grads["c_ctx"] = _sum_slots("sum_dcctx", dcc_every)[0]

    for n, shp, ax in _SHARDED_SMALL:
        grads[n] = lax.dynamic_slice_in_dim(grads[n], k * (shp[ax] // N_CHIP), shp[ax] // N_CHIP, axis=ax + 1)
    sums0 = _rs_front("l0_", [blocks(gws[0][n], n, ax) for n, ax in _SHARDED])
    red0 = _rs_sum("l0_", _rs_chip_exchange("l0_rs_chip_exchange", sums0))
    for (n, _), a in zip(_SHARDED, _pair_allgather("rs_pair_allgather", red0, red1)):
        grads[n] = a.reshape(2, 2 * a.shape[2], a.shape[3])
    return loss[0, 0], grad_x, grads


_WEIGHTS = ("c_ctx", "w_ada", "b_ada", "norm1_g", "norm2_g", "w_in", "q_norm_g", "k_norm_g", "gmlp_norm_g", "w_spatial",
            "b_spatial", "w_alpha2", "b_alpha", "gla_norm_g", "w_br_a", "w_br_b", "w_br_c", "w_out", "w_ffn_up", "conv_w",
            "conv_b", "w_ffn_down", "final_norm_g")
_BIG = ("w_ada", "w_in", "w_br_a", "w_br_b", "w_br_c", "w_out", "w_ffn_up", "w_ffn_down")


def _update(W, G, M, V):
    delta, new_m, new_v = {}, {}, {}
    for n in _BIG:
        delta[n], new_m[n], new_v[n] = _adamw("adamw_" + n, W[n], G[n], M[n], V[n])
    rest = [n for n in _WEIGHTS if n not in _BIG]
    tot = sum(_prod(W[n].shape) for n in rest)
    rows = -(-tot // (PACK_C * 8)) * 8

    def cat(dct):
        flat = jnp.concatenate([dct[n].reshape(-1) for n in rest] + [jnp.zeros((rows * PACK_C - tot,), F32)])
        return flat.reshape(1, rows, PACK_C)

    outs = _adamw("adamw_small", cat(W), cat(G), cat(M), cat(V))
    o = 0
    for n in rest:
        sz, shp = _prod(W[n].shape), W[n].shape
        delta[n], new_m[n], new_v[n] = (a.reshape(-1)[o:o + sz].reshape(shp) for a in outs)
        o += sz
    return delta, new_m, new_v


def kernel(x, c, ctx, c_ctx, w_ada, b_ada, norm1_g, norm2_g, w_in, q_norm_g, k_norm_g, gmlp_norm_g, w_spatial, b_spatial, w_alpha2, b_alpha, gla_norm_g, w_br_a, w_br_b, w_br_c, w_out, w_ffn_up, conv_w, conv_b, w_ffn_down, final_norm_g, loss_target, m_c_ctx, m_w_ada, m_b_ada, m_norm1_g, m_norm2_g, m_w_in, m_q_norm_g, m_k_norm_g, m_gmlp_norm_g, m_w_spatial, m_b_spatial, m_w_alpha2, m_b_alpha, m_gla_norm_g, m_w_br_a, m_w_br_b, m_w_br_c, m_w_out, m_w_ffn_up, m_conv_w, m_conv_b, m_w_ffn_down, m_final_norm_g, v_c_ctx, v_w_ada, v_b_ada, v_norm1_g, v_norm2_g, v_w_in, v_q_norm_g, v_k_norm_g, v_gmlp_norm_g, v_w_spatial, v_b_spatial, v_w_alpha2, v_b_alpha, v_gla_norm_g, v_w_br_a, v_w_br_b, v_w_br_c, v_w_out, v_w_ffn_up, v_conv_w, v_conv_b, v_w_ffn_down, v_final_norm_g):
    W = dict(c_ctx=c_ctx, w_ada=w_ada, b_ada=b_ada, norm1_g=norm1_g, norm2_g=norm2_g, w_in=w_in, q_norm_g=q_norm_g,
             k_norm_g=k_norm_g, gmlp_norm_g=gmlp_norm_g, w_spatial=w_spatial, b_spatial=b_spatial, w_alpha2=w_alpha2,
             b_alpha=b_alpha, gla_norm_g=gla_norm_g, w_br_a=w_br_a, w_br_b=w_br_b, w_br_c=w_br_c, w_out=w_out,
             w_ffn_up=w_ffn_up, conv_w=conv_w, conv_b=conv_b, w_ffn_down=w_ffn_down, final_norm_g=final_norm_g)
    M = dict(c_ctx=m_c_ctx, w_ada=m_w_ada, b_ada=m_b_ada, norm1_g=m_norm1_g, norm2_g=m_norm2_g, w_in=m_w_in,
             q_norm_g=m_q_norm_g, k_norm_g=m_k_norm_g, gmlp_norm_g=m_gmlp_norm_g, w_spatial=m_w_spatial,
             b_spatial=m_b_spatial, w_alpha2=m_w_alpha2, b_alpha=m_b_alpha, gla_norm_g=m_gla_norm_g, w_br_a=m_w_br_a,
             w_br_b=m_w_br_b, w_br_c=m_w_br_c, w_out=m_w_out, w_ffn_up=m_w_ffn_up, conv_w=m_conv_w, conv_b=m_conv_b,
             w_ffn_down=m_w_ffn_down, final_norm_g=m_final_norm_g)
    V = dict(c_ctx=v_c_ctx, w_ada=v_w_ada, b_ada=v_b_ada, norm1_g=v_norm1_g, norm2_g=v_norm2_g, w_in=v_w_in,
             q_norm_g=v_q_norm_g, k_norm_g=v_k_norm_g, gmlp_norm_g=v_gmlp_norm_g, w_spatial=v_w_spatial,
             b_spatial=v_b_spatial, w_alpha2=v_w_alpha2, b_alpha=v_b_alpha, gla_norm_g=v_gla_norm_g, w_br_a=v_w_br_a,
             w_br_b=v_w_br_b, w_br_c=v_w_br_c, w_out=v_w_out, w_ffn_up=v_w_ffn_up, conv_w=v_conv_w, conv_b=v_conv_b,
             w_ffn_down=v_w_ffn_down, final_norm_g=v_final_norm_g)
    loss_local, grad_x, G = _step(x[0], c, ctx[0], c_ctx, W, loss_target[0])
    loss = lax.psum(loss_local, ("x", "y", "c"))
    delta, new_m, new_v = _update(W, G, M, V)
    return (loss, grad_x[None], *[G[n] for n in _WEIGHTS], *[delta[n] for n in _WEIGHTS],
            *[new_m[n] for n in _WEIGHTS], *[new_v[n] for n in _WEIGHTS])
```
